```python
import jax, jax.numpy as jnp
from jax import lax
import numpy as np

D_MODEL = 1024
BATCH = 2
SEQ = 8192
DEPTH = 2

RET_HEADS = 4
RET_DV = D_MODEL // (2 * RET_HEADS)
RET_DK = RET_DV // 2
RET_CHUNK = 128
ROPE_BASE = 10000.0
LRU_WIDTH = D_MODEL // 2
LRU_BLOCKS = 4
LRU_BW = LRU_WIDTH // LRU_BLOCKS
LRU_CONV = 4
LRU_C = 8.0
HG_HEADS = 8
HG_DK = D_MODEL // HG_HEADS
HG_DV = D_MODEL // HG_HEADS
HG_CHUNK = 32
N_EXPERTS = 16
N_GROUPS = 4
EXPERTS_PER_GROUP = N_EXPERTS // N_GROUPS
TOP_K = 2
D_EXPERT = 512
MOE_BLOCK = 128
DN_ALPHA = (2.0 * DEPTH) ** 0.25
DN_BETA = (8.0 * DEPTH) ** -0.25
N_EVEN = (DEPTH + 1) // 2
N_ODD = DEPTH // 2
LN_EPS = 1e-5
RMS_EPS = 1e-6
EVEN_SPLITS = (RET_HEADS * RET_DK,
               2 * RET_HEADS * RET_DK,
               2 * RET_HEADS * RET_DK + RET_HEADS * RET_DV,
               2 * RET_HEADS * RET_DK + 2 * RET_HEADS * RET_DV,
               2 * RET_HEADS * RET_DK + 2 * RET_HEADS * RET_DV + LRU_WIDTH)
EVEN_IN = 2 * RET_HEADS * RET_DK + 2 * RET_HEADS * RET_DV + 2 * LRU_WIDTH
EVEN_OUT = RET_HEADS * RET_DV + LRU_WIDTH
ODD_IN = 5 * D_MODEL

kernel_name = 'hybrid_retention_rglru_hgrn2_grouped_moe_encoder'


def _layer_norm(x, w, b):
    xf = x.astype(jnp.float32)
    mu = jnp.mean(xf, axis=-1, keepdims=True)
    var = jnp.mean(jnp.square(xf - mu), axis=-1, keepdims=True)
    return ((xf - mu) * lax.rsqrt(var + LN_EPS) * w + b).astype(x.dtype)


def _head_norm(x):
    mu = jnp.mean(x, axis=-1, keepdims=True)
    var = jnp.mean(jnp.square(x - mu), axis=-1, keepdims=True)
    return (x - mu) * lax.rsqrt(var + LN_EPS)


def _rms_norm(x, w):
    return x * lax.rsqrt(jnp.mean(jnp.square(x), axis=-1, keepdims=True) + RMS_EPS) * w.astype(jnp.float32)


def _rotary(x, pos):
    d = x.shape[-1]
    inv_freq = ROPE_BASE ** (-jnp.arange(0, d, 2, dtype=jnp.float32) / d)
    ang = pos.astype(jnp.float32)[:, None] * inv_freq[None, :]
    cos = jnp.cos(ang)[None, :, None, :]
    sin = jnp.sin(ang)[None, :, None, :]
    x1, x2 = x[..., 0::2], x[..., 1::2]
    return jnp.stack([x1 * cos - x2 * sin, x1 * sin + x2 * cos], axis=-1).reshape(x.shape)


def _retention_bidir(q, k, v):
    B, S, H, dk = q.shape
    dv = v.shape[-1]
    C = RET_CHUNK
    N = S // C
    log_gamma = jnp.log1p(-(2.0 ** (-5.0 - jnp.arange(H, dtype=jnp.float32))))
    qc = (q * dk ** -0.5).reshape(B, N, C, H, dk)
    kc = k.reshape(B, N, C, H, dk)
    vc = v.reshape(B, N, C, H, dv)
    idx = jnp.arange(C, dtype=jnp.float32)
    d_intra = jnp.exp(log_gamma[:, None, None] * jnp.abs(idx[:, None] - idx[None, :]))
    scores = jnp.einsum('bnihd,bnjhd->bnhij', qc, kc) * d_intra
    o = jnp.einsum('bnhij,bnjhe->bnihe', scores, vc)
    q_dec_f = jnp.exp(log_gamma[None, :] * (idx[:, None] + 1.0))
    k_dec_f = jnp.exp(log_gamma[None, :] * (C - 1.0 - idx[:, None]))
    q_dec_b = jnp.exp(log_gamma[None, :] * (C - idx[:, None]))
    k_dec_b = jnp.exp(log_gamma[None, :] * idx[:, None])
    chunk_dec = jnp.exp(log_gamma * C)[None, :, None, None]
    kv_f = jnp.einsum('bnjhd,jh,bnjhe->nbhde', kc, k_dec_f, vc)
    kv_b = jnp.einsum('bnjhd,jh,bnjhe->nbhde', kc, k_dec_b, vc)

    def step(state, kv):
        return state * chunk_dec + kv, state

    init = jnp.zeros((B, H, dk, dv), jnp.float32)
    _, st_f = lax.scan(step, init, kv_f)
    _, st_b = lax.scan(step, init, kv_b, reverse=True)
    o = (o + jnp.einsum('bnihd,ih,nbhde->bnihe', qc, q_dec_f, st_f)
           + jnp.einsum('bnihd,ih,nbhde->bnihe', qc, q_dec_b, st_b))
    return o.reshape(B, S, H, dv)


def _rglru_bidir(x, conv_w, conv_b, gate_w, gate_b, lam):
    B, S, W = x.shape
    lo = LRU_CONV // 2
    hi = LRU_CONV - 1 - lo
    xc = lax.conv_general_dilated(x, conv_w.astype(jnp.float32)[:, None, :], window_strides=(1,),
                                  padding=[(lo, hi)], dimension_numbers=('NWC', 'WIO', 'NWC'),
                                  feature_group_count=W) + conv_b.astype(jnp.float32)
    xb = xc.reshape(B, S, LRU_BLOCKS, LRU_BW)
    gates = jax.nn.sigmoid(jnp.einsum('bsnc,zgncd->zgbsnd', xb, gate_w.astype(jnp.float32))
                           + gate_b.astype(jnp.float32)[:, :, None, None])
    gates = gates.reshape(2, 2, B, S, W)
    r, i = gates[:, 0], gates[:, 1]
    log_a = -LRU_C * r * jax.nn.softplus(-lam.astype(jnp.float32))[:, None, None, :]
    a = jnp.exp(log_a)
    b = jnp.sqrt(-jnp.expm1(2.0 * log_a)) * (i * xc[None])

    def comb(left, right):
        a1, b1 = left
        a2, b2 = right
        return a1 * a2, a2 * b1 + b2

    _, h_f = lax.associative_scan(comb, (a[0], b[0]), axis=1)
    _, h_b = lax.associative_scan(comb, (a[1], b[1]), axis=1, reverse=True)
    return h_f + h_b


def _gla_chunks(q, k, v, log_f):
    B, S, H, dk = q.shape
    dv = v.shape[-1]
    C = HG_CHUNK
    N = S // C

    def to_chunks(t):
        return t.reshape(B, N, C, H, t.shape[-1]).transpose(1, 0, 3, 2, 4)

    causal = jnp.tril(jnp.ones((C, C), dtype=bool))[:, :, None]

    def step(state, inp):
        qc, kc, vc, gc = inp
        b = jnp.cumsum(gc, axis=2)
        diff = b[:, :, :, None, :] - b[:, :, None, :, :]
        decay = jnp.exp(jnp.where(causal, diff, -jnp.inf))
        attn = jnp.einsum('bhid,bhjd,bhijd->bhij', qc, kc, decay)
        o = (jnp.einsum('bhij,bhje->bhie', attn, vc)
             + jnp.einsum('bhid,bhde->bhie', qc * jnp.exp(b), state))
        b_last = b[:, :, -1]
        new_state = (state * jnp.exp(b_last)[..., None]
                     + jnp.einsum('bhjd,bhje->bhde', kc * jnp.exp(b_last[:, :, None, :] - b), vc))
        return new_state, o

    init = jnp.zeros((B, H, dk, dv), jnp.float32)
    _, o = lax.scan(step, init, (to_chunks(q), to_chunks(k), to_chunks(v), to_chunks(log_f)))
    return o.transpose(1, 0, 3, 2, 4).reshape(B, S, H, dv)


def _even_mixer(h, w_in, w_out, conv_w, conv_b, gate_w, gate_b, lam):
    B, S, _ = h.shape
    proj = (h @ w_in).astype(jnp.float32)
    q, k, v, g, xr, gr = jnp.split(proj, list(EVEN_SPLITS), axis=-1)
    pos = jnp.arange(S)
    q = _rotary(q.reshape(B, S, RET_HEADS, RET_DK), pos)
    k = _rotary(k.reshape(B, S, RET_HEADS, RET_DK), pos)
    v = v.reshape(B, S, RET_HEADS, RET_DV)
    ret = _head_norm(_retention_bidir(q, k, v)).reshape(B, S, RET_HEADS * RET_DV)
    ret_out = jax.nn.silu(g) * ret
    lru_out = _rglru_bidir(xr, conv_w, conv_b, gate_w, gate_b, lam) * jax.nn.gelu(gr)
    return jnp.concatenate([ret_out, lru_out], axis=-1).astype(h.dtype) @ w_out


def _odd_mixer(h, w_in, w_out, lower_bounds, norm_w, layer):
    B, S, _ = h.shape
    proj = (h @ w_in).astype(jnp.float32)
    q, v, z_f, z_b, g = jnp.split(proj, 5, axis=-1)
    lb_all = jnp.cumsum(jax.nn.softmax(lower_bounds.astype(jnp.float32), axis=0), axis=0)
    lb = lb_all[layer] - lb_all[0]

    def heads(t):
        return t.reshape(B, S, HG_HEADS, -1)

    def forget(z):
        log_f = jnp.logaddexp(jnp.log(lb), jnp.log1p(-lb) + jax.nn.log_sigmoid(z))
        key = (1.0 - lb) * jax.nn.sigmoid(-z)
        return heads(log_f), heads(key)

    qh = heads(jax.nn.silu(q)) * HG_DK ** -0.5
    vh = heads(v)
    lf_f, k_f = forget(z_f)
    lf_b, k_b = forget(z_b)
    o_f = _gla_chunks(qh, k_f, vh, lf_f)
    flip = lambda t: jnp.flip(t, axis=1)
    o_b = flip(_gla_chunks(flip(qh), flip(k_b), flip(vh), flip(lf_b)))
    o = _rms_norm((o_f + o_b).reshape(B, S, HG_HEADS * HG_DV), norm_w) * jax.nn.silu(g)
    return o.astype(h.dtype) @ w_out


def _moe(h, router_w, w_gate, w_up, w_down):
    B, S, D = h.shape
    T = B * S
    x = h.reshape(T, D)
    probs = jax.nn.softmax((x @ router_w).astype(jnp.float32), axis=-1)
    top_v, top_i = lax.top_k(probs.reshape(T, N_GROUPS, EXPERTS_PER_GROUP), TOP_K)
    g_sel = jnp.argmax(jnp.sum(top_v, axis=-1), axis=-1)
    tok = jnp.arange(T, dtype=jnp.int32)
    vals = top_v[tok, g_sel]
    eidx = top_i[tok, g_sel] + g_sel[:, None].astype(jnp.int32) * EXPERTS_PER_GROUP
    gates = vals / jnp.sum(vals, axis=-1, keepdims=True)
    A = T * TOP_K
    flat_e = eidx.reshape(A).astype(jnp.int32)
    flat_tok = jnp.repeat(tok, TOP_K)
    flat_g = gates.reshape(A)
    order = jnp.argsort(flat_e)
    se = flat_e[order]
    counts = jnp.bincount(flat_e, length=N_EXPERTS).astype(jnp.int32)
    padded = (counts + MOE_BLOCK - 1) // MOE_BLOCK * MOE_BLOCK
    start = jnp.cumsum(counts) - counts
    ends_p = jnp.cumsum(padded)
    pstart = ends_p - padded
    dest = pstart[se] + jnp.arange(A, dtype=jnp.int32) - start[se]
    P = (A + MOE_BLOCK - 1) // MOE_BLOCK * MOE_BLOCK + N_EXPERTS * MOE_BLOCK
    nb = P // MOE_BLOCK
    buf_tok = jnp.zeros((P,), jnp.int32).at[dest].set(flat_tok[order])
    buf_gate = jnp.zeros((P,), jnp.float32).at[dest].set(flat_g[order])
    block_start = jnp.arange(nb, dtype=jnp.int32) * MOE_BLOCK
    block_e = jnp.minimum(jnp.searchsorted(ends_p, block_start, side='right'), N_EXPERTS - 1)
    xb = x[buf_tok].reshape(nb, MOE_BLOCK, D)

    def expert_block(args):
        xblk, e = args
        hid = jax.nn.silu(xblk @ w_gate[e]) * (xblk @ w_up[e])
        return hid @ w_down[e]

    yb = lax.map(expert_block, (xb, block_e)).reshape(P, D)
    y = jnp.zeros((T, D), h.dtype).at[buf_tok].add(yb * buf_gate[:, None].astype(yb.dtype))
    return y.reshape(B, S, D)


def setup_inputs(seed: int = 0) -> dict:
    key = jax.random.key(seed)
    ks = jax.random.split(key, 18)
    f32 = jnp.float32

    def nrm(k, shape, scale):
        return jax.random.normal(k, shape, f32) * scale

    x = nrm(ks[0], (BATCH, SEQ, D_MODEL), 1.0)
    w_in_even = nrm(ks[1], (N_EVEN, D_MODEL, EVEN_IN), D_MODEL ** -0.5)
    w_out_even = nrm(ks[2], (N_EVEN, EVEN_OUT, D_MODEL), EVEN_OUT ** -0.5 * DN_BETA)
    lru_conv_w = nrm(ks[3], (N_EVEN, LRU_CONV, LRU_WIDTH), LRU_CONV ** -0.5)
    lru_conv_b = nrm(ks[4], (N_EVEN, LRU_WIDTH), 0.01)
    lru_gate_w = nrm(ks[5], (N_EVEN, 2, 2, LRU_BLOCKS, LRU_BW, LRU_BW), LRU_BW ** -0.5)
    lru_gate_b = nrm(ks[6], (N_EVEN, 2, 2, LRU_BLOCKS, LRU_BW), 0.01)
    u = jax.random.uniform(ks[7], (N_EVEN, 2, LRU_WIDTH), f32, 0.9, 0.999)
    a = u ** (1.0 / LRU_C)
    lru_lambda = jnp.log(a) - jnp.log1p(-a)
    w_in_odd = nrm(ks[8], (N_ODD, D_MODEL, ODD_IN), D_MODEL ** -0.5)
    w_out_odd = nrm(ks[9], (N_ODD, HG_HEADS * HG_DV, D_MODEL), (HG_HEADS * HG_DV) ** -0.5 * DN_BETA)
    hg_lower_bounds = nrm(ks[10], (DEPTH, HG_HEADS * HG_DK), 0.5)
    hg_norm_w = 1.0 + nrm(ks[11], (N_ODD, HG_HEADS * HG_DV), 0.02)
    ln_w = 1.0 + nrm(ks[12], (DEPTH, 2, D_MODEL), 0.02)
    ln_b = nrm(ks[13], (DEPTH, 2, D_MODEL), 0.02)
    router_w = nrm(ks[14], (D_MODEL, N_EXPERTS), D_MODEL ** -0.5)
    moe_w_gate = nrm(ks[15], (DEPTH, N_EXPERTS, D_MODEL, D_EXPERT), D_MODEL ** -0.5)
    moe_w_up = nrm(ks[16], (DEPTH, N_EXPERTS, D_MODEL, D_EXPERT), D_MODEL ** -0.5)
    moe_w_down = nrm(ks[17], (DEPTH, N_EXPERTS, D_EXPERT, D_MODEL), D_EXPERT ** -0.5 * DN_BETA)
    return {'x': x, 'w_in_even': w_in_even, 'w_out_even': w_out_even,
            'lru_conv_w': lru_conv_w, 'lru_conv_b': lru_conv_b,
            'lru_gate_w': lru_gate_w, 'lru_gate_b': lru_gate_b, 'lru_lambda': lru_lambda,
            'w_in_odd': w_in_odd, 'w_out_odd': w_out_odd,
            'hg_lower_bounds': hg_lower_bounds, 'hg_norm_w': hg_norm_w,
            'ln_w': ln_w, 'ln_b': ln_b, 'router_w': router_w,
            'moe_w_gate': moe_w_gate, 'moe_w_up': moe_w_up, 'moe_w_down': moe_w_down}


def reference(x, w_in_even, w_out_even, lru_conv_w, lru_conv_b, lru_gate_w, lru_gate_b,
              lru_lambda, w_in_odd, w_out_odd, hg_lower_bounds, hg_norm_w, ln_w, ln_b,
              router_w, moe_w_gate, moe_w_up, moe_w_down):
    h = x
    for layer in range(DEPTH):
        j = layer // 2
        if layer % 2 == 0:
            y = _even_mixer(h, w_in_even[j], w_out_even[j], lru_conv_w[j], lru_conv_b[j],
                            lru_gate_w[j], lru_gate_b[j], lru_lambda[j])
        else:
            y = _odd_mixer(h, w_in_odd[j], w_out_odd[j], hg_lower_bounds, hg_norm_w[j], layer)
        h = _layer_norm(DN_ALPHA * h + y, ln_w[layer, 0], ln_b[layer, 0])
        y = _moe(h, router_w, moe_w_gate[layer], moe_w_up[layer], moe_w_down[layer])
        h = _layer_norm(DN_ALPHA * h + y, ln_w[layer, 1], ln_b[layer, 1])
    return h
```

```python
import functools
import math

import jax
import jax.numpy as jnp
from jax import lax
from jax.experimental import pallas as pl
from jax.experimental.pallas import tpu as pltpu

F32 = jnp.float32
BF16 = jnp.bfloat16

RET_HEADS = 4
RET_DK = 64
RET_DV = 128
RET_CHUNK = 128
ROPE_BASE = 10000.0
LRU_BLOCKS = 4
LRU_BW = 128
LRU_CONV = 4
LRU_C = 8.0
HG_HEADS = 8
HG_DK = 128
HG_DV = 128
N_EXPERTS = 16
N_GROUPS = 4
EXPERTS_PER_GROUP = 4
LN_EPS = 1e-5
RMS_EPS = 1e-6

LANES = 128
SUBLANES = 8
GLA_CHUNK = 32
MOE_ROWS = 256
VMEM_LIMIT = 48 * 1024 * 1024

_NT = (((1,), (1,)), ((), ()))
_TN = (((0,), (0,)), ((), ()))


def _params(sem):
    return pltpu.CompilerParams(dimension_semantics=sem, vmem_limit_bytes=VMEM_LIMIT)


def _sigmoid(x):
    return 1.0 / (1.0 + jnp.exp(-x))


def _silu(x):
    return x * _sigmoid(x)


def _softplus(x):
    return jnp.maximum(x, 0.0) + jnp.log1p(jnp.exp(-jnp.abs(x)))


def _layer_norm(u, w, b):
    mu = jnp.mean(u, axis=-1, keepdims=True)
    d = u - mu
    var = jnp.mean(d * d, axis=-1, keepdims=True)
    return d * lax.rsqrt(var + LN_EPS) * w + b


def _proj_kernel(x_ref, w_ref, inv_ref, o_ref, xb_ref, *, rotary, seq_len, tm):
    i = pl.program_id(0)
    j = pl.program_id(1)

    @pl.when(j == 0)
    def _():
        xb_ref[...] = x_ref[...].astype(BF16)

    acc = jnp.dot(xb_ref[...], w_ref[...], preferred_element_type=F32)

    if not rotary:
        o_ref[...] = acc
        return

    @pl.when(j != 0)
    def _():
        o_ref[...] = acc

    @pl.when(j == 0)
    def _():
        row = lax.broadcasted_iota(jnp.int32, (tm, 1), 0) + i * tm
        pos = (row % seq_len).astype(F32)
        ang = pos * inv_ref[...]
        cos = jnp.cos(ang)
        sin = jnp.sin(ang)
        scale = RET_DK ** -0.5
        q1, q2 = acc[:, 0:128], acc[:, 128:256]
        k1, k2 = acc[:, 256:384], acc[:, 384:512]
        o_ref[:, 0:128] = (q1 * cos - q2 * sin) * scale
        o_ref[:, 128:256] = (q1 * sin + q2 * cos) * scale
        o_ref[:, 256:384] = k1 * cos - k2 * sin
        o_ref[:, 384:512] = k1 * sin + k2 * cos


def _project(x, w_bf16, inv_freq, *, rotary, seq_len):
    t, k = x.shape
    n = w_bf16.shape[1]
    tm = min(1024, seq_len)
    tn = 512
    return pl.pallas_call(
        functools.partial(_proj_kernel, rotary=rotary, seq_len=seq_len, tm=tm),
        grid=(t // tm, n // tn),
        in_specs=[
            pl.BlockSpec((tm, k), lambda i, j: (i, 0)),
            pl.BlockSpec((k, tn), lambda i, j: (0, j)),
            pl.BlockSpec((1, LANES), lambda i, j: (0, 0)),
        ],
        out_specs=pl.BlockSpec((tm, tn), lambda i, j: (i, j)),
        out_shape=jax.ShapeDtypeStruct((t, n), F32),
        scratch_shapes=[pltpu.VMEM((tm, k), BF16)],
        compiler_params=_params(("arbitrary", "arbitrary")),
        name="in_proj",
    )(x, w_bf16, inv_freq)


def _ret_log_gamma(head):
    out = jnp.full(head.shape, math.log1p(-(2.0 ** -5.0)), F32)
    for h in range(1, RET_HEADS):
        out = jnp.where(head == h, math.log1p(-(2.0 ** (-5.0 - h))), out)
    return out


def _ret_lane_log_gamma():
    lane = lax.broadcasted_iota(jnp.int32, (1, 2 * LANES), 1)
    return _ret_log_gamma((lane % LANES) // (RET_DK // 2))


def _ret_state_mask():
    row_head = lax.broadcasted_iota(jnp.int32, (RET_HEADS * RET_DV, 2 * LANES), 0) // RET_DV
    col_head = (lax.broadcasted_iota(jnp.int32, (RET_HEADS * RET_DV, 2 * LANES), 1) % LANES) // (RET_DK // 2)
    return row_head == col_head


def _ret_bstate_kernel(k_ref, v_ref, sb_ref, s_ref):
    n = pl.program_id(1)
    c = RET_CHUNK

    @pl.when(n == 0)
    def _():
        s_ref[...] = jnp.zeros_like(s_ref)

    sb_ref[0, 0] = s_ref[...].astype(BF16)
    lg = _ret_lane_log_gamma()
    idx = lax.broadcasted_iota(jnp.int32, (c, 1), 0).astype(F32)
    kb = (k_ref[...] * jnp.exp(lg * idx)).astype(BF16)
    upd = lax.dot_general(v_ref[...].astype(BF16), kb, _TN, preferred_element_type=F32)
    s_ref[...] = s_ref[...] * jnp.exp(lg * float(c)) + upd


def _ret_out_kernel(q_ref, k_ref, v_ref, g_ref, sb_ref, o_ref, s_ref):
    n = pl.program_id(1)
    c = RET_CHUNK

    @pl.when(n == 0)
    def _():
        s_ref[...] = jnp.zeros_like(s_ref)

    lg = _ret_lane_log_gamma()
    idx = lax.broadcasted_iota(jnp.int32, (c, 1), 0).astype(F32)
    q = q_ref[...]
    k = k_ref[...]
    v = v_ref[...]
    vb = v.astype(BF16)
    kb = k.astype(BF16)
    mask = _ret_state_mask()

    qf = (q * jnp.exp(lg * (idx + 1.0))).astype(BF16)
    qb = (q * jnp.exp(lg * (float(c) - idx))).astype(BF16)
    sf = jnp.where(mask, s_ref[...], 0.0).astype(BF16)
    sb = jnp.where(mask, sb_ref[0, 0], 0.0).astype(BF16)
    cross = (lax.dot_general(qf, sf, _NT, preferred_element_type=F32)
             + lax.dot_general(qb, sb, _NT, preferred_element_type=F32))

    lane = lax.broadcasted_iota(jnp.int32, (1, 2 * LANES), 1)
    lane_head = (lane % LANES) // (RET_DK // 2)
    ii = lax.broadcasted_iota(jnp.int32, (c, c), 0)
    jj = lax.broadcasted_iota(jnp.int32, (c, c), 1)
    dist = jnp.abs(ii - jj).astype(F32)
    for h in range(RET_HEADS):
        qh = jnp.where(lane_head == h, q, 0.0).astype(BF16)
        s = lax.dot_general(qh, kb, _NT, preferred_element_type=F32)
        s = s * jnp.exp(math.log1p(-(2.0 ** (-5.0 - h))) * dist)
        cols = slice(h * RET_DV, (h + 1) * RET_DV)
        o = jnp.dot(s.astype(BF16), vb[:, cols], preferred_element_type=F32) + cross[:, cols]
        mu = jnp.mean(o, axis=-1, keepdims=True)
        d = o - mu
        var = jnp.mean(d * d, axis=-1, keepdims=True)
        o_ref[:, cols] = _silu(g_ref[:, cols]) * (d * lax.rsqrt(var + LN_EPS))

    kf = (k * jnp.exp(lg * (float(c) - 1.0 - idx))).astype(BF16)
    upd = lax.dot_general(vb, kf, _TN, preferred_element_type=F32)
    s_ref[...] = s_ref[...] * jnp.exp(lg * float(c)) + upd


def _retention(proj, batch, seq_len):
    t = proj.shape[0]
    c = RET_CHUNK
    nc = seq_len // c
    dv = RET_HEADS * RET_DV
    state_shape = (dv, 2 * LANES)
    rev = lambda b, n: b * nc + (nc - 1 - n)
    fwd = lambda b, n: b * nc + n
    sb = pl.pallas_call(
        _ret_bstate_kernel,
        grid=(batch, nc),
        in_specs=[
            pl.BlockSpec((c, 2 * LANES), lambda b, n: (rev(b, n), 1)),
            pl.BlockSpec((c, dv), lambda b, n: (rev(b, n), 1)),
        ],
        out_specs=pl.BlockSpec((1, 1) + state_shape, lambda b, n: (b, nc - 1 - n, 0, 0)),
        out_shape=jax.ShapeDtypeStruct((batch, nc) + state_shape, BF16),
        scratch_shapes=[pltpu.VMEM(state_shape, F32)],
        compiler_params=_params(("arbitrary", "arbitrary")),
        name="ret_bstate",
    )(proj, proj)
    return pl.pallas_call(
        _ret_out_kernel,
        grid=(batch, nc),
        in_specs=[
            pl.BlockSpec((c, 2 * LANES), lambda b, n: (fwd(b, n), 0)),
            pl.BlockSpec((c, 2 * LANES), lambda b, n: (fwd(b, n), 1)),
            pl.BlockSpec((c, dv), lambda b, n: (fwd(b, n), 1)),
            pl.BlockSpec((c, dv), lambda b, n: (fwd(b, n), 2)),
            pl.BlockSpec((1, 1) + state_shape, lambda b, n: (b, n, 0, 0)),
        ],
        out_specs=pl.BlockSpec((c, dv), lambda b, n: (fwd(b, n), 0)),
        out_shape=jax.ShapeDtypeStruct((t, dv), F32),
        scratch_shapes=[pltpu.VMEM(state_shape, F32)],
        compiler_params=_params(("arbitrary", "arbitrary")),
        name="ret_out",
    )(proj, proj, proj, proj, sb)


def _lru_kernel(*refs, reverse, nt, ts):
    if reverse:
        (xp_ref, x_ref, xn_ref, cw_ref, cb_ref, gw_ref, gb_ref, lam_ref, hf_ref, gr_ref,
         o_ref, xx_ref, a_ref, b_ref, h_ref) = refs
    else:
        (xp_ref, x_ref, xn_ref, cw_ref, cb_ref, gw_ref, gb_ref, lam_ref,
         o_ref, xx_ref, a_ref, b_ref, h_ref) = refs
    i = pl.program_id(1)
    tile = (nt - 1 - i) if reverse else i
    halo = SUBLANES

    @pl.when(i == 0)
    def _():
        h_ref[...] = jnp.zeros_like(h_ref)

    xx_ref[0:halo, :] = jnp.where(tile == 0, 0.0, xp_ref[...])
    xx_ref[halo:halo + ts, :] = x_ref[...]
    xx_ref[halo + ts:2 * halo + ts, :] = jnp.where(tile == nt - 1, 0.0, xn_ref[...])

    lo = LRU_CONV // 2
    xc = cb_ref[...]
    for tap in range(LRU_CONV):
        xc = xc + cw_ref[tap:tap + 1, :] * xx_ref[pl.ds(halo - lo + tap, ts), :]

    for n in range(LRU_BLOCKS):
        cols = slice(n * LRU_BW, (n + 1) * LRU_BW)
        xn = xc[:, cols]
        g = jnp.dot(xn.astype(BF16), gw_ref[n], preferred_element_type=F32) + gb_ref[n:n + 1, :]
        r = _sigmoid(g[:, :LRU_BW])
        ig = _sigmoid(g[:, LRU_BW:])
        log_a = (-LRU_C) * r * _softplus(-lam_ref[:, cols])
        a_ref[:, cols] = jnp.exp(log_a)
        b_ref[:, cols] = jnp.sqrt(1.0 - jnp.exp(2.0 * log_a)) * (ig * xn)

    def step(s, h):
        row = (ts - 1 - s) if reverse else s
        h = a_ref[pl.ds(row, 1), :] * h + b_ref[pl.ds(row, 1), :]
        b_ref[pl.ds(row, 1), :] = h
        return h

    h_ref[...] = lax.fori_loop(0, ts, step, h_ref[...], unroll=8)

    if reverse:
        o_ref[...] = (hf_ref[...] + b_ref[...]) * jax.nn.gelu(gr_ref[...])
    else:
        o_ref[...] = b_ref[...]


def _rglru(proj, conv_w, conv_b, gate_w, gate_b, lam, batch, seq_len):
    t = proj.shape[0]
    w = LRU_BLOCKS * LRU_BW
    ts = min(512, seq_len)
    nt = seq_len // ts
    x_col, g_col = 3, 4
    rows8 = ts // SUBLANES
    last8 = t // SUBLANES - 1
    gw = jnp.concatenate([gate_w[:, 0], gate_w[:, 1]], axis=-1).astype(BF16)
    gb = jnp.concatenate([gate_b[:, 0], gate_b[:, 1]], axis=-1)
    cb = conv_b.reshape(1, w)

    def call(reverse, extra):
        tidx = (lambda b, i: b * nt + (nt - 1 - i)) if reverse else (lambda b, i: b * nt + i)
        z = 1 if reverse else 0
        in_specs = [
            pl.BlockSpec((SUBLANES, w), lambda b, i: (jnp.maximum(tidx(b, i) * rows8 - 1, 0), x_col)),
            pl.BlockSpec((ts, w), lambda b, i: (tidx(b, i), x_col)),
            pl.BlockSpec((SUBLANES, w), lambda b, i: (jnp.minimum((tidx(b, i) + 1) * rows8, last8), x_col)),
            pl.BlockSpec((LRU_CONV, w), lambda b, i: (0, 0)),
            pl.BlockSpec((1, w), lambda b, i: (0, 0)),
            pl.BlockSpec((None, LRU_BLOCKS, LRU_BW, 2 * LRU_BW), lambda b, i: (z, 0, 0, 0)),
            pl.BlockSpec((None, LRU_BLOCKS, 2 * LRU_BW), lambda b, i: (z, 0, 0)),
            pl.BlockSpec((None, 1, w), lambda b, i: (z, 0, 0)),
        ]
        args = [proj, proj, proj, conv_w, cb, gw, gb, lam.reshape(2, 1, w)]
        if reverse:
            in_specs += [pl.BlockSpec((ts, w), lambda b, i: (tidx(b, i), 0)),
                         pl.BlockSpec((ts, w), lambda b, i: (tidx(b, i), g_col))]
            args += [extra, proj]
        return pl.pallas_call(
            functools.partial(_lru_kernel, reverse=reverse, nt=nt, ts=ts),
            grid=(batch, nt),
            in_specs=in_specs,
            out_specs=pl.BlockSpec((ts, w), lambda b, i: (tidx(b, i), 0)),
            out_shape=jax.ShapeDtypeStruct((t, w), F32),
            scratch_shapes=[pltpu.VMEM((ts + 2 * SUBLANES, w), F32), pltpu.VMEM((ts, w), F32),
                            pltpu.VMEM((ts, w), F32), pltpu.VMEM((1, w), F32)],
            compiler_params=_params(("arbitrary", "arbitrary")),
            name="lru_bwd" if reverse else "lru_fwd",
        )(*args)

    h_fwd = call(False, None)
    return call(True, h_fwd)


def _gla_kernel(*refs, reverse, ts, layer):
    if reverse:
        q_ref, v_ref, z_ref, lbp_ref, of_ref, g_ref, nw_ref, o_ref, s_ref, acc_ref = refs
    else:
        q_ref, v_ref, z_ref, lbp_ref, o_ref, s_ref = refs
        acc_ref = o_ref
    c = GLA_CHUNK
    i = pl.program_id(1)

    @pl.when(i == 0)
    def _():
        s_ref[...] = jnp.zeros_like(s_ref)

    p = lbp_ref[...]
    e = jnp.exp(p - jnp.max(p, axis=0, keepdims=True))
    sm = e / jnp.sum(e, axis=0, keepdims=True)
    lb = jnp.zeros((1, p.shape[1]), F32)
    for r in range(1, layer + 1):
        lb = lb + sm[r:r + 1, :]
    log_lb = jnp.log(lb)
    log_1mlb = jnp.log1p(-lb)

    ii = lax.broadcasted_iota(jnp.int32, (c, c), 0)
    jj = lax.broadcasted_iota(jnp.int32, (c, c), 1)
    keep = (jj >= ii) if reverse else (jj <= ii)
    tri = keep.astype(F32)
    mid = c // 2
    end = 0 if reverse else c - 1
    nchunks = ts // c

    for cc in range(nchunks):
        ch = (nchunks - 1 - cc) if reverse else cc
        rows = pl.ds(ch * c, c)
        z = z_ref[rows, :]
        u = log_1mlb - _softplus(-z)
        log_f = jnp.maximum(log_lb, u) + jnp.log1p(jnp.exp(-jnp.abs(log_lb - u)))
        key = (1.0 - lb) * _sigmoid(-z)
        qs = _silu(q_ref[rows, :]) * (HG_DK ** -0.5)
        bcum = jnp.dot(tri, log_f, preferred_element_type=F32, precision=lax.Precision.HIGHEST)
        ref_row = bcum[mid:mid + 1, :]
        b_end = bcum[end:end + 1, :]
        qe = (qs * jnp.exp(bcum - ref_row)).astype(BF16)
        ke = (key * jnp.exp(ref_row - bcum)).astype(BF16)
        qb = (qs * jnp.exp(bcum)).astype(BF16)
        kd = (key * jnp.exp(b_end - bcum)).astype(BF16)
        decay = jnp.exp(b_end)
        vb = v_ref[rows, :].astype(BF16)
        for h in range(HG_HEADS):
            cols = slice(h * HG_DK, (h + 1) * HG_DK)
            att = lax.dot_general(qe[:, cols], ke[:, cols], _NT, preferred_element_type=F32)
            att = jnp.where(keep, att, 0.0).astype(BF16)
            st = s_ref[h]
            o = (jnp.dot(att, vb[:, cols], preferred_element_type=F32)
                 + lax.dot_general(qb[:, cols], st.astype(BF16), _NT, preferred_element_type=F32))
            acc_ref[rows, cols] = o
            s_ref[h] = st * decay[:, cols] + lax.dot_general(vb[:, cols], kd[:, cols], _TN,
                                                             preferred_element_type=F32)

    if reverse:
        o = acc_ref[...] + of_ref[...]
        ms = jnp.mean(o * o, axis=-1, keepdims=True)
        o_ref[...] = o * lax.rsqrt(ms + RMS_EPS) * nw_ref[...] * _silu(g_ref[...])


def _gla(proj, lower_bounds, norm_w, layer, batch, seq_len):
    t = proj.shape[0]
    d = HG_HEADS * HG_DK
    ts = min(256, seq_len)
    nt = seq_len // ts
    depth = lower_bounds.shape[0]
    state = pltpu.VMEM((HG_HEADS, HG_DV, HG_DK), F32)

    def call(reverse, extra):
        tidx = (lambda b, i: b * nt + (nt - 1 - i)) if reverse else (lambda b, i: b * nt + i)
        in_specs = [
            pl.BlockSpec((ts, d), lambda b, i: (tidx(b, i), 0)),
            pl.BlockSpec((ts, d), lambda b, i: (tidx(b, i), 1)),
            pl.BlockSpec((ts, d), lambda b, i: (tidx(b, i), 3 if reverse else 2)),
            pl.BlockSpec((depth, d), lambda b, i: (0, 0)),
        ]
        args = [proj, proj, proj, lower_bounds]
        scratch = [state]
        if reverse:
            in_specs += [pl.BlockSpec((ts, d), lambda b, i: (tidx(b, i), 0)),
                         pl.BlockSpec((ts, d), lambda b, i: (tidx(b, i), 4)),
                         pl.BlockSpec((1, d), lambda b, i: (0, 0))]
            args += [extra, proj, norm_w.reshape(1, d)]
            scratch.append(pltpu.VMEM((ts, d), F32))
        return pl.pallas_call(
            functools.partial(_gla_kernel, reverse=reverse, ts=ts, layer=layer),
            grid=(batch, nt),
            in_specs=in_specs,
            out_specs=pl.BlockSpec((ts, d), lambda b, i: (tidx(b, i), 0)),
            out_shape=jax.ShapeDtypeStruct((t, d), F32),
            scratch_shapes=scratch,
            compiler_params=_params(("arbitrary", "arbitrary")),
            name="gla_bwd" if reverse else "gla_fwd",
        )(*args)

    o_fwd = call(False, None)
    return call(True, o_fwd)


def _top2(p):
    v1 = jnp.maximum(jnp.maximum(p[0], p[1]), jnp.maximum(p[2], p[3]))
    i1 = jnp.where(p[0] == v1, 0, jnp.where(p[1] == v1, 1, jnp.where(p[2] == v1, 2, 3)))
    q = [jnp.where(i1 == k, -1.0, p[k]) for k in range(4)]
    v2 = jnp.maximum(jnp.maximum(q[0], q[1]), jnp.maximum(q[2], q[3]))
    i2 = jnp.where(q[0] == v2, 0, jnp.where(q[1] == v2, 1, jnp.where(q[2] == v2, 2, 3)))
    return v1, i1, v2, i2


def _outproj_kernel(*refs, n_mix, alpha, tm):
    mix_refs = refs[:n_mix]
    w_refs = refs[n_mix:2 * n_mix]
    h_ref, lnw_ref, lnb_ref, rw_ref, o_ref, info_ref, cnt_ref, carry_ref = refs[2 * n_mix:]
    i = pl.program_id(0)

    @pl.when(i == 0)
    def _():
        carry_ref[...] = jnp.zeros_like(carry_ref)

    y = jnp.dot(mix_refs[0][...].astype(BF16), w_refs[0][...], preferred_element_type=F32)
    for m in range(1, n_mix):
        y = y + jnp.dot(mix_refs[m][...].astype(BF16), w_refs[m][...], preferred_element_type=F32)
    h1 = _layer_norm(alpha * h_ref[...] + y, lnw_ref[...], lnb_ref[...])
    o_ref[...] = h1

    logits = lax.dot_general(rw_ref[...], h1, _NT, preferred_element_type=F32,
                             precision=lax.Precision.HIGHEST)
    ex = jnp.exp(logits - jnp.max(logits, axis=0, keepdims=True))
    probs = ex / jnp.sum(ex, axis=0, keepdims=True)
    best = None
    for g in range(N_GROUPS):
        rows = [probs[g * EXPERTS_PER_GROUP + k:g * EXPERTS_PER_GROUP + k + 1, :]
                for k in range(EXPERTS_PER_GROUP)]
        v1, i1, v2, i2 = _top2(rows)
        score = v1 + v2
        cand = (score, v1, i1 + g * EXPERTS_PER_GROUP, v2, i2 + g * EXPERTS_PER_GROUP)
        if best is None:
            best = cand
        else:
            take = cand[0] > best[0]
            best = tuple(jnp.where(take, cn, bs) for cn, bs in zip(cand, best))
    _, v1, e1, v2, e2 = best
    denom = v1 + v2
    g1 = v1 / denom
    g2 = v2 / denom

    eid = lax.broadcasted_iota(jnp.int32, (N_EXPERTS, tm), 0)
    oh1 = (eid == e1).astype(F32)
    oh2 = (eid == e2).astype(F32)
    oh = oh1 + oh2
    tt = lax.broadcasted_iota(jnp.int32, (tm, tm), 0)
    uu = lax.broadcasted_iota(jnp.int32, (tm, tm), 1)
    before = (tt < uu).astype(BF16)
    base = carry_ref[:, 0:1] + jnp.dot(oh.astype(BF16), before, preferred_element_type=F32)
    rank1 = jnp.sum(oh1 * base, axis=0, keepdims=True)
    rank2 = jnp.sum(oh2 * base, axis=0, keepdims=True)
    carry_ref[...] = carry_ref[...] + jnp.sum(oh, axis=1, keepdims=True)
    cnt_ref[...] = carry_ref[...]
    zero = jnp.zeros_like(g1)
    info_ref[...] = jnp.concatenate(
        [e1.astype(F32), e2.astype(F32), g1, g2, rank1, rank2, zero, zero], axis=0)


def _out_project(mixes, weights_bf16, h, ln_w, ln_b, router_wt, alpha):
    t, d = h.shape
    tm = 512
    n_mix = len(mixes)
    in_specs = ([pl.BlockSpec((tm, m.shape[1]), lambda i: (i, 0)) for m in mixes]
                + [pl.BlockSpec(w.shape, lambda i: (0, 0)) for w in weights_bf16]
                + [pl.BlockSpec((tm, d), lambda i: (i, 0)),
                   pl.BlockSpec((1, d), lambda i: (0, 0)),
                   pl.BlockSpec((1, d), lambda i: (0, 0)),
                   pl.BlockSpec((N_EXPERTS, d), lambda i: (0, 0))])
    return pl.pallas_call(
        functools.partial(_outproj_kernel, n_mix=n_mix, alpha=alpha, tm=tm),
        grid=(t // tm,),
        in_specs=in_specs,
        out_specs=[pl.BlockSpec((tm, d), lambda i: (i, 0)),
                   pl.BlockSpec((SUBLANES, tm), lambda i: (0, i)),
                   pl.BlockSpec((N_EXPERTS, LANES), lambda i: (0, 0))],
        out_shape=[jax.ShapeDtypeStruct((t, d), F32),
                   jax.ShapeDtypeStruct((SUBLANES, t), F32),
                   jax.ShapeDtypeStruct((N_EXPERTS, LANES), F32)],
        scratch_shapes=[pltpu.VMEM((N_EXPERTS, LANES), F32)],
        compiler_params=_params(("arbitrary",)),
        name="out_proj_router",
    )(*mixes, *weights_bf16, h, ln_w.reshape(1, d), ln_b.reshape(1, d), router_wt)


def _row_copy(src_ref, src_row, dst_ref, dst_row, sem):
    return pltpu.make_async_copy(src_ref.at[pl.ds(src_row, 1)], dst_ref.at[pl.ds(dst_row, 1)], sem)


def _dispatch_kernel(d1_ref, d2_ref, h_ref, xin_ref, xb_ref, sem, *, tm):
    del xin_ref
    base = pl.program_id(0) * tm

    def start(t, carry):
        _row_copy(h_ref, t, xb_ref, d1_ref[base + t], sem.at[0]).start()
        _row_copy(h_ref, t, xb_ref, d2_ref[base + t], sem.at[1]).start()
        return carry

    def wait(t, carry):
        _row_copy(h_ref, 0, xb_ref, 0, sem.at[0]).wait()
        _row_copy(h_ref, 0, xb_ref, 0, sem.at[1]).wait()
        return carry

    lax.fori_loop(0, tm, start, 0)
    lax.fori_loop(0, tm, wait, 0)


def _dispatch(h, dest1, dest2, n_rows):
    t, d = h.shape
    tm = 512
    slots = jnp.zeros((n_rows, d), F32)
    return pl.pallas_call(
        functools.partial(_dispatch_kernel, tm=tm),
        grid_spec=pltpu.PrefetchScalarGridSpec(
            num_scalar_prefetch=2,
            grid=(t // tm,),
            in_specs=[pl.BlockSpec((tm, d), lambda i, d1, d2: (i, 0)),
                      pl.BlockSpec(memory_space=pl.ANY)],
            out_specs=pl.BlockSpec(memory_space=pl.ANY),
            scratch_shapes=[pltpu.SemaphoreType.DMA((2,))],
        ),
        out_shape=jax.ShapeDtypeStruct((n_rows, d), F32),
        input_output_aliases={3: 0},
        compiler_params=_params(("arbitrary",)),
        name="moe_dispatch",
    )(dest1, dest2, h, slots)


def _expert_kernel(be_ref, nv_ref, x_ref, wg_ref, wu_ref, wd_ref, o_ref, wgb_ref, wub_ref, wdb_ref):
    i = pl.program_id(0)
    changed = jnp.logical_or(i == 0, be_ref[i] != be_ref[jnp.maximum(i - 1, 0)])

    @pl.when(changed)
    def _():
        wgb_ref[...] = wg_ref[0].astype(BF16)
        wub_ref[...] = wu_ref[0].astype(BF16)
        wdb_ref[...] = wd_ref[0].astype(BF16)

    @pl.when(i < nv_ref[0])
    def _():
        x = x_ref[...].astype(BF16)
        gate = jnp.dot(x, wgb_ref[...], preferred_element_type=F32)
        up = jnp.dot(x, wub_ref[...], preferred_element_type=F32)
        hid = (_silu(gate) * up).astype(BF16)
        o_ref[...] = jnp.dot(hid, wdb_ref[...], preferred_element_type=F32)

    @pl.when(i >= nv_ref[0])
    def _():
        o_ref[...] = jnp.zeros_like(o_ref)


def _experts(xb, block_e, n_valid, w_gate, w_up, w_down):
    p, d = xb.shape
    de = w_gate.shape[2]
    nb = p // MOE_ROWS
    return pl.pallas_call(
        _expert_kernel,
        grid_spec=pltpu.PrefetchScalarGridSpec(
            num_scalar_prefetch=2,
            grid=(nb,),
            in_specs=[
                pl.BlockSpec((MOE_ROWS, d), lambda i, be, nv: (jnp.minimum(i, nv[0] - 1), 0)),
                pl.BlockSpec((1, d, de), lambda i, be, nv: (be[i], 0, 0)),
                pl.BlockSpec((1, d, de), lambda i, be, nv: (be[i], 0, 0)),
                pl.BlockSpec((1, de, d), lambda i, be, nv: (be[i], 0, 0)),
            ],
            out_specs=pl.BlockSpec((MOE_ROWS, d), lambda i, be, nv: (i, 0)),
            scratch_shapes=[pltpu.VMEM((d, de), BF16), pltpu.VMEM((d, de), BF16),
                            pltpu.VMEM((de, d), BF16)],
        ),
        out_shape=jax.ShapeDtypeStruct((p, d), F32),
        compiler_params=_params(("arbitrary",)),
        name="moe_experts",
    )(block_e, n_valid, xb, w_gate, w_up, w_down)


def _combine_kernel(d1_ref, d2_ref, h_ref, gates_ref, lnw_ref, lnb_ref, yb_ref, o_ref, r_ref, sem,
                    *, tm, alpha):
    base = pl.program_id(0) * tm

    def start(t, carry):
        _row_copy(yb_ref, d1_ref[base + t], r_ref.at[0], t, sem.at[0]).start()
        _row_copy(yb_ref, d2_ref[base + t], r_ref.at[1], t, sem.at[1]).start()
        return carry

    def wait(t, carry):
        _row_copy(yb_ref, 0, r_ref.at[0], 0, sem.at[0]).wait()
        _row_copy(yb_ref, 0, r_ref.at[1], 0, sem.at[1]).wait()
        return carry

    lax.fori_loop(0, tm, start, 0)
    lax.fori_loop(0, tm, wait, 0)
    y = gates_ref[:, 0:1] * r_ref[0] + gates_ref[:, 1:2] * r_ref[1]
    o_ref[...] = _layer_norm(alpha * h_ref[...] + y, lnw_ref[...], lnb_ref[...])


def _combine(h, yb, dest1, dest2, gates, ln_w, ln_b, alpha):
    t, d = h.shape
    tm = 512
    return pl.pallas_call(
        functools.partial(_combine_kernel, tm=tm, alpha=alpha),
        grid_spec=pltpu.PrefetchScalarGridSpec(
            num_scalar_prefetch=2,
            grid=(t // tm,),
            in_specs=[pl.BlockSpec((tm, d), lambda i, d1, d2: (i, 0)),
                      pl.BlockSpec((tm, 2), lambda i, d1, d2: (i, 0)),
                      pl.BlockSpec((1, d), lambda i, d1, d2: (0, 0)),
                      pl.BlockSpec((1, d), lambda i, d1, d2: (0, 0)),
                      pl.BlockSpec(memory_space=pl.ANY)],
            out_specs=pl.BlockSpec((tm, d), lambda i, d1, d2: (i, 0)),
            scratch_shapes=[pltpu.VMEM((2, tm, d), F32), pltpu.SemaphoreType.DMA((2,))],
        ),
        out_shape=jax.ShapeDtypeStruct((t, d), F32),
        compiler_params=_params(("arbitrary",)),
        name="moe_combine",
    )(dest1, dest2, h, gates, ln_w.reshape(1, d), ln_b.reshape(1, d), yb)


def _moe(h1, info, counts, w_gate, w_up, w_down, ln_w, ln_b, alpha):
    t = h1.shape[0]
    n_rows = 2 * t + N_EXPERTS * MOE_ROWS
    nb = n_rows // MOE_ROWS
    cnt = counts[:, 0].astype(jnp.int32)
    padded = (cnt + MOE_ROWS - 1) // MOE_ROWS * MOE_ROWS
    ends = jnp.cumsum(padded)
    pstart = ends - padded
    e1 = info[0].astype(jnp.int32)
    e2 = info[1].astype(jnp.int32)
    dest1 = pstart[e1] + info[4].astype(jnp.int32)
    dest2 = pstart[e2] + info[5].astype(jnp.int32)
    gates = jnp.stack([info[2], info[3]], axis=1)
    n_valid = (ends[-1] // MOE_ROWS).astype(jnp.int32).reshape(1)
    blk = jnp.minimum(jnp.arange(nb, dtype=jnp.int32), n_valid[0] - 1) * MOE_ROWS
    block_e = jnp.minimum(jnp.searchsorted(ends, blk, side="right"), N_EXPERTS - 1).astype(jnp.int32)

    xb = _dispatch(h1, dest1, dest2, n_rows)
    yb = _experts(xb, block_e, n_valid, w_gate, w_up, w_down)
    return _combine(h1, yb, dest1, dest2, gates, ln_w, ln_b, alpha)


def _rotary_column_order(w_in):
    d = w_in.shape[0]
    nq = RET_HEADS * RET_DK

    def perm(w):
        return w.reshape(d, RET_HEADS, RET_DK // 2, 2).transpose(0, 3, 1, 2).reshape(d, nq)

    return jnp.concatenate([perm(w_in[:, :nq]), perm(w_in[:, nq:2 * nq]), w_in[:, 2 * nq:]], axis=1)


def kernel(x, w_in_even, w_out_even, lru_conv_w, lru_conv_b, lru_gate_w, lru_gate_b, lru_lambda,
           w_in_odd, w_out_odd, hg_lower_bounds, hg_norm_w, ln_w, ln_b, router_w,
           moe_w_gate, moe_w_up, moe_w_down):
    batch, seq_len, d = x.shape
    depth = ln_w.shape[0]
    alpha = (2.0 * depth) ** 0.25
    t = batch * seq_len
    h = x.reshape(t, d)
    router_wt = router_w.T
    half = RET_DK // 2
    inv_freq = ROPE_BASE ** (-jnp.arange(0, RET_DK, 2, dtype=F32) / RET_DK)
    inv_freq = jnp.tile(inv_freq, LANES // half).reshape(1, LANES)

    for layer in range(depth):
        j = layer // 2
        if layer % 2 == 0:
            w_in = _rotary_column_order(w_in_even[j]).astype(BF16)
            proj = _project(h, w_in, inv_freq, rotary=True, seq_len=seq_len)
            ret = _retention(proj, batch, seq_len)
            lru = _rglru(proj, lru_conv_w[j], lru_conv_b[j], lru_gate_w[j], lru_gate_b[j],
                         lru_lambda[j], batch, seq_len)
            w_out = w_out_even[j].astype(BF16)
            nret = RET_HEADS * RET_DV
            mixes, weights = [ret, lru], [w_out[:nret], w_out[nret:]]
        else:
            proj = _project(h, w_in_odd[j].astype(BF16), inv_freq, rotary=False, seq_len=seq_len)
            mix = _gla(proj, hg_lower_bounds, hg_norm_w[j], layer, batch, seq_len)
            mixes, weights = [mix], [w_out_odd[j].astype(BF16)]
        h1, info, counts = _out_project(mixes, weights, h, ln_w[layer, 0], ln_b[layer, 0],
                                        router_wt, alpha)
        h = _moe(h1, info, counts, moe_w_gate[layer], moe_w_up[layer], moe_w_down[layer],
                 ln_w[layer, 1], ln_b[layer, 1], alpha)
    return h.reshape(batch, seq_len, d)
```

```python
import functools
import math

import jax
import jax.numpy as jnp
from jax import lax
from jax.experimental import pallas as pl
from jax.experimental.pallas import tpu as pltpu

F32 = jnp.float32
BF16 = jnp.bfloat16

RET_HEADS = 4
RET_DK = 64
RET_DV = 128
RET_CHUNK = 128
ROPE_BASE = 10000.0
LRU_BLOCKS = 4
LRU_BW = 128
LRU_CONV = 4
LRU_C = 8.0
HG_HEADS = 8
HG_DK = 128
HG_DV = 128
N_EXPERTS = 16
N_GROUPS = 4
EXPERTS_PER_GROUP = 4
LN_EPS = 1e-5
RMS_EPS = 1e-6

LANES = 128
SUBLANES = 8
GLA_CHUNK = 32
MOE_ROWS = 256
VMEM_LIMIT = 48 * 1024 * 1024

_NT = (((1,), (1,)), ((), ()))
_TN = (((0,), (0,)), ((), ()))


def _params(sem):
    return pltpu.CompilerParams(dimension_semantics=sem, vmem_limit_bytes=VMEM_LIMIT)


def _sigmoid(x):
    return 1.0 / (1.0 + jnp.exp(-x))


def _silu(x):
    return x * _sigmoid(x)


def _softplus(x):
    return jnp.maximum(x, 0.0) + jnp.log1p(jnp.exp(-jnp.abs(x)))


def _layer_norm(u, w, b):
    mu = jnp.mean(u, axis=-1, keepdims=True)
    d = u - mu
    var = jnp.mean(d * d, axis=-1, keepdims=True)
    return d * lax.rsqrt(var + LN_EPS) * w + b


def _proj_kernel(x_ref, w_ref, inv_ref, o_ref, xb_ref, *, rotary, seq_len, tm):
    i = pl.program_id(0)
    j = pl.program_id(1)

    @pl.when(j == 0)
    def _():
        xb_ref[...] = x_ref[...].astype(BF16)

    acc = jnp.dot(xb_ref[...], w_ref[...], preferred_element_type=F32)

    if not rotary:
        o_ref[...] = acc
        return

    @pl.when(j != 0)
    def _():
        o_ref[...] = acc

    @pl.when(j == 0)
    def _():
        row = lax.broadcasted_iota(jnp.int32, (tm, 1), 0) + i * tm
        pos = (row % seq_len).astype(F32)
        ang = pos * inv_ref[...]
        cos = jnp.cos(ang)
        sin = jnp.sin(ang)
        scale = RET_DK ** -0.5
        q1, q2 = acc[:, 0:128], acc[:, 128:256]
        k1, k2 = acc[:, 256:384], acc[:, 384:512]
        o_ref[:, 0:128] = (q1 * cos - q2 * sin) * scale
        o_ref[:, 128:256] = (q1 * sin + q2 * cos) * scale
        o_ref[:, 256:384] = k1 * cos - k2 * sin
        o_ref[:, 384:512] = k1 * sin + k2 * cos


def _project(x, w_bf16, inv_freq, *, rotary, seq_len):
    t, k = x.shape
    n = w_bf16.shape[1]
    tm = min(1024, seq_len)
    tn = 512
    return pl.pallas_call(
        functools.partial(_proj_kernel, rotary=rotary, seq_len=seq_len, tm=tm),
        grid=(t // tm, n // tn),
        in_specs=[
            pl.BlockSpec((tm, k), lambda i, j: (i, 0)),
            pl.BlockSpec((k, tn), lambda i, j: (0, j)),
            pl.BlockSpec((1, LANES), lambda i, j: (0, 0)),
        ],
        out_specs=pl.BlockSpec((tm, tn), lambda i, j: (i, j)),
        out_shape=jax.ShapeDtypeStruct((t, n), F32),
        scratch_shapes=[pltpu.VMEM((tm, k), BF16)],
        compiler_params=_params(("arbitrary", "arbitrary")),
        name="in_proj",
    )(x, w_bf16, inv_freq)


def _ret_log_gamma(head):
    out = jnp.full(head.shape, math.log1p(-(2.0 ** -5.0)), F32)
    for h in range(1, RET_HEADS):
        out = jnp.where(head == h, math.log1p(-(2.0 ** (-5.0 - h))), out)
    return out


def _ret_lane_log_gamma():
    lane = lax.broadcasted_iota(jnp.int32, (1, 2 * LANES), 1)
    return _ret_log_gamma((lane % LANES) // (RET_DK // 2))


def _ret_state_mask():
    row_head = lax.broadcasted_iota(jnp.int32, (RET_HEADS * RET_DV, 2 * LANES), 0) // RET_DV
    col_head = (lax.broadcasted_iota(jnp.int32, (RET_HEADS * RET_DV, 2 * LANES), 1) % LANES) // (RET_DK // 2)
    return row_head == col_head


def _ret_bstate_kernel(k_ref, v_ref, sb_ref, s_ref):
    n = pl.program_id(1)
    c = RET_CHUNK

    @pl.when(n == 0)
    def _():
        s_ref[...] = jnp.zeros_like(s_ref)

    sb_ref[0, 0] = s_ref[...].astype(BF16)
    lg = _ret_lane_log_gamma()
    idx = lax.broadcasted_iota(jnp.int32, (c, 1), 0).astype(F32)
    kb = (k_ref[...] * jnp.exp(lg * idx)).astype(BF16)
    upd = lax.dot_general(v_ref[...].astype(BF16), kb, _TN, preferred_element_type=F32)
    s_ref[...] = s_ref[...] * jnp.exp(lg * float(c)) + upd


def _ret_out_kernel(q_ref, k_ref, v_ref, g_ref, sb_ref, o_ref, s_ref):
    n = pl.program_id(1)
    c = RET_CHUNK

    @pl.when(n == 0)
    def _():
        s_ref[...] = jnp.zeros_like(s_ref)

    lg = _ret_lane_log_gamma()
    idx = lax.broadcasted_iota(jnp.int32, (c, 1), 0).astype(F32)
    q = q_ref[...]
    k = k_ref[...]
    v = v_ref[...]
    vb = v.astype(BF16)
    kb = k.astype(BF16)
    mask = _ret_state_mask()

    qf = (q * jnp.exp(lg * (idx + 1.0))).astype(BF16)
    qb = (q * jnp.exp(lg * (float(c) - idx))).astype(BF16)
    sf = jnp.where(mask, s_ref[...], 0.0).astype(BF16)
    sb = jnp.where(mask, sb_ref[0, 0], 0.0).astype(BF16)
    cross = (lax.dot_general(qf, sf, _NT, preferred_element_type=F32)
             + lax.dot_general(qb, sb, _NT, preferred_element_type=F32))

    lane = lax.broadcasted_iota(jnp.int32, (1, 2 * LANES), 1)
    lane_head = (lane % LANES) // (RET_DK // 2)
    ii = lax.broadcasted_iota(jnp.int32, (c, c), 0)
    jj = lax.broadcasted_iota(jnp.int32, (c, c), 1)
    dist = jnp.abs(ii - jj).astype(F32)
    for h in range(RET_HEADS):
        qh = jnp.where(lane_head == h, q, 0.0).astype(BF16)
        s = lax.dot_general(qh, kb, _NT, preferred_element_type=F32)
        s = s * jnp.exp(math.log1p(-(2.0 ** (-5.0 - h))) * dist)
        cols = slice(h * RET_DV, (h + 1) * RET_DV)
        o = jnp.dot(s.astype(BF16), vb[:, cols], preferred_element_type=F32) + cross[:, cols]
        mu = jnp.mean(o, axis=-1, keepdims=True)
        d = o - mu
        var = jnp.mean(d * d, axis=-1, keepdims=True)
        o_ref[:, cols] = _silu(g_ref[:, cols]) * (d * lax.rsqrt(var + LN_EPS))

    kf = (k * jnp.exp(lg * (float(c) - 1.0 - idx))).astype(BF16)
    upd = lax.dot_general(vb, kf, _TN, preferred_element_type=F32)
    s_ref[...] = s_ref[...] * jnp.exp(lg * float(c)) + upd


def _retention(proj, batch, seq_len):
    t = proj.shape[0]
    c = RET_CHUNK
    nc = seq_len // c
    dv = RET_HEADS * RET_DV
    state_shape = (dv, 2 * LANES)
    rev = lambda b, n: b * nc + (nc - 1 - n)
    fwd = lambda b, n: b * nc + n
    sb = pl.pallas_call(
        _ret_bstate_kernel,
        grid=(batch, nc),
        in_specs=[
            pl.BlockSpec((c, 2 * LANES), lambda b, n: (rev(b, n), 1)),
            pl.BlockSpec((c, dv), lambda b, n: (rev(b, n), 1)),
        ],
        out_specs=pl.BlockSpec((1, 1) + state_shape, lambda b, n: (b, nc - 1 - n, 0, 0)),
        out_shape=jax.ShapeDtypeStruct((batch, nc) + state_shape, BF16),
        scratch_shapes=[pltpu.VMEM(state_shape, F32)],
        compiler_params=_params(("arbitrary", "arbitrary")),
        name="ret_bstate",
    )(proj, proj)
    return pl.pallas_call(
        _ret_out_kernel,
        grid=(batch, nc),
        in_specs=[
            pl.BlockSpec((c, 2 * LANES), lambda b, n: (fwd(b, n), 0)),
            pl.BlockSpec((c, 2 * LANES), lambda b, n: (fwd(b, n), 1)),
            pl.BlockSpec((c, dv), lambda b, n: (fwd(b, n), 1)),
            pl.BlockSpec((c, dv), lambda b, n: (fwd(b, n), 2)),
            pl.BlockSpec((1, 1) + state_shape, lambda b, n: (b, n, 0, 0)),
        ],
        out_specs=pl.BlockSpec((c, dv), lambda b, n: (fwd(b, n), 0)),
        out_shape=jax.ShapeDtypeStruct((t, dv), F32),
        scratch_shapes=[pltpu.VMEM(state_shape, F32)],
        compiler_params=_params(("arbitrary", "arbitrary")),
        name="ret_out",
    )(proj, proj, proj, proj, sb)


def _lru_kernel(*refs, reverse, nt, ts):
    if reverse:
        (xp_ref, x_ref, xn_ref, cw_ref, cb_ref, gw_ref, gb_ref, lam_ref, hf_ref, gr_ref,
         o_ref, xx_ref, a_ref, b_ref, h_ref) = refs
    else:
        (xp_ref, x_ref, xn_ref, cw_ref, cb_ref, gw_ref, gb_ref, lam_ref,
         o_ref, xx_ref, a_ref, b_ref, h_ref) = refs
    i = pl.program_id(1)
    tile = (nt - 1 - i) if reverse else i
    halo = SUBLANES

    @pl.when(i == 0)
    def _():
        h_ref[...] = jnp.zeros_like(h_ref)

    xx_ref[0:halo, :] = jnp.where(tile == 0, 0.0, xp_ref[...])
    xx_ref[halo:halo + ts, :] = x_ref[...]
    xx_ref[halo + ts:2 * halo + ts, :] = jnp.where(tile == nt - 1, 0.0, xn_ref[...])

    lo = LRU_CONV // 2
    xc = cb_ref[...]
    for tap in range(LRU_CONV):
        xc = xc + cw_ref[tap:tap + 1, :] * xx_ref[pl.ds(halo - lo + tap, ts), :]

    for n in range(LRU_BLOCKS):
        cols = slice(n * LRU_BW, (n + 1) * LRU_BW)
        xn = xc[:, cols]
        g = jnp.dot(xn.astype(BF16), gw_ref[n], preferred_element_type=F32) + gb_ref[n:n + 1, :]
        r = _sigmoid(g[:, :LRU_BW])
        ig = _sigmoid(g[:, LRU_BW:])
        log_a = (-LRU_C) * r * _softplus(-lam_ref[:, cols])
        a_ref[:, cols] = jnp.exp(log_a)
        b_ref[:, cols] = jnp.sqrt(1.0 - jnp.exp(2.0 * log_a)) * (ig * xn)

    def step(s, h):
        row = (ts - 1 - s) if reverse else s
        h = a_ref[pl.ds(row, 1), :] * h + b_ref[pl.ds(row, 1), :]
        b_ref[pl.ds(row, 1), :] = h
        return h

    h_ref[...] = lax.fori_loop(0, ts, step, h_ref[...], unroll=8)

    if reverse:
        o_ref[...] = (hf_ref[...] + b_ref[...]) * jax.nn.gelu(gr_ref[...])
    else:
        o_ref[...] = b_ref[...]


def _rglru(proj, conv_w, conv_b, gate_w, gate_b, lam, batch, seq_len):
    t = proj.shape[0]
    w = LRU_BLOCKS * LRU_BW
    ts = min(512, seq_len)
    nt = seq_len // ts
    x_col, g_col = 3, 4
    rows8 = ts // SUBLANES
    last8 = t // SUBLANES - 1
    gw = jnp.concatenate([gate_w[:, 0], gate_w[:, 1]], axis=-1).astype(BF16)
    gb = jnp.concatenate([gate_b[:, 0], gate_b[:, 1]], axis=-1)
    cb = conv_b.reshape(1, w)

    def call(reverse, extra):
        tidx = (lambda b, i: b * nt + (nt - 1 - i)) if reverse else (lambda b, i: b * nt + i)
        z = 1 if reverse else 0
        in_specs = [
            pl.BlockSpec((SUBLANES, w), lambda b, i: (jnp.maximum(tidx(b, i) * rows8 - 1, 0), x_col)),
            pl.BlockSpec((ts, w), lambda b, i: (tidx(b, i), x_col)),
            pl.BlockSpec((SUBLANES, w), lambda b, i: (jnp.minimum((tidx(b, i) + 1) * rows8, last8), x_col)),
            pl.BlockSpec((LRU_CONV, w), lambda b, i: (0, 0)),
            pl.BlockSpec((1, w), lambda b, i: (0, 0)),
            pl.BlockSpec((None, LRU_BLOCKS, LRU_BW, 2 * LRU_BW), lambda b, i: (z, 0, 0, 0)),
            pl.BlockSpec((None, LRU_BLOCKS, 2 * LRU_BW), lambda b, i: (z, 0, 0)),
            pl.BlockSpec((None, 1, w), lambda b, i: (z, 0, 0)),
        ]
        args = [proj, proj, proj, conv_w, cb, gw, gb, lam.reshape(2, 1, w)]
        if reverse:
            in_specs += [pl.BlockSpec((ts, w), lambda b, i: (tidx(b, i), 0)),
                         pl.BlockSpec((ts, w), lambda b, i: (tidx(b, i), g_col))]
            args += [extra, proj]
        return pl.pallas_call(
            functools.partial(_lru_kernel, reverse=reverse, nt=nt, ts=ts),
            grid=(batch, nt),
            in_specs=in_specs,
            out_specs=pl.BlockSpec((ts, w), lambda b, i: (tidx(b, i), 0)),
            out_shape=jax.ShapeDtypeStruct((t, w), F32),
            scratch_shapes=[pltpu.VMEM((ts + 2 * SUBLANES, w), F32), pltpu.VMEM((ts, w), F32),
                            pltpu.VMEM((ts, w), F32), pltpu.VMEM((1, w), F32)],
            compiler_params=_params(("arbitrary", "arbitrary")),
            name="lru_bwd" if reverse else "lru_fwd",
        )(*args)

    h_fwd = call(False, None)
    return call(True, h_fwd)


def _gla_kernel(*refs, reverse, ts, layer):
    if reverse:
        q_ref, v_ref, z_ref, lbp_ref, of_ref, g_ref, nw_ref, o_ref, s_ref, acc_ref = refs
    else:
        q_ref, v_ref, z_ref, lbp_ref, o_ref, s_ref = refs
        acc_ref = o_ref
    c = GLA_CHUNK
    i = pl.program_id(1)

    @pl.when(i == 0)
    def _():
        s_ref[...] = jnp.zeros_like(s_ref)

    p = lbp_ref[...]
    e = jnp.exp(p - jnp.max(p, axis=0, keepdims=True))
    sm = e / jnp.sum(e, axis=0, keepdims=True)
    lb = jnp.zeros((1, p.shape[1]), F32)
    for r in range(1, layer + 1):
        lb = lb + sm[r:r + 1, :]

    ii = lax.broadcasted_iota(jnp.int32, (c, c), 0)
    jj = lax.broadcasted_iota(jnp.int32, (c, c), 1)
    tri = ((jj >= ii) if reverse else (jj <= ii)).astype(BF16)
    hc = HG_HEADS * c
    si = lax.broadcasted_iota(jnp.int32, (hc, hc), 0)
    sj = lax.broadcasted_iota(jnp.int32, (hc, hc), 1)
    same_head = (si // c) == (sj // c)
    keep = jnp.logical_and(same_head, (sj >= si) if reverse else (sj <= si))
    mid = c // 2
    end = 0 if reverse else c - 1
    nchunks = ts // c
    head_cols = [slice(h * HG_DK, (h + 1) * HG_DK) for h in range(HG_HEADS)]

    def stack(a):
        return jnp.concatenate([a[:, cols] for cols in head_cols], axis=0)

    for cc in range(nchunks):
        ch = (nchunks - 1 - cc) if reverse else cc
        rows = pl.ds(ch * c, c)
        sig = _sigmoid(z_ref[rows, :])
        log_f = jnp.log(lb + (1.0 - lb) * sig)
        key = (1.0 - lb) * (1.0 - sig)
        qs = _silu(q_ref[rows, :]) * (HG_DK ** -0.5)
        f_hi = log_f.astype(BF16)
        rem = log_f - f_hi.astype(F32)
        f_mid = rem.astype(BF16)
        f_lo = (rem - f_mid.astype(F32)).astype(BF16)
        bcum = (jnp.dot(tri, f_lo, preferred_element_type=F32)
                + jnp.dot(tri, f_mid, preferred_element_type=F32)
                + jnp.dot(tri, f_hi, preferred_element_type=F32))
        ref_row = bcum[mid:mid + 1, :]
        b_end = bcum[end:end + 1, :]
        qe = (qs * jnp.exp(bcum - ref_row)).astype(BF16)
        ke = (key * jnp.exp(ref_row - bcum)).astype(BF16)
        qb = (qs * jnp.exp(bcum)).astype(BF16)
        kd = (key * jnp.exp(b_end - bcum)).astype(BF16)
        decay = jnp.exp(b_end)
        vb = v_ref[rows, :].astype(BF16)
        vs = stack(vb)
        att = lax.dot_general(stack(qe), stack(ke), _NT, preferred_element_type=F32)
        att = jnp.where(keep, att, 0.0).astype(BF16)
        intra = jnp.dot(att, vs, preferred_element_type=F32)
        for h, cols in enumerate(head_cols):
            st = s_ref[h]
            inter = lax.dot_general(qb[:, cols], st.astype(BF16), _NT, preferred_element_type=F32)
            acc_ref[rows, cols] = intra[h * c:(h + 1) * c, :] + inter
            s_ref[h] = st * decay[:, cols] + lax.dot_general(vb[:, cols], kd[:, cols], _TN,
                                                             preferred_element_type=F32)

    if reverse:
        o = acc_ref[...] + of_ref[...]
        ms = jnp.mean(o * o, axis=-1, keepdims=True)
        o_ref[...] = o * lax.rsqrt(ms + RMS_EPS) * nw_ref[...] * _silu(g_ref[...])


def _gla(proj, lower_bounds, norm_w, layer, batch, seq_len):
    t = proj.shape[0]
    d = HG_HEADS * HG_DK
    ts = min(256, seq_len)
    nt = seq_len // ts
    depth = lower_bounds.shape[0]
    state = pltpu.VMEM((HG_HEADS, HG_DV, HG_DK), F32)

    def call(reverse, extra):
        tidx = (lambda b, i: b * nt + (nt - 1 - i)) if reverse else (lambda b, i: b * nt + i)
        in_specs = [
            pl.BlockSpec((ts, d), lambda b, i: (tidx(b, i), 0)),
            pl.BlockSpec((ts, d), lambda b, i: (tidx(b, i), 1)),
            pl.BlockSpec((ts, d), lambda b, i: (tidx(b, i), 3 if reverse else 2)),
            pl.BlockSpec((depth, d), lambda b, i: (0, 0)),
        ]
        args = [proj, proj, proj, lower_bounds]
        scratch = [state]
        if reverse:
            in_specs += [pl.BlockSpec((ts, d), lambda b, i: (tidx(b, i), 0)),
                         pl.BlockSpec((ts, d), lambda b, i: (tidx(b, i), 4)),
                         pl.BlockSpec((1, d), lambda b, i: (0, 0))]
            args += [extra, proj, norm_w.reshape(1, d)]
            scratch.append(pltpu.VMEM((ts, d), F32))
        return pl.pallas_call(
            functools.partial(_gla_kernel, reverse=reverse, ts=ts, layer=layer),
            grid=(batch, nt),
            in_specs=in_specs,
            out_specs=pl.BlockSpec((ts, d), lambda b, i: (tidx(b, i), 0)),
            out_shape=jax.ShapeDtypeStruct((t, d), F32),
            scratch_shapes=scratch,
            compiler_params=_params(("arbitrary", "arbitrary")),
            name="gla_bwd" if reverse else "gla_fwd",
        )(*args)

    o_fwd = call(False, None)
    return call(True, o_fwd)


def _top2(p):
    v1 = jnp.maximum(jnp.maximum(p[0], p[1]), jnp.maximum(p[2], p[3]))
    i1 = jnp.where(p[0] == v1, 0, jnp.where(p[1] == v1, 1, jnp.where(p[2] == v1, 2, 3)))
    q = [jnp.where(i1 == k, -1.0, p[k]) for k in range(4)]
    v2 = jnp.maximum(jnp.maximum(q[0], q[1]), jnp.maximum(q[2], q[3]))
    i2 = jnp.where(q[0] == v2, 0, jnp.where(q[1] == v2, 1, jnp.where(q[2] == v2, 2, 3)))
    return v1, i1, v2, i2


def _outproj_kernel(*refs, n_mix, alpha, tm):
    mix_refs = refs[:n_mix]
    w_refs = refs[n_mix:2 * n_mix]
    h_ref, lnw_ref, lnb_ref, rw_ref, o_ref, info_ref, cnt_ref, carry_ref = refs[2 * n_mix:]
    i = pl.program_id(0)

    @pl.when(i == 0)
    def _():
        carry_ref[...] = jnp.zeros_like(carry_ref)

    y = jnp.dot(mix_refs[0][...].astype(BF16), w_refs[0][...], preferred_element_type=F32)
    for m in range(1, n_mix):
        y = y + jnp.dot(mix_refs[m][...].astype(BF16), w_refs[m][...], preferred_element_type=F32)
    h1 = _layer_norm(alpha * h_ref[...] + y, lnw_ref[...], lnb_ref[...])
    o_ref[...] = h1

    logits = lax.dot_general(rw_ref[...], h1, _NT, preferred_element_type=F32,
                             precision=lax.Precision.HIGHEST)
    ex = jnp.exp(logits - jnp.max(logits, axis=0, keepdims=True))
    probs = ex / jnp.sum(ex, axis=0, keepdims=True)
    best = None
    for g in range(N_GROUPS):
        rows = [probs[g * EXPERTS_PER_GROUP + k:g * EXPERTS_PER_GROUP + k + 1, :]
                for k in range(EXPERTS_PER_GROUP)]
        v1, i1, v2, i2 = _top2(rows)
        score = v1 + v2
        cand = (score, v1, i1 + g * EXPERTS_PER_GROUP, v2, i2 + g * EXPERTS_PER_GROUP)
        if best is None:
            best = cand
        else:
            take = cand[0] > best[0]
            best = tuple(jnp.where(take, cn, bs) for cn, bs in zip(cand, best))
    _, v1, e1, v2, e2 = best
    denom = v1 + v2
    g1 = v1 / denom
    g2 = v2 / denom

    eid = lax.broadcasted_iota(jnp.int32, (N_EXPERTS, tm), 0)
    oh1 = (eid == e1).astype(F32)
    oh2 = (eid == e2).astype(F32)
    oh = oh1 + oh2
    tt = lax.broadcasted_iota(jnp.int32, (tm, tm), 0)
    uu = lax.broadcasted_iota(jnp.int32, (tm, tm), 1)
    before = (tt < uu).astype(BF16)
    base = carry_ref[:, 0:1] + jnp.dot(oh.astype(BF16), before, preferred_element_type=F32)
    rank1 = jnp.sum(oh1 * base, axis=0, keepdims=True)
    rank2 = jnp.sum(oh2 * base, axis=0, keepdims=True)
    carry_ref[...] = carry_ref[...] + jnp.sum(oh, axis=1, keepdims=True)
    cnt_ref[...] = carry_ref[...]
    zero = jnp.zeros_like(g1)
    info_ref[...] = jnp.concatenate(
        [e1.astype(F32), e2.astype(F32), g1, g2, rank1, rank2, zero, zero], axis=0)


def _out_project(mixes, weights_bf16, h, ln_w, ln_b, router_wt, alpha):
    t, d = h.shape
    tm = 512
    n_mix = len(mixes)
    in_specs = ([pl.BlockSpec((tm, m.shape[1]), lambda i: (i, 0)) for m in mixes]
                + [pl.BlockSpec(w.shape, lambda i: (0, 0)) for w in weights_bf16]
                + [pl.BlockSpec((tm, d), lambda i: (i, 0)),
                   pl.BlockSpec((1, d), lambda i: (0, 0)),
                   pl.BlockSpec((1, d), lambda i: (0, 0)),
                   pl.BlockSpec((N_EXPERTS, d), lambda i: (0, 0))])
    return pl.pallas_call(
        functools.partial(_outproj_kernel, n_mix=n_mix, alpha=alpha, tm=tm),
        grid=(t // tm,),
        in_specs=in_specs,
        out_specs=[pl.BlockSpec((tm, d), lambda i: (i, 0)),
                   pl.BlockSpec((SUBLANES, tm), lambda i: (0, i)),
                   pl.BlockSpec((N_EXPERTS, LANES), lambda i: (0, 0))],
        out_shape=[jax.ShapeDtypeStruct((t, d), F32),
                   jax.ShapeDtypeStruct((SUBLANES, t), F32),
                   jax.ShapeDtypeStruct((N_EXPERTS, LANES), F32)],
        scratch_shapes=[pltpu.VMEM((N_EXPERTS, LANES), F32)],
        compiler_params=_params(("arbitrary",)),
        name="out_proj_router",
    )(*mixes, *weights_bf16, h, ln_w.reshape(1, d), ln_b.reshape(1, d), router_wt)


def _row_copy(src_ref, src_row, dst_ref, dst_row, sem):
    return pltpu.make_async_copy(src_ref.at[pl.ds(src_row, 1)], dst_ref.at[pl.ds(dst_row, 1)], sem)


def _dispatch_kernel(d1_ref, d2_ref, zs_ref, h_ref, xb_ref, zero_ref, sem, *, tm):
    i = pl.program_id(0)
    base = i * tm

    @pl.when(i == 0)
    def _():
        zero_ref[...] = jnp.zeros_like(zero_ref)
        def clear(row):
            start = pl.multiple_of(row, MOE_ROWS)
            return pltpu.make_async_copy(zero_ref, xb_ref.at[pl.ds(start, MOE_ROWS)], sem.at[0])

        for e in range(N_EXPERTS):
            clear(zs_ref[e]).start()
        for e in range(N_EXPERTS):
            clear(zs_ref[e]).wait()

        def clear_tail(b, carry):
            clear(b * MOE_ROWS).start()
            clear(b * MOE_ROWS).wait()
            return carry

        lax.fori_loop(zs_ref[N_EXPERTS], xb_ref.shape[0] // MOE_ROWS, clear_tail, 0)

    def start(t, carry):
        _row_copy(h_ref, t, xb_ref, d1_ref[base + t], sem.at[0]).start()
        _row_copy(h_ref, t, xb_ref, d2_ref[base + t], sem.at[1]).start(priority=1)
        return carry

    lax.fori_loop(0, tm, start, 0, unroll=8)
    pltpu.make_async_copy(h_ref, xb_ref.at[pl.ds(0, tm)], sem.at[0]).wait()
    pltpu.make_async_copy(h_ref, xb_ref.at[pl.ds(0, tm)], sem.at[1]).wait()


def _dispatch(h, dest1, dest2, zero_start, n_rows):
    t, d = h.shape
    tm = 512
    return pl.pallas_call(
        functools.partial(_dispatch_kernel, tm=tm),
        grid_spec=pltpu.PrefetchScalarGridSpec(
            num_scalar_prefetch=3,
            grid=(t // tm,),
            in_specs=[pl.BlockSpec((tm, d), lambda i, d1, d2, zs: (i, 0))],
            out_specs=pl.BlockSpec(memory_space=pl.ANY),
            scratch_shapes=[pltpu.VMEM((MOE_ROWS, d), F32), pltpu.SemaphoreType.DMA((2,))],
        ),
        out_shape=jax.ShapeDtypeStruct((n_rows, d), F32),
        compiler_params=_params(("arbitrary",)),
        name="moe_dispatch",
    )(dest1, dest2, zero_start, h)


def _expert_kernel(be_ref, nv_ref, x_ref, wg_ref, wu_ref, wd_ref, o_ref, wgb_ref, wub_ref, wdb_ref):
    i = pl.program_id(0)
    changed = jnp.logical_or(i == 0, be_ref[i] != be_ref[jnp.maximum(i - 1, 0)])

    @pl.when(changed)
    def _():
        wgb_ref[...] = wg_ref[...].astype(BF16)
        wub_ref[...] = wu_ref[...].astype(BF16)
        wdb_ref[...] = wd_ref[...].astype(BF16)

    @pl.when(i < nv_ref[0])
    def _():
        x = x_ref[...].astype(BF16)
        gate = jnp.dot(x, wgb_ref[...], preferred_element_type=F32)
        up = jnp.dot(x, wub_ref[...], preferred_element_type=F32)
        hid = (_silu(gate) * up).astype(BF16)
        o_ref[...] = jnp.dot(hid, wdb_ref[...], preferred_element_type=F32)

    @pl.when(i >= nv_ref[0])
    def _():
        o_ref[...] = jnp.zeros_like(o_ref)


def _experts(xb, block_e, n_valid, w_gate, w_up, w_down, layer):
    p, d = xb.shape
    de = w_gate.shape[3]
    nb = p // MOE_ROWS
    return pl.pallas_call(
        _expert_kernel,
        grid_spec=pltpu.PrefetchScalarGridSpec(
            num_scalar_prefetch=2,
            grid=(nb,),
            in_specs=[
                pl.BlockSpec((MOE_ROWS, d), lambda i, be, nv: (jnp.minimum(i, nv[0] - 1), 0)),
                pl.BlockSpec((None, None, d, de), lambda i, be, nv: (layer, be[i], 0, 0)),
                pl.BlockSpec((None, None, d, de), lambda i, be, nv: (layer, be[i], 0, 0)),
                pl.BlockSpec((None, None, de, d), lambda i, be, nv: (layer, be[i], 0, 0)),
            ],
            out_specs=pl.BlockSpec((MOE_ROWS, d), lambda i, be, nv: (i, 0)),
            scratch_shapes=[pltpu.VMEM((d, de), BF16), pltpu.VMEM((d, de), BF16),
                            pltpu.VMEM((de, d), BF16)],
        ),
        out_shape=jax.ShapeDtypeStruct((p, d), F32),
        compiler_params=_params(("arbitrary",)),
        name="moe_experts",
    )(block_e, n_valid, xb, w_gate, w_up, w_down)


def _combine_kernel(d1_ref, d2_ref, h_ref, gates_ref, lnw_ref, lnb_ref, yb_ref, o_ref, r_ref, sem,
                    *, tm, alpha):
    base = pl.program_id(0) * tm

    def start(t, carry):
        _row_copy(yb_ref, d1_ref[base + t], r_ref.at[0], t, sem.at[0]).start()
        _row_copy(yb_ref, d2_ref[base + t], r_ref.at[1], t, sem.at[1]).start(priority=1)
        return carry

    lax.fori_loop(0, tm, start, 0, unroll=8)
    pltpu.make_async_copy(yb_ref.at[pl.ds(0, tm)], r_ref.at[0], sem.at[0]).wait()
    pltpu.make_async_copy(yb_ref.at[pl.ds(0, tm)], r_ref.at[1], sem.at[1]).wait()
    y = gates_ref[:, 0:1] * r_ref[0] + gates_ref[:, 1:2] * r_ref[1]
    o_ref[...] = _layer_norm(alpha * h_ref[...] + y, lnw_ref[...], lnb_ref[...])


def _combine(h, yb, dest1, dest2, gates, ln_w, ln_b, alpha):
    t, d = h.shape
    tm = 512
    return pl.pallas_call(
        functools.partial(_combine_kernel, tm=tm, alpha=alpha),
        grid_spec=pltpu.PrefetchScalarGridSpec(
            num_scalar_prefetch=2,
            grid=(t // tm,),
            in_specs=[pl.BlockSpec((tm, d), lambda i, d1, d2: (i, 0)),
                      pl.BlockSpec((tm, 2), lambda i, d1, d2: (i, 0)),
                      pl.BlockSpec((1, d), lambda i, d1, d2: (0, 0)),
                      pl.BlockSpec((1, d), lambda i, d1, d2: (0, 0)),
                      pl.BlockSpec(memory_space=pl.ANY)],
            out_specs=pl.BlockSpec((tm, d), lambda i, d1, d2: (i, 0)),
            scratch_shapes=[pltpu.VMEM((2, tm, d), F32), pltpu.SemaphoreType.DMA((2,))],
        ),
        out_shape=jax.ShapeDtypeStruct((t, d), F32),
        compiler_params=_params(("arbitrary",)),
        name="moe_combine",
    )(dest1, dest2, h, gates, ln_w.reshape(1, d), ln_b.reshape(1, d), yb)


def _moe(h1, info, counts, w_gate, w_up, w_down, layer, ln_w, ln_b, alpha):
    t = h1.shape[0]
    n_rows = 2 * t + N_EXPERTS * MOE_ROWS
    nb = n_rows // MOE_ROWS
    cnt = counts[:, 0].astype(jnp.int32)
    padded = (cnt + MOE_ROWS - 1) // MOE_ROWS * MOE_ROWS
    ends = jnp.cumsum(padded)
    pstart = ends - padded
    e1 = info[0].astype(jnp.int32)
    e2 = info[1].astype(jnp.int32)
    expert_ids = jnp.arange(N_EXPERTS, dtype=jnp.int32)[:, None]
    dest1 = jnp.sum(jnp.where(e1[None, :] == expert_ids, pstart[:, None], 0), axis=0) + info[4].astype(jnp.int32)
    dest2 = jnp.sum(jnp.where(e2[None, :] == expert_ids, pstart[:, None], 0), axis=0) + info[5].astype(jnp.int32)
    gates = jnp.stack([info[2], info[3]], axis=1)
    n_valid = (ends[-1] // MOE_ROWS).astype(jnp.int32).reshape(1)
    blk = jnp.minimum(jnp.arange(nb, dtype=jnp.int32), n_valid[0] - 1) * MOE_ROWS
    block_e = jnp.minimum(jnp.sum((ends[None, :] <= blk[:, None]).astype(jnp.int32), axis=1), N_EXPERTS - 1)
    zero_start = jnp.concatenate([jnp.maximum(ends - MOE_ROWS, 0), n_valid]).astype(jnp.int32)

    xb = _dispatch(h1, dest1, dest2, zero_start, n_rows)
    yb = _experts(xb, block_e, n_valid, w_gate, w_up, w_down, layer)
    return _combine(h1, yb, dest1, dest2, gates, ln_w, ln_b, alpha)


def _rotary_column_order(w_in):
    d = w_in.shape[0]
    nq = RET_HEADS * RET_DK

    def perm(w):
        return w.reshape(d, RET_HEADS, RET_DK // 2, 2).transpose(0, 3, 1, 2).reshape(d, nq)

    return jnp.concatenate([perm(w_in[:, :nq]), perm(w_in[:, nq:2 * nq]), w_in[:, 2 * nq:]], axis=1)


def kernel(x, w_in_even, w_out_even, lru_conv_w, lru_conv_b, lru_gate_w, lru_gate_b, lru_lambda,
           w_in_odd, w_out_odd, hg_lower_bounds, hg_norm_w, ln_w, ln_b, router_w,
           moe_w_gate, moe_w_up, moe_w_down):
    batch, seq_len, d = x.shape
    depth = ln_w.shape[0]
    alpha = (2.0 * depth) ** 0.25
    t = batch * seq_len
    h = x.reshape(t, d)
    router_wt = router_w.T
    half = RET_DK // 2
    inv_freq = ROPE_BASE ** (-jnp.arange(0, RET_DK, 2, dtype=F32) / RET_DK)
    inv_freq = jnp.tile(inv_freq, LANES // half).reshape(1, LANES)

    for layer in range(depth):
        j = layer // 2
        if layer % 2 == 0:
            w_in = _rotary_column_order(w_in_even[j]).astype(BF16)
            proj = _project(h, w_in, inv_freq, rotary=True, seq_len=seq_len)
            ret = _retention(proj, batch, seq_len)
            lru = _rglru(proj, lru_conv_w[j], lru_conv_b[j], lru_gate_w[j], lru_gate_b[j],
                         lru_lambda[j], batch, seq_len)
            w_out = w_out_even[j].astype(BF16)
            nret = RET_HEADS * RET_DV
            mixes, weights = [ret, lru], [w_out[:nret], w_out[nret:]]
        else:
            proj = _project(h, w_in_odd[j].astype(BF16), inv_freq, rotary=False, seq_len=seq_len)
            mix = _gla(proj, hg_lower_bounds, hg_norm_w[j], layer, batch, seq_len)
            mixes, weights = [mix], [w_out_odd[j].astype(BF16)]
        h1, info, counts = _out_project(mixes, weights, h, ln_w[layer, 0], ln_b[layer, 0],
                                        router_wt, alpha)
        h = _moe(h1, info, counts, moe_w_gate, moe_w_up, moe_w_down, layer,
                 ln_w[layer, 1], ln_b[layer, 1], alpha)
    return h.reshape(batch, seq_len, d)
```

```python
import functools
import math

import jax
import jax.numpy as jnp
from jax import lax
from jax.experimental import pallas as pl
from jax.experimental.pallas import tpu as pltpu

F32 = jnp.float32
BF16 = jnp.bfloat16

RET_HEADS = 4
RET_DK = 64
RET_DV = 128
RET_CHUNK = 128
ROPE_BASE = 10000.0
LRU_BLOCKS = 4
LRU_BW = 128
LRU_CONV = 4
LRU_C = 8.0
HG_HEADS = 8
HG_DK = 128
HG_DV = 128
N_EXPERTS = 16
N_GROUPS = 4
EXPERTS_PER_GROUP = 4
LN_EPS = 1e-5
RMS_EPS = 1e-6

LANES = 128
SUBLANES = 8
PROJ_ROWS = 512
PROJ_COLS = 512
RET_CHUNKS_PER_STEP = 4
LRU_ROWS = 512
GLA_CHUNK = 32
GLA_ROWS = 256
TOKEN_ROWS = 512
MOE_ROWS = 256
VMEM_LIMIT = 48 * 1024 * 1024

_NT = (((1,), (1,)), ((), ()))
_TN = (((0,), (0,)), ((), ()))


def _params(sem):
    return pltpu.CompilerParams(dimension_semantics=sem, vmem_limit_bytes=VMEM_LIMIT)


def _sigmoid(x):
    return 0.5 * jnp.tanh(0.5 * x) + 0.5


def _silu(x):
    return x * _sigmoid(x)


def _softplus(x):
    return jnp.maximum(x, 0.0) + jnp.log1p(jnp.exp(-jnp.abs(x)))


def _layer_norm(u, w, b):
    mu = jnp.mean(u, axis=-1, keepdims=True)
    d = u - mu
    var = jnp.mean(d * d, axis=-1, keepdims=True)
    return d * lax.rsqrt(var + LN_EPS) * w + b


def _split3(x):
    hi = x.astype(BF16)
    rem = x - hi.astype(F32)
    mid = rem.astype(BF16)
    lo = (rem - mid.astype(F32)).astype(BF16)
    return hi, mid, lo


def _proj_kernel(x_ref, w_ref, inv_ref, ob_ref, of_ref, *, rotary, seq_len, tm, n_bf):
    i = pl.program_id(0)
    xb = x_ref[...].astype(BF16)
    tn = PROJ_COLS
    for j in range(w_ref.shape[1] // tn):
        acc = jnp.dot(xb, w_ref[:, j * tn:(j + 1) * tn], preferred_element_type=F32)
        if rotary and j == 0:
            row = lax.broadcasted_iota(jnp.int32, (tm, 1), 0) + i * tm
            ang = (row % seq_len).astype(F32) * inv_ref[...]
            cos = jnp.cos(ang)
            sin = jnp.sin(ang)
            scale = RET_DK ** -0.5
            q1, q2 = acc[:, 0:128], acc[:, 128:256]
            k1, k2 = acc[:, 256:384], acc[:, 384:512]
            ob_ref[:, 0:128] = ((q1 * cos - q2 * sin) * scale).astype(BF16)
            ob_ref[:, 128:256] = ((q1 * sin + q2 * cos) * scale).astype(BF16)
            ob_ref[:, 256:384] = (k1 * cos - k2 * sin).astype(BF16)
            ob_ref[:, 384:512] = (k1 * sin + k2 * cos).astype(BF16)
        elif (j + 1) * tn <= n_bf:
            ob_ref[:, j * tn:(j + 1) * tn] = acc.astype(BF16)
        else:
            of_ref[:, j * tn - n_bf:(j + 1) * tn - n_bf] = acc


def _project(x, w_bf16, inv_freq, n_bf, *, rotary, seq_len):
    t, k = x.shape
    n = w_bf16.shape[1]
    tm = min(PROJ_ROWS, seq_len)
    return pl.pallas_call(
        functools.partial(_proj_kernel, rotary=rotary, seq_len=seq_len, tm=tm, n_bf=n_bf),
        grid=(t // tm,),
        in_specs=[
            pl.BlockSpec((tm, k), lambda i: (i, 0)),
            pl.BlockSpec((k, n), lambda i: (0, 0)),
            pl.BlockSpec((1, LANES), lambda i: (0, 0)),
        ],
        out_specs=[pl.BlockSpec((tm, n_bf), lambda i: (i, 0)),
                   pl.BlockSpec((tm, n - n_bf), lambda i: (i, 0))],
        out_shape=[jax.ShapeDtypeStruct((t, n_bf), BF16),
                   jax.ShapeDtypeStruct((t, n - n_bf), F32)],
        compiler_params=_params(("arbitrary",)),
        name="in_proj",
    )(x, w_bf16, inv_freq)


def _ret_log_gamma(head):
    out = jnp.full(head.shape, math.log1p(-(2.0 ** -5.0)), F32)
    for h in range(1, RET_HEADS):
        out = jnp.where(head == h, math.log1p(-(2.0 ** (-5.0 - h))), out)
    return out


def _ret_lane_head():
    lane = lax.broadcasted_iota(jnp.int32, (1, 2 * LANES), 1)
    return (lane % LANES) // (RET_DK // 2)


def _ret_state_mask():
    shape = (RET_HEADS * RET_DV, 2 * LANES)
    row_head = lax.broadcasted_iota(jnp.int32, shape, 0) // RET_DV
    col_head = (lax.broadcasted_iota(jnp.int32, shape, 1) % LANES) // (RET_DK // 2)
    return row_head == col_head


def _ret_bstate_kernel(k_ref, v_ref, sb_ref, s_ref, *, cps):
    c = RET_CHUNK

    @pl.when(pl.program_id(1) == 0)
    def _():
        s_ref[...] = jnp.zeros_like(s_ref)

    lg = _ret_log_gamma(_ret_lane_head())
    idx = lax.broadcasted_iota(jnp.int32, (c, 1), 0).astype(F32)
    k_decay = jnp.exp(lg * idx)
    chunk_decay = jnp.exp(lg * float(c))
    mask = _ret_state_mask()
    for cc in reversed(range(cps)):
        rows = slice(cc * c, (cc + 1) * c)
        sb_ref[0, cc] = s_ref[...].astype(BF16)
        kb = (k_ref[rows, :] * k_decay).astype(BF16)
        upd = lax.dot_general(v_ref[rows, :], kb, _TN, preferred_element_type=F32)
        s_ref[...] = s_ref[...] * chunk_decay + jnp.where(mask, upd, 0.0)


def _ret_out_kernel(q_ref, k_ref, v_ref, g_ref, sb_ref, o_ref, s_ref, *, cps):
    c = RET_CHUNK

    @pl.when(pl.program_id(1) == 0)
    def _():
        s_ref[...] = jnp.zeros_like(s_ref)

    lane_head = _ret_lane_head()
    lg = _ret_log_gamma(lane_head)
    idx = lax.broadcasted_iota(jnp.int32, (c, 1), 0).astype(F32)
    q_decay_f = jnp.exp(lg * (idx + 1.0))
    q_decay_b = jnp.exp(lg * (float(c) - idx))
    k_decay = jnp.exp(lg * (float(c) - 1.0 - idx))
    chunk_decay = jnp.exp(lg * float(c))
    mask = _ret_state_mask()
    ii = lax.broadcasted_iota(jnp.int32, (c, c), 0)
    jj = lax.broadcasted_iota(jnp.int32, (c, c), 1)
    dist = jnp.abs(ii - jj).astype(F32)
    intra_decay = [jnp.exp(math.log1p(-(2.0 ** (-5.0 - h))) * dist) for h in range(RET_HEADS)]

    for cc in range(cps):
        rows = slice(cc * c, (cc + 1) * c)
        q = q_ref[rows, :]
        k = k_ref[rows, :]
        v = v_ref[rows, :]
        qf = (q * q_decay_f).astype(BF16)
        qb = (q * q_decay_b).astype(BF16)
        cross = (lax.dot_general(qf, s_ref[...].astype(BF16), _NT, preferred_element_type=F32)
                 + lax.dot_general(qb, sb_ref[0, cc], _NT, preferred_element_type=F32))
        for h in range(RET_HEADS):
            qh = jnp.where(lane_head == h, q, jnp.zeros_like(q))
            s = lax.dot_general(qh, k, _NT, preferred_element_type=F32) * intra_decay[h]
            cols = slice(h * RET_DV, (h + 1) * RET_DV)
            o = jnp.dot(s.astype(BF16), v[:, cols], preferred_element_type=F32) + cross[:, cols]
            mu = jnp.mean(o, axis=-1, keepdims=True)
            d = o - mu
            var = jnp.mean(d * d, axis=-1, keepdims=True)
            gate = _silu(g_ref[rows, cols].astype(F32))
            o_ref[rows, cols] = (gate * (d * lax.rsqrt(var + LN_EPS))).astype(BF16)
        kf = (k * k_decay).astype(BF16)
        upd = lax.dot_general(v, kf, _TN, preferred_element_type=F32)
        s_ref[...] = s_ref[...] * chunk_decay + jnp.where(mask, upd, 0.0)


def _retention(proj, batch, seq_len):
    t = proj.shape[0]
    c = RET_CHUNK
    cps = min(RET_CHUNKS_PER_STEP, seq_len // c)
    rows = cps * c
    ns = seq_len // rows
    dv = RET_HEADS * RET_DV
    state_shape = (dv, 2 * LANES)
    rev = lambda b, n: b * ns + (ns - 1 - n)
    fwd = lambda b, n: b * ns + n
    sb = pl.pallas_call(
        functools.partial(_ret_bstate_kernel, cps=cps),
        grid=(batch, ns),
        in_specs=[
            pl.BlockSpec((rows, 2 * LANES), lambda b, n: (rev(b, n), 1)),
            pl.BlockSpec((rows, dv), lambda b, n: (rev(b, n), 1)),
        ],
        out_specs=pl.BlockSpec((1, cps) + state_shape, lambda b, n: (b, ns - 1 - n, 0, 0)),
        out_shape=jax.ShapeDtypeStruct((batch, ns * cps) + state_shape, BF16),
        scratch_shapes=[pltpu.VMEM(state_shape, F32)],
        compiler_params=_params(("arbitrary", "arbitrary")),
        name="ret_bstate",
    )(proj, proj)
    return pl.pallas_call(
        functools.partial(_ret_out_kernel, cps=cps),
        grid=(batch, ns),
        in_specs=[
            pl.BlockSpec((rows, 2 * LANES), lambda b, n: (fwd(b, n), 0)),
            pl.BlockSpec((rows, 2 * LANES), lambda b, n: (fwd(b, n), 1)),
            pl.BlockSpec((rows, dv), lambda b, n: (fwd(b, n), 1)),
            pl.BlockSpec((rows, dv), lambda b, n: (fwd(b, n), 2)),
            pl.BlockSpec((1, cps) + state_shape, lambda b, n: (b, n, 0, 0)),
        ],
        out_specs=pl.BlockSpec((rows, dv), lambda b, n: (fwd(b, n), 0)),
        out_shape=jax.ShapeDtypeStruct((t, dv), BF16),
        scratch_shapes=[pltpu.VMEM(state_shape, F32)],
        compiler_params=_params(("arbitrary", "arbitrary")),
        name="ret_out",
    )(proj, proj, proj, proj, sb)


def _lru_kernel(xfp_ref, xf_ref, xfn_ref, xbp_ref, xb_ref, xbn_ref, cw_ref, cb_ref, gw_ref, gb_ref,
                lam_ref, hf_ref, hb_ref, xx_ref, a_ref, b_ref, h_ref, *, nt, ts, batch):
    i = pl.program_id(0)
    halo = SUBLANES
    lo = LRU_CONV // 2

    @pl.when(i == 0)
    def _():
        h_ref[...] = jnp.zeros_like(h_ref)

    def prepare(xp_ref, x_ref, xn_ref, tile, z, slot):
        for b in range(batch):
            xx_ref[0:halo, :] = jnp.where(tile == 0, 0.0, xp_ref[b])
            xx_ref[halo:halo + ts, :] = x_ref[b]
            xx_ref[halo + ts:2 * halo + ts, :] = jnp.where(tile == nt - 1, 0.0, xn_ref[b])
            xc = cb_ref[...]
            for tap in range(LRU_CONV):
                xc = xc + cw_ref[tap:tap + 1, :] * xx_ref[pl.ds(halo - lo + tap, ts), :]
            for n in range(LRU_BLOCKS):
                cols = slice(n * LRU_BW, (n + 1) * LRU_BW)
                xn = xc[:, cols]
                g = (jnp.dot(xn.astype(BF16), gw_ref[z, n], preferred_element_type=F32)
                     + gb_ref[z, n:n + 1, :])
                r = _sigmoid(g[:, :LRU_BW])
                ig = _sigmoid(g[:, LRU_BW:])
                a = jnp.exp((-LRU_C) * r * _softplus(-lam_ref[z, :, cols]))
                a_ref[slot + b, :, cols] = a
                b_ref[slot + b, :, cols] = jnp.sqrt(1.0 - a * a) * (ig * xn)

    prepare(xfp_ref, xf_ref, xfn_ref, i, 0, 0)
    prepare(xbp_ref, xb_ref, xbn_ref, nt - 1 - i, 1, batch)

    def step(s, hs):
        out = []
        for k in range(2 * batch):
            row = s if k < batch else ts - 1 - s
            h = a_ref[k, pl.ds(row, 1), :] * hs[k] + b_ref[k, pl.ds(row, 1), :]
            if k < batch:
                hf_ref[k, pl.ds(row, 1), :] = h
            else:
                hb_ref[k - batch, pl.ds(row, 1), :] = h
            out.append(h)
        return tuple(out)

    hs = lax.fori_loop(0, ts, step, tuple(h_ref[k] for k in range(2 * batch)), unroll=8)
    for k in range(2 * batch):
        h_ref[k] = hs[k]


def _rglru(proj_f32, conv_w, conv_b, gate_w, gate_b, lam, batch, seq_len):
    w = LRU_BLOCKS * LRU_BW
    ts = min(LRU_ROWS, seq_len)
    nt = seq_len // ts
    rows8 = ts // SUBLANES
    last8 = seq_len // SUBLANES - 1
    x3 = proj_f32.reshape(batch, seq_len, proj_f32.shape[1])
    gw = jnp.concatenate([gate_w[:, 0], gate_w[:, 1]], axis=-1).astype(BF16)
    gb = jnp.concatenate([gate_b[:, 0], gate_b[:, 1]], axis=-1)
    bwd = lambda i: nt - 1 - i

    def tile_specs(tile):
        return [
            pl.BlockSpec((batch, SUBLANES, w), lambda i: (0, jnp.maximum(tile(i) * rows8 - 1, 0), 0)),
            pl.BlockSpec((batch, ts, w), lambda i: (0, tile(i), 0)),
            pl.BlockSpec((batch, SUBLANES, w), lambda i: (0, jnp.minimum((tile(i) + 1) * rows8, last8), 0)),
        ]

    full = lambda a: pl.BlockSpec(a.shape, lambda i: (0,) * a.ndim)
    cb = conv_b.reshape(1, w)
    lam3 = lam.reshape(2, 1, w)
    state = jax.ShapeDtypeStruct((batch, seq_len, w), F32)
    h_fwd, h_bwd = pl.pallas_call(
        functools.partial(_lru_kernel, nt=nt, ts=ts, batch=batch),
        grid=(nt,),
        in_specs=tile_specs(lambda i: i) + tile_specs(bwd) + [full(conv_w), full(cb), full(gw), full(gb),
                                                               full(lam3)],
        out_specs=[pl.BlockSpec((batch, ts, w), lambda i: (0, i, 0)),
                   pl.BlockSpec((batch, ts, w), lambda i: (0, bwd(i), 0))],
        out_shape=[state, state],
        scratch_shapes=[pltpu.VMEM((ts + 2 * SUBLANES, w), F32), pltpu.VMEM((2 * batch, ts, w), F32),
                        pltpu.VMEM((2 * batch, ts, w), F32), pltpu.VMEM((2 * batch, 1, w), F32)],
        compiler_params=_params(("arbitrary",)),
        name="lru_scan",
    )(x3, x3, x3, x3, x3, x3, conv_w, cb, gw, gb, lam3)
    return h_fwd.reshape(batch * seq_len, w), h_bwd.reshape(batch * seq_len, w)


def _gla_kernel(qf_ref, vf_ref, zf_ref, qb_ref, vb_ref, zb_ref, lbp_ref, of_ref, ob_ref, s_ref,
                *, ts, layer, batch):
    c = GLA_CHUNK
    nchunks = ts // c

    @pl.when(pl.program_id(0) == 0)
    def _():
        s_ref[...] = jnp.zeros_like(s_ref)

    p = lbp_ref[...]
    e = jnp.exp(p - jnp.max(p, axis=0, keepdims=True))
    sm = e / jnp.sum(e, axis=0, keepdims=True)
    lb = jnp.zeros((1, p.shape[1]), F32)
    for r in range(1, layer + 1):
        lb = lb + sm[r:r + 1, :]

    head_cols = [slice(h * HG_DK, (h + 1) * HG_DK) for h in range(HG_HEADS)]
    hc = HG_HEADS * c
    ii = lax.broadcasted_iota(jnp.int32, (c, c), 0)
    jj = lax.broadcasted_iota(jnp.int32, (c, c), 1)
    si = lax.broadcasted_iota(jnp.int32, (hc, hc), 0)
    sj = lax.broadcasted_iota(jnp.int32, (hc, hc), 1)
    same_head = si // c == sj // c
    mid = c // 2

    def stack(a):
        return jnp.concatenate([a[:, cols] for cols in head_cols], axis=0)

    def chunk(q_ref, v_ref, z_ref, o_ref, b, ch, reverse):
        slot = b + (batch if reverse else 0)
        rows = pl.ds(ch * c, c)
        tri = ((jj >= ii) if reverse else (jj <= ii)).astype(BF16)
        keep = jnp.logical_and(same_head, (sj >= si) if reverse else (sj <= si))
        end = 0 if reverse else c - 1
        sig = _sigmoid(z_ref[b, rows, :])
        log_f = jnp.log(lb + (1.0 - lb) * sig)
        key = (1.0 - lb) * (1.0 - sig)
        qs = _silu(q_ref[b, rows, :].astype(F32)) * (HG_DK ** -0.5)
        f_hi, f_mid, f_lo = _split3(log_f)
        bcum = (jnp.dot(tri, f_lo, preferred_element_type=F32)
                + jnp.dot(tri, f_mid, preferred_element_type=F32)
                + jnp.dot(tri, f_hi, preferred_element_type=F32))
        ref_row = bcum[mid:mid + 1, :]
        b_end = bcum[end:end + 1, :]
        qe = (qs * jnp.exp(bcum - ref_row)).astype(BF16)
        ke = (key * jnp.exp(ref_row - bcum)).astype(BF16)
        qd = (qs * jnp.exp(bcum)).astype(BF16)
        kd = (key * jnp.exp(b_end - bcum)).astype(BF16)
        decay = jnp.exp(b_end)
        v = v_ref[b, rows, :]
        att = lax.dot_general(stack(qe), stack(ke), _NT, preferred_element_type=F32)
        att = jnp.where(keep, att, 0.0).astype(BF16)
        intra = jnp.dot(att, stack(v), preferred_element_type=F32)
        for h, cols in enumerate(head_cols):
            st = s_ref[slot, h]
            inter = lax.dot_general(qd[:, cols], st.astype(BF16), _NT, preferred_element_type=F32)
            o_ref[b, rows, cols] = intra[h * c:(h + 1) * c, :] + inter
            s_ref[slot, h] = st * decay[:, cols] + lax.dot_general(v[:, cols], kd[:, cols], _TN,
                                                                  preferred_element_type=F32)

    for cc in range(nchunks):
        for b in range(batch):
            chunk(qf_ref, vf_ref, zf_ref, of_ref, b, cc, False)
        for b in range(batch):
            chunk(qb_ref, vb_ref, zb_ref, ob_ref, b, nchunks - 1 - cc, True)


def _gla(proj_bf16, proj_f32, lower_bounds, layer, batch, seq_len):
    d = HG_HEADS * HG_DK
    ts = min(GLA_ROWS, seq_len)
    nt = seq_len // ts
    depth = lower_bounds.shape[0]
    pb = proj_bf16.reshape(batch, seq_len, proj_bf16.shape[1])
    pf = proj_f32.reshape(batch, seq_len, proj_f32.shape[1])
    bwd = lambda i: nt - 1 - i
    blk = lambda tile, col: pl.BlockSpec((batch, ts, d), lambda i: (0, tile(i), col))
    fwd = lambda i: i
    out = jax.ShapeDtypeStruct((batch, seq_len, d), F32)
    o_fwd, o_bwd = pl.pallas_call(
        functools.partial(_gla_kernel, ts=ts, layer=layer, batch=batch),
        grid=(nt,),
        in_specs=[blk(fwd, 0), blk(fwd, 1), blk(fwd, 0), blk(bwd, 0), blk(bwd, 1), blk(bwd, 1),
                  pl.BlockSpec((depth, d), lambda i: (0, 0))],
        out_specs=[blk(fwd, 0), blk(bwd, 0)],
        out_shape=[out, out],
        scratch_shapes=[pltpu.VMEM((2 * batch, HG_HEADS, HG_DV, HG_DK), F32)],
        compiler_params=_params(("arbitrary",)),
        name="gla_scan",
    )(pb, pb, pf, pb, pb, pf, lower_bounds)
    return o_fwd.reshape(batch * seq_len, d), o_bwd.reshape(batch * seq_len, d)


def _top2(p):
    v1 = jnp.maximum(jnp.maximum(p[0], p[1]), jnp.maximum(p[2], p[3]))
    i1 = jnp.where(p[0] == v1, 0, jnp.where(p[1] == v1, 1, jnp.where(p[2] == v1, 2, 3)))
    q = [jnp.where(i1 == k, -1.0, p[k]) for k in range(4)]
    v2 = jnp.maximum(jnp.maximum(q[0], q[1]), jnp.maximum(q[2], q[3]))
    i2 = jnp.where(q[0] == v2, 0, jnp.where(q[1] == v2, 1, jnp.where(q[2] == v2, 2, 3)))
    return v1, i1, v2, i2


def _outproj_kernel(*refs, even, alpha, tm):
    if even:
        ret_ref, hf_ref, hb_ref, gr_ref, w0_ref, w1_ref = refs[:6]
        rest = refs[6:]
        lru = ((hf_ref[...] + hb_ref[...]) * jax.nn.gelu(gr_ref[...])).astype(BF16)
        y = (jnp.dot(ret_ref[...], w0_ref[...], preferred_element_type=F32)
             + jnp.dot(lru, w1_ref[...], preferred_element_type=F32))
    else:
        of_ref, ob_ref, g_ref, nw_ref, w0_ref = refs[:5]
        rest = refs[5:]
        o = of_ref[...] + ob_ref[...]
        ms = jnp.mean(o * o, axis=-1, keepdims=True)
        mix = o * lax.rsqrt(ms + RMS_EPS) * nw_ref[...] * _silu(g_ref[...].astype(F32))
        y = jnp.dot(mix.astype(BF16), w0_ref[...], preferred_element_type=F32)
    h_ref, lnw_ref, lnb_ref, rw_ref, o_ref, info_ref, cnt_ref, carry_ref = rest
    i = pl.program_id(0)

    @pl.when(i == 0)
    def _():
        carry_ref[...] = jnp.zeros_like(carry_ref)

    h1 = _layer_norm(alpha * h_ref[...] + y, lnw_ref[...], lnb_ref[...])
    o_ref[...] = h1

    h_hi = h1.astype(BF16)
    h_lo = (h1 - h_hi.astype(F32)).astype(BF16)
    rw = rw_ref[...]
    r_hi = rw.astype(BF16)
    r_lo = (rw - r_hi.astype(F32)).astype(BF16)
    logits = (lax.dot_general(r_lo, h_hi, _NT, preferred_element_type=F32)
              + lax.dot_general(r_hi, h_lo, _NT, preferred_element_type=F32)
              + lax.dot_general(r_hi, h_hi, _NT, preferred_element_type=F32))
    ex = jnp.exp(logits - jnp.max(logits, axis=0, keepdims=True))
    probs = ex / jnp.sum(ex, axis=0, keepdims=True)
    best = None
    for g in range(N_GROUPS):
        rows = [probs[g * EXPERTS_PER_GROUP + k:g * EXPERTS_PER_GROUP + k + 1, :]
                for k in range(EXPERTS_PER_GROUP)]
        v1, i1, v2, i2 = _top2(rows)
        cand = (v1 + v2, v1, i1 + g * EXPERTS_PER_GROUP, v2, i2 + g * EXPERTS_PER_GROUP)
        if best is None:
            best = cand
        else:
            take = cand[0] > best[0]
            best = tuple(jnp.where(take, cn, bs) for cn, bs in zip(cand, best))
    _, v1, e1, v2, e2 = best
    denom = v1 + v2
    g1 = v1 / denom
    g2 = v2 / denom

    eid = lax.broadcasted_iota(jnp.int32, (N_EXPERTS, tm), 0)
    oh1 = (eid == e1).astype(F32)
    oh2 = (eid == e2).astype(F32)
    oh = oh1 + oh2
    tt = lax.broadcasted_iota(jnp.int32, (tm, tm), 0)
    uu = lax.broadcasted_iota(jnp.int32, (tm, tm), 1)
    before = (tt < uu).astype(BF16)
    base = carry_ref[:, 0:1] + jnp.dot(oh.astype(BF16), before, preferred_element_type=F32)
    rank1 = jnp.sum(oh1 * base, axis=0, keepdims=True)
    rank2 = jnp.sum(oh2 * base, axis=0, keepdims=True)
    carry_ref[...] = carry_ref[...] + jnp.sum(oh, axis=1, keepdims=True)
    cnt_ref[...] = carry_ref[...]
    zero = jnp.zeros_like(g1)
    info_ref[...] = jnp.concatenate(
        [e1.astype(F32), e2.astype(F32), g1, g2, rank1, rank2, zero, zero], axis=0)


def _out_project(even, mixer_inputs, weights_bf16, h, ln_w, ln_b, router_wt, alpha):
    t, d = h.shape
    tm = TOKEN_ROWS
    row = lambda i: (i, 0)
    const = lambda i: (0, 0)
    if even:
        ret, h_fwd, h_bwd, proj_f32 = mixer_inputs
        w = h_fwd.shape[1]
        in_specs = [pl.BlockSpec((tm, ret.shape[1]), row), pl.BlockSpec((tm, w), row),
                    pl.BlockSpec((tm, w), row), pl.BlockSpec((tm, w), lambda i: (i, 1))]
        args = [ret, h_fwd, h_bwd, proj_f32]
    else:
        o_fwd, o_bwd, proj_bf16, norm_w = mixer_inputs
        in_specs = [pl.BlockSpec((tm, d), row), pl.BlockSpec((tm, d), row),
                    pl.BlockSpec((tm, d), lambda i: (i, 2)), pl.BlockSpec((1, d), const)]
        args = [o_fwd, o_bwd, proj_bf16, norm_w.reshape(1, d)]
    in_specs += [pl.BlockSpec(wm.shape, const) for wm in weights_bf16]
    in_specs += [pl.BlockSpec((tm, d), row), pl.BlockSpec((1, d), const), pl.BlockSpec((1, d), const),
                 pl.BlockSpec((N_EXPERTS, d), const)]
    args += list(weights_bf16) + [h, ln_w.reshape(1, d), ln_b.reshape(1, d), router_wt]
    return pl.pallas_call(
        functools.partial(_outproj_kernel, even=even, alpha=alpha, tm=tm),
        grid=(t // tm,),
        in_specs=in_specs,
        out_specs=[pl.BlockSpec((tm, d), row),
                   pl.BlockSpec((SUBLANES, tm), lambda i: (0, i)),
                   pl.BlockSpec((N_EXPERTS, LANES), const)],
        out_shape=[jax.ShapeDtypeStruct((t, d), F32),
                   jax.ShapeDtypeStruct((SUBLANES, t), F32),
                   jax.ShapeDtypeStruct((N_EXPERTS, LANES), F32)],
        scratch_shapes=[pltpu.VMEM((N_EXPERTS, LANES), F32)],
        compiler_params=_params(("arbitrary",)),
        name="out_proj_router",
    )(*args)


def _row_copy(src_ref, src_row, dst_ref, dst_row, sem):
    return pltpu.make_async_copy(src_ref.at[pl.ds(src_row, 1)], dst_ref.at[pl.ds(dst_row, 1)], sem)


def _dispatch_kernel(d1_ref, d2_ref, zs_ref, h_ref, xb_ref, zero_ref, sem, *, tm):
    i = pl.program_id(0)
    base = i * tm

    @pl.when(i == 0)
    def _():
        zero_ref[...] = jnp.zeros_like(zero_ref)

        def clear(row):
            start = pl.multiple_of(row, MOE_ROWS)
            return pltpu.make_async_copy(zero_ref, xb_ref.at[pl.ds(start, MOE_ROWS)], sem.at[0])

        for e in range(N_EXPERTS):
            clear(zs_ref[e]).start()
        for e in range(N_EXPERTS):
            clear(zs_ref[e]).wait()

        def clear_tail(b, carry):
            clear(b * MOE_ROWS).start()
            clear(b * MOE_ROWS).wait()
            return carry

        lax.fori_loop(zs_ref[N_EXPERTS], xb_ref.shape[0] // MOE_ROWS, clear_tail, 0)

    def start(t, carry):
        _row_copy(h_ref, t, xb_ref, d1_ref[base + t], sem.at[0]).start()
        _row_copy(h_ref, t, xb_ref, d2_ref[base + t], sem.at[1]).start(priority=1)
        return carry

    lax.fori_loop(0, tm, start, 0, unroll=8)
    pltpu.make_async_copy(h_ref, xb_ref.at[pl.ds(0, tm)], sem.at[0]).wait()
    pltpu.make_async_copy(h_ref, xb_ref.at[pl.ds(0, tm)], sem.at[1]).wait()


def _dispatch(h, dest1, dest2, zero_start, n_rows):
    t, d = h.shape
    tm = TOKEN_ROWS
    return pl.pallas_call(
        functools.partial(_dispatch_kernel, tm=tm),
        grid_spec=pltpu.PrefetchScalarGridSpec(
            num_scalar_prefetch=3,
            grid=(t // tm,),
            in_specs=[pl.BlockSpec((tm, d), lambda i, d1, d2, zs: (i, 0))],
            out_specs=pl.BlockSpec(memory_space=pl.ANY),
            scratch_shapes=[pltpu.VMEM((MOE_ROWS, d), F32), pltpu.SemaphoreType.DMA((2,))],
        ),
        out_shape=jax.ShapeDtypeStruct((n_rows, d), F32),
        compiler_params=_params(("arbitrary",)),
        name="moe_dispatch",
    )(dest1, dest2, zero_start, h)


def _expert_kernel(be_ref, nv_ref, x_ref, wg_ref, wu_ref, wd_ref, o_ref, wgb_ref, wub_ref, wdb_ref):
    i = pl.program_id(0)
    changed = jnp.logical_or(i == 0, be_ref[i] != be_ref[jnp.maximum(i - 1, 0)])

    @pl.when(changed)
    def _():
        wgb_ref[...] = wg_ref[...].astype(BF16)
        wub_ref[...] = wu_ref[...].astype(BF16)
        wdb_ref[...] = wd_ref[...].astype(BF16)

    @pl.when(i < nv_ref[0])
    def _():
        x = x_ref[...].astype(BF16)
        gate = jnp.dot(x, wgb_ref[...], preferred_element_type=F32)
        up = jnp.dot(x, wub_ref[...], preferred_element_type=F32)
        hid = (_silu(gate) * up).astype(BF16)
        o_ref[...] = jnp.dot(hid, wdb_ref[...], preferred_element_type=F32)

    @pl.when(i >= nv_ref[0])
    def _():
        o_ref[...] = jnp.zeros_like(o_ref)


def _experts(xb, block_e, n_valid, w_gate, w_up, w_down, layer):
    p, d = xb.shape
    de = w_gate.shape[3]
    nb = p // MOE_ROWS
    return pl.pallas_call(
        _expert_kernel,
        grid_spec=pltpu.PrefetchScalarGridSpec(
            num_scalar_prefetch=2,
            grid=(nb,),
            in_specs=[
                pl.BlockSpec((MOE_ROWS, d), lambda i, be, nv: (jnp.maximum(jnp.minimum(i, nv[0] - 1), 0), 0)),
                pl.BlockSpec((None, None, d, de), lambda i, be, nv: (layer, be[i], 0, 0)),
                pl.BlockSpec((None, None, d, de), lambda i, be, nv: (layer, be[i], 0, 0)),
                pl.BlockSpec((None, None, de, d), lambda i, be, nv: (layer, be[i], 0, 0)),
            ],
            out_specs=pl.BlockSpec((MOE_ROWS, d), lambda i, be, nv: (i, 0)),
            scratch_shapes=[pltpu.VMEM((d, de), BF16), pltpu.VMEM((d, de), BF16),
                            pltpu.VMEM((de, d), BF16)],
        ),
        out_shape=jax.ShapeDtypeStruct((p, d), F32),
        compiler_params=_params(("arbitrary",)),
        name="moe_experts",
    )(block_e, n_valid, xb, w_gate, w_up, w_down)


def _combine_kernel(d1_ref, d2_ref, h_ref, gates_ref, lnw_ref, lnb_ref, yb_ref, o_ref, r_ref, sem,
                    *, tm, alpha):
    base = pl.program_id(0) * tm

    def start(t, carry):
        _row_copy(yb_ref, d1_ref[base + t], r_ref.at[0], t, sem.at[0]).start()
        _row_copy(yb_ref, d2_ref[base + t], r_ref.at[1], t, sem.at[1]).start(priority=1)
        return carry

    lax.fori_loop(0, tm, start, 0, unroll=8)
    pltpu.make_async_copy(yb_ref.at[pl.ds(0, tm)], r_ref.at[0], sem.at[0]).wait()
    pltpu.make_async_copy(yb_ref.at[pl.ds(0, tm)], r_ref.at[1], sem.at[1]).wait()
    y = gates_ref[:, 0:1] * r_ref[0] + gates_ref[:, 1:2] * r_ref[1]
    o_ref[...] = _layer_norm(alpha * h_ref[...] + y, lnw_ref[...], lnb_ref[...])


def _combine(h, yb, dest1, dest2, gates, ln_w, ln_b, alpha):
    t, d = h.shape
    tm = TOKEN_ROWS
    return pl.pallas_call(
        functools.partial(_combine_kernel, tm=tm, alpha=alpha),
        grid_spec=pltpu.PrefetchScalarGridSpec(
            num_scalar_prefetch=2,
            grid=(t // tm,),
            in_specs=[pl.BlockSpec((tm, d), lambda i, d1, d2: (i, 0)),
                      pl.BlockSpec((tm, 2), lambda i, d1, d2: (i, 0)),
                      pl.BlockSpec((1, d), lambda i, d1, d2: (0, 0)),
                      pl.BlockSpec((1, d), lambda i, d1, d2: (0, 0)),
                      pl.BlockSpec(memory_space=pl.ANY)],
            out_specs=pl.BlockSpec((tm, d), lambda i, d1, d2: (i, 0)),
            scratch_shapes=[pltpu.VMEM((2, tm, d), F32), pltpu.SemaphoreType.DMA((2,))],
        ),
        out_shape=jax.ShapeDtypeStruct((t, d), F32),
        compiler_params=_params(("arbitrary",)),
        name="moe_combine",
    )(dest1, dest2, h, gates, ln_w.reshape(1, d), ln_b.reshape(1, d), yb)


def _moe(h1, info, counts, w_gate, w_up, w_down, layer, ln_w, ln_b, alpha):
    t = h1.shape[0]
    n_rows = 2 * t + N_EXPERTS * MOE_ROWS
    nb = n_rows // MOE_ROWS
    cnt = counts[:, 0].astype(jnp.int32)
    padded = (cnt + MOE_ROWS - 1) // MOE_ROWS * MOE_ROWS
    ends = jnp.cumsum(padded)
    pstart = ends - padded
    e1 = info[0].astype(jnp.int32)
    e2 = info[1].astype(jnp.int32)
    expert_ids = jnp.arange(N_EXPERTS, dtype=jnp.int32)[:, None]
    dest1 = jnp.sum(jnp.where(e1[None, :] == expert_ids, pstart[:, None], 0), axis=0) + info[4].astype(jnp.int32)
    dest2 = jnp.sum(jnp.where(e2[None, :] == expert_ids, pstart[:, None], 0), axis=0) + info[5].astype(jnp.int32)
    gates = jnp.stack([info[2], info[3]], axis=1)
    n_valid = (ends[-1] // MOE_ROWS).astype(jnp.int32).reshape(1)
    blk = jnp.minimum(jnp.arange(nb, dtype=jnp.int32), n_valid[0] - 1) * MOE_ROWS
    block_e = jnp.minimum(jnp.sum((ends[None, :] <= blk[:, None]).astype(jnp.int32), axis=1), N_EXPERTS - 1)
    zero_start = jnp.concatenate([jnp.maximum(ends - MOE_ROWS, 0), n_valid]).astype(jnp.int32)

    xb = _dispatch(h1, dest1, dest2, zero_start, n_rows)
    yb = _experts(xb, block_e, n_valid, w_gate, w_up, w_down, layer)
    return _combine(h1, yb, dest1, dest2, gates, ln_w, ln_b, alpha)


def _rotary_column_order(w_in):
    d = w_in.shape[0]
    nq = RET_HEADS * RET_DK

    def perm(w):
        return w.reshape(d, RET_HEADS, RET_DK // 2, 2).transpose(0, 3, 1, 2).reshape(d, nq)

    return jnp.concatenate([perm(w_in[:, :nq]), perm(w_in[:, nq:2 * nq]), w_in[:, 2 * nq:]], axis=1)


def kernel(x, w_in_even, w_out_even, lru_conv_w, lru_conv_b, lru_gate_w, lru_gate_b, lru_lambda,
           w_in_odd, w_out_odd, hg_lower_bounds, hg_norm_w, ln_w, ln_b, router_w,
           moe_w_gate, moe_w_up, moe_w_down):
    batch, seq_len, d = x.shape
    depth = ln_w.shape[0]
    alpha = (2.0 * depth) ** 0.25
    t = batch * seq_len
    h = x.reshape(t, d)
    router_wt = router_w.T
    half = RET_DK // 2
    inv_freq = ROPE_BASE ** (-jnp.arange(0, RET_DK, 2, dtype=F32) / RET_DK)
    inv_freq = jnp.tile(inv_freq, LANES // half).reshape(1, LANES)
    nret = RET_HEADS * RET_DV

    for layer in range(depth):
        j = layer // 2
        if layer % 2 == 0:
            w_in = _rotary_column_order(w_in_even[j]).astype(BF16)
            n_bf = 2 * RET_HEADS * RET_DK + 2 * nret
            proj_b, proj_f = _project(h, w_in, inv_freq, n_bf, rotary=True, seq_len=seq_len)
            ret = _retention(proj_b, batch, seq_len)
            h_fwd, h_bwd = _rglru(proj_f, lru_conv_w[j], lru_conv_b[j], lru_gate_w[j], lru_gate_b[j],
                                  lru_lambda[j], batch, seq_len)
            w_out = w_out_even[j].astype(BF16)
            mixer_inputs, weights = (ret, h_fwd, h_bwd, proj_f), [w_out[:nret], w_out[nret:]]
        else:
            w = w_in_odd[j]
            w_in = jnp.concatenate([w[:, :2 * d], w[:, 4 * d:], w[:, 2 * d:4 * d]], axis=1).astype(BF16)
            proj_b, proj_f = _project(h, w_in, inv_freq, 3 * d, rotary=False, seq_len=seq_len)
            o_fwd, o_bwd = _gla(proj_b, proj_f, hg_lower_bounds, layer, batch, seq_len)
            mixer_inputs, weights = (o_fwd, o_bwd, proj_b, hg_norm_w[j]), [w_out_odd[j].astype(BF16)]
        h1, info, counts = _out_project(layer % 2 == 0, mixer_inputs, weights, h, ln_w[layer, 0],
                                        ln_b[layer, 0], router_wt, alpha)
        h = _moe(h1, info, counts, moe_w_gate, moe_w_up, moe_w_down, layer,
                 ln_w[layer, 1], ln_b[layer, 1], alpha)
    return h.reshape(batch, seq_len, d)
```

```python
import functools
import math

import jax
import jax.numpy as jnp
from jax import lax
from jax.experimental import pallas as pl
from jax.experimental.pallas import tpu as pltpu

F32 = jnp.float32
BF16 = jnp.bfloat16

RET_HEADS = 4
RET_DK = 64
RET_DV = 128
RET_CHUNK = 128
ROPE_BASE = 10000.0
LRU_BLOCKS = 4
LRU_BW = 128
LRU_CONV = 4
LRU_C = 8.0
HG_HEADS = 8
HG_DK = 128
HG_DV = 128
N_EXPERTS = 16
N_GROUPS = 4
EXPERTS_PER_GROUP = 4
LN_EPS = 1e-5
RMS_EPS = 1e-6

LANES = 128
SUBLANES = 8
PROJ_ROWS = 512
PROJ_COLS = 512
RET_CHUNKS_PER_STEP = 4
LRU_ROWS = 512
GLA_CHUNK = 32
GLA_ROWS = 256
TOKEN_ROWS = 512
MOE_ROWS = 256
VMEM_LIMIT = 48 * 1024 * 1024

_NT = (((1,), (1,)), ((), ()))
_TN = (((0,), (0,)), ((), ()))


def _params(sem):
    return pltpu.CompilerParams(dimension_semantics=sem, vmem_limit_bytes=VMEM_LIMIT)


def _sigmoid(x):
    return 0.5 * jnp.tanh(0.5 * x) + 0.5


def _silu(x):
    return x * _sigmoid(x)


def _softplus(x):
    return jnp.maximum(x, 0.0) + jnp.log1p(jnp.exp(-jnp.abs(x)))


def _layer_norm(u, w, b):
    mu = jnp.mean(u, axis=-1, keepdims=True)
    d = u - mu
    var = jnp.mean(d * d, axis=-1, keepdims=True)
    return d * lax.rsqrt(var + LN_EPS) * w + b


def _split3(x):
    hi = x.astype(BF16)
    rem = x - hi.astype(F32)
    mid = rem.astype(BF16)
    lo = (rem - mid.astype(F32)).astype(BF16)
    return hi, mid, lo


def _proj_kernel(x_ref, w_ref, inv_ref, ob_ref, of_ref, *, rotary, seq_len, tm, n_bf):
    i = pl.program_id(0)
    xb = x_ref[...].astype(BF16)
    tn = PROJ_COLS
    for j in range(w_ref.shape[1] // tn):
        acc = jnp.dot(xb, w_ref[:, j * tn:(j + 1) * tn], preferred_element_type=F32)
        if rotary and j == 0:
            row = lax.broadcasted_iota(jnp.int32, (tm, 1), 0) + i * tm
            ang = (row % seq_len).astype(F32) * inv_ref[...]
            cos = jnp.cos(ang)
            sin = jnp.sin(ang)
            scale = RET_DK ** -0.5
            q1, q2 = acc[:, 0:128], acc[:, 128:256]
            k1, k2 = acc[:, 256:384], acc[:, 384:512]
            ob_ref[:, 0:128] = ((q1 * cos - q2 * sin) * scale).astype(BF16)
            ob_ref[:, 128:256] = ((q1 * sin + q2 * cos) * scale).astype(BF16)
            ob_ref[:, 256:384] = (k1 * cos - k2 * sin).astype(BF16)
            ob_ref[:, 384:512] = (k1 * sin + k2 * cos).astype(BF16)
        elif (j + 1) * tn <= n_bf:
            ob_ref[:, j * tn:(j + 1) * tn] = acc.astype(BF16)
        else:
            of_ref[:, j * tn - n_bf:(j + 1) * tn - n_bf] = acc


def _project(x, w_bf16, inv_freq, n_bf, *, rotary, seq_len):
    t, k = x.shape
    n = w_bf16.shape[1]
    tm = min(PROJ_ROWS, seq_len)
    return pl.pallas_call(
        functools.partial(_proj_kernel, rotary=rotary, seq_len=seq_len, tm=tm, n_bf=n_bf),
        grid=(t // tm,),
        in_specs=[
            pl.BlockSpec((tm, k), lambda i: (i, 0)),
            pl.BlockSpec((k, n), lambda i: (0, 0)),
            pl.BlockSpec((1, LANES), lambda i: (0, 0)),
        ],
        out_specs=[pl.BlockSpec((tm, n_bf), lambda i: (i, 0)),
                   pl.BlockSpec((tm, n - n_bf), lambda i: (i, 0))],
        out_shape=[jax.ShapeDtypeStruct((t, n_bf), BF16),
                   jax.ShapeDtypeStruct((t, n - n_bf), F32)],
        compiler_params=_params(("arbitrary",)),
        name="in_proj",
    )(x, w_bf16, inv_freq)


def _ret_log_gamma(head):
    out = jnp.full(head.shape, math.log1p(-(2.0 ** -5.0)), F32)
    for h in range(1, RET_HEADS):
        out = jnp.where(head == h, math.log1p(-(2.0 ** (-5.0 - h))), out)
    return out


def _ret_lane_head():
    lane = lax.broadcasted_iota(jnp.int32, (1, 2 * LANES), 1)
    return (lane % LANES) // (RET_DK // 2)


def _ret_state_mask():
    shape = (RET_HEADS * RET_DV, 2 * LANES)
    row_head = lax.broadcasted_iota(jnp.int32, shape, 0) // RET_DV
    col_head = (lax.broadcasted_iota(jnp.int32, shape, 1) % LANES) // (RET_DK // 2)
    return row_head == col_head


def _ret_bstate_kernel(k_ref, v_ref, sb_ref, s_ref, *, cps):
    c = RET_CHUNK

    @pl.when(pl.program_id(1) == 0)
    def _():
        s_ref[...] = jnp.zeros_like(s_ref)

    lg = _ret_log_gamma(_ret_lane_head())
    idx = lax.broadcasted_iota(jnp.int32, (c, 1), 0).astype(F32)
    k_decay = jnp.exp(lg * idx)
    chunk_decay = jnp.exp(lg * float(c))
    mask = _ret_state_mask()
    for cc in reversed(range(cps)):
        rows = slice(cc * c, (cc + 1) * c)
        sb_ref[0, cc] = s_ref[...].astype(BF16)
        kb = (k_ref[rows, :] * k_decay).astype(BF16)
        upd = lax.dot_general(v_ref[rows, :], kb, _TN, preferred_element_type=F32)
        s_ref[...] = s_ref[...] * chunk_decay + jnp.where(mask, upd, 0.0)


def _ret_out_kernel(q_ref, k_ref, v_ref, g_ref, sb_ref, o_ref, s_ref, *, cps):
    c = RET_CHUNK

    @pl.when(pl.program_id(1) == 0)
    def _():
        s_ref[...] = jnp.zeros_like(s_ref)

    lane_head = _ret_lane_head()
    lg = _ret_log_gamma(lane_head)
    idx = lax.broadcasted_iota(jnp.int32, (c, 1), 0).astype(F32)
    q_decay_f = jnp.exp(lg * (idx + 1.0))
    q_decay_b = jnp.exp(lg * (float(c) - idx))
    k_decay = jnp.exp(lg * (float(c) - 1.0 - idx))
    chunk_decay = jnp.exp(lg * float(c))
    mask = _ret_state_mask()
    ii = lax.broadcasted_iota(jnp.int32, (c, c), 0)
    jj = lax.broadcasted_iota(jnp.int32, (c, c), 1)
    dist = jnp.abs(ii - jj).astype(F32)
    intra_decay = [jnp.exp(math.log1p(-(2.0 ** (-5.0 - h))) * dist) for h in range(RET_HEADS)]

    for cc in range(cps):
        rows = slice(cc * c, (cc + 1) * c)
        q = q_ref[rows, :]
        k = k_ref[rows, :]
        v = v_ref[rows, :]
        qf = (q * q_decay_f).astype(BF16)
        qb = (q * q_decay_b).astype(BF16)
        cross = (lax.dot_general(qf, s_ref[...].astype(BF16), _NT, preferred_element_type=F32)
                 + lax.dot_general(qb, sb_ref[0, cc], _NT, preferred_element_type=F32))
        for h in range(RET_HEADS):
            qh = jnp.where(lane_head == h, q, jnp.zeros_like(q))
            s = lax.dot_general(qh, k, _NT, preferred_element_type=F32) * intra_decay[h]
            cols = slice(h * RET_DV, (h + 1) * RET_DV)
            o = jnp.dot(s.astype(BF16), v[:, cols], preferred_element_type=F32) + cross[:, cols]
            mu = jnp.mean(o, axis=-1, keepdims=True)
            d = o - mu
            var = jnp.mean(d * d, axis=-1, keepdims=True)
            gate = _silu(g_ref[rows, cols].astype(F32))
            o_ref[rows, cols] = (gate * (d * lax.rsqrt(var + LN_EPS))).astype(BF16)
        kf = (k * k_decay).astype(BF16)
        upd = lax.dot_general(v, kf, _TN, preferred_element_type=F32)
        s_ref[...] = s_ref[...] * chunk_decay + jnp.where(mask, upd, 0.0)


def _retention(proj, batch, seq_len):
    t = proj.shape[0]
    c = RET_CHUNK
    cps = min(RET_CHUNKS_PER_STEP, seq_len // c)
    rows = cps * c
    ns = seq_len // rows
    dv = RET_HEADS * RET_DV
    state_shape = (dv, 2 * LANES)
    rev = lambda b, n: b * ns + (ns - 1 - n)
    fwd = lambda b, n: b * ns + n
    sb = pl.pallas_call(
        functools.partial(_ret_bstate_kernel, cps=cps),
        grid=(batch, ns),
        in_specs=[
            pl.BlockSpec((rows, 2 * LANES), lambda b, n: (rev(b, n), 1)),
            pl.BlockSpec((rows, dv), lambda b, n: (rev(b, n), 1)),
        ],
        out_specs=pl.BlockSpec((1, cps) + state_shape, lambda b, n: (b, ns - 1 - n, 0, 0)),
        out_shape=jax.ShapeDtypeStruct((batch, ns * cps) + state_shape, BF16),
        scratch_shapes=[pltpu.VMEM(state_shape, F32)],
        compiler_params=_params(("arbitrary", "arbitrary")),
        name="ret_bstate",
    )(proj, proj)
    return pl.pallas_call(
        functools.partial(_ret_out_kernel, cps=cps),
        grid=(batch, ns),
        in_specs=[
            pl.BlockSpec((rows, 2 * LANES), lambda b, n: (fwd(b, n), 0)),
            pl.BlockSpec((rows, 2 * LANES), lambda b, n: (fwd(b, n), 1)),
            pl.BlockSpec((rows, dv), lambda b, n: (fwd(b, n), 1)),
            pl.BlockSpec((rows, dv), lambda b, n: (fwd(b, n), 2)),
            pl.BlockSpec((1, cps) + state_shape, lambda b, n: (b, n, 0, 0)),
        ],
        out_specs=pl.BlockSpec((rows, dv), lambda b, n: (fwd(b, n), 0)),
        out_shape=jax.ShapeDtypeStruct((t, dv), BF16),
        scratch_shapes=[pltpu.VMEM(state_shape, F32)],
        compiler_params=_params(("arbitrary", "arbitrary")),
        name="ret_out",
    )(proj, proj, proj, proj, sb)


def _lru_kernel(xfp_ref, xf_ref, xfn_ref, xbp_ref, xb_ref, xbn_ref, cw_ref, cb_ref, gw_ref, gb_ref,
                lam_ref, hf_ref, hb_ref, xx_ref, a_ref, b_ref, h_ref, *, nt, ts, batch):
    i = pl.program_id(0)
    halo = SUBLANES
    lo = LRU_CONV // 2

    @pl.when(i == 0)
    def _():
        h_ref[...] = jnp.zeros_like(h_ref)

    def prepare(xp_ref, x_ref, xn_ref, tile, z, slot):
        for b in range(batch):
            xx_ref[0:halo, :] = jnp.where(tile == 0, 0.0, xp_ref[b])
            xx_ref[halo:halo + ts, :] = x_ref[b]
            xx_ref[halo + ts:2 * halo + ts, :] = jnp.where(tile == nt - 1, 0.0, xn_ref[b])
            xc = cb_ref[...]
            for tap in range(LRU_CONV):
                xc = xc + cw_ref[tap:tap + 1, :] * xx_ref[pl.ds(halo - lo + tap, ts), :]
            for n in range(LRU_BLOCKS):
                cols = slice(n * LRU_BW, (n + 1) * LRU_BW)
                xn = xc[:, cols]
                g = (jnp.dot(xn.astype(BF16), gw_ref[z, n], preferred_element_type=F32)
                     + gb_ref[z, n:n + 1, :])
                r = _sigmoid(g[:, :LRU_BW])
                ig = _sigmoid(g[:, LRU_BW:])
                a = jnp.exp((-LRU_C) * r * _softplus(-lam_ref[z, :, cols]))
                a_ref[slot + b, :, cols] = a
                b_ref[slot + b, :, cols] = jnp.sqrt(1.0 - a * a) * (ig * xn)

    prepare(xfp_ref, xf_ref, xfn_ref, i, 0, 0)
    prepare(xbp_ref, xb_ref, xbn_ref, nt - 1 - i, 1, batch)

    def step(s, hs):
        out = []
        for k in range(2 * batch):
            row = s if k < batch else ts - 1 - s
            h = a_ref[k, pl.ds(row, 1), :] * hs[k] + b_ref[k, pl.ds(row, 1), :]
            if k < batch:
                hf_ref[k, pl.ds(row, 1), :] = h
            else:
                hb_ref[k - batch, pl.ds(row, 1), :] = h
            out.append(h)
        return tuple(out)

    hs = lax.fori_loop(0, ts, step, tuple(h_ref[k] for k in range(2 * batch)), unroll=8)
    for k in range(2 * batch):
        h_ref[k] = hs[k]


def _rglru(proj_f32, conv_w, conv_b, gate_w, gate_b, lam, batch, seq_len):
    w = LRU_BLOCKS * LRU_BW
    ts = min(LRU_ROWS, seq_len)
    nt = seq_len // ts
    rows8 = ts // SUBLANES
    last8 = seq_len // SUBLANES - 1
    x3 = proj_f32.reshape(batch, seq_len, proj_f32.shape[1])
    gw = jnp.concatenate([gate_w[:, 0], gate_w[:, 1]], axis=-1).astype(BF16)
    gb = jnp.concatenate([gate_b[:, 0], gate_b[:, 1]], axis=-1)
    bwd = lambda i: nt - 1 - i

    def tile_specs(tile):
        return [
            pl.BlockSpec((batch, SUBLANES, w), lambda i: (0, jnp.maximum(tile(i) * rows8 - 1, 0), 0)),
            pl.BlockSpec((batch, ts, w), lambda i: (0, tile(i), 0)),
            pl.BlockSpec((batch, SUBLANES, w), lambda i: (0, jnp.minimum((tile(i) + 1) * rows8, last8), 0)),
        ]

    full = lambda a: pl.BlockSpec(a.shape, lambda i: (0,) * a.ndim)
    cb = conv_b.reshape(1, w)
    lam3 = lam.reshape(2, 1, w)
    state = jax.ShapeDtypeStruct((batch, seq_len, w), F32)
    h_fwd, h_bwd = pl.pallas_call(
        functools.partial(_lru_kernel, nt=nt, ts=ts, batch=batch),
        grid=(nt,),
        in_specs=tile_specs(lambda i: i) + tile_specs(bwd) + [full(conv_w), full(cb), full(gw), full(gb),
                                                               full(lam3)],
        out_specs=[pl.BlockSpec((batch, ts, w), lambda i: (0, i, 0)),
                   pl.BlockSpec((batch, ts, w), lambda i: (0, bwd(i), 0))],
        out_shape=[state, state],
        scratch_shapes=[pltpu.VMEM((ts + 2 * SUBLANES, w), F32), pltpu.VMEM((2 * batch, ts, w), F32),
                        pltpu.VMEM((2 * batch, ts, w), F32), pltpu.VMEM((2 * batch, 1, w), F32)],
        compiler_params=_params(("arbitrary",)),
        name="lru_scan",
    )(x3, x3, x3, x3, x3, x3, conv_w, cb, gw, gb, lam3)
    return h_fwd.reshape(batch * seq_len, w), h_bwd.reshape(batch * seq_len, w)


def _gla_kernel(qf_ref, vf_ref, zf_ref, qb_ref, vb_ref, zb_ref, lbp_ref, of_ref, ob_ref, s_ref,
                *, ts, layer, batch):
    c = GLA_CHUNK
    nchunks = ts // c

    @pl.when(pl.program_id(0) == 0)
    def _():
        s_ref[...] = jnp.zeros_like(s_ref)

    p = lbp_ref[...]
    e = jnp.exp(p - jnp.max(p, axis=0, keepdims=True))
    sm = e / jnp.sum(e, axis=0, keepdims=True)
    lb = jnp.zeros((1, p.shape[1]), F32)
    for r in range(1, layer + 1):
        lb = lb + sm[r:r + 1, :]

    head_cols = [slice(h * HG_DK, (h + 1) * HG_DK) for h in range(HG_HEADS)]
    hc = HG_HEADS * c
    ii = lax.broadcasted_iota(jnp.int32, (c, c), 0)
    jj = lax.broadcasted_iota(jnp.int32, (c, c), 1)
    si = lax.broadcasted_iota(jnp.int32, (hc, hc), 0)
    sj = lax.broadcasted_iota(jnp.int32, (hc, hc), 1)
    same_head = si // c == sj // c
    mid = c // 2

    def stack(a):
        return jnp.concatenate([a[:, cols] for cols in head_cols], axis=0)

    def chunk(q_ref, v_ref, z_ref, o_ref, b, ch, reverse):
        slot = b + (batch if reverse else 0)
        rows = pl.ds(ch * c, c)
        tri = ((jj >= ii) if reverse else (jj <= ii)).astype(BF16)
        keep = jnp.logical_and(same_head, (sj >= si) if reverse else (sj <= si))
        end = 0 if reverse else c - 1
        sig = _sigmoid(z_ref[b, rows, :])
        log_f = jnp.log(lb + (1.0 - lb) * sig)
        key = (1.0 - lb) * (1.0 - sig)
        qs = _silu(q_ref[b, rows, :].astype(F32)) * (HG_DK ** -0.5)
        f_hi, f_mid, f_lo = _split3(log_f)
        bcum = (jnp.dot(tri, f_lo, preferred_element_type=F32)
                + jnp.dot(tri, f_mid, preferred_element_type=F32)
                + jnp.dot(tri, f_hi, preferred_element_type=F32))
        ref_row = bcum[mid:mid + 1, :]
        b_end = bcum[end:end + 1, :]
        qe = (qs * jnp.exp(bcum - ref_row)).astype(BF16)
        ke = (key * jnp.exp(ref_row - bcum)).astype(BF16)
        qd = (qs * jnp.exp(bcum)).astype(BF16)
        kd = (key * jnp.exp(b_end - bcum)).astype(BF16)
        decay = jnp.exp(b_end)
        v = v_ref[b, rows, :]
        att = lax.dot_general(stack(qe), stack(ke), _NT, preferred_element_type=F32)
        att = jnp.where(keep, att, 0.0).astype(BF16)
        intra = jnp.dot(att, stack(v), preferred_element_type=F32)
        for h, cols in enumerate(head_cols):
            st = s_ref[slot, h]
            inter = lax.dot_general(qd[:, cols], st.astype(BF16), _NT, preferred_element_type=F32)
            o_ref[b, rows, cols] = intra[h * c:(h + 1) * c, :] + inter
            s_ref[slot, h] = st * decay[:, cols] + lax.dot_general(v[:, cols], kd[:, cols], _TN,
                                                                  preferred_element_type=F32)

    for cc in range(nchunks):
        for b in range(batch):
            chunk(qf_ref, vf_ref, zf_ref, of_ref, b, cc, False)
        for b in range(batch):
            chunk(qb_ref, vb_ref, zb_ref, ob_ref, b, nchunks - 1 - cc, True)


def _gla(proj_bf16, proj_f32, lower_bounds, layer, batch, seq_len):
    d = HG_HEADS * HG_DK
    ts = min(GLA_ROWS, seq_len)
    nt = seq_len // ts
    depth = lower_bounds.shape[0]
    pb = proj_bf16.reshape(batch, seq_len, proj_bf16.shape[1])
    pf = proj_f32.reshape(batch, seq_len, proj_f32.shape[1])
    bwd = lambda i: nt - 1 - i
    blk = lambda tile, col: pl.BlockSpec((batch, ts, d), lambda i: (0, tile(i), col))
    fwd = lambda i: i
    out = jax.ShapeDtypeStruct((batch, seq_len, d), F32)
    o_fwd, o_bwd = pl.pallas_call(
        functools.partial(_gla_kernel, ts=ts, layer=layer, batch=batch),
        grid=(nt,),
        in_specs=[blk(fwd, 0), blk(fwd, 1), blk(fwd, 0), blk(bwd, 0), blk(bwd, 1), blk(bwd, 1),
                  pl.BlockSpec((depth, d), lambda i: (0, 0))],
        out_specs=[blk(fwd, 0), blk(bwd, 0)],
        out_shape=[out, out],
        scratch_shapes=[pltpu.VMEM((2 * batch, HG_HEADS, HG_DV, HG_DK), F32)],
        compiler_params=_params(("arbitrary",)),
        name="gla_scan",
    )(pb, pb, pf, pb, pb, pf, lower_bounds)
    return o_fwd.reshape(batch * seq_len, d), o_bwd.reshape(batch * seq_len, d)


def _top2(p):
    v1 = jnp.maximum(jnp.maximum(p[0], p[1]), jnp.maximum(p[2], p[3]))
    i1 = jnp.where(p[0] == v1, 0, jnp.where(p[1] == v1, 1, jnp.where(p[2] == v1, 2, 3)))
    q = [jnp.where(i1 == k, -1.0, p[k]) for k in range(4)]
    v2 = jnp.maximum(jnp.maximum(q[0], q[1]), jnp.maximum(q[2], q[3]))
    i2 = jnp.where(q[0] == v2, 0, jnp.where(q[1] == v2, 1, jnp.where(q[2] == v2, 2, 3)))
    return v1, i1, v2, i2


def _outproj_kernel(*refs, even, alpha, tm):
    if even:
        ret_ref, hf_ref, hb_ref, gr_ref, w0_ref, w1_ref = refs[:6]
        rest = refs[6:]
        lru = ((hf_ref[...] + hb_ref[...]) * jax.nn.gelu(gr_ref[...])).astype(BF16)
        y = (jnp.dot(ret_ref[...], w0_ref[...], preferred_element_type=F32)
             + jnp.dot(lru, w1_ref[...], preferred_element_type=F32))
    else:
        of_ref, ob_ref, g_ref, nw_ref, w0_ref = refs[:5]
        rest = refs[5:]
        o = of_ref[...] + ob_ref[...]
        ms = jnp.mean(o * o, axis=-1, keepdims=True)
        mix = o * lax.rsqrt(ms + RMS_EPS) * nw_ref[...] * _silu(g_ref[...].astype(F32))
        y = jnp.dot(mix.astype(BF16), w0_ref[...], preferred_element_type=F32)
    h_ref, lnw_ref, lnb_ref, rw_ref, o_ref, info_ref, cnt_ref, carry_ref = rest
    i = pl.program_id(0)

    @pl.when(i == 0)
    def _():
        carry_ref[...] = jnp.zeros_like(carry_ref)

    h1 = _layer_norm(alpha * h_ref[...] + y, lnw_ref[...], lnb_ref[...])
    o_ref[...] = h1

    h_hi = h1.astype(BF16)
    h_lo = (h1 - h_hi.astype(F32)).astype(BF16)
    rw = rw_ref[...]
    r_hi = rw.astype(BF16)
    r_lo = (rw - r_hi.astype(F32)).astype(BF16)
    logits = (lax.dot_general(r_lo, h_hi, _NT, preferred_element_type=F32)
              + lax.dot_general(r_hi, h_lo, _NT, preferred_element_type=F32)
              + lax.dot_general(r_hi, h_hi, _NT, preferred_element_type=F32))
    ex = jnp.exp(logits - jnp.max(logits, axis=0, keepdims=True))
    probs = ex / jnp.sum(ex, axis=0, keepdims=True)
    best = None
    for g in range(N_GROUPS):
        rows = [probs[g * EXPERTS_PER_GROUP + k:g * EXPERTS_PER_GROUP + k + 1, :]
                for k in range(EXPERTS_PER_GROUP)]
        v1, i1, v2, i2 = _top2(rows)
        cand = (v1 + v2, v1, i1 + g * EXPERTS_PER_GROUP, v2, i2 + g * EXPERTS_PER_GROUP)
        if best is None:
            best = cand
        else:
            take = cand[0] > best[0]
            best = tuple(jnp.where(take, cn, bs) for cn, bs in zip(cand, best))
    _, v1, e1, v2, e2 = best
    denom = v1 + v2
    g1 = v1 / denom
    g2 = v2 / denom

    eid = lax.broadcasted_iota(jnp.int32, (N_EXPERTS, tm), 0)
    oh1 = (eid == e1).astype(F32)
    oh2 = (eid == e2).astype(F32)
    oh = oh1 + oh2
    tt = lax.broadcasted_iota(jnp.int32, (tm, tm), 0)
    uu = lax.broadcasted_iota(jnp.int32, (tm, tm), 1)
    before = (tt < uu).astype(BF16)
    base = carry_ref[:, 0:1] + jnp.dot(oh.astype(BF16), before, preferred_element_type=F32)
    rank1 = jnp.sum(oh1 * base, axis=0, keepdims=True)
    rank2 = jnp.sum(oh2 * base, axis=0, keepdims=True)
    carry_ref[...] = carry_ref[...] + jnp.sum(oh, axis=1, keepdims=True)
    cnt_ref[...] = carry_ref[...]
    zero = jnp.zeros_like(g1)
    info_ref[...] = jnp.concatenate(
        [e1.astype(F32), e2.astype(F32), g1, g2, rank1, rank2, zero, zero], axis=0)


def _out_project(even, mixer_inputs, weights_bf16, h, ln_w, ln_b, router_wt, alpha):
    t, d = h.shape
    tm = TOKEN_ROWS
    row = lambda i: (i, 0)
    const = lambda i: (0, 0)
    if even:
        ret, h_fwd, h_bwd, proj_f32 = mixer_inputs
        w = h_fwd.shape[1]
        in_specs = [pl.BlockSpec((tm, ret.shape[1]), row), pl.BlockSpec((tm, w), row),
                    pl.BlockSpec((tm, w), row), pl.BlockSpec((tm, w), lambda i: (i, 1))]
        args = [ret, h_fwd, h_bwd, proj_f32]
    else:
        o_fwd, o_bwd, proj_bf16, norm_w = mixer_inputs
        in_specs = [pl.BlockSpec((tm, d), row), pl.BlockSpec((tm, d), row),
                    pl.BlockSpec((tm, d), lambda i: (i, 2)), pl.BlockSpec((1, d), const)]
        args = [o_fwd, o_bwd, proj_bf16, norm_w.reshape(1, d)]
    in_specs += [pl.BlockSpec(wm.shape, const) for wm in weights_bf16]
    in_specs += [pl.BlockSpec((tm, d), row), pl.BlockSpec((1, d), const), pl.BlockSpec((1, d), const),
                 pl.BlockSpec((N_EXPERTS, d), const)]
    args += list(weights_bf16) + [h, ln_w.reshape(1, d), ln_b.reshape(1, d), router_wt]
    return pl.pallas_call(
        functools.partial(_outproj_kernel, even=even, alpha=alpha, tm=tm),
        grid=(t // tm,),
        in_specs=in_specs,
        out_specs=[pl.BlockSpec((tm, d), row),
                   pl.BlockSpec((SUBLANES, tm), lambda i: (0, i)),
                   pl.BlockSpec((N_EXPERTS, LANES), const)],
        out_shape=[jax.ShapeDtypeStruct((t, d), F32),
                   jax.ShapeDtypeStruct((SUBLANES, t), F32),
                   jax.ShapeDtypeStruct((N_EXPERTS, LANES), F32)],
        scratch_shapes=[pltpu.VMEM((N_EXPERTS, LANES), F32)],
        compiler_params=_params(("arbitrary",)),
        name="out_proj_router",
    )(*args)


def _row_copy(src_ref, src_row, dst_ref, dst_row, sem):
    return pltpu.make_async_copy(src_ref.at[pl.ds(src_row, 1)], dst_ref.at[pl.ds(dst_row, 1)], sem)


def _dispatch_kernel(d1_ref, d2_ref, zs_ref, h_ref, xb_ref, zero_ref, sem, *, tm):
    i = pl.program_id(0)
    base = i * tm

    @pl.when(i == 0)
    def _():
        zero_ref[...] = jnp.zeros_like(zero_ref)

        def clear(row):
            start = pl.multiple_of(row, MOE_ROWS)
            return pltpu.make_async_copy(zero_ref, xb_ref.at[pl.ds(start, MOE_ROWS)], sem.at[0])

        for e in range(N_EXPERTS):
            clear(zs_ref[e]).start()
        for e in range(N_EXPERTS):
            clear(zs_ref[e]).wait()

        def clear_tail(b, carry):
            clear(b * MOE_ROWS).start()
            clear(b * MOE_ROWS).wait()
            return carry

        lax.fori_loop(zs_ref[N_EXPERTS], xb_ref.shape[0] // MOE_ROWS, clear_tail, 0)

    def start(t, carry):
        _row_copy(h_ref, t, xb_ref, d1_ref[base + t], sem.at[0]).start()
        _row_copy(h_ref, t, xb_ref, d2_ref[base + t], sem.at[1]).start(priority=1)
        return carry

    lax.fori_loop(0, tm, start, 0, unroll=8)
    pltpu.make_async_copy(h_ref, xb_ref.at[pl.ds(0, tm)], sem.at[0]).wait()
    pltpu.make_async_copy(h_ref, xb_ref.at[pl.ds(0, tm)], sem.at[1]).wait()


def _dispatch(h, dest1, dest2, zero_start, n_rows):
    t, d = h.shape
    tm = TOKEN_ROWS
    return pl.pallas_call(
        functools.partial(_dispatch_kernel, tm=tm),
        grid_spec=pltpu.PrefetchScalarGridSpec(
            num_scalar_prefetch=3,
            grid=(t // tm,),
            in_specs=[pl.BlockSpec((tm, d), lambda i, d1, d2, zs: (i, 0))],
            out_specs=pl.BlockSpec(memory_space=pl.ANY),
            scratch_shapes=[pltpu.VMEM((MOE_ROWS, d), F32), pltpu.SemaphoreType.DMA((2,))],
        ),
        out_shape=jax.ShapeDtypeStruct((n_rows, d), F32),
        compiler_params=_params(("arbitrary",)),
        name="moe_dispatch",
    )(dest1, dest2, zero_start, h)


def _expert_kernel(be_ref, nv_ref, x_ref, wg_ref, wu_ref, wd_ref, o_ref, wgb_ref, wub_ref, wdb_ref):
    i = pl.program_id(0)
    changed = jnp.logical_or(i == 0, be_ref[i] != be_ref[jnp.maximum(i - 1, 0)])

    @pl.when(changed)
    def _():
        wgb_ref[...] = wg_ref[...].astype(BF16)
        wub_ref[...] = wu_ref[...].astype(BF16)
        wdb_ref[...] = wd_ref[...].astype(BF16)

    @pl.when(i < nv_ref[0])
    def _():
        x = x_ref[...].astype(BF16)
        gate = jnp.dot(x, wgb_ref[...], preferred_element_type=F32)
        up = jnp.dot(x, wub_ref[...], preferred_element_type=F32)
        hid = (_silu(gate) * up).astype(BF16)
        o_ref[...] = jnp.dot(hid, wdb_ref[...], preferred_element_type=F32)

    @pl.when(i >= nv_ref[0])
    def _():
        o_ref[...] = jnp.zeros_like(o_ref)


def _experts(xb, block_e, n_valid, w_gate, w_up, w_down, layer):
    p, d = xb.shape
    de = w_gate.shape[3]
    nb = p // MOE_ROWS
    return pl.pallas_call(
        _expert_kernel,
        grid_spec=pltpu.PrefetchScalarGridSpec(
            num_scalar_prefetch=2,
            grid=(nb,),
            in_specs=[
                pl.BlockSpec((MOE_ROWS, d), lambda i, be, nv: (jnp.maximum(jnp.minimum(i, nv[0] - 1), 0), 0)),
                pl.BlockSpec((None, None, d, de), lambda i, be, nv: (layer, be[i], 0, 0)),
                pl.BlockSpec((None, None, d, de), lambda i, be, nv: (layer, be[i], 0, 0)),
                pl.BlockSpec((None, None, de, d), lambda i, be, nv: (layer, be[i], 0, 0)),
            ],
            out_specs=pl.BlockSpec((MOE_ROWS, d), lambda i, be, nv: (i, 0)),
            scratch_shapes=[pltpu.VMEM((d, de), BF16), pltpu.VMEM((d, de), BF16),
                            pltpu.VMEM((de, d), BF16)],
        ),
        out_shape=jax.ShapeDtypeStruct((p, d), F32),
        compiler_params=_params(("arbitrary",)),
        name="moe_experts",
    )(block_e, n_valid, xb, w_gate, w_up, w_down)


def _combine_kernel(d1_ref, d2_ref, h_ref, gates_ref, lnw_ref, lnb_ref, yb_ref, o_ref, r_ref, sem,
                    *, tm, alpha):
    base = pl.program_id(0) * tm

    def start(t, carry):
        _row_copy(yb_ref, d1_ref[base + t], r_ref.at[0], t, sem.at[0]).start()
        _row_copy(yb_ref, d2_ref[base + t], r_ref.at[1], t, sem.at[1]).start(priority=1)
        return carry

    lax.fori_loop(0, tm, start, 0, unroll=8)
    pltpu.make_async_copy(yb_ref.at[pl.ds(0, tm)], r_ref.at[0], sem.at[0]).wait()
    pltpu.make_async_copy(yb_ref.at[pl.ds(0, tm)], r_ref.at[1], sem.at[1]).wait()
    y = gates_ref[:, 0:1] * r_ref[0] + gates_ref[:, 1:2] * r_ref[1]
    o_ref[...] = _layer_norm(alpha * h_ref[...] + y, lnw_ref[...], lnb_ref[...])


def _combine(h, yb, dest1, dest2, gates, ln_w, ln_b, alpha):
    t, d = h.shape
    tm = TOKEN_ROWS
    return pl.pallas_call(
        functools.partial(_combine_kernel, tm=tm, alpha=alpha),
        grid_spec=pltpu.PrefetchScalarGridSpec(
            num_scalar_prefetch=2,
            grid=(t // tm,),
            in_specs=[pl.BlockSpec((tm, d), lambda i, d1, d2: (i, 0)),
                      pl.BlockSpec((tm, 2), lambda i, d1, d2: (i, 0)),
                      pl.BlockSpec((1, d), lambda i, d1, d2: (0, 0)),
                      pl.BlockSpec((1, d), lambda i, d1, d2: (0, 0)),
                      pl.BlockSpec(memory_space=pl.ANY)],
            out_specs=pl.BlockSpec((tm, d), lambda i, d1, d2: (i, 0)),
            scratch_shapes=[pltpu.VMEM((2, tm, d), F32), pltpu.SemaphoreType.DMA((2,))],
        ),
        out_shape=jax.ShapeDtypeStruct((t, d), F32),
        compiler_params=_params(("arbitrary",)),
        name="moe_combine",
    )(dest1, dest2, h, gates, ln_w.reshape(1, d), ln_b.reshape(1, d), yb)


def _invert_kernel(d1_ref, d2_ref, init_ref, ids_ref, sem, *, t):
    fill = pltpu.make_async_copy(init_ref, ids_ref, sem)
    fill.start()
    fill.wait()

    def body(tok, carry):
        ids_ref[d1_ref[tok]] = tok
        ids_ref[d2_ref[tok]] = tok + t
        return carry

    lax.fori_loop(0, t, body, 0, unroll=8)


def _invert(dest1, dest2, n_rows):
    t = dest1.shape[0]
    smem = pl.BlockSpec(memory_space=pltpu.SMEM)
    return pl.pallas_call(
        functools.partial(_invert_kernel, t=t),
        in_specs=[smem, smem, pl.BlockSpec(memory_space=pl.ANY)],
        out_specs=smem,
        out_shape=jax.ShapeDtypeStruct((n_rows,), jnp.int32),
        scratch_shapes=[pltpu.SemaphoreType.DMA(())],
        name="moe_invert",
    )(dest1, dest2, jnp.full((n_rows,), -1, jnp.int32))


def _expert_fused_kernel(be_ref, src_ref, dst_ref, h_ref, wg_ref, wu_ref, wd_ref, yk_ref,
                         xbuf, ybuf, wgb_ref, wub_ref, wdb_ref, sem_g, sem_s):
    i = pl.program_id(0)
    nb = pl.num_programs(0)
    slot = i % 2
    other = 1 - slot

    def gather(block, buf):
        base = block * MOE_ROWS
        for r in range(MOE_ROWS):
            _row_copy(h_ref, src_ref[base + r], xbuf.at[buf], r, sem_g.at[buf]).start()

    def scatter(block, buf):
        base = block * MOE_ROWS
        for r in range(MOE_ROWS):
            _row_copy(ybuf.at[buf], r, yk_ref, dst_ref[base + r], sem_s.at[buf]).start(priority=1)

    def wait_gather(buf):
        pltpu.make_async_copy(h_ref.at[pl.ds(0, MOE_ROWS)], xbuf.at[buf], sem_g.at[buf]).wait()

    def wait_scatter(buf):
        pltpu.make_async_copy(ybuf.at[buf], yk_ref.at[pl.ds(0, MOE_ROWS)], sem_s.at[buf]).wait()

    @pl.when(i == 0)
    def _():
        ybuf[...] = jnp.zeros_like(ybuf)
        gather(0, 0)

    wait_gather(slot)

    @pl.when(i >= 1)
    def _():
        wait_scatter(slot)

    changed = jnp.logical_or(i == 0, be_ref[i] != be_ref[jnp.maximum(i - 1, 0)])

    @pl.when(changed)
    def _():
        wgb_ref[...] = wg_ref[...].astype(BF16)
        wub_ref[...] = wu_ref[...].astype(BF16)
        wdb_ref[...] = wd_ref[...].astype(BF16)

    gather(jnp.minimum(i + 1, nb - 1), other)
    scatter(jnp.maximum(i - 1, 0), other)
    x = xbuf[slot].astype(BF16)
    gate = jnp.dot(x, wgb_ref[...], preferred_element_type=F32)
    up = jnp.dot(x, wub_ref[...], preferred_element_type=F32)
    hid = (_silu(gate) * up).astype(BF16)
    ybuf[slot] = jnp.dot(hid, wdb_ref[...], preferred_element_type=F32)

    @pl.when(i == nb - 1)
    def _():
        scatter(i, slot)
        wait_gather(other)
        wait_scatter(other)
        wait_scatter(slot)


def _experts_fused(h, src_tok, dst_row, block_e, w_gate, w_up, w_down, layer):
    t, d = h.shape
    de = w_gate.shape[3]
    p = src_tok.shape[0]
    nb = p // MOE_ROWS
    weight = lambda shape: pl.BlockSpec((None, None) + shape, lambda i, be, src, dst: (layer, be[i], 0, 0))
    return pl.pallas_call(
        _expert_fused_kernel,
        grid_spec=pltpu.PrefetchScalarGridSpec(
            num_scalar_prefetch=3,
            grid=(nb,),
            in_specs=[pl.BlockSpec(memory_space=pl.ANY), weight((d, de)), weight((d, de)), weight((de, d))],
            out_specs=pl.BlockSpec(memory_space=pl.ANY),
            scratch_shapes=[pltpu.VMEM((2, MOE_ROWS, d), F32), pltpu.VMEM((2, MOE_ROWS, d), F32),
                            pltpu.VMEM((d, de), BF16), pltpu.VMEM((d, de), BF16), pltpu.VMEM((de, d), BF16),
                            pltpu.SemaphoreType.DMA((2,)), pltpu.SemaphoreType.DMA((2,))],
        ),
        out_shape=jax.ShapeDtypeStruct((p, d), F32),
        compiler_params=_params(("arbitrary",)),
        name="moe_experts",
    )(block_e, src_tok, dst_row, h, w_gate, w_up, w_down)


def _combine_dense_kernel(h_ref, y0_ref, y1_ref, gates_ref, lnw_ref, lnb_ref, o_ref, *, alpha):
    y = gates_ref[:, 0:1] * y0_ref[...] + gates_ref[:, 1:2] * y1_ref[...]
    o_ref[...] = _layer_norm(alpha * h_ref[...] + y, lnw_ref[...], lnb_ref[...])


def _combine_dense(h, yk, gates, ln_w, ln_b, alpha):
    t, d = h.shape
    tm = TOKEN_ROWS
    nt = t // tm
    row = lambda i: (i, 0)
    const = lambda i: (0, 0)
    return pl.pallas_call(
        functools.partial(_combine_dense_kernel, alpha=alpha),
        grid=(nt,),
        in_specs=[pl.BlockSpec((tm, d), row), pl.BlockSpec((tm, d), row),
                  pl.BlockSpec((tm, d), lambda i: (nt + i, 0)), pl.BlockSpec((tm, 2), row),
                  pl.BlockSpec((1, d), const), pl.BlockSpec((1, d), const)],
        out_specs=pl.BlockSpec((tm, d), row),
        out_shape=jax.ShapeDtypeStruct((t, d), F32),
        compiler_params=_params(("arbitrary",)),
        name="moe_combine",
    )(h, yk, yk, gates, ln_w.reshape(1, d), ln_b.reshape(1, d))


def _moe_fused(h1, info, counts, w_gate, w_up, w_down, layer, ln_w, ln_b, alpha):
    t = h1.shape[0]
    n_rows = 2 * t + N_EXPERTS * MOE_ROWS
    nb = n_rows // MOE_ROWS
    cnt = counts[:, 0].astype(jnp.int32)
    padded = (cnt + MOE_ROWS - 1) // MOE_ROWS * MOE_ROWS
    ends = jnp.cumsum(padded)
    pstart = ends - padded
    e1 = info[0].astype(jnp.int32)
    e2 = info[1].astype(jnp.int32)
    expert_ids = jnp.arange(N_EXPERTS, dtype=jnp.int32)[:, None]
    dest1 = jnp.sum(jnp.where(e1[None, :] == expert_ids, pstart[:, None], 0), axis=0) + info[4].astype(jnp.int32)
    dest2 = jnp.sum(jnp.where(e2[None, :] == expert_ids, pstart[:, None], 0), axis=0) + info[5].astype(jnp.int32)
    gates = jnp.stack([info[2], info[3]], axis=1)
    blk = jnp.arange(nb, dtype=jnp.int32) * MOE_ROWS
    block_e = jnp.minimum(jnp.sum((ends[None, :] <= blk[:, None]).astype(jnp.int32), axis=1), N_EXPERTS - 1)
    ids = _invert(dest1, dest2, n_rows)
    is_pad = ids < 0
    src_tok = jnp.where(is_pad, 0, ids % t)
    dst_row = jnp.where(is_pad, 2 * t - 1 + jnp.cumsum(is_pad.astype(jnp.int32)), ids)
    yk = _experts_fused(h1, src_tok, dst_row, block_e, w_gate, w_up, w_down, layer)
    return _combine_dense(h1, yk, gates, ln_w, ln_b, alpha)


def _moe(h1, info, counts, w_gate, w_up, w_down, layer, ln_w, ln_b, alpha):
    t = h1.shape[0]
    n_rows = 2 * t + N_EXPERTS * MOE_ROWS
    nb = n_rows // MOE_ROWS
    cnt = counts[:, 0].astype(jnp.int32)
    padded = (cnt + MOE_ROWS - 1) // MOE_ROWS * MOE_ROWS
    ends = jnp.cumsum(padded)
    pstart = ends - padded
    e1 = info[0].astype(jnp.int32)
    e2 = info[1].astype(jnp.int32)
    expert_ids = jnp.arange(N_EXPERTS, dtype=jnp.int32)[:, None]
    dest1 = jnp.sum(jnp.where(e1[None, :] == expert_ids, pstart[:, None], 0), axis=0) + info[4].astype(jnp.int32)
    dest2 = jnp.sum(jnp.where(e2[None, :] == expert_ids, pstart[:, None], 0), axis=0) + info[5].astype(jnp.int32)
    gates = jnp.stack([info[2], info[3]], axis=1)
    n_valid = (ends[-1] // MOE_ROWS).astype(jnp.int32).reshape(1)
    blk = jnp.minimum(jnp.arange(nb, dtype=jnp.int32), n_valid[0] - 1) * MOE_ROWS
    block_e = jnp.minimum(jnp.sum((ends[None, :] <= blk[:, None]).astype(jnp.int32), axis=1), N_EXPERTS - 1)
    zero_start = jnp.concatenate([jnp.maximum(ends - MOE_ROWS, 0), n_valid]).astype(jnp.int32)

    xb = _dispatch(h1, dest1, dest2, zero_start, n_rows)
    yb = _experts(xb, block_e, n_valid, w_gate, w_up, w_down, layer)
    return _combine(h1, yb, dest1, dest2, gates, ln_w, ln_b, alpha)


def _rotary_column_order(w_in):
    d = w_in.shape[0]
    nq = RET_HEADS * RET_DK

    def perm(w):
        return w.reshape(d, RET_HEADS, RET_DK // 2, 2).transpose(0, 3, 1, 2).reshape(d, nq)

    return jnp.concatenate([perm(w_in[:, :nq]), perm(w_in[:, nq:2 * nq]), w_in[:, 2 * nq:]], axis=1)


def kernel(x, w_in_even, w_out_even, lru_conv_w, lru_conv_b, lru_gate_w, lru_gate_b, lru_lambda,
           w_in_odd, w_out_odd, hg_lower_bounds, hg_norm_w, ln_w, ln_b, router_w,
           moe_w_gate, moe_w_up, moe_w_down):
    batch, seq_len, d = x.shape
    depth = ln_w.shape[0]
    alpha = (2.0 * depth) ** 0.25
    t = batch * seq_len
    h = x.reshape(t, d)
    router_wt = router_w.T
    half = RET_DK // 2
    inv_freq = ROPE_BASE ** (-jnp.arange(0, RET_DK, 2, dtype=F32) / RET_DK)
    inv_freq = jnp.tile(inv_freq, LANES // half).reshape(1, LANES)
    nret = RET_HEADS * RET_DV

    for layer in range(depth):
        j = layer // 2
        if layer % 2 == 0:
            w_in = _rotary_column_order(w_in_even[j]).astype(BF16)
            n_bf = 2 * RET_HEADS * RET_DK + 2 * nret
            proj_b, proj_f = _project(h, w_in, inv_freq, n_bf, rotary=True, seq_len=seq_len)
            ret = _retention(proj_b, batch, seq_len)
            h_fwd, h_bwd = _rglru(proj_f, lru_conv_w[j], lru_conv_b[j], lru_gate_w[j], lru_gate_b[j],
                                  lru_lambda[j], batch, seq_len)
            w_out = w_out_even[j].astype(BF16)
            mixer_inputs, weights = (ret, h_fwd, h_bwd, proj_f), [w_out[:nret], w_out[nret:]]
        else:
            w = w_in_odd[j]
            w_in = jnp.concatenate([w[:, :2 * d], w[:, 4 * d:], w[:, 2 * d:4 * d]], axis=1).astype(BF16)
            proj_b, proj_f = _project(h, w_in, inv_freq, 3 * d, rotary=False, seq_len=seq_len)
            o_fwd, o_bwd = _gla(proj_b, proj_f, hg_lower_bounds, layer, batch, seq_len)
            mixer_inputs, weights = (o_fwd, o_bwd, proj_b, hg_norm_w[j]), [w_out_odd[j].astype(BF16)]
        h1, info, counts = _out_project(layer % 2 == 0, mixer_inputs, weights, h, ln_w[layer, 0],
                                        ln_b[layer, 0], router_wt, alpha)
        h = _moe_fused(h1, info, counts, moe_w_gate, moe_w_up, moe_w_down, layer,
                       ln_w[layer, 1], ln_b[layer, 1], alpha)
    return h.reshape(batch, seq_len, d)
```

```python
import functools
import math

import jax
import jax.numpy as jnp
from jax import lax
from jax.experimental import pallas as pl
from jax.experimental.pallas import tpu as pltpu

F32 = jnp.float32
BF16 = jnp.bfloat16

RET_HEADS = 4
RET_DK = 64
RET_DV = 128
RET_CHUNK = 128
ROPE_BASE = 10000.0
LRU_BLOCKS = 4
LRU_BW = 128
LRU_CONV = 4
LRU_C = 8.0
HG_HEADS = 8
HG_DK = 128
HG_DV = 128
N_EXPERTS = 16
N_GROUPS = 4
EXPERTS_PER_GROUP = 4
LN_EPS = 1e-5
RMS_EPS = 1e-6

LANES = 128
SUBLANES = 8
PROJ_ROWS = 512
PROJ_COLS = 512
RET_CHUNKS_PER_STEP = 4
LRU_ROWS = 512
GLA_CHUNK = 32
GLA_ROWS = 256
TOKEN_ROWS = 512
SORT_ROWS = 256
MOE_ROWS = 256
VMEM_LIMIT = 48 * 1024 * 1024

_NT = (((1,), (1,)), ((), ()))
_TN = (((0,), (0,)), ((), ()))


def _params(sem):
    return pltpu.CompilerParams(dimension_semantics=sem, vmem_limit_bytes=VMEM_LIMIT)


def _sigmoid(x):
    return 0.5 * jnp.tanh(0.5 * x) + 0.5


def _silu(x):
    return x * _sigmoid(x)


def _softplus(x):
    return jnp.maximum(x, 0.0) + jnp.log1p(jnp.exp(-jnp.abs(x)))


def _layer_norm(u, w, b):
    mu = jnp.mean(u, axis=-1, keepdims=True)
    d = u - mu
    var = jnp.mean(d * d, axis=-1, keepdims=True)
    return d * lax.rsqrt(var + LN_EPS) * w + b


def _split3(x):
    hi = x.astype(BF16)
    rem = x - hi.astype(F32)
    mid = rem.astype(BF16)
    lo = (rem - mid.astype(F32)).astype(BF16)
    return hi, mid, lo


def _proj_kernel(x_ref, w_ref, inv_ref, ob_ref, of_ref, *, rotary, seq_len, tm, n_bf):
    i = pl.program_id(0)
    xb = x_ref[...].astype(BF16)
    tn = PROJ_COLS
    for j in range(w_ref.shape[1] // tn):
        acc = jnp.dot(xb, w_ref[:, j * tn:(j + 1) * tn], preferred_element_type=F32)
        if rotary and j == 0:
            row = lax.broadcasted_iota(jnp.int32, (tm, 1), 0) + i * tm
            ang = (row % seq_len).astype(F32) * inv_ref[...]
            cos = jnp.cos(ang)
            sin = jnp.sin(ang)
            scale = RET_DK ** -0.5
            q1, q2 = acc[:, 0:128], acc[:, 128:256]
            k1, k2 = acc[:, 256:384], acc[:, 384:512]
            ob_ref[:, 0:128] = ((q1 * cos - q2 * sin) * scale).astype(BF16)
            ob_ref[:, 128:256] = ((q1 * sin + q2 * cos) * scale).astype(BF16)
            ob_ref[:, 256:384] = (k1 * cos - k2 * sin).astype(BF16)
            ob_ref[:, 384:512] = (k1 * sin + k2 * cos).astype(BF16)
        elif (j + 1) * tn <= n_bf:
            ob_ref[:, j * tn:(j + 1) * tn] = acc.astype(BF16)
        else:
            of_ref[:, j * tn - n_bf:(j + 1) * tn - n_bf] = acc


def _project(x, w_bf16, inv_freq, n_bf, *, rotary, seq_len):
    t, k = x.shape
    n = w_bf16.shape[1]
    tm = min(PROJ_ROWS, seq_len)
    return pl.pallas_call(
        functools.partial(_proj_kernel, rotary=rotary, seq_len=seq_len, tm=tm, n_bf=n_bf),
        grid=(t // tm,),
        in_specs=[
            pl.BlockSpec((tm, k), lambda i: (i, 0)),
            pl.BlockSpec((k, n), lambda i: (0, 0)),
            pl.BlockSpec((1, LANES), lambda i: (0, 0)),
        ],
        out_specs=[pl.BlockSpec((tm, n_bf), lambda i: (i, 0)),
                   pl.BlockSpec((tm, n - n_bf), lambda i: (i, 0))],
        out_shape=[jax.ShapeDtypeStruct((t, n_bf), BF16),
                   jax.ShapeDtypeStruct((t, n - n_bf), F32)],
        compiler_params=_params(("arbitrary",)),
        name="in_proj",
    )(x, w_bf16, inv_freq)


def _ret_log_gamma(head):
    out = jnp.full(head.shape, math.log1p(-(2.0 ** -5.0)), F32)
    for h in range(1, RET_HEADS):
        out = jnp.where(head == h, math.log1p(-(2.0 ** (-5.0 - h))), out)
    return out


def _ret_lane_head():
    lane = lax.broadcasted_iota(jnp.int32, (1, 2 * LANES), 1)
    return (lane % LANES) // (RET_DK // 2)


def _ret_state_mask():
    shape = (RET_HEADS * RET_DV, 2 * LANES)
    row_head = lax.broadcasted_iota(jnp.int32, shape, 0) // RET_DV
    col_head = (lax.broadcasted_iota(jnp.int32, shape, 1) % LANES) // (RET_DK // 2)
    return row_head == col_head


def _ret_bstate_kernel(k_ref, v_ref, sb_ref, s_ref, *, cps):
    c = RET_CHUNK

    @pl.when(pl.program_id(1) == 0)
    def _():
        s_ref[...] = jnp.zeros_like(s_ref)

    lg = _ret_log_gamma(_ret_lane_head())
    idx = lax.broadcasted_iota(jnp.int32, (c, 1), 0).astype(F32)
    k_decay = jnp.exp(lg * idx)
    chunk_decay = jnp.exp(lg * float(c))
    mask = _ret_state_mask()
    for cc in reversed(range(cps)):
        rows = slice(cc * c, (cc + 1) * c)
        sb_ref[0, cc] = s_ref[...].astype(BF16)
        kb = (k_ref[rows, :] * k_decay).astype(BF16)
        upd = lax.dot_general(v_ref[rows, :], kb, _TN, preferred_element_type=F32)
        s_ref[...] = s_ref[...] * chunk_decay + jnp.where(mask, upd, 0.0)


def _ret_out_kernel(q_ref, k_ref, v_ref, g_ref, sb_ref, o_ref, s_ref, *, cps):
    c = RET_CHUNK

    @pl.when(pl.program_id(1) == 0)
    def _():
        s_ref[...] = jnp.zeros_like(s_ref)

    lane_head = _ret_lane_head()
    lg = _ret_log_gamma(lane_head)
    idx = lax.broadcasted_iota(jnp.int32, (c, 1), 0).astype(F32)
    q_decay_f = jnp.exp(lg * (idx + 1.0))
    q_decay_b = jnp.exp(lg * (float(c) - idx))
    k_decay = jnp.exp(lg * (float(c) - 1.0 - idx))
    chunk_decay = jnp.exp(lg * float(c))
    mask = _ret_state_mask()
    ii = lax.broadcasted_iota(jnp.int32, (c, c), 0)
    jj = lax.broadcasted_iota(jnp.int32, (c, c), 1)
    dist = jnp.abs(ii - jj).astype(F32)
    intra_decay = [jnp.exp(math.log1p(-(2.0 ** (-5.0 - h))) * dist) for h in range(RET_HEADS)]

    for cc in range(cps):
        rows = slice(cc * c, (cc + 1) * c)
        q = q_ref[rows, :]
        k = k_ref[rows, :]
        v = v_ref[rows, :]
        qf = (q * q_decay_f).astype(BF16)
        qb = (q * q_decay_b).astype(BF16)
        cross = (lax.dot_general(qf, s_ref[...].astype(BF16), _NT, preferred_element_type=F32)
                 + lax.dot_general(qb, sb_ref[0, cc], _NT, preferred_element_type=F32))
        for h in range(RET_HEADS):
            qh = jnp.where(lane_head == h, q, jnp.zeros_like(q))
            s = lax.dot_general(qh, k, _NT, preferred_element_type=F32) * intra_decay[h]
            cols = slice(h * RET_DV, (h + 1) * RET_DV)
            o = jnp.dot(s.astype(BF16), v[:, cols], preferred_element_type=F32) + cross[:, cols]
            mu = jnp.mean(o, axis=-1, keepdims=True)
            d = o - mu
            var = jnp.mean(d * d, axis=-1, keepdims=True)
            gate = _silu(g_ref[rows, cols].astype(F32))
            o_ref[rows, cols] = (gate * (d * lax.rsqrt(var + LN_EPS))).astype(BF16)
        kf = (k * k_decay).astype(BF16)
        upd = lax.dot_general(v, kf, _TN, preferred_element_type=F32)
        s_ref[...] = s_ref[...] * chunk_decay + jnp.where(mask, upd, 0.0)


def _retention(proj, batch, seq_len):
    t = proj.shape[0]
    c = RET_CHUNK
    cps = min(RET_CHUNKS_PER_STEP, seq_len // c)
    rows = cps * c
    ns = seq_len // rows
    dv = RET_HEADS * RET_DV
    state_shape = (dv, 2 * LANES)
    rev = lambda b, n: b * ns + (ns - 1 - n)
    fwd = lambda b, n: b * ns + n
    sb = pl.pallas_call(
        functools.partial(_ret_bstate_kernel, cps=cps),
        grid=(batch, ns),
        in_specs=[
            pl.BlockSpec((rows, 2 * LANES), lambda b, n: (rev(b, n), 1)),
            pl.BlockSpec((rows, dv), lambda b, n: (rev(b, n), 1)),
        ],
        out_specs=pl.BlockSpec((1, cps) + state_shape, lambda b, n: (b, ns - 1 - n, 0, 0)),
        out_shape=jax.ShapeDtypeStruct((batch, ns * cps) + state_shape, BF16),
        scratch_shapes=[pltpu.VMEM(state_shape, F32)],
        compiler_params=_params(("arbitrary", "arbitrary")),
        name="ret_bstate",
    )(proj, proj)
    return pl.pallas_call(
        functools.partial(_ret_out_kernel, cps=cps),
        grid=(batch, ns),
        in_specs=[
            pl.BlockSpec((rows, 2 * LANES), lambda b, n: (fwd(b, n), 0)),
            pl.BlockSpec((rows, 2 * LANES), lambda b, n: (fwd(b, n), 1)),
            pl.BlockSpec((rows, dv), lambda b, n: (fwd(b, n), 1)),
            pl.BlockSpec((rows, dv), lambda b, n: (fwd(b, n), 2)),
            pl.BlockSpec((1, cps) + state_shape, lambda b, n: (b, n, 0, 0)),
        ],
        out_specs=pl.BlockSpec((rows, dv), lambda b, n: (fwd(b, n), 0)),
        out_shape=jax.ShapeDtypeStruct((t, dv), BF16),
        scratch_shapes=[pltpu.VMEM(state_shape, F32)],
        compiler_params=_params(("arbitrary", "arbitrary")),
        name="ret_out",
    )(proj, proj, proj, proj, sb)


def _lru_kernel(xfp_ref, xf_ref, xfn_ref, xbp_ref, xb_ref, xbn_ref, cw_ref, cb_ref, gw_ref, gb_ref,
                lam_ref, hf_ref, hb_ref, xx_ref, a_ref, b_ref, h_ref, *, nt, ts, batch):
    i = pl.program_id(0)
    halo = SUBLANES
    lo = LRU_CONV // 2

    @pl.when(i == 0)
    def _():
        h_ref[...] = jnp.zeros_like(h_ref)

    def prepare(xp_ref, x_ref, xn_ref, tile, z, slot):
        for b in range(batch):
            xx_ref[0:halo, :] = jnp.where(tile == 0, 0.0, xp_ref[b])
            xx_ref[halo:halo + ts, :] = x_ref[b]
            xx_ref[halo + ts:2 * halo + ts, :] = jnp.where(tile == nt - 1, 0.0, xn_ref[b])
            xc = cb_ref[...]
            for tap in range(LRU_CONV):
                xc = xc + cw_ref[tap:tap + 1, :] * xx_ref[pl.ds(halo - lo + tap, ts), :]
            for n in range(LRU_BLOCKS):
                cols = slice(n * LRU_BW, (n + 1) * LRU_BW)
                xn = xc[:, cols]
                g = (jnp.dot(xn.astype(BF16), gw_ref[z, n], preferred_element_type=F32)
                     + gb_ref[z, n:n + 1, :])
                r = _sigmoid(g[:, :LRU_BW])
                ig = _sigmoid(g[:, LRU_BW:])
                a = jnp.exp((-LRU_C) * r * _softplus(-lam_ref[z, :, cols]))
                a_ref[slot + b, :, cols] = a
                b_ref[slot + b, :, cols] = jnp.sqrt(1.0 - a * a) * (ig * xn)

    prepare(xfp_ref, xf_ref, xfn_ref, i, 0, 0)
    prepare(xbp_ref, xb_ref, xbn_ref, nt - 1 - i, 1, batch)

    def step(s, hs):
        out = []
        for k in range(2 * batch):
            row = s if k < batch else ts - 1 - s
            h = a_ref[k, pl.ds(row, 1), :] * hs[k] + b_ref[k, pl.ds(row, 1), :]
            if k < batch:
                hf_ref[k, pl.ds(row, 1), :] = h
            else:
                hb_ref[k - batch, pl.ds(row, 1), :] = h
            out.append(h)
        return tuple(out)

    hs = lax.fori_loop(0, ts, step, tuple(h_ref[k] for k in range(2 * batch)), unroll=8)
    for k in range(2 * batch):
        h_ref[k] = hs[k]


def _rglru(proj_f32, conv_w, conv_b, gate_w, gate_b, lam, batch, seq_len):
    w = LRU_BLOCKS * LRU_BW
    ts = min(LRU_ROWS, seq_len)
    nt = seq_len // ts
    rows8 = ts // SUBLANES
    last8 = seq_len // SUBLANES - 1
    x3 = proj_f32.reshape(batch, seq_len, proj_f32.shape[1])
    gw = jnp.concatenate([gate_w[:, 0], gate_w[:, 1]], axis=-1).astype(BF16)
    gb = jnp.concatenate([gate_b[:, 0], gate_b[:, 1]], axis=-1)
    bwd = lambda i: nt - 1 - i

    def tile_specs(tile):
        return [
            pl.BlockSpec((batch, SUBLANES, w), lambda i: (0, jnp.maximum(tile(i) * rows8 - 1, 0), 0)),
            pl.BlockSpec((batch, ts, w), lambda i: (0, tile(i), 0)),
            pl.BlockSpec((batch, SUBLANES, w), lambda i: (0, jnp.minimum((tile(i) + 1) * rows8, last8), 0)),
        ]

    full = lambda a: pl.BlockSpec(a.shape, lambda i: (0,) * a.ndim)
    cb = conv_b.reshape(1, w)
    lam3 = lam.reshape(2, 1, w)
    state = jax.ShapeDtypeStruct((batch, seq_len, w), F32)
    h_fwd, h_bwd = pl.pallas_call(
        functools.partial(_lru_kernel, nt=nt, ts=ts, batch=batch),
        grid=(nt,),
        in_specs=tile_specs(lambda i: i) + tile_specs(bwd) + [full(conv_w), full(cb), full(gw), full(gb),
                                                               full(lam3)],
        out_specs=[pl.BlockSpec((batch, ts, w), lambda i: (0, i, 0)),
                   pl.BlockSpec((batch, ts, w), lambda i: (0, bwd(i), 0))],
        out_shape=[state, state],
        scratch_shapes=[pltpu.VMEM((ts + 2 * SUBLANES, w), F32), pltpu.VMEM((2 * batch, ts, w), F32),
                        pltpu.VMEM((2 * batch, ts, w), F32), pltpu.VMEM((2 * batch, 1, w), F32)],
        compiler_params=_params(("arbitrary",)),
        name="lru_scan",
    )(x3, x3, x3, x3, x3, x3, conv_w, cb, gw, gb, lam3)
    return h_fwd.reshape(batch * seq_len, w), h_bwd.reshape(batch * seq_len, w)


def _gla_kernel(qf_ref, vf_ref, zf_ref, qb_ref, vb_ref, zb_ref, lbp_ref, of_ref, ob_ref, s_ref,
                *, ts, layer, batch):
    c = GLA_CHUNK
    nchunks = ts // c

    @pl.when(pl.program_id(0) == 0)
    def _():
        s_ref[...] = jnp.zeros_like(s_ref)

    p = lbp_ref[...]
    e = jnp.exp(p - jnp.max(p, axis=0, keepdims=True))
    sm = e / jnp.sum(e, axis=0, keepdims=True)
    lb = jnp.zeros((1, p.shape[1]), F32)
    for r in range(1, layer + 1):
        lb = lb + sm[r:r + 1, :]

    head_cols = [slice(h * HG_DK, (h + 1) * HG_DK) for h in range(HG_HEADS)]
    hc = HG_HEADS * c
    ii = lax.broadcasted_iota(jnp.int32, (c, c), 0)
    jj = lax.broadcasted_iota(jnp.int32, (c, c), 1)
    si = lax.broadcasted_iota(jnp.int32, (hc, hc), 0)
    sj = lax.broadcasted_iota(jnp.int32, (hc, hc), 1)
    same_head = si // c == sj // c
    mid = c // 2

    def stack(a):
        return jnp.concatenate([a[:, cols] for cols in head_cols], axis=0)

    tri = {False: (jj <= ii).astype(BF16), True: (jj >= ii).astype(BF16)}
    keep = {False: jnp.logical_and(same_head, sj <= si), True: jnp.logical_and(same_head, sj >= si)}
    chains = ([(qf_ref, vf_ref, zf_ref, of_ref, b, False) for b in range(batch)]
              + [(qb_ref, vb_ref, zb_ref, ob_ref, b, True) for b in range(batch)])

    for cc in range(nchunks):
        work = []
        for q_ref, v_ref, z_ref, o_ref, b, reverse in chains:
            rows = pl.ds(((nchunks - 1 - cc) if reverse else cc) * c, c)
            sig = _sigmoid(z_ref[b, rows, :])
            log_f = jnp.log(lb + (1.0 - lb) * sig)
            work.append(dict(rows=rows, reverse=reverse, o_ref=o_ref, b=b,
                             key=(1.0 - lb) * (1.0 - sig),
                             qs=_silu(q_ref[b, rows, :].astype(F32)) * (HG_DK ** -0.5),
                             v=v_ref[b, rows, :], split=_split3(log_f)))
        for w in work:
            f_hi, f_mid, f_lo = w["split"]
            t = tri[w["reverse"]]
            w["bcum"] = (jnp.dot(t, f_lo, preferred_element_type=F32)
                         + jnp.dot(t, f_mid, preferred_element_type=F32)
                         + jnp.dot(t, f_hi, preferred_element_type=F32))
        for w in work:
            bcum = w["bcum"]
            end = 0 if w["reverse"] else c - 1
            ref_row = bcum[mid:mid + 1, :]
            b_end = bcum[end:end + 1, :]
            w["qe"] = (w["qs"] * jnp.exp(bcum - ref_row)).astype(BF16)
            w["ke"] = (w["key"] * jnp.exp(ref_row - bcum)).astype(BF16)
            w["qd"] = (w["qs"] * jnp.exp(bcum)).astype(BF16)
            w["kd"] = (w["key"] * jnp.exp(b_end - bcum)).astype(BF16)
            w["decay"] = jnp.exp(b_end)
        for w in work:
            w["att"] = lax.dot_general(stack(w["qe"]), stack(w["ke"]), _NT, preferred_element_type=F32)
        for w in work:
            att = jnp.where(keep[w["reverse"]], w["att"], 0.0).astype(BF16)
            w["intra"] = jnp.dot(att, stack(w["v"]), preferred_element_type=F32)
        for k, w in enumerate(work):
            w["st"] = [s_ref[k, h] for h in range(HG_HEADS)]
            w["inter"] = [lax.dot_general(w["qd"][:, cols], w["st"][h].astype(BF16), _NT,
                                          preferred_element_type=F32) for h, cols in enumerate(head_cols)]
        for w in work:
            w["upd"] = [lax.dot_general(w["v"][:, cols], w["kd"][:, cols], _TN, preferred_element_type=F32)
                        for cols in head_cols]
        for k, w in enumerate(work):
            for h, cols in enumerate(head_cols):
                w["o_ref"][w["b"], w["rows"], cols] = w["intra"][h * c:(h + 1) * c, :] + w["inter"][h]
                s_ref[k, h] = w["st"][h] * w["decay"][:, cols] + w["upd"][h]


def _gla(proj_bf16, proj_f32, lower_bounds, layer, batch, seq_len):
    d = HG_HEADS * HG_DK
    ts = min(GLA_ROWS, seq_len)
    nt = seq_len // ts
    depth = lower_bounds.shape[0]
    pb = proj_bf16.reshape(batch, seq_len, proj_bf16.shape[1])
    pf = proj_f32.reshape(batch, seq_len, proj_f32.shape[1])
    bwd = lambda i: nt - 1 - i
    blk = lambda tile, col: pl.BlockSpec((batch, ts, d), lambda i: (0, tile(i), col))
    fwd = lambda i: i
    out = jax.ShapeDtypeStruct((batch, seq_len, d), F32)
    o_fwd, o_bwd = pl.pallas_call(
        functools.partial(_gla_kernel, ts=ts, layer=layer, batch=batch),
        grid=(nt,),
        in_specs=[blk(fwd, 0), blk(fwd, 1), blk(fwd, 0), blk(bwd, 0), blk(bwd, 1), blk(bwd, 1),
                  pl.BlockSpec((depth, d), lambda i: (0, 0))],
        out_specs=[blk(fwd, 0), blk(bwd, 0)],
        out_shape=[out, out],
        scratch_shapes=[pltpu.VMEM((2 * batch, HG_HEADS, HG_DV, HG_DK), F32)],
        compiler_params=_params(("arbitrary",)),
        name="gla_scan",
    )(pb, pb, pf, pb, pb, pf, lower_bounds)
    return o_fwd.reshape(batch * seq_len, d), o_bwd.reshape(batch * seq_len, d)


def _top2(p):
    v1 = jnp.maximum(jnp.maximum(p[0], p[1]), jnp.maximum(p[2], p[3]))
    i1 = jnp.where(p[0] == v1, 0, jnp.where(p[1] == v1, 1, jnp.where(p[2] == v1, 2, 3)))
    q = [jnp.where(i1 == k, -1.0, p[k]) for k in range(4)]
    v2 = jnp.maximum(jnp.maximum(q[0], q[1]), jnp.maximum(q[2], q[3]))
    i2 = jnp.where(q[0] == v2, 0, jnp.where(q[1] == v2, 1, jnp.where(q[2] == v2, 2, 3)))
    return v1, i1, v2, i2


def _outproj_kernel(*refs, even, alpha, tm):
    if even:
        ret_ref, hf_ref, hb_ref, gr_ref, w0_ref, w1_ref = refs[:6]
        rest = refs[6:]
        lru = ((hf_ref[...] + hb_ref[...]) * jax.nn.gelu(gr_ref[...])).astype(BF16)
        y = (jnp.dot(ret_ref[...], w0_ref[...], preferred_element_type=F32)
             + jnp.dot(lru, w1_ref[...], preferred_element_type=F32))
    else:
        of_ref, ob_ref, g_ref, nw_ref, w0_ref = refs[:5]
        rest = refs[5:]
        o = of_ref[...] + ob_ref[...]
        ms = jnp.mean(o * o, axis=-1, keepdims=True)
        mix = o * lax.rsqrt(ms + RMS_EPS) * nw_ref[...] * _silu(g_ref[...].astype(F32))
        y = jnp.dot(mix.astype(BF16), w0_ref[...], preferred_element_type=F32)
    h_ref, lnw_ref, lnb_ref, rw_ref, o_ref, info_ref, cnt_ref, carry_ref = rest
    i = pl.program_id(0)

    @pl.when(i == 0)
    def _():
        carry_ref[...] = jnp.zeros_like(carry_ref)

    h1 = _layer_norm(alpha * h_ref[...] + y, lnw_ref[...], lnb_ref[...])
    o_ref[...] = h1

    h_hi = h1.astype(BF16)
    h_lo = (h1 - h_hi.astype(F32)).astype(BF16)
    rw = rw_ref[...]
    r_hi = rw.astype(BF16)
    r_lo = (rw - r_hi.astype(F32)).astype(BF16)
    logits = (lax.dot_general(r_lo, h_hi, _NT, preferred_element_type=F32)
              + lax.dot_general(r_hi, h_lo, _NT, preferred_element_type=F32)
              + lax.dot_general(r_hi, h_hi, _NT, preferred_element_type=F32))
    ex = jnp.exp(logits - jnp.max(logits, axis=0, keepdims=True))
    probs = ex / jnp.sum(ex, axis=0, keepdims=True)
    best = None
    for g in range(N_GROUPS):
        rows = [probs[g * EXPERTS_PER_GROUP + k:g * EXPERTS_PER_GROUP + k + 1, :]
                for k in range(EXPERTS_PER_GROUP)]
        v1, i1, v2, i2 = _top2(rows)
        cand = (v1 + v2, v1, i1 + g * EXPERTS_PER_GROUP, v2, i2 + g * EXPERTS_PER_GROUP)
        if best is None:
            best = cand
        else:
            take = cand[0] > best[0]
            best = tuple(jnp.where(take, cn, bs) for cn, bs in zip(cand, best))
    _, v1, e1, v2, e2 = best
    denom = v1 + v2
    g1 = v1 / denom
    g2 = v2 / denom

    eid = lax.broadcasted_iota(jnp.int32, (N_EXPERTS, tm), 0)
    oh1 = (eid == e1).astype(F32)
    oh2 = (eid == e2).astype(F32)
    oh = oh1 + oh2
    tt = lax.broadcasted_iota(jnp.int32, (tm, tm), 0)
    uu = lax.broadcasted_iota(jnp.int32, (tm, tm), 1)
    before = (tt < uu).astype(BF16)
    base = carry_ref[:, 0:1] + jnp.dot(oh.astype(BF16), before, preferred_element_type=F32)
    rank1 = jnp.sum(oh1 * base, axis=0, keepdims=True)
    rank2 = jnp.sum(oh2 * base, axis=0, keepdims=True)
    carry_ref[...] = carry_ref[...] + jnp.sum(oh, axis=1, keepdims=True)
    cnt_ref[...] = carry_ref[...]
    zero = jnp.zeros_like(g1)
    info_ref[...] = jnp.concatenate(
        [e1.astype(F32), e2.astype(F32), g1, g2, rank1, rank2, zero, zero], axis=0)


def _out_project(even, mixer_inputs, weights_bf16, h, ln_w, ln_b, router_wt, alpha):
    t, d = h.shape
    tm = TOKEN_ROWS
    row = lambda i: (i, 0)
    const = lambda i: (0, 0)
    if even:
        ret, h_fwd, h_bwd, proj_f32 = mixer_inputs
        w = h_fwd.shape[1]
        in_specs = [pl.BlockSpec((tm, ret.shape[1]), row), pl.BlockSpec((tm, w), row),
                    pl.BlockSpec((tm, w), row), pl.BlockSpec((tm, w), lambda i: (i, 1))]
        args = [ret, h_fwd, h_bwd, proj_f32]
    else:
        o_fwd, o_bwd, proj_bf16, norm_w = mixer_inputs
        in_specs = [pl.BlockSpec((tm, d), row), pl.BlockSpec((tm, d), row),
                    pl.BlockSpec((tm, d), lambda i: (i, 2)), pl.BlockSpec((1, d), const)]
        args = [o_fwd, o_bwd, proj_bf16, norm_w.reshape(1, d)]
    in_specs += [pl.BlockSpec(wm.shape, const) for wm in weights_bf16]
    in_specs += [pl.BlockSpec((tm, d), row), pl.BlockSpec((1, d), const), pl.BlockSpec((1, d), const),
                 pl.BlockSpec((N_EXPERTS, d), const)]
    args += list(weights_bf16) + [h, ln_w.reshape(1, d), ln_b.reshape(1, d), router_wt]
    return pl.pallas_call(
        functools.partial(_outproj_kernel, even=even, alpha=alpha, tm=tm),
        grid=(t // tm,),
        in_specs=in_specs,
        out_specs=[pl.BlockSpec((tm, d), row),
                   pl.BlockSpec((SUBLANES, tm), lambda i: (0, i)),
                   pl.BlockSpec((N_EXPERTS, LANES), const)],
        out_shape=[jax.ShapeDtypeStruct((t, d), F32),
                   jax.ShapeDtypeStruct((SUBLANES, t), F32),
                   jax.ShapeDtypeStruct((N_EXPERTS, LANES), F32)],
        scratch_shapes=[pltpu.VMEM((N_EXPERTS, LANES), F32)],
        compiler_params=_params(("arbitrary",)),
        name="out_proj_router",
    )(*args)


def _row_copy(src_ref, src_row, dst_ref, dst_row, sem):
    return pltpu.make_async_copy(src_ref.at[pl.ds(src_row, 1)], dst_ref.at[pl.ds(dst_row, 1)], sem)


def _dispatch_kernel(d1_ref, d2_ref, zs_ref, h_ref, xb_ref, zero_ref, sem, *, tm):
    i = pl.program_id(0)
    base = i * tm

    @pl.when(i == 0)
    def _():
        zero_ref[...] = jnp.zeros_like(zero_ref)

        def clear(row):
            start = pl.multiple_of(row, MOE_ROWS)
            return pltpu.make_async_copy(zero_ref, xb_ref.at[pl.ds(start, MOE_ROWS)], sem.at[0])

        for e in range(N_EXPERTS):
            clear(zs_ref[e]).start()
        for e in range(N_EXPERTS):
            clear(zs_ref[e]).wait()

        def clear_tail(b, carry):
            clear(b * MOE_ROWS).start()
            clear(b * MOE_ROWS).wait()
            return carry

        lax.fori_loop(zs_ref[N_EXPERTS], xb_ref.shape[0] // MOE_ROWS, clear_tail, 0)

    def start(t, carry):
        _row_copy(h_ref, t, xb_ref, d1_ref[base + t], sem.at[0]).start()
        _row_copy(h_ref, t, xb_ref, d2_ref[base + t], sem.at[1]).start(priority=1)
        return carry

    lax.fori_loop(0, tm, start, 0, unroll=8)
    pltpu.make_async_copy(h_ref, xb_ref.at[pl.ds(0, tm)], sem.at[0]).wait()
    pltpu.make_async_copy(h_ref, xb_ref.at[pl.ds(0, tm)], sem.at[1]).wait()


def _dispatch(h, dest1, dest2, zero_start, n_rows):
    t, d = h.shape
    tm = TOKEN_ROWS
    return pl.pallas_call(
        functools.partial(_dispatch_kernel, tm=tm),
        grid_spec=pltpu.PrefetchScalarGridSpec(
            num_scalar_prefetch=3,
            grid=(t // tm,),
            in_specs=[pl.BlockSpec((tm, d), lambda i, d1, d2, zs: (i, 0))],
            out_specs=pl.BlockSpec(memory_space=pl.ANY),
            scratch_shapes=[pltpu.VMEM((MOE_ROWS, d), F32), pltpu.SemaphoreType.DMA((2,))],
        ),
        out_shape=jax.ShapeDtypeStruct((n_rows, d), F32),
        compiler_params=_params(("arbitrary",)),
        name="moe_dispatch",
    )(dest1, dest2, zero_start, h)


def _expert_kernel(be_ref, nv_ref, x_ref, wg_ref, wu_ref, wd_ref, o_ref, wgb_ref, wub_ref, wdb_ref):
    i = pl.program_id(0)
    changed = jnp.logical_or(i == 0, be_ref[i] != be_ref[jnp.maximum(i - 1, 0)])

    @pl.when(changed)
    def _():
        wgb_ref[...] = wg_ref[...].astype(BF16)
        wub_ref[...] = wu_ref[...].astype(BF16)
        wdb_ref[...] = wd_ref[...].astype(BF16)

    @pl.when(i < nv_ref[0])
    def _():
        x = x_ref[...].astype(BF16)
        gate = jnp.dot(x, wgb_ref[...], preferred_element_type=F32)
        up = jnp.dot(x, wub_ref[...], preferred_element_type=F32)
        hid = (_silu(gate) * up).astype(BF16)
        o_ref[...] = jnp.dot(hid, wdb_ref[...], preferred_element_type=F32)

    @pl.when(i >= nv_ref[0])
    def _():
        o_ref[...] = jnp.zeros_like(o_ref)


def _experts(xb, block_e, n_valid, w_gate, w_up, w_down, layer):
    p, d = xb.shape
    de = w_gate.shape[3]
    nb = p // MOE_ROWS
    return pl.pallas_call(
        _expert_kernel,
        grid_spec=pltpu.PrefetchScalarGridSpec(
            num_scalar_prefetch=2,
            grid=(nb,),
            in_specs=[
                pl.BlockSpec((MOE_ROWS, d), lambda i, be, nv: (jnp.maximum(jnp.minimum(i, nv[0] - 1), 0), 0)),
                pl.BlockSpec((None, None, d, de), lambda i, be, nv: (layer, be[i], 0, 0)),
                pl.BlockSpec((None, None, d, de), lambda i, be, nv: (layer, be[i], 0, 0)),
                pl.BlockSpec((None, None, de, d), lambda i, be, nv: (layer, be[i], 0, 0)),
            ],
            out_specs=pl.BlockSpec((MOE_ROWS, d), lambda i, be, nv: (i, 0)),
            scratch_shapes=[pltpu.VMEM((d, de), BF16), pltpu.VMEM((d, de), BF16),
                            pltpu.VMEM((de, d), BF16)],
        ),
        out_shape=jax.ShapeDtypeStruct((p, d), F32),
        compiler_params=_params(("arbitrary",)),
        name="moe_experts",
    )(block_e, n_valid, xb, w_gate, w_up, w_down)


def _combine_kernel(d1_ref, d2_ref, h_ref, gates_ref, lnw_ref, lnb_ref, yb_ref, o_ref, r_ref, sem,
                    *, tm, alpha):
    base = pl.program_id(0) * tm

    def start(t, carry):
        _row_copy(yb_ref, d1_ref[base + t], r_ref.at[0], t, sem.at[0]).start()
        _row_copy(yb_ref, d2_ref[base + t], r_ref.at[1], t, sem.at[1]).start(priority=1)
        return carry

    lax.fori_loop(0, tm, start, 0, unroll=8)
    pltpu.make_async_copy(yb_ref.at[pl.ds(0, tm)], r_ref.at[0], sem.at[0]).wait()
    pltpu.make_async_copy(yb_ref.at[pl.ds(0, tm)], r_ref.at[1], sem.at[1]).wait()
    y = gates_ref[:, 0:1] * r_ref[0] + gates_ref[:, 1:2] * r_ref[1]
    o_ref[...] = _layer_norm(alpha * h_ref[...] + y, lnw_ref[...], lnb_ref[...])


def _combine(h, yb, dest1, dest2, gates, ln_w, ln_b, alpha):
    t, d = h.shape
    tm = TOKEN_ROWS
    return pl.pallas_call(
        functools.partial(_combine_kernel, tm=tm, alpha=alpha),
        grid_spec=pltpu.PrefetchScalarGridSpec(
            num_scalar_prefetch=2,
            grid=(t // tm,),
            in_specs=[pl.BlockSpec((tm, d), lambda i, d1, d2: (i, 0)),
                      pl.BlockSpec((tm, 2), lambda i, d1, d2: (i, 0)),
                      pl.BlockSpec((1, d), lambda i, d1, d2: (0, 0)),
                      pl.BlockSpec((1, d), lambda i, d1, d2: (0, 0)),
                      pl.BlockSpec(memory_space=pl.ANY)],
            out_specs=pl.BlockSpec((tm, d), lambda i, d1, d2: (i, 0)),
            scratch_shapes=[pltpu.VMEM((2, tm, d), F32), pltpu.SemaphoreType.DMA((2,))],
        ),
        out_shape=jax.ShapeDtypeStruct((t, d), F32),
        compiler_params=_params(("arbitrary",)),
        name="moe_combine",
    )(dest1, dest2, h, gates, ln_w.reshape(1, d), ln_b.reshape(1, d), yb)


def _invert_kernel(d1_ref, d2_ref, init_ref, ids_ref, sem, *, t):
    fill = pltpu.make_async_copy(init_ref, ids_ref, sem)
    fill.start()
    fill.wait()

    def body(tok, carry):
        ids_ref[d1_ref[tok]] = tok
        ids_ref[d2_ref[tok]] = tok + t
        return carry

    lax.fori_loop(0, t, body, 0, unroll=8)


def _invert(dest1, dest2, n_rows):
    t = dest1.shape[0]
    smem = pl.BlockSpec(memory_space=pltpu.SMEM)
    return pl.pallas_call(
        functools.partial(_invert_kernel, t=t),
        in_specs=[smem, smem, pl.BlockSpec(memory_space=pl.ANY)],
        out_specs=smem,
        out_shape=jax.ShapeDtypeStruct((n_rows,), jnp.int32),
        scratch_shapes=[pltpu.SemaphoreType.DMA(())],
        name="moe_invert",
    )(dest1, dest2, jnp.full((n_rows,), -1, jnp.int32))


def _expert_fused_kernel(be_ref, src_ref, dst_ref, h_ref, wg_ref, wu_ref, wd_ref, yk_ref,
                         xbuf, ybuf, wgb_ref, wub_ref, wdb_ref, sem_g, sem_s):
    i = pl.program_id(0)
    nb = pl.num_programs(0)
    slot = i % 2
    other = 1 - slot

    def gather(block, buf):
        base = block * MOE_ROWS
        for r in range(MOE_ROWS):
            _row_copy(h_ref, src_ref[base + r], xbuf.at[buf], r, sem_g.at[buf]).start()

    def scatter(block, buf):
        base = block * MOE_ROWS
        for r in range(MOE_ROWS):
            _row_copy(ybuf.at[buf], r, yk_ref, dst_ref[base + r], sem_s.at[buf]).start(priority=1)

    def wait_gather(buf):
        pltpu.make_async_copy(h_ref.at[pl.ds(0, MOE_ROWS)], xbuf.at[buf], sem_g.at[buf]).wait()

    def wait_scatter(buf):
        pltpu.make_async_copy(ybuf.at[buf], yk_ref.at[pl.ds(0, MOE_ROWS)], sem_s.at[buf]).wait()

    @pl.when(i == 0)
    def _():
        ybuf[...] = jnp.zeros_like(ybuf)
        gather(0, 0)

    wait_gather(slot)

    @pl.when(i >= 1)
    def _():
        wait_scatter(slot)

    changed = jnp.logical_or(i == 0, be_ref[i] != be_ref[jnp.maximum(i - 1, 0)])

    @pl.when(changed)
    def _():
        wgb_ref[...] = wg_ref[...].astype(BF16)
        wub_ref[...] = wu_ref[...].astype(BF16)
        wdb_ref[...] = wd_ref[...].astype(BF16)

    gather(jnp.minimum(i + 1, nb - 1), other)
    scatter(jnp.maximum(i - 1, 0), other)
    x = xbuf[slot].astype(BF16)
    gate = jnp.dot(x, wgb_ref[...], preferred_element_type=F32)
    up = jnp.dot(x, wub_ref[...], preferred_element_type=F32)
    hid = (_silu(gate) * up).astype(BF16)
    ybuf[slot] = jnp.dot(hid, wdb_ref[...], preferred_element_type=F32)

    @pl.when(i == nb - 1)
    def _():
        scatter(i, slot)
        wait_gather(other)
        wait_scatter(other)
        wait_scatter(slot)


def _experts_fused(h, src_tok, dst_row, block_e, w_gate, w_up, w_down, layer):
    t, d = h.shape
    de = w_gate.shape[3]
    p = src_tok.shape[0]
    nb = p // MOE_ROWS
    weight = lambda shape: pl.BlockSpec((None, None) + shape, lambda i, be, src, dst: (layer, be[i], 0, 0))
    return pl.pallas_call(
        _expert_fused_kernel,
        grid_spec=pltpu.PrefetchScalarGridSpec(
            num_scalar_prefetch=3,
            grid=(nb,),
            in_specs=[pl.BlockSpec(memory_space=pl.ANY), weight((d, de)), weight((d, de)), weight((de, d))],
            out_specs=pl.BlockSpec(memory_space=pl.ANY),
            scratch_shapes=[pltpu.VMEM((2, MOE_ROWS, d), F32), pltpu.VMEM((2, MOE_ROWS, d), F32),
                            pltpu.VMEM((d, de), BF16), pltpu.VMEM((d, de), BF16), pltpu.VMEM((de, d), BF16),
                            pltpu.SemaphoreType.DMA((2,)), pltpu.SemaphoreType.DMA((2,))],
        ),
        out_shape=jax.ShapeDtypeStruct((p, d), F32),
        compiler_params=_params(("arbitrary",)),
        name="moe_experts",
    )(block_e, src_tok, dst_row, h, w_gate, w_up, w_down)


def _combine_dense_kernel(h_ref, y0_ref, y1_ref, gates_ref, lnw_ref, lnb_ref, o_ref, *, alpha):
    y = gates_ref[:, 0:1] * y0_ref[...] + gates_ref[:, 1:2] * y1_ref[...]
    o_ref[...] = _layer_norm(alpha * h_ref[...] + y, lnw_ref[...], lnb_ref[...])


def _combine_dense(h, yk, gates, ln_w, ln_b, alpha):
    t, d = h.shape
    tm = TOKEN_ROWS
    nt = t // tm
    row = lambda i: (i, 0)
    const = lambda i: (0, 0)
    return pl.pallas_call(
        functools.partial(_combine_dense_kernel, alpha=alpha),
        grid=(nt,),
        in_specs=[pl.BlockSpec((tm, d), row), pl.BlockSpec((tm, d), row),
                  pl.BlockSpec((tm, d), lambda i: (nt + i, 0)), pl.BlockSpec((tm, 2), row),
                  pl.BlockSpec((1, d), const), pl.BlockSpec((1, d), const)],
        out_specs=pl.BlockSpec((tm, d), row),
        out_shape=jax.ShapeDtypeStruct((t, d), F32),
        compiler_params=_params(("arbitrary",)),
        name="moe_combine",
    )(h, yk, yk, gates, ln_w.reshape(1, d), ln_b.reshape(1, d))


def _moe_fused(h1, info, counts, w_gate, w_up, w_down, layer, ln_w, ln_b, alpha):
    t = h1.shape[0]
    n_rows = 2 * t + N_EXPERTS * MOE_ROWS
    nb = n_rows // MOE_ROWS
    cnt = counts[:, 0].astype(jnp.int32)
    padded = (cnt + MOE_ROWS - 1) // MOE_ROWS * MOE_ROWS
    ends = jnp.cumsum(padded)
    pstart = ends - padded
    e1 = info[0].astype(jnp.int32)
    e2 = info[1].astype(jnp.int32)
    expert_ids = jnp.arange(N_EXPERTS, dtype=jnp.int32)[:, None]
    dest1 = jnp.sum(jnp.where(e1[None, :] == expert_ids, pstart[:, None], 0), axis=0) + info[4].astype(jnp.int32)
    dest2 = jnp.sum(jnp.where(e2[None, :] == expert_ids, pstart[:, None], 0), axis=0) + info[5].astype(jnp.int32)
    gates = jnp.stack([info[2], info[3]], axis=1)
    blk = jnp.arange(nb, dtype=jnp.int32) * MOE_ROWS
    block_e = jnp.minimum(jnp.sum((ends[None, :] <= blk[:, None]).astype(jnp.int32), axis=1), N_EXPERTS - 1)
    ids = _invert(dest1, dest2, n_rows)
    is_pad = ids < 0
    src_tok = jnp.where(is_pad, 0, ids % t)
    dst_row = jnp.where(is_pad, 2 * t - 1 + jnp.cumsum(is_pad.astype(jnp.int32)), ids)
    yk = _experts_fused(h1, src_tok, dst_row, block_e, w_gate, w_up, w_down, layer)
    return _combine_dense(h1, yk, gates, ln_w, ln_b, alpha)


def _clear_padding_blocks(zs_ref, xb_ref, zero_ref, sem):
    zero_ref[...] = jnp.zeros_like(zero_ref)

    def clear(row):
        start = pl.multiple_of(row, MOE_ROWS)
        return pltpu.make_async_copy(zero_ref, xb_ref.at[pl.ds(start, MOE_ROWS)], sem)

    for e in range(N_EXPERTS):
        clear(zs_ref[e]).start()
    for e in range(N_EXPERTS):
        clear(zs_ref[e]).wait()

    def clear_tail(b, carry):
        clear(b * MOE_ROWS).start()
        clear(b * MOE_ROWS).wait()
        return carry

    lax.fori_loop(zs_ref[N_EXPERTS], xb_ref.shape[0] // MOE_ROWS, clear_tail, 0)


def _run_copy(src_ref, src_row, dst_ref, dst_row, sem):
    src = pl.multiple_of(src_row, SUBLANES)
    dst = pl.multiple_of(dst_row, SUBLANES)
    return pltpu.make_async_copy(src_ref.at[pl.ds(src, SUBLANES)], dst_ref.at[pl.ds(dst, SUBLANES)], sem)


def _sort_dispatch_kernel(dst_ref, nch_ref, zs_ref, h_ref, lp_ref, xb_ref, xs_ref, zero_ref, sem, *, tm, nck):
    i = pl.program_id(0)

    @pl.when(i == 0)
    def _():
        _clear_padding_blocks(zs_ref, xb_ref, zero_ref, sem)

    lp = lp_ref[...]
    pos = lax.broadcasted_iota(jnp.int32, (xs_ref.shape[0], tm), 0)
    perm = jnp.logical_or(pos == lp[0:1, :], pos == lp[1:2, :]).astype(BF16)
    xs_ref[...] = jnp.dot(perm, h_ref[...].astype(BF16), preferred_element_type=F32)

    n = nch_ref[i]

    def start(j, carry):
        _run_copy(xs_ref, j * SUBLANES, xb_ref, dst_ref[i * nck + j], sem).start()
        return carry

    def wait(j, carry):
        _run_copy(xs_ref, 0, xb_ref, 0, sem).wait()
        return carry

    lax.fori_loop(0, n, start, 0)
    lax.fori_loop(0, n, wait, 0)


def _sort_dispatch(h, lp_rows, chunk_dst, n_chunks, zero_start, n_rows, tm, sorted_rows):
    t, d = h.shape
    nck = sorted_rows // SUBLANES
    return pl.pallas_call(
        functools.partial(_sort_dispatch_kernel, tm=tm, nck=nck),
        grid_spec=pltpu.PrefetchScalarGridSpec(
            num_scalar_prefetch=3,
            grid=(t // tm,),
            in_specs=[pl.BlockSpec((tm, d), lambda i, a, b, c: (i, 0)),
                      pl.BlockSpec((2, tm), lambda i, a, b, c: (0, i))],
            out_specs=pl.BlockSpec(memory_space=pl.ANY),
            scratch_shapes=[pltpu.VMEM((sorted_rows, d), F32), pltpu.VMEM((MOE_ROWS, d), F32),
                            pltpu.SemaphoreType.DMA(())],
        ),
        out_shape=jax.ShapeDtypeStruct((n_rows, d), F32),
        compiler_params=_params(("arbitrary",)),
        name="moe_dispatch",
    )(chunk_dst, n_chunks, zero_start, h, lp_rows)


def _sort_combine_kernel(src_ref, nch_ref, h_ref, lp_ref, gates_ref, lnw_ref, lnb_ref, yb_ref, o_ref,
                         ys_ref, sem, *, tm, nck, alpha):
    i = pl.program_id(0)

    @pl.when(i == 0)
    def _():
        ys_ref[...] = jnp.zeros_like(ys_ref)

    n = nch_ref[i]

    def start(j, carry):
        _run_copy(yb_ref, src_ref[i * nck + j], ys_ref, j * SUBLANES, sem).start()
        return carry

    def wait(j, carry):
        _run_copy(yb_ref, 0, ys_ref, 0, sem).wait()
        return carry

    lax.fori_loop(0, n, start, 0)
    lax.fori_loop(0, n, wait, 0)

    lp = lp_ref[...]
    pos = lax.broadcasted_iota(jnp.int32, (tm, ys_ref.shape[0]), 1)
    ys = ys_ref[...].astype(BF16)
    y1 = jnp.dot((pos == lp[:, 0:1]).astype(BF16), ys, preferred_element_type=F32)
    y2 = jnp.dot((pos == lp[:, 1:2]).astype(BF16), ys, preferred_element_type=F32)
    y = gates_ref[:, 0:1] * y1 + gates_ref[:, 1:2] * y2
    o_ref[...] = _layer_norm(alpha * h_ref[...] + y, lnw_ref[...], lnb_ref[...])


def _sort_combine(h, yb, lp_cols, gates, chunk_src, n_chunks, ln_w, ln_b, alpha, tm, sorted_rows):
    t, d = h.shape
    nck = sorted_rows // SUBLANES
    row = lambda i, a, b: (i, 0)
    const = lambda i, a, b: (0, 0)
    return pl.pallas_call(
        functools.partial(_sort_combine_kernel, tm=tm, nck=nck, alpha=alpha),
        grid_spec=pltpu.PrefetchScalarGridSpec(
            num_scalar_prefetch=2,
            grid=(t // tm,),
            in_specs=[pl.BlockSpec((tm, d), row), pl.BlockSpec((tm, 2), row), pl.BlockSpec((tm, 2), row),
                      pl.BlockSpec((1, d), const), pl.BlockSpec((1, d), const),
                      pl.BlockSpec(memory_space=pl.ANY)],
            out_specs=pl.BlockSpec((tm, d), row),
            scratch_shapes=[pltpu.VMEM((sorted_rows, d), F32), pltpu.SemaphoreType.DMA(())],
        ),
        out_shape=jax.ShapeDtypeStruct((t, d), F32),
        compiler_params=_params(("arbitrary",)),
        name="moe_combine",
    )(chunk_src, n_chunks, h, lp_cols, gates, ln_w.reshape(1, d), ln_b.reshape(1, d), yb)


def _moe_sorted(h1, info, w_gate, w_up, w_down, layer, ln_w, ln_b, alpha):
    t = h1.shape[0]
    tm = SORT_ROWS
    ntile = t // tm
    run_pad = SUBLANES - 1
    sorted_rows = 2 * tm + LANES
    nck = sorted_rows // SUBLANES
    n_rows = (2 * t + ntile * N_EXPERTS * run_pad + MOE_ROWS - 1) // MOE_ROWS * MOE_ROWS + N_EXPERTS * MOE_ROWS
    nb = n_rows // MOE_ROWS
    i32 = jnp.int32
    e1, e2 = info[0].astype(i32), info[1].astype(i32)
    rank1, rank2 = info[4].astype(i32), info[5].astype(i32)
    expert_ids = jnp.arange(N_EXPERTS, dtype=i32)[None, :]
    oh1 = e1[:, None] == expert_ids
    oh2 = e2[:, None] == expert_ids
    cnt = jnp.logical_or(oh1, oh2).astype(i32).reshape(ntile, tm, N_EXPERTS).sum(axis=1)
    cnt8 = (cnt + run_pad) // SUBLANES * SUBLANES
    local = jnp.cumsum(cnt8, axis=1) - cnt8
    before = jnp.cumsum(cnt, axis=0) - cnt
    seg = cnt8.sum(axis=0)
    seg_pad = (seg + MOE_ROWS - 1) // MOE_ROWS * MOE_ROWS
    ends = jnp.cumsum(seg_pad)
    slot = (ends - seg_pad)[None, :] + jnp.cumsum(cnt8, axis=0) - cnt8
    shift = jnp.repeat(local - before, tm, axis=0)
    lp1 = jnp.sum(jnp.where(oh1, shift, 0), axis=1) + rank1
    lp2 = jnp.sum(jnp.where(oh2, shift, 0), axis=1) + rank2
    chunk_row = jnp.arange(nck, dtype=i32) * SUBLANES
    run_of = jnp.sum(((local + cnt8)[:, None, :] <= chunk_row[None, :, None]).astype(i32), axis=2)
    run_of = jnp.minimum(run_of, N_EXPERTS - 1)
    chunk_slot = jnp.take_along_axis(slot - local, run_of, axis=1) + chunk_row[None, :]
    n_chunks = cnt8.sum(axis=1) // SUBLANES
    n_valid = (ends[-1] // MOE_ROWS).astype(i32).reshape(1)
    blk = jnp.minimum(jnp.arange(nb, dtype=i32), n_valid[0] - 1) * MOE_ROWS
    block_e = jnp.minimum(jnp.sum((ends[None, :] <= blk[:, None]).astype(i32), axis=1), N_EXPERTS - 1)
    zero_start = jnp.concatenate([jnp.maximum(ends - MOE_ROWS, 0), n_valid]).astype(i32)
    gates = jnp.stack([info[2], info[3]], axis=1)
    chunk_slot = chunk_slot.reshape(-1).astype(i32)

    xb = _sort_dispatch(h1, jnp.stack([lp1, lp2], axis=0), chunk_slot, n_chunks, zero_start, n_rows, tm,
                        sorted_rows)
    yb = _experts(xb, block_e, n_valid, w_gate, w_up, w_down, layer)
    return _sort_combine(h1, yb, jnp.stack([lp1, lp2], axis=1), gates, chunk_slot, n_chunks, ln_w, ln_b,
                         alpha, tm, sorted_rows)


def _moe(h1, info, counts, w_gate, w_up, w_down, layer, ln_w, ln_b, alpha):
    t = h1.shape[0]
    n_rows = 2 * t + N_EXPERTS * MOE_ROWS
    nb = n_rows // MOE_ROWS
    cnt = counts[:, 0].astype(jnp.int32)
    padded = (cnt + MOE_ROWS - 1) // MOE_ROWS * MOE_ROWS
    ends = jnp.cumsum(padded)
    pstart = ends - padded
    e1 = info[0].astype(jnp.int32)
    e2 = info[1].astype(jnp.int32)
    expert_ids = jnp.arange(N_EXPERTS, dtype=jnp.int32)[:, None]
    dest1 = jnp.sum(jnp.where(e1[None, :] == expert_ids, pstart[:, None], 0), axis=0) + info[4].astype(jnp.int32)
    dest2 = jnp.sum(jnp.where(e2[None, :] == expert_ids, pstart[:, None], 0), axis=0) + info[5].astype(jnp.int32)
    gates = jnp.stack([info[2], info[3]], axis=1)
    n_valid = (ends[-1] // MOE_ROWS).astype(jnp.int32).reshape(1)
    blk = jnp.minimum(jnp.arange(nb, dtype=jnp.int32), n_valid[0] - 1) * MOE_ROWS
    block_e = jnp.minimum(jnp.sum((ends[None, :] <= blk[:, None]).astype(jnp.int32), axis=1), N_EXPERTS - 1)
    zero_start = jnp.concatenate([jnp.maximum(ends - MOE_ROWS, 0), n_valid]).astype(jnp.int32)

    xb = _dispatch(h1, dest1, dest2, zero_start, n_rows)
    yb = _experts(xb, block_e, n_valid, w_gate, w_up, w_down, layer)
    return _combine(h1, yb, dest1, dest2, gates, ln_w, ln_b, alpha)


def _rotary_column_order(w_in):
    d = w_in.shape[0]
    nq = RET_HEADS * RET_DK

    def perm(w):
        return w.reshape(d, RET_HEADS, RET_DK // 2, 2).transpose(0, 3, 1, 2).reshape(d, nq)

    return jnp.concatenate([perm(w_in[:, :nq]), perm(w_in[:, nq:2 * nq]), w_in[:, 2 * nq:]], axis=1)


def kernel(x, w_in_even, w_out_even, lru_conv_w, lru_conv_b, lru_gate_w, lru_gate_b, lru_lambda,
           w_in_odd, w_out_odd, hg_lower_bounds, hg_norm_w, ln_w, ln_b, router_w,
           moe_w_gate, moe_w_up, moe_w_down):
    batch, seq_len, d = x.shape
    depth = ln_w.shape[0]
    alpha = (2.0 * depth) ** 0.25
    t = batch * seq_len
    h = x.reshape(t, d)
    router_wt = router_w.T
    half = RET_DK // 2
    inv_freq = ROPE_BASE ** (-jnp.arange(0, RET_DK, 2, dtype=F32) / RET_DK)
    inv_freq = jnp.tile(inv_freq, LANES // half).reshape(1, LANES)
    nret = RET_HEADS * RET_DV

    for layer in range(depth):
        j = layer // 2
        if layer % 2 == 0:
            w_in = _rotary_column_order(w_in_even[j]).astype(BF16)
            n_bf = 2 * RET_HEADS * RET_DK + 2 * nret
            proj_b, proj_f = _project(h, w_in, inv_freq, n_bf, rotary=True, seq_len=seq_len)
            ret = _retention(proj_b, batch, seq_len)
            h_fwd, h_bwd = _rglru(proj_f, lru_conv_w[j], lru_conv_b[j], lru_gate_w[j], lru_gate_b[j],
                                  lru_lambda[j], batch, seq_len)
            w_out = w_out_even[j].astype(BF16)
            mixer_inputs, weights = (ret, h_fwd, h_bwd, proj_f), [w_out[:nret], w_out[nret:]]
        else:
            w = w_in_odd[j]
            w_in = jnp.concatenate([w[:, :2 * d], w[:, 4 * d:], w[:, 2 * d:4 * d]], axis=1).astype(BF16)
            proj_b, proj_f = _project(h, w_in, inv_freq, 3 * d, rotary=False, seq_len=seq_len)
            o_fwd, o_bwd = _gla(proj_b, proj_f, hg_lower_bounds, layer, batch, seq_len)
            mixer_inputs, weights = (o_fwd, o_bwd, proj_b, hg_norm_w[j]), [w_out_odd[j].astype(BF16)]
        h1, info, counts = _out_project(layer % 2 == 0, mixer_inputs, weights, h, ln_w[layer, 0],
                                        ln_b[layer, 0], router_wt, alpha)
        h = _moe_sorted(h1, info, moe_w_gate, moe_w_up, moe_w_down, layer,
                        ln_w[layer, 1], ln_b[layer, 1], alpha)
    return h.reshape(batch, seq_len, d)
```

```python
import functools
import math

import jax
import jax.numpy as jnp
from jax import lax
from jax.experimental import pallas as pl
from jax.experimental.pallas import tpu as pltpu

F32 = jnp.float32
BF16 = jnp.bfloat16

RET_HEADS = 4
RET_DK = 64
RET_DV = 128
RET_CHUNK = 128
ROPE_BASE = 10000.0
LRU_BLOCKS = 4
LRU_BW = 128
LRU_CONV = 4
LRU_C = 8.0
HG_HEADS = 8
HG_DK = 128
HG_DV = 128
N_EXPERTS = 16
N_GROUPS = 4
EXPERTS_PER_GROUP = 4
LN_EPS = 1e-5
RMS_EPS = 1e-6

LANES = 128
SUBLANES = 8
PROJ_ROWS = 512
PROJ_COLS = 512
RET_CHUNKS_PER_STEP = 4
LRU_ROWS = 512
GLA_CHUNK = 32
GLA_ROWS = 256
TOKEN_ROWS = 512
SORT_ROWS = 256
MOE_ROWS = 256
VMEM_LIMIT = 48 * 1024 * 1024

_NT = (((1,), (1,)), ((), ()))
_TN = (((0,), (0,)), ((), ()))


def _params(sem):
    return pltpu.CompilerParams(dimension_semantics=sem, vmem_limit_bytes=VMEM_LIMIT)


def _sigmoid(x):
    return 0.5 * jnp.tanh(0.5 * x) + 0.5


def _silu(x):
    return x * _sigmoid(x)


def _softplus(x):
    return jnp.maximum(x, 0.0) + jnp.log1p(jnp.exp(-jnp.abs(x)))


def _layer_norm(u, w, b):
    mu = jnp.mean(u, axis=-1, keepdims=True)
    d = u - mu
    var = jnp.mean(d * d, axis=-1, keepdims=True)
    return d * lax.rsqrt(var + LN_EPS) * w + b


def _split3(x):
    hi = x.astype(BF16)
    rem = x - hi.astype(F32)
    mid = rem.astype(BF16)
    lo = (rem - mid.astype(F32)).astype(BF16)
    return hi, mid, lo


def _proj_kernel(x_ref, w_ref, inv_ref, ob_ref, of_ref, *, rotary, seq_len, tm, n_bf):
    i = pl.program_id(0)
    xb = x_ref[...].astype(BF16)
    tn = PROJ_COLS
    for j in range(w_ref.shape[1] // tn):
        acc = jnp.dot(xb, w_ref[:, j * tn:(j + 1) * tn], preferred_element_type=F32)
        if rotary and j == 0:
            row = lax.broadcasted_iota(jnp.int32, (tm, 1), 0) + i * tm
            ang = (row % seq_len).astype(F32) * inv_ref[...]
            cos = jnp.cos(ang)
            sin = jnp.sin(ang)
            scale = RET_DK ** -0.5
            q1, q2 = acc[:, 0:128], acc[:, 128:256]
            k1, k2 = acc[:, 256:384], acc[:, 384:512]
            ob_ref[:, 0:128] = ((q1 * cos - q2 * sin) * scale).astype(BF16)
            ob_ref[:, 128:256] = ((q1 * sin + q2 * cos) * scale).astype(BF16)
            ob_ref[:, 256:384] = (k1 * cos - k2 * sin).astype(BF16)
            ob_ref[:, 384:512] = (k1 * sin + k2 * cos).astype(BF16)
        elif (j + 1) * tn <= n_bf:
            ob_ref[:, j * tn:(j + 1) * tn] = acc.astype(BF16)
        else:
            of_ref[:, j * tn - n_bf:(j + 1) * tn - n_bf] = acc


def _project(x, w_bf16, inv_freq, n_bf, *, rotary, seq_len):
    t, k = x.shape
    n = w_bf16.shape[1]
    tm = min(PROJ_ROWS, seq_len)
    return pl.pallas_call(
        functools.partial(_proj_kernel, rotary=rotary, seq_len=seq_len, tm=tm, n_bf=n_bf),
        grid=(t // tm,),
        in_specs=[
            pl.BlockSpec((tm, k), lambda i: (i, 0)),
            pl.BlockSpec((k, n), lambda i: (0, 0)),
            pl.BlockSpec((1, LANES), lambda i: (0, 0)),
        ],
        out_specs=[pl.BlockSpec((tm, n_bf), lambda i: (i, 0)),
                   pl.BlockSpec((tm, n - n_bf), lambda i: (i, 0))],
        out_shape=[jax.ShapeDtypeStruct((t, n_bf), BF16),
                   jax.ShapeDtypeStruct((t, n - n_bf), F32)],
        compiler_params=_params(("arbitrary",)),
        name="in_proj",
    )(x, w_bf16, inv_freq)


def _ret_log_gamma(head):
    out = jnp.full(head.shape, math.log1p(-(2.0 ** -5.0)), F32)
    for h in range(1, RET_HEADS):
        out = jnp.where(head == h, math.log1p(-(2.0 ** (-5.0 - h))), out)
    return out


def _ret_lane_head():
    lane = lax.broadcasted_iota(jnp.int32, (1, 2 * LANES), 1)
    return (lane % LANES) // (RET_DK // 2)


def _ret_state_mask():
    shape = (RET_HEADS * RET_DV, 2 * LANES)
    row_head = lax.broadcasted_iota(jnp.int32, shape, 0) // RET_DV
    col_head = (lax.broadcasted_iota(jnp.int32, shape, 1) % LANES) // (RET_DK // 2)
    return row_head == col_head


def _ret_bstate_kernel(k_ref, v_ref, sb_ref, s_ref, *, cps):
    c = RET_CHUNK

    @pl.when(pl.program_id(1) == 0)
    def _():
        s_ref[...] = jnp.zeros_like(s_ref)

    lg = _ret_log_gamma(_ret_lane_head())
    idx = lax.broadcasted_iota(jnp.int32, (c, 1), 0).astype(F32)
    k_decay = jnp.exp(lg * idx)
    chunk_decay = jnp.exp(lg * float(c))
    mask = _ret_state_mask()
    for cc in reversed(range(cps)):
        rows = slice(cc * c, (cc + 1) * c)
        sb_ref[0, cc] = s_ref[...].astype(BF16)
        kb = (k_ref[rows, :] * k_decay).astype(BF16)
        upd = lax.dot_general(v_ref[rows, :], kb, _TN, preferred_element_type=F32)
        s_ref[...] = s_ref[...] * chunk_decay + jnp.where(mask, upd, 0.0)


def _ret_out_kernel(q_ref, k_ref, v_ref, g_ref, sb_ref, o_ref, s_ref, *, cps):
    c = RET_CHUNK

    @pl.when(pl.program_id(1) == 0)
    def _():
        s_ref[...] = jnp.zeros_like(s_ref)

    lane_head = _ret_lane_head()
    lg = _ret_log_gamma(lane_head)
    idx = lax.broadcasted_iota(jnp.int32, (c, 1), 0).astype(F32)
    q_decay_f = jnp.exp(lg * (idx + 1.0))
    q_decay_b = jnp.exp(lg * (float(c) - idx))
    k_decay = jnp.exp(lg * (float(c) - 1.0 - idx))
    chunk_decay = jnp.exp(lg * float(c))
    mask = _ret_state_mask()
    ii = lax.broadcasted_iota(jnp.int32, (c, c), 0)
    jj = lax.broadcasted_iota(jnp.int32, (c, c), 1)
    dist = jnp.abs(ii - jj).astype(F32)
    intra_decay = [jnp.exp(math.log1p(-(2.0 ** (-5.0 - h))) * dist) for h in range(RET_HEADS)]

    for cc in range(cps):
        rows = slice(cc * c, (cc + 1) * c)
        q = q_ref[rows, :]
        k = k_ref[rows, :]
        v = v_ref[rows, :]
        qf = (q * q_decay_f).astype(BF16)
        qb = (q * q_decay_b).astype(BF16)
        cross = (lax.dot_general(qf, s_ref[...].astype(BF16), _NT, preferred_element_type=F32)
                 + lax.dot_general(qb, sb_ref[0, cc], _NT, preferred_element_type=F32))
        for h in range(RET_HEADS):
            qh = jnp.where(lane_head == h, q, jnp.zeros_like(q))
            s = lax.dot_general(qh, k, _NT, preferred_element_type=F32) * intra_decay[h]
            cols = slice(h * RET_DV, (h + 1) * RET_DV)
            o = jnp.dot(s.astype(BF16), v[:, cols], preferred_element_type=F32) + cross[:, cols]
            mu = jnp.mean(o, axis=-1, keepdims=True)
            d = o - mu
            var = jnp.mean(d * d, axis=-1, keepdims=True)
            gate = _silu(g_ref[rows, cols].astype(F32))
            o_ref[rows, cols] = (gate * (d * lax.rsqrt(var + LN_EPS))).astype(BF16)
        kf = (k * k_decay).astype(BF16)
        upd = lax.dot_general(v, kf, _TN, preferred_element_type=F32)
        s_ref[...] = s_ref[...] * chunk_decay + jnp.where(mask, upd, 0.0)


def _retention(proj, batch, seq_len):
    t = proj.shape[0]
    c = RET_CHUNK
    cps = min(RET_CHUNKS_PER_STEP, seq_len // c)
    rows = cps * c
    ns = seq_len // rows
    dv = RET_HEADS * RET_DV
    state_shape = (dv, 2 * LANES)
    rev = lambda b, n: b * ns + (ns - 1 - n)
    fwd = lambda b, n: b * ns + n
    sb = pl.pallas_call(
        functools.partial(_ret_bstate_kernel, cps=cps),
        grid=(batch, ns),
        in_specs=[
            pl.BlockSpec((rows, 2 * LANES), lambda b, n: (rev(b, n), 1)),
            pl.BlockSpec((rows, dv), lambda b, n: (rev(b, n), 1)),
        ],
        out_specs=pl.BlockSpec((1, cps) + state_shape, lambda b, n: (b, ns - 1 - n, 0, 0)),
        out_shape=jax.ShapeDtypeStruct((batch, ns * cps) + state_shape, BF16),
        scratch_shapes=[pltpu.VMEM(state_shape, F32)],
        compiler_params=_params(("arbitrary", "arbitrary")),
        name="ret_bstate",
    )(proj, proj)
    return pl.pallas_call(
        functools.partial(_ret_out_kernel, cps=cps),
        grid=(batch, ns),
        in_specs=[
            pl.BlockSpec((rows, 2 * LANES), lambda b, n: (fwd(b, n), 0)),
            pl.BlockSpec((rows, 2 * LANES), lambda b, n: (fwd(b, n), 1)),
            pl.BlockSpec((rows, dv), lambda b, n: (fwd(b, n), 1)),
            pl.BlockSpec((rows, dv), lambda b, n: (fwd(b, n), 2)),
            pl.BlockSpec((1, cps) + state_shape, lambda b, n: (b, n, 0, 0)),
        ],
        out_specs=pl.BlockSpec((rows, dv), lambda b, n: (fwd(b, n), 0)),
        out_shape=jax.ShapeDtypeStruct((t, dv), BF16),
        scratch_shapes=[pltpu.VMEM(state_shape, F32)],
        compiler_params=_params(("arbitrary", "arbitrary")),
        name="ret_out",
    )(proj, proj, proj, proj, sb)


def _lru_kernel(xfp_ref, xf_ref, xfn_ref, xbp_ref, xb_ref, xbn_ref, cw_ref, cb_ref, gw_ref, gb_ref,
                lam_ref, hf_ref, hb_ref, xx_ref, a_ref, b_ref, h_ref, *, nt, ts, batch):
    i = pl.program_id(0)
    halo = SUBLANES
    lo = LRU_CONV // 2

    @pl.when(i == 0)
    def _():
        h_ref[...] = jnp.zeros_like(h_ref)

    def prepare(xp_ref, x_ref, xn_ref, tile, z, slot):
        for b in range(batch):
            xx_ref[0:halo, :] = jnp.where(tile == 0, 0.0, xp_ref[b])
            xx_ref[halo:halo + ts, :] = x_ref[b]
            xx_ref[halo + ts:2 * halo + ts, :] = jnp.where(tile == nt - 1, 0.0, xn_ref[b])
            xc = cb_ref[...]
            for tap in range(LRU_CONV):
                xc = xc + cw_ref[tap:tap + 1, :] * xx_ref[pl.ds(halo - lo + tap, ts), :]
            for n in range(LRU_BLOCKS):
                cols = slice(n * LRU_BW, (n + 1) * LRU_BW)
                xn = xc[:, cols]
                g = (jnp.dot(xn.astype(BF16), gw_ref[z, n], preferred_element_type=F32)
                     + gb_ref[z, n:n + 1, :])
                r = _sigmoid(g[:, :LRU_BW])
                ig = _sigmoid(g[:, LRU_BW:])
                a = jnp.exp((-LRU_C) * r * _softplus(-lam_ref[z, :, cols]))
                a_ref[slot + b, :, cols] = a
                b_ref[slot + b, :, cols] = jnp.sqrt(1.0 - a * a) * (ig * xn)

    prepare(xfp_ref, xf_ref, xfn_ref, i, 0, 0)
    prepare(xbp_ref, xb_ref, xbn_ref, nt - 1 - i, 1, batch)

    def step(s, hs):
        out = []
        for k in range(2 * batch):
            row = s if k < batch else ts - 1 - s
            h = a_ref[k, pl.ds(row, 1), :] * hs[k] + b_ref[k, pl.ds(row, 1), :]
            if k < batch:
                hf_ref[k, pl.ds(row, 1), :] = h
            else:
                hb_ref[k - batch, pl.ds(row, 1), :] = h
            out.append(h)
        return tuple(out)

    hs = lax.fori_loop(0, ts, step, tuple(h_ref[k] for k in range(2 * batch)), unroll=8)
    for k in range(2 * batch):
        h_ref[k] = hs[k]


def _rglru(proj_f32, conv_w, conv_b, gate_w, gate_b, lam, batch, seq_len):
    w = LRU_BLOCKS * LRU_BW
    ts = min(LRU_ROWS, seq_len)
    nt = seq_len // ts
    rows8 = ts // SUBLANES
    last8 = seq_len // SUBLANES - 1
    x3 = proj_f32.reshape(batch, seq_len, proj_f32.shape[1])
    gw = jnp.concatenate([gate_w[:, 0], gate_w[:, 1]], axis=-1).astype(BF16)
    gb = jnp.concatenate([gate_b[:, 0], gate_b[:, 1]], axis=-1)
    bwd = lambda i: nt - 1 - i

    def tile_specs(tile):
        return [
            pl.BlockSpec((batch, SUBLANES, w), lambda i: (0, jnp.maximum(tile(i) * rows8 - 1, 0), 0)),
            pl.BlockSpec((batch, ts, w), lambda i: (0, tile(i), 0)),
            pl.BlockSpec((batch, SUBLANES, w), lambda i: (0, jnp.minimum((tile(i) + 1) * rows8, last8), 0)),
        ]

    full = lambda a: pl.BlockSpec(a.shape, lambda i: (0,) * a.ndim)
    cb = conv_b.reshape(1, w)
    lam3 = lam.reshape(2, 1, w)
    state = jax.ShapeDtypeStruct((batch, seq_len, w), F32)
    h_fwd, h_bwd = pl.pallas_call(
        functools.partial(_lru_kernel, nt=nt, ts=ts, batch=batch),
        grid=(nt,),
        in_specs=tile_specs(lambda i: i) + tile_specs(bwd) + [full(conv_w), full(cb), full(gw), full(gb),
                                                               full(lam3)],
        out_specs=[pl.BlockSpec((batch, ts, w), lambda i: (0, i, 0)),
                   pl.BlockSpec((batch, ts, w), lambda i: (0, bwd(i), 0))],
        out_shape=[state, state],
        scratch_shapes=[pltpu.VMEM((ts + 2 * SUBLANES, w), F32), pltpu.VMEM((2 * batch, ts, w), F32),
                        pltpu.VMEM((2 * batch, ts, w), F32), pltpu.VMEM((2 * batch, 1, w), F32)],
        compiler_params=_params(("arbitrary",)),
        name="lru_scan",
    )(x3, x3, x3, x3, x3, x3, conv_w, cb, gw, gb, lam3)
    return h_fwd.reshape(batch * seq_len, w), h_bwd.reshape(batch * seq_len, w)


def _gla_kernel(qf_ref, vf_ref, zf_ref, qb_ref, vb_ref, zb_ref, lbp_ref, of_ref, ob_ref, s_ref,
                *, ts, layer, batch):
    c = GLA_CHUNK
    nchunks = ts // c

    @pl.when(pl.program_id(0) == 0)
    def _():
        s_ref[...] = jnp.zeros_like(s_ref)

    p = lbp_ref[...]
    e = jnp.exp(p - jnp.max(p, axis=0, keepdims=True))
    sm = e / jnp.sum(e, axis=0, keepdims=True)
    lb = jnp.zeros((1, p.shape[1]), F32)
    for r in range(1, layer + 1):
        lb = lb + sm[r:r + 1, :]

    head_cols = [slice(h * HG_DK, (h + 1) * HG_DK) for h in range(HG_HEADS)]
    hc = HG_HEADS * c
    ii = lax.broadcasted_iota(jnp.int32, (c, c), 0)
    jj = lax.broadcasted_iota(jnp.int32, (c, c), 1)
    si = lax.broadcasted_iota(jnp.int32, (hc, hc), 0)
    sj = lax.broadcasted_iota(jnp.int32, (hc, hc), 1)
    same_head = si // c == sj // c
    mid = c // 2

    def stack(a):
        return jnp.concatenate([a[:, cols] for cols in head_cols], axis=0)

    tri = {False: (jj <= ii).astype(BF16), True: (jj >= ii).astype(BF16)}
    keep = {False: jnp.logical_and(same_head, sj <= si), True: jnp.logical_and(same_head, sj >= si)}
    chains = ([(qf_ref, vf_ref, zf_ref, of_ref, b, False) for b in range(batch)]
              + [(qb_ref, vb_ref, zb_ref, ob_ref, b, True) for b in range(batch)])

    for cc in range(nchunks):
        work = []
        for q_ref, v_ref, z_ref, o_ref, b, reverse in chains:
            rows = pl.ds(((nchunks - 1 - cc) if reverse else cc) * c, c)
            sig = _sigmoid(z_ref[b, rows, :])
            log_f = jnp.log(lb + (1.0 - lb) * sig)
            work.append(dict(rows=rows, reverse=reverse, o_ref=o_ref, b=b,
                             key=(1.0 - lb) * (1.0 - sig),
                             qs=_silu(q_ref[b, rows, :].astype(F32)) * (HG_DK ** -0.5),
                             v=v_ref[b, rows, :], split=_split3(log_f)))
        for w in work:
            f_hi, f_mid, f_lo = w["split"]
            t = tri[w["reverse"]]
            w["bcum"] = (jnp.dot(t, f_lo, preferred_element_type=F32)
                         + jnp.dot(t, f_mid, preferred_element_type=F32)
                         + jnp.dot(t, f_hi, preferred_element_type=F32))
        for w in work:
            bcum = w["bcum"]
            end = 0 if w["reverse"] else c - 1
            ref_row = bcum[mid:mid + 1, :]
            b_end = bcum[end:end + 1, :]
            w["qe"] = (w["qs"] * jnp.exp(bcum - ref_row)).astype(BF16)
            w["ke"] = (w["key"] * jnp.exp(ref_row - bcum)).astype(BF16)
            w["qd"] = (w["qs"] * jnp.exp(bcum)).astype(BF16)
            w["kd"] = (w["key"] * jnp.exp(b_end - bcum)).astype(BF16)
            w["decay"] = jnp.exp(b_end)
        for w in work:
            w["att"] = lax.dot_general(stack(w["qe"]), stack(w["ke"]), _NT, preferred_element_type=F32)
        for w in work:
            att = jnp.where(keep[w["reverse"]], w["att"], 0.0).astype(BF16)
            w["intra"] = jnp.dot(att, stack(w["v"]), preferred_element_type=F32)
        for k, w in enumerate(work):
            w["st"] = [s_ref[k, h] for h in range(HG_HEADS)]
            w["inter"] = [lax.dot_general(w["qd"][:, cols], w["st"][h].astype(BF16), _NT,
                                          preferred_element_type=F32) for h, cols in enumerate(head_cols)]
        for w in work:
            w["upd"] = [lax.dot_general(w["v"][:, cols], w["kd"][:, cols], _TN, preferred_element_type=F32)
                        for cols in head_cols]
        for k, w in enumerate(work):
            for h, cols in enumerate(head_cols):
                w["o_ref"][w["b"], w["rows"], cols] = w["intra"][h * c:(h + 1) * c, :] + w["inter"][h]
                s_ref[k, h] = w["st"][h] * w["decay"][:, cols] + w["upd"][h]


def _gla(proj_bf16, proj_f32, lower_bounds, layer, batch, seq_len):
    d = HG_HEADS * HG_DK
    ts = min(GLA_ROWS, seq_len)
    nt = seq_len // ts
    depth = lower_bounds.shape[0]
    pb = proj_bf16.reshape(batch, seq_len, proj_bf16.shape[1])
    pf = proj_f32.reshape(batch, seq_len, proj_f32.shape[1])
    bwd = lambda i: nt - 1 - i
    blk = lambda tile, col: pl.BlockSpec((batch, ts, d), lambda i: (0, tile(i), col))
    fwd = lambda i: i
    out = jax.ShapeDtypeStruct((batch, seq_len, d), F32)
    o_fwd, o_bwd = pl.pallas_call(
        functools.partial(_gla_kernel, ts=ts, layer=layer, batch=batch),
        grid=(nt,),
        in_specs=[blk(fwd, 0), blk(fwd, 1), blk(fwd, 0), blk(bwd, 0), blk(bwd, 1), blk(bwd, 1),
                  pl.BlockSpec((depth, d), lambda i: (0, 0))],
        out_specs=[blk(fwd, 0), blk(bwd, 0)],
        out_shape=[out, out],
        scratch_shapes=[pltpu.VMEM((2 * batch, HG_HEADS, HG_DV, HG_DK), F32)],
        compiler_params=_params(("arbitrary",)),
        name="gla_scan",
    )(pb, pb, pf, pb, pb, pf, lower_bounds)
    return o_fwd.reshape(batch * seq_len, d), o_bwd.reshape(batch * seq_len, d)


def _top2(p):
    v1 = jnp.maximum(jnp.maximum(p[0], p[1]), jnp.maximum(p[2], p[3]))
    i1 = jnp.where(p[0] == v1, 0, jnp.where(p[1] == v1, 1, jnp.where(p[2] == v1, 2, 3)))
    q = [jnp.where(i1 == k, -1.0, p[k]) for k in range(4)]
    v2 = jnp.maximum(jnp.maximum(q[0], q[1]), jnp.maximum(q[2], q[3]))
    i2 = jnp.where(q[0] == v2, 0, jnp.where(q[1] == v2, 1, jnp.where(q[2] == v2, 2, 3)))
    return v1, i1, v2, i2


def _outproj_kernel(*refs, even, alpha, tm):
    if even:
        ret_ref, hf_ref, hb_ref, gr_ref, w0_ref, w1_ref = refs[:6]
        rest = refs[6:]
        lru = ((hf_ref[...] + hb_ref[...]) * jax.nn.gelu(gr_ref[...])).astype(BF16)
        y = (jnp.dot(ret_ref[...], w0_ref[...], preferred_element_type=F32)
             + jnp.dot(lru, w1_ref[...], preferred_element_type=F32))
    else:
        of_ref, ob_ref, g_ref, nw_ref, w0_ref = refs[:5]
        rest = refs[5:]
        o = of_ref[...] + ob_ref[...]
        ms = jnp.mean(o * o, axis=-1, keepdims=True)
        mix = o * lax.rsqrt(ms + RMS_EPS) * nw_ref[...] * _silu(g_ref[...].astype(F32))
        y = jnp.dot(mix.astype(BF16), w0_ref[...], preferred_element_type=F32)
    h_ref, lnw_ref, lnb_ref, rw_ref, o_ref, info_ref, cnt_ref, carry_ref = rest
    i = pl.program_id(0)

    @pl.when(i == 0)
    def _():
        carry_ref[...] = jnp.zeros_like(carry_ref)

    h1 = _layer_norm(alpha * h_ref[...] + y, lnw_ref[...], lnb_ref[...])
    o_ref[...] = h1

    h_hi = h1.astype(BF16)
    h_lo = (h1 - h_hi.astype(F32)).astype(BF16)
    rw = rw_ref[...]
    r_hi = rw.astype(BF16)
    r_lo = (rw - r_hi.astype(F32)).astype(BF16)
    logits = (lax.dot_general(r_lo, h_hi, _NT, preferred_element_type=F32)
              + lax.dot_general(r_hi, h_lo, _NT, preferred_element_type=F32)
              + lax.dot_general(r_hi, h_hi, _NT, preferred_element_type=F32))
    ex = jnp.exp(logits - jnp.max(logits, axis=0, keepdims=True))
    probs = ex / jnp.sum(ex, axis=0, keepdims=True)
    best = None
    for g in range(N_GROUPS):
        rows = [probs[g * EXPERTS_PER_GROUP + k:g * EXPERTS_PER_GROUP + k + 1, :]
                for k in range(EXPERTS_PER_GROUP)]
        v1, i1, v2, i2 = _top2(rows)
        cand = (v1 + v2, v1, i1 + g * EXPERTS_PER_GROUP, v2, i2 + g * EXPERTS_PER_GROUP)
        if best is None:
            best = cand
        else:
            take = cand[0] > best[0]
            best = tuple(jnp.where(take, cn, bs) for cn, bs in zip(cand, best))
    _, v1, e1, v2, e2 = best
    denom = v1 + v2
    g1 = v1 / denom
    g2 = v2 / denom

    eid = lax.broadcasted_iota(jnp.int32, (N_EXPERTS, tm), 0)
    oh1 = (eid == e1).astype(F32)
    oh2 = (eid == e2).astype(F32)
    oh = oh1 + oh2
    tt = lax.broadcasted_iota(jnp.int32, (tm, tm), 0)
    uu = lax.broadcasted_iota(jnp.int32, (tm, tm), 1)
    before = (tt < uu).astype(BF16)
    base = carry_ref[:, 0:1] + jnp.dot(oh.astype(BF16), before, preferred_element_type=F32)
    rank1 = jnp.sum(oh1 * base, axis=0, keepdims=True)
    rank2 = jnp.sum(oh2 * base, axis=0, keepdims=True)
    carry_ref[...] = carry_ref[...] + jnp.sum(oh, axis=1, keepdims=True)
    cnt_ref[...] = carry_ref[...]
    zero = jnp.zeros_like(g1)
    info_ref[...] = jnp.concatenate(
        [e1.astype(F32), e2.astype(F32), g1, g2, rank1, rank2, zero, zero], axis=0)


def _out_project(even, mixer_inputs, weights_bf16, h, ln_w, ln_b, router_wt, alpha):
    t, d = h.shape
    tm = TOKEN_ROWS
    row = lambda i: (i, 0)
    const = lambda i: (0, 0)
    if even:
        ret, h_fwd, h_bwd, proj_f32 = mixer_inputs
        w = h_fwd.shape[1]
        in_specs = [pl.BlockSpec((tm, ret.shape[1]), row), pl.BlockSpec((tm, w), row),
                    pl.BlockSpec((tm, w), row), pl.BlockSpec((tm, w), lambda i: (i, 1))]
        args = [ret, h_fwd, h_bwd, proj_f32]
    else:
        o_fwd, o_bwd, proj_bf16, norm_w = mixer_inputs
        in_specs = [pl.BlockSpec((tm, d), row), pl.BlockSpec((tm, d), row),
                    pl.BlockSpec((tm, d), lambda i: (i, 2)), pl.BlockSpec((1, d), const)]
        args = [o_fwd, o_bwd, proj_bf16, norm_w.reshape(1, d)]
    in_specs += [pl.BlockSpec(wm.shape, const) for wm in weights_bf16]
    in_specs += [pl.BlockSpec((tm, d), row), pl.BlockSpec((1, d), const), pl.BlockSpec((1, d), const),
                 pl.BlockSpec((N_EXPERTS, d), const)]
    args += list(weights_bf16) + [h, ln_w.reshape(1, d), ln_b.reshape(1, d), router_wt]
    return pl.pallas_call(
        functools.partial(_outproj_kernel, even=even, alpha=alpha, tm=tm),
        grid=(t // tm,),
        in_specs=in_specs,
        out_specs=[pl.BlockSpec((tm, d), row),
                   pl.BlockSpec((SUBLANES, tm), lambda i: (0, i)),
                   pl.BlockSpec((N_EXPERTS, LANES), const)],
        out_shape=[jax.ShapeDtypeStruct((t, d), F32),
                   jax.ShapeDtypeStruct((SUBLANES, t), F32),
                   jax.ShapeDtypeStruct((N_EXPERTS, LANES), F32)],
        scratch_shapes=[pltpu.VMEM((N_EXPERTS, LANES), F32)],
        compiler_params=_params(("arbitrary",)),
        name="out_proj_router",
    )(*args)


def _row_copy(src_ref, src_row, dst_ref, dst_row, sem):
    return pltpu.make_async_copy(src_ref.at[pl.ds(src_row, 1)], dst_ref.at[pl.ds(dst_row, 1)], sem)


def _dispatch_kernel(d1_ref, d2_ref, zs_ref, h_ref, xb_ref, zero_ref, sem, *, tm):
    i = pl.program_id(0)
    base = i * tm

    @pl.when(i == 0)
    def _():
        zero_ref[...] = jnp.zeros_like(zero_ref)

        def clear(row):
            start = pl.multiple_of(row, MOE_ROWS)
            return pltpu.make_async_copy(zero_ref, xb_ref.at[pl.ds(start, MOE_ROWS)], sem.at[0])

        for e in range(N_EXPERTS):
            clear(zs_ref[e]).start()
        for e in range(N_EXPERTS):
            clear(zs_ref[e]).wait()

        def clear_tail(b, carry):
            clear(b * MOE_ROWS).start()
            clear(b * MOE_ROWS).wait()
            return carry

        lax.fori_loop(zs_ref[N_EXPERTS], xb_ref.shape[0] // MOE_ROWS, clear_tail, 0)

    def start(t, carry):
        _row_copy(h_ref, t, xb_ref, d1_ref[base + t], sem.at[0]).start()
        _row_copy(h_ref, t, xb_ref, d2_ref[base + t], sem.at[1]).start(priority=1)
        return carry

    lax.fori_loop(0, tm, start, 0, unroll=8)
    pltpu.make_async_copy(h_ref, xb_ref.at[pl.ds(0, tm)], sem.at[0]).wait()
    pltpu.make_async_copy(h_ref, xb_ref.at[pl.ds(0, tm)], sem.at[1]).wait()


def _dispatch(h, dest1, dest2, zero_start, n_rows):
    t, d = h.shape
    tm = TOKEN_ROWS
    return pl.pallas_call(
        functools.partial(_dispatch_kernel, tm=tm),
        grid_spec=pltpu.PrefetchScalarGridSpec(
            num_scalar_prefetch=3,
            grid=(t // tm,),
            in_specs=[pl.BlockSpec((tm, d), lambda i, d1, d2, zs: (i, 0))],
            out_specs=pl.BlockSpec(memory_space=pl.ANY),
            scratch_shapes=[pltpu.VMEM((MOE_ROWS, d), F32), pltpu.SemaphoreType.DMA((2,))],
        ),
        out_shape=jax.ShapeDtypeStruct((n_rows, d), F32),
        compiler_params=_params(("arbitrary",)),
        name="moe_dispatch",
    )(dest1, dest2, zero_start, h)


def _expert_kernel(be_ref, nv_ref, x_ref, wg_ref, wu_ref, wd_ref, o_ref, wgb_ref, wub_ref, wdb_ref):
    i = pl.program_id(0)
    changed = jnp.logical_or(i == 0, be_ref[i] != be_ref[jnp.maximum(i - 1, 0)])

    @pl.when(changed)
    def _():
        wgb_ref[...] = wg_ref[...].astype(BF16)
        wub_ref[...] = wu_ref[...].astype(BF16)
        wdb_ref[...] = wd_ref[...].astype(BF16)

    @pl.when(i < nv_ref[0])
    def _():
        x = x_ref[...].astype(BF16)
        gate = jnp.dot(x, wgb_ref[...], preferred_element_type=F32)
        up = jnp.dot(x, wub_ref[...], preferred_element_type=F32)
        hid = (_silu(gate) * up).astype(BF16)
        o_ref[...] = jnp.dot(hid, wdb_ref[...], preferred_element_type=F32)

    @pl.when(i >= nv_ref[0])
    def _():
        o_ref[...] = jnp.zeros_like(o_ref)


def _experts(xb, block_e, n_valid, w_gate, w_up, w_down, layer):
    p, d = xb.shape
    de = w_gate.shape[3]
    nb = p // MOE_ROWS
    return pl.pallas_call(
        _expert_kernel,
        grid_spec=pltpu.PrefetchScalarGridSpec(
            num_scalar_prefetch=2,
            grid=(nb,),
            in_specs=[
                pl.BlockSpec((MOE_ROWS, d), lambda i, be, nv: (jnp.maximum(jnp.minimum(i, nv[0] - 1), 0), 0)),
                pl.BlockSpec((None, None, d, de), lambda i, be, nv: (layer, be[i], 0, 0)),
                pl.BlockSpec((None, None, d, de), lambda i, be, nv: (layer, be[i], 0, 0)),
                pl.BlockSpec((None, None, de, d), lambda i, be, nv: (layer, be[i], 0, 0)),
            ],
            out_specs=pl.BlockSpec((MOE_ROWS, d), lambda i, be, nv: (i, 0)),
            scratch_shapes=[pltpu.VMEM((d, de), BF16), pltpu.VMEM((d, de), BF16),
                            pltpu.VMEM((de, d), BF16)],
        ),
        out_shape=jax.ShapeDtypeStruct((p, d), F32),
        compiler_params=_params(("arbitrary",)),
        name="moe_experts",
    )(block_e, n_valid, xb, w_gate, w_up, w_down)


def _combine_kernel(d1_ref, d2_ref, h_ref, gates_ref, lnw_ref, lnb_ref, yb_ref, o_ref, r_ref, sem,
                    *, tm, alpha):
    base = pl.program_id(0) * tm

    def start(t, carry):
        _row_copy(yb_ref, d1_ref[base + t], r_ref.at[0], t, sem.at[0]).start()
        _row_copy(yb_ref, d2_ref[base + t], r_ref.at[1], t, sem.at[1]).start(priority=1)
        return carry

    lax.fori_loop(0, tm, start, 0, unroll=8)
    pltpu.make_async_copy(yb_ref.at[pl.ds(0, tm)], r_ref.at[0], sem.at[0]).wait()
    pltpu.make_async_copy(yb_ref.at[pl.ds(0, tm)], r_ref.at[1], sem.at[1]).wait()
    y = gates_ref[:, 0:1] * r_ref[0] + gates_ref[:, 1:2] * r_ref[1]
    o_ref[...] = _layer_norm(alpha * h_ref[...] + y, lnw_ref[...], lnb_ref[...])


def _combine(h, yb, dest1, dest2, gates, ln_w, ln_b, alpha):
    t, d = h.shape
    tm = TOKEN_ROWS
    return pl.pallas_call(
        functools.partial(_combine_kernel, tm=tm, alpha=alpha),
        grid_spec=pltpu.PrefetchScalarGridSpec(
            num_scalar_prefetch=2,
            grid=(t // tm,),
            in_specs=[pl.BlockSpec((tm, d), lambda i, d1, d2: (i, 0)),
                      pl.BlockSpec((tm, 2), lambda i, d1, d2: (i, 0)),
                      pl.BlockSpec((1, d), lambda i, d1, d2: (0, 0)),
                      pl.BlockSpec((1, d), lambda i, d1, d2: (0, 0)),
                      pl.BlockSpec(memory_space=pl.ANY)],
            out_specs=pl.BlockSpec((tm, d), lambda i, d1, d2: (i, 0)),
            scratch_shapes=[pltpu.VMEM((2, tm, d), F32), pltpu.SemaphoreType.DMA((2,))],
        ),
        out_shape=jax.ShapeDtypeStruct((t, d), F32),
        compiler_params=_params(("arbitrary",)),
        name="moe_combine",
    )(dest1, dest2, h, gates, ln_w.reshape(1, d), ln_b.reshape(1, d), yb)


def _invert_kernel(d1_ref, d2_ref, init_ref, ids_ref, sem, *, t):
    fill = pltpu.make_async_copy(init_ref, ids_ref, sem)
    fill.start()
    fill.wait()

    def body(tok, carry):
        ids_ref[d1_ref[tok]] = tok
        ids_ref[d2_ref[tok]] = tok + t
        return carry

    lax.fori_loop(0, t, body, 0, unroll=8)


def _invert(dest1, dest2, n_rows):
    t = dest1.shape[0]
    smem = pl.BlockSpec(memory_space=pltpu.SMEM)
    return pl.pallas_call(
        functools.partial(_invert_kernel, t=t),
        in_specs=[smem, smem, pl.BlockSpec(memory_space=pl.ANY)],
        out_specs=smem,
        out_shape=jax.ShapeDtypeStruct((n_rows,), jnp.int32),
        scratch_shapes=[pltpu.SemaphoreType.DMA(())],
        name="moe_invert",
    )(dest1, dest2, jnp.full((n_rows,), -1, jnp.int32))


def _expert_fused_kernel(be_ref, src_ref, dst_ref, h_ref, wg_ref, wu_ref, wd_ref, yk_ref,
                         xbuf, ybuf, wgb_ref, wub_ref, wdb_ref, sem_g, sem_s):
    i = pl.program_id(0)
    nb = pl.num_programs(0)
    slot = i % 2
    other = 1 - slot

    def gather(block, buf):
        base = block * MOE_ROWS
        for r in range(MOE_ROWS):
            _row_copy(h_ref, src_ref[base + r], xbuf.at[buf], r, sem_g.at[buf]).start()

    def scatter(block, buf):
        base = block * MOE_ROWS
        for r in range(MOE_ROWS):
            _row_copy(ybuf.at[buf], r, yk_ref, dst_ref[base + r], sem_s.at[buf]).start(priority=1)

    def wait_gather(buf):
        pltpu.make_async_copy(h_ref.at[pl.ds(0, MOE_ROWS)], xbuf.at[buf], sem_g.at[buf]).wait()

    def wait_scatter(buf):
        pltpu.make_async_copy(ybuf.at[buf], yk_ref.at[pl.ds(0, MOE_ROWS)], sem_s.at[buf]).wait()

    @pl.when(i == 0)
    def _():
        ybuf[...] = jnp.zeros_like(ybuf)
        gather(0, 0)

    wait_gather(slot)

    @pl.when(i >= 1)
    def _():
        wait_scatter(slot)

    changed = jnp.logical_or(i == 0, be_ref[i] != be_ref[jnp.maximum(i - 1, 0)])

    @pl.when(changed)
    def _():
        wgb_ref[...] = wg_ref[...].astype(BF16)
        wub_ref[...] = wu_ref[...].astype(BF16)
        wdb_ref[...] = wd_ref[...].astype(BF16)

    gather(jnp.minimum(i + 1, nb - 1), other)
    scatter(jnp.maximum(i - 1, 0), other)
    x = xbuf[slot].astype(BF16)
    gate = jnp.dot(x, wgb_ref[...], preferred_element_type=F32)
    up = jnp.dot(x, wub_ref[...], preferred_element_type=F32)
    hid = (_silu(gate) * up).astype(BF16)
    ybuf[slot] = jnp.dot(hid, wdb_ref[...], preferred_element_type=F32)

    @pl.when(i == nb - 1)
    def _():
        scatter(i, slot)
        wait_gather(other)
        wait_scatter(other)
        wait_scatter(slot)


def _experts_fused(h, src_tok, dst_row, block_e, w_gate, w_up, w_down, layer):
    t, d = h.shape
    de = w_gate.shape[3]
    p = src_tok.shape[0]
    nb = p // MOE_ROWS
    weight = lambda shape: pl.BlockSpec((None, None) + shape, lambda i, be, src, dst: (layer, be[i], 0, 0))
    return pl.pallas_call(
        _expert_fused_kernel,
        grid_spec=pltpu.PrefetchScalarGridSpec(
            num_scalar_prefetch=3,
            grid=(nb,),
            in_specs=[pl.BlockSpec(memory_space=pl.ANY), weight((d, de)), weight((d, de)), weight((de, d))],
            out_specs=pl.BlockSpec(memory_space=pl.ANY),
            scratch_shapes=[pltpu.VMEM((2, MOE_ROWS, d), F32), pltpu.VMEM((2, MOE_ROWS, d), F32),
                            pltpu.VMEM((d, de), BF16), pltpu.VMEM((d, de), BF16), pltpu.VMEM((de, d), BF16),
                            pltpu.SemaphoreType.DMA((2,)), pltpu.SemaphoreType.DMA((2,))],
        ),
        out_shape=jax.ShapeDtypeStruct((p, d), F32),
        compiler_params=_params(("arbitrary",)),
        name="moe_experts",
    )(block_e, src_tok, dst_row, h, w_gate, w_up, w_down)


def _combine_dense_kernel(h_ref, y0_ref, y1_ref, gates_ref, lnw_ref, lnb_ref, o_ref, *, alpha):
    y = gates_ref[:, 0:1] * y0_ref[...] + gates_ref[:, 1:2] * y1_ref[...]
    o_ref[...] = _layer_norm(alpha * h_ref[...] + y, lnw_ref[...], lnb_ref[...])


def _combine_dense(h, yk, gates, ln_w, ln_b, alpha):
    t, d = h.shape
    tm = TOKEN_ROWS
    nt = t // tm
    row = lambda i: (i, 0)
    const = lambda i: (0, 0)
    return pl.pallas_call(
        functools.partial(_combine_dense_kernel, alpha=alpha),
        grid=(nt,),
        in_specs=[pl.BlockSpec((tm, d), row), pl.BlockSpec((tm, d), row),
                  pl.BlockSpec((tm, d), lambda i: (nt + i, 0)), pl.BlockSpec((tm, 2), row),
                  pl.BlockSpec((1, d), const), pl.BlockSpec((1, d), const)],
        out_specs=pl.BlockSpec((tm, d), row),
        out_shape=jax.ShapeDtypeStruct((t, d), F32),
        compiler_params=_params(("arbitrary",)),
        name="moe_combine",
    )(h, yk, yk, gates, ln_w.reshape(1, d), ln_b.reshape(1, d))


def _moe_fused(h1, info, counts, w_gate, w_up, w_down, layer, ln_w, ln_b, alpha):
    t = h1.shape[0]
    n_rows = 2 * t + N_EXPERTS * MOE_ROWS
    nb = n_rows // MOE_ROWS
    cnt = counts[:, 0].astype(jnp.int32)
    padded = (cnt + MOE_ROWS - 1) // MOE_ROWS * MOE_ROWS
    ends = jnp.cumsum(padded)
    pstart = ends - padded
    e1 = info[0].astype(jnp.int32)
    e2 = info[1].astype(jnp.int32)
    expert_ids = jnp.arange(N_EXPERTS, dtype=jnp.int32)[:, None]
    dest1 = jnp.sum(jnp.where(e1[None, :] == expert_ids, pstart[:, None], 0), axis=0) + info[4].astype(jnp.int32)
    dest2 = jnp.sum(jnp.where(e2[None, :] == expert_ids, pstart[:, None], 0), axis=0) + info[5].astype(jnp.int32)
    gates = jnp.stack([info[2], info[3]], axis=1)
    blk = jnp.arange(nb, dtype=jnp.int32) * MOE_ROWS
    block_e = jnp.minimum(jnp.sum((ends[None, :] <= blk[:, None]).astype(jnp.int32), axis=1), N_EXPERTS - 1)
    ids = _invert(dest1, dest2, n_rows)
    is_pad = ids < 0
    src_tok = jnp.where(is_pad, 0, ids % t)
    dst_row = jnp.where(is_pad, 2 * t - 1 + jnp.cumsum(is_pad.astype(jnp.int32)), ids)
    yk = _experts_fused(h1, src_tok, dst_row, block_e, w_gate, w_up, w_down, layer)
    return _combine_dense(h1, yk, gates, ln_w, ln_b, alpha)


def _clear_padding_blocks(zs_ref, xb_ref, zero_ref, sem):
    zero_ref[...] = jnp.zeros_like(zero_ref)

    def clear(row):
        start = pl.multiple_of(row, MOE_ROWS)
        return pltpu.make_async_copy(zero_ref, xb_ref.at[pl.ds(start, MOE_ROWS)], sem)

    for e in range(N_EXPERTS):
        clear(zs_ref[e]).start()
    for e in range(N_EXPERTS):
        clear(zs_ref[e]).wait()

    def clear_tail(b, carry):
        clear(b * MOE_ROWS).start()
        clear(b * MOE_ROWS).wait()
        return carry

    lax.fori_loop(zs_ref[N_EXPERTS], xb_ref.shape[0] // MOE_ROWS, clear_tail, 0)


def _run_copy(src_ref, src_row, dst_ref, dst_row, sem):
    src = pl.multiple_of(src_row, SUBLANES)
    dst = pl.multiple_of(dst_row, SUBLANES)
    return pltpu.make_async_copy(src_ref.at[pl.ds(src, SUBLANES)], dst_ref.at[pl.ds(dst, SUBLANES)], sem)


def _sort_dispatch_kernel(dst_ref, nch_ref, zs_ref, h_ref, lp_ref, xb_ref, xs_ref, zero_ref, sem, *, tm, nck):
    i = pl.program_id(0)

    @pl.when(i == 0)
    def _():
        _clear_padding_blocks(zs_ref, xb_ref, zero_ref, sem.at[0])

    slot = i % 2
    last = pl.num_programs(0) - 1

    def wait_runs(buf, count):
        def wait(j, carry):
            _run_copy(xs_ref.at[buf], 0, xb_ref, 0, sem.at[buf]).wait()
            return carry

        lax.fori_loop(0, count, wait, 0)

    @pl.when(i >= 2)
    def _():
        wait_runs(slot, nch_ref[jnp.maximum(i - 2, 0)])

    lp = lp_ref[...]
    pos = lax.broadcasted_iota(jnp.int32, (xs_ref.shape[1], tm), 0)
    perm = jnp.logical_or(pos == lp[0:1, :], pos == lp[1:2, :]).astype(BF16)
    xs_ref[slot] = jnp.dot(perm, h_ref[...].astype(BF16), preferred_element_type=F32)

    def start(j, carry):
        _run_copy(xs_ref.at[slot], j * SUBLANES, xb_ref, dst_ref[i * nck + j], sem.at[slot]).start()
        return carry

    lax.fori_loop(0, nch_ref[i], start, 0)

    @pl.when(i == last)
    def _():
        @pl.when(i >= 1)
        def _():
            wait_runs(1 - slot, nch_ref[jnp.maximum(i - 1, 0)])

        wait_runs(slot, nch_ref[i])


def _sort_dispatch(h, lp_rows, chunk_dst, n_chunks, zero_start, n_rows, tm, sorted_rows):
    t, d = h.shape
    nck = sorted_rows // SUBLANES
    return pl.pallas_call(
        functools.partial(_sort_dispatch_kernel, tm=tm, nck=nck),
        grid_spec=pltpu.PrefetchScalarGridSpec(
            num_scalar_prefetch=3,
            grid=(t // tm,),
            in_specs=[pl.BlockSpec((tm, d), lambda i, a, b, c: (i, 0)),
                      pl.BlockSpec((2, tm), lambda i, a, b, c: (0, i))],
            out_specs=pl.BlockSpec(memory_space=pl.ANY),
            scratch_shapes=[pltpu.VMEM((2, sorted_rows, d), F32), pltpu.VMEM((MOE_ROWS, d), F32),
                            pltpu.SemaphoreType.DMA((2,))],
        ),
        out_shape=jax.ShapeDtypeStruct((n_rows, d), F32),
        compiler_params=_params(("arbitrary",)),
        name="moe_dispatch",
    )(chunk_dst, n_chunks, zero_start, h, lp_rows)


def _sort_combine_kernel(src_ref, nch_ref, h_ref, lp_ref, gates_ref, lnw_ref, lnb_ref, yb_ref, o_ref,
                         ys_ref, sem, *, tm, nck, alpha):
    i = pl.program_id(0)

    slot = i % 2

    def fetch(tile, buf):
        def start(j, carry):
            _run_copy(yb_ref, src_ref[tile * nck + j], ys_ref.at[buf], j * SUBLANES, sem.at[buf]).start()
            return carry

        lax.fori_loop(0, nch_ref[tile], start, 0)

    @pl.when(i == 0)
    def _():
        ys_ref[...] = jnp.zeros_like(ys_ref)
        fetch(0, 0)

    @pl.when(i + 1 < pl.num_programs(0))
    def _():
        fetch(i + 1, 1 - slot)

    def wait(j, carry):
        _run_copy(yb_ref, 0, ys_ref.at[slot], 0, sem.at[slot]).wait()
        return carry

    lax.fori_loop(0, nch_ref[i], wait, 0)

    lp = lp_ref[...]
    pos = lax.broadcasted_iota(jnp.int32, (tm, ys_ref.shape[1]), 1)
    ys = ys_ref[slot].astype(BF16)
    y1 = jnp.dot((pos == lp[:, 0:1]).astype(BF16), ys, preferred_element_type=F32)
    y2 = jnp.dot((pos == lp[:, 1:2]).astype(BF16), ys, preferred_element_type=F32)
    y = gates_ref[:, 0:1] * y1 + gates_ref[:, 1:2] * y2
    o_ref[...] = _layer_norm(alpha * h_ref[...] + y, lnw_ref[...], lnb_ref[...])


def _sort_combine(h, yb, lp_cols, gates, chunk_src, n_chunks, ln_w, ln_b, alpha, tm, sorted_rows):
    t, d = h.shape
    nck = sorted_rows // SUBLANES
    row = lambda i, a, b: (i, 0)
    const = lambda i, a, b: (0, 0)
    return pl.pallas_call(
        functools.partial(_sort_combine_kernel, tm=tm, nck=nck, alpha=alpha),
        grid_spec=pltpu.PrefetchScalarGridSpec(
            num_scalar_prefetch=2,
            grid=(t // tm,),
            in_specs=[pl.BlockSpec((tm, d), row), pl.BlockSpec((tm, 2), row), pl.BlockSpec((tm, 2), row),
                      pl.BlockSpec((1, d), const), pl.BlockSpec((1, d), const),
                      pl.BlockSpec(memory_space=pl.ANY)],
            out_specs=pl.BlockSpec((tm, d), row),
            scratch_shapes=[pltpu.VMEM((2, sorted_rows, d), F32), pltpu.SemaphoreType.DMA((2,))],
        ),
        out_shape=jax.ShapeDtypeStruct((t, d), F32),
        compiler_params=_params(("arbitrary",)),
        name="moe_combine",
    )(chunk_src, n_chunks, h, lp_cols, gates, ln_w.reshape(1, d), ln_b.reshape(1, d), yb)


def _moe_sorted(h1, info, w_gate, w_up, w_down, layer, ln_w, ln_b, alpha):
    t = h1.shape[0]
    tm = SORT_ROWS
    ntile = t // tm
    run_pad = SUBLANES - 1
    sorted_rows = 2 * tm + LANES
    nck = sorted_rows // SUBLANES
    n_rows = (2 * t + ntile * N_EXPERTS * run_pad + MOE_ROWS - 1) // MOE_ROWS * MOE_ROWS + N_EXPERTS * MOE_ROWS
    nb = n_rows // MOE_ROWS
    i32 = jnp.int32
    e1, e2 = info[0].astype(i32), info[1].astype(i32)
    rank1, rank2 = info[4].astype(i32), info[5].astype(i32)
    expert_ids = jnp.arange(N_EXPERTS, dtype=i32)[None, :]
    oh1 = e1[:, None] == expert_ids
    oh2 = e2[:, None] == expert_ids
    cnt = jnp.logical_or(oh1, oh2).astype(i32).reshape(ntile, tm, N_EXPERTS).sum(axis=1)
    cnt8 = (cnt + run_pad) // SUBLANES * SUBLANES
    local = jnp.cumsum(cnt8, axis=1) - cnt8
    before = jnp.cumsum(cnt, axis=0) - cnt
    seg = cnt8.sum(axis=0)
    seg_pad = (seg + MOE_ROWS - 1) // MOE_ROWS * MOE_ROWS
    ends = jnp.cumsum(seg_pad)
    slot = (ends - seg_pad)[None, :] + jnp.cumsum(cnt8, axis=0) - cnt8
    shift = jnp.repeat(local - before, tm, axis=0)
    lp1 = jnp.sum(jnp.where(oh1, shift, 0), axis=1) + rank1
    lp2 = jnp.sum(jnp.where(oh2, shift, 0), axis=1) + rank2
    chunk_row = jnp.arange(nck, dtype=i32) * SUBLANES
    run_of = jnp.sum(((local + cnt8)[:, None, :] <= chunk_row[None, :, None]).astype(i32), axis=2)
    run_of = jnp.minimum(run_of, N_EXPERTS - 1)
    chunk_slot = jnp.sum(jnp.where(run_of[:, :, None] == expert_ids[None], (slot - local)[:, None, :], 0),
                         axis=2) + chunk_row[None, :]
    n_chunks = cnt8.sum(axis=1) // SUBLANES
    n_valid = (ends[-1] // MOE_ROWS).astype(i32).reshape(1)
    blk = jnp.minimum(jnp.arange(nb, dtype=i32), n_valid[0] - 1) * MOE_ROWS
    block_e = jnp.minimum(jnp.sum((ends[None, :] <= blk[:, None]).astype(i32), axis=1), N_EXPERTS - 1)
    zero_start = jnp.concatenate([jnp.maximum(ends - MOE_ROWS, 0), n_valid]).astype(i32)
    gates = jnp.stack([info[2], info[3]], axis=1)
    chunk_slot = chunk_slot.reshape(-1).astype(i32)

    xb = _sort_dispatch(h1, jnp.stack([lp1, lp2], axis=0), chunk_slot, n_chunks, zero_start, n_rows, tm,
                        sorted_rows)
    yb = _experts(xb, block_e, n_valid, w_gate, w_up, w_down, layer)
    return _sort_combine(h1, yb, jnp.stack([lp1, lp2], axis=1), gates, chunk_slot, n_chunks, ln_w, ln_b,
                         alpha, tm, sorted_rows)


def _moe(h1, info, counts, w_gate, w_up, w_down, layer, ln_w, ln_b, alpha):
    t = h1.shape[0]
    n_rows = 2 * t + N_EXPERTS * MOE_ROWS
    nb = n_rows // MOE_ROWS
    cnt = counts[:, 0].astype(jnp.int32)
    padded = (cnt + MOE_ROWS - 1) // MOE_ROWS * MOE_ROWS
    ends = jnp.cumsum(padded)
    pstart = ends - padded
    e1 = info[0].astype(jnp.int32)
    e2 = info[1].astype(jnp.int32)
    expert_ids = jnp.arange(N_EXPERTS, dtype=jnp.int32)[:, None]
    dest1 = jnp.sum(jnp.where(e1[None, :] == expert_ids, pstart[:, None], 0), axis=0) + info[4].astype(jnp.int32)
    dest2 = jnp.sum(jnp.where(e2[None, :] == expert_ids, pstart[:, None], 0), axis=0) + info[5].astype(jnp.int32)
    gates = jnp.stack([info[2], info[3]], axis=1)
    n_valid = (ends[-1] // MOE_ROWS).astype(jnp.int32).reshape(1)
    blk = jnp.minimum(jnp.arange(nb, dtype=jnp.int32), n_valid[0] - 1) * MOE_ROWS
    block_e = jnp.minimum(jnp.sum((ends[None, :] <= blk[:, None]).astype(jnp.int32), axis=1), N_EXPERTS - 1)
    zero_start = jnp.concatenate([jnp.maximum(ends - MOE_ROWS, 0), n_valid]).astype(jnp.int32)

    xb = _dispatch(h1, dest1, dest2, zero_start, n_rows)
    yb = _experts(xb, block_e, n_valid, w_gate, w_up, w_down, layer)
    return _combine(h1, yb, dest1, dest2, gates, ln_w, ln_b, alpha)


def _rotary_column_order(w_in):
    d = w_in.shape[0]
    nq = RET_HEADS * RET_DK

    def perm(w):
        return w.reshape(d, RET_HEADS, RET_DK // 2, 2).transpose(0, 3, 1, 2).reshape(d, nq)

    return jnp.concatenate([perm(w_in[:, :nq]), perm(w_in[:, nq:2 * nq]), w_in[:, 2 * nq:]], axis=1)


def kernel(x, w_in_even, w_out_even, lru_conv_w, lru_conv_b, lru_gate_w, lru_gate_b, lru_lambda,
           w_in_odd, w_out_odd, hg_lower_bounds, hg_norm_w, ln_w, ln_b, router_w,
           moe_w_gate, moe_w_up, moe_w_down):
    batch, seq_len, d = x.shape
    depth = ln_w.shape[0]
    alpha = (2.0 * depth) ** 0.25
    t = batch * seq_len
    h = x.reshape(t, d)
    router_wt = router_w.T
    half = RET_DK // 2
    inv_freq = ROPE_BASE ** (-jnp.arange(0, RET_DK, 2, dtype=F32) / RET_DK)
    inv_freq = jnp.tile(inv_freq, LANES // half).reshape(1, LANES)
    nret = RET_HEADS * RET_DV

    for layer in range(depth):
        j = layer // 2
        if layer % 2 == 0:
            w_in = _rotary_column_order(w_in_even[j]).astype(BF16)
            n_bf = 2 * RET_HEADS * RET_DK + 2 * nret
            proj_b, proj_f = _project(h, w_in, inv_freq, n_bf, rotary=True, seq_len=seq_len)
            ret = _retention(proj_b, batch, seq_len)
            h_fwd, h_bwd = _rglru(proj_f, lru_conv_w[j], lru_conv_b[j], lru_gate_w[j], lru_gate_b[j],
                                  lru_lambda[j], batch, seq_len)
            w_out = w_out_even[j].astype(BF16)
            mixer_inputs, weights = (ret, h_fwd, h_bwd, proj_f), [w_out[:nret], w_out[nret:]]
        else:
            w = w_in_odd[j]
            w_in = jnp.concatenate([w[:, :2 * d], w[:, 4 * d:], w[:, 2 * d:4 * d]], axis=1).astype(BF16)
            proj_b, proj_f = _project(h, w_in, inv_freq, 3 * d, rotary=False, seq_len=seq_len)
            o_fwd, o_bwd = _gla(proj_b, proj_f, hg_lower_bounds, layer, batch, seq_len)
            mixer_inputs, weights = (o_fwd, o_bwd, proj_b, hg_norm_w[j]), [w_out_odd[j].astype(BF16)]
        h1, info, counts = _out_project(layer % 2 == 0, mixer_inputs, weights, h, ln_w[layer, 0],
                                        ln_b[layer, 0], router_wt, alpha)
        h = _moe_sorted(h1, info, moe_w_gate, moe_w_up, moe_w_down, layer,
                        ln_w[layer, 1], ln_b[layer, 1], alpha)
    return h.reshape(batch, seq_len, d)
```

```python
import functools
import math

import jax
import jax.numpy as jnp
from jax import lax
from jax.experimental import pallas as pl
from jax.experimental.pallas import tpu as pltpu

F32 = jnp.float32
BF16 = jnp.bfloat16

RET_HEADS = 4
RET_DK = 64
RET_DV = 128
RET_CHUNK = 128
ROPE_BASE = 10000.0
LRU_BLOCKS = 4
LRU_BW = 128
LRU_CONV = 4
LRU_C = 8.0
HG_HEADS = 8
HG_DK = 128
HG_DV = 128
N_EXPERTS = 16
N_GROUPS = 4
EXPERTS_PER_GROUP = 4
LN_EPS = 1e-5
RMS_EPS = 1e-6

LANES = 128
SUBLANES = 8
PROJ_ROWS = 512
PROJ_COLS = 512
RET_CHUNKS_PER_STEP = 4
LRU_ROWS = 512
GLA_CHUNK = 32
GLA_ROWS = 256
TOKEN_ROWS = 512
SORT_ROWS = 256
MOE_ROWS = 256
VMEM_LIMIT = 48 * 1024 * 1024

_NT = (((1,), (1,)), ((), ()))
_TN = (((0,), (0,)), ((), ()))


def _params(sem):
    return pltpu.CompilerParams(dimension_semantics=sem, vmem_limit_bytes=VMEM_LIMIT)


def _sigmoid(x):
    return 0.5 * jnp.tanh(0.5 * x) + 0.5


def _silu(x):
    return x * _sigmoid(x)


def _softplus(x):
    return jnp.maximum(x, 0.0) + jnp.log1p(jnp.exp(-jnp.abs(x)))


def _layer_norm(u, w, b):
    mu = jnp.mean(u, axis=-1, keepdims=True)
    d = u - mu
    var = jnp.mean(d * d, axis=-1, keepdims=True)
    return d * lax.rsqrt(var + LN_EPS) * w + b


def _split3(x):
    hi = x.astype(BF16)
    rem = x - hi.astype(F32)
    mid = rem.astype(BF16)
    lo = (rem - mid.astype(F32)).astype(BF16)
    return hi, mid, lo


def _proj_kernel(x_ref, w_ref, inv_ref, ob_ref, of_ref, *, rotary, seq_len, tm, n_bf):
    i = pl.program_id(0)
    xb = x_ref[...].astype(BF16)
    tn = PROJ_COLS
    for j in range(w_ref.shape[1] // tn):
        acc = jnp.dot(xb, w_ref[:, j * tn:(j + 1) * tn], preferred_element_type=F32)
        if rotary and j == 0:
            row = lax.broadcasted_iota(jnp.int32, (tm, 1), 0) + i * tm
            ang = (row % seq_len).astype(F32) * inv_ref[...]
            cos = jnp.cos(ang)
            sin = jnp.sin(ang)
            scale = RET_DK ** -0.5
            q1, q2 = acc[:, 0:128], acc[:, 128:256]
            k1, k2 = acc[:, 256:384], acc[:, 384:512]
            ob_ref[:, 0:128] = ((q1 * cos - q2 * sin) * scale).astype(BF16)
            ob_ref[:, 128:256] = ((q1 * sin + q2 * cos) * scale).astype(BF16)
            ob_ref[:, 256:384] = (k1 * cos - k2 * sin).astype(BF16)
            ob_ref[:, 384:512] = (k1 * sin + k2 * cos).astype(BF16)
        elif (j + 1) * tn <= n_bf:
            ob_ref[:, j * tn:(j + 1) * tn] = acc.astype(BF16)
        else:
            of_ref[:, j * tn - n_bf:(j + 1) * tn - n_bf] = acc


def _project(x, w_bf16, inv_freq, n_bf, *, rotary, seq_len):
    t, k = x.shape
    n = w_bf16.shape[1]
    tm = min(PROJ_ROWS, seq_len)
    return pl.pallas_call(
        functools.partial(_proj_kernel, rotary=rotary, seq_len=seq_len, tm=tm, n_bf=n_bf),
        grid=(t // tm,),
        in_specs=[
            pl.BlockSpec((tm, k), lambda i: (i, 0)),
            pl.BlockSpec((k, n), lambda i: (0, 0)),
            pl.BlockSpec((1, LANES), lambda i: (0, 0)),
        ],
        out_specs=[pl.BlockSpec((tm, n_bf), lambda i: (i, 0)),
                   pl.BlockSpec((tm, n - n_bf), lambda i: (i, 0))],
        out_shape=[jax.ShapeDtypeStruct((t, n_bf), BF16),
                   jax.ShapeDtypeStruct((t, n - n_bf), F32)],
        compiler_params=_params(("arbitrary",)),
        name="in_proj",
    )(x, w_bf16, inv_freq)


def _ret_log_gamma(head):
    out = jnp.full(head.shape, math.log1p(-(2.0 ** -5.0)), F32)
    for h in range(1, RET_HEADS):
        out = jnp.where(head == h, math.log1p(-(2.0 ** (-5.0 - h))), out)
    return out


def _ret_lane_head():
    lane = lax.broadcasted_iota(jnp.int32, (1, 2 * LANES), 1)
    return (lane % LANES) // (RET_DK // 2)


def _ret_state_mask():
    shape = (RET_HEADS * RET_DV, 2 * LANES)
    row_head = lax.broadcasted_iota(jnp.int32, shape, 0) // RET_DV
    col_head = (lax.broadcasted_iota(jnp.int32, shape, 1) % LANES) // (RET_DK // 2)
    return row_head == col_head


def _ret_bstate_kernel(k_ref, v_ref, sb_ref, s_ref, *, cps):
    c = RET_CHUNK

    @pl.when(pl.program_id(1) == 0)
    def _():
        s_ref[...] = jnp.zeros_like(s_ref)

    lg = _ret_log_gamma(_ret_lane_head())
    idx = lax.broadcasted_iota(jnp.int32, (c, 1), 0).astype(F32)
    k_decay = jnp.exp(lg * idx)
    chunk_decay = jnp.exp(lg * float(c))
    mask = _ret_state_mask()
    for cc in reversed(range(cps)):
        rows = slice(cc * c, (cc + 1) * c)
        sb_ref[0, cc] = s_ref[...].astype(BF16)
        kb = (k_ref[rows, :] * k_decay).astype(BF16)
        upd = lax.dot_general(v_ref[rows, :], kb, _TN, preferred_element_type=F32)
        s_ref[...] = s_ref[...] * chunk_decay + jnp.where(mask, upd, 0.0)


def _ret_out_kernel(q_ref, k_ref, v_ref, g_ref, sb_ref, o_ref, s_ref, *, cps):
    c = RET_CHUNK

    @pl.when(pl.program_id(1) == 0)
    def _():
        s_ref[...] = jnp.zeros_like(s_ref)

    lane_head = _ret_lane_head()
    lg = _ret_log_gamma(lane_head)
    idx = lax.broadcasted_iota(jnp.int32, (c, 1), 0).astype(F32)
    q_decay_f = jnp.exp(lg * (idx + 1.0))
    q_decay_b = jnp.exp(lg * (float(c) - idx))
    k_decay = jnp.exp(lg * (float(c) - 1.0 - idx))
    chunk_decay = jnp.exp(lg * float(c))
    mask = _ret_state_mask()
    ii = lax.broadcasted_iota(jnp.int32, (c, c), 0)
    jj = lax.broadcasted_iota(jnp.int32, (c, c), 1)
    dist = jnp.abs(ii - jj).astype(F32)
    intra_decay = [jnp.exp(math.log1p(-(2.0 ** (-5.0 - h))) * dist) for h in range(RET_HEADS)]

    for cc in range(cps):
        rows = slice(cc * c, (cc + 1) * c)
        q = q_ref[rows, :]
        k = k_ref[rows, :]
        v = v_ref[rows, :]
        qf = (q * q_decay_f).astype(BF16)
        qb = (q * q_decay_b).astype(BF16)
        cross = (lax.dot_general(qf, s_ref[...].astype(BF16), _NT, preferred_element_type=F32)
                 + lax.dot_general(qb, sb_ref[0, cc], _NT, preferred_element_type=F32))
        for h in range(RET_HEADS):
            qh = jnp.where(lane_head == h, q, jnp.zeros_like(q))
            s = lax.dot_general(qh, k, _NT, preferred_element_type=F32) * intra_decay[h]
            cols = slice(h * RET_DV, (h + 1) * RET_DV)
            o = jnp.dot(s.astype(BF16), v[:, cols], preferred_element_type=F32) + cross[:, cols]
            mu = jnp.mean(o, axis=-1, keepdims=True)
            d = o - mu
            var = jnp.mean(d * d, axis=-1, keepdims=True)
            gate = _silu(g_ref[rows, cols].astype(F32))
            o_ref[rows, cols] = (gate * (d * lax.rsqrt(var + LN_EPS))).astype(BF16)
        kf = (k * k_decay).astype(BF16)
        upd = lax.dot_general(v, kf, _TN, preferred_element_type=F32)
        s_ref[...] = s_ref[...] * chunk_decay + jnp.where(mask, upd, 0.0)


def _retention(proj, batch, seq_len):
    t = proj.shape[0]
    c = RET_CHUNK
    cps = min(RET_CHUNKS_PER_STEP, seq_len // c)
    rows = cps * c
    ns = seq_len // rows
    dv = RET_HEADS * RET_DV
    state_shape = (dv, 2 * LANES)
    rev = lambda b, n: b * ns + (ns - 1 - n)
    fwd = lambda b, n: b * ns + n
    sb = pl.pallas_call(
        functools.partial(_ret_bstate_kernel, cps=cps),
        grid=(batch, ns),
        in_specs=[
            pl.BlockSpec((rows, 2 * LANES), lambda b, n: (rev(b, n), 1)),
            pl.BlockSpec((rows, dv), lambda b, n: (rev(b, n), 1)),
        ],
        out_specs=pl.BlockSpec((1, cps) + state_shape, lambda b, n: (b, ns - 1 - n, 0, 0)),
        out_shape=jax.ShapeDtypeStruct((batch, ns * cps) + state_shape, BF16),
        scratch_shapes=[pltpu.VMEM(state_shape, F32)],
        compiler_params=_params(("arbitrary", "arbitrary")),
        name="ret_bstate",
    )(proj, proj)
    return pl.pallas_call(
        functools.partial(_ret_out_kernel, cps=cps),
        grid=(batch, ns),
        in_specs=[
            pl.BlockSpec((rows, 2 * LANES), lambda b, n: (fwd(b, n), 0)),
            pl.BlockSpec((rows, 2 * LANES), lambda b, n: (fwd(b, n), 1)),
            pl.BlockSpec((rows, dv), lambda b, n: (fwd(b, n), 1)),
            pl.BlockSpec((rows, dv), lambda b, n: (fwd(b, n), 2)),
            pl.BlockSpec((1, cps) + state_shape, lambda b, n: (b, n, 0, 0)),
        ],
        out_specs=pl.BlockSpec((rows, dv), lambda b, n: (fwd(b, n), 0)),
        out_shape=jax.ShapeDtypeStruct((t, dv), BF16),
        scratch_shapes=[pltpu.VMEM(state_shape, F32)],
        compiler_params=_params(("arbitrary", "arbitrary")),
        name="ret_out",
    )(proj, proj, proj, proj, sb)


def _lru_kernel(xfp_ref, xf_ref, xfn_ref, xbp_ref, xb_ref, xbn_ref, cw_ref, cb_ref, gw_ref, gb_ref,
                lam_ref, hf_ref, hb_ref, xx_ref, a_ref, b_ref, h_ref, *, nt, ts, batch):
    i = pl.program_id(0)
    halo = SUBLANES
    lo = LRU_CONV // 2

    @pl.when(i == 0)
    def _():
        h_ref[...] = jnp.zeros_like(h_ref)

    def prepare(xp_ref, x_ref, xn_ref, tile, z, slot):
        for b in range(batch):
            xx_ref[0:halo, :] = jnp.where(tile == 0, 0.0, xp_ref[b])
            xx_ref[halo:halo + ts, :] = x_ref[b]
            xx_ref[halo + ts:2 * halo + ts, :] = jnp.where(tile == nt - 1, 0.0, xn_ref[b])
            xc = cb_ref[...]
            for tap in range(LRU_CONV):
                xc = xc + cw_ref[tap:tap + 1, :] * xx_ref[pl.ds(halo - lo + tap, ts), :]
            for n in range(LRU_BLOCKS):
                cols = slice(n * LRU_BW, (n + 1) * LRU_BW)
                xn = xc[:, cols]
                g = (jnp.dot(xn.astype(BF16), gw_ref[z, n], preferred_element_type=F32)
                     + gb_ref[z, n:n + 1, :])
                r = _sigmoid(g[:, :LRU_BW])
                ig = _sigmoid(g[:, LRU_BW:])
                a = jnp.exp((-LRU_C) * r * _softplus(-lam_ref[z, :, cols]))
                a_ref[slot + b, :, cols] = a
                b_ref[slot + b, :, cols] = jnp.sqrt(1.0 - a * a) * (ig * xn)

    prepare(xfp_ref, xf_ref, xfn_ref, i, 0, 0)
    prepare(xbp_ref, xb_ref, xbn_ref, nt - 1 - i, 1, batch)

    def step(s, hs):
        out = []
        for k in range(2 * batch):
            row = s if k < batch else ts - 1 - s
            h = a_ref[k, pl.ds(row, 1), :] * hs[k] + b_ref[k, pl.ds(row, 1), :]
            if k < batch:
                hf_ref[k, pl.ds(row, 1), :] = h
            else:
                hb_ref[k - batch, pl.ds(row, 1), :] = h
            out.append(h)
        return tuple(out)

    hs = lax.fori_loop(0, ts, step, tuple(h_ref[k] for k in range(2 * batch)), unroll=8)
    for k in range(2 * batch):
        h_ref[k] = hs[k]


def _rglru(proj_f32, conv_w, conv_b, gate_w, gate_b, lam, batch, seq_len):
    w = LRU_BLOCKS * LRU_BW
    ts = min(LRU_ROWS, seq_len)
    nt = seq_len // ts
    rows8 = ts // SUBLANES
    last8 = seq_len // SUBLANES - 1
    x3 = proj_f32.reshape(batch, seq_len, proj_f32.shape[1])
    gw = jnp.concatenate([gate_w[:, 0], gate_w[:, 1]], axis=-1).astype(BF16)
    gb = jnp.concatenate([gate_b[:, 0], gate_b[:, 1]], axis=-1)
    bwd = lambda i: nt - 1 - i

    def tile_specs(tile):
        return [
            pl.BlockSpec((batch, SUBLANES, w), lambda i: (0, jnp.maximum(tile(i) * rows8 - 1, 0), 0)),
            pl.BlockSpec((batch, ts, w), lambda i: (0, tile(i), 0)),
            pl.BlockSpec((batch, SUBLANES, w), lambda i: (0, jnp.minimum((tile(i) + 1) * rows8, last8), 0)),
        ]

    full = lambda a: pl.BlockSpec(a.shape, lambda i: (0,) * a.ndim)
    cb = conv_b.reshape(1, w)
    lam3 = lam.reshape(2, 1, w)
    state = jax.ShapeDtypeStruct((batch, seq_len, w), F32)
    h_fwd, h_bwd = pl.pallas_call(
        functools.partial(_lru_kernel, nt=nt, ts=ts, batch=batch),
        grid=(nt,),
        in_specs=tile_specs(lambda i: i) + tile_specs(bwd) + [full(conv_w), full(cb), full(gw), full(gb),
                                                               full(lam3)],
        out_specs=[pl.BlockSpec((batch, ts, w), lambda i: (0, i, 0)),
                   pl.BlockSpec((batch, ts, w), lambda i: (0, bwd(i), 0))],
        out_shape=[state, state],
        scratch_shapes=[pltpu.VMEM((ts + 2 * SUBLANES, w), F32), pltpu.VMEM((2 * batch, ts, w), F32),
                        pltpu.VMEM((2 * batch, ts, w), F32), pltpu.VMEM((2 * batch, 1, w), F32)],
        compiler_params=_params(("arbitrary",)),
        name="lru_scan",
    )(x3, x3, x3, x3, x3, x3, conv_w, cb, gw, gb, lam3)
    return h_fwd.reshape(batch * seq_len, w), h_bwd.reshape(batch * seq_len, w)


def _gla_kernel(qf_ref, vf_ref, zf_ref, qb_ref, vb_ref, zb_ref, lbp_ref, of_ref, ob_ref, s_ref,
                *, ts, layer, batch):
    c = GLA_CHUNK
    nchunks = ts // c

    @pl.when(pl.program_id(0) == 0)
    def _():
        s_ref[...] = jnp.zeros_like(s_ref)

    p = lbp_ref[...]
    e = jnp.exp(p - jnp.max(p, axis=0, keepdims=True))
    sm = e / jnp.sum(e, axis=0, keepdims=True)
    lb = jnp.zeros((1, p.shape[1]), F32)
    for r in range(1, layer + 1):
        lb = lb + sm[r:r + 1, :]

    head_cols = [slice(h * HG_DK, (h + 1) * HG_DK) for h in range(HG_HEADS)]
    hc = HG_HEADS * c
    ii = lax.broadcasted_iota(jnp.int32, (c, c), 0)
    jj = lax.broadcasted_iota(jnp.int32, (c, c), 1)
    si = lax.broadcasted_iota(jnp.int32, (hc, hc), 0)
    sj = lax.broadcasted_iota(jnp.int32, (hc, hc), 1)
    same_head = si // c == sj // c
    mid = c // 2

    def stack(a):
        return jnp.concatenate([a[:, cols] for cols in head_cols], axis=0)

    tri = {False: (jj <= ii).astype(BF16), True: (jj >= ii).astype(BF16)}
    keep = {False: jnp.logical_and(same_head, sj <= si), True: jnp.logical_and(same_head, sj >= si)}
    chains = ([(qf_ref, vf_ref, zf_ref, of_ref, b, False) for b in range(batch)]
              + [(qb_ref, vb_ref, zb_ref, ob_ref, b, True) for b in range(batch)])

    for cc in range(nchunks):
        work = []
        for q_ref, v_ref, z_ref, o_ref, b, reverse in chains:
            rows = pl.ds(((nchunks - 1 - cc) if reverse else cc) * c, c)
            sig = _sigmoid(z_ref[b, rows, :])
            log_f = jnp.log(lb + (1.0 - lb) * sig)
            work.append(dict(rows=rows, reverse=reverse, o_ref=o_ref, b=b,
                             key=(1.0 - lb) * (1.0 - sig),
                             qs=_silu(q_ref[b, rows, :].astype(F32)) * (HG_DK ** -0.5),
                             v=v_ref[b, rows, :], split=_split3(log_f)))
        for w in work:
            f_hi, f_mid, f_lo = w["split"]
            t = tri[w["reverse"]]
            w["bcum"] = (jnp.dot(t, f_lo, preferred_element_type=F32)
                         + jnp.dot(t, f_mid, preferred_element_type=F32)
                         + jnp.dot(t, f_hi, preferred_element_type=F32))
        for w in work:
            bcum = w["bcum"]
            end = 0 if w["reverse"] else c - 1
            ref_row = bcum[mid:mid + 1, :]
            b_end = bcum[end:end + 1, :]
            w["qe"] = (w["qs"] * jnp.exp(bcum - ref_row)).astype(BF16)
            w["ke"] = (w["key"] * jnp.exp(ref_row - bcum)).astype(BF16)
            w["qd"] = (w["qs"] * jnp.exp(bcum)).astype(BF16)
            w["kd"] = (w["key"] * jnp.exp(b_end - bcum)).astype(BF16)
            w["decay"] = jnp.exp(b_end)
        for w in work:
            w["att"] = lax.dot_general(stack(w["qe"]), stack(w["ke"]), _NT, preferred_element_type=F32)
        for w in work:
            att = jnp.where(keep[w["reverse"]], w["att"], 0.0).astype(BF16)
            w["intra"] = jnp.dot(att, stack(w["v"]), preferred_element_type=F32)
        for k, w in enumerate(work):
            w["st"] = [s_ref[k, h] for h in range(HG_HEADS)]
            w["inter"] = [lax.dot_general(w["qd"][:, cols], w["st"][h].astype(BF16), _NT,
                                          preferred_element_type=F32) for h, cols in enumerate(head_cols)]
        for w in work:
            w["upd"] = [lax.dot_general(w["v"][:, cols], w["kd"][:, cols], _TN, preferred_element_type=F32)
                        for cols in head_cols]
        for k, w in enumerate(work):
            for h, cols in enumerate(head_cols):
                w["o_ref"][w["b"], w["rows"], cols] = w["intra"][h * c:(h + 1) * c, :] + w["inter"][h]
                s_ref[k, h] = w["st"][h] * w["decay"][:, cols] + w["upd"][h]


def _gla(proj_bf16, proj_f32, lower_bounds, layer, batch, seq_len):
    d = HG_HEADS * HG_DK
    ts = min(GLA_ROWS, seq_len)
    nt = seq_len // ts
    depth = lower_bounds.shape[0]
    pb = proj_bf16.reshape(batch, seq_len, proj_bf16.shape[1])
    pf = proj_f32.reshape(batch, seq_len, proj_f32.shape[1])
    bwd = lambda i: nt - 1 - i
    blk = lambda tile, col: pl.BlockSpec((batch, ts, d), lambda i: (0, tile(i), col))
    fwd = lambda i: i
    out = jax.ShapeDtypeStruct((batch, seq_len, d), F32)
    o_fwd, o_bwd = pl.pallas_call(
        functools.partial(_gla_kernel, ts=ts, layer=layer, batch=batch),
        grid=(nt,),
        in_specs=[blk(fwd, 0), blk(fwd, 1), blk(fwd, 0), blk(bwd, 0), blk(bwd, 1), blk(bwd, 1),
                  pl.BlockSpec((depth, d), lambda i: (0, 0))],
        out_specs=[blk(fwd, 0), blk(bwd, 0)],
        out_shape=[out, out],
        scratch_shapes=[pltpu.VMEM((2 * batch, HG_HEADS, HG_DV, HG_DK), F32)],
        compiler_params=_params(("arbitrary",)),
        name="gla_scan",
    )(pb, pb, pf, pb, pb, pf, lower_bounds)
    return o_fwd.reshape(batch * seq_len, d), o_bwd.reshape(batch * seq_len, d)


def _top2(p):
    v1 = jnp.maximum(jnp.maximum(p[0], p[1]), jnp.maximum(p[2], p[3]))
    i1 = jnp.where(p[0] == v1, 0, jnp.where(p[1] == v1, 1, jnp.where(p[2] == v1, 2, 3)))
    q = [jnp.where(i1 == k, -1.0, p[k]) for k in range(4)]
    v2 = jnp.maximum(jnp.maximum(q[0], q[1]), jnp.maximum(q[2], q[3]))
    i2 = jnp.where(q[0] == v2, 0, jnp.where(q[1] == v2, 1, jnp.where(q[2] == v2, 2, 3)))
    return v1, i1, v2, i2


def _outproj_kernel(*refs, even, alpha, tm):
    if even:
        ret_ref, hf_ref, hb_ref, gr_ref, w0_ref, w1_ref = refs[:6]
        rest = refs[6:]
        lru = ((hf_ref[...] + hb_ref[...]) * jax.nn.gelu(gr_ref[...])).astype(BF16)
        y = (jnp.dot(ret_ref[...], w0_ref[...], preferred_element_type=F32)
             + jnp.dot(lru, w1_ref[...], preferred_element_type=F32))
    else:
        of_ref, ob_ref, g_ref, nw_ref, w0_ref = refs[:5]
        rest = refs[5:]
        o = of_ref[...] + ob_ref[...]
        ms = jnp.mean(o * o, axis=-1, keepdims=True)
        mix = o * lax.rsqrt(ms + RMS_EPS) * nw_ref[...] * _silu(g_ref[...].astype(F32))
        y = jnp.dot(mix.astype(BF16), w0_ref[...], preferred_element_type=F32)
    h_ref, lnw_ref, lnb_ref, rw_ref, o_ref, info_ref, cnt_ref, carry_ref = rest
    i = pl.program_id(0)

    @pl.when(i == 0)
    def _():
        carry_ref[...] = jnp.zeros_like(carry_ref)

    h1 = _layer_norm(alpha * h_ref[...] + y, lnw_ref[...], lnb_ref[...])
    o_ref[...] = h1

    h_hi = h1.astype(BF16)
    h_lo = (h1 - h_hi.astype(F32)).astype(BF16)
    rw = rw_ref[...]
    r_hi = rw.astype(BF16)
    r_lo = (rw - r_hi.astype(F32)).astype(BF16)
    logits = (lax.dot_general(r_lo, h_hi, _NT, preferred_element_type=F32)
              + lax.dot_general(r_hi, h_lo, _NT, preferred_element_type=F32)
              + lax.dot_general(r_hi, h_hi, _NT, preferred_element_type=F32))
    ex = jnp.exp(logits - jnp.max(logits, axis=0, keepdims=True))
    probs = ex / jnp.sum(ex, axis=0, keepdims=True)
    best = None
    for g in range(N_GROUPS):
        rows = [probs[g * EXPERTS_PER_GROUP + k:g * EXPERTS_PER_GROUP + k + 1, :]
                for k in range(EXPERTS_PER_GROUP)]
        v1, i1, v2, i2 = _top2(rows)
        cand = (v1 + v2, v1, i1 + g * EXPERTS_PER_GROUP, v2, i2 + g * EXPERTS_PER_GROUP)
        if best is None:
            best = cand
        else:
            take = cand[0] > best[0]
            best = tuple(jnp.where(take, cn, bs) for cn, bs in zip(cand, best))
    _, v1, e1, v2, e2 = best
    denom = v1 + v2
    g1 = v1 / denom
    g2 = v2 / denom

    eid = lax.broadcasted_iota(jnp.int32, (N_EXPERTS, tm), 0)
    oh1 = (eid == e1).astype(F32)
    oh2 = (eid == e2).astype(F32)
    oh = oh1 + oh2
    tt = lax.broadcasted_iota(jnp.int32, (tm, tm), 0)
    uu = lax.broadcasted_iota(jnp.int32, (tm, tm), 1)
    before = (tt < uu).astype(BF16)
    base = carry_ref[:, 0:1] + jnp.dot(oh.astype(BF16), before, preferred_element_type=F32)
    rank1 = jnp.sum(oh1 * base, axis=0, keepdims=True)
    rank2 = jnp.sum(oh2 * base, axis=0, keepdims=True)
    carry_ref[...] = carry_ref[...] + jnp.sum(oh, axis=1, keepdims=True)
    cnt_ref[...] = carry_ref[...]
    zero = jnp.zeros_like(g1)
    info_ref[...] = jnp.concatenate(
        [e1.astype(F32), e2.astype(F32), g1, g2, rank1, rank2, zero, zero], axis=0)


def _out_project(even, mixer_inputs, weights_bf16, h, ln_w, ln_b, router_wt, alpha):
    t, d = h.shape
    tm = TOKEN_ROWS
    row = lambda i: (i, 0)
    const = lambda i: (0, 0)
    if even:
        ret, h_fwd, h_bwd, proj_f32 = mixer_inputs
        w = h_fwd.shape[1]
        in_specs = [pl.BlockSpec((tm, ret.shape[1]), row), pl.BlockSpec((tm, w), row),
                    pl.BlockSpec((tm, w), row), pl.BlockSpec((tm, w), lambda i: (i, 1))]
        args = [ret, h_fwd, h_bwd, proj_f32]
    else:
        o_fwd, o_bwd, proj_bf16, norm_w = mixer_inputs
        in_specs = [pl.BlockSpec((tm, d), row), pl.BlockSpec((tm, d), row),
                    pl.BlockSpec((tm, d), lambda i: (i, 2)), pl.BlockSpec((1, d), const)]
        args = [o_fwd, o_bwd, proj_bf16, norm_w.reshape(1, d)]
    in_specs += [pl.BlockSpec(wm.shape, const) for wm in weights_bf16]
    in_specs += [pl.BlockSpec((tm, d), row), pl.BlockSpec((1, d), const), pl.BlockSpec((1, d), const),
                 pl.BlockSpec((N_EXPERTS, d), const)]
    args += list(weights_bf16) + [h, ln_w.reshape(1, d), ln_b.reshape(1, d), router_wt]
    return pl.pallas_call(
        functools.partial(_outproj_kernel, even=even, alpha=alpha, tm=tm),
        grid=(t // tm,),
        in_specs=in_specs,
        out_specs=[pl.BlockSpec((tm, d), row),
                   pl.BlockSpec((SUBLANES, tm), lambda i: (0, i)),
                   pl.BlockSpec((N_EXPERTS, LANES), const)],
        out_shape=[jax.ShapeDtypeStruct((t, d), F32),
                   jax.ShapeDtypeStruct((SUBLANES, t), F32),
                   jax.ShapeDtypeStruct((N_EXPERTS, LANES), F32)],
        scratch_shapes=[pltpu.VMEM((N_EXPERTS, LANES), F32)],
        compiler_params=_params(("arbitrary",)),
        name="out_proj_router",
    )(*args)


def _row_copy(src_ref, src_row, dst_ref, dst_row, sem):
    return pltpu.make_async_copy(src_ref.at[pl.ds(src_row, 1)], dst_ref.at[pl.ds(dst_row, 1)], sem)


def _dispatch_kernel(d1_ref, d2_ref, zs_ref, h_ref, xb_ref, zero_ref, sem, *, tm):
    i = pl.program_id(0)
    base = i * tm

    @pl.when(i == 0)
    def _():
        zero_ref[...] = jnp.zeros_like(zero_ref)

        def clear(row):
            start = pl.multiple_of(row, MOE_ROWS)
            return pltpu.make_async_copy(zero_ref, xb_ref.at[pl.ds(start, MOE_ROWS)], sem.at[0])

        for e in range(N_EXPERTS):
            clear(zs_ref[e]).start()
        for e in range(N_EXPERTS):
            clear(zs_ref[e]).wait()

        def clear_tail(b, carry):
            clear(b * MOE_ROWS).start()
            clear(b * MOE_ROWS).wait()
            return carry

        lax.fori_loop(zs_ref[N_EXPERTS], xb_ref.shape[0] // MOE_ROWS, clear_tail, 0)

    def start(t, carry):
        _row_copy(h_ref, t, xb_ref, d1_ref[base + t], sem.at[0]).start()
        _row_copy(h_ref, t, xb_ref, d2_ref[base + t], sem.at[1]).start(priority=1)
        return carry

    lax.fori_loop(0, tm, start, 0, unroll=8)
    pltpu.make_async_copy(h_ref, xb_ref.at[pl.ds(0, tm)], sem.at[0]).wait()
    pltpu.make_async_copy(h_ref, xb_ref.at[pl.ds(0, tm)], sem.at[1]).wait()


def _dispatch(h, dest1, dest2, zero_start, n_rows):
    t, d = h.shape
    tm = TOKEN_ROWS
    return pl.pallas_call(
        functools.partial(_dispatch_kernel, tm=tm),
        grid_spec=pltpu.PrefetchScalarGridSpec(
            num_scalar_prefetch=3,
            grid=(t // tm,),
            in_specs=[pl.BlockSpec((tm, d), lambda i, d1, d2, zs: (i, 0))],
            out_specs=pl.BlockSpec(memory_space=pl.ANY),
            scratch_shapes=[pltpu.VMEM((MOE_ROWS, d), F32), pltpu.SemaphoreType.DMA((2,))],
        ),
        out_shape=jax.ShapeDtypeStruct((n_rows, d), F32),
        compiler_params=_params(("arbitrary",)),
        name="moe_dispatch",
    )(dest1, dest2, zero_start, h)


def _expert_kernel(be_ref, nv_ref, x_ref, wg_ref, wu_ref, wd_ref, o_ref, wgb_ref, wub_ref, wdb_ref):
    i = pl.program_id(0)
    changed = jnp.logical_or(i == 0, be_ref[i] != be_ref[jnp.maximum(i - 1, 0)])

    @pl.when(changed)
    def _():
        wgb_ref[...] = wg_ref[...].astype(BF16)
        wub_ref[...] = wu_ref[...].astype(BF16)
        wdb_ref[...] = wd_ref[...].astype(BF16)

    @pl.when(i < nv_ref[0])
    def _():
        x = x_ref[...].astype(BF16)
        gate = jnp.dot(x, wgb_ref[...], preferred_element_type=F32)
        up = jnp.dot(x, wub_ref[...], preferred_element_type=F32)
        hid = (_silu(gate) * up).astype(BF16)
        o_ref[...] = jnp.dot(hid, wdb_ref[...], preferred_element_type=F32)

    @pl.when(i >= nv_ref[0])
    def _():
        o_ref[...] = jnp.zeros_like(o_ref)


def _experts(xb, block_e, n_valid, w_gate, w_up, w_down, layer):
    p, d = xb.shape
    de = w_gate.shape[3]
    nb = p // MOE_ROWS
    return pl.pallas_call(
        _expert_kernel,
        grid_spec=pltpu.PrefetchScalarGridSpec(
            num_scalar_prefetch=2,
            grid=(nb,),
            in_specs=[
                pl.BlockSpec((MOE_ROWS, d), lambda i, be, nv: (jnp.maximum(jnp.minimum(i, nv[0] - 1), 0), 0)),
                pl.BlockSpec((None, None, d, de), lambda i, be, nv: (layer, be[i], 0, 0)),
                pl.BlockSpec((None, None, d, de), lambda i, be, nv: (layer, be[i], 0, 0)),
                pl.BlockSpec((None, None, de, d), lambda i, be, nv: (layer, be[i], 0, 0)),
            ],
            out_specs=pl.BlockSpec((MOE_ROWS, d), lambda i, be, nv: (i, 0)),
            scratch_shapes=[pltpu.VMEM((d, de), BF16), pltpu.VMEM((d, de), BF16),
                            pltpu.VMEM((de, d), BF16)],
        ),
        out_shape=jax.ShapeDtypeStruct((p, d), F32),
        compiler_params=_params(("arbitrary",)),
        name="moe_experts",
    )(block_e, n_valid, xb, w_gate, w_up, w_down)


def _combine_kernel(d1_ref, d2_ref, h_ref, gates_ref, lnw_ref, lnb_ref, yb_ref, o_ref, r_ref, sem,
                    *, tm, alpha):
    base = pl.program_id(0) * tm

    def start(t, carry):
        _row_copy(yb_ref, d1_ref[base + t], r_ref.at[0], t, sem.at[0]).start()
        _row_copy(yb_ref, d2_ref[base + t], r_ref.at[1], t, sem.at[1]).start(priority=1)
        return carry

    lax.fori_loop(0, tm, start, 0, unroll=8)
    pltpu.make_async_copy(yb_ref.at[pl.ds(0, tm)], r_ref.at[0], sem.at[0]).wait()
    pltpu.make_async_copy(yb_ref.at[pl.ds(0, tm)], r_ref.at[1], sem.at[1]).wait()
    y = gates_ref[:, 0:1] * r_ref[0] + gates_ref[:, 1:2] * r_ref[1]
    o_ref[...] = _layer_norm(alpha * h_ref[...] + y, lnw_ref[...], lnb_ref[...])


def _combine(h, yb, dest1, dest2, gates, ln_w, ln_b, alpha):
    t, d = h.shape
    tm = TOKEN_ROWS
    return pl.pallas_call(
        functools.partial(_combine_kernel, tm=tm, alpha=alpha),
        grid_spec=pltpu.PrefetchScalarGridSpec(
            num_scalar_prefetch=2,
            grid=(t // tm,),
            in_specs=[pl.BlockSpec((tm, d), lambda i, d1, d2: (i, 0)),
                      pl.BlockSpec((tm, 2), lambda i, d1, d2: (i, 0)),
                      pl.BlockSpec((1, d), lambda i, d1, d2: (0, 0)),
                      pl.BlockSpec((1, d), lambda i, d1, d2: (0, 0)),
                      pl.BlockSpec(memory_space=pl.ANY)],
            out_specs=pl.BlockSpec((tm, d), lambda i, d1, d2: (i, 0)),
            scratch_shapes=[pltpu.VMEM((2, tm, d), F32), pltpu.SemaphoreType.DMA((2,))],
        ),
        out_shape=jax.ShapeDtypeStruct((t, d), F32),
        compiler_params=_params(("arbitrary",)),
        name="moe_combine",
    )(dest1, dest2, h, gates, ln_w.reshape(1, d), ln_b.reshape(1, d), yb)


def _invert_kernel(d1_ref, d2_ref, init_ref, ids_ref, sem, *, t):
    fill = pltpu.make_async_copy(init_ref, ids_ref, sem)
    fill.start()
    fill.wait()

    def body(tok, carry):
        ids_ref[d1_ref[tok]] = tok
        ids_ref[d2_ref[tok]] = tok + t
        return carry

    lax.fori_loop(0, t, body, 0, unroll=8)


def _invert(dest1, dest2, n_rows):
    t = dest1.shape[0]
    smem = pl.BlockSpec(memory_space=pltpu.SMEM)
    return pl.pallas_call(
        functools.partial(_invert_kernel, t=t),
        in_specs=[smem, smem, pl.BlockSpec(memory_space=pl.ANY)],
        out_specs=smem,
        out_shape=jax.ShapeDtypeStruct((n_rows,), jnp.int32),
        scratch_shapes=[pltpu.SemaphoreType.DMA(())],
        name="moe_invert",
    )(dest1, dest2, jnp.full((n_rows,), -1, jnp.int32))


def _expert_fused_kernel(be_ref, src_ref, dst_ref, h_ref, wg_ref, wu_ref, wd_ref, yk_ref,
                         xbuf, ybuf, wgb_ref, wub_ref, wdb_ref, sem_g, sem_s):
    i = pl.program_id(0)
    nb = pl.num_programs(0)
    slot = i % 2
    other = 1 - slot

    def gather(block, buf):
        base = block * MOE_ROWS
        for r in range(MOE_ROWS):
            _row_copy(h_ref, src_ref[base + r], xbuf.at[buf], r, sem_g.at[buf]).start()

    def scatter(block, buf):
        base = block * MOE_ROWS
        for r in range(MOE_ROWS):
            _row_copy(ybuf.at[buf], r, yk_ref, dst_ref[base + r], sem_s.at[buf]).start(priority=1)

    def wait_gather(buf):
        pltpu.make_async_copy(h_ref.at[pl.ds(0, MOE_ROWS)], xbuf.at[buf], sem_g.at[buf]).wait()

    def wait_scatter(buf):
        pltpu.make_async_copy(ybuf.at[buf], yk_ref.at[pl.ds(0, MOE_ROWS)], sem_s.at[buf]).wait()

    @pl.when(i == 0)
    def _():
        ybuf[...] = jnp.zeros_like(ybuf)
        gather(0, 0)

    wait_gather(slot)

    @pl.when(i >= 1)
    def _():
        wait_scatter(slot)

    changed = jnp.logical_or(i == 0, be_ref[i] != be_ref[jnp.maximum(i - 1, 0)])

    @pl.when(changed)
    def _():
        wgb_ref[...] = wg_ref[...].astype(BF16)
        wub_ref[...] = wu_ref[...].astype(BF16)
        wdb_ref[...] = wd_ref[...].astype(BF16)

    gather(jnp.minimum(i + 1, nb - 1), other)
    scatter(jnp.maximum(i - 1, 0), other)
    x = xbuf[slot].astype(BF16)
    gate = jnp.dot(x, wgb_ref[...], preferred_element_type=F32)
    up = jnp.dot(x, wub_ref[...], preferred_element_type=F32)
    hid = (_silu(gate) * up).astype(BF16)
    ybuf[slot] = jnp.dot(hid, wdb_ref[...], preferred_element_type=F32)

    @pl.when(i == nb - 1)
    def _():
        scatter(i, slot)
        wait_gather(other)
        wait_scatter(other)
        wait_scatter(slot)


def _experts_fused(h, src_tok, dst_row, block_e, w_gate, w_up, w_down, layer):
    t, d = h.shape
    de = w_gate.shape[3]
    p = src_tok.shape[0]
    nb = p // MOE_ROWS
    weight = lambda shape: pl.BlockSpec((None, None) + shape, lambda i, be, src, dst: (layer, be[i], 0, 0))
    return pl.pallas_call(
        _expert_fused_kernel,
        grid_spec=pltpu.PrefetchScalarGridSpec(
            num_scalar_prefetch=3,
            grid=(nb,),
            in_specs=[pl.BlockSpec(memory_space=pl.ANY), weight((d, de)), weight((d, de)), weight((de, d))],
            out_specs=pl.BlockSpec(memory_space=pl.ANY),
            scratch_shapes=[pltpu.VMEM((2, MOE_ROWS, d), F32), pltpu.VMEM((2, MOE_ROWS, d), F32),
                            pltpu.VMEM((d, de), BF16), pltpu.VMEM((d, de), BF16), pltpu.VMEM((de, d), BF16),
                            pltpu.SemaphoreType.DMA((2,)), pltpu.SemaphoreType.DMA((2,))],
        ),
        out_shape=jax.ShapeDtypeStruct((p, d), F32),
        compiler_params=_params(("arbitrary",)),
        name="moe_experts",
    )(block_e, src_tok, dst_row, h, w_gate, w_up, w_down)


def _combine_dense_kernel(h_ref, y0_ref, y1_ref, gates_ref, lnw_ref, lnb_ref, o_ref, *, alpha):
    y = gates_ref[:, 0:1] * y0_ref[...] + gates_ref[:, 1:2] * y1_ref[...]
    o_ref[...] = _layer_norm(alpha * h_ref[...] + y, lnw_ref[...], lnb_ref[...])


def _combine_dense(h, yk, gates, ln_w, ln_b, alpha):
    t, d = h.shape
    tm = TOKEN_ROWS
    nt = t // tm
    row = lambda i: (i, 0)
    const = lambda i: (0, 0)
    return pl.pallas_call(
        functools.partial(_combine_dense_kernel, alpha=alpha),
        grid=(nt,),
        in_specs=[pl.BlockSpec((tm, d), row), pl.BlockSpec((tm, d), row),
                  pl.BlockSpec((tm, d), lambda i: (nt + i, 0)), pl.BlockSpec((tm, 2), row),
                  pl.BlockSpec((1, d), const), pl.BlockSpec((1, d), const)],
        out_specs=pl.BlockSpec((tm, d), row),
        out_shape=jax.ShapeDtypeStruct((t, d), F32),
        compiler_params=_params(("arbitrary",)),
        name="moe_combine",
    )(h, yk, yk, gates, ln_w.reshape(1, d), ln_b.reshape(1, d))


def _moe_fused(h1, info, counts, w_gate, w_up, w_down, layer, ln_w, ln_b, alpha):
    t = h1.shape[0]
    n_rows = 2 * t + N_EXPERTS * MOE_ROWS
    nb = n_rows // MOE_ROWS
    cnt = counts[:, 0].astype(jnp.int32)
    padded = (cnt + MOE_ROWS - 1) // MOE_ROWS * MOE_ROWS
    ends = jnp.cumsum(padded)
    pstart = ends - padded
    e1 = info[0].astype(jnp.int32)
    e2 = info[1].astype(jnp.int32)
    expert_ids = jnp.arange(N_EXPERTS, dtype=jnp.int32)[:, None]
    dest1 = jnp.sum(jnp.where(e1[None, :] == expert_ids, pstart[:, None], 0), axis=0) + info[4].astype(jnp.int32)
    dest2 = jnp.sum(jnp.where(e2[None, :] == expert_ids, pstart[:, None], 0), axis=0) + info[5].astype(jnp.int32)
    gates = jnp.stack([info[2], info[3]], axis=1)
    blk = jnp.arange(nb, dtype=jnp.int32) * MOE_ROWS
    block_e = jnp.minimum(jnp.sum((ends[None, :] <= blk[:, None]).astype(jnp.int32), axis=1), N_EXPERTS - 1)
    ids = _invert(dest1, dest2, n_rows)
    is_pad = ids < 0
    src_tok = jnp.where(is_pad, 0, ids % t)
    dst_row = jnp.where(is_pad, 2 * t - 1 + jnp.cumsum(is_pad.astype(jnp.int32)), ids)
    yk = _experts_fused(h1, src_tok, dst_row, block_e, w_gate, w_up, w_down, layer)
    return _combine_dense(h1, yk, gates, ln_w, ln_b, alpha)


def _pack_halves(x):
    n = x.shape[1] // 2
    hi = pltpu.bitcast(x[:, :n], jnp.uint32)
    lo = pltpu.bitcast(x[:, n:], jnp.uint32)
    return hi | (lo >> 16)


def _unpack_halves(p):
    hi = pltpu.bitcast(p & jnp.uint32(0xFFFF0000), F32)
    lo = pltpu.bitcast(p << 16, F32)
    return jnp.concatenate([hi.astype(BF16), lo.astype(BF16)], axis=1)


def _seg_expert_kernel(first_ref, count_ref, tail_ref, x_ref, wg_ref, wu_ref, wd_ref, y_ref,
                       xbuf, ybuf, wgb_ref, wub_ref, wdb_ref, semx, semy):
    e = pl.program_id(0)
    wgb_ref[...] = wg_ref[...].astype(BF16)
    wub_ref[...] = wu_ref[...].astype(BF16)
    wdb_ref[...] = wd_ref[...].astype(BF16)
    first = first_ref[e]
    count = count_ref[e]

    def rows(j):
        return pl.ds(pl.multiple_of((first + j) * MOE_ROWS, MOE_ROWS), MOE_ROWS)

    def fetch(j, buf):
        return pltpu.make_async_copy(x_ref.at[rows(j)], xbuf.at[buf], semx.at[buf])

    def put(j, buf):
        return pltpu.make_async_copy(ybuf.at[buf], y_ref.at[rows(j)], semy.at[buf])

    @pl.when(count > 0)
    def _():
        fetch(0, 0).start()

    def body(j, carry):
        buf = j % 2

        @pl.when(j + 1 < count)
        def _():
            fetch(j + 1, 1 - buf).start()

        fetch(j, buf).wait()

        @pl.when(j >= 2)
        def _():
            put(j - 2, buf).wait()

        x = _unpack_halves(xbuf[buf])
        gate = jnp.dot(x, wgb_ref[...], preferred_element_type=F32)
        up = jnp.dot(x, wub_ref[...], preferred_element_type=F32)
        hid = (_silu(gate) * up).astype(BF16)
        y = jnp.dot(hid, wdb_ref[...], preferred_element_type=F32)
        ybuf[buf] = _pack_halves(y.astype(BF16).astype(F32))
        put(j, buf).start()
        return carry

    lax.fori_loop(0, count, body, 0)

    @pl.when(count >= 2)
    def _():
        put(count - 2, count % 2).wait()

    @pl.when(count >= 1)
    def _():
        put(count - 1, (count - 1) % 2).wait()

    @pl.when(e == pl.num_programs(0) - 1)
    def _():
        ybuf[0] = jnp.zeros_like(ybuf[0])

        def clear(b, carry):
            cp = pltpu.make_async_copy(
                ybuf.at[0], y_ref.at[pl.ds(pl.multiple_of(b * MOE_ROWS, MOE_ROWS), MOE_ROWS)], semy.at[0])
            cp.start()
            cp.wait()
            return carry

        lax.fori_loop(tail_ref[0], y_ref.shape[0] // MOE_ROWS, clear, 0)


def _seg_experts(xb, seg_first, seg_count, n_valid, w_gate, w_up, w_down, layer):
    p, half = xb.shape
    d = 2 * half
    de = w_gate.shape[3]
    weight = lambda shape: pl.BlockSpec((None, None) + shape, lambda e, a, b, c: (layer, e, 0, 0))
    return pl.pallas_call(
        _seg_expert_kernel,
        grid_spec=pltpu.PrefetchScalarGridSpec(
            num_scalar_prefetch=3,
            grid=(N_EXPERTS,),
            in_specs=[pl.BlockSpec(memory_space=pl.ANY), weight((d, de)), weight((d, de)), weight((de, d))],
            out_specs=pl.BlockSpec(memory_space=pl.ANY),
            scratch_shapes=[pltpu.VMEM((2, MOE_ROWS, half), jnp.uint32),
                            pltpu.VMEM((2, MOE_ROWS, half), jnp.uint32),
                            pltpu.VMEM((d, de), BF16), pltpu.VMEM((d, de), BF16), pltpu.VMEM((de, d), BF16),
                            pltpu.SemaphoreType.DMA((2,)), pltpu.SemaphoreType.DMA((2,))],
        ),
        out_shape=jax.ShapeDtypeStruct((p, half), jnp.uint32),
        compiler_params=_params(("arbitrary",)),
        name="moe_experts",
    )(seg_first, seg_count, n_valid, xb, w_gate, w_up, w_down)


def _clear_padding_blocks(zs_ref, xb_ref, zero_ref, sem):
    zero_ref[...] = jnp.zeros_like(zero_ref)

    def clear(row):
        start = pl.multiple_of(row, MOE_ROWS)
        return pltpu.make_async_copy(zero_ref, xb_ref.at[pl.ds(start, MOE_ROWS)], sem)

    for e in range(N_EXPERTS):
        clear(zs_ref[e]).start()
    for e in range(N_EXPERTS):
        clear(zs_ref[e]).wait()

    def clear_tail(b, carry):
        clear(b * MOE_ROWS).start()
        clear(b * MOE_ROWS).wait()
        return carry

    lax.fori_loop(zs_ref[N_EXPERTS], xb_ref.shape[0] // MOE_ROWS, clear_tail, 0)


def _run_copy(src_ref, src_row, dst_ref, dst_row, sem):
    src = pl.multiple_of(src_row, SUBLANES)
    dst = pl.multiple_of(dst_row, SUBLANES)
    return pltpu.make_async_copy(src_ref.at[pl.ds(src, SUBLANES)], dst_ref.at[pl.ds(dst, SUBLANES)], sem)


def _sort_dispatch_kernel(dst_ref, nch_ref, zs_ref, h_ref, lp_ref, xb_ref, xs_ref, zero_ref, sem, *, tm, nck):
    i = pl.program_id(0)

    @pl.when(i == 0)
    def _():
        _clear_padding_blocks(zs_ref, xb_ref, zero_ref, sem.at[0])

    slot = i % 2
    last = pl.num_programs(0) - 1

    def wait_runs(buf, count):
        def wait(j, carry):
            _run_copy(xs_ref.at[buf], 0, xb_ref, 0, sem.at[buf]).wait()
            return carry

        lax.fori_loop(0, count, wait, 0)

    @pl.when(i >= 2)
    def _():
        wait_runs(slot, nch_ref[jnp.maximum(i - 2, 0)])

    lp = lp_ref[...]
    pos = lax.broadcasted_iota(jnp.int32, (xs_ref.shape[1], tm), 0)
    perm = jnp.logical_or(pos == lp[0:1, :], pos == lp[1:2, :]).astype(BF16)
    xs_ref[slot] = _pack_halves(jnp.dot(perm, h_ref[...].astype(BF16), preferred_element_type=F32))

    def start(j, carry):
        _run_copy(xs_ref.at[slot], j * SUBLANES, xb_ref, dst_ref[i * nck + j], sem.at[slot]).start()
        return carry

    lax.fori_loop(0, nch_ref[i], start, 0)

    @pl.when(i == last)
    def _():
        @pl.when(i >= 1)
        def _():
            wait_runs(1 - slot, nch_ref[jnp.maximum(i - 1, 0)])

        wait_runs(slot, nch_ref[i])


def _sort_dispatch(h, lp_rows, chunk_dst, n_chunks, zero_start, n_rows, tm, sorted_rows):
    t, d = h.shape
    nck = sorted_rows // SUBLANES
    return pl.pallas_call(
        functools.partial(_sort_dispatch_kernel, tm=tm, nck=nck),
        grid_spec=pltpu.PrefetchScalarGridSpec(
            num_scalar_prefetch=3,
            grid=(t // tm,),
            in_specs=[pl.BlockSpec((tm, d), lambda i, a, b, c: (i, 0)),
                      pl.BlockSpec((2, tm), lambda i, a, b, c: (0, i))],
            out_specs=pl.BlockSpec(memory_space=pl.ANY),
            scratch_shapes=[pltpu.VMEM((2, sorted_rows, d // 2), jnp.uint32),
                            pltpu.VMEM((MOE_ROWS, d // 2), jnp.uint32), pltpu.SemaphoreType.DMA((2,))],
        ),
        out_shape=jax.ShapeDtypeStruct((n_rows, d // 2), jnp.uint32),
        compiler_params=_params(("arbitrary",)),
        name="moe_dispatch",
    )(chunk_dst, n_chunks, zero_start, h, lp_rows)


def _sort_combine_kernel(src_ref, nch_ref, h_ref, lp_ref, gates_ref, lnw_ref, lnb_ref, yb_ref, o_ref,
                         ys_ref, sem, *, tm, nck, alpha):
    i = pl.program_id(0)

    slot = i % 2

    def fetch(tile, buf):
        def start(j, carry):
            _run_copy(yb_ref, src_ref[tile * nck + j], ys_ref.at[buf], j * SUBLANES, sem.at[buf]).start()
            return carry

        lax.fori_loop(0, nch_ref[tile], start, 0)

    @pl.when(i == 0)
    def _():
        ys_ref[...] = jnp.zeros_like(ys_ref)
        fetch(0, 0)

    @pl.when(i + 1 < pl.num_programs(0))
    def _():
        fetch(i + 1, 1 - slot)

    def wait(j, carry):
        _run_copy(yb_ref, 0, ys_ref.at[slot], 0, sem.at[slot]).wait()
        return carry

    lax.fori_loop(0, nch_ref[i], wait, 0)

    lp = lp_ref[...]
    pos = lax.broadcasted_iota(jnp.int32, (tm, ys_ref.shape[1]), 1)
    ys = _unpack_halves(ys_ref[slot])
    y1 = jnp.dot((pos == lp[:, 0:1]).astype(BF16), ys, preferred_element_type=F32)
    y2 = jnp.dot((pos == lp[:, 1:2]).astype(BF16), ys, preferred_element_type=F32)
    y = gates_ref[:, 0:1] * y1 + gates_ref[:, 1:2] * y2
    o_ref[...] = _layer_norm(alpha * h_ref[...] + y, lnw_ref[...], lnb_ref[...])


def _sort_combine(h, yb, lp_cols, gates, chunk_src, n_chunks, ln_w, ln_b, alpha, tm, sorted_rows):
    t, d = h.shape
    nck = sorted_rows // SUBLANES
    row = lambda i, a, b: (i, 0)
    const = lambda i, a, b: (0, 0)
    return pl.pallas_call(
        functools.partial(_sort_combine_kernel, tm=tm, nck=nck, alpha=alpha),
        grid_spec=pltpu.PrefetchScalarGridSpec(
            num_scalar_prefetch=2,
            grid=(t // tm,),
            in_specs=[pl.BlockSpec((tm, d), row), pl.BlockSpec((tm, 2), row), pl.BlockSpec((tm, 2), row),
                      pl.BlockSpec((1, d), const), pl.BlockSpec((1, d), const),
                      pl.BlockSpec(memory_space=pl.ANY)],
            out_specs=pl.BlockSpec((tm, d), row),
            scratch_shapes=[pltpu.VMEM((2, sorted_rows, d // 2), jnp.uint32), pltpu.SemaphoreType.DMA((2,))],
        ),
        out_shape=jax.ShapeDtypeStruct((t, d), F32),
        compiler_params=_params(("arbitrary",)),
        name="moe_combine",
    )(chunk_src, n_chunks, h, lp_cols, gates, ln_w.reshape(1, d), ln_b.reshape(1, d), yb)


def _moe_sorted(h1, info, w_gate, w_up, w_down, layer, ln_w, ln_b, alpha):
    t = h1.shape[0]
    tm = SORT_ROWS
    ntile = t // tm
    run_pad = SUBLANES - 1
    sorted_rows = 2 * tm + LANES
    nck = sorted_rows // SUBLANES
    n_rows = (2 * t + ntile * N_EXPERTS * run_pad + MOE_ROWS - 1) // MOE_ROWS * MOE_ROWS + N_EXPERTS * MOE_ROWS
    nb = n_rows // MOE_ROWS
    i32 = jnp.int32
    e1, e2 = info[0].astype(i32), info[1].astype(i32)
    rank1, rank2 = info[4].astype(i32), info[5].astype(i32)
    expert_ids = jnp.arange(N_EXPERTS, dtype=i32)[None, :]
    oh1 = e1[:, None] == expert_ids
    oh2 = e2[:, None] == expert_ids
    cnt = jnp.logical_or(oh1, oh2).astype(i32).reshape(ntile, tm, N_EXPERTS).sum(axis=1)
    cnt8 = (cnt + run_pad) // SUBLANES * SUBLANES
    local = jnp.cumsum(cnt8, axis=1) - cnt8
    before = jnp.cumsum(cnt, axis=0) - cnt
    seg = cnt8.sum(axis=0)
    seg_pad = (seg + MOE_ROWS - 1) // MOE_ROWS * MOE_ROWS
    ends = jnp.cumsum(seg_pad)
    slot = (ends - seg_pad)[None, :] + jnp.cumsum(cnt8, axis=0) - cnt8
    shift = jnp.repeat(local - before, tm, axis=0)
    lp1 = jnp.sum(jnp.where(oh1, shift, 0), axis=1) + rank1
    lp2 = jnp.sum(jnp.where(oh2, shift, 0), axis=1) + rank2
    chunk_row = jnp.arange(nck, dtype=i32) * SUBLANES
    run_of = jnp.sum(((local + cnt8)[:, None, :] <= chunk_row[None, :, None]).astype(i32), axis=2)
    run_of = jnp.minimum(run_of, N_EXPERTS - 1)
    chunk_slot = jnp.sum(jnp.where(run_of[:, :, None] == expert_ids[None], (slot - local)[:, None, :], 0),
                         axis=2) + chunk_row[None, :]
    n_chunks = cnt8.sum(axis=1) // SUBLANES
    n_valid = (ends[-1] // MOE_ROWS).astype(i32).reshape(1)
    zero_start = jnp.concatenate([jnp.maximum(ends - MOE_ROWS, 0), n_valid]).astype(i32)
    gates = jnp.stack([info[2], info[3]], axis=1)
    chunk_slot = chunk_slot.reshape(-1).astype(i32)
    seg_first = ((ends - seg_pad) // MOE_ROWS).astype(i32)
    seg_count = (seg_pad // MOE_ROWS).astype(i32)

    xb = _sort_dispatch(h1, jnp.stack([lp1, lp2], axis=0), chunk_slot, n_chunks, zero_start, n_rows, tm,
                        sorted_rows)
    yb = _seg_experts(xb, seg_first, seg_count, n_valid, w_gate, w_up, w_down, layer)
    return _sort_combine(h1, yb, jnp.stack([lp1, lp2], axis=1), gates, chunk_slot, n_chunks, ln_w, ln_b,
                         alpha, tm, sorted_rows)


def _moe(h1, info, counts, w_gate, w_up, w_down, layer, ln_w, ln_b, alpha):
    t = h1.shape[0]
    n_rows = 2 * t + N_EXPERTS * MOE_ROWS
    nb = n_rows // MOE_ROWS
    cnt = counts[:, 0].astype(jnp.int32)
    padded = (cnt + MOE_ROWS - 1) // MOE_ROWS * MOE_ROWS
    ends = jnp.cumsum(padded)
    pstart = ends - padded
    e1 = info[0].astype(jnp.int32)
    e2 = info[1].astype(jnp.int32)
    expert_ids = jnp.arange(N_EXPERTS, dtype=jnp.int32)[:, None]
    dest1 = jnp.sum(jnp.where(e1[None, :] == expert_ids, pstart[:, None], 0), axis=0) + info[4].astype(jnp.int32)
    dest2 = jnp.sum(jnp.where(e2[None, :] == expert_ids, pstart[:, None], 0), axis=0) + info[5].astype(jnp.int32)
    gates = jnp.stack([info[2], info[3]], axis=1)
    n_valid = (ends[-1] // MOE_ROWS).astype(jnp.int32).reshape(1)
    blk = jnp.minimum(jnp.arange(nb, dtype=jnp.int32), n_valid[0] - 1) * MOE_ROWS
    block_e = jnp.minimum(jnp.sum((ends[None, :] <= blk[:, None]).astype(jnp.int32), axis=1), N_EXPERTS - 1)
    zero_start = jnp.concatenate([jnp.maximum(ends - MOE_ROWS, 0), n_valid]).astype(jnp.int32)

    xb = _dispatch(h1, dest1, dest2, zero_start, n_rows)
    yb = _experts(xb, block_e, n_valid, w_gate, w_up, w_down, layer)
    return _combine(h1, yb, dest1, dest2, gates, ln_w, ln_b, alpha)


def _rotary_column_order(w_in):
    d = w_in.shape[0]
    nq = RET_HEADS * RET_DK

    def perm(w):
        return w.reshape(d, RET_HEADS, RET_DK // 2, 2).transpose(0, 3, 1, 2).reshape(d, nq)

    return jnp.concatenate([perm(w_in[:, :nq]), perm(w_in[:, nq:2 * nq]), w_in[:, 2 * nq:]], axis=1)


def kernel(x, w_in_even, w_out_even, lru_conv_w, lru_conv_b, lru_gate_w, lru_gate_b, lru_lambda,
           w_in_odd, w_out_odd, hg_lower_bounds, hg_norm_w, ln_w, ln_b, router_w,
           moe_w_gate, moe_w_up, moe_w_down):
    batch, seq_len, d = x.shape
    depth = ln_w.shape[0]
    alpha = (2.0 * depth) ** 0.25
    t = batch * seq_len
    h = x.reshape(t, d)
    router_wt = router_w.T
    half = RET_DK // 2
    inv_freq = ROPE_BASE ** (-jnp.arange(0, RET_DK, 2, dtype=F32) / RET_DK)
    inv_freq = jnp.tile(inv_freq, LANES // half).reshape(1, LANES)
    nret = RET_HEADS * RET_DV

    for layer in range(depth):
        j = layer // 2
        if layer % 2 == 0:
            w_in = _rotary_column_order(w_in_even[j]).astype(BF16)
            n_bf = 2 * RET_HEADS * RET_DK + 2 * nret
            proj_b, proj_f = _project(h, w_in, inv_freq, n_bf, rotary=True, seq_len=seq_len)
            ret = _retention(proj_b, batch, seq_len)
            h_fwd, h_bwd = _rglru(proj_f, lru_conv_w[j], lru_conv_b[j], lru_gate_w[j], lru_gate_b[j],
                                  lru_lambda[j], batch, seq_len)
            w_out = w_out_even[j].astype(BF16)
            mixer_inputs, weights = (ret, h_fwd, h_bwd, proj_f), [w_out[:nret], w_out[nret:]]
        else:
            w = w_in_odd[j]
            w_in = jnp.concatenate([w[:, :2 * d], w[:, 4 * d:], w[:, 2 * d:4 * d]], axis=1).astype(BF16)
            proj_b, proj_f = _project(h, w_in, inv_freq, 3 * d, rotary=False, seq_len=seq_len)
            o_fwd, o_bwd = _gla(proj_b, proj_f, hg_lower_bounds, layer, batch, seq_len)
            mixer_inputs, weights = (o_fwd, o_bwd, proj_b, hg_norm_w[j]), [w_out_odd[j].astype(BF16)]
        h1, info, counts = _out_project(layer % 2 == 0, mixer_inputs, weights, h, ln_w[layer, 0],
                                        ln_b[layer, 0], router_wt, alpha)
        h = _moe_sorted(h1, info, moe_w_gate, moe_w_up, moe_w_down, layer,
                        ln_w[layer, 1], ln_b[layer, 1], alpha)
    return h.reshape(batch, seq_len, d)
```

```python
import functools
import math

import jax
import jax.numpy as jnp
from jax import lax
from jax.experimental import pallas as pl
from jax.experimental.pallas import tpu as pltpu

F32 = jnp.float32
BF16 = jnp.bfloat16

RET_HEADS = 4
RET_DK = 64
RET_DV = 128
RET_CHUNK = 128
ROPE_BASE = 10000.0
LRU_BLOCKS = 4
LRU_BW = 128
LRU_CONV = 4
LRU_C = 8.0
HG_HEADS = 8
HG_DK = 128
HG_DV = 128
N_EXPERTS = 16
N_GROUPS = 4
EXPERTS_PER_GROUP = 4
LN_EPS = 1e-5
RMS_EPS = 1e-6

LANES = 128
SUBLANES = 8
PROJ_ROWS = 512
PROJ_COLS = 512
RET_CHUNKS_PER_STEP = 4
LRU_ROWS = 512
GLA_CHUNK = 32
GLA_ROWS = 256
TOKEN_ROWS = 512
SORT_ROWS = 256
MOE_ROWS = 512
VMEM_LIMIT = 48 * 1024 * 1024

_NT = (((1,), (1,)), ((), ()))
_TN = (((0,), (0,)), ((), ()))


def _params(sem):
    return pltpu.CompilerParams(dimension_semantics=sem, vmem_limit_bytes=VMEM_LIMIT)


def _sigmoid(x):
    return 0.5 * jnp.tanh(0.5 * x) + 0.5


def _silu(x):
    return x * _sigmoid(x)


def _softplus(x):
    return jnp.maximum(x, 0.0) + jnp.log1p(jnp.exp(-jnp.abs(x)))


def _layer_norm(u, w, b):
    mu = jnp.mean(u, axis=-1, keepdims=True)
    d = u - mu
    var = jnp.mean(d * d, axis=-1, keepdims=True)
    return d * lax.rsqrt(var + LN_EPS) * w + b


def _split3(x):
    hi = x.astype(BF16)
    rem = x - hi.astype(F32)
    mid = rem.astype(BF16)
    lo = (rem - mid.astype(F32)).astype(BF16)
    return hi, mid, lo


def _proj_kernel(x_ref, w_ref, inv_ref, ob_ref, of_ref, *, rotary, seq_len, tm, n_bf):
    i = pl.program_id(0)
    xb = x_ref[...].astype(BF16)
    tn = PROJ_COLS
    for j in range(w_ref.shape[1] // tn):
        acc = jnp.dot(xb, w_ref[:, j * tn:(j + 1) * tn], preferred_element_type=F32)
        if rotary and j == 0:
            row = lax.broadcasted_iota(jnp.int32, (tm, 1), 0) + i * tm
            ang = (row % seq_len).astype(F32) * inv_ref[...]
            cos = jnp.cos(ang)
            sin = jnp.sin(ang)
            scale = RET_DK ** -0.5
            q1, q2 = acc[:, 0:128], acc[:, 128:256]
            k1, k2 = acc[:, 256:384], acc[:, 384:512]
            ob_ref[:, 0:128] = ((q1 * cos - q2 * sin) * scale).astype(BF16)
            ob_ref[:, 128:256] = ((q1 * sin + q2 * cos) * scale).astype(BF16)
            ob_ref[:, 256:384] = (k1 * cos - k2 * sin).astype(BF16)
            ob_ref[:, 384:512] = (k1 * sin + k2 * cos).astype(BF16)
        elif (j + 1) * tn <= n_bf:
            ob_ref[:, j * tn:(j + 1) * tn] = acc.astype(BF16)
        else:
            of_ref[:, j * tn - n_bf:(j + 1) * tn - n_bf] = acc


def _project(x, w_bf16, inv_freq, n_bf, *, rotary, seq_len):
    t, k = x.shape
    n = w_bf16.shape[1]
    tm = min(PROJ_ROWS, seq_len)
    return pl.pallas_call(
        functools.partial(_proj_kernel, rotary=rotary, seq_len=seq_len, tm=tm, n_bf=n_bf),
        grid=(t // tm,),
        in_specs=[
            pl.BlockSpec((tm, k), lambda i: (i, 0)),
            pl.BlockSpec((k, n), lambda i: (0, 0)),
            pl.BlockSpec((1, LANES), lambda i: (0, 0)),
        ],
        out_specs=[pl.BlockSpec((tm, n_bf), lambda i: (i, 0)),
                   pl.BlockSpec((tm, n - n_bf), lambda i: (i, 0))],
        out_shape=[jax.ShapeDtypeStruct((t, n_bf), BF16),
                   jax.ShapeDtypeStruct((t, n - n_bf), F32)],
        compiler_params=_params(("arbitrary",)),
        name="in_proj",
    )(x, w_bf16, inv_freq)


def _ret_log_gamma(head):
    out = jnp.full(head.shape, math.log1p(-(2.0 ** -5.0)), F32)
    for h in range(1, RET_HEADS):
        out = jnp.where(head == h, math.log1p(-(2.0 ** (-5.0 - h))), out)
    return out


def _ret_lane_head():
    lane = lax.broadcasted_iota(jnp.int32, (1, 2 * LANES), 1)
    return (lane % LANES) // (RET_DK // 2)


def _ret_state_mask():
    shape = (RET_HEADS * RET_DV, 2 * LANES)
    row_head = lax.broadcasted_iota(jnp.int32, shape, 0) // RET_DV
    col_head = (lax.broadcasted_iota(jnp.int32, shape, 1) % LANES) // (RET_DK // 2)
    return row_head == col_head


def _ret_bstate_kernel(k_ref, v_ref, sb_ref, s_ref, *, cps):
    c = RET_CHUNK

    @pl.when(pl.program_id(1) == 0)
    def _():
        s_ref[...] = jnp.zeros_like(s_ref)

    lg = _ret_log_gamma(_ret_lane_head())
    idx = lax.broadcasted_iota(jnp.int32, (c, 1), 0).astype(F32)
    k_decay = jnp.exp(lg * idx)
    chunk_decay = jnp.exp(lg * float(c))
    mask = _ret_state_mask()
    for cc in reversed(range(cps)):
        rows = slice(cc * c, (cc + 1) * c)
        sb_ref[0, cc] = s_ref[...].astype(BF16)
        kb = (k_ref[rows, :] * k_decay).astype(BF16)
        upd = lax.dot_general(v_ref[rows, :], kb, _TN, preferred_element_type=F32)
        s_ref[...] = s_ref[...] * chunk_decay + jnp.where(mask, upd, 0.0)


def _ret_out_kernel(q_ref, k_ref, v_ref, g_ref, sb_ref, o_ref, s_ref, *, cps):
    c = RET_CHUNK

    @pl.when(pl.program_id(1) == 0)
    def _():
        s_ref[...] = jnp.zeros_like(s_ref)

    lane_head = _ret_lane_head()
    lg = _ret_log_gamma(lane_head)
    idx = lax.broadcasted_iota(jnp.int32, (c, 1), 0).astype(F32)
    q_decay_f = jnp.exp(lg * (idx + 1.0))
    q_decay_b = jnp.exp(lg * (float(c) - idx))
    k_decay = jnp.exp(lg * (float(c) - 1.0 - idx))
    chunk_decay = jnp.exp(lg * float(c))
    mask = _ret_state_mask()
    ii = lax.broadcasted_iota(jnp.int32, (c, c), 0)
    jj = lax.broadcasted_iota(jnp.int32, (c, c), 1)
    dist = jnp.abs(ii - jj).astype(F32)
    intra_decay = [jnp.exp(math.log1p(-(2.0 ** (-5.0 - h))) * dist) for h in range(RET_HEADS)]

    for cc in range(cps):
        rows = slice(cc * c, (cc + 1) * c)
        q = q_ref[rows, :]
        k = k_ref[rows, :]
        v = v_ref[rows, :]
        qf = (q * q_decay_f).astype(BF16)
        qb = (q * q_decay_b).astype(BF16)
        cross = (lax.dot_general(qf, s_ref[...].astype(BF16), _NT, preferred_element_type=F32)
                 + lax.dot_general(qb, sb_ref[0, cc], _NT, preferred_element_type=F32))
        for h in range(RET_HEADS):
            qh = jnp.where(lane_head == h, q, jnp.zeros_like(q))
            s = lax.dot_general(qh, k, _NT, preferred_element_type=F32) * intra_decay[h]
            cols = slice(h * RET_DV, (h + 1) * RET_DV)
            o = jnp.dot(s.astype(BF16), v[:, cols], preferred_element_type=F32) + cross[:, cols]
            mu = jnp.mean(o, axis=-1, keepdims=True)
            d = o - mu
            var = jnp.mean(d * d, axis=-1, keepdims=True)
            gate = _silu(g_ref[rows, cols].astype(F32))
            o_ref[rows, cols] = (gate * (d * lax.rsqrt(var + LN_EPS))).astype(BF16)
        kf = (k * k_decay).astype(BF16)
        upd = lax.dot_general(v, kf, _TN, preferred_element_type=F32)
        s_ref[...] = s_ref[...] * chunk_decay + jnp.where(mask, upd, 0.0)


def _retention(proj, batch, seq_len):
    t = proj.shape[0]
    c = RET_CHUNK
    cps = min(RET_CHUNKS_PER_STEP, seq_len // c)
    rows = cps * c
    ns = seq_len // rows
    dv = RET_HEADS * RET_DV
    state_shape = (dv, 2 * LANES)
    rev = lambda b, n: b * ns + (ns - 1 - n)
    fwd = lambda b, n: b * ns + n
    sb = pl.pallas_call(
        functools.partial(_ret_bstate_kernel, cps=cps),
        grid=(batch, ns),
        in_specs=[
            pl.BlockSpec((rows, 2 * LANES), lambda b, n: (rev(b, n), 1)),
            pl.BlockSpec((rows, dv), lambda b, n: (rev(b, n), 1)),
        ],
        out_specs=pl.BlockSpec((1, cps) + state_shape, lambda b, n: (b, ns - 1 - n, 0, 0)),
        out_shape=jax.ShapeDtypeStruct((batch, ns * cps) + state_shape, BF16),
        scratch_shapes=[pltpu.VMEM(state_shape, F32)],
        compiler_params=_params(("arbitrary", "arbitrary")),
        name="ret_bstate",
    )(proj, proj)
    return pl.pallas_call(
        functools.partial(_ret_out_kernel, cps=cps),
        grid=(batch, ns),
        in_specs=[
            pl.BlockSpec((rows, 2 * LANES), lambda b, n: (fwd(b, n), 0)),
            pl.BlockSpec((rows, 2 * LANES), lambda b, n: (fwd(b, n), 1)),
            pl.BlockSpec((rows, dv), lambda b, n: (fwd(b, n), 1)),
            pl.BlockSpec((rows, dv), lambda b, n: (fwd(b, n), 2)),
            pl.BlockSpec((1, cps) + state_shape, lambda b, n: (b, n, 0, 0)),
        ],
        out_specs=pl.BlockSpec((rows, dv), lambda b, n: (fwd(b, n), 0)),
        out_shape=jax.ShapeDtypeStruct((t, dv), BF16),
        scratch_shapes=[pltpu.VMEM(state_shape, F32)],
        compiler_params=_params(("arbitrary", "arbitrary")),
        name="ret_out",
    )(proj, proj, proj, proj, sb)


def _lru_kernel(xfp_ref, xf_ref, xfn_ref, xbp_ref, xb_ref, xbn_ref, cw_ref, cb_ref, gw_ref, gb_ref,
                lam_ref, hf_ref, hb_ref, xx_ref, a_ref, b_ref, h_ref, *, nt, ts, batch):
    i = pl.program_id(0)
    halo = SUBLANES
    lo = LRU_CONV // 2

    @pl.when(i == 0)
    def _():
        h_ref[...] = jnp.zeros_like(h_ref)

    def prepare(xp_ref, x_ref, xn_ref, tile, z, slot):
        for b in range(batch):
            xx_ref[0:halo, :] = jnp.where(tile == 0, 0.0, xp_ref[b])
            xx_ref[halo:halo + ts, :] = x_ref[b]
            xx_ref[halo + ts:2 * halo + ts, :] = jnp.where(tile == nt - 1, 0.0, xn_ref[b])
            xc = cb_ref[...]
            for tap in range(LRU_CONV):
                xc = xc + cw_ref[tap:tap + 1, :] * xx_ref[pl.ds(halo - lo + tap, ts), :]
            for n in range(LRU_BLOCKS):
                cols = slice(n * LRU_BW, (n + 1) * LRU_BW)
                xn = xc[:, cols]
                g = (jnp.dot(xn.astype(BF16), gw_ref[z, n], preferred_element_type=F32)
                     + gb_ref[z, n:n + 1, :])
                r = _sigmoid(g[:, :LRU_BW])
                ig = _sigmoid(g[:, LRU_BW:])
                a = jnp.exp((-LRU_C) * r * _softplus(-lam_ref[z, :, cols]))
                a_ref[slot + b, :, cols] = a
                b_ref[slot + b, :, cols] = jnp.sqrt(1.0 - a * a) * (ig * xn)

    prepare(xfp_ref, xf_ref, xfn_ref, i, 0, 0)
    prepare(xbp_ref, xb_ref, xbn_ref, nt - 1 - i, 1, batch)

    def step(s, hs):
        out = []
        for k in range(2 * batch):
            row = s if k < batch else ts - 1 - s
            h = a_ref[k, pl.ds(row, 1), :] * hs[k] + b_ref[k, pl.ds(row, 1), :]
            if k < batch:
                hf_ref[k, pl.ds(row, 1), :] = h
            else:
                hb_ref[k - batch, pl.ds(row, 1), :] = h
            out.append(h)
        return tuple(out)

    hs = lax.fori_loop(0, ts, step, tuple(h_ref[k] for k in range(2 * batch)), unroll=8)
    for k in range(2 * batch):
        h_ref[k] = hs[k]


def _rglru(proj_f32, conv_w, conv_b, gate_w, gate_b, lam, batch, seq_len):
    w = LRU_BLOCKS * LRU_BW
    ts = min(LRU_ROWS, seq_len)
    nt = seq_len // ts
    rows8 = ts // SUBLANES
    last8 = seq_len // SUBLANES - 1
    x3 = proj_f32.reshape(batch, seq_len, proj_f32.shape[1])
    gw = jnp.concatenate([gate_w[:, 0], gate_w[:, 1]], axis=-1).astype(BF16)
    gb = jnp.concatenate([gate_b[:, 0], gate_b[:, 1]], axis=-1)
    bwd = lambda i: nt - 1 - i

    def tile_specs(tile):
        return [
            pl.BlockSpec((batch, SUBLANES, w), lambda i: (0, jnp.maximum(tile(i) * rows8 - 1, 0), 0)),
            pl.BlockSpec((batch, ts, w), lambda i: (0, tile(i), 0)),
            pl.BlockSpec((batch, SUBLANES, w), lambda i: (0, jnp.minimum((tile(i) + 1) * rows8, last8), 0)),
        ]

    full = lambda a: pl.BlockSpec(a.shape, lambda i: (0,) * a.ndim)
    cb = conv_b.reshape(1, w)
    lam3 = lam.reshape(2, 1, w)
    state = jax.ShapeDtypeStruct((batch, seq_len, w), F32)
    h_fwd, h_bwd = pl.pallas_call(
        functools.partial(_lru_kernel, nt=nt, ts=ts, batch=batch),
        grid=(nt,),
        in_specs=tile_specs(lambda i: i) + tile_specs(bwd) + [full(conv_w), full(cb), full(gw), full(gb),
                                                               full(lam3)],
        out_specs=[pl.BlockSpec((batch, ts, w), lambda i: (0, i, 0)),
                   pl.BlockSpec((batch, ts, w), lambda i: (0, bwd(i), 0))],
        out_shape=[state, state],
        scratch_shapes=[pltpu.VMEM((ts + 2 * SUBLANES, w), F32), pltpu.VMEM((2 * batch, ts, w), F32),
                        pltpu.VMEM((2 * batch, ts, w), F32), pltpu.VMEM((2 * batch, 1, w), F32)],
        compiler_params=_params(("arbitrary",)),
        name="lru_scan",
    )(x3, x3, x3, x3, x3, x3, conv_w, cb, gw, gb, lam3)
    return h_fwd.reshape(batch * seq_len, w), h_bwd.reshape(batch * seq_len, w)


def _gla_kernel(qf_ref, vf_ref, zf_ref, qb_ref, vb_ref, zb_ref, lbp_ref, of_ref, ob_ref, s_ref,
                *, ts, layer, batch):
    c = GLA_CHUNK
    nchunks = ts // c

    @pl.when(pl.program_id(0) == 0)
    def _():
        s_ref[...] = jnp.zeros_like(s_ref)

    p = lbp_ref[...]
    e = jnp.exp(p - jnp.max(p, axis=0, keepdims=True))
    sm = e / jnp.sum(e, axis=0, keepdims=True)
    lb = jnp.zeros((1, p.shape[1]), F32)
    for r in range(1, layer + 1):
        lb = lb + sm[r:r + 1, :]

    head_cols = [slice(h * HG_DK, (h + 1) * HG_DK) for h in range(HG_HEADS)]
    hc = HG_HEADS * c
    ii = lax.broadcasted_iota(jnp.int32, (c, c), 0)
    jj = lax.broadcasted_iota(jnp.int32, (c, c), 1)
    si = lax.broadcasted_iota(jnp.int32, (hc, hc), 0)
    sj = lax.broadcasted_iota(jnp.int32, (hc, hc), 1)
    same_head = si // c == sj // c
    mid = c // 2

    def stack(a):
        return jnp.concatenate([a[:, cols] for cols in head_cols], axis=0)

    tri = {False: (jj <= ii).astype(BF16), True: (jj >= ii).astype(BF16)}
    keep = {False: jnp.logical_and(same_head, sj <= si), True: jnp.logical_and(same_head, sj >= si)}
    chains = ([(qf_ref, vf_ref, zf_ref, of_ref, b, False) for b in range(batch)]
              + [(qb_ref, vb_ref, zb_ref, ob_ref, b, True) for b in range(batch)])

    for cc in range(nchunks):
        work = []
        for q_ref, v_ref, z_ref, o_ref, b, reverse in chains:
            rows = pl.ds(((nchunks - 1 - cc) if reverse else cc) * c, c)
            sig = _sigmoid(z_ref[b, rows, :])
            log_f = jnp.log(lb + (1.0 - lb) * sig)
            work.append(dict(rows=rows, reverse=reverse, o_ref=o_ref, b=b,
                             key=(1.0 - lb) * (1.0 - sig),
                             qs=_silu(q_ref[b, rows, :].astype(F32)) * (HG_DK ** -0.5),
                             v=v_ref[b, rows, :], split=_split3(log_f)))
        for w in work:
            f_hi, f_mid, f_lo = w["split"]
            t = tri[w["reverse"]]
            w["bcum"] = (jnp.dot(t, f_lo, preferred_element_type=F32)
                         + jnp.dot(t, f_mid, preferred_element_type=F32)
                         + jnp.dot(t, f_hi, preferred_element_type=F32))
        for w in work:
            bcum = w["bcum"]
            end = 0 if w["reverse"] else c - 1
            ref_row = bcum[mid:mid + 1, :]
            b_end = bcum[end:end + 1, :]
            w["qe"] = (w["qs"] * jnp.exp(bcum - ref_row)).astype(BF16)
            w["ke"] = (w["key"] * jnp.exp(ref_row - bcum)).astype(BF16)
            w["qd"] = (w["qs"] * jnp.exp(bcum)).astype(BF16)
            w["kd"] = (w["key"] * jnp.exp(b_end - bcum)).astype(BF16)
            w["decay"] = jnp.exp(b_end)
        for w in work:
            w["att"] = lax.dot_general(stack(w["qe"]), stack(w["ke"]), _NT, preferred_element_type=F32)
        for w in work:
            att = jnp.where(keep[w["reverse"]], w["att"], 0.0).astype(BF16)
            w["intra"] = jnp.dot(att, stack(w["v"]), preferred_element_type=F32)
        for k, w in enumerate(work):
            w["st"] = [s_ref[k, h] for h in range(HG_HEADS)]
            w["inter"] = [lax.dot_general(w["qd"][:, cols], w["st"][h].astype(BF16), _NT,
                                          preferred_element_type=F32) for h, cols in enumerate(head_cols)]
        for w in work:
            w["upd"] = [lax.dot_general(w["v"][:, cols], w["kd"][:, cols], _TN, preferred_element_type=F32)
                        for cols in head_cols]
        for k, w in enumerate(work):
            for h, cols in enumerate(head_cols):
                w["o_ref"][w["b"], w["rows"], cols] = w["intra"][h * c:(h + 1) * c, :] + w["inter"][h]
                s_ref[k, h] = w["st"][h] * w["decay"][:, cols] + w["upd"][h]


def _gla(proj_bf16, proj_f32, lower_bounds, layer, batch, seq_len):
    d = HG_HEADS * HG_DK
    ts = min(GLA_ROWS, seq_len)
    nt = seq_len // ts
    depth = lower_bounds.shape[0]
    pb = proj_bf16.reshape(batch, seq_len, proj_bf16.shape[1])
    pf = proj_f32.reshape(batch, seq_len, proj_f32.shape[1])
    bwd = lambda i: nt - 1 - i
    blk = lambda tile, col: pl.BlockSpec((batch, ts, d), lambda i: (0, tile(i), col))
    fwd = lambda i: i
    out = jax.ShapeDtypeStruct((batch, seq_len, d), F32)
    o_fwd, o_bwd = pl.pallas_call(
        functools.partial(_gla_kernel, ts=ts, layer=layer, batch=batch),
        grid=(nt,),
        in_specs=[blk(fwd, 0), blk(fwd, 1), blk(fwd, 0), blk(bwd, 0), blk(bwd, 1), blk(bwd, 1),
                  pl.BlockSpec((depth, d), lambda i: (0, 0))],
        out_specs=[blk(fwd, 0), blk(bwd, 0)],
        out_shape=[out, out],
        scratch_shapes=[pltpu.VMEM((2 * batch, HG_HEADS, HG_DV, HG_DK), F32)],
        compiler_params=_params(("arbitrary",)),
        name="gla_scan",
    )(pb, pb, pf, pb, pb, pf, lower_bounds)
    return o_fwd.reshape(batch * seq_len, d), o_bwd.reshape(batch * seq_len, d)


def _top2(p):
    v1 = jnp.maximum(jnp.maximum(p[0], p[1]), jnp.maximum(p[2], p[3]))
    i1 = jnp.where(p[0] == v1, 0, jnp.where(p[1] == v1, 1, jnp.where(p[2] == v1, 2, 3)))
    q = [jnp.where(i1 == k, -1.0, p[k]) for k in range(4)]
    v2 = jnp.maximum(jnp.maximum(q[0], q[1]), jnp.maximum(q[2], q[3]))
    i2 = jnp.where(q[0] == v2, 0, jnp.where(q[1] == v2, 1, jnp.where(q[2] == v2, 2, 3)))
    return v1, i1, v2, i2


def _outproj_kernel(*refs, even, alpha, tm):
    if even:
        ret_ref, hf_ref, hb_ref, gr_ref, w0_ref, w1_ref = refs[:6]
        rest = refs[6:]
        lru = ((hf_ref[...] + hb_ref[...]) * jax.nn.gelu(gr_ref[...])).astype(BF16)
        y = (jnp.dot(ret_ref[...], w0_ref[...], preferred_element_type=F32)
             + jnp.dot(lru, w1_ref[...], preferred_element_type=F32))
    else:
        of_ref, ob_ref, g_ref, nw_ref, w0_ref = refs[:5]
        rest = refs[5:]
        o = of_ref[...] + ob_ref[...]
        ms = jnp.mean(o * o, axis=-1, keepdims=True)
        mix = o * lax.rsqrt(ms + RMS_EPS) * nw_ref[...] * _silu(g_ref[...].astype(F32))
        y = jnp.dot(mix.astype(BF16), w0_ref[...], preferred_element_type=F32)
    h_ref, lnw_ref, lnb_ref, rw_ref, o_ref, info_ref, cnt_ref, carry_ref = rest
    i = pl.program_id(0)

    @pl.when(i == 0)
    def _():
        carry_ref[...] = jnp.zeros_like(carry_ref)

    h1 = _layer_norm(alpha * h_ref[...] + y, lnw_ref[...], lnb_ref[...])
    o_ref[...] = h1

    h_hi = h1.astype(BF16)
    h_lo = (h1 - h_hi.astype(F32)).astype(BF16)
    rw = rw_ref[...]
    r_hi = rw.astype(BF16)
    r_lo = (rw - r_hi.astype(F32)).astype(BF16)
    logits = (lax.dot_general(r_lo, h_hi, _NT, preferred_element_type=F32)
              + lax.dot_general(r_hi, h_lo, _NT, preferred_element_type=F32)
              + lax.dot_general(r_hi, h_hi, _NT, preferred_element_type=F32))
    ex = jnp.exp(logits - jnp.max(logits, axis=0, keepdims=True))
    probs = ex / jnp.sum(ex, axis=0, keepdims=True)
    best = None
    for g in range(N_GROUPS):
        rows = [probs[g * EXPERTS_PER_GROUP + k:g * EXPERTS_PER_GROUP + k + 1, :]
                for k in range(EXPERTS_PER_GROUP)]
        v1, i1, v2, i2 = _top2(rows)
        cand = (v1 + v2, v1, i1 + g * EXPERTS_PER_GROUP, v2, i2 + g * EXPERTS_PER_GROUP)
        if best is None:
            best = cand
        else:
            take = cand[0] > best[0]
            best = tuple(jnp.where(take, cn, bs) for cn, bs in zip(cand, best))
    _, v1, e1, v2, e2 = best
    denom = v1 + v2
    g1 = v1 / denom
    g2 = v2 / denom

    eid = lax.broadcasted_iota(jnp.int32, (N_EXPERTS, tm), 0)
    oh1 = (eid == e1).astype(F32)
    oh2 = (eid == e2).astype(F32)
    oh = oh1 + oh2
    tt = lax.broadcasted_iota(jnp.int32, (tm, tm), 0)
    uu = lax.broadcasted_iota(jnp.int32, (tm, tm), 1)
    before = (tt < uu).astype(BF16)
    base = carry_ref[:, 0:1] + jnp.dot(oh.astype(BF16), before, preferred_element_type=F32)
    rank1 = jnp.sum(oh1 * base, axis=0, keepdims=True)
    rank2 = jnp.sum(oh2 * base, axis=0, keepdims=True)
    carry_ref[...] = carry_ref[...] + jnp.sum(oh, axis=1, keepdims=True)
    cnt_ref[...] = carry_ref[...]
    zero = jnp.zeros_like(g1)
    info_ref[...] = jnp.concatenate(
        [e1.astype(F32), e2.astype(F32), g1, g2, rank1, rank2, zero, zero], axis=0)


def _out_project(even, mixer_inputs, weights_bf16, h, ln_w, ln_b, router_wt, alpha):
    t, d = h.shape
    tm = TOKEN_ROWS
    row = lambda i: (i, 0)
    const = lambda i: (0, 0)
    if even:
        ret, h_fwd, h_bwd, proj_f32 = mixer_inputs
        w = h_fwd.shape[1]
        in_specs = [pl.BlockSpec((tm, ret.shape[1]), row), pl.BlockSpec((tm, w), row),
                    pl.BlockSpec((tm, w), row), pl.BlockSpec((tm, w), lambda i: (i, 1))]
        args = [ret, h_fwd, h_bwd, proj_f32]
    else:
        o_fwd, o_bwd, proj_bf16, norm_w = mixer_inputs
        in_specs = [pl.BlockSpec((tm, d), row), pl.BlockSpec((tm, d), row),
                    pl.BlockSpec((tm, d), lambda i: (i, 2)), pl.BlockSpec((1, d), const)]
        args = [o_fwd, o_bwd, proj_bf16, norm_w.reshape(1, d)]
    in_specs += [pl.BlockSpec(wm.shape, const) for wm in weights_bf16]
    in_specs += [pl.BlockSpec((tm, d), row), pl.BlockSpec((1, d), const), pl.BlockSpec((1, d), const),
                 pl.BlockSpec((N_EXPERTS, d), const)]
    args += list(weights_bf16) + [h, ln_w.reshape(1, d), ln_b.reshape(1, d), router_wt]
    return pl.pallas_call(
        functools.partial(_outproj_kernel, even=even, alpha=alpha, tm=tm),
        grid=(t // tm,),
        in_specs=in_specs,
        out_specs=[pl.BlockSpec((tm, d), row),
                   pl.BlockSpec((SUBLANES, tm), lambda i: (0, i)),
                   pl.BlockSpec((N_EXPERTS, LANES), const)],
        out_shape=[jax.ShapeDtypeStruct((t, d), F32),
                   jax.ShapeDtypeStruct((SUBLANES, t), F32),
                   jax.ShapeDtypeStruct((N_EXPERTS, LANES), F32)],
        scratch_shapes=[pltpu.VMEM((N_EXPERTS, LANES), F32)],
        compiler_params=_params(("arbitrary",)),
        name="out_proj_router",
    )(*args)


def _row_copy(src_ref, src_row, dst_ref, dst_row, sem):
    return pltpu.make_async_copy(src_ref.at[pl.ds(src_row, 1)], dst_ref.at[pl.ds(dst_row, 1)], sem)


def _dispatch_kernel(d1_ref, d2_ref, zs_ref, h_ref, xb_ref, zero_ref, sem, *, tm):
    i = pl.program_id(0)
    base = i * tm

    @pl.when(i == 0)
    def _():
        zero_ref[...] = jnp.zeros_like(zero_ref)

        def clear(row):
            start = pl.multiple_of(row, MOE_ROWS)
            return pltpu.make_async_copy(zero_ref, xb_ref.at[pl.ds(start, MOE_ROWS)], sem.at[0])

        for e in range(N_EXPERTS):
            clear(zs_ref[e]).start()
        for e in range(N_EXPERTS):
            clear(zs_ref[e]).wait()

        def clear_tail(b, carry):
            clear(b * MOE_ROWS).start()
            clear(b * MOE_ROWS).wait()
            return carry

        lax.fori_loop(zs_ref[N_EXPERTS], xb_ref.shape[0] // MOE_ROWS, clear_tail, 0)

    def start(t, carry):
        _row_copy(h_ref, t, xb_ref, d1_ref[base + t], sem.at[0]).start()
        _row_copy(h_ref, t, xb_ref, d2_ref[base + t], sem.at[1]).start(priority=1)
        return carry

    lax.fori_loop(0, tm, start, 0, unroll=8)
    pltpu.make_async_copy(h_ref, xb_ref.at[pl.ds(0, tm)], sem.at[0]).wait()
    pltpu.make_async_copy(h_ref, xb_ref.at[pl.ds(0, tm)], sem.at[1]).wait()


def _dispatch(h, dest1, dest2, zero_start, n_rows):
    t, d = h.shape
    tm = TOKEN_ROWS
    return pl.pallas_call(
        functools.partial(_dispatch_kernel, tm=tm),
        grid_spec=pltpu.PrefetchScalarGridSpec(
            num_scalar_prefetch=3,
            grid=(t // tm,),
            in_specs=[pl.BlockSpec((tm, d), lambda i, d1, d2, zs: (i, 0))],
            out_specs=pl.BlockSpec(memory_space=pl.ANY),
            scratch_shapes=[pltpu.VMEM((MOE_ROWS, d), F32), pltpu.SemaphoreType.DMA((2,))],
        ),
        out_shape=jax.ShapeDtypeStruct((n_rows, d), F32),
        compiler_params=_params(("arbitrary",)),
        name="moe_dispatch",
    )(dest1, dest2, zero_start, h)


def _expert_kernel(be_ref, nv_ref, x_ref, wg_ref, wu_ref, wd_ref, o_ref, wgb_ref, wub_ref, wdb_ref):
    i = pl.program_id(0)
    changed = jnp.logical_or(i == 0, be_ref[i] != be_ref[jnp.maximum(i - 1, 0)])

    @pl.when(changed)
    def _():
        wgb_ref[...] = wg_ref[...].astype(BF16)
        wub_ref[...] = wu_ref[...].astype(BF16)
        wdb_ref[...] = wd_ref[...].astype(BF16)

    @pl.when(i < nv_ref[0])
    def _():
        x = x_ref[...].astype(BF16)
        gate = jnp.dot(x, wgb_ref[...], preferred_element_type=F32)
        up = jnp.dot(x, wub_ref[...], preferred_element_type=F32)
        hid = (_silu(gate) * up).astype(BF16)
        o_ref[...] = jnp.dot(hid, wdb_ref[...], preferred_element_type=F32)

    @pl.when(i >= nv_ref[0])
    def _():
        o_ref[...] = jnp.zeros_like(o_ref)


def _experts(xb, block_e, n_valid, w_gate, w_up, w_down, layer):
    p, d = xb.shape
    de = w_gate.shape[3]
    nb = p // MOE_ROWS
    return pl.pallas_call(
        _expert_kernel,
        grid_spec=pltpu.PrefetchScalarGridSpec(
            num_scalar_prefetch=2,
            grid=(nb,),
            in_specs=[
                pl.BlockSpec((MOE_ROWS, d), lambda i, be, nv: (jnp.maximum(jnp.minimum(i, nv[0] - 1), 0), 0)),
                pl.BlockSpec((None, None, d, de), lambda i, be, nv: (layer, be[i], 0, 0)),
                pl.BlockSpec((None, None, d, de), lambda i, be, nv: (layer, be[i], 0, 0)),
                pl.BlockSpec((None, None, de, d), lambda i, be, nv: (layer, be[i], 0, 0)),
            ],
            out_specs=pl.BlockSpec((MOE_ROWS, d), lambda i, be, nv: (i, 0)),
            scratch_shapes=[pltpu.VMEM((d, de), BF16), pltpu.VMEM((d, de), BF16),
                            pltpu.VMEM((de, d), BF16)],
        ),
        out_shape=jax.ShapeDtypeStruct((p, d), F32),
        compiler_params=_params(("arbitrary",)),
        name="moe_experts",
    )(block_e, n_valid, xb, w_gate, w_up, w_down)


def _combine_kernel(d1_ref, d2_ref, h_ref, gates_ref, lnw_ref, lnb_ref, yb_ref, o_ref, r_ref, sem,
                    *, tm, alpha):
    base = pl.program_id(0) * tm

    def start(t, carry):
        _row_copy(yb_ref, d1_ref[base + t], r_ref.at[0], t, sem.at[0]).start()
        _row_copy(yb_ref, d2_ref[base + t], r_ref.at[1], t, sem.at[1]).start(priority=1)
        return carry

    lax.fori_loop(0, tm, start, 0, unroll=8)
    pltpu.make_async_copy(yb_ref.at[pl.ds(0, tm)], r_ref.at[0], sem.at[0]).wait()
    pltpu.make_async_copy(yb_ref.at[pl.ds(0, tm)], r_ref.at[1], sem.at[1]).wait()
    y = gates_ref[:, 0:1] * r_ref[0] + gates_ref[:, 1:2] * r_ref[1]
    o_ref[...] = _layer_norm(alpha * h_ref[...] + y, lnw_ref[...], lnb_ref[...])


def _combine(h, yb, dest1, dest2, gates, ln_w, ln_b, alpha):
    t, d = h.shape
    tm = TOKEN_ROWS
    return pl.pallas_call(
        functools.partial(_combine_kernel, tm=tm, alpha=alpha),
        grid_spec=pltpu.PrefetchScalarGridSpec(
            num_scalar_prefetch=2,
            grid=(t // tm,),
            in_specs=[pl.BlockSpec((tm, d), lambda i, d1, d2: (i, 0)),
                      pl.BlockSpec((tm, 2), lambda i, d1, d2: (i, 0)),
                      pl.BlockSpec((1, d), lambda i, d1, d2: (0, 0)),
                      pl.BlockSpec((1, d), lambda i, d1, d2: (0, 0)),
                      pl.BlockSpec(memory_space=pl.ANY)],
            out_specs=pl.BlockSpec((tm, d), lambda i, d1, d2: (i, 0)),
            scratch_shapes=[pltpu.VMEM((2, tm, d), F32), pltpu.SemaphoreType.DMA((2,))],
        ),
        out_shape=jax.ShapeDtypeStruct((t, d), F32),
        compiler_params=_params(("arbitrary",)),
        name="moe_combine",
    )(dest1, dest2, h, gates, ln_w.reshape(1, d), ln_b.reshape(1, d), yb)


def _invert_kernel(d1_ref, d2_ref, init_ref, ids_ref, sem, *, t):
    fill = pltpu.make_async_copy(init_ref, ids_ref, sem)
    fill.start()
    fill.wait()

    def body(tok, carry):
        ids_ref[d1_ref[tok]] = tok
        ids_ref[d2_ref[tok]] = tok + t
        return carry

    lax.fori_loop(0, t, body, 0, unroll=8)


def _invert(dest1, dest2, n_rows):
    t = dest1.shape[0]
    smem = pl.BlockSpec(memory_space=pltpu.SMEM)
    return pl.pallas_call(
        functools.partial(_invert_kernel, t=t),
        in_specs=[smem, smem, pl.BlockSpec(memory_space=pl.ANY)],
        out_specs=smem,
        out_shape=jax.ShapeDtypeStruct((n_rows,), jnp.int32),
        scratch_shapes=[pltpu.SemaphoreType.DMA(())],
        name="moe_invert",
    )(dest1, dest2, jnp.full((n_rows,), -1, jnp.int32))


def _expert_fused_kernel(be_ref, src_ref, dst_ref, h_ref, wg_ref, wu_ref, wd_ref, yk_ref,
                         xbuf, ybuf, wgb_ref, wub_ref, wdb_ref, sem_g, sem_s):
    i = pl.program_id(0)
    nb = pl.num_programs(0)
    slot = i % 2
    other = 1 - slot

    def gather(block, buf):
        base = block * MOE_ROWS
        for r in range(MOE_ROWS):
            _row_copy(h_ref, src_ref[base + r], xbuf.at[buf], r, sem_g.at[buf]).start()

    def scatter(block, buf):
        base = block * MOE_ROWS
        for r in range(MOE_ROWS):
            _row_copy(ybuf.at[buf], r, yk_ref, dst_ref[base + r], sem_s.at[buf]).start(priority=1)

    def wait_gather(buf):
        pltpu.make_async_copy(h_ref.at[pl.ds(0, MOE_ROWS)], xbuf.at[buf], sem_g.at[buf]).wait()

    def wait_scatter(buf):
        pltpu.make_async_copy(ybuf.at[buf], yk_ref.at[pl.ds(0, MOE_ROWS)], sem_s.at[buf]).wait()

    @pl.when(i == 0)
    def _():
        ybuf[...] = jnp.zeros_like(ybuf)
        gather(0, 0)

    wait_gather(slot)

    @pl.when(i >= 1)
    def _():
        wait_scatter(slot)

    changed = jnp.logical_or(i == 0, be_ref[i] != be_ref[jnp.maximum(i - 1, 0)])

    @pl.when(changed)
    def _():
        wgb_ref[...] = wg_ref[...].astype(BF16)
        wub_ref[...] = wu_ref[...].astype(BF16)
        wdb_ref[...] = wd_ref[...].astype(BF16)

    gather(jnp.minimum(i + 1, nb - 1), other)
    scatter(jnp.maximum(i - 1, 0), other)
    x = xbuf[slot].astype(BF16)
    gate = jnp.dot(x, wgb_ref[...], preferred_element_type=F32)
    up = jnp.dot(x, wub_ref[...], preferred_element_type=F32)
    hid = (_silu(gate) * up).astype(BF16)
    ybuf[slot] = jnp.dot(hid, wdb_ref[...], preferred_element_type=F32)

    @pl.when(i == nb - 1)
    def _():
        scatter(i, slot)
        wait_gather(other)
        wait_scatter(other)
        wait_scatter(slot)


def _experts_fused(h, src_tok, dst_row, block_e, w_gate, w_up, w_down, layer):
    t, d = h.shape
    de = w_gate.shape[3]
    p = src_tok.shape[0]
    nb = p // MOE_ROWS
    weight = lambda shape: pl.BlockSpec((None, None) + shape, lambda i, be, src, dst: (layer, be[i], 0, 0))
    return pl.pallas_call(
        _expert_fused_kernel,
        grid_spec=pltpu.PrefetchScalarGridSpec(
            num_scalar_prefetch=3,
            grid=(nb,),
            in_specs=[pl.BlockSpec(memory_space=pl.ANY), weight((d, de)), weight((d, de)), weight((de, d))],
            out_specs=pl.BlockSpec(memory_space=pl.ANY),
            scratch_shapes=[pltpu.VMEM((2, MOE_ROWS, d), F32), pltpu.VMEM((2, MOE_ROWS, d), F32),
                            pltpu.VMEM((d, de), BF16), pltpu.VMEM((d, de), BF16), pltpu.VMEM((de, d), BF16),
                            pltpu.SemaphoreType.DMA((2,)), pltpu.SemaphoreType.DMA((2,))],
        ),
        out_shape=jax.ShapeDtypeStruct((p, d), F32),
        compiler_params=_params(("arbitrary",)),
        name="moe_experts",
    )(block_e, src_tok, dst_row, h, w_gate, w_up, w_down)


def _combine_dense_kernel(h_ref, y0_ref, y1_ref, gates_ref, lnw_ref, lnb_ref, o_ref, *, alpha):
    y = gates_ref[:, 0:1] * y0_ref[...] + gates_ref[:, 1:2] * y1_ref[...]
    o_ref[...] = _layer_norm(alpha * h_ref[...] + y, lnw_ref[...], lnb_ref[...])


def _combine_dense(h, yk, gates, ln_w, ln_b, alpha):
    t, d = h.shape
    tm = TOKEN_ROWS
    nt = t // tm
    row = lambda i: (i, 0)
    const = lambda i: (0, 0)
    return pl.pallas_call(
        functools.partial(_combine_dense_kernel, alpha=alpha),
        grid=(nt,),
        in_specs=[pl.BlockSpec((tm, d), row), pl.BlockSpec((tm, d), row),
                  pl.BlockSpec((tm, d), lambda i: (nt + i, 0)), pl.BlockSpec((tm, 2), row),
                  pl.BlockSpec((1, d), const), pl.BlockSpec((1, d), const)],
        out_specs=pl.BlockSpec((tm, d), row),
        out_shape=jax.ShapeDtypeStruct((t, d), F32),
        compiler_params=_params(("arbitrary",)),
        name="moe_combine",
    )(h, yk, yk, gates, ln_w.reshape(1, d), ln_b.reshape(1, d))


def _moe_fused(h1, info, counts, w_gate, w_up, w_down, layer, ln_w, ln_b, alpha):
    t = h1.shape[0]
    n_rows = 2 * t + N_EXPERTS * MOE_ROWS
    nb = n_rows // MOE_ROWS
    cnt = counts[:, 0].astype(jnp.int32)
    padded = (cnt + MOE_ROWS - 1) // MOE_ROWS * MOE_ROWS
    ends = jnp.cumsum(padded)
    pstart = ends - padded
    e1 = info[0].astype(jnp.int32)
    e2 = info[1].astype(jnp.int32)
    expert_ids = jnp.arange(N_EXPERTS, dtype=jnp.int32)[:, None]
    dest1 = jnp.sum(jnp.where(e1[None, :] == expert_ids, pstart[:, None], 0), axis=0) + info[4].astype(jnp.int32)
    dest2 = jnp.sum(jnp.where(e2[None, :] == expert_ids, pstart[:, None], 0), axis=0) + info[5].astype(jnp.int32)
    gates = jnp.stack([info[2], info[3]], axis=1)
    blk = jnp.arange(nb, dtype=jnp.int32) * MOE_ROWS
    block_e = jnp.minimum(jnp.sum((ends[None, :] <= blk[:, None]).astype(jnp.int32), axis=1), N_EXPERTS - 1)
    ids = _invert(dest1, dest2, n_rows)
    is_pad = ids < 0
    src_tok = jnp.where(is_pad, 0, ids % t)
    dst_row = jnp.where(is_pad, 2 * t - 1 + jnp.cumsum(is_pad.astype(jnp.int32)), ids)
    yk = _experts_fused(h1, src_tok, dst_row, block_e, w_gate, w_up, w_down, layer)
    return _combine_dense(h1, yk, gates, ln_w, ln_b, alpha)


def _pack_halves(x):
    n = x.shape[1] // 2
    hi = pltpu.bitcast(x[:, :n], jnp.uint32)
    lo = pltpu.bitcast(x[:, n:], jnp.uint32)
    return hi | (lo >> 16)


def _unpack_halves(p):
    hi = pltpu.bitcast(p & jnp.uint32(0xFFFF0000), F32)
    lo = pltpu.bitcast(p << 16, F32)
    return jnp.concatenate([hi.astype(BF16), lo.astype(BF16)], axis=1)


def _seg_expert_kernel(first_ref, count_ref, tail_ref, x_ref, wg_ref, wu_ref, wd_ref, y_ref,
                       xbuf, ybuf, wgb_ref, wub_ref, wdb_ref, semx, semy):
    e = pl.program_id(0)
    wgb_ref[...] = wg_ref[...].astype(BF16)
    wub_ref[...] = wu_ref[...].astype(BF16)
    wdb_ref[...] = wd_ref[...].astype(BF16)
    first = first_ref[e]
    count = count_ref[e]

    def rows(j):
        return pl.ds(pl.multiple_of((first + j) * MOE_ROWS, MOE_ROWS), MOE_ROWS)

    def fetch(j, buf):
        return pltpu.make_async_copy(x_ref.at[rows(j)], xbuf.at[buf], semx.at[buf])

    def put(j, buf):
        return pltpu.make_async_copy(ybuf.at[buf], y_ref.at[rows(j)], semy.at[buf])

    @pl.when(count > 0)
    def _():
        fetch(0, 0).start()

    def body(j, carry):
        buf = j % 2

        @pl.when(j + 1 < count)
        def _():
            fetch(j + 1, 1 - buf).start()

        fetch(j, buf).wait()

        @pl.when(j >= 2)
        def _():
            put(j - 2, buf).wait()

        x = _unpack_halves(xbuf[buf])
        gate = jnp.dot(x, wgb_ref[...], preferred_element_type=F32)
        up = jnp.dot(x, wub_ref[...], preferred_element_type=F32)
        hid = (_silu(gate) * up).astype(BF16)
        y = jnp.dot(hid, wdb_ref[...], preferred_element_type=F32)
        ybuf[buf] = _pack_halves(y.astype(BF16).astype(F32))
        put(j, buf).start()
        return carry

    lax.fori_loop(0, count, body, 0)

    @pl.when(count >= 2)
    def _():
        put(count - 2, count % 2).wait()

    @pl.when(count >= 1)
    def _():
        put(count - 1, (count - 1) % 2).wait()

    @pl.when(e == pl.num_programs(0) - 1)
    def _():
        ybuf[0] = jnp.zeros_like(ybuf[0])

        def clear(b, carry):
            cp = pltpu.make_async_copy(
                ybuf.at[0], y_ref.at[pl.ds(pl.multiple_of(b * MOE_ROWS, MOE_ROWS), MOE_ROWS)], semy.at[0])
            cp.start()
            cp.wait()
            return carry

        lax.fori_loop(tail_ref[0], y_ref.shape[0] // MOE_ROWS, clear, 0)


def _seg_experts(xb, seg_first, seg_count, n_valid, w_gate, w_up, w_down, layer):
    p, half = xb.shape
    d = 2 * half
    de = w_gate.shape[3]
    weight = lambda shape: pl.BlockSpec((None, None) + shape, lambda e, a, b, c: (layer, e, 0, 0))
    return pl.pallas_call(
        _seg_expert_kernel,
        grid_spec=pltpu.PrefetchScalarGridSpec(
            num_scalar_prefetch=3,
            grid=(N_EXPERTS,),
            in_specs=[pl.BlockSpec(memory_space=pl.ANY), weight((d, de)), weight((d, de)), weight((de, d))],
            out_specs=pl.BlockSpec(memory_space=pl.ANY),
            scratch_shapes=[pltpu.VMEM((2, MOE_ROWS, half), jnp.uint32),
                            pltpu.VMEM((2, MOE_ROWS, half), jnp.uint32),
                            pltpu.VMEM((d, de), BF16), pltpu.VMEM((d, de), BF16), pltpu.VMEM((de, d), BF16),
                            pltpu.SemaphoreType.DMA((2,)), pltpu.SemaphoreType.DMA((2,))],
        ),
        out_shape=jax.ShapeDtypeStruct((p, half), jnp.uint32),
        compiler_params=_params(("arbitrary",)),
        name="moe_experts",
    )(seg_first, seg_count, n_valid, xb, w_gate, w_up, w_down)


def _clear_padding_blocks(zs_ref, xb_ref, zero_ref, sem):
    zero_ref[...] = jnp.zeros_like(zero_ref)

    def clear(row):
        start = pl.multiple_of(row, MOE_ROWS)
        return pltpu.make_async_copy(zero_ref, xb_ref.at[pl.ds(start, MOE_ROWS)], sem)

    for e in range(N_EXPERTS):
        clear(zs_ref[e]).start()
    for e in range(N_EXPERTS):
        clear(zs_ref[e]).wait()

    def clear_tail(b, carry):
        clear(b * MOE_ROWS).start()
        clear(b * MOE_ROWS).wait()
        return carry

    lax.fori_loop(zs_ref[N_EXPERTS], xb_ref.shape[0] // MOE_ROWS, clear_tail, 0)


def _run_copy(src_ref, src_row, dst_ref, dst_row, sem):
    src = pl.multiple_of(src_row, SUBLANES)
    dst = pl.multiple_of(dst_row, SUBLANES)
    return pltpu.make_async_copy(src_ref.at[pl.ds(src, SUBLANES)], dst_ref.at[pl.ds(dst, SUBLANES)], sem)


def _sort_dispatch_kernel(dst_ref, nch_ref, zs_ref, h_ref, lp_ref, xb_ref, xs_ref, zero_ref, sem, *, tm, nck):
    i = pl.program_id(0)

    @pl.when(i == 0)
    def _():
        _clear_padding_blocks(zs_ref, xb_ref, zero_ref, sem.at[0])

    slot = i % 2
    last = pl.num_programs(0) - 1

    def wait_runs(buf, count):
        def wait(j, carry):
            _run_copy(xs_ref.at[buf], 0, xb_ref, 0, sem.at[buf]).wait()
            return carry

        lax.fori_loop(0, count, wait, 0)

    @pl.when(i >= 2)
    def _():
        wait_runs(slot, nch_ref[jnp.maximum(i - 2, 0)])

    lp = lp_ref[...]
    pos = lax.broadcasted_iota(jnp.int32, (xs_ref.shape[1], tm), 0)
    perm = jnp.logical_or(pos == lp[0:1, :], pos == lp[1:2, :]).astype(BF16)
    xs_ref[slot] = _pack_halves(jnp.dot(perm, h_ref[...].astype(BF16), preferred_element_type=F32))

    def start(j, carry):
        _run_copy(xs_ref.at[slot], j * SUBLANES, xb_ref, dst_ref[i * nck + j], sem.at[slot]).start()
        return carry

    lax.fori_loop(0, nch_ref[i], start, 0)

    @pl.when(i == last)
    def _():
        @pl.when(i >= 1)
        def _():
            wait_runs(1 - slot, nch_ref[jnp.maximum(i - 1, 0)])

        wait_runs(slot, nch_ref[i])


def _sort_dispatch(h, lp_rows, chunk_dst, n_chunks, zero_start, n_rows, tm, sorted_rows):
    t, d = h.shape
    nck = sorted_rows // SUBLANES
    return pl.pallas_call(
        functools.partial(_sort_dispatch_kernel, tm=tm, nck=nck),
        grid_spec=pltpu.PrefetchScalarGridSpec(
            num_scalar_prefetch=3,
            grid=(t // tm,),
            in_specs=[pl.BlockSpec((tm, d), lambda i, a, b, c: (i, 0)),
                      pl.BlockSpec((2, tm), lambda i, a, b, c: (0, i))],
            out_specs=pl.BlockSpec(memory_space=pl.ANY),
            scratch_shapes=[pltpu.VMEM((2, sorted_rows, d // 2), jnp.uint32),
                            pltpu.VMEM((MOE_ROWS, d // 2), jnp.uint32), pltpu.SemaphoreType.DMA((2,))],
        ),
        out_shape=jax.ShapeDtypeStruct((n_rows, d // 2), jnp.uint32),
        compiler_params=_params(("arbitrary",)),
        name="moe_dispatch",
    )(chunk_dst, n_chunks, zero_start, h, lp_rows)


def _sort_combine_kernel(src_ref, nch_ref, h_ref, lp_ref, gates_ref, lnw_ref, lnb_ref, yb_ref, o_ref,
                         ys_ref, sem, *, tm, nck, alpha):
    i = pl.program_id(0)

    slot = i % 2

    def fetch(tile, buf):
        def start(j, carry):
            _run_copy(yb_ref, src_ref[tile * nck + j], ys_ref.at[buf], j * SUBLANES, sem.at[buf]).start()
            return carry

        lax.fori_loop(0, nch_ref[tile], start, 0)

    @pl.when(i == 0)
    def _():
        ys_ref[...] = jnp.zeros_like(ys_ref)
        fetch(0, 0)

    @pl.when(i + 1 < pl.num_programs(0))
    def _():
        fetch(i + 1, 1 - slot)

    def wait(j, carry):
        _run_copy(yb_ref, 0, ys_ref.at[slot], 0, sem.at[slot]).wait()
        return carry

    lax.fori_loop(0, nch_ref[i], wait, 0)

    lp = lp_ref[...]
    pos = lax.broadcasted_iota(jnp.int32, (tm, ys_ref.shape[1]), 1)
    ys = _unpack_halves(ys_ref[slot])
    y1 = jnp.dot((pos == lp[:, 0:1]).astype(BF16), ys, preferred_element_type=F32)
    y2 = jnp.dot((pos == lp[:, 1:2]).astype(BF16), ys, preferred_element_type=F32)
    y = gates_ref[:, 0:1] * y1 + gates_ref[:, 1:2] * y2
    o_ref[...] = _layer_norm(alpha * h_ref[...] + y, lnw_ref[...], lnb_ref[...])


def _sort_combine(h, yb, lp_cols, gates, chunk_src, n_chunks, ln_w, ln_b, alpha, tm, sorted_rows):
    t, d = h.shape
    nck = sorted_rows // SUBLANES
    row = lambda i, a, b: (i, 0)
    const = lambda i, a, b: (0, 0)
    return pl.pallas_call(
        functools.partial(_sort_combine_kernel, tm=tm, nck=nck, alpha=alpha),
        grid_spec=pltpu.PrefetchScalarGridSpec(
            num_scalar_prefetch=2,
            grid=(t // tm,),
            in_specs=[pl.BlockSpec((tm, d), row), pl.BlockSpec((tm, 2), row), pl.BlockSpec((tm, 2), row),
                      pl.BlockSpec((1, d), const), pl.BlockSpec((1, d), const),
                      pl.BlockSpec(memory_space=pl.ANY)],
            out_specs=pl.BlockSpec((tm, d), row),
            scratch_shapes=[pltpu.VMEM((2, sorted_rows, d // 2), jnp.uint32), pltpu.SemaphoreType.DMA((2,))],
        ),
        out_shape=jax.ShapeDtypeStruct((t, d), F32),
        compiler_params=_params(("arbitrary",)),
        name="moe_combine",
    )(chunk_src, n_chunks, h, lp_cols, gates, ln_w.reshape(1, d), ln_b.reshape(1, d), yb)


def _moe_sorted(h1, info, w_gate, w_up, w_down, layer, ln_w, ln_b, alpha):
    t = h1.shape[0]
    tm = SORT_ROWS
    ntile = t // tm
    run_pad = SUBLANES - 1
    sorted_rows = 2 * tm + LANES
    nck = sorted_rows // SUBLANES
    n_rows = (2 * t + ntile * N_EXPERTS * run_pad + MOE_ROWS - 1) // MOE_ROWS * MOE_ROWS + N_EXPERTS * MOE_ROWS
    nb = n_rows // MOE_ROWS
    i32 = jnp.int32
    e1, e2 = info[0].astype(i32), info[1].astype(i32)
    rank1, rank2 = info[4].astype(i32), info[5].astype(i32)
    expert_ids = jnp.arange(N_EXPERTS, dtype=i32)[None, :]
    oh1 = e1[:, None] == expert_ids
    oh2 = e2[:, None] == expert_ids
    cnt = jnp.logical_or(oh1, oh2).astype(i32).reshape(ntile, tm, N_EXPERTS).sum(axis=1)
    cnt8 = (cnt + run_pad) // SUBLANES * SUBLANES
    local = jnp.cumsum(cnt8, axis=1) - cnt8
    before = jnp.cumsum(cnt, axis=0) - cnt
    seg = cnt8.sum(axis=0)
    seg_pad = (seg + MOE_ROWS - 1) // MOE_ROWS * MOE_ROWS
    ends = jnp.cumsum(seg_pad)
    slot = (ends - seg_pad)[None, :] + jnp.cumsum(cnt8, axis=0) - cnt8
    shift = jnp.repeat(local - before, tm, axis=0)
    lp1 = jnp.sum(jnp.where(oh1, shift, 0), axis=1) + rank1
    lp2 = jnp.sum(jnp.where(oh2, shift, 0), axis=1) + rank2
    chunk_row = jnp.arange(nck, dtype=i32) * SUBLANES
    run_of = jnp.sum(((local + cnt8)[:, None, :] <= chunk_row[None, :, None]).astype(i32), axis=2)
    run_of = jnp.minimum(run_of, N_EXPERTS - 1)
    chunk_slot = jnp.sum(jnp.where(run_of[:, :, None] == expert_ids[None], (slot - local)[:, None, :], 0),
                         axis=2) + chunk_row[None, :]
    n_chunks = cnt8.sum(axis=1) // SUBLANES
    n_valid = (ends[-1] // MOE_ROWS).astype(i32).reshape(1)
    zero_start = jnp.concatenate([jnp.maximum(ends - MOE_ROWS, 0), n_valid]).astype(i32)
    gates = jnp.stack([info[2], info[3]], axis=1)
    chunk_slot = chunk_slot.reshape(-1).astype(i32)
    seg_first = ((ends - seg_pad) // MOE_ROWS).astype(i32)
    seg_count = (seg_pad // MOE_ROWS).astype(i32)

    xb = _sort_dispatch(h1, jnp.stack([lp1, lp2], axis=0), chunk_slot, n_chunks, zero_start, n_rows, tm,
                        sorted_rows)
    yb = _seg_experts(xb, seg_first, seg_count, n_valid, w_gate, w_up, w_down, layer)
    return _sort_combine(h1, yb, jnp.stack([lp1, lp2], axis=1), gates, chunk_slot, n_chunks, ln_w, ln_b,
                         alpha, tm, sorted_rows)


def _moe(h1, info, counts, w_gate, w_up, w_down, layer, ln_w, ln_b, alpha):
    t = h1.shape[0]
    n_rows = 2 * t + N_EXPERTS * MOE_ROWS
    nb = n_rows // MOE_ROWS
    cnt = counts[:, 0].astype(jnp.int32)
    padded = (cnt + MOE_ROWS - 1) // MOE_ROWS * MOE_ROWS
    ends = jnp.cumsum(padded)
    pstart = ends - padded
    e1 = info[0].astype(jnp.int32)
    e2 = info[1].astype(jnp.int32)
    expert_ids = jnp.arange(N_EXPERTS, dtype=jnp.int32)[:, None]
    dest1 = jnp.sum(jnp.where(e1[None, :] == expert_ids, pstart[:, None], 0), axis=0) + info[4].astype(jnp.int32)
    dest2 = jnp.sum(jnp.where(e2[None, :] == expert_ids, pstart[:, None], 0), axis=0) + info[5].astype(jnp.int32)
    gates = jnp.stack([info[2], info[3]], axis=1)
    n_valid = (ends[-1] // MOE_ROWS).astype(jnp.int32).reshape(1)
    blk = jnp.minimum(jnp.arange(nb, dtype=jnp.int32), n_valid[0] - 1) * MOE_ROWS
    block_e = jnp.minimum(jnp.sum((ends[None, :] <= blk[:, None]).astype(jnp.int32), axis=1), N_EXPERTS - 1)
    zero_start = jnp.concatenate([jnp.maximum(ends - MOE_ROWS, 0), n_valid]).astype(jnp.int32)

    xb = _dispatch(h1, dest1, dest2, zero_start, n_rows)
    yb = _experts(xb, block_e, n_valid, w_gate, w_up, w_down, layer)
    return _combine(h1, yb, dest1, dest2, gates, ln_w, ln_b, alpha)


def _rotary_column_order(w_in):
    d = w_in.shape[0]
    nq = RET_HEADS * RET_DK

    def perm(w):
        return w.reshape(d, RET_HEADS, RET_DK // 2, 2).transpose(0, 3, 1, 2).reshape(d, nq)

    return jnp.concatenate([perm(w_in[:, :nq]), perm(w_in[:, nq:2 * nq]), w_in[:, 2 * nq:]], axis=1)


def kernel(x, w_in_even, w_out_even, lru_conv_w, lru_conv_b, lru_gate_w, lru_gate_b, lru_lambda,
           w_in_odd, w_out_odd, hg_lower_bounds, hg_norm_w, ln_w, ln_b, router_w,
           moe_w_gate, moe_w_up, moe_w_down):
    batch, seq_len, d = x.shape
    depth = ln_w.shape[0]
    alpha = (2.0 * depth) ** 0.25
    t = batch * seq_len
    h = x.reshape(t, d)
    router_wt = router_w.T
    half = RET_DK // 2
    inv_freq = ROPE_BASE ** (-jnp.arange(0, RET_DK, 2, dtype=F32) / RET_DK)
    inv_freq = jnp.tile(inv_freq, LANES // half).reshape(1, LANES)
    nret = RET_HEADS * RET_DV

    for layer in range(depth):
        j = layer // 2
        if layer % 2 == 0:
            w_in = _rotary_column_order(w_in_even[j]).astype(BF16)
            n_bf = 2 * RET_HEADS * RET_DK + 2 * nret
            proj_b, proj_f = _project(h, w_in, inv_freq, n_bf, rotary=True, seq_len=seq_len)
            ret = _retention(proj_b, batch, seq_len)
            h_fwd, h_bwd = _rglru(proj_f, lru_conv_w[j], lru_conv_b[j], lru_gate_w[j], lru_gate_b[j],
                                  lru_lambda[j], batch, seq_len)
            w_out = w_out_even[j].astype(BF16)
            mixer_inputs, weights = (ret, h_fwd, h_bwd, proj_f), [w_out[:nret], w_out[nret:]]
        else:
            w = w_in_odd[j]
            w_in = jnp.concatenate([w[:, :2 * d], w[:, 4 * d:], w[:, 2 * d:4 * d]], axis=1).astype(BF16)
            proj_b, proj_f = _project(h, w_in, inv_freq, 3 * d, rotary=False, seq_len=seq_len)
            o_fwd, o_bwd = _gla(proj_b, proj_f, hg_lower_bounds, layer, batch, seq_len)
            mixer_inputs, weights = (o_fwd, o_bwd, proj_b, hg_norm_w[j]), [w_out_odd[j].astype(BF16)]
        h1, info, counts = _out_project(layer % 2 == 0, mixer_inputs, weights, h, ln_w[layer, 0],
                                        ln_b[layer, 0], router_wt, alpha)
        h = _moe_sorted(h1, info, moe_w_gate, moe_w_up, moe_w_down, layer,
                        ln_w[layer, 1], ln_b[layer, 1], alpha)
    return h.reshape(batch, seq_len, d)
```

```python
import functools
import math

import jax
import jax.numpy as jnp
from jax import lax
from jax.experimental import pallas as pl
from jax.experimental.pallas import tpu as pltpu

F32 = jnp.float32
BF16 = jnp.bfloat16

RET_HEADS = 4
RET_DK = 64
RET_DV = 128
RET_CHUNK = 128
ROPE_BASE = 10000.0
LRU_BLOCKS = 4
LRU_BW = 128
LRU_CONV = 4
LRU_C = 8.0
HG_HEADS = 8
HG_DK = 128
HG_DV = 128
N_EXPERTS = 16
N_GROUPS = 4
EXPERTS_PER_GROUP = 4
LN_EPS = 1e-5
RMS_EPS = 1e-6

LANES = 128
SUBLANES = 8
PROJ_ROWS = 512
PROJ_COLS = 512
RET_CHUNKS_PER_STEP = 4
LRU_ROWS = 512
GLA_CHUNK = 32
GLA_ROWS = 256
TOKEN_ROWS = 512
SORT_ROWS = 256
MOE_ROWS = 512
VMEM_LIMIT = 48 * 1024 * 1024

_NT = (((1,), (1,)), ((), ()))
_TN = (((0,), (0,)), ((), ()))


def _params(sem):
    return pltpu.CompilerParams(dimension_semantics=sem, vmem_limit_bytes=VMEM_LIMIT)


def _sigmoid(x):
    return 0.5 * jnp.tanh(0.5 * x) + 0.5


def _silu(x):
    return x * _sigmoid(x)


def _softplus(x):
    return jnp.maximum(x, 0.0) + jnp.log1p(jnp.exp(-jnp.abs(x)))


def _layer_norm(u, w, b):
    mu = jnp.mean(u, axis=-1, keepdims=True)
    d = u - mu
    var = jnp.mean(d * d, axis=-1, keepdims=True)
    return d * lax.rsqrt(var + LN_EPS) * w + b


def _split3(x):
    hi = x.astype(BF16)
    rem = x - hi.astype(F32)
    mid = rem.astype(BF16)
    lo = (rem - mid.astype(F32)).astype(BF16)
    return hi, mid, lo


def _proj_kernel(x_ref, w_ref, inv_ref, ob_ref, of_ref, *, rotary, seq_len, tm, n_bf):
    i = pl.program_id(0)
    xb = x_ref[...].astype(BF16)
    tn = PROJ_COLS
    for j in range(w_ref.shape[1] // tn):
        acc = jnp.dot(xb, w_ref[:, j * tn:(j + 1) * tn], preferred_element_type=F32)
        if rotary and j == 0:
            row = lax.broadcasted_iota(jnp.int32, (tm, 1), 0) + i * tm
            ang = (row % seq_len).astype(F32) * inv_ref[...]
            cos = jnp.cos(ang)
            sin = jnp.sin(ang)
            scale = RET_DK ** -0.5
            q1, q2 = acc[:, 0:128], acc[:, 128:256]
            k1, k2 = acc[:, 256:384], acc[:, 384:512]
            ob_ref[:, 0:128] = ((q1 * cos - q2 * sin) * scale).astype(BF16)
            ob_ref[:, 128:256] = ((q1 * sin + q2 * cos) * scale).astype(BF16)
            ob_ref[:, 256:384] = (k1 * cos - k2 * sin).astype(BF16)
            ob_ref[:, 384:512] = (k1 * sin + k2 * cos).astype(BF16)
        elif (j + 1) * tn <= n_bf:
            ob_ref[:, j * tn:(j + 1) * tn] = acc.astype(BF16)
        else:
            of_ref[:, j * tn - n_bf:(j + 1) * tn - n_bf] = acc


def _project(x, w_bf16, inv_freq, n_bf, *, rotary, seq_len):
    t, k = x.shape
    n = w_bf16.shape[1]
    tm = min(PROJ_ROWS, seq_len)
    return pl.pallas_call(
        functools.partial(_proj_kernel, rotary=rotary, seq_len=seq_len, tm=tm, n_bf=n_bf),
        grid=(t // tm,),
        in_specs=[
            pl.BlockSpec((tm, k), lambda i: (i, 0)),
            pl.BlockSpec((k, n), lambda i: (0, 0)),
            pl.BlockSpec((1, LANES), lambda i: (0, 0)),
        ],
        out_specs=[pl.BlockSpec((tm, n_bf), lambda i: (i, 0)),
                   pl.BlockSpec((tm, n - n_bf), lambda i: (i, 0))],
        out_shape=[jax.ShapeDtypeStruct((t, n_bf), BF16),
                   jax.ShapeDtypeStruct((t, n - n_bf), F32)],
        compiler_params=_params(("arbitrary",)),
        name="in_proj",
    )(x, w_bf16, inv_freq)


def _ret_log_gamma(head):
    out = jnp.full(head.shape, math.log1p(-(2.0 ** -5.0)), F32)
    for h in range(1, RET_HEADS):
        out = jnp.where(head == h, math.log1p(-(2.0 ** (-5.0 - h))), out)
    return out


def _ret_lane_head():
    lane = lax.broadcasted_iota(jnp.int32, (1, 2 * LANES), 1)
    return (lane % LANES) // (RET_DK // 2)


def _ret_state_mask():
    shape = (RET_HEADS * RET_DV, 2 * LANES)
    row_head = lax.broadcasted_iota(jnp.int32, shape, 0) // RET_DV
    col_head = (lax.broadcasted_iota(jnp.int32, shape, 1) % LANES) // (RET_DK // 2)
    return row_head == col_head


def _ret_bstate_kernel(k_ref, v_ref, sb_ref, s_ref, *, cps):
    c = RET_CHUNK

    @pl.when(pl.program_id(1) == 0)
    def _():
        s_ref[...] = jnp.zeros_like(s_ref)

    lg = _ret_log_gamma(_ret_lane_head())
    idx = lax.broadcasted_iota(jnp.int32, (c, 1), 0).astype(F32)
    k_decay = jnp.exp(lg * idx)
    chunk_decay = jnp.exp(lg * float(c))
    mask = _ret_state_mask()
    for cc in reversed(range(cps)):
        rows = slice(cc * c, (cc + 1) * c)
        sb_ref[0, cc] = s_ref[...].astype(BF16)
        kb = (k_ref[rows, :] * k_decay).astype(BF16)
        upd = lax.dot_general(v_ref[rows, :], kb, _TN, preferred_element_type=F32)
        s_ref[...] = s_ref[...] * chunk_decay + jnp.where(mask, upd, 0.0)


def _ret_out_kernel(q_ref, k_ref, v_ref, g_ref, sb_ref, o_ref, s_ref, *, cps):
    c = RET_CHUNK

    @pl.when(pl.program_id(1) == 0)
    def _():
        s_ref[...] = jnp.zeros_like(s_ref)

    lane_head = _ret_lane_head()
    lg = _ret_log_gamma(lane_head)
    idx = lax.broadcasted_iota(jnp.int32, (c, 1), 0).astype(F32)
    q_decay_f = jnp.exp(lg * (idx + 1.0))
    q_decay_b = jnp.exp(lg * (float(c) - idx))
    k_decay = jnp.exp(lg * (float(c) - 1.0 - idx))
    chunk_decay = jnp.exp(lg * float(c))
    mask = _ret_state_mask()
    ii = lax.broadcasted_iota(jnp.int32, (c, c), 0)
    jj = lax.broadcasted_iota(jnp.int32, (c, c), 1)
    dist = jnp.abs(ii - jj).astype(F32)
    intra_decay = [jnp.exp(math.log1p(-(2.0 ** (-5.0 - h))) * dist) for h in range(RET_HEADS)]

    for cc in range(cps):
        rows = slice(cc * c, (cc + 1) * c)
        q = q_ref[rows, :]
        k = k_ref[rows, :]
        v = v_ref[rows, :]
        qf = (q * q_decay_f).astype(BF16)
        qb = (q * q_decay_b).astype(BF16)
        cross = (lax.dot_general(qf, s_ref[...].astype(BF16), _NT, preferred_element_type=F32)
                 + lax.dot_general(qb, sb_ref[0, cc], _NT, preferred_element_type=F32))
        for h in range(RET_HEADS):
            qh = jnp.where(lane_head == h, q, jnp.zeros_like(q))
            s = lax.dot_general(qh, k, _NT, preferred_element_type=F32) * intra_decay[h]
            cols = slice(h * RET_DV, (h + 1) * RET_DV)
            o = jnp.dot(s.astype(BF16), v[:, cols], preferred_element_type=F32) + cross[:, cols]
            mu = jnp.mean(o, axis=-1, keepdims=True)
            d = o - mu
            var = jnp.mean(d * d, axis=-1, keepdims=True)
            gate = _silu(g_ref[rows, cols].astype(F32))
            o_ref[rows, cols] = (gate * (d * lax.rsqrt(var + LN_EPS))).astype(BF16)
        kf = (k * k_decay).astype(BF16)
        upd = lax.dot_general(v, kf, _TN, preferred_element_type=F32)
        s_ref[...] = s_ref[...] * chunk_decay + jnp.where(mask, upd, 0.0)


def _retention(proj, batch, seq_len):
    t = proj.shape[0]
    c = RET_CHUNK
    cps = min(RET_CHUNKS_PER_STEP, seq_len // c)
    rows = cps * c
    ns = seq_len // rows
    dv = RET_HEADS * RET_DV
    state_shape = (dv, 2 * LANES)
    rev = lambda b, n: b * ns + (ns - 1 - n)
    fwd = lambda b, n: b * ns + n
    sb = pl.pallas_call(
        functools.partial(_ret_bstate_kernel, cps=cps),
        grid=(batch, ns),
        in_specs=[
            pl.BlockSpec((rows, 2 * LANES), lambda b, n: (rev(b, n), 1)),
            pl.BlockSpec((rows, dv), lambda b, n: (rev(b, n), 1)),
        ],
        out_specs=pl.BlockSpec((1, cps) + state_shape, lambda b, n: (b, ns - 1 - n, 0, 0)),
        out_shape=jax.ShapeDtypeStruct((batch, ns * cps) + state_shape, BF16),
        scratch_shapes=[pltpu.VMEM(state_shape, F32)],
        compiler_params=_params(("arbitrary", "arbitrary")),
        name="ret_bstate",
    )(proj, proj)
    return pl.pallas_call(
        functools.partial(_ret_out_kernel, cps=cps),
        grid=(batch, ns),
        in_specs=[
            pl.BlockSpec((rows, 2 * LANES), lambda b, n: (fwd(b, n), 0)),
            pl.BlockSpec((rows, 2 * LANES), lambda b, n: (fwd(b, n), 1)),
            pl.BlockSpec((rows, dv), lambda b, n: (fwd(b, n), 1)),
            pl.BlockSpec((rows, dv), lambda b, n: (fwd(b, n), 2)),
            pl.BlockSpec((1, cps) + state_shape, lambda b, n: (b, n, 0, 0)),
        ],
        out_specs=pl.BlockSpec((rows, dv), lambda b, n: (fwd(b, n), 0)),
        out_shape=jax.ShapeDtypeStruct((t, dv), BF16),
        scratch_shapes=[pltpu.VMEM(state_shape, F32)],
        compiler_params=_params(("arbitrary", "arbitrary")),
        name="ret_out",
    )(proj, proj, proj, proj, sb)


def _lru_kernel(xfp_ref, xf_ref, xfn_ref, xbp_ref, xb_ref, xbn_ref, cw_ref, cb_ref, gw_ref, gb_ref,
                lam_ref, hf_ref, hb_ref, xx_ref, a_ref, b_ref, h_ref, *, nt, ts, batch):
    i = pl.program_id(0)
    halo = SUBLANES
    lo = LRU_CONV // 2

    @pl.when(i == 0)
    def _():
        h_ref[...] = jnp.zeros_like(h_ref)

    def prepare(xp_ref, x_ref, xn_ref, tile, z, slot):
        for b in range(batch):
            xx_ref[0:halo, :] = jnp.where(tile == 0, 0.0, xp_ref[b])
            xx_ref[halo:halo + ts, :] = x_ref[b]
            xx_ref[halo + ts:2 * halo + ts, :] = jnp.where(tile == nt - 1, 0.0, xn_ref[b])
            xc = cb_ref[...]
            for tap in range(LRU_CONV):
                xc = xc + cw_ref[tap:tap + 1, :] * xx_ref[pl.ds(halo - lo + tap, ts), :]
            for n in range(LRU_BLOCKS):
                cols = slice(n * LRU_BW, (n + 1) * LRU_BW)
                xn = xc[:, cols]
                g = (jnp.dot(xn.astype(BF16), gw_ref[z, n], preferred_element_type=F32)
                     + gb_ref[z, n:n + 1, :])
                r = _sigmoid(g[:, :LRU_BW])
                ig = _sigmoid(g[:, LRU_BW:])
                a = jnp.exp((-LRU_C) * r * _softplus(-lam_ref[z, :, cols]))
                a_ref[slot + b, :, cols] = a
                b_ref[slot + b, :, cols] = jnp.sqrt(1.0 - a * a) * (ig * xn)

    prepare(xfp_ref, xf_ref, xfn_ref, i, 0, 0)
    prepare(xbp_ref, xb_ref, xbn_ref, nt - 1 - i, 1, batch)

    def step(s, hs):
        out = []
        for k in range(2 * batch):
            row = s if k < batch else ts - 1 - s
            h = a_ref[k, pl.ds(row, 1), :] * hs[k] + b_ref[k, pl.ds(row, 1), :]
            if k < batch:
                hf_ref[k, pl.ds(row, 1), :] = h
            else:
                hb_ref[k - batch, pl.ds(row, 1), :] = h
            out.append(h)
        return tuple(out)

    hs = lax.fori_loop(0, ts, step, tuple(h_ref[k] for k in range(2 * batch)), unroll=8)
    for k in range(2 * batch):
        h_ref[k] = hs[k]


def _rglru(proj_f32, conv_w, conv_b, gate_w, gate_b, lam, batch, seq_len):
    w = LRU_BLOCKS * LRU_BW
    ts = min(LRU_ROWS, seq_len)
    nt = seq_len // ts
    rows8 = ts // SUBLANES
    last8 = seq_len // SUBLANES - 1
    x3 = proj_f32.reshape(batch, seq_len, proj_f32.shape[1])
    gw = jnp.concatenate([gate_w[:, 0], gate_w[:, 1]], axis=-1).astype(BF16)
    gb = jnp.concatenate([gate_b[:, 0], gate_b[:, 1]], axis=-1)
    bwd = lambda i: nt - 1 - i

    def tile_specs(tile):
        return [
            pl.BlockSpec((batch, SUBLANES, w), lambda i: (0, jnp.maximum(tile(i) * rows8 - 1, 0), 0)),
            pl.BlockSpec((batch, ts, w), lambda i: (0, tile(i), 0)),
            pl.BlockSpec((batch, SUBLANES, w), lambda i: (0, jnp.minimum((tile(i) + 1) * rows8, last8), 0)),
        ]

    full = lambda a: pl.BlockSpec(a.shape, lambda i: (0,) * a.ndim)
    cb = conv_b.reshape(1, w)
    lam3 = lam.reshape(2, 1, w)
    state = jax.ShapeDtypeStruct((batch, seq_len, w), F32)
    h_fwd, h_bwd = pl.pallas_call(
        functools.partial(_lru_kernel, nt=nt, ts=ts, batch=batch),
        grid=(nt,),
        in_specs=tile_specs(lambda i: i) + tile_specs(bwd) + [full(conv_w), full(cb), full(gw), full(gb),
                                                               full(lam3)],
        out_specs=[pl.BlockSpec((batch, ts, w), lambda i: (0, i, 0)),
                   pl.BlockSpec((batch, ts, w), lambda i: (0, bwd(i), 0))],
        out_shape=[state, state],
        scratch_shapes=[pltpu.VMEM((ts + 2 * SUBLANES, w), F32), pltpu.VMEM((2 * batch, ts, w), F32),
                        pltpu.VMEM((2 * batch, ts, w), F32), pltpu.VMEM((2 * batch, 1, w), F32)],
        compiler_params=_params(("arbitrary",)),
        name="lru_scan",
    )(x3, x3, x3, x3, x3, x3, conv_w, cb, gw, gb, lam3)
    return h_fwd.reshape(batch * seq_len, w), h_bwd.reshape(batch * seq_len, w)


def _gla_kernel(qf_ref, vf_ref, zf_ref, qb_ref, vb_ref, zb_ref, lbp_ref, of_ref, ob_ref, s_ref,
                *, ts, layer, batch):
    c = GLA_CHUNK
    nchunks = ts // c

    @pl.when(pl.program_id(0) == 0)
    def _():
        s_ref[...] = jnp.zeros_like(s_ref)

    p = lbp_ref[...]
    e = jnp.exp(p - jnp.max(p, axis=0, keepdims=True))
    sm = e / jnp.sum(e, axis=0, keepdims=True)
    lb = jnp.zeros((1, p.shape[1]), F32)
    for r in range(1, layer + 1):
        lb = lb + sm[r:r + 1, :]

    head_cols = [slice(h * HG_DK, (h + 1) * HG_DK) for h in range(HG_HEADS)]
    hc = HG_HEADS * c
    ii = lax.broadcasted_iota(jnp.int32, (c, c), 0)
    jj = lax.broadcasted_iota(jnp.int32, (c, c), 1)
    si = lax.broadcasted_iota(jnp.int32, (hc, hc), 0)
    sj = lax.broadcasted_iota(jnp.int32, (hc, hc), 1)
    same_head = si // c == sj // c
    mid = c // 2

    def stack(a):
        return jnp.concatenate([a[:, cols] for cols in head_cols], axis=0)

    tri = {False: (jj <= ii).astype(BF16), True: (jj >= ii).astype(BF16)}
    tri3 = {r: jnp.concatenate([m, m, m], axis=1) for r, m in tri.items()}
    zero_block = jnp.zeros((c, HG_DK), BF16)
    keep = {False: jnp.logical_and(same_head, sj <= si), True: jnp.logical_and(same_head, sj >= si)}
    chains = ([(qf_ref, vf_ref, zf_ref, of_ref, b, False) for b in range(batch)]
              + [(qb_ref, vb_ref, zb_ref, ob_ref, b, True) for b in range(batch)])

    for cc in range(nchunks):
        work = []
        for q_ref, v_ref, z_ref, o_ref, b, reverse in chains:
            rows = pl.ds(((nchunks - 1 - cc) if reverse else cc) * c, c)
            sig = _sigmoid(z_ref[b, rows, :])
            log_f = jnp.log2(lb + (1.0 - lb) * sig)
            work.append(dict(rows=rows, reverse=reverse, o_ref=o_ref, b=b,
                             key=(1.0 - lb) * (1.0 - sig),
                             qs=_silu(q_ref[b, rows, :].astype(F32)) * (HG_DK ** -0.5),
                             v=v_ref[b, rows, :], split=_split3(log_f)))
        for w in work:
            f_hi, f_mid, f_lo = w["split"]
            w["bcum"] = jnp.dot(tri3[w["reverse"]], jnp.concatenate([f_lo, f_mid, f_hi], axis=0),
                                preferred_element_type=F32)
        for w in work:
            bcum = w["bcum"]
            end = 0 if w["reverse"] else c - 1
            ref_row = bcum[mid:mid + 1, :]
            b_end = bcum[end:end + 1, :]
            w["qe"] = (w["qs"] * jnp.exp2(bcum - ref_row)).astype(BF16)
            w["ke"] = w["key"] * jnp.exp2(ref_row - bcum)
            w["qd"] = (w["qs"] * jnp.exp2(bcum)).astype(BF16)
            w["kd"] = (w["key"] * jnp.exp2(b_end - bcum)).astype(BF16)
            w["decay"] = jnp.exp2(b_end)
        for w in work:
            w["att"] = jnp.dot(stack(w["qe"]), stack(w["ke"]).T.astype(BF16), preferred_element_type=F32)
        for w in work:
            att = jnp.where(keep[w["reverse"]], w["att"], 0.0).astype(BF16)
            w["intra"] = jnp.dot(att, stack(w["v"]), preferred_element_type=F32)
        for k, w in enumerate(work):
            w["st"] = [s_ref[k, h] for h in range(HG_HEADS)]
            w["inter"] = [jnp.dot(w["qd"][:, cols], w["st"][h].T.astype(BF16), preferred_element_type=F32)
                          for h, cols in enumerate(head_cols)]
        for w in work:
            upd = []
            for h in range(0, HG_HEADS, 2):
                ca, cb = head_cols[h], head_cols[h + 1]
                lhs = jnp.concatenate([w["v"][:, ca], w["v"][:, cb]], axis=0)
                rhs = jnp.concatenate([jnp.concatenate([w["kd"][:, ca], zero_block], axis=1),
                                       jnp.concatenate([zero_block, w["kd"][:, cb]], axis=1)], axis=0)
                pair = lax.dot_general(lhs, rhs, _TN, preferred_element_type=F32)
                upd += [pair[:, :HG_DK], pair[:, HG_DK:]]
            w["upd"] = upd
        for k, w in enumerate(work):
            for h, cols in enumerate(head_cols):
                w["o_ref"][w["b"], w["rows"], cols] = w["intra"][h * c:(h + 1) * c, :] + w["inter"][h]
                s_ref[k, h] = w["st"][h] * w["decay"][:, cols] + w["upd"][h]


def _gla(proj_bf16, proj_f32, lower_bounds, layer, batch, seq_len):
    d = HG_HEADS * HG_DK
    ts = min(GLA_ROWS, seq_len)
    nt = seq_len // ts
    depth = lower_bounds.shape[0]
    pb = proj_bf16.reshape(batch, seq_len, proj_bf16.shape[1])
    pf = proj_f32.reshape(batch, seq_len, proj_f32.shape[1])
    bwd = lambda i: nt - 1 - i
    blk = lambda tile, col: pl.BlockSpec((batch, ts, d), lambda i: (0, tile(i), col))
    fwd = lambda i: i
    out = jax.ShapeDtypeStruct((batch, seq_len, d), F32)
    o_fwd, o_bwd = pl.pallas_call(
        functools.partial(_gla_kernel, ts=ts, layer=layer, batch=batch),
        grid=(nt,),
        in_specs=[blk(fwd, 0), blk(fwd, 1), blk(fwd, 0), blk(bwd, 0), blk(bwd, 1), blk(bwd, 1),
                  pl.BlockSpec((depth, d), lambda i: (0, 0))],
        out_specs=[blk(fwd, 0), blk(bwd, 0)],
        out_shape=[out, out],
        scratch_shapes=[pltpu.VMEM((2 * batch, HG_HEADS, HG_DV, HG_DK), F32)],
        compiler_params=_params(("arbitrary",)),
        name="gla_scan",
    )(pb, pb, pf, pb, pb, pf, lower_bounds)
    return o_fwd.reshape(batch * seq_len, d), o_bwd.reshape(batch * seq_len, d)


def _top2(p):
    v1 = jnp.maximum(jnp.maximum(p[0], p[1]), jnp.maximum(p[2], p[3]))
    i1 = jnp.where(p[0] == v1, 0, jnp.where(p[1] == v1, 1, jnp.where(p[2] == v1, 2, 3)))
    q = [jnp.where(i1 == k, -1.0, p[k]) for k in range(4)]
    v2 = jnp.maximum(jnp.maximum(q[0], q[1]), jnp.maximum(q[2], q[3]))
    i2 = jnp.where(q[0] == v2, 0, jnp.where(q[1] == v2, 1, jnp.where(q[2] == v2, 2, 3)))
    return v1, i1, v2, i2


def _outproj_kernel(*refs, even, alpha, tm):
    if even:
        ret_ref, hf_ref, hb_ref, gr_ref, w0_ref, w1_ref = refs[:6]
        rest = refs[6:]
        lru = ((hf_ref[...] + hb_ref[...]) * jax.nn.gelu(gr_ref[...])).astype(BF16)
        y = (jnp.dot(ret_ref[...], w0_ref[...], preferred_element_type=F32)
             + jnp.dot(lru, w1_ref[...], preferred_element_type=F32))
    else:
        of_ref, ob_ref, g_ref, nw_ref, w0_ref = refs[:5]
        rest = refs[5:]
        o = of_ref[...] + ob_ref[...]
        ms = jnp.mean(o * o, axis=-1, keepdims=True)
        mix = o * lax.rsqrt(ms + RMS_EPS) * nw_ref[...] * _silu(g_ref[...].astype(F32))
        y = jnp.dot(mix.astype(BF16), w0_ref[...], preferred_element_type=F32)
    h_ref, lnw_ref, lnb_ref, rw_ref, o_ref, info_ref, cnt_ref, carry_ref = rest
    i = pl.program_id(0)

    @pl.when(i == 0)
    def _():
        carry_ref[...] = jnp.zeros_like(carry_ref)

    h1 = _layer_norm(alpha * h_ref[...] + y, lnw_ref[...], lnb_ref[...])
    o_ref[...] = h1

    h_hi = h1.astype(BF16)
    h_lo = (h1 - h_hi.astype(F32)).astype(BF16)
    rw = rw_ref[...]
    r_hi = rw.astype(BF16)
    r_lo = (rw - r_hi.astype(F32)).astype(BF16)
    logits = (lax.dot_general(r_lo, h_hi, _NT, preferred_element_type=F32)
              + lax.dot_general(r_hi, h_lo, _NT, preferred_element_type=F32)
              + lax.dot_general(r_hi, h_hi, _NT, preferred_element_type=F32))
    ex = jnp.exp(logits - jnp.max(logits, axis=0, keepdims=True))
    probs = ex / jnp.sum(ex, axis=0, keepdims=True)
    best = None
    for g in range(N_GROUPS):
        rows = [probs[g * EXPERTS_PER_GROUP + k:g * EXPERTS_PER_GROUP + k + 1, :]
                for k in range(EXPERTS_PER_GROUP)]
        v1, i1, v2, i2 = _top2(rows)
        cand = (v1 + v2, v1, i1 + g * EXPERTS_PER_GROUP, v2, i2 + g * EXPERTS_PER_GROUP)
        if best is None:
            best = cand
        else:
            take = cand[0] > best[0]
            best = tuple(jnp.where(take, cn, bs) for cn, bs in zip(cand, best))
    _, v1, e1, v2, e2 = best
    denom = v1 + v2
    g1 = v1 / denom
    g2 = v2 / denom

    eid = lax.broadcasted_iota(jnp.int32, (N_EXPERTS, tm), 0)
    oh1 = (eid == e1).astype(F32)
    oh2 = (eid == e2).astype(F32)
    oh = oh1 + oh2
    tt = lax.broadcasted_iota(jnp.int32, (tm, tm), 0)
    uu = lax.broadcasted_iota(jnp.int32, (tm, tm), 1)
    before = (tt < uu).astype(BF16)
    base = carry_ref[:, 0:1] + jnp.dot(oh.astype(BF16), before, preferred_element_type=F32)
    rank1 = jnp.sum(oh1 * base, axis=0, keepdims=True)
    rank2 = jnp.sum(oh2 * base, axis=0, keepdims=True)
    carry_ref[...] = carry_ref[...] + jnp.sum(oh, axis=1, keepdims=True)
    cnt_ref[...] = carry_ref[...]
    zero = jnp.zeros_like(g1)
    info_ref[...] = jnp.concatenate(
        [e1.astype(F32), e2.astype(F32), g1, g2, rank1, rank2, zero, zero], axis=0)


def _out_project(even, mixer_inputs, weights_bf16, h, ln_w, ln_b, router_wt, alpha):
    t, d = h.shape
    tm = TOKEN_ROWS
    row = lambda i: (i, 0)
    const = lambda i: (0, 0)
    if even:
        ret, h_fwd, h_bwd, proj_f32 = mixer_inputs
        w = h_fwd.shape[1]
        in_specs = [pl.BlockSpec((tm, ret.shape[1]), row), pl.BlockSpec((tm, w), row),
                    pl.BlockSpec((tm, w), row), pl.BlockSpec((tm, w), lambda i: (i, 1))]
        args = [ret, h_fwd, h_bwd, proj_f32]
    else:
        o_fwd, o_bwd, proj_bf16, norm_w = mixer_inputs
        in_specs = [pl.BlockSpec((tm, d), row), pl.BlockSpec((tm, d), row),
                    pl.BlockSpec((tm, d), lambda i: (i, 2)), pl.BlockSpec((1, d), const)]
        args = [o_fwd, o_bwd, proj_bf16, norm_w.reshape(1, d)]
    in_specs += [pl.BlockSpec(wm.shape, const) for wm in weights_bf16]
    in_specs += [pl.BlockSpec((tm, d), row), pl.BlockSpec((1, d), const), pl.BlockSpec((1, d), const),
                 pl.BlockSpec((N_EXPERTS, d), const)]
    args += list(weights_bf16) + [h, ln_w.reshape(1, d), ln_b.reshape(1, d), router_wt]
    return pl.pallas_call(
        functools.partial(_outproj_kernel, even=even, alpha=alpha, tm=tm),
        grid=(t // tm,),
        in_specs=in_specs,
        out_specs=[pl.BlockSpec((tm, d), row),
                   pl.BlockSpec((SUBLANES, tm), lambda i: (0, i)),
                   pl.BlockSpec((N_EXPERTS, LANES), const)],
        out_shape=[jax.ShapeDtypeStruct((t, d), F32),
                   jax.ShapeDtypeStruct((SUBLANES, t), F32),
                   jax.ShapeDtypeStruct((N_EXPERTS, LANES), F32)],
        scratch_shapes=[pltpu.VMEM((N_EXPERTS, LANES), F32)],
        compiler_params=_params(("arbitrary",)),
        name="out_proj_router",
    )(*args)


def _row_copy(src_ref, src_row, dst_ref, dst_row, sem):
    return pltpu.make_async_copy(src_ref.at[pl.ds(src_row, 1)], dst_ref.at[pl.ds(dst_row, 1)], sem)


def _dispatch_kernel(d1_ref, d2_ref, zs_ref, h_ref, xb_ref, zero_ref, sem, *, tm):
    i = pl.program_id(0)
    base = i * tm

    @pl.when(i == 0)
    def _():
        zero_ref[...] = jnp.zeros_like(zero_ref)

        def clear(row):
            start = pl.multiple_of(row, MOE_ROWS)
            return pltpu.make_async_copy(zero_ref, xb_ref.at[pl.ds(start, MOE_ROWS)], sem.at[0])

        for e in range(N_EXPERTS):
            clear(zs_ref[e]).start()
        for e in range(N_EXPERTS):
            clear(zs_ref[e]).wait()

        def clear_tail(b, carry):
            clear(b * MOE_ROWS).start()
            clear(b * MOE_ROWS).wait()
            return carry

        lax.fori_loop(zs_ref[N_EXPERTS], xb_ref.shape[0] // MOE_ROWS, clear_tail, 0)

    def start(t, carry):
        _row_copy(h_ref, t, xb_ref, d1_ref[base + t], sem.at[0]).start()
        _row_copy(h_ref, t, xb_ref, d2_ref[base + t], sem.at[1]).start(priority=1)
        return carry

    lax.fori_loop(0, tm, start, 0, unroll=8)
    pltpu.make_async_copy(h_ref, xb_ref.at[pl.ds(0, tm)], sem.at[0]).wait()
    pltpu.make_async_copy(h_ref, xb_ref.at[pl.ds(0, tm)], sem.at[1]).wait()


def _dispatch(h, dest1, dest2, zero_start, n_rows):
    t, d = h.shape
    tm = TOKEN_ROWS
    return pl.pallas_call(
        functools.partial(_dispatch_kernel, tm=tm),
        grid_spec=pltpu.PrefetchScalarGridSpec(
            num_scalar_prefetch=3,
            grid=(t // tm,),
            in_specs=[pl.BlockSpec((tm, d), lambda i, d1, d2, zs: (i, 0))],
            out_specs=pl.BlockSpec(memory_space=pl.ANY),
            scratch_shapes=[pltpu.VMEM((MOE_ROWS, d), F32), pltpu.SemaphoreType.DMA((2,))],
        ),
        out_shape=jax.ShapeDtypeStruct((n_rows, d), F32),
        compiler_params=_params(("arbitrary",)),
        name="moe_dispatch",
    )(dest1, dest2, zero_start, h)


def _expert_kernel(be_ref, nv_ref, x_ref, wg_ref, wu_ref, wd_ref, o_ref, wgb_ref, wub_ref, wdb_ref):
    i = pl.program_id(0)
    changed = jnp.logical_or(i == 0, be_ref[i] != be_ref[jnp.maximum(i - 1, 0)])

    @pl.when(changed)
    def _():
        wgb_ref[...] = wg_ref[...].astype(BF16)
        wub_ref[...] = wu_ref[...].astype(BF16)
        wdb_ref[...] = wd_ref[...].astype(BF16)

    @pl.when(i < nv_ref[0])
    def _():
        x = x_ref[...].astype(BF16)
        gate = jnp.dot(x, wgb_ref[...], preferred_element_type=F32)
        up = jnp.dot(x, wub_ref[...], preferred_element_type=F32)
        hid = (_silu(gate) * up).astype(BF16)
        o_ref[...] = jnp.dot(hid, wdb_ref[...], preferred_element_type=F32)

    @pl.when(i >= nv_ref[0])
    def _():
        o_ref[...] = jnp.zeros_like(o_ref)


def _experts(xb, block_e, n_valid, w_gate, w_up, w_down, layer):
    p, d = xb.shape
    de = w_gate.shape[3]
    nb = p // MOE_ROWS
    return pl.pallas_call(
        _expert_kernel,
        grid_spec=pltpu.PrefetchScalarGridSpec(
            num_scalar_prefetch=2,
            grid=(nb,),
            in_specs=[
                pl.BlockSpec((MOE_ROWS, d), lambda i, be, nv: (jnp.maximum(jnp.minimum(i, nv[0] - 1), 0), 0)),
                pl.BlockSpec((None, None, d, de), lambda i, be, nv: (layer, be[i], 0, 0)),
                pl.BlockSpec((None, None, d, de), lambda i, be, nv: (layer, be[i], 0, 0)),
                pl.BlockSpec((None, None, de, d), lambda i, be, nv: (layer, be[i], 0, 0)),
            ],
            out_specs=pl.BlockSpec((MOE_ROWS, d), lambda i, be, nv: (i, 0)),
            scratch_shapes=[pltpu.VMEM((d, de), BF16), pltpu.VMEM((d, de), BF16),
                            pltpu.VMEM((de, d), BF16)],
        ),
        out_shape=jax.ShapeDtypeStruct((p, d), F32),
        compiler_params=_params(("arbitrary",)),
        name="moe_experts",
    )(block_e, n_valid, xb, w_gate, w_up, w_down)


def _combine_kernel(d1_ref, d2_ref, h_ref, gates_ref, lnw_ref, lnb_ref, yb_ref, o_ref, r_ref, sem,
                    *, tm, alpha):
    base = pl.program_id(0) * tm

    def start(t, carry):
        _row_copy(yb_ref, d1_ref[base + t], r_ref.at[0], t, sem.at[0]).start()
        _row_copy(yb_ref, d2_ref[base + t], r_ref.at[1], t, sem.at[1]).start(priority=1)
        return carry

    lax.fori_loop(0, tm, start, 0, unroll=8)
    pltpu.make_async_copy(yb_ref.at[pl.ds(0, tm)], r_ref.at[0], sem.at[0]).wait()
    pltpu.make_async_copy(yb_ref.at[pl.ds(0, tm)], r_ref.at[1], sem.at[1]).wait()
    y = gates_ref[:, 0:1] * r_ref[0] + gates_ref[:, 1:2] * r_ref[1]
    o_ref[...] = _layer_norm(alpha * h_ref[...] + y, lnw_ref[...], lnb_ref[...])


def _combine(h, yb, dest1, dest2, gates, ln_w, ln_b, alpha):
    t, d = h.shape
    tm = TOKEN_ROWS
    return pl.pallas_call(
        functools.partial(_combine_kernel, tm=tm, alpha=alpha),
        grid_spec=pltpu.PrefetchScalarGridSpec(
            num_scalar_prefetch=2,
            grid=(t // tm,),
            in_specs=[pl.BlockSpec((tm, d), lambda i, d1, d2: (i, 0)),
                      pl.BlockSpec((tm, 2), lambda i, d1, d2: (i, 0)),
                      pl.BlockSpec((1, d), lambda i, d1, d2: (0, 0)),
                      pl.BlockSpec((1, d), lambda i, d1, d2: (0, 0)),
                      pl.BlockSpec(memory_space=pl.ANY)],
            out_specs=pl.BlockSpec((tm, d), lambda i, d1, d2: (i, 0)),
            scratch_shapes=[pltpu.VMEM((2, tm, d), F32), pltpu.SemaphoreType.DMA((2,))],
        ),
        out_shape=jax.ShapeDtypeStruct((t, d), F32),
        compiler_params=_params(("arbitrary",)),
        name="moe_combine",
    )(dest1, dest2, h, gates, ln_w.reshape(1, d), ln_b.reshape(1, d), yb)


def _invert_kernel(d1_ref, d2_ref, init_ref, ids_ref, sem, *, t):
    fill = pltpu.make_async_copy(init_ref, ids_ref, sem)
    fill.start()
    fill.wait()

    def body(tok, carry):
        ids_ref[d1_ref[tok]] = tok
        ids_ref[d2_ref[tok]] = tok + t
        return carry

    lax.fori_loop(0, t, body, 0, unroll=8)


def _invert(dest1, dest2, n_rows):
    t = dest1.shape[0]
    smem = pl.BlockSpec(memory_space=pltpu.SMEM)
    return pl.pallas_call(
        functools.partial(_invert_kernel, t=t),
        in_specs=[smem, smem, pl.BlockSpec(memory_space=pl.ANY)],
        out_specs=smem,
        out_shape=jax.ShapeDtypeStruct((n_rows,), jnp.int32),
        scratch_shapes=[pltpu.SemaphoreType.DMA(())],
        name="moe_invert",
    )(dest1, dest2, jnp.full((n_rows,), -1, jnp.int32))


def _expert_fused_kernel(be_ref, src_ref, dst_ref, h_ref, wg_ref, wu_ref, wd_ref, yk_ref,
                         xbuf, ybuf, wgb_ref, wub_ref, wdb_ref, sem_g, sem_s):
    i = pl.program_id(0)
    nb = pl.num_programs(0)
    slot = i % 2
    other = 1 - slot

    def gather(block, buf):
        base = block * MOE_ROWS
        for r in range(MOE_ROWS):
            _row_copy(h_ref, src_ref[base + r], xbuf.at[buf], r, sem_g.at[buf]).start()

    def scatter(block, buf):
        base = block * MOE_ROWS
        for r in range(MOE_ROWS):
            _row_copy(ybuf.at[buf], r, yk_ref, dst_ref[base + r], sem_s.at[buf]).start(priority=1)

    def wait_gather(buf):
        pltpu.make_async_copy(h_ref.at[pl.ds(0, MOE_ROWS)], xbuf.at[buf], sem_g.at[buf]).wait()

    def wait_scatter(buf):
        pltpu.make_async_copy(ybuf.at[buf], yk_ref.at[pl.ds(0, MOE_ROWS)], sem_s.at[buf]).wait()

    @pl.when(i == 0)
    def _():
        ybuf[...] = jnp.zeros_like(ybuf)
        gather(0, 0)

    wait_gather(slot)

    @pl.when(i >= 1)
    def _():
        wait_scatter(slot)

    changed = jnp.logical_or(i == 0, be_ref[i] != be_ref[jnp.maximum(i - 1, 0)])

    @pl.when(changed)
    def _():
        wgb_ref[...] = wg_ref[...].astype(BF16)
        wub_ref[...] = wu_ref[...].astype(BF16)
        wdb_ref[...] = wd_ref[...].astype(BF16)

    gather(jnp.minimum(i + 1, nb - 1), other)
    scatter(jnp.maximum(i - 1, 0), other)
    x = xbuf[slot].astype(BF16)
    gate = jnp.dot(x, wgb_ref[...], preferred_element_type=F32)
    up = jnp.dot(x, wub_ref[...], preferred_element_type=F32)
    hid = (_silu(gate) * up).astype(BF16)
    ybuf[slot] = jnp.dot(hid, wdb_ref[...], preferred_element_type=F32)

    @pl.when(i == nb - 1)
    def _():
        scatter(i, slot)
        wait_gather(other)
        wait_scatter(other)
        wait_scatter(slot)


def _experts_fused(h, src_tok, dst_row, block_e, w_gate, w_up, w_down, layer):
    t, d = h.shape
    de = w_gate.shape[3]
    p = src_tok.shape[0]
    nb = p // MOE_ROWS
    weight = lambda shape: pl.BlockSpec((None, None) + shape, lambda i, be, src, dst: (layer, be[i], 0, 0))
    return pl.pallas_call(
        _expert_fused_kernel,
        grid_spec=pltpu.PrefetchScalarGridSpec(
            num_scalar_prefetch=3,
            grid=(nb,),
            in_specs=[pl.BlockSpec(memory_space=pl.ANY), weight((d, de)), weight((d, de)), weight((de, d))],
            out_specs=pl.BlockSpec(memory_space=pl.ANY),
            scratch_shapes=[pltpu.VMEM((2, MOE_ROWS, d), F32), pltpu.VMEM((2, MOE_ROWS, d), F32),
                            pltpu.VMEM((d, de), BF16), pltpu.VMEM((d, de), BF16), pltpu.VMEM((de, d), BF16),
                            pltpu.SemaphoreType.DMA((2,)), pltpu.SemaphoreType.DMA((2,))],
        ),
        out_shape=jax.ShapeDtypeStruct((p, d), F32),
        compiler_params=_params(("arbitrary",)),
        name="moe_experts",
    )(block_e, src_tok, dst_row, h, w_gate, w_up, w_down)


def _combine_dense_kernel(h_ref, y0_ref, y1_ref, gates_ref, lnw_ref, lnb_ref, o_ref, *, alpha):
    y = gates_ref[:, 0:1] * y0_ref[...] + gates_ref[:, 1:2] * y1_ref[...]
    o_ref[...] = _layer_norm(alpha * h_ref[...] + y, lnw_ref[...], lnb_ref[...])


def _combine_dense(h, yk, gates, ln_w, ln_b, alpha):
    t, d = h.shape
    tm = TOKEN_ROWS
    nt = t // tm
    row = lambda i: (i, 0)
    const = lambda i: (0, 0)
    return pl.pallas_call(
        functools.partial(_combine_dense_kernel, alpha=alpha),
        grid=(nt,),
        in_specs=[pl.BlockSpec((tm, d), row), pl.BlockSpec((tm, d), row),
                  pl.BlockSpec((tm, d), lambda i: (nt + i, 0)), pl.BlockSpec((tm, 2), row),
                  pl.BlockSpec((1, d), const), pl.BlockSpec((1, d), const)],
        out_specs=pl.BlockSpec((tm, d), row),
        out_shape=jax.ShapeDtypeStruct((t, d), F32),
        compiler_params=_params(("arbitrary",)),
        name="moe_combine",
    )(h, yk, yk, gates, ln_w.reshape(1, d), ln_b.reshape(1, d))


def _moe_fused(h1, info, counts, w_gate, w_up, w_down, layer, ln_w, ln_b, alpha):
    t = h1.shape[0]
    n_rows = 2 * t + N_EXPERTS * MOE_ROWS
    nb = n_rows // MOE_ROWS
    cnt = counts[:, 0].astype(jnp.int32)
    padded = (cnt + MOE_ROWS - 1) // MOE_ROWS * MOE_ROWS
    ends = jnp.cumsum(padded)
    pstart = ends - padded
    e1 = info[0].astype(jnp.int32)
    e2 = info[1].astype(jnp.int32)
    expert_ids = jnp.arange(N_EXPERTS, dtype=jnp.int32)[:, None]
    dest1 = jnp.sum(jnp.where(e1[None, :] == expert_ids, pstart[:, None], 0), axis=0) + info[4].astype(jnp.int32)
    dest2 = jnp.sum(jnp.where(e2[None, :] == expert_ids, pstart[:, None], 0), axis=0) + info[5].astype(jnp.int32)
    gates = jnp.stack([info[2], info[3]], axis=1)
    blk = jnp.arange(nb, dtype=jnp.int32) * MOE_ROWS
    block_e = jnp.minimum(jnp.sum((ends[None, :] <= blk[:, None]).astype(jnp.int32), axis=1), N_EXPERTS - 1)
    ids = _invert(dest1, dest2, n_rows)
    is_pad = ids < 0
    src_tok = jnp.where(is_pad, 0, ids % t)
    dst_row = jnp.where(is_pad, 2 * t - 1 + jnp.cumsum(is_pad.astype(jnp.int32)), ids)
    yk = _experts_fused(h1, src_tok, dst_row, block_e, w_gate, w_up, w_down, layer)
    return _combine_dense(h1, yk, gates, ln_w, ln_b, alpha)


def _pack_halves(x):
    n = x.shape[1] // 2
    hi = pltpu.bitcast(x[:, :n], jnp.uint32)
    lo = pltpu.bitcast(x[:, n:], jnp.uint32)
    return hi | (lo >> 16)


def _unpack_halves(p):
    hi = pltpu.bitcast(p & jnp.uint32(0xFFFF0000), F32)
    lo = pltpu.bitcast(p << 16, F32)
    return jnp.concatenate([hi.astype(BF16), lo.astype(BF16)], axis=1)


def _seg_expert_kernel(first_ref, count_ref, tail_ref, x_ref, wg_ref, wu_ref, wd_ref, y_ref,
                       xbuf, ybuf, wgb_ref, wub_ref, wdb_ref, semx, semy):
    e = pl.program_id(0)
    wgb_ref[...] = wg_ref[...].astype(BF16)
    wub_ref[...] = wu_ref[...].astype(BF16)
    wdb_ref[...] = wd_ref[...].astype(BF16)
    first = first_ref[e]
    n_used = tail_ref[0]

    def rows(g):
        return pl.ds(pl.multiple_of(g * MOE_ROWS, MOE_ROWS), MOE_ROWS)

    def fetch(g):
        return pltpu.make_async_copy(x_ref.at[rows(g)], xbuf.at[g % 2], semx.at[g % 2])

    def put(g):
        return pltpu.make_async_copy(ybuf.at[g % 2], y_ref.at[rows(g)], semy.at[g % 2])

    @pl.when(e == 0)
    def _():
        fetch(0).start()

    def body(g, carry):
        @pl.when(g + 1 < n_used)
        def _():
            fetch(g + 1).start()

        fetch(g).wait()

        @pl.when(g >= 2)
        def _():
            put(g - 2).wait()

        x = _unpack_halves(xbuf[g % 2])
        gate = jnp.dot(x, wgb_ref[...], preferred_element_type=F32)
        up = jnp.dot(x, wub_ref[...], preferred_element_type=F32)
        hid = (_silu(gate) * up).astype(BF16)
        y = jnp.dot(hid, wdb_ref[...], preferred_element_type=F32)
        ybuf[g % 2] = _pack_halves(y.astype(BF16).astype(F32))
        put(g).start()
        return carry

    lax.fori_loop(first, first + count_ref[e], body, 0)

    @pl.when(e == pl.num_programs(0) - 1)
    def _():
        @pl.when(n_used >= 2)
        def _():
            put(n_used - 2).wait()

        put(n_used - 1).wait()
        ybuf[0] = jnp.zeros_like(ybuf[0])

        def clear(b, carry):
            cp = pltpu.make_async_copy(
                ybuf.at[0], y_ref.at[pl.ds(pl.multiple_of(b * MOE_ROWS, MOE_ROWS), MOE_ROWS)], semy.at[0])
            cp.start()
            cp.wait()
            return carry

        lax.fori_loop(tail_ref[0], y_ref.shape[0] // MOE_ROWS, clear, 0)


def _seg_experts(xb, seg_first, seg_count, n_valid, w_gate, w_up, w_down, layer):
    p, half = xb.shape
    d = 2 * half
    de = w_gate.shape[3]
    weight = lambda shape: pl.BlockSpec((None, None) + shape, lambda e, a, b, c: (layer, e, 0, 0))
    return pl.pallas_call(
        _seg_expert_kernel,
        grid_spec=pltpu.PrefetchScalarGridSpec(
            num_scalar_prefetch=3,
            grid=(N_EXPERTS,),
            in_specs=[pl.BlockSpec(memory_space=pl.ANY), weight((d, de)), weight((d, de)), weight((de, d))],
            out_specs=pl.BlockSpec(memory_space=pl.ANY),
            scratch_shapes=[pltpu.VMEM((2, MOE_ROWS, half), jnp.uint32),
                            pltpu.VMEM((2, MOE_ROWS, half), jnp.uint32),
                            pltpu.VMEM((d, de), BF16), pltpu.VMEM((d, de), BF16), pltpu.VMEM((de, d), BF16),
                            pltpu.SemaphoreType.DMA((2,)), pltpu.SemaphoreType.DMA((2,))],
        ),
        out_shape=jax.ShapeDtypeStruct((p, half), jnp.uint32),
        compiler_params=_params(("arbitrary",)),
        name="moe_experts",
    )(seg_first, seg_count, n_valid, xb, w_gate, w_up, w_down)


def _clear_padding_blocks(zs_ref, xb_ref, zero_ref, sem):
    zero_ref[...] = jnp.zeros_like(zero_ref)

    def clear(row):
        start = pl.multiple_of(row, MOE_ROWS)
        return pltpu.make_async_copy(zero_ref, xb_ref.at[pl.ds(start, MOE_ROWS)], sem)

    for e in range(N_EXPERTS):
        clear(zs_ref[e]).start()
    for e in range(N_EXPERTS):
        clear(zs_ref[e]).wait()

    def clear_tail(b, carry):
        clear(b * MOE_ROWS).start()
        clear(b * MOE_ROWS).wait()
        return carry

    lax.fori_loop(zs_ref[N_EXPERTS], xb_ref.shape[0] // MOE_ROWS, clear_tail, 0)


def _run_copy(src_ref, src_row, dst_ref, dst_row, sem):
    src = pl.multiple_of(src_row, SUBLANES)
    dst = pl.multiple_of(dst_row, SUBLANES)
    return pltpu.make_async_copy(src_ref.at[pl.ds(src, SUBLANES)], dst_ref.at[pl.ds(dst, SUBLANES)], sem)


def _sort_dispatch_kernel(dst_ref, nch_ref, zs_ref, h_ref, lp_ref, xb_ref, xs_ref, zero_ref, sem, *, tm, nck):
    i = pl.program_id(0)

    @pl.when(i == 0)
    def _():
        _clear_padding_blocks(zs_ref, xb_ref, zero_ref, sem.at[0])

    slot = i % 2
    last = pl.num_programs(0) - 1

    def wait_runs(buf, count):
        def wait(j, carry):
            _run_copy(xs_ref.at[buf], 0, xb_ref, 0, sem.at[buf]).wait()
            return carry

        lax.fori_loop(0, count, wait, 0)

    @pl.when(i >= 2)
    def _():
        wait_runs(slot, nch_ref[jnp.maximum(i - 2, 0)])

    lp = lp_ref[...]
    pos = lax.broadcasted_iota(jnp.int32, (xs_ref.shape[1], tm), 0)
    perm = jnp.logical_or(pos == lp[0:1, :], pos == lp[1:2, :]).astype(BF16)
    xs_ref[slot] = _pack_halves(jnp.dot(perm, h_ref[...].astype(BF16), preferred_element_type=F32))

    def start(j, carry):
        _run_copy(xs_ref.at[slot], j * SUBLANES, xb_ref, dst_ref[i * nck + j], sem.at[slot]).start()
        return carry

    lax.fori_loop(0, nch_ref[i], start, 0)

    @pl.when(i == last)
    def _():
        @pl.when(i >= 1)
        def _():
            wait_runs(1 - slot, nch_ref[jnp.maximum(i - 1, 0)])

        wait_runs(slot, nch_ref[i])


def _sort_dispatch(h, lp_rows, chunk_dst, n_chunks, zero_start, n_rows, tm, sorted_rows):
    t, d = h.shape
    nck = sorted_rows // SUBLANES
    return pl.pallas_call(
        functools.partial(_sort_dispatch_kernel, tm=tm, nck=nck),
        grid_spec=pltpu.PrefetchScalarGridSpec(
            num_scalar_prefetch=3,
            grid=(t // tm,),
            in_specs=[pl.BlockSpec((tm, d), lambda i, a, b, c: (i, 0)),
                      pl.BlockSpec((2, tm), lambda i, a, b, c: (0, i))],
            out_specs=pl.BlockSpec(memory_space=pl.ANY),
            scratch_shapes=[pltpu.VMEM((2, sorted_rows, d // 2), jnp.uint32),
                            pltpu.VMEM((MOE_ROWS, d // 2), jnp.uint32), pltpu.SemaphoreType.DMA((2,))],
        ),
        out_shape=jax.ShapeDtypeStruct((n_rows, d // 2), jnp.uint32),
        compiler_params=_params(("arbitrary",)),
        name="moe_dispatch",
    )(chunk_dst, n_chunks, zero_start, h, lp_rows)


def _sort_combine_kernel(src_ref, nch_ref, h_ref, lp_ref, gates_ref, lnw_ref, lnb_ref, yb_ref, o_ref,
                         ys_ref, sem, *, tm, nck, alpha):
    i = pl.program_id(0)

    slot = i % 2

    def fetch(tile, buf):
        def start(j, carry):
            _run_copy(yb_ref, src_ref[tile * nck + j], ys_ref.at[buf], j * SUBLANES, sem.at[buf]).start()
            return carry

        lax.fori_loop(0, nch_ref[tile], start, 0)

    @pl.when(i == 0)
    def _():
        ys_ref[...] = jnp.zeros_like(ys_ref)
        fetch(0, 0)

    @pl.when(i + 1 < pl.num_programs(0))
    def _():
        fetch(i + 1, 1 - slot)

    def wait(j, carry):
        _run_copy(yb_ref, 0, ys_ref.at[slot], 0, sem.at[slot]).wait()
        return carry

    lax.fori_loop(0, nch_ref[i], wait, 0)

    lp = lp_ref[...]
    pos = lax.broadcasted_iota(jnp.int32, (tm, ys_ref.shape[1]), 1)
    ys = _unpack_halves(ys_ref[slot])
    y1 = jnp.dot((pos == lp[:, 0:1]).astype(BF16), ys, preferred_element_type=F32)
    y2 = jnp.dot((pos == lp[:, 1:2]).astype(BF16), ys, preferred_element_type=F32)
    y = gates_ref[:, 0:1] * y1 + gates_ref[:, 1:2] * y2
    o_ref[...] = _layer_norm(alpha * h_ref[...] + y, lnw_ref[...], lnb_ref[...])


def _sort_combine(h, yb, lp_cols, gates, chunk_src, n_chunks, ln_w, ln_b, alpha, tm, sorted_rows):
    t, d = h.shape
    nck = sorted_rows // SUBLANES
    row = lambda i, a, b: (i, 0)
    const = lambda i, a, b: (0, 0)
    return pl.pallas_call(
        functools.partial(_sort_combine_kernel, tm=tm, nck=nck, alpha=alpha),
        grid_spec=pltpu.PrefetchScalarGridSpec(
            num_scalar_prefetch=2,
            grid=(t // tm,),
            in_specs=[pl.BlockSpec((tm, d), row), pl.BlockSpec((tm, 2), row), pl.BlockSpec((tm, 2), row),
                      pl.BlockSpec((1, d), const), pl.BlockSpec((1, d), const),
                      pl.BlockSpec(memory_space=pl.ANY)],
            out_specs=pl.BlockSpec((tm, d), row),
            scratch_shapes=[pltpu.VMEM((2, sorted_rows, d // 2), jnp.uint32), pltpu.SemaphoreType.DMA((2,))],
        ),
        out_shape=jax.ShapeDtypeStruct((t, d), F32),
        compiler_params=_params(("arbitrary",)),
        name="moe_combine",
    )(chunk_src, n_chunks, h, lp_cols, gates, ln_w.reshape(1, d), ln_b.reshape(1, d), yb)


def _moe_sorted(h1, info, w_gate, w_up, w_down, layer, ln_w, ln_b, alpha):
    t = h1.shape[0]
    tm = SORT_ROWS
    ntile = t // tm
    run_pad = SUBLANES - 1
    sorted_rows = 2 * tm + LANES
    nck = sorted_rows // SUBLANES
    n_rows = (2 * t + ntile * N_EXPERTS * run_pad + MOE_ROWS - 1) // MOE_ROWS * MOE_ROWS + N_EXPERTS * MOE_ROWS
    nb = n_rows // MOE_ROWS
    i32 = jnp.int32
    e1, e2 = info[0].astype(i32), info[1].astype(i32)
    rank1, rank2 = info[4].astype(i32), info[5].astype(i32)
    expert_ids = jnp.arange(N_EXPERTS, dtype=i32)[None, :]
    oh1 = e1[:, None] == expert_ids
    oh2 = e2[:, None] == expert_ids
    cnt = jnp.logical_or(oh1, oh2).astype(i32).reshape(ntile, tm, N_EXPERTS).sum(axis=1)
    cnt8 = (cnt + run_pad) // SUBLANES * SUBLANES
    local = jnp.cumsum(cnt8, axis=1) - cnt8
    before = jnp.cumsum(cnt, axis=0) - cnt
    seg = cnt8.sum(axis=0)
    seg_pad = (seg + MOE_ROWS - 1) // MOE_ROWS * MOE_ROWS
    ends = jnp.cumsum(seg_pad)
    slot = (ends - seg_pad)[None, :] + jnp.cumsum(cnt8, axis=0) - cnt8
    shift = jnp.repeat(local - before, tm, axis=0)
    lp1 = jnp.sum(jnp.where(oh1, shift, 0), axis=1) + rank1
    lp2 = jnp.sum(jnp.where(oh2, shift, 0), axis=1) + rank2
    chunk_row = jnp.arange(nck, dtype=i32) * SUBLANES
    run_of = jnp.sum(((local + cnt8)[:, None, :] <= chunk_row[None, :, None]).astype(i32), axis=2)
    run_of = jnp.minimum(run_of, N_EXPERTS - 1)
    chunk_slot = jnp.sum(jnp.where(run_of[:, :, None] == expert_ids[None], (slot - local)[:, None, :], 0),
                         axis=2) + chunk_row[None, :]
    n_chunks = cnt8.sum(axis=1) // SUBLANES
    n_valid = (ends[-1] // MOE_ROWS).astype(i32).reshape(1)
    zero_start = jnp.concatenate([jnp.maximum(ends - MOE_ROWS, 0), n_valid]).astype(i32)
    gates = jnp.stack([info[2], info[3]], axis=1)
    chunk_slot = chunk_slot.reshape(-1).astype(i32)
    seg_first = ((ends - seg_pad) // MOE_ROWS).astype(i32)
    seg_count = (seg_pad // MOE_ROWS).astype(i32)

    xb = _sort_dispatch(h1, jnp.stack([lp1, lp2], axis=0), chunk_slot, n_chunks, zero_start, n_rows, tm,
                        sorted_rows)
    yb = _seg_experts(xb, seg_first, seg_count, n_valid, w_gate, w_up, w_down, layer)
    return _sort_combine(h1, yb, jnp.stack([lp1, lp2], axis=1), gates, chunk_slot, n_chunks, ln_w, ln_b,
                         alpha, tm, sorted_rows)


def _moe(h1, info, counts, w_gate, w_up, w_down, layer, ln_w, ln_b, alpha):
    t = h1.shape[0]
    n_rows = 2 * t + N_EXPERTS * MOE_ROWS
    nb = n_rows // MOE_ROWS
    cnt = counts[:, 0].astype(jnp.int32)
    padded = (cnt + MOE_ROWS - 1) // MOE_ROWS * MOE_ROWS
    ends = jnp.cumsum(padded)
    pstart = ends - padded
    e1 = info[0].astype(jnp.int32)
    e2 = info[1].astype(jnp.int32)
    expert_ids = jnp.arange(N_EXPERTS, dtype=jnp.int32)[:, None]
    dest1 = jnp.sum(jnp.where(e1[None, :] == expert_ids, pstart[:, None], 0), axis=0) + info[4].astype(jnp.int32)
    dest2 = jnp.sum(jnp.where(e2[None, :] == expert_ids, pstart[:, None], 0), axis=0) + info[5].astype(jnp.int32)
    gates = jnp.stack([info[2], info[3]], axis=1)
    n_valid = (ends[-1] // MOE_ROWS).astype(jnp.int32).reshape(1)
    blk = jnp.minimum(jnp.arange(nb, dtype=jnp.int32), n_valid[0] - 1) * MOE_ROWS
    block_e = jnp.minimum(jnp.sum((ends[None, :] <= blk[:, None]).astype(jnp.int32), axis=1), N_EXPERTS - 1)
    zero_start = jnp.concatenate([jnp.maximum(ends - MOE_ROWS, 0), n_valid]).astype(jnp.int32)

    xb = _dispatch(h1, dest1, dest2, zero_start, n_rows)
    yb = _experts(xb, block_e, n_valid, w_gate, w_up, w_down, layer)
    return _combine(h1, yb, dest1, dest2, gates, ln_w, ln_b, alpha)


def _rotary_column_order(w_in):
    d = w_in.shape[0]
    nq = RET_HEADS * RET_DK

    def perm(w):
        return w.reshape(d, RET_HEADS, RET_DK // 2, 2).transpose(0, 3, 1, 2).reshape(d, nq)

    return jnp.concatenate([perm(w_in[:, :nq]), perm(w_in[:, nq:2 * nq]), w_in[:, 2 * nq:]], axis=1)


def kernel(x, w_in_even, w_out_even, lru_conv_w, lru_conv_b, lru_gate_w, lru_gate_b, lru_lambda,
           w_in_odd, w_out_odd, hg_lower_bounds, hg_norm_w, ln_w, ln_b, router_w,
           moe_w_gate, moe_w_up, moe_w_down):
    batch, seq_len, d = x.shape
    depth = ln_w.shape[0]
    alpha = (2.0 * depth) ** 0.25
    t = batch * seq_len
    h = x.reshape(t, d)
    router_wt = router_w.T
    half = RET_DK // 2
    inv_freq = ROPE_BASE ** (-jnp.arange(0, RET_DK, 2, dtype=F32) / RET_DK)
    inv_freq = jnp.tile(inv_freq, LANES // half).reshape(1, LANES)
    nret = RET_HEADS * RET_DV

    for layer in range(depth):
        j = layer // 2
        if layer % 2 == 0:
            w_in = _rotary_column_order(w_in_even[j]).astype(BF16)
            n_bf = 2 * RET_HEADS * RET_DK + 2 * nret
            proj_b, proj_f = _project(h, w_in, inv_freq, n_bf, rotary=True, seq_len=seq_len)
            ret = _retention(proj_b, batch, seq_len)
            h_fwd, h_bwd = _rglru(proj_f, lru_conv_w[j], lru_conv_b[j], lru_gate_w[j], lru_gate_b[j],
                                  lru_lambda[j], batch, seq_len)
            w_out = w_out_even[j].astype(BF16)
            mixer_inputs, weights = (ret, h_fwd, h_bwd, proj_f), [w_out[:nret], w_out[nret:]]
        else:
            w = w_in_odd[j]
            w_in = jnp.concatenate([w[:, :2 * d], w[:, 4 * d:], w[:, 2 * d:4 * d]], axis=1).astype(BF16)
            proj_b, proj_f = _project(h, w_in, inv_freq, 3 * d, rotary=False, seq_len=seq_len)
            o_fwd, o_bwd = _gla(proj_b, proj_f, hg_lower_bounds, layer, batch, seq_len)
            mixer_inputs, weights = (o_fwd, o_bwd, proj_b, hg_norm_w[j]), [w_out_odd[j].astype(BF16)]
        h1, info, counts = _out_project(layer % 2 == 0, mixer_inputs, weights, h, ln_w[layer, 0],
                                        ln_b[layer, 0], router_wt, alpha)
        h = _moe_sorted(h1, info, moe_w_gate, moe_w_up, moe_w_down, layer,
                        ln_w[layer, 1], ln_b[layer, 1], alpha)
    return h.reshape(batch, seq_len, d)
```

```python
import functools
import math

import jax
import jax.numpy as jnp
from jax import lax
from jax.experimental import pallas as pl
from jax.experimental.pallas import tpu as pltpu

F32 = jnp.float32
BF16 = jnp.bfloat16

RET_HEADS = 4
RET_DK = 64
RET_DV = 128
RET_CHUNK = 128
ROPE_BASE = 10000.0
LRU_BLOCKS = 4
LRU_BW = 128
LRU_CONV = 4
LRU_C = 8.0
HG_HEADS = 8
HG_DK = 128
HG_DV = 128
N_EXPERTS = 16
N_GROUPS = 4
EXPERTS_PER_GROUP = 4
LN_EPS = 1e-5
RMS_EPS = 1e-6

LANES = 128
SUBLANES = 8
PROJ_ROWS = 512
PROJ_COLS = 512
RET_CHUNKS_PER_STEP = 4
LRU_ROWS = 512
GLA_CHUNK = 32
GLA_ROWS = 256
TOKEN_ROWS = 512
SORT_ROWS = 256
MOE_ROWS = 512
VMEM_LIMIT = 56 * 1024 * 1024

_NT = (((1,), (1,)), ((), ()))
_TN = (((0,), (0,)), ((), ()))


def _params(sem):
    return pltpu.CompilerParams(dimension_semantics=sem, vmem_limit_bytes=VMEM_LIMIT)


def _sigmoid(x):
    return 0.5 * jnp.tanh(0.5 * x) + 0.5


def _silu(x):
    return x * _sigmoid(x)


def _softplus(x):
    return jnp.maximum(x, 0.0) + jnp.log1p(jnp.exp(-jnp.abs(x)))


def _layer_norm(u, w, b):
    mu = jnp.mean(u, axis=-1, keepdims=True)
    d = u - mu
    var = jnp.mean(d * d, axis=-1, keepdims=True)
    return d * lax.rsqrt(var + LN_EPS) * w + b


def _split3(x):
    hi = x.astype(BF16)
    rem = x - hi.astype(F32)
    mid = rem.astype(BF16)
    lo = (rem - mid.astype(F32)).astype(BF16)
    return hi, mid, lo


def _proj_kernel(x_ref, w_ref, inv_ref, ob_ref, of_ref, *, rotary, seq_len, tm, n_bf):
    i = pl.program_id(0)
    xb = x_ref[...].astype(BF16)
    tn = PROJ_COLS
    for j in range(w_ref.shape[1] // tn):
        acc = jnp.dot(xb, w_ref[:, j * tn:(j + 1) * tn], preferred_element_type=F32)
        if rotary and j == 0:
            row = lax.broadcasted_iota(jnp.int32, (tm, 1), 0) + i * tm
            ang = (row % seq_len).astype(F32) * inv_ref[...]
            cos = jnp.cos(ang)
            sin = jnp.sin(ang)
            scale = RET_DK ** -0.5
            q1, q2 = acc[:, 0:128], acc[:, 128:256]
            k1, k2 = acc[:, 256:384], acc[:, 384:512]
            ob_ref[:, 0:128] = ((q1 * cos - q2 * sin) * scale).astype(BF16)
            ob_ref[:, 128:256] = ((q1 * sin + q2 * cos) * scale).astype(BF16)
            ob_ref[:, 256:384] = (k1 * cos - k2 * sin).astype(BF16)
            ob_ref[:, 384:512] = (k1 * sin + k2 * cos).astype(BF16)
        elif (j + 1) * tn <= n_bf:
            ob_ref[:, j * tn:(j + 1) * tn] = acc.astype(BF16)
        else:
            of_ref[:, j * tn - n_bf:(j + 1) * tn - n_bf] = acc


def _gla_lower_bound(p, layer):
    e = jnp.exp(p - jnp.max(p, axis=0, keepdims=True))
    sm = e / jnp.sum(e, axis=0, keepdims=True)
    lb = jnp.zeros((1, p.shape[1]), F32)
    for r in range(1, layer + 1):
        lb = lb + sm[r:r + 1, :]
    return lb


def _proj_gla_kernel(x_ref, w_ref, lbp_ref, ob_ref, of_ref, *, layer):
    xb = x_ref[...].astype(BF16)
    d = w_ref.shape[1] // 5
    tn = PROJ_COLS
    lb = _gla_lower_bound(lbp_ref[...], layer)
    for j in range(5 * d // tn):
        col = j * tn
        acc = jnp.dot(xb, w_ref[:, col:col + tn], preferred_element_type=F32)
        if col < d:
            ob_ref[:, col:col + tn] = (_silu(acc) * (HG_DK ** -0.5)).astype(BF16)
        elif col < 3 * d:
            ob_ref[:, col:col + tn] = acc.astype(BF16)
        else:
            zc = col - 3 * d
            lbt = lb[:, zc % d:zc % d + tn]
            sig = _sigmoid(acc)
            of_ref[:, zc:zc + tn] = jnp.log2(lbt + (1.0 - lbt) * sig)
            ob_ref[:, col:col + tn] = ((1.0 - lbt) * (1.0 - sig)).astype(BF16)


def _project_gla(x, w_bf16, lower_bounds, layer, seq_len):
    t, k = x.shape
    n = w_bf16.shape[1]
    d = n // 5
    tm = min(PROJ_ROWS, seq_len)
    return pl.pallas_call(
        functools.partial(_proj_gla_kernel, layer=layer),
        grid=(t // tm,),
        in_specs=[
            pl.BlockSpec((tm, k), lambda i: (i, 0)),
            pl.BlockSpec((k, n), lambda i: (0, 0)),
            pl.BlockSpec(lower_bounds.shape, lambda i: (0, 0)),
        ],
        out_specs=[pl.BlockSpec((tm, n), lambda i: (i, 0)),
                   pl.BlockSpec((tm, 2 * d), lambda i: (i, 0))],
        out_shape=[jax.ShapeDtypeStruct((t, n), BF16),
                   jax.ShapeDtypeStruct((t, 2 * d), F32)],
        compiler_params=_params(("arbitrary",)),
        name="in_proj_gla",
    )(x, w_bf16, lower_bounds)


def _project(x, w_bf16, inv_freq, n_bf, *, rotary, seq_len):
    t, k = x.shape
    n = w_bf16.shape[1]
    tm = min(PROJ_ROWS, seq_len)
    return pl.pallas_call(
        functools.partial(_proj_kernel, rotary=rotary, seq_len=seq_len, tm=tm, n_bf=n_bf),
        grid=(t // tm,),
        in_specs=[
            pl.BlockSpec((tm, k), lambda i: (i, 0)),
            pl.BlockSpec((k, n), lambda i: (0, 0)),
            pl.BlockSpec((1, LANES), lambda i: (0, 0)),
        ],
        out_specs=[pl.BlockSpec((tm, n_bf), lambda i: (i, 0)),
                   pl.BlockSpec((tm, n - n_bf), lambda i: (i, 0))],
        out_shape=[jax.ShapeDtypeStruct((t, n_bf), BF16),
                   jax.ShapeDtypeStruct((t, n - n_bf), F32)],
        compiler_params=_params(("arbitrary",)),
        name="in_proj",
    )(x, w_bf16, inv_freq)


def _ret_log_gamma(head):
    out = jnp.full(head.shape, math.log1p(-(2.0 ** -5.0)), F32)
    for h in range(1, RET_HEADS):
        out = jnp.where(head == h, math.log1p(-(2.0 ** (-5.0 - h))), out)
    return out


def _ret_lane_head():
    lane = lax.broadcasted_iota(jnp.int32, (1, 2 * LANES), 1)
    return (lane % LANES) // (RET_DK // 2)


def _ret_state_mask():
    shape = (RET_HEADS * RET_DV, 2 * LANES)
    row_head = lax.broadcasted_iota(jnp.int32, shape, 0) // RET_DV
    col_head = (lax.broadcasted_iota(jnp.int32, shape, 1) % LANES) // (RET_DK // 2)
    return row_head == col_head


def _ret_bstate_kernel(k_ref, v_ref, sb_ref, s_ref, *, cps):
    c = RET_CHUNK

    @pl.when(pl.program_id(1) == 0)
    def _():
        s_ref[...] = jnp.zeros_like(s_ref)

    lg = _ret_log_gamma(_ret_lane_head())
    idx = lax.broadcasted_iota(jnp.int32, (c, 1), 0).astype(F32)
    k_decay = jnp.exp(lg * idx)
    chunk_decay = jnp.exp(lg * float(c))
    mask = _ret_state_mask()
    for cc in reversed(range(cps)):
        rows = slice(cc * c, (cc + 1) * c)
        sb_ref[0, cc] = s_ref[...].astype(BF16)
        kb = (k_ref[rows, :] * k_decay).astype(BF16)
        upd = lax.dot_general(v_ref[rows, :], kb, _TN, preferred_element_type=F32)
        s_ref[...] = s_ref[...] * chunk_decay + jnp.where(mask, upd, 0.0)


def _ret_out_kernel(q_ref, k_ref, v_ref, g_ref, sb_ref, o_ref, s_ref, *, cps):
    c = RET_CHUNK

    @pl.when(pl.program_id(1) == 0)
    def _():
        s_ref[...] = jnp.zeros_like(s_ref)

    lane_head = _ret_lane_head()
    lg = _ret_log_gamma(lane_head)
    idx = lax.broadcasted_iota(jnp.int32, (c, 1), 0).astype(F32)
    q_decay_f = jnp.exp(lg * (idx + 1.0))
    q_decay_b = jnp.exp(lg * (float(c) - idx))
    k_decay = jnp.exp(lg * (float(c) - 1.0 - idx))
    chunk_decay = jnp.exp(lg * float(c))
    mask = _ret_state_mask()
    ii = lax.broadcasted_iota(jnp.int32, (c, c), 0)
    jj = lax.broadcasted_iota(jnp.int32, (c, c), 1)
    dist = jnp.abs(ii - jj).astype(F32)
    intra_decay = [jnp.exp(math.log1p(-(2.0 ** (-5.0 - h))) * dist) for h in range(RET_HEADS)]

    for cc in range(cps):
        rows = slice(cc * c, (cc + 1) * c)
        q = q_ref[rows, :]
        k = k_ref[rows, :]
        v = v_ref[rows, :]
        qf = (q * q_decay_f).astype(BF16)
        qb = (q * q_decay_b).astype(BF16)
        cross = (lax.dot_general(qf, s_ref[...].astype(BF16), _NT, preferred_element_type=F32)
                 + lax.dot_general(qb, sb_ref[0, cc], _NT, preferred_element_type=F32))
        for h in range(RET_HEADS):
            qh = jnp.where(lane_head == h, q, jnp.zeros_like(q))
            s = lax.dot_general(qh, k, _NT, preferred_element_type=F32) * intra_decay[h]
            cols = slice(h * RET_DV, (h + 1) * RET_DV)
            o = jnp.dot(s.astype(BF16), v[:, cols], preferred_element_type=F32) + cross[:, cols]
            mu = jnp.mean(o, axis=-1, keepdims=True)
            d = o - mu
            var = jnp.mean(d * d, axis=-1, keepdims=True)
            gate = _silu(g_ref[rows, cols].astype(F32))
            o_ref[rows, cols] = (gate * (d * lax.rsqrt(var + LN_EPS))).astype(BF16)
        kf = (k * k_decay).astype(BF16)
        upd = lax.dot_general(v, kf, _TN, preferred_element_type=F32)
        s_ref[...] = s_ref[...] * chunk_decay + jnp.where(mask, upd, 0.0)


def _retention(proj, batch, seq_len):
    t = proj.shape[0]
    c = RET_CHUNK
    cps = min(RET_CHUNKS_PER_STEP, seq_len // c)
    rows = cps * c
    ns = seq_len // rows
    dv = RET_HEADS * RET_DV
    state_shape = (dv, 2 * LANES)
    rev = lambda b, n: b * ns + (ns - 1 - n)
    fwd = lambda b, n: b * ns + n
    sb = pl.pallas_call(
        functools.partial(_ret_bstate_kernel, cps=cps),
        grid=(batch, ns),
        in_specs=[
            pl.BlockSpec((rows, 2 * LANES), lambda b, n: (rev(b, n), 1)),
            pl.BlockSpec((rows, dv), lambda b, n: (rev(b, n), 1)),
        ],
        out_specs=pl.BlockSpec((1, cps) + state_shape, lambda b, n: (b, ns - 1 - n, 0, 0)),
        out_shape=jax.ShapeDtypeStruct((batch, ns * cps) + state_shape, BF16),
        scratch_shapes=[pltpu.VMEM(state_shape, F32)],
        compiler_params=_params(("arbitrary", "arbitrary")),
        name="ret_bstate",
    )(proj, proj)
    return pl.pallas_call(
        functools.partial(_ret_out_kernel, cps=cps),
        grid=(batch, ns),
        in_specs=[
            pl.BlockSpec((rows, 2 * LANES), lambda b, n: (fwd(b, n), 0)),
            pl.BlockSpec((rows, 2 * LANES), lambda b, n: (fwd(b, n), 1)),
            pl.BlockSpec((rows, dv), lambda b, n: (fwd(b, n), 1)),
            pl.BlockSpec((rows, dv), lambda b, n: (fwd(b, n), 2)),
            pl.BlockSpec((1, cps) + state_shape, lambda b, n: (b, n, 0, 0)),
        ],
        out_specs=pl.BlockSpec((rows, dv), lambda b, n: (fwd(b, n), 0)),
        out_shape=jax.ShapeDtypeStruct((t, dv), BF16),
        scratch_shapes=[pltpu.VMEM(state_shape, F32)],
        compiler_params=_params(("arbitrary", "arbitrary")),
        name="ret_out",
    )(proj, proj, proj, proj, sb)


def _lru_kernel(xfp_ref, xf_ref, xfn_ref, xbp_ref, xb_ref, xbn_ref, cw_ref, cb_ref, gw_ref, gb_ref,
                lam_ref, hf_ref, hb_ref, xx_ref, a_ref, b_ref, h_ref, *, nt, ts, batch):
    i = pl.program_id(0)
    halo = SUBLANES
    lo = LRU_CONV // 2

    @pl.when(i == 0)
    def _():
        h_ref[...] = jnp.zeros_like(h_ref)

    def prepare(xp_ref, x_ref, xn_ref, tile, z, slot):
        for b in range(batch):
            xx_ref[0:halo, :] = jnp.where(tile == 0, 0.0, xp_ref[b])
            xx_ref[halo:halo + ts, :] = x_ref[b]
            xx_ref[halo + ts:2 * halo + ts, :] = jnp.where(tile == nt - 1, 0.0, xn_ref[b])
            xc = cb_ref[...]
            for tap in range(LRU_CONV):
                xc = xc + cw_ref[tap:tap + 1, :] * xx_ref[pl.ds(halo - lo + tap, ts), :]
            for n in range(LRU_BLOCKS):
                cols = slice(n * LRU_BW, (n + 1) * LRU_BW)
                xn = xc[:, cols]
                g = (jnp.dot(xn.astype(BF16), gw_ref[z, n], preferred_element_type=F32)
                     + gb_ref[z, n:n + 1, :])
                r = _sigmoid(g[:, :LRU_BW])
                ig = _sigmoid(g[:, LRU_BW:])
                a = jnp.exp((-LRU_C) * r * _softplus(-lam_ref[z, :, cols]))
                a_ref[slot + b, :, cols] = a
                b_ref[slot + b, :, cols] = jnp.sqrt(1.0 - a * a) * (ig * xn)

    prepare(xfp_ref, xf_ref, xfn_ref, i, 0, 0)
    prepare(xbp_ref, xb_ref, xbn_ref, nt - 1 - i, 1, batch)

    def step(s, hs):
        out = []
        for k in range(2 * batch):
            row = s if k < batch else ts - 1 - s
            h = a_ref[k, pl.ds(row, 1), :] * hs[k] + b_ref[k, pl.ds(row, 1), :]
            if k < batch:
                hf_ref[k, pl.ds(row, 1), :] = h
            else:
                hb_ref[k - batch, pl.ds(row, 1), :] = h
            out.append(h)
        return tuple(out)

    hs = lax.fori_loop(0, ts, step, tuple(h_ref[k] for k in range(2 * batch)), unroll=8)
    for k in range(2 * batch):
        h_ref[k] = hs[k]


def _rglru(proj_f32, conv_w, conv_b, gate_w, gate_b, lam, batch, seq_len):
    w = LRU_BLOCKS * LRU_BW
    ts = min(LRU_ROWS, seq_len)
    nt = seq_len // ts
    rows8 = ts // SUBLANES
    last8 = seq_len // SUBLANES - 1
    x3 = proj_f32.reshape(batch, seq_len, proj_f32.shape[1])
    gw = jnp.concatenate([gate_w[:, 0], gate_w[:, 1]], axis=-1).astype(BF16)
    gb = jnp.concatenate([gate_b[:, 0], gate_b[:, 1]], axis=-1)
    bwd = lambda i: nt - 1 - i

    def tile_specs(tile):
        return [
            pl.BlockSpec((batch, SUBLANES, w), lambda i: (0, jnp.maximum(tile(i) * rows8 - 1, 0), 0)),
            pl.BlockSpec((batch, ts, w), lambda i: (0, tile(i), 0)),
            pl.BlockSpec((batch, SUBLANES, w), lambda i: (0, jnp.minimum((tile(i) + 1) * rows8, last8), 0)),
        ]

    full = lambda a: pl.BlockSpec(a.shape, lambda i: (0,) * a.ndim)
    cb = conv_b.reshape(1, w)
    lam3 = lam.reshape(2, 1, w)
    state = jax.ShapeDtypeStruct((batch, seq_len, w), F32)
    h_fwd, h_bwd = pl.pallas_call(
        functools.partial(_lru_kernel, nt=nt, ts=ts, batch=batch),
        grid=(nt,),
        in_specs=tile_specs(lambda i: i) + tile_specs(bwd) + [full(conv_w), full(cb), full(gw), full(gb),
                                                               full(lam3)],
        out_specs=[pl.BlockSpec((batch, ts, w), lambda i: (0, i, 0)),
                   pl.BlockSpec((batch, ts, w), lambda i: (0, bwd(i), 0))],
        out_shape=[state, state],
        scratch_shapes=[pltpu.VMEM((ts + 2 * SUBLANES, w), F32), pltpu.VMEM((2 * batch, ts, w), F32),
                        pltpu.VMEM((2 * batch, ts, w), F32), pltpu.VMEM((2 * batch, 1, w), F32)],
        compiler_params=_params(("arbitrary",)),
        name="lru_scan",
    )(x3, x3, x3, x3, x3, x3, conv_w, cb, gw, gb, lam3)
    return h_fwd.reshape(batch * seq_len, w), h_bwd.reshape(batch * seq_len, w)


def _gla_kernel(qf_ref, vf_ref, kf_ref, lf_ref, qb_ref, vb_ref, kb_ref, lb_ref, of_ref, ob_ref, s_ref,
                *, ts, batch):
    c = GLA_CHUNK
    nchunks = ts // c

    @pl.when(pl.program_id(0) == 0)
    def _():
        s_ref[...] = jnp.zeros_like(s_ref)

    head_cols = [slice(h * HG_DK, (h + 1) * HG_DK) for h in range(HG_HEADS)]
    hc = HG_HEADS * c
    ii = lax.broadcasted_iota(jnp.int32, (c, c), 0)
    jj = lax.broadcasted_iota(jnp.int32, (c, c), 1)
    si = lax.broadcasted_iota(jnp.int32, (hc, hc), 0)
    sj = lax.broadcasted_iota(jnp.int32, (hc, hc), 1)
    same_head = si // c == sj // c
    mid = c // 2

    def stack(a):
        return jnp.concatenate([a[:, cols] for cols in head_cols], axis=0)

    tri = {False: (jj <= ii).astype(BF16), True: (jj >= ii).astype(BF16)}
    tri2 = {r: jnp.concatenate([m, m], axis=1) for r, m in tri.items()}
    zero_block = jnp.zeros((c, HG_DK), BF16)
    zero_state = jnp.zeros((HG_DK, HG_DV), BF16)
    keep = {False: jnp.logical_and(same_head, sj <= si), True: jnp.logical_and(same_head, sj >= si)}
    chains = ([(qf_ref, vf_ref, kf_ref, lf_ref, of_ref, b, False) for b in range(batch)]
              + [(qb_ref, vb_ref, kb_ref, lb_ref, ob_ref, b, True) for b in range(batch)])

    def gates(cc):
        work = []
        for q_ref, v_ref, k_ref, l_ref, o_ref, b, reverse in chains:
            rows = pl.ds(((nchunks - 1 - cc) if reverse else cc) * c, c)
            log_f = l_ref[b, rows, :]
            f_hi = log_f.astype(BF16)
            f_lo = (log_f - f_hi.astype(F32)).astype(BF16)
            work.append(dict(rows=rows, reverse=reverse, o_ref=o_ref, b=b,
                             key=k_ref[b, rows, :].astype(F32), qs=q_ref[b, rows, :].astype(F32),
                             v=v_ref[b, rows, :], split=(f_hi, f_lo)))
        return work

    def cumulate(work):
        for w in work:
            f_hi, f_lo = w["split"]
            w["bcum"] = jnp.dot(tri2[w["reverse"]], jnp.concatenate([f_lo, f_hi], axis=0),
                                preferred_element_type=F32)

    half = c // 2
    row = lax.broadcasted_iota(jnp.int32, (c, 1), 0)
    first_half = row < half

    def decays(work):
        for w in work:
            bcum = w["bcum"]
            rev = w["reverse"]
            end = 0 if rev else c - 1
            b_end = bcum[end:end + 1, :]
            ref = jnp.where(first_half, bcum[half // 2:half // 2 + 1, :], bcum[half + half // 2:half + half // 2 + 1, :])
            qe = w["qs"] * jnp.exp2(bcum - ref)
            ke = w["key"] * jnp.exp2(ref - bcum)
            boundary = bcum[half:half + 1, :] if rev else bcum[half - 1:half, :]
            cross = jnp.exp2(-jnp.abs(bcum - boundary))
            queries = first_half if rev else jnp.logical_not(first_half)
            w["q3"] = [jnp.where(first_half, qe, 0.0).astype(BF16), jnp.where(first_half, 0.0, qe).astype(BF16),
                       jnp.where(queries, w["qs"] * cross, 0.0).astype(BF16)]
            w["k3"] = [jnp.where(first_half, ke, 0.0), jnp.where(first_half, 0.0, ke),
                       jnp.where(queries, 0.0, w["key"] * cross)]
            w["qd"] = (w["qs"] * jnp.exp2(bcum)).astype(BF16)
            w["kd"] = (w["key"] * jnp.exp2(b_end - bcum)).astype(BF16)
            w["decay"] = jnp.exp2(b_end)

    def stack3(parts):
        return jnp.concatenate([jnp.concatenate([p[:, cols] for p in parts], axis=1) for cols in head_cols], axis=0)

    def scores(work):
        for w in work:
            w["att"] = jnp.dot(stack3(w["q3"]), stack3(w["k3"]).T.astype(BF16), preferred_element_type=F32)

    def intra(work):
        for w in work:
            att = jnp.where(keep[w["reverse"]], w["att"], 0.0).astype(BF16)
            w["intra"] = jnp.dot(att, stack(w["v"]), preferred_element_type=F32)

    def inter(work):
        for k, w in enumerate(work):
            w["st"] = [s_ref[k, h] for h in range(HG_HEADS)]
            out = []
            for h in range(0, HG_HEADS, 2):
                sa = w["st"][h].T.astype(BF16)
                sb = w["st"][h + 1].T.astype(BF16)
                rhs = jnp.concatenate([jnp.concatenate([sa, zero_state], axis=1),
                                       jnp.concatenate([zero_state, sb], axis=1)], axis=0)
                pair = jnp.dot(w["qd"][:, h * HG_DK:(h + 2) * HG_DK], rhs, preferred_element_type=F32)
                out += [pair[:, :HG_DV], pair[:, HG_DV:]]
            w["inter"] = out

    def update(work):
        for w in work:
            upd = []
            for h in range(0, HG_HEADS, 2):
                ca, cb = head_cols[h], head_cols[h + 1]
                lhs = jnp.concatenate([w["v"][:, ca], w["v"][:, cb]], axis=0)
                rhs = jnp.concatenate([jnp.concatenate([w["kd"][:, ca], zero_block], axis=1),
                                       jnp.concatenate([zero_block, w["kd"][:, cb]], axis=1)], axis=0)
                pair = lax.dot_general(lhs, rhs, _TN, preferred_element_type=F32)
                upd += [pair[:, :HG_DK], pair[:, HG_DK:]]
            w["upd"] = upd

    def finish(work):
        for k, w in enumerate(work):
            for h, cols in enumerate(head_cols):
                w["o_ref"][w["b"], w["rows"], cols] = w["intra"][h * c:(h + 1) * c, :] + w["inter"][h]
                s_ref[k, h] = w["st"][h] * w["decay"][:, cols] + w["upd"][h]

    cur = gates(0)
    cumulate(cur)
    decays(cur)
    for cc in range(nchunks):
        more = cc + 1 < nchunks
        scores(cur)
        nxt = gates(cc + 1) if more else None
        if more:
            cumulate(nxt)
        intra(cur)
        inter(cur)
        update(cur)
        if more:
            decays(nxt)
        finish(cur)
        cur = nxt


def _gla(proj_bf16, proj_f32, batch, seq_len):
    d = HG_HEADS * HG_DK
    ts = min(GLA_ROWS, seq_len)
    nt = seq_len // ts
    pb = proj_bf16.reshape(batch, seq_len, proj_bf16.shape[1])
    pf = proj_f32.reshape(batch, seq_len, proj_f32.shape[1])
    bwd = lambda i: nt - 1 - i
    blk = lambda tile, col: pl.BlockSpec((batch, ts, d), lambda i: (0, tile(i), col))
    fwd = lambda i: i
    out = jax.ShapeDtypeStruct((batch, seq_len, d), F32)
    o_fwd, o_bwd = pl.pallas_call(
        functools.partial(_gla_kernel, ts=ts, batch=batch),
        grid=(nt,),
        in_specs=[blk(fwd, 0), blk(fwd, 1), blk(fwd, 3), blk(fwd, 0),
                  blk(bwd, 0), blk(bwd, 1), blk(bwd, 4), blk(bwd, 1)],
        out_specs=[blk(fwd, 0), blk(bwd, 0)],
        out_shape=[out, out],
        scratch_shapes=[pltpu.VMEM((2 * batch, HG_HEADS, HG_DV, HG_DK), F32)],
        compiler_params=_params(("arbitrary",)),
        name="gla_scan",
    )(pb, pb, pb, pf, pb, pb, pb, pf)
    return o_fwd.reshape(batch * seq_len, d), o_bwd.reshape(batch * seq_len, d)


def _top2(p):
    v1 = jnp.maximum(jnp.maximum(p[0], p[1]), jnp.maximum(p[2], p[3]))
    i1 = jnp.where(p[0] == v1, 0, jnp.where(p[1] == v1, 1, jnp.where(p[2] == v1, 2, 3)))
    q = [jnp.where(i1 == k, -1.0, p[k]) for k in range(4)]
    v2 = jnp.maximum(jnp.maximum(q[0], q[1]), jnp.maximum(q[2], q[3]))
    i2 = jnp.where(q[0] == v2, 0, jnp.where(q[1] == v2, 1, jnp.where(q[2] == v2, 2, 3)))
    return v1, i1, v2, i2


def _outproj_kernel(*refs, even, alpha, tm):
    if even:
        ret_ref, hf_ref, hb_ref, gr_ref, w0_ref, w1_ref = refs[:6]
        rest = refs[6:]
        lru = ((hf_ref[...] + hb_ref[...]) * jax.nn.gelu(gr_ref[...])).astype(BF16)
        y = (jnp.dot(ret_ref[...], w0_ref[...], preferred_element_type=F32)
             + jnp.dot(lru, w1_ref[...], preferred_element_type=F32))
    else:
        of_ref, ob_ref, g_ref, nw_ref, w0_ref = refs[:5]
        rest = refs[5:]
        o = of_ref[...] + ob_ref[...]
        ms = jnp.mean(o * o, axis=-1, keepdims=True)
        mix = o * lax.rsqrt(ms + RMS_EPS) * nw_ref[...] * _silu(g_ref[...].astype(F32))
        y = jnp.dot(mix.astype(BF16), w0_ref[...], preferred_element_type=F32)
    h_ref, lnw_ref, lnb_ref, rw_ref, o_ref, info_ref, cnt_ref, carry_ref = rest
    i = pl.program_id(0)

    @pl.when(i == 0)
    def _():
        carry_ref[...] = jnp.zeros_like(carry_ref)

    h1 = _layer_norm(alpha * h_ref[...] + y, lnw_ref[...], lnb_ref[...])
    o_ref[...] = h1

    h_hi = h1.astype(BF16)
    h_lo = (h1 - h_hi.astype(F32)).astype(BF16)
    rw = rw_ref[...]
    r_hi = rw.astype(BF16)
    r_lo = (rw - r_hi.astype(F32)).astype(BF16)
    logits = (lax.dot_general(r_lo, h_hi, _NT, preferred_element_type=F32)
              + lax.dot_general(r_hi, h_lo, _NT, preferred_element_type=F32)
              + lax.dot_general(r_hi, h_hi, _NT, preferred_element_type=F32))
    ex = jnp.exp(logits - jnp.max(logits, axis=0, keepdims=True))
    probs = ex / jnp.sum(ex, axis=0, keepdims=True)
    best = None
    for g in range(N_GROUPS):
        rows = [probs[g * EXPERTS_PER_GROUP + k:g * EXPERTS_PER_GROUP + k + 1, :]
                for k in range(EXPERTS_PER_GROUP)]
        v1, i1, v2, i2 = _top2(rows)
        cand = (v1 + v2, v1, i1 + g * EXPERTS_PER_GROUP, v2, i2 + g * EXPERTS_PER_GROUP)
        if best is None:
            best = cand
        else:
            take = cand[0] > best[0]
            best = tuple(jnp.where(take, cn, bs) for cn, bs in zip(cand, best))
    _, v1, e1, v2, e2 = best
    denom = v1 + v2
    g1 = v1 / denom
    g2 = v2 / denom

    eid = lax.broadcasted_iota(jnp.int32, (N_EXPERTS, tm), 0)
    oh1 = (eid == e1).astype(F32)
    oh2 = (eid == e2).astype(F32)
    oh = oh1 + oh2
    tt = lax.broadcasted_iota(jnp.int32, (tm, tm), 0)
    uu = lax.broadcasted_iota(jnp.int32, (tm, tm), 1)
    before = (tt < uu).astype(BF16)
    base = carry_ref[:, 0:1] + jnp.dot(oh.astype(BF16), before, preferred_element_type=F32)
    rank1 = jnp.sum(oh1 * base, axis=0, keepdims=True)
    rank2 = jnp.sum(oh2 * base, axis=0, keepdims=True)
    carry_ref[...] = carry_ref[...] + jnp.sum(oh, axis=1, keepdims=True)
    cnt_ref[...] = carry_ref[...]
    zero = jnp.zeros_like(g1)
    info_ref[...] = jnp.concatenate(
        [e1.astype(F32), e2.astype(F32), g1, g2, rank1, rank2, zero, zero], axis=0)


def _out_project(even, mixer_inputs, weights_bf16, h, ln_w, ln_b, router_wt, alpha):
    t, d = h.shape
    tm = TOKEN_ROWS
    row = lambda i: (i, 0)
    const = lambda i: (0, 0)
    if even:
        ret, h_fwd, h_bwd, proj_f32 = mixer_inputs
        w = h_fwd.shape[1]
        in_specs = [pl.BlockSpec((tm, ret.shape[1]), row), pl.BlockSpec((tm, w), row),
                    pl.BlockSpec((tm, w), row), pl.BlockSpec((tm, w), lambda i: (i, 1))]
        args = [ret, h_fwd, h_bwd, proj_f32]
    else:
        o_fwd, o_bwd, proj_bf16, norm_w = mixer_inputs
        in_specs = [pl.BlockSpec((tm, d), row), pl.BlockSpec((tm, d), row),
                    pl.BlockSpec((tm, d), lambda i: (i, 2)), pl.BlockSpec((1, d), const)]
        args = [o_fwd, o_bwd, proj_bf16, norm_w.reshape(1, d)]
    in_specs += [pl.BlockSpec(wm.shape, const) for wm in weights_bf16]
    in_specs += [pl.BlockSpec((tm, d), row), pl.BlockSpec((1, d), const), pl.BlockSpec((1, d), const),
                 pl.BlockSpec((N_EXPERTS, d), const)]
    args += list(weights_bf16) + [h, ln_w.reshape(1, d), ln_b.reshape(1, d), router_wt]
    return pl.pallas_call(
        functools.partial(_outproj_kernel, even=even, alpha=alpha, tm=tm),
        grid=(t // tm,),
        in_specs=in_specs,
        out_specs=[pl.BlockSpec((tm, d), row),
                   pl.BlockSpec((SUBLANES, tm), lambda i: (0, i)),
                   pl.BlockSpec((N_EXPERTS, LANES), const)],
        out_shape=[jax.ShapeDtypeStruct((t, d), F32),
                   jax.ShapeDtypeStruct((SUBLANES, t), F32),
                   jax.ShapeDtypeStruct((N_EXPERTS, LANES), F32)],
        scratch_shapes=[pltpu.VMEM((N_EXPERTS, LANES), F32)],
        compiler_params=_params(("arbitrary",)),
        name="out_proj_router",
    )(*args)


def _row_copy(src_ref, src_row, dst_ref, dst_row, sem):
    return pltpu.make_async_copy(src_ref.at[pl.ds(src_row, 1)], dst_ref.at[pl.ds(dst_row, 1)], sem)


def _dispatch_kernel(d1_ref, d2_ref, zs_ref, h_ref, xb_ref, zero_ref, sem, *, tm):
    i = pl.program_id(0)
    base = i * tm

    @pl.when(i == 0)
    def _():
        zero_ref[...] = jnp.zeros_like(zero_ref)

        def clear(row):
            start = pl.multiple_of(row, MOE_ROWS)
            return pltpu.make_async_copy(zero_ref, xb_ref.at[pl.ds(start, MOE_ROWS)], sem.at[0])

        for e in range(N_EXPERTS):
            clear(zs_ref[e]).start()
        for e in range(N_EXPERTS):
            clear(zs_ref[e]).wait()

        def clear_tail(b, carry):
            clear(b * MOE_ROWS).start()
            clear(b * MOE_ROWS).wait()
            return carry

        lax.fori_loop(zs_ref[N_EXPERTS], xb_ref.shape[0] // MOE_ROWS, clear_tail, 0)

    def start(t, carry):
        _row_copy(h_ref, t, xb_ref, d1_ref[base + t], sem.at[0]).start()
        _row_copy(h_ref, t, xb_ref, d2_ref[base + t], sem.at[1]).start(priority=1)
        return carry

    lax.fori_loop(0, tm, start, 0, unroll=8)
    pltpu.make_async_copy(h_ref, xb_ref.at[pl.ds(0, tm)], sem.at[0]).wait()
    pltpu.make_async_copy(h_ref, xb_ref.at[pl.ds(0, tm)], sem.at[1]).wait()


def _dispatch(h, dest1, dest2, zero_start, n_rows):
    t, d = h.shape
    tm = TOKEN_ROWS
    return pl.pallas_call(
        functools.partial(_dispatch_kernel, tm=tm),
        grid_spec=pltpu.PrefetchScalarGridSpec(
            num_scalar_prefetch=3,
            grid=(t // tm,),
            in_specs=[pl.BlockSpec((tm, d), lambda i, d1, d2, zs: (i, 0))],
            out_specs=pl.BlockSpec(memory_space=pl.ANY),
            scratch_shapes=[pltpu.VMEM((MOE_ROWS, d), F32), pltpu.SemaphoreType.DMA((2,))],
        ),
        out_shape=jax.ShapeDtypeStruct((n_rows, d), F32),
        compiler_params=_params(("arbitrary",)),
        name="moe_dispatch",
    )(dest1, dest2, zero_start, h)


def _expert_kernel(be_ref, nv_ref, x_ref, wg_ref, wu_ref, wd_ref, o_ref, wgb_ref, wub_ref, wdb_ref):
    i = pl.program_id(0)
    changed = jnp.logical_or(i == 0, be_ref[i] != be_ref[jnp.maximum(i - 1, 0)])

    @pl.when(changed)
    def _():
        wgb_ref[...] = wg_ref[...].astype(BF16)
        wub_ref[...] = wu_ref[...].astype(BF16)
        wdb_ref[...] = wd_ref[...].astype(BF16)

    @pl.when(i < nv_ref[0])
    def _():
        x = x_ref[...].astype(BF16)
        gate = jnp.dot(x, wgb_ref[...], preferred_element_type=F32)
        up = jnp.dot(x, wub_ref[...], preferred_element_type=F32)
        hid = (_silu(gate) * up).astype(BF16)
        o_ref[...] = jnp.dot(hid, wdb_ref[...], preferred_element_type=F32)

    @pl.when(i >= nv_ref[0])
    def _():
        o_ref[...] = jnp.zeros_like(o_ref)


def _experts(xb, block_e, n_valid, w_gate, w_up, w_down, layer):
    p, d = xb.shape
    de = w_gate.shape[3]
    nb = p // MOE_ROWS
    return pl.pallas_call(
        _expert_kernel,
        grid_spec=pltpu.PrefetchScalarGridSpec(
            num_scalar_prefetch=2,
            grid=(nb,),
            in_specs=[
                pl.BlockSpec((MOE_ROWS, d), lambda i, be, nv: (jnp.maximum(jnp.minimum(i, nv[0] - 1), 0), 0)),
                pl.BlockSpec((None, None, d, de), lambda i, be, nv: (layer, be[i], 0, 0)),
                pl.BlockSpec((None, None, d, de), lambda i, be, nv: (layer, be[i], 0, 0)),
                pl.BlockSpec((None, None, de, d), lambda i, be, nv: (layer, be[i], 0, 0)),
            ],
            out_specs=pl.BlockSpec((MOE_ROWS, d), lambda i, be, nv: (i, 0)),
            scratch_shapes=[pltpu.VMEM((d, de), BF16), pltpu.VMEM((d, de), BF16),
                            pltpu.VMEM((de, d), BF16)],
        ),
        out_shape=jax.ShapeDtypeStruct((p, d), F32),
        compiler_params=_params(("arbitrary",)),
        name="moe_experts",
    )(block_e, n_valid, xb, w_gate, w_up, w_down)


def _combine_kernel(d1_ref, d2_ref, h_ref, gates_ref, lnw_ref, lnb_ref, yb_ref, o_ref, r_ref, sem,
                    *, tm, alpha):
    base = pl.program_id(0) * tm

    def start(t, carry):
        _row_copy(yb_ref, d1_ref[base + t], r_ref.at[0], t, sem.at[0]).start()
        _row_copy(yb_ref, d2_ref[base + t], r_ref.at[1], t, sem.at[1]).start(priority=1)
        return carry

    lax.fori_loop(0, tm, start, 0, unroll=8)
    pltpu.make_async_copy(yb_ref.at[pl.ds(0, tm)], r_ref.at[0], sem.at[0]).wait()
    pltpu.make_async_copy(yb_ref.at[pl.ds(0, tm)], r_ref.at[1], sem.at[1]).wait()
    y = gates_ref[:, 0:1] * r_ref[0] + gates_ref[:, 1:2] * r_ref[1]
    o_ref[...] = _layer_norm(alpha * h_ref[...] + y, lnw_ref[...], lnb_ref[...])


def _combine(h, yb, dest1, dest2, gates, ln_w, ln_b, alpha):
    t, d = h.shape
    tm = TOKEN_ROWS
    return pl.pallas_call(
        functools.partial(_combine_kernel, tm=tm, alpha=alpha),
        grid_spec=pltpu.PrefetchScalarGridSpec(
            num_scalar_prefetch=2,
            grid=(t // tm,),
            in_specs=[pl.BlockSpec((tm, d), lambda i, d1, d2: (i, 0)),
                      pl.BlockSpec((tm, 2), lambda i, d1, d2: (i, 0)),
                      pl.BlockSpec((1, d), lambda i, d1, d2: (0, 0)),
                      pl.BlockSpec((1, d), lambda i, d1, d2: (0, 0)),
                      pl.BlockSpec(memory_space=pl.ANY)],
            out_specs=pl.BlockSpec((tm, d), lambda i, d1, d2: (i, 0)),
            scratch_shapes=[pltpu.VMEM((2, tm, d), F32), pltpu.SemaphoreType.DMA((2,))],
        ),
        out_shape=jax.ShapeDtypeStruct((t, d), F32),
        compiler_params=_params(("arbitrary",)),
        name="moe_combine",
    )(dest1, dest2, h, gates, ln_w.reshape(1, d), ln_b.reshape(1, d), yb)


def _invert_kernel(d1_ref, d2_ref, init_ref, ids_ref, sem, *, t):
    fill = pltpu.make_async_copy(init_ref, ids_ref, sem)
    fill.start()
    fill.wait()

    def body(tok, carry):
        ids_ref[d1_ref[tok]] = tok
        ids_ref[d2_ref[tok]] = tok + t
        return carry

    lax.fori_loop(0, t, body, 0, unroll=8)


def _invert(dest1, dest2, n_rows):
    t = dest1.shape[0]
    smem = pl.BlockSpec(memory_space=pltpu.SMEM)
    return pl.pallas_call(
        functools.partial(_invert_kernel, t=t),
        in_specs=[smem, smem, pl.BlockSpec(memory_space=pl.ANY)],
        out_specs=smem,
        out_shape=jax.ShapeDtypeStruct((n_rows,), jnp.int32),
        scratch_shapes=[pltpu.SemaphoreType.DMA(())],
        name="moe_invert",
    )(dest1, dest2, jnp.full((n_rows,), -1, jnp.int32))


def _expert_fused_kernel(be_ref, src_ref, dst_ref, h_ref, wg_ref, wu_ref, wd_ref, yk_ref,
                         xbuf, ybuf, wgb_ref, wub_ref, wdb_ref, sem_g, sem_s):
    i = pl.program_id(0)
    nb = pl.num_programs(0)
    slot = i % 2
    other = 1 - slot

    def gather(block, buf):
        base = block * MOE_ROWS
        for r in range(MOE_ROWS):
            _row_copy(h_ref, src_ref[base + r], xbuf.at[buf], r, sem_g.at[buf]).start()

    def scatter(block, buf):
        base = block * MOE_ROWS
        for r in range(MOE_ROWS):
            _row_copy(ybuf.at[buf], r, yk_ref, dst_ref[base + r], sem_s.at[buf]).start(priority=1)

    def wait_gather(buf):
        pltpu.make_async_copy(h_ref.at[pl.ds(0, MOE_ROWS)], xbuf.at[buf], sem_g.at[buf]).wait()

    def wait_scatter(buf):
        pltpu.make_async_copy(ybuf.at[buf], yk_ref.at[pl.ds(0, MOE_ROWS)], sem_s.at[buf]).wait()

    @pl.when(i == 0)
    def _():
        ybuf[...] = jnp.zeros_like(ybuf)
        gather(0, 0)

    wait_gather(slot)

    @pl.when(i >= 1)
    def _():
        wait_scatter(slot)

    changed = jnp.logical_or(i == 0, be_ref[i] != be_ref[jnp.maximum(i - 1, 0)])

    @pl.when(changed)
    def _():
        wgb_ref[...] = wg_ref[...].astype(BF16)
        wub_ref[...] = wu_ref[...].astype(BF16)
        wdb_ref[...] = wd_ref[...].astype(BF16)

    gather(jnp.minimum(i + 1, nb - 1), other)
    scatter(jnp.maximum(i - 1, 0), other)
    x = xbuf[slot].astype(BF16)
    gate = jnp.dot(x, wgb_ref[...], preferred_element_type=F32)
    up = jnp.dot(x, wub_ref[...], preferred_element_type=F32)
    hid = (_silu(gate) * up).astype(BF16)
    ybuf[slot] = jnp.dot(hid, wdb_ref[...], preferred_element_type=F32)

    @pl.when(i == nb - 1)
    def _():
        scatter(i, slot)
        wait_gather(other)
        wait_scatter(other)
        wait_scatter(slot)


def _experts_fused(h, src_tok, dst_row, block_e, w_gate, w_up, w_down, layer):
    t, d = h.shape
    de = w_gate.shape[3]
    p = src_tok.shape[0]
    nb = p // MOE_ROWS
    weight = lambda shape: pl.BlockSpec((None, None) + shape, lambda i, be, src, dst: (layer, be[i], 0, 0))
    return pl.pallas_call(
        _expert_fused_kernel,
        grid_spec=pltpu.PrefetchScalarGridSpec(
            num_scalar_prefetch=3,
            grid=(nb,),
            in_specs=[pl.BlockSpec(memory_space=pl.ANY), weight((d, de)), weight((d, de)), weight((de, d))],
            out_specs=pl.BlockSpec(memory_space=pl.ANY),
            scratch_shapes=[pltpu.VMEM((2, MOE_ROWS, d), F32), pltpu.VMEM((2, MOE_ROWS, d), F32),
                            pltpu.VMEM((d, de), BF16), pltpu.VMEM((d, de), BF16), pltpu.VMEM((de, d), BF16),
                            pltpu.SemaphoreType.DMA((2,)), pltpu.SemaphoreType.DMA((2,))],
        ),
        out_shape=jax.ShapeDtypeStruct((p, d), F32),
        compiler_params=_params(("arbitrary",)),
        name="moe_experts",
    )(block_e, src_tok, dst_row, h, w_gate, w_up, w_down)


def _combine_dense_kernel(h_ref, y0_ref, y1_ref, gates_ref, lnw_ref, lnb_ref, o_ref, *, alpha):
    y = gates_ref[:, 0:1] * y0_ref[...] + gates_ref[:, 1:2] * y1_ref[...]
    o_ref[...] = _layer_norm(alpha * h_ref[...] + y, lnw_ref[...], lnb_ref[...])


def _combine_dense(h, yk, gates, ln_w, ln_b, alpha):
    t, d = h.shape
    tm = TOKEN_ROWS
    nt = t // tm
    row = lambda i: (i, 0)
    const = lambda i: (0, 0)
    return pl.pallas_call(
        functools.partial(_combine_dense_kernel, alpha=alpha),
        grid=(nt,),
        in_specs=[pl.BlockSpec((tm, d), row), pl.BlockSpec((tm, d), row),
                  pl.BlockSpec((tm, d), lambda i: (nt + i, 0)), pl.BlockSpec((tm, 2), row),
                  pl.BlockSpec((1, d), const), pl.BlockSpec((1, d), const)],
        out_specs=pl.BlockSpec((tm, d), row),
        out_shape=jax.ShapeDtypeStruct((t, d), F32),
        compiler_params=_params(("arbitrary",)),
        name="moe_combine",
    )(h, yk, yk, gates, ln_w.reshape(1, d), ln_b.reshape(1, d))


def _moe_fused(h1, info, counts, w_gate, w_up, w_down, layer, ln_w, ln_b, alpha):
    t = h1.shape[0]
    n_rows = 2 * t + N_EXPERTS * MOE_ROWS
    nb = n_rows // MOE_ROWS
    cnt = counts[:, 0].astype(jnp.int32)
    padded = (cnt + MOE_ROWS - 1) // MOE_ROWS * MOE_ROWS
    ends = jnp.cumsum(padded)
    pstart = ends - padded
    e1 = info[0].astype(jnp.int32)
    e2 = info[1].astype(jnp.int32)
    expert_ids = jnp.arange(N_EXPERTS, dtype=jnp.int32)[:, None]
    dest1 = jnp.sum(jnp.where(e1[None, :] == expert_ids, pstart[:, None], 0), axis=0) + info[4].astype(jnp.int32)
    dest2 = jnp.sum(jnp.where(e2[None, :] == expert_ids, pstart[:, None], 0), axis=0) + info[5].astype(jnp.int32)
    gates = jnp.stack([info[2], info[3]], axis=1)
    blk = jnp.arange(nb, dtype=jnp.int32) * MOE_ROWS
    block_e = jnp.minimum(jnp.sum((ends[None, :] <= blk[:, None]).astype(jnp.int32), axis=1), N_EXPERTS - 1)
    ids = _invert(dest1, dest2, n_rows)
    is_pad = ids < 0
    src_tok = jnp.where(is_pad, 0, ids % t)
    dst_row = jnp.where(is_pad, 2 * t - 1 + jnp.cumsum(is_pad.astype(jnp.int32)), ids)
    yk = _experts_fused(h1, src_tok, dst_row, block_e, w_gate, w_up, w_down, layer)
    return _combine_dense(h1, yk, gates, ln_w, ln_b, alpha)


def _pack_halves(x):
    n = x.shape[1] // 2
    hi = pltpu.bitcast(x[:, :n], jnp.uint32)
    lo = pltpu.bitcast(x[:, n:], jnp.uint32)
    return hi | (lo >> 16)


def _unpack_halves(p):
    hi = pltpu.bitcast(p & jnp.uint32(0xFFFF0000), F32)
    lo = pltpu.bitcast(p << 16, F32)
    return jnp.concatenate([hi.astype(BF16), lo.astype(BF16)], axis=1)


def _seg_expert_kernel(first_ref, count_ref, tail_ref, x_ref, wg_ref, wu_ref, wd_ref, y_ref,
                       xbuf, ybuf, wgb_ref, wub_ref, wdb_ref, semx, semy):
    e = pl.program_id(0)
    wgb_ref[...] = wg_ref[...].astype(BF16)
    wub_ref[...] = wu_ref[...].astype(BF16)
    wdb_ref[...] = wd_ref[...].astype(BF16)
    first = first_ref[e]
    n_used = tail_ref[0]

    def rows(g):
        return pl.ds(pl.multiple_of(g * MOE_ROWS, MOE_ROWS), MOE_ROWS)

    def fetch(g):
        return pltpu.make_async_copy(x_ref.at[rows(g)], xbuf.at[g % 2], semx.at[g % 2])

    def put(g):
        return pltpu.make_async_copy(ybuf.at[g % 2], y_ref.at[rows(g)], semy.at[g % 2])

    @pl.when(e == 0)
    def _():
        fetch(0).start()

    def body(g, carry):
        @pl.when(g + 1 < n_used)
        def _():
            fetch(g + 1).start()

        fetch(g).wait()

        @pl.when(g >= 2)
        def _():
            put(g - 2).wait()

        x = _unpack_halves(xbuf[g % 2])
        gate = jnp.dot(x, wgb_ref[...], preferred_element_type=F32)
        up = jnp.dot(x, wub_ref[...], preferred_element_type=F32)
        hid = (_silu(gate) * up).astype(BF16)
        y = jnp.dot(hid, wdb_ref[...], preferred_element_type=F32)
        ybuf[g % 2] = _pack_halves(y.astype(BF16).astype(F32))
        put(g).start()
        return carry

    lax.fori_loop(first, first + count_ref[e], body, 0)

    @pl.when(e == pl.num_programs(0) - 1)
    def _():
        @pl.when(n_used >= 2)
        def _():
            put(n_used - 2).wait()

        put(n_used - 1).wait()
        ybuf[0] = jnp.zeros_like(ybuf[0])

        def clear(b, carry):
            cp = pltpu.make_async_copy(
                ybuf.at[0], y_ref.at[pl.ds(pl.multiple_of(b * MOE_ROWS, MOE_ROWS), MOE_ROWS)], semy.at[0])
            cp.start()
            cp.wait()
            return carry

        lax.fori_loop(tail_ref[0], y_ref.shape[0] // MOE_ROWS, clear, 0)


def _seg_experts(xb, seg_first, seg_count, n_valid, w_gate, w_up, w_down, layer):
    p, half = xb.shape
    d = 2 * half
    de = w_gate.shape[3]
    weight = lambda shape: pl.BlockSpec((None, None) + shape, lambda e, a, b, c: (layer, e, 0, 0))
    return pl.pallas_call(
        _seg_expert_kernel,
        grid_spec=pltpu.PrefetchScalarGridSpec(
            num_scalar_prefetch=3,
            grid=(N_EXPERTS,),
            in_specs=[pl.BlockSpec(memory_space=pl.ANY), weight((d, de)), weight((d, de)), weight((de, d))],
            out_specs=pl.BlockSpec(memory_space=pl.ANY),
            scratch_shapes=[pltpu.VMEM((2, MOE_ROWS, half), jnp.uint32),
                            pltpu.VMEM((2, MOE_ROWS, half), jnp.uint32),
                            pltpu.VMEM((d, de), BF16), pltpu.VMEM((d, de), BF16), pltpu.VMEM((de, d), BF16),
                            pltpu.SemaphoreType.DMA((2,)), pltpu.SemaphoreType.DMA((2,))],
        ),
        out_shape=jax.ShapeDtypeStruct((p, half), jnp.uint32),
        compiler_params=_params(("arbitrary",)),
        name="moe_experts",
    )(seg_first, seg_count, n_valid, xb, w_gate, w_up, w_down)


def _clear_padding_blocks(zs_ref, xb_ref, zero_ref, sem):
    zero_ref[...] = jnp.zeros_like(zero_ref)

    def clear(row):
        start = pl.multiple_of(row, MOE_ROWS)
        return pltpu.make_async_copy(zero_ref, xb_ref.at[pl.ds(start, MOE_ROWS)], sem)

    for e in range(N_EXPERTS):
        clear(zs_ref[e]).start()
    for e in range(N_EXPERTS):
        clear(zs_ref[e]).wait()

    def clear_tail(b, carry):
        clear(b * MOE_ROWS).start()
        clear(b * MOE_ROWS).wait()
        return carry

    lax.fori_loop(zs_ref[N_EXPERTS], xb_ref.shape[0] // MOE_ROWS, clear_tail, 0)


def _run_copy(src_ref, src_row, dst_ref, dst_row, sem):
    src = pl.multiple_of(src_row, SUBLANES)
    dst = pl.multiple_of(dst_row, SUBLANES)
    return pltpu.make_async_copy(src_ref.at[pl.ds(src, SUBLANES)], dst_ref.at[pl.ds(dst, SUBLANES)], sem)


def _sort_dispatch_kernel(dst_ref, nch_ref, zs_ref, h_ref, lp_ref, xb_ref, xs_ref, zero_ref, sem, *, tm, nck):
    i = pl.program_id(0)

    @pl.when(i == 0)
    def _():
        _clear_padding_blocks(zs_ref, xb_ref, zero_ref, sem.at[0])

    slot = i % 2
    last = pl.num_programs(0) - 1

    def wait_runs(buf, count):
        def wait(j, carry):
            _run_copy(xs_ref.at[buf], 0, xb_ref, 0, sem.at[buf]).wait()
            return carry

        lax.fori_loop(0, count, wait, 0)

    @pl.when(i >= 2)
    def _():
        wait_runs(slot, nch_ref[jnp.maximum(i - 2, 0)])

    lp = lp_ref[...]
    pos = lax.broadcasted_iota(jnp.int32, (xs_ref.shape[1], tm), 0)
    perm = jnp.logical_or(pos == lp[0:1, :], pos == lp[1:2, :]).astype(BF16)
    xs_ref[slot] = _pack_halves(jnp.dot(perm, h_ref[...].astype(BF16), preferred_element_type=F32))

    def start(j, carry):
        _run_copy(xs_ref.at[slot], j * SUBLANES, xb_ref, dst_ref[i * nck + j], sem.at[slot]).start()
        return carry

    lax.fori_loop(0, nch_ref[i], start, 0)

    @pl.when(i == last)
    def _():
        @pl.when(i >= 1)
        def _():
            wait_runs(1 - slot, nch_ref[jnp.maximum(i - 1, 0)])

        wait_runs(slot, nch_ref[i])


def _sort_dispatch(h, lp_rows, chunk_dst, n_chunks, zero_start, n_rows, tm, sorted_rows):
    t, d = h.shape
    nck = sorted_rows // SUBLANES
    return pl.pallas_call(
        functools.partial(_sort_dispatch_kernel, tm=tm, nck=nck),
        grid_spec=pltpu.PrefetchScalarGridSpec(
            num_scalar_prefetch=3,
            grid=(t // tm,),
            in_specs=[pl.BlockSpec((tm, d), lambda i, a, b, c: (i, 0)),
                      pl.BlockSpec((2, tm), lambda i, a, b, c: (0, i))],
            out_specs=pl.BlockSpec(memory_space=pl.ANY),
            scratch_shapes=[pltpu.VMEM((2, sorted_rows, d // 2), jnp.uint32),
                            pltpu.VMEM((MOE_ROWS, d // 2), jnp.uint32), pltpu.SemaphoreType.DMA((2,))],
        ),
        out_shape=jax.ShapeDtypeStruct((n_rows, d // 2), jnp.uint32),
        compiler_params=_params(("arbitrary",)),
        name="moe_dispatch",
    )(chunk_dst, n_chunks, zero_start, h, lp_rows)


def _sort_combine_kernel(src_ref, nch_ref, h_ref, lp_ref, gates_ref, lnw_ref, lnb_ref, yb_ref, o_ref,
                         ys_ref, sem, *, tm, nck, alpha):
    i = pl.program_id(0)

    slot = i % 2

    def fetch(tile, buf):
        def start(j, carry):
            _run_copy(yb_ref, src_ref[tile * nck + j], ys_ref.at[buf], j * SUBLANES, sem.at[buf]).start()
            return carry

        lax.fori_loop(0, nch_ref[tile], start, 0)

    @pl.when(i == 0)
    def _():
        ys_ref[...] = jnp.zeros_like(ys_ref)
        fetch(0, 0)

    @pl.when(i + 1 < pl.num_programs(0))
    def _():
        fetch(i + 1, 1 - slot)

    def wait(j, carry):
        _run_copy(yb_ref, 0, ys_ref.at[slot], 0, sem.at[slot]).wait()
        return carry

    lax.fori_loop(0, nch_ref[i], wait, 0)

    lp = lp_ref[...]
    pos = lax.broadcasted_iota(jnp.int32, (tm, ys_ref.shape[1]), 1)
    ys = _unpack_halves(ys_ref[slot])
    y1 = jnp.dot((pos == lp[:, 0:1]).astype(BF16), ys, preferred_element_type=F32)
    y2 = jnp.dot((pos == lp[:, 1:2]).astype(BF16), ys, preferred_element_type=F32)
    y = gates_ref[:, 0:1] * y1 + gates_ref[:, 1:2] * y2
    o_ref[...] = _layer_norm(alpha * h_ref[...] + y, lnw_ref[...], lnb_ref[...])


def _sort_combine(h, yb, lp_cols, gates, chunk_src, n_chunks, ln_w, ln_b, alpha, tm, sorted_rows):
    t, d = h.shape
    nck = sorted_rows // SUBLANES
    row = lambda i, a, b: (i, 0)
    const = lambda i, a, b: (0, 0)
    return pl.pallas_call(
        functools.partial(_sort_combine_kernel, tm=tm, nck=nck, alpha=alpha),
        grid_spec=pltpu.PrefetchScalarGridSpec(
            num_scalar_prefetch=2,
            grid=(t // tm,),
            in_specs=[pl.BlockSpec((tm, d), row), pl.BlockSpec((tm, 2), row), pl.BlockSpec((tm, 2), row),
                      pl.BlockSpec((1, d), const), pl.BlockSpec((1, d), const),
                      pl.BlockSpec(memory_space=pl.ANY)],
            out_specs=pl.BlockSpec((tm, d), row),
            scratch_shapes=[pltpu.VMEM((2, sorted_rows, d // 2), jnp.uint32), pltpu.SemaphoreType.DMA((2,))],
        ),
        out_shape=jax.ShapeDtypeStruct((t, d), F32),
        compiler_params=_params(("arbitrary",)),
        name="moe_combine",
    )(chunk_src, n_chunks, h, lp_cols, gates, ln_w.reshape(1, d), ln_b.reshape(1, d), yb)


def _moe_sorted(h1, info, w_gate, w_up, w_down, layer, ln_w, ln_b, alpha):
    t = h1.shape[0]
    tm = SORT_ROWS
    ntile = t // tm
    run_pad = SUBLANES - 1
    sorted_rows = 2 * tm + LANES
    nck = sorted_rows // SUBLANES
    n_rows = (2 * t + ntile * N_EXPERTS * run_pad + MOE_ROWS - 1) // MOE_ROWS * MOE_ROWS + N_EXPERTS * MOE_ROWS
    nb = n_rows // MOE_ROWS
    i32 = jnp.int32
    e1, e2 = info[0].astype(i32), info[1].astype(i32)
    rank1, rank2 = info[4].astype(i32), info[5].astype(i32)
    expert_ids = jnp.arange(N_EXPERTS, dtype=i32)[None, :]
    oh1 = e1[:, None] == expert_ids
    oh2 = e2[:, None] == expert_ids
    cnt = jnp.logical_or(oh1, oh2).astype(i32).reshape(ntile, tm, N_EXPERTS).sum(axis=1)
    cnt8 = (cnt + run_pad) // SUBLANES * SUBLANES
    local = jnp.cumsum(cnt8, axis=1) - cnt8
    before = jnp.cumsum(cnt, axis=0) - cnt
    seg = cnt8.sum(axis=0)
    seg_pad = (seg + MOE_ROWS - 1) // MOE_ROWS * MOE_ROWS
    ends = jnp.cumsum(seg_pad)
    slot = (ends - seg_pad)[None, :] + jnp.cumsum(cnt8, axis=0) - cnt8
    shift = jnp.repeat(local - before, tm, axis=0)
    lp1 = jnp.sum(jnp.where(oh1, shift, 0), axis=1) + rank1
    lp2 = jnp.sum(jnp.where(oh2, shift, 0), axis=1) + rank2
    chunk_row = jnp.arange(nck, dtype=i32) * SUBLANES
    run_of = jnp.sum(((local + cnt8)[:, None, :] <= chunk_row[None, :, None]).astype(i32), axis=2)
    run_of = jnp.minimum(run_of, N_EXPERTS - 1)
    chunk_slot = jnp.sum(jnp.where(run_of[:, :, None] == expert_ids[None], (slot - local)[:, None, :], 0),
                         axis=2) + chunk_row[None, :]
    n_chunks = cnt8.sum(axis=1) // SUBLANES
    n_valid = (ends[-1] // MOE_ROWS).astype(i32).reshape(1)
    zero_start = jnp.concatenate([jnp.maximum(ends - MOE_ROWS, 0), n_valid]).astype(i32)
    gates = jnp.stack([info[2], info[3]], axis=1)
    chunk_slot = chunk_slot.reshape(-1).astype(i32)
    seg_first = ((ends - seg_pad) // MOE_ROWS).astype(i32)
    seg_count = (seg_pad // MOE_ROWS).astype(i32)

    xb = _sort_dispatch(h1, jnp.stack([lp1, lp2], axis=0), chunk_slot, n_chunks, zero_start, n_rows, tm,
                        sorted_rows)
    yb = _seg_experts(xb, seg_first, seg_count, n_valid, w_gate, w_up, w_down, layer)
    return _sort_combine(h1, yb, jnp.stack([lp1, lp2], axis=1), gates, chunk_slot, n_chunks, ln_w, ln_b,
                         alpha, tm, sorted_rows)


def _moe(h1, info, counts, w_gate, w_up, w_down, layer, ln_w, ln_b, alpha):
    t = h1.shape[0]
    n_rows = 2 * t + N_EXPERTS * MOE_ROWS
    nb = n_rows // MOE_ROWS
    cnt = counts[:, 0].astype(jnp.int32)
    padded = (cnt + MOE_ROWS - 1) // MOE_ROWS * MOE_ROWS
    ends = jnp.cumsum(padded)
    pstart = ends - padded
    e1 = info[0].astype(jnp.int32)
    e2 = info[1].astype(jnp.int32)
    expert_ids = jnp.arange(N_EXPERTS, dtype=jnp.int32)[:, None]
    dest1 = jnp.sum(jnp.where(e1[None, :] == expert_ids, pstart[:, None], 0), axis=0) + info[4].astype(jnp.int32)
    dest2 = jnp.sum(jnp.where(e2[None, :] == expert_ids, pstart[:, None], 0), axis=0) + info[5].astype(jnp.int32)
    gates = jnp.stack([info[2], info[3]], axis=1)
    n_valid = (ends[-1] // MOE_ROWS).astype(jnp.int32).reshape(1)
    blk = jnp.minimum(jnp.arange(nb, dtype=jnp.int32), n_valid[0] - 1) * MOE_ROWS
    block_e = jnp.minimum(jnp.sum((ends[None, :] <= blk[:, None]).astype(jnp.int32), axis=1), N_EXPERTS - 1)
    zero_start = jnp.concatenate([jnp.maximum(ends - MOE_ROWS, 0), n_valid]).astype(jnp.int32)

    xb = _dispatch(h1, dest1, dest2, zero_start, n_rows)
    yb = _experts(xb, block_e, n_valid, w_gate, w_up, w_down, layer)
    return _combine(h1, yb, dest1, dest2, gates, ln_w, ln_b, alpha)


def _rotary_column_order(w_in):
    d = w_in.shape[0]
    nq = RET_HEADS * RET_DK

    def perm(w):
        return w.reshape(d, RET_HEADS, RET_DK // 2, 2).transpose(0, 3, 1, 2).reshape(d, nq)

    return jnp.concatenate([perm(w_in[:, :nq]), perm(w_in[:, nq:2 * nq]), w_in[:, 2 * nq:]], axis=1)


def kernel(x, w_in_even, w_out_even, lru_conv_w, lru_conv_b, lru_gate_w, lru_gate_b, lru_lambda,
           w_in_odd, w_out_odd, hg_lower_bounds, hg_norm_w, ln_w, ln_b, router_w,
           moe_w_gate, moe_w_up, moe_w_down):
    batch, seq_len, d = x.shape
    depth = ln_w.shape[0]
    alpha = (2.0 * depth) ** 0.25
    t = batch * seq_len
    h = x.reshape(t, d)
    router_wt = router_w.T
    half = RET_DK // 2
    inv_freq = ROPE_BASE ** (-jnp.arange(0, RET_DK, 2, dtype=F32) / RET_DK)
    inv_freq = jnp.tile(inv_freq, LANES // half).reshape(1, LANES)
    nret = RET_HEADS * RET_DV

    for layer in range(depth):
        j = layer // 2
        if layer % 2 == 0:
            w_in = _rotary_column_order(w_in_even[j]).astype(BF16)
            n_bf = 2 * RET_HEADS * RET_DK + 2 * nret
            proj_b, proj_f = _project(h, w_in, inv_freq, n_bf, rotary=True, seq_len=seq_len)
            ret = _retention(proj_b, batch, seq_len)
            h_fwd, h_bwd = _rglru(proj_f, lru_conv_w[j], lru_conv_b[j], lru_gate_w[j], lru_gate_b[j],
                                  lru_lambda[j], batch, seq_len)
            w_out = w_out_even[j].astype(BF16)
            mixer_inputs, weights = (ret, h_fwd, h_bwd, proj_f), [w_out[:nret], w_out[nret:]]
        else:
            w = w_in_odd[j]
            w_in = jnp.concatenate([w[:, :2 * d], w[:, 4 * d:], w[:, 2 * d:4 * d]], axis=1).astype(BF16)
            proj_b, proj_f = _project_gla(h, w_in, hg_lower_bounds, layer, seq_len)
            o_fwd, o_bwd = _gla(proj_b, proj_f, batch, seq_len)
            mixer_inputs, weights = (o_fwd, o_bwd, proj_b, hg_norm_w[j]), [w_out_odd[j].astype(BF16)]
        h1, info, counts = _out_project(layer % 2 == 0, mixer_inputs, weights, h, ln_w[layer, 0],
                                        ln_b[layer, 0], router_wt, alpha)
        h = _moe_sorted(h1, info, moe_w_gate, moe_w_up, moe_w_down, layer,
                        ln_w[layer, 1], ln_b[layer, 1], alpha)
    return h.reshape(batch, seq_len, d)
```

```python
import functools
import math

import jax
import jax.numpy as jnp
from jax import lax
from jax.experimental import pallas as pl
from jax.experimental.pallas import tpu as pltpu

F32 = jnp.float32
BF16 = jnp.bfloat16

RET_HEADS = 4
RET_DK = 64
RET_DV = 128
RET_CHUNK = 128
ROPE_BASE = 10000.0
LRU_BLOCKS = 4
LRU_BW = 128
LRU_CONV = 4
LRU_C = 8.0
HG_HEADS = 8
HG_DK = 128
HG_DV = 128
N_EXPERTS = 16
N_GROUPS = 4
EXPERTS_PER_GROUP = 4
LN_EPS = 1e-5
RMS_EPS = 1e-6

LANES = 128
SUBLANES = 8
PROJ_ROWS = 512
PROJ_COLS = 512
RET_CHUNKS_PER_STEP = 4
LRU_ROWS = 512
GLA_CHUNK = 32
GLA_ROWS = 256
TOKEN_ROWS = 512
SORT_ROWS = 256
MOE_ROWS = 512
VMEM_LIMIT = 56 * 1024 * 1024

_NT = (((1,), (1,)), ((), ()))
_TN = (((0,), (0,)), ((), ()))


def _params(sem):
    return pltpu.CompilerParams(dimension_semantics=sem, vmem_limit_bytes=VMEM_LIMIT)


def _sigmoid(x):
    return 0.5 * jnp.tanh(0.5 * x) + 0.5


def _silu(x):
    return x * _sigmoid(x)


def _softplus(x):
    return jnp.maximum(x, 0.0) + jnp.log1p(jnp.exp(-jnp.abs(x)))


def _layer_norm(u, w, b):
    mu = jnp.mean(u, axis=-1, keepdims=True)
    d = u - mu
    var = jnp.mean(d * d, axis=-1, keepdims=True)
    return d * lax.rsqrt(var + LN_EPS) * w + b


def _proj_ret_kernel(x_ref, w_ref, inv_ref, ob_ref, of_ref, *, seq_len, tm, n_bf):
    i = pl.program_id(0)
    xb = x_ref[...].astype(BF16)
    tn = PROJ_COLS
    for j in range(w_ref.shape[1] // tn):
        acc = jnp.dot(xb, w_ref[:, j * tn:(j + 1) * tn], preferred_element_type=F32)
        if j == 0:
            row = lax.broadcasted_iota(jnp.int32, (tm, 1), 0) + i * tm
            ang = (row % seq_len).astype(F32) * inv_ref[...]
            cos = jnp.cos(ang)
            sin = jnp.sin(ang)
            scale = RET_DK ** -0.5
            q1, q2 = acc[:, 0:128], acc[:, 128:256]
            k1, k2 = acc[:, 256:384], acc[:, 384:512]
            ob_ref[:, 0:128] = ((q1 * cos - q2 * sin) * scale).astype(BF16)
            ob_ref[:, 128:256] = ((q1 * sin + q2 * cos) * scale).astype(BF16)
            ob_ref[:, 256:384] = (k1 * cos - k2 * sin).astype(BF16)
            ob_ref[:, 384:512] = (k1 * sin + k2 * cos).astype(BF16)
        elif (j + 1) * tn <= n_bf:
            ob_ref[:, j * tn:(j + 1) * tn] = acc.astype(BF16)
        else:
            of_ref[:, j * tn - n_bf:(j + 1) * tn - n_bf] = acc


def _gla_lower_bound(p, layer):
    e = jnp.exp(p - jnp.max(p, axis=0, keepdims=True))
    sm = e / jnp.sum(e, axis=0, keepdims=True)
    lb = jnp.zeros((1, p.shape[1]), F32)
    for r in range(1, layer + 1):
        lb = lb + sm[r:r + 1, :]
    return lb


def _proj_gla_kernel(x_ref, w_ref, lbp_ref, ob_ref, of_ref, *, layer):
    xb = x_ref[...].astype(BF16)
    d = w_ref.shape[1] // 5
    tn = PROJ_COLS
    lb = _gla_lower_bound(lbp_ref[...], layer)
    mean = 0.5 * (1.0 + lb)
    spread = 0.5 * (1.0 - lb)
    q_scale = 0.5 * HG_DK ** -0.5
    for j in range(5 * d // tn):
        col = j * tn
        acc = jnp.dot(xb, w_ref[:, col:col + tn], preferred_element_type=F32)
        if col < d:
            ob_ref[:, col:col + tn] = (acc * (q_scale * jnp.tanh(0.5 * acc) + q_scale)).astype(BF16)
        elif col < 3 * d:
            ob_ref[:, col:col + tn] = acc.astype(BF16)
        else:
            zc = col - 3 * d
            swing = spread[:, zc % d:zc % d + tn] * jnp.tanh(0.5 * acc)
            of_ref[:, zc:zc + tn] = jnp.log2(mean[:, zc % d:zc % d + tn] + swing)
            ob_ref[:, col:col + tn] = (spread[:, zc % d:zc % d + tn] - swing).astype(BF16)


def _project(kernel_fn, name, x, w_bf16, side, n_bf, n_f32, seq_len):
    t, k = x.shape
    n = w_bf16.shape[1]
    tm = min(PROJ_ROWS, seq_len)
    return pl.pallas_call(
        kernel_fn,
        grid=(t // tm,),
        in_specs=[
            pl.BlockSpec((tm, k), lambda i: (i, 0)),
            pl.BlockSpec((k, n), lambda i: (0, 0)),
            pl.BlockSpec(side.shape, lambda i: (0, 0)),
        ],
        out_specs=[pl.BlockSpec((tm, n_bf), lambda i: (i, 0)),
                   pl.BlockSpec((tm, n_f32), lambda i: (i, 0))],
        out_shape=[jax.ShapeDtypeStruct((t, n_bf), BF16),
                   jax.ShapeDtypeStruct((t, n_f32), F32)],
        compiler_params=_params(("arbitrary",)),
        name=name,
    )(x, w_bf16, side)


def _ret_log_gamma(head):
    out = jnp.full(head.shape, math.log1p(-(2.0 ** -5.0)), F32)
    for h in range(1, RET_HEADS):
        out = jnp.where(head == h, math.log1p(-(2.0 ** (-5.0 - h))), out)
    return out


def _ret_lane_head():
    lane = lax.broadcasted_iota(jnp.int32, (1, 2 * LANES), 1)
    return (lane % LANES) // (RET_DK // 2)


def _ret_state_mask():
    shape = (RET_HEADS * RET_DV, 2 * LANES)
    row_head = lax.broadcasted_iota(jnp.int32, shape, 0) // RET_DV
    col_head = (lax.broadcasted_iota(jnp.int32, shape, 1) % LANES) // (RET_DK // 2)
    return row_head == col_head


def _ret_bstate_kernel(k_ref, v_ref, sb_ref, s_ref, *, cps):
    c = RET_CHUNK

    @pl.when(pl.program_id(1) == 0)
    def _():
        s_ref[...] = jnp.zeros_like(s_ref)

    lg = _ret_log_gamma(_ret_lane_head())
    idx = lax.broadcasted_iota(jnp.int32, (c, 1), 0).astype(F32)
    k_decay = jnp.exp(lg * idx)
    chunk_decay = jnp.exp(lg * float(c))
    mask = _ret_state_mask()
    for cc in reversed(range(cps)):
        rows = slice(cc * c, (cc + 1) * c)
        sb_ref[0, cc] = s_ref[...].astype(BF16)
        kb = (k_ref[rows, :] * k_decay).astype(BF16)
        upd = lax.dot_general(v_ref[rows, :], kb, _TN, preferred_element_type=F32)
        s_ref[...] = s_ref[...] * chunk_decay + jnp.where(mask, upd, 0.0)


def _ret_out_kernel(q_ref, k_ref, v_ref, g_ref, sb_ref, o_ref, s_ref, *, cps):
    c = RET_CHUNK

    @pl.when(pl.program_id(1) == 0)
    def _():
        s_ref[...] = jnp.zeros_like(s_ref)

    lane_head = _ret_lane_head()
    lg = _ret_log_gamma(lane_head)
    idx = lax.broadcasted_iota(jnp.int32, (c, 1), 0).astype(F32)
    q_decay_f = jnp.exp(lg * (idx + 1.0))
    q_decay_b = jnp.exp(lg * (float(c) - idx))
    k_decay = jnp.exp(lg * (float(c) - 1.0 - idx))
    chunk_decay = jnp.exp(lg * float(c))
    mask = _ret_state_mask()
    ii = lax.broadcasted_iota(jnp.int32, (c, c), 0)
    jj = lax.broadcasted_iota(jnp.int32, (c, c), 1)
    dist = jnp.abs(ii - jj).astype(F32)
    intra_decay = [jnp.exp(math.log1p(-(2.0 ** (-5.0 - h))) * dist) for h in range(RET_HEADS)]

    for cc in range(cps):
        rows = slice(cc * c, (cc + 1) * c)
        q = q_ref[rows, :]
        k = k_ref[rows, :]
        v = v_ref[rows, :]
        qf = (q * q_decay_f).astype(BF16)
        qb = (q * q_decay_b).astype(BF16)
        cross = (lax.dot_general(qf, s_ref[...].astype(BF16), _NT, preferred_element_type=F32)
                 + lax.dot_general(qb, sb_ref[0, cc], _NT, preferred_element_type=F32))
        for h in range(RET_HEADS):
            qh = jnp.where(lane_head == h, q, jnp.zeros_like(q))
            s = lax.dot_general(qh, k, _NT, preferred_element_type=F32) * intra_decay[h]
            cols = slice(h * RET_DV, (h + 1) * RET_DV)
            o = jnp.dot(s.astype(BF16), v[:, cols], preferred_element_type=F32) + cross[:, cols]
            mu = jnp.mean(o, axis=-1, keepdims=True)
            d = o - mu
            var = jnp.mean(d * d, axis=-1, keepdims=True)
            gate = _silu(g_ref[rows, cols].astype(F32))
            o_ref[rows, cols] = (gate * (d * lax.rsqrt(var + LN_EPS))).astype(BF16)
        kf = (k * k_decay).astype(BF16)
        upd = lax.dot_general(v, kf, _TN, preferred_element_type=F32)
        s_ref[...] = s_ref[...] * chunk_decay + jnp.where(mask, upd, 0.0)


def _retention(proj, batch, seq_len):
    t = proj.shape[0]
    c = RET_CHUNK
    cps = min(RET_CHUNKS_PER_STEP, seq_len // c)
    rows = cps * c
    ns = seq_len // rows
    dv = RET_HEADS * RET_DV
    state_shape = (dv, 2 * LANES)
    rev = lambda b, n: b * ns + (ns - 1 - n)
    fwd = lambda b, n: b * ns + n
    sb = pl.pallas_call(
        functools.partial(_ret_bstate_kernel, cps=cps),
        grid=(batch, ns),
        in_specs=[
            pl.BlockSpec((rows, 2 * LANES), lambda b, n: (rev(b, n), 1)),
            pl.BlockSpec((rows, dv), lambda b, n: (rev(b, n), 1)),
        ],
        out_specs=pl.BlockSpec((1, cps) + state_shape, lambda b, n: (b, ns - 1 - n, 0, 0)),
        out_shape=jax.ShapeDtypeStruct((batch, ns * cps) + state_shape, BF16),
        scratch_shapes=[pltpu.VMEM(state_shape, F32)],
        compiler_params=_params(("arbitrary", "arbitrary")),
        name="ret_bstate",
    )(proj, proj)
    return pl.pallas_call(
        functools.partial(_ret_out_kernel, cps=cps),
        grid=(batch, ns),
        in_specs=[
            pl.BlockSpec((rows, 2 * LANES), lambda b, n: (fwd(b, n), 0)),
            pl.BlockSpec((rows, 2 * LANES), lambda b, n: (fwd(b, n), 1)),
            pl.BlockSpec((rows, dv), lambda b, n: (fwd(b, n), 1)),
            pl.BlockSpec((rows, dv), lambda b, n: (fwd(b, n), 2)),
            pl.BlockSpec((1, cps) + state_shape, lambda b, n: (b, n, 0, 0)),
        ],
        out_specs=pl.BlockSpec((rows, dv), lambda b, n: (fwd(b, n), 0)),
        out_shape=jax.ShapeDtypeStruct((t, dv), BF16),
        scratch_shapes=[pltpu.VMEM(state_shape, F32)],
        compiler_params=_params(("arbitrary", "arbitrary")),
        name="ret_out",
    )(proj, proj, proj, proj, sb)


def _lru_kernel(xfp_ref, xf_ref, xfn_ref, xbp_ref, xb_ref, xbn_ref, cw_ref, cb_ref, gw_ref, gb_ref,
                lam_ref, hf_ref, hb_ref, xx_ref, a_ref, b_ref, h_ref, *, nt, ts, batch):
    i = pl.program_id(0)
    halo = SUBLANES
    lo = LRU_CONV // 2

    @pl.when(i == 0)
    def _():
        h_ref[...] = jnp.zeros_like(h_ref)

    def prepare(xp_ref, x_ref, xn_ref, tile, z, slot):
        for b in range(batch):
            xx_ref[0:halo, :] = jnp.where(tile == 0, 0.0, xp_ref[b])
            xx_ref[halo:halo + ts, :] = x_ref[b]
            xx_ref[halo + ts:2 * halo + ts, :] = jnp.where(tile == nt - 1, 0.0, xn_ref[b])
            xc = cb_ref[...]
            for tap in range(LRU_CONV):
                xc = xc + cw_ref[tap:tap + 1, :] * xx_ref[pl.ds(halo - lo + tap, ts), :]
            for n in range(LRU_BLOCKS):
                cols = slice(n * LRU_BW, (n + 1) * LRU_BW)
                xn = xc[:, cols]
                g = (jnp.dot(xn.astype(BF16), gw_ref[z, n], preferred_element_type=F32)
                     + gb_ref[z, n:n + 1, :])
                r = _sigmoid(g[:, :LRU_BW])
                ig = _sigmoid(g[:, LRU_BW:])
                a = jnp.exp((-LRU_C) * r * _softplus(-lam_ref[z, :, cols]))
                a_ref[slot + b, :, cols] = a
                b_ref[slot + b, :, cols] = jnp.sqrt(1.0 - a * a) * (ig * xn)

    prepare(xfp_ref, xf_ref, xfn_ref, i, 0, 0)
    prepare(xbp_ref, xb_ref, xbn_ref, nt - 1 - i, 1, batch)

    def step(s, hs):
        out = []
        for k in range(2 * batch):
            row = s if k < batch else ts - 1 - s
            h = a_ref[k, pl.ds(row, 1), :] * hs[k] + b_ref[k, pl.ds(row, 1), :]
            if k < batch:
                hf_ref[k, pl.ds(row, 1), :] = h
            else:
                hb_ref[k - batch, pl.ds(row, 1), :] = h
            out.append(h)
        return tuple(out)

    hs = lax.fori_loop(0, ts, step, tuple(h_ref[k] for k in range(2 * batch)), unroll=8)
    for k in range(2 * batch):
        h_ref[k] = hs[k]


def _rglru(proj_f32, conv_w, conv_b, gate_w, gate_b, lam, batch, seq_len):
    w = LRU_BLOCKS * LRU_BW
    ts = min(LRU_ROWS, seq_len)
    nt = seq_len // ts
    rows8 = ts // SUBLANES
    last8 = seq_len // SUBLANES - 1
    x3 = proj_f32.reshape(batch, seq_len, proj_f32.shape[1])
    gw = jnp.concatenate([gate_w[:, 0], gate_w[:, 1]], axis=-1).astype(BF16)
    gb = jnp.concatenate([gate_b[:, 0], gate_b[:, 1]], axis=-1)
    bwd = lambda i: nt - 1 - i

    def tile_specs(tile):
        return [
            pl.BlockSpec((batch, SUBLANES, w), lambda i: (0, jnp.maximum(tile(i) * rows8 - 1, 0), 0)),
            pl.BlockSpec((batch, ts, w), lambda i: (0, tile(i), 0)),
            pl.BlockSpec((batch, SUBLANES, w), lambda i: (0, jnp.minimum((tile(i) + 1) * rows8, last8), 0)),
        ]

    full = lambda a: pl.BlockSpec(a.shape, lambda i: (0,) * a.ndim)
    cb = conv_b.reshape(1, w)
    lam3 = lam.reshape(2, 1, w)
    state = jax.ShapeDtypeStruct((batch, seq_len, w), F32)
    h_fwd, h_bwd = pl.pallas_call(
        functools.partial(_lru_kernel, nt=nt, ts=ts, batch=batch),
        grid=(nt,),
        in_specs=tile_specs(lambda i: i) + tile_specs(bwd) + [full(conv_w), full(cb), full(gw), full(gb),
                                                               full(lam3)],
        out_specs=[pl.BlockSpec((batch, ts, w), lambda i: (0, i, 0)),
                   pl.BlockSpec((batch, ts, w), lambda i: (0, bwd(i), 0))],
        out_shape=[state, state],
        scratch_shapes=[pltpu.VMEM((ts + 2 * SUBLANES, w), F32), pltpu.VMEM((2 * batch, ts, w), F32),
                        pltpu.VMEM((2 * batch, ts, w), F32), pltpu.VMEM((2 * batch, 1, w), F32)],
        compiler_params=_params(("arbitrary",)),
        name="lru_scan",
    )(x3, x3, x3, x3, x3, x3, conv_w, cb, gw, gb, lam3)
    return h_fwd.reshape(batch * seq_len, w), h_bwd.reshape(batch * seq_len, w)


def _gla_kernel(qf_ref, vf_ref, kf_ref, lf_ref, qb_ref, vb_ref, kb_ref, lb_ref, of_ref, ob_ref, s_ref,
                *, ts, batch):
    c = GLA_CHUNK
    nchunks = ts // c

    @pl.when(pl.program_id(0) == 0)
    def _():
        s_ref[...] = jnp.zeros_like(s_ref)

    head_cols = [slice(h * HG_DK, (h + 1) * HG_DK) for h in range(HG_HEADS)]
    hc = HG_HEADS * c
    ii = lax.broadcasted_iota(jnp.int32, (c, c), 0)
    jj = lax.broadcasted_iota(jnp.int32, (c, c), 1)
    si = lax.broadcasted_iota(jnp.int32, (hc, hc), 0)
    sj = lax.broadcasted_iota(jnp.int32, (hc, hc), 1)
    same_head = si // c == sj // c

    def stack(a):
        return jnp.concatenate([a[:, cols] for cols in head_cols], axis=0)

    tri = {False: (jj <= ii).astype(BF16), True: (jj >= ii).astype(BF16)}
    tri2 = {r: jnp.concatenate([m, m], axis=1) for r, m in tri.items()}
    zero_block = jnp.zeros((c, HG_DK), BF16)
    zero_state = jnp.zeros((HG_DK, HG_DV), BF16)
    keep = {False: jnp.logical_and(same_head, sj <= si), True: jnp.logical_and(same_head, sj >= si)}
    chains = ([(qf_ref, vf_ref, kf_ref, lf_ref, of_ref, b, False) for b in range(batch)]
              + [(qb_ref, vb_ref, kb_ref, lb_ref, ob_ref, b, True) for b in range(batch)])

    def gates(cc):
        work = []
        for q_ref, v_ref, k_ref, l_ref, o_ref, b, reverse in chains:
            rows = pl.ds(((nchunks - 1 - cc) if reverse else cc) * c, c)
            log_f = l_ref[b, rows, :]
            f_hi = log_f.astype(BF16)
            f_lo = (log_f - f_hi.astype(F32)).astype(BF16)
            work.append(dict(rows=rows, reverse=reverse, o_ref=o_ref, b=b,
                             key=k_ref[b, rows, :].astype(F32), qs=q_ref[b, rows, :].astype(F32),
                             v=v_ref[b, rows, :], split=(f_hi, f_lo)))
        return work

    def cumulate(work):
        for w in work:
            f_hi, f_lo = w["split"]
            w["bcum"] = jnp.dot(tri2[w["reverse"]], jnp.concatenate([f_lo, f_hi], axis=0),
                                preferred_element_type=F32)

    half = c // 2
    row = lax.broadcasted_iota(jnp.int32, (c, 1), 0)
    first_half = row < half

    def decays(work):
        for w in work:
            bcum = w["bcum"]
            rev = w["reverse"]
            end = 0 if rev else c - 1
            b_end = bcum[end:end + 1, :]
            ref = jnp.where(first_half, bcum[half // 2:half // 2 + 1, :], bcum[half + half // 2:half + half // 2 + 1, :])
            qe = w["qs"] * jnp.exp2(bcum - ref)
            ke = w["key"] * jnp.exp2(ref - bcum)
            boundary = bcum[half:half + 1, :] if rev else bcum[half - 1:half, :]
            cross = jnp.exp2(-jnp.abs(bcum - boundary))
            queries = first_half if rev else jnp.logical_not(first_half)
            w["q3"] = [jnp.where(first_half, qe, 0.0).astype(BF16), jnp.where(first_half, 0.0, qe).astype(BF16),
                       jnp.where(queries, w["qs"] * cross, 0.0).astype(BF16)]
            w["k3"] = [jnp.where(first_half, ke, 0.0), jnp.where(first_half, 0.0, ke),
                       jnp.where(queries, 0.0, w["key"] * cross)]
            w["qd"] = (w["qs"] * jnp.exp2(bcum)).astype(BF16)
            w["kd"] = (w["key"] * jnp.exp2(b_end - bcum)).astype(BF16)
            w["decay"] = jnp.exp2(b_end)

    def stack3(parts):
        return jnp.concatenate([jnp.concatenate([p[:, cols] for p in parts], axis=1) for cols in head_cols], axis=0)

    def scores(work):
        for w in work:
            w["att"] = jnp.dot(stack3(w["q3"]), stack3(w["k3"]).T.astype(BF16), preferred_element_type=F32)

    def intra(work):
        for w in work:
            att = jnp.where(keep[w["reverse"]], w["att"], 0.0).astype(BF16)
            w["intra"] = jnp.dot(att, stack(w["v"]), preferred_element_type=F32)

    def inter(work):
        for k, w in enumerate(work):
            w["st"] = [s_ref[k, h] for h in range(HG_HEADS)]
            out = []
            for h in range(0, HG_HEADS, 2):
                sa = w["st"][h].T.astype(BF16)
                sb = w["st"][h + 1].T.astype(BF16)
                rhs = jnp.concatenate([jnp.concatenate([sa, zero_state], axis=1),
                                       jnp.concatenate([zero_state, sb], axis=1)], axis=0)
                pair = jnp.dot(w["qd"][:, h * HG_DK:(h + 2) * HG_DK], rhs, preferred_element_type=F32)
                out += [pair[:, :HG_DV], pair[:, HG_DV:]]
            w["inter"] = out

    def update(work):
        for w in work:
            upd = []
            for h in range(0, HG_HEADS, 2):
                ca, cb = head_cols[h], head_cols[h + 1]
                lhs = jnp.concatenate([w["v"][:, ca], w["v"][:, cb]], axis=0)
                rhs = jnp.concatenate([jnp.concatenate([w["kd"][:, ca], zero_block], axis=1),
                                       jnp.concatenate([zero_block, w["kd"][:, cb]], axis=1)], axis=0)
                pair = lax.dot_general(lhs, rhs, _TN, preferred_element_type=F32)
                upd += [pair[:, :HG_DK], pair[:, HG_DK:]]
            w["upd"] = upd

    def finish(work):
        for k, w in enumerate(work):
            for h, cols in enumerate(head_cols):
                w["o_ref"][w["b"], w["rows"], cols] = w["intra"][h * c:(h + 1) * c, :] + w["inter"][h]
                s_ref[k, h] = w["st"][h] * w["decay"][:, cols] + w["upd"][h]

    cur = gates(0)
    cumulate(cur)
    decays(cur)
    for cc in range(nchunks):
        more = cc + 1 < nchunks
        scores(cur)
        nxt = gates(cc + 1) if more else None
        if more:
            cumulate(nxt)
        intra(cur)
        inter(cur)
        update(cur)
        if more:
            decays(nxt)
        finish(cur)
        cur = nxt


def _gla(proj_bf16, proj_f32, batch, seq_len):
    d = HG_HEADS * HG_DK
    ts = min(GLA_ROWS, seq_len)
    nt = seq_len // ts
    pb = proj_bf16.reshape(batch, seq_len, proj_bf16.shape[1])
    pf = proj_f32.reshape(batch, seq_len, proj_f32.shape[1])
    bwd = lambda i: nt - 1 - i
    blk = lambda tile, col: pl.BlockSpec((batch, ts, d), lambda i: (0, tile(i), col))
    fwd = lambda i: i
    out = jax.ShapeDtypeStruct((batch, seq_len, d), F32)
    o_fwd, o_bwd = pl.pallas_call(
        functools.partial(_gla_kernel, ts=ts, batch=batch),
        grid=(nt,),
        in_specs=[blk(fwd, 0), blk(fwd, 1), blk(fwd, 3), blk(fwd, 0),
                  blk(bwd, 0), blk(bwd, 1), blk(bwd, 4), blk(bwd, 1)],
        out_specs=[blk(fwd, 0), blk(bwd, 0)],
        out_shape=[out, out],
        scratch_shapes=[pltpu.VMEM((2 * batch, HG_HEADS, HG_DV, HG_DK), F32)],
        compiler_params=_params(("arbitrary",)),
        name="gla_scan",
    )(pb, pb, pb, pf, pb, pb, pb, pf)
    return o_fwd.reshape(batch * seq_len, d), o_bwd.reshape(batch * seq_len, d)


def _top2(p):
    v1 = jnp.maximum(jnp.maximum(p[0], p[1]), jnp.maximum(p[2], p[3]))
    i1 = jnp.where(p[0] == v1, 0, jnp.where(p[1] == v1, 1, jnp.where(p[2] == v1, 2, 3)))
    q = [jnp.where(i1 == k, -1.0, p[k]) for k in range(4)]
    v2 = jnp.maximum(jnp.maximum(q[0], q[1]), jnp.maximum(q[2], q[3]))
    i2 = jnp.where(q[0] == v2, 0, jnp.where(q[1] == v2, 1, jnp.where(q[2] == v2, 2, 3)))
    return v1, i1, v2, i2


def _outproj_kernel(*refs, even, alpha, tm):
    if even:
        ret_ref, hf_ref, hb_ref, gr_ref, w0_ref, w1_ref = refs[:6]
        rest = refs[6:]
        lru = ((hf_ref[...] + hb_ref[...]) * jax.nn.gelu(gr_ref[...])).astype(BF16)
        y = (jnp.dot(ret_ref[...], w0_ref[...], preferred_element_type=F32)
             + jnp.dot(lru, w1_ref[...], preferred_element_type=F32))
    else:
        of_ref, ob_ref, g_ref, nw_ref, w0_ref = refs[:5]
        rest = refs[5:]
        o = of_ref[...] + ob_ref[...]
        ms = jnp.mean(o * o, axis=-1, keepdims=True)
        mix = o * lax.rsqrt(ms + RMS_EPS) * nw_ref[...] * _silu(g_ref[...].astype(F32))
        y = jnp.dot(mix.astype(BF16), w0_ref[...], preferred_element_type=F32)
    h_ref, lnw_ref, lnb_ref, rw_ref, o_ref, info_ref, carry_ref = rest
    i = pl.program_id(0)

    @pl.when(i == 0)
    def _():
        carry_ref[...] = jnp.zeros_like(carry_ref)

    h1 = _layer_norm(alpha * h_ref[...] + y, lnw_ref[...], lnb_ref[...])
    o_ref[...] = h1

    h_hi = h1.astype(BF16)
    h_lo = (h1 - h_hi.astype(F32)).astype(BF16)
    rw = rw_ref[...]
    r_hi = rw.astype(BF16)
    r_lo = (rw - r_hi.astype(F32)).astype(BF16)
    logits = (lax.dot_general(r_lo, h_hi, _NT, preferred_element_type=F32)
              + lax.dot_general(r_hi, h_lo, _NT, preferred_element_type=F32)
              + lax.dot_general(r_hi, h_hi, _NT, preferred_element_type=F32))
    ex = jnp.exp(logits - jnp.max(logits, axis=0, keepdims=True))
    probs = ex / jnp.sum(ex, axis=0, keepdims=True)
    best = None
    for g in range(N_GROUPS):
        rows = [probs[g * EXPERTS_PER_GROUP + k:g * EXPERTS_PER_GROUP + k + 1, :]
                for k in range(EXPERTS_PER_GROUP)]
        v1, i1, v2, i2 = _top2(rows)
        cand = (v1 + v2, v1, i1 + g * EXPERTS_PER_GROUP, v2, i2 + g * EXPERTS_PER_GROUP)
        if best is None:
            best = cand
        else:
            take = cand[0] > best[0]
            best = tuple(jnp.where(take, cn, bs) for cn, bs in zip(cand, best))
    _, v1, e1, v2, e2 = best
    denom = v1 + v2
    g1 = v1 / denom
    g2 = v2 / denom

    eid = lax.broadcasted_iota(jnp.int32, (N_EXPERTS, tm), 0)
    oh1 = (eid == e1).astype(F32)
    oh2 = (eid == e2).astype(F32)
    oh = oh1 + oh2
    tt = lax.broadcasted_iota(jnp.int32, (tm, tm), 0)
    uu = lax.broadcasted_iota(jnp.int32, (tm, tm), 1)
    before = (tt < uu).astype(BF16)
    base = carry_ref[:, 0:1] + jnp.dot(oh.astype(BF16), before, preferred_element_type=F32)
    rank1 = jnp.sum(oh1 * base, axis=0, keepdims=True)
    rank2 = jnp.sum(oh2 * base, axis=0, keepdims=True)
    carry_ref[...] = carry_ref[...] + jnp.sum(oh, axis=1, keepdims=True)
    zero = jnp.zeros_like(g1)
    info_ref[...] = jnp.concatenate(
        [e1.astype(F32), e2.astype(F32), g1, g2, rank1, rank2, zero, zero], axis=0)


def _out_project(even, mixer_inputs, weights_bf16, h, ln_w, ln_b, router_wt, alpha):
    t, d = h.shape
    tm = TOKEN_ROWS
    row = lambda i: (i, 0)
    const = lambda i: (0, 0)
    if even:
        ret, h_fwd, h_bwd, proj_f32 = mixer_inputs
        w = h_fwd.shape[1]
        in_specs = [pl.BlockSpec((tm, ret.shape[1]), row), pl.BlockSpec((tm, w), row),
                    pl.BlockSpec((tm, w), row), pl.BlockSpec((tm, w), lambda i: (i, 1))]
        args = [ret, h_fwd, h_bwd, proj_f32]
    else:
        o_fwd, o_bwd, proj_bf16, norm_w = mixer_inputs
        in_specs = [pl.BlockSpec((tm, d), row), pl.BlockSpec((tm, d), row),
                    pl.BlockSpec((tm, d), lambda i: (i, 2)), pl.BlockSpec((1, d), const)]
        args = [o_fwd, o_bwd, proj_bf16, norm_w.reshape(1, d)]
    in_specs += [pl.BlockSpec(wm.shape, const) for wm in weights_bf16]
    in_specs += [pl.BlockSpec((tm, d), row), pl.BlockSpec((1, d), const), pl.BlockSpec((1, d), const),
                 pl.BlockSpec((N_EXPERTS, d), const)]
    args += list(weights_bf16) + [h, ln_w.reshape(1, d), ln_b.reshape(1, d), router_wt]
    return pl.pallas_call(
        functools.partial(_outproj_kernel, even=even, alpha=alpha, tm=tm),
        grid=(t // tm,),
        in_specs=in_specs,
        out_specs=[pl.BlockSpec((tm, d), row),
                   pl.BlockSpec((SUBLANES, tm), lambda i: (0, i))],
        out_shape=[jax.ShapeDtypeStruct((t, d), F32),
                   jax.ShapeDtypeStruct((SUBLANES, t), F32)],
        scratch_shapes=[pltpu.VMEM((N_EXPERTS, LANES), F32)],
        compiler_params=_params(("arbitrary",)),
        name="out_proj_router",
    )(*args)


def _pack_halves(x):
    n = x.shape[1] // 2
    hi = pltpu.bitcast(x[:, :n], jnp.uint32)
    lo = pltpu.bitcast(x[:, n:], jnp.uint32)
    return hi | (lo >> 16)


def _unpack_halves(p):
    hi = pltpu.bitcast(p & jnp.uint32(0xFFFF0000), F32)
    lo = pltpu.bitcast(p << 16, F32)
    return jnp.concatenate([hi.astype(BF16), lo.astype(BF16)], axis=1)


def _seg_expert_kernel(first_ref, count_ref, tail_ref, x_ref, wg_ref, wu_ref, wd_ref, y_ref,
                       xbuf, ybuf, wgb_ref, wub_ref, wdb_ref, semx, semy):
    e = pl.program_id(0)
    wgb_ref[...] = wg_ref[...].astype(BF16)
    wub_ref[...] = wu_ref[...].astype(BF16)
    wdb_ref[...] = wd_ref[...].astype(BF16)
    first = first_ref[e]
    n_used = tail_ref[0]

    def rows(g):
        return pl.ds(pl.multiple_of(g * MOE_ROWS, MOE_ROWS), MOE_ROWS)

    def fetch(g):
        return pltpu.make_async_copy(x_ref.at[rows(g)], xbuf.at[g % 2], semx.at[g % 2])

    def put(g):
        return pltpu.make_async_copy(ybuf.at[g % 2], y_ref.at[rows(g)], semy.at[g % 2])

    @pl.when(e == 0)
    def _():
        fetch(0).start()

    def body(g, carry):
        @pl.when(g + 1 < n_used)
        def _():
            fetch(g + 1).start()

        fetch(g).wait()

        @pl.when(g >= 2)
        def _():
            put(g - 2).wait()

        x = _unpack_halves(xbuf[g % 2])
        gate = jnp.dot(x, wgb_ref[...], preferred_element_type=F32)
        up = jnp.dot(x, wub_ref[...], preferred_element_type=F32)
        hid = (_silu(gate) * up).astype(BF16)
        y = jnp.dot(hid, wdb_ref[...], preferred_element_type=F32)
        ybuf[g % 2] = _pack_halves(y.astype(BF16).astype(F32))
        put(g).start()
        return carry

    lax.fori_loop(first, first + count_ref[e], body, 0)

    @pl.when(e == pl.num_programs(0) - 1)
    def _():
        @pl.when(n_used >= 2)
        def _():
            put(n_used - 2).wait()

        put(n_used - 1).wait()
        ybuf[0] = jnp.zeros_like(ybuf[0])

        def clear(b, carry):
            cp = pltpu.make_async_copy(
                ybuf.at[0], y_ref.at[pl.ds(pl.multiple_of(b * MOE_ROWS, MOE_ROWS), MOE_ROWS)], semy.at[0])
            cp.start()
            cp.wait()
            return carry

        lax.fori_loop(tail_ref[0], y_ref.shape[0] // MOE_ROWS, clear, 0)


def _seg_experts(xb, seg_first, seg_count, n_valid, w_gate, w_up, w_down, layer):
    p, half = xb.shape
    d = 2 * half
    de = w_gate.shape[3]
    weight = lambda shape: pl.BlockSpec((None, None) + shape, lambda e, a, b, c: (layer, e, 0, 0))
    return pl.pallas_call(
        _seg_expert_kernel,
        grid_spec=pltpu.PrefetchScalarGridSpec(
            num_scalar_prefetch=3,
            grid=(N_EXPERTS,),
            in_specs=[pl.BlockSpec(memory_space=pl.ANY), weight((d, de)), weight((d, de)), weight((de, d))],
            out_specs=pl.BlockSpec(memory_space=pl.ANY),
            scratch_shapes=[pltpu.VMEM((2, MOE_ROWS, half), jnp.uint32),
                            pltpu.VMEM((2, MOE_ROWS, half), jnp.uint32),
                            pltpu.VMEM((d, de), BF16), pltpu.VMEM((d, de), BF16), pltpu.VMEM((de, d), BF16),
                            pltpu.SemaphoreType.DMA((2,)), pltpu.SemaphoreType.DMA((2,))],
        ),
        out_shape=jax.ShapeDtypeStruct((p, half), jnp.uint32),
        compiler_params=_params(("arbitrary",)),
        name="moe_experts",
    )(seg_first, seg_count, n_valid, xb, w_gate, w_up, w_down)


def _clear_padding_blocks(zs_ref, xb_ref, zero_ref, sem):
    zero_ref[...] = jnp.zeros_like(zero_ref)

    def clear(row):
        start = pl.multiple_of(row, MOE_ROWS)
        return pltpu.make_async_copy(zero_ref, xb_ref.at[pl.ds(start, MOE_ROWS)], sem)

    for e in range(N_EXPERTS):
        clear(zs_ref[e]).start()
    for e in range(N_EXPERTS):
        clear(zs_ref[e]).wait()

    def clear_tail(b, carry):
        clear(b * MOE_ROWS).start()
        clear(b * MOE_ROWS).wait()
        return carry

    lax.fori_loop(zs_ref[N_EXPERTS], xb_ref.shape[0] // MOE_ROWS, clear_tail, 0)


def _run_copy(src_ref, src_row, dst_ref, dst_row, sem):
    src = pl.multiple_of(src_row, SUBLANES)
    dst = pl.multiple_of(dst_row, SUBLANES)
    return pltpu.make_async_copy(src_ref.at[pl.ds(src, SUBLANES)], dst_ref.at[pl.ds(dst, SUBLANES)], sem)


def _sort_dispatch_kernel(dst_ref, nch_ref, zs_ref, h_ref, lp_ref, xb_ref, xs_ref, zero_ref, sem, *, tm, nck):
    i = pl.program_id(0)

    @pl.when(i == 0)
    def _():
        _clear_padding_blocks(zs_ref, xb_ref, zero_ref, sem.at[0])

    slot = i % 2
    last = pl.num_programs(0) - 1

    def wait_runs(buf, count):
        def wait(j, carry):
            _run_copy(xs_ref.at[buf], 0, xb_ref, 0, sem.at[buf]).wait()
            return carry

        lax.fori_loop(0, count, wait, 0)

    @pl.when(i >= 2)
    def _():
        wait_runs(slot, nch_ref[jnp.maximum(i - 2, 0)])

    lp = lp_ref[...]
    pos = lax.broadcasted_iota(jnp.int32, (xs_ref.shape[1], tm), 0)
    perm = jnp.logical_or(pos == lp[0:1, :], pos == lp[1:2, :]).astype(BF16)
    xs_ref[slot] = _pack_halves(jnp.dot(perm, h_ref[...].astype(BF16), preferred_element_type=F32))

    def start(j, carry):
        _run_copy(xs_ref.at[slot], j * SUBLANES, xb_ref, dst_ref[i * nck + j], sem.at[slot]).start()
        return carry

    lax.fori_loop(0, nch_ref[i], start, 0)

    @pl.when(i == last)
    def _():
        @pl.when(i >= 1)
        def _():
            wait_runs(1 - slot, nch_ref[jnp.maximum(i - 1, 0)])

        wait_runs(slot, nch_ref[i])


def _sort_dispatch(h, lp_rows, chunk_dst, n_chunks, zero_start, n_rows, tm, sorted_rows):
    t, d = h.shape
    nck = sorted_rows // SUBLANES
    return pl.pallas_call(
        functools.partial(_sort_dispatch_kernel, tm=tm, nck=nck),
        grid_spec=pltpu.PrefetchScalarGridSpec(
            num_scalar_prefetch=3,
            grid=(t // tm,),
            in_specs=[pl.BlockSpec((tm, d), lambda i, a, b, c: (i, 0)),
                      pl.BlockSpec((2, tm), lambda i, a, b, c: (0, i))],
            out_specs=pl.BlockSpec(memory_space=pl.ANY),
            scratch_shapes=[pltpu.VMEM((2, sorted_rows, d // 2), jnp.uint32),
                            pltpu.VMEM((MOE_ROWS, d // 2), jnp.uint32), pltpu.SemaphoreType.DMA((2,))],
        ),
        out_shape=jax.ShapeDtypeStruct((n_rows, d // 2), jnp.uint32),
        compiler_params=_params(("arbitrary",)),
        name="moe_dispatch",
    )(chunk_dst, n_chunks, zero_start, h, lp_rows)


def _sort_combine_kernel(src_ref, nch_ref, h_ref, lp_ref, gates_ref, lnw_ref, lnb_ref, yb_ref, o_ref,
                         ys_ref, sem, *, tm, nck, alpha):
    i = pl.program_id(0)
    slot = i % 2

    def fetch(tile, buf):
        def start(j, carry):
            _run_copy(yb_ref, src_ref[tile * nck + j], ys_ref.at[buf], j * SUBLANES, sem.at[buf]).start()
            return carry

        lax.fori_loop(0, nch_ref[tile], start, 0)

    @pl.when(i == 0)
    def _():
        ys_ref[...] = jnp.zeros_like(ys_ref)
        fetch(0, 0)

    @pl.when(i + 1 < pl.num_programs(0))
    def _():
        fetch(i + 1, 1 - slot)

    def wait(j, carry):
        _run_copy(yb_ref, 0, ys_ref.at[slot], 0, sem.at[slot]).wait()
        return carry

    lax.fori_loop(0, nch_ref[i], wait, 0)

    lp = lp_ref[...]
    pos = lax.broadcasted_iota(jnp.int32, (tm, ys_ref.shape[1]), 1)
    ys = _unpack_halves(ys_ref[slot])
    y1 = jnp.dot((pos == lp[:, 0:1]).astype(BF16), ys, preferred_element_type=F32)
    y2 = jnp.dot((pos == lp[:, 1:2]).astype(BF16), ys, preferred_element_type=F32)
    y = gates_ref[:, 0:1] * y1 + gates_ref[:, 1:2] * y2
    o_ref[...] = _layer_norm(alpha * h_ref[...] + y, lnw_ref[...], lnb_ref[...])


def _sort_combine(h, yb, lp_cols, gates, chunk_src, n_chunks, ln_w, ln_b, alpha, tm, sorted_rows):
    t, d = h.shape
    nck = sorted_rows // SUBLANES
    row = lambda i, a, b: (i, 0)
    const = lambda i, a, b: (0, 0)
    return pl.pallas_call(
        functools.partial(_sort_combine_kernel, tm=tm, nck=nck, alpha=alpha),
        grid_spec=pltpu.PrefetchScalarGridSpec(
            num_scalar_prefetch=2,
            grid=(t // tm,),
            in_specs=[pl.BlockSpec((tm, d), row), pl.BlockSpec((tm, 2), row), pl.BlockSpec((tm, 2), row),
                      pl.BlockSpec((1, d), const), pl.BlockSpec((1, d), const),
                      pl.BlockSpec(memory_space=pl.ANY)],
            out_specs=pl.BlockSpec((tm, d), row),
            scratch_shapes=[pltpu.VMEM((2, sorted_rows, d // 2), jnp.uint32), pltpu.SemaphoreType.DMA((2,))],
        ),
        out_shape=jax.ShapeDtypeStruct((t, d), F32),
        compiler_params=_params(("arbitrary",)),
        name="moe_combine",
    )(chunk_src, n_chunks, h, lp_cols, gates, ln_w.reshape(1, d), ln_b.reshape(1, d), yb)


def _moe(h1, info, w_gate, w_up, w_down, layer, ln_w, ln_b, alpha):
    t = h1.shape[0]
    tm = SORT_ROWS
    ntile = t // tm
    run_pad = SUBLANES - 1
    sorted_rows = 2 * tm + LANES
    nck = sorted_rows // SUBLANES
    n_rows = (2 * t + ntile * N_EXPERTS * run_pad + MOE_ROWS - 1) // MOE_ROWS * MOE_ROWS + N_EXPERTS * MOE_ROWS
    i32 = jnp.int32
    e1, e2 = info[0].astype(i32), info[1].astype(i32)
    rank1, rank2 = info[4].astype(i32), info[5].astype(i32)
    expert_ids = jnp.arange(N_EXPERTS, dtype=i32)[None, :]
    oh1 = e1[:, None] == expert_ids
    oh2 = e2[:, None] == expert_ids
    cnt = jnp.logical_or(oh1, oh2).astype(i32).reshape(ntile, tm, N_EXPERTS).sum(axis=1)
    cnt8 = (cnt + run_pad) // SUBLANES * SUBLANES
    local = jnp.cumsum(cnt8, axis=1) - cnt8
    before = jnp.cumsum(cnt, axis=0) - cnt
    seg = cnt8.sum(axis=0)
    seg_pad = (seg + MOE_ROWS - 1) // MOE_ROWS * MOE_ROWS
    ends = jnp.cumsum(seg_pad)
    slot = (ends - seg_pad)[None, :] + jnp.cumsum(cnt8, axis=0) - cnt8
    shift = jnp.repeat(local - before, tm, axis=0)
    lp1 = jnp.sum(jnp.where(oh1, shift, 0), axis=1) + rank1
    lp2 = jnp.sum(jnp.where(oh2, shift, 0), axis=1) + rank2
    chunk_row = jnp.arange(nck, dtype=i32) * SUBLANES
    run_of = jnp.sum(((local + cnt8)[:, None, :] <= chunk_row[None, :, None]).astype(i32), axis=2)
    run_of = jnp.minimum(run_of, N_EXPERTS - 1)
    chunk_slot = jnp.sum(jnp.where(run_of[:, :, None] == expert_ids[None], (slot - local)[:, None, :], 0),
                         axis=2) + chunk_row[None, :]
    n_chunks = cnt8.sum(axis=1) // SUBLANES
    n_valid = (ends[-1] // MOE_ROWS).astype(i32).reshape(1)
    zero_start = jnp.concatenate([jnp.maximum(ends - MOE_ROWS, 0), n_valid]).astype(i32)
    gates = jnp.stack([info[2], info[3]], axis=1)
    chunk_slot = chunk_slot.reshape(-1).astype(i32)
    seg_first = ((ends - seg_pad) // MOE_ROWS).astype(i32)
    seg_count = (seg_pad // MOE_ROWS).astype(i32)

    xb = _sort_dispatch(h1, jnp.stack([lp1, lp2], axis=0), chunk_slot, n_chunks, zero_start, n_rows, tm,
                        sorted_rows)
    yb = _seg_experts(xb, seg_first, seg_count, n_valid, w_gate, w_up, w_down, layer)
    return _sort_combine(h1, yb, jnp.stack([lp1, lp2], axis=1), gates, chunk_slot, n_chunks, ln_w, ln_b,
                         alpha, tm, sorted_rows)


def _rotary_column_order(w_in):
    d = w_in.shape[0]
    nq = RET_HEADS * RET_DK

    def perm(w):
        return w.reshape(d, RET_HEADS, RET_DK // 2, 2).transpose(0, 3, 1, 2).reshape(d, nq)

    return jnp.concatenate([perm(w_in[:, :nq]), perm(w_in[:, nq:2 * nq]), w_in[:, 2 * nq:]], axis=1)


def kernel(x, w_in_even, w_out_even, lru_conv_w, lru_conv_b, lru_gate_w, lru_gate_b, lru_lambda,
           w_in_odd, w_out_odd, hg_lower_bounds, hg_norm_w, ln_w, ln_b, router_w,
           moe_w_gate, moe_w_up, moe_w_down):
    batch, seq_len, d = x.shape
    depth = ln_w.shape[0]
    alpha = (2.0 * depth) ** 0.25
    t = batch * seq_len
    h = x.reshape(t, d)
    router_wt = router_w.T
    half = RET_DK // 2
    inv_freq = ROPE_BASE ** (-jnp.arange(0, RET_DK, 2, dtype=F32) / RET_DK)
    inv_freq = jnp.tile(inv_freq, LANES // half).reshape(1, LANES)
    nret = RET_HEADS * RET_DV
    tm = min(PROJ_ROWS, seq_len)

    for layer in range(depth):
        j = layer // 2
        if layer % 2 == 0:
            w_in = _rotary_column_order(w_in_even[j]).astype(BF16)
            n_bf = 2 * RET_HEADS * RET_DK + 2 * nret
            proj_kernel = functools.partial(_proj_ret_kernel, seq_len=seq_len, tm=tm, n_bf=n_bf)
            proj_b, proj_f = _project(proj_kernel, "in_proj", h, w_in, inv_freq, n_bf, w_in.shape[1] - n_bf,
                                      seq_len)
            ret = _retention(proj_b, batch, seq_len)
            h_fwd, h_bwd = _rglru(proj_f, lru_conv_w[j], lru_conv_b[j], lru_gate_w[j], lru_gate_b[j],
                                  lru_lambda[j], batch, seq_len)
            w_out = w_out_even[j].astype(BF16)
            mixer_inputs, weights = (ret, h_fwd, h_bwd, proj_f), [w_out[:nret], w_out[nret:]]
        else:
            w = w_in_odd[j]
            w_in = jnp.concatenate([w[:, :2 * d], w[:, 4 * d:], w[:, 2 * d:4 * d]], axis=1).astype(BF16)
            proj_kernel = functools.partial(_proj_gla_kernel, layer=layer)
            proj_b, proj_f = _project(proj_kernel, "in_proj_gla", h, w_in, hg_lower_bounds, 5 * d, 2 * d,
                                      seq_len)
            o_fwd, o_bwd = _gla(proj_b, proj_f, batch, seq_len)
            mixer_inputs, weights = (o_fwd, o_bwd, proj_b, hg_norm_w[j]), [w_out_odd[j].astype(BF16)]
        h1, info = _out_project(layer % 2 == 0, mixer_inputs, weights, h, ln_w[layer, 0], ln_b[layer, 0],
                                router_wt, alpha)
        h = _moe(h1, info, moe_w_gate, moe_w_up, moe_w_down, layer, ln_w[layer, 1], ln_b[layer, 1], alpha)
    return h.reshape(batch, seq_len, d)
```

```python
import functools
import math

import jax
import jax.numpy as jnp
from jax import lax
from jax.experimental import pallas as pl
from jax.experimental.pallas import tpu as pltpu

F32 = jnp.float32
BF16 = jnp.bfloat16

RET_HEADS = 4
RET_DK = 64
RET_DV = 128
RET_CHUNK = 128
ROPE_BASE = 10000.0
LRU_BLOCKS = 4
LRU_BW = 128
LRU_CONV = 4
LRU_C = 8.0
HG_HEADS = 8
HG_DK = 128
HG_DV = 128
N_EXPERTS = 16
N_GROUPS = 4
EXPERTS_PER_GROUP = 4
LN_EPS = 1e-5
RMS_EPS = 1e-6

LANES = 128
SUBLANES = 8
PROJ_ROWS = 512
PROJ_COLS = 512
RET_CHUNKS_PER_STEP = 4
LRU_ROWS = 512
GLA_CHUNK = 32
GLA_ROWS = 256
TOKEN_ROWS = 512
SORT_ROWS = 256
MOE_ROWS = 512
VMEM_LIMIT = 56 * 1024 * 1024

_NT = (((1,), (1,)), ((), ()))
_TN = (((0,), (0,)), ((), ()))


def _params(sem):
    return pltpu.CompilerParams(dimension_semantics=sem, vmem_limit_bytes=VMEM_LIMIT)


def _sigmoid(x):
    return 0.5 * jnp.tanh(0.5 * x) + 0.5


def _silu(x):
    return x * _sigmoid(x)


def _softplus(x):
    return jnp.maximum(x, 0.0) + jnp.log1p(jnp.exp(-jnp.abs(x)))


def _layer_norm(u, w, b):
    mu = jnp.mean(u, axis=-1, keepdims=True)
    d = u - mu
    var = jnp.mean(d * d, axis=-1, keepdims=True)
    return d * lax.rsqrt(var + LN_EPS) * w + b


def _rotary_table_kernel(inv_ref, cos_ref, sin_ref, *, tm):
    row = lax.broadcasted_iota(jnp.int32, (tm, 1), 0) + pl.program_id(0) * tm
    ang = row.astype(F32) * inv_ref[...]
    cos_ref[...] = jnp.cos(ang)
    sin_ref[...] = jnp.sin(ang)


def _rotary_table(inv_freq, seq_len):
    tm = min(PROJ_ROWS, seq_len)
    table = jax.ShapeDtypeStruct((seq_len, LANES), F32)
    return pl.pallas_call(
        functools.partial(_rotary_table_kernel, tm=tm),
        grid=(seq_len // tm,),
        in_specs=[pl.BlockSpec((1, LANES), lambda i: (0, 0))],
        out_specs=[pl.BlockSpec((tm, LANES), lambda i: (i, 0))] * 2,
        out_shape=[table, table],
        compiler_params=_params(("arbitrary",)),
        name="rotary_table",
    )(inv_freq)


def _proj_ret_kernel(x_ref, w_ref, cos_ref, sin_ref, ob_ref, of_ref, *, n_bf):
    xb = x_ref[...].astype(BF16)
    tn = PROJ_COLS
    for j in range(w_ref.shape[1] // tn):
        acc = jnp.dot(xb, w_ref[:, j * tn:(j + 1) * tn], preferred_element_type=F32)
        if j == 0:
            cos = cos_ref[...]
            sin = sin_ref[...]
            scale = RET_DK ** -0.5
            q1, q2 = acc[:, 0:128], acc[:, 128:256]
            k1, k2 = acc[:, 256:384], acc[:, 384:512]
            ob_ref[:, 0:128] = ((q1 * cos - q2 * sin) * scale).astype(BF16)
            ob_ref[:, 128:256] = ((q1 * sin + q2 * cos) * scale).astype(BF16)
            ob_ref[:, 256:384] = (k1 * cos - k2 * sin).astype(BF16)
            ob_ref[:, 384:512] = (k1 * sin + k2 * cos).astype(BF16)
        elif (j + 1) * tn <= n_bf:
            ob_ref[:, j * tn:(j + 1) * tn] = acc.astype(BF16)
        else:
            of_ref[:, j * tn - n_bf:(j + 1) * tn - n_bf] = acc


def _gla_lower_bound(p, layer):
    e = jnp.exp(p - jnp.max(p, axis=0, keepdims=True))
    sm = e / jnp.sum(e, axis=0, keepdims=True)
    lb = jnp.zeros((1, p.shape[1]), F32)
    for r in range(1, layer + 1):
        lb = lb + sm[r:r + 1, :]
    return lb


def _proj_gla_kernel(x_ref, w_ref, lbp_ref, ob_ref, of_ref, *, layer):
    xb = x_ref[...].astype(BF16)
    d = w_ref.shape[1] // 5
    tn = PROJ_COLS
    lb = _gla_lower_bound(lbp_ref[...], layer)
    mean = 0.5 * (1.0 + lb)
    spread = 0.5 * (1.0 - lb)
    q_scale = 0.5 * HG_DK ** -0.5
    for j in range(5 * d // tn):
        col = j * tn
        acc = jnp.dot(xb, w_ref[:, col:col + tn], preferred_element_type=F32)
        if col < d:
            ob_ref[:, col:col + tn] = (acc * (q_scale * jnp.tanh(0.5 * acc) + q_scale)).astype(BF16)
        elif col < 3 * d:
            ob_ref[:, col:col + tn] = acc.astype(BF16)
        else:
            zc = col - 3 * d
            swing = spread[:, zc % d:zc % d + tn] * jnp.tanh(0.5 * acc)
            of_ref[:, zc:zc + tn] = jnp.log2(mean[:, zc % d:zc % d + tn] + swing)
            ob_ref[:, col:col + tn] = (spread[:, zc % d:zc % d + tn] - swing).astype(BF16)


def _project(kernel_fn, name, x, w_bf16, sides, side_specs, n_bf, n_f32, seq_len):
    t, k = x.shape
    n = w_bf16.shape[1]
    tm = min(PROJ_ROWS, seq_len)
    return pl.pallas_call(
        kernel_fn,
        grid=(t // tm,),
        in_specs=[pl.BlockSpec((tm, k), lambda i: (i, 0)), pl.BlockSpec((k, n), lambda i: (0, 0))] + side_specs,
        out_specs=[pl.BlockSpec((tm, n_bf), lambda i: (i, 0)),
                   pl.BlockSpec((tm, n_f32), lambda i: (i, 0))],
        out_shape=[jax.ShapeDtypeStruct((t, n_bf), BF16),
                   jax.ShapeDtypeStruct((t, n_f32), F32)],
        compiler_params=_params(("arbitrary",)),
        name=name,
    )(x, w_bf16, *sides)


def _ret_log_gamma(head):
    out = jnp.full(head.shape, math.log1p(-(2.0 ** -5.0)), F32)
    for h in range(1, RET_HEADS):
        out = jnp.where(head == h, math.log1p(-(2.0 ** (-5.0 - h))), out)
    return out


def _ret_lane_head():
    lane = lax.broadcasted_iota(jnp.int32, (1, 2 * LANES), 1)
    return (lane % LANES) // (RET_DK // 2)


def _ret_state_mask():
    shape = (RET_HEADS * RET_DV, 2 * LANES)
    row_head = lax.broadcasted_iota(jnp.int32, shape, 0) // RET_DV
    col_head = (lax.broadcasted_iota(jnp.int32, shape, 1) % LANES) // (RET_DK // 2)
    return row_head == col_head


def _ret_bstate_kernel(k_ref, v_ref, sb_ref, s_ref, *, cps):
    c = RET_CHUNK

    @pl.when(pl.program_id(1) == 0)
    def _():
        s_ref[...] = jnp.zeros_like(s_ref)

    lg = _ret_log_gamma(_ret_lane_head())
    idx = lax.broadcasted_iota(jnp.int32, (c, 1), 0).astype(F32)
    k_decay = jnp.exp(lg * idx)
    chunk_decay = jnp.exp(lg * float(c))
    mask = _ret_state_mask()
    for cc in reversed(range(cps)):
        rows = slice(cc * c, (cc + 1) * c)
        sb_ref[0, cc] = s_ref[...].astype(BF16)
        kb = (k_ref[rows, :] * k_decay).astype(BF16)
        upd = lax.dot_general(v_ref[rows, :], kb, _TN, preferred_element_type=F32)
        s_ref[...] = s_ref[...] * chunk_decay + jnp.where(mask, upd, 0.0)


def _ret_out_kernel(q_ref, k_ref, v_ref, g_ref, sb_ref, o_ref, s_ref, *, cps):
    c = RET_CHUNK

    @pl.when(pl.program_id(1) == 0)
    def _():
        s_ref[...] = jnp.zeros_like(s_ref)

    lane_head = _ret_lane_head()
    lg = _ret_log_gamma(lane_head)
    idx = lax.broadcasted_iota(jnp.int32, (c, 1), 0).astype(F32)
    q_decay_f = jnp.exp(lg * (idx + 1.0))
    q_decay_b = jnp.exp(lg * (float(c) - idx))
    k_decay = jnp.exp(lg * (float(c) - 1.0 - idx))
    chunk_decay = jnp.exp(lg * float(c))
    mask = _ret_state_mask()
    ii = lax.broadcasted_iota(jnp.int32, (c, c), 0)
    jj = lax.broadcasted_iota(jnp.int32, (c, c), 1)
    dist = jnp.abs(ii - jj).astype(F32)
    intra_decay = [jnp.exp(math.log1p(-(2.0 ** (-5.0 - h))) * dist) for h in range(RET_HEADS)]

    for cc in range(cps):
        rows = slice(cc * c, (cc + 1) * c)
        q = q_ref[rows, :]
        k = k_ref[rows, :]
        v = v_ref[rows, :]
        qf = (q * q_decay_f).astype(BF16)
        qb = (q * q_decay_b).astype(BF16)
        cross = (lax.dot_general(qf, s_ref[...].astype(BF16), _NT, preferred_element_type=F32)
                 + lax.dot_general(qb, sb_ref[0, cc], _NT, preferred_element_type=F32))
        for h in range(RET_HEADS):
            qh = jnp.where(lane_head == h, q, jnp.zeros_like(q))
            s = lax.dot_general(qh, k, _NT, preferred_element_type=F32) * intra_decay[h]
            cols = slice(h * RET_DV, (h + 1) * RET_DV)
            o = jnp.dot(s.astype(BF16), v[:, cols], preferred_element_type=F32) + cross[:, cols]
            mu = jnp.mean(o, axis=-1, keepdims=True)
            d = o - mu
            var = jnp.mean(d * d, axis=-1, keepdims=True)
            gate = _silu(g_ref[rows, cols].astype(F32))
            o_ref[rows, cols] = (gate * (d * lax.rsqrt(var + LN_EPS))).astype(BF16)
        kf = (k * k_decay).astype(BF16)
        upd = lax.dot_general(v, kf, _TN, preferred_element_type=F32)
        s_ref[...] = s_ref[...] * chunk_decay + jnp.where(mask, upd, 0.0)


def _retention(proj, batch, seq_len):
    t = proj.shape[0]
    c = RET_CHUNK
    cps = min(RET_CHUNKS_PER_STEP, seq_len // c)
    rows = cps * c
    ns = seq_len // rows
    dv = RET_HEADS * RET_DV
    state_shape = (dv, 2 * LANES)
    rev = lambda b, n: b * ns + (ns - 1 - n)
    fwd = lambda b, n: b * ns + n
    sb = pl.pallas_call(
        functools.partial(_ret_bstate_kernel, cps=cps),
        grid=(batch, ns),
        in_specs=[
            pl.BlockSpec((rows, 2 * LANES), lambda b, n: (rev(b, n), 1)),
            pl.BlockSpec((rows, dv), lambda b, n: (rev(b, n), 1)),
        ],
        out_specs=pl.BlockSpec((1, cps) + state_shape, lambda b, n: (b, ns - 1 - n, 0, 0)),
        out_shape=jax.ShapeDtypeStruct((batch, ns * cps) + state_shape, BF16),
        scratch_shapes=[pltpu.VMEM(state_shape, F32)],
        compiler_params=_params(("arbitrary", "arbitrary")),
        name="ret_bstate",
    )(proj, proj)
    return pl.pallas_call(
        functools.partial(_ret_out_kernel, cps=cps),
        grid=(batch, ns),
        in_specs=[
            pl.BlockSpec((rows, 2 * LANES), lambda b, n: (fwd(b, n), 0)),
            pl.BlockSpec((rows, 2 * LANES), lambda b, n: (fwd(b, n), 1)),
            pl.BlockSpec((rows, dv), lambda b, n: (fwd(b, n), 1)),
            pl.BlockSpec((rows, dv), lambda b, n: (fwd(b, n), 2)),
            pl.BlockSpec((1, cps) + state_shape, lambda b, n: (b, n, 0, 0)),
        ],
        out_specs=pl.BlockSpec((rows, dv), lambda b, n: (fwd(b, n), 0)),
        out_shape=jax.ShapeDtypeStruct((t, dv), BF16),
        scratch_shapes=[pltpu.VMEM(state_shape, F32)],
        compiler_params=_params(("arbitrary", "arbitrary")),
        name="ret_out",
    )(proj, proj, proj, proj, sb)


def _lru_kernel(xfp_ref, xf_ref, xfn_ref, xbp_ref, xb_ref, xbn_ref, cw_ref, cb_ref, gw_ref, gb_ref,
                lam_ref, hf_ref, hb_ref, xx_ref, a_ref, b_ref, h_ref, *, nt, ts, batch):
    i = pl.program_id(0)
    halo = SUBLANES
    lo = LRU_CONV // 2

    @pl.when(i == 0)
    def _():
        h_ref[...] = jnp.zeros_like(h_ref)

    def prepare(xp_ref, x_ref, xn_ref, tile, z, slot):
        for b in range(batch):
            xx_ref[0:halo, :] = jnp.where(tile == 0, 0.0, xp_ref[b])
            xx_ref[halo:halo + ts, :] = x_ref[b]
            xx_ref[halo + ts:2 * halo + ts, :] = jnp.where(tile == nt - 1, 0.0, xn_ref[b])
            xc = cb_ref[...]
            for tap in range(LRU_CONV):
                xc = xc + cw_ref[tap:tap + 1, :] * xx_ref[pl.ds(halo - lo + tap, ts), :]
            for n in range(LRU_BLOCKS):
                cols = slice(n * LRU_BW, (n + 1) * LRU_BW)
                xn = xc[:, cols]
                g = (jnp.dot(xn.astype(BF16), gw_ref[z, n], preferred_element_type=F32)
                     + gb_ref[z, n:n + 1, :])
                r = _sigmoid(g[:, :LRU_BW])
                ig = _sigmoid(g[:, LRU_BW:])
                a = jnp.exp((-LRU_C) * r * _softplus(-lam_ref[z, :, cols]))
                a_ref[slot + b, :, cols] = a
                b_ref[slot + b, :, cols] = jnp.sqrt(1.0 - a * a) * (ig * xn)

    prepare(xfp_ref, xf_ref, xfn_ref, i, 0, 0)
    prepare(xbp_ref, xb_ref, xbn_ref, nt - 1 - i, 1, batch)

    def step(s, hs):
        out = []
        for k in range(2 * batch):
            row = s if k < batch else ts - 1 - s
            h = a_ref[k, pl.ds(row, 1), :] * hs[k] + b_ref[k, pl.ds(row, 1), :]
            if k < batch:
                hf_ref[k, pl.ds(row, 1), :] = h
            else:
                hb_ref[k - batch, pl.ds(row, 1), :] = h
            out.append(h)
        return tuple(out)

    hs = lax.fori_loop(0, ts, step, tuple(h_ref[k] for k in range(2 * batch)), unroll=8)
    for k in range(2 * batch):
        h_ref[k] = hs[k]


def _rglru(proj_f32, conv_w, conv_b, gate_w, gate_b, lam, batch, seq_len):
    w = LRU_BLOCKS * LRU_BW
    ts = min(LRU_ROWS, seq_len)
    nt = seq_len // ts
    rows8 = ts // SUBLANES
    last8 = seq_len // SUBLANES - 1
    x3 = proj_f32.reshape(batch, seq_len, proj_f32.shape[1])
    gw = jnp.concatenate([gate_w[:, 0], gate_w[:, 1]], axis=-1).astype(BF16)
    gb = jnp.concatenate([gate_b[:, 0], gate_b[:, 1]], axis=-1)
    bwd = lambda i: nt - 1 - i

    def tile_specs(tile):
        return [
            pl.BlockSpec((batch, SUBLANES, w), lambda i: (0, jnp.maximum(tile(i) * rows8 - 1, 0), 0)),
            pl.BlockSpec((batch, ts, w), lambda i: (0, tile(i), 0)),
            pl.BlockSpec((batch, SUBLANES, w), lambda i: (0, jnp.minimum((tile(i) + 1) * rows8, last8), 0)),
        ]

    full = lambda a: pl.BlockSpec(a.shape, lambda i: (0,) * a.ndim)
    cb = conv_b.reshape(1, w)
    lam3 = lam.reshape(2, 1, w)
    state = jax.ShapeDtypeStruct((batch, seq_len, w), F32)
    h_fwd, h_bwd = pl.pallas_call(
        functools.partial(_lru_kernel, nt=nt, ts=ts, batch=batch),
        grid=(nt,),
        in_specs=tile_specs(lambda i: i) + tile_specs(bwd) + [full(conv_w), full(cb), full(gw), full(gb),
                                                               full(lam3)],
        out_specs=[pl.BlockSpec((batch, ts, w), lambda i: (0, i, 0)),
                   pl.BlockSpec((batch, ts, w), lambda i: (0, bwd(i), 0))],
        out_shape=[state, state],
        scratch_shapes=[pltpu.VMEM((ts + 2 * SUBLANES, w), F32), pltpu.VMEM((2 * batch, ts, w), F32),
                        pltpu.VMEM((2 * batch, ts, w), F32), pltpu.VMEM((2 * batch, 1, w), F32)],
        compiler_params=_params(("arbitrary",)),
        name="lru_scan",
    )(x3, x3, x3, x3, x3, x3, conv_w, cb, gw, gb, lam3)
    return h_fwd.reshape(batch * seq_len, w), h_bwd.reshape(batch * seq_len, w)


def _gla_kernel(qf_ref, vf_ref, kf_ref, lf_ref, qb_ref, vb_ref, kb_ref, lb_ref, of_ref, ob_ref, s_ref,
                *, ts, batch):
    c = GLA_CHUNK
    nchunks = ts // c

    @pl.when(pl.program_id(0) == 0)
    def _():
        s_ref[...] = jnp.zeros_like(s_ref)

    head_cols = [slice(h * HG_DK, (h + 1) * HG_DK) for h in range(HG_HEADS)]
    hc = HG_HEADS * c
    ii = lax.broadcasted_iota(jnp.int32, (c, c), 0)
    jj = lax.broadcasted_iota(jnp.int32, (c, c), 1)
    si = lax.broadcasted_iota(jnp.int32, (hc, hc), 0)
    sj = lax.broadcasted_iota(jnp.int32, (hc, hc), 1)
    same_head = si // c == sj // c

    def stack(a):
        return jnp.concatenate([a[:, cols] for cols in head_cols], axis=0)

    tri = {False: (jj <= ii).astype(BF16), True: (jj >= ii).astype(BF16)}
    tri2 = {r: jnp.concatenate([m, m], axis=1) for r, m in tri.items()}
    zero_block = jnp.zeros((c, HG_DK), BF16)
    zero_state = jnp.zeros((HG_DK, HG_DV), BF16)
    keep = {False: jnp.logical_and(same_head, sj <= si), True: jnp.logical_and(same_head, sj >= si)}
    chains = ([(qf_ref, vf_ref, kf_ref, lf_ref, of_ref, b, False) for b in range(batch)]
              + [(qb_ref, vb_ref, kb_ref, lb_ref, ob_ref, b, True) for b in range(batch)])

    def gates(cc):
        work = []
        for q_ref, v_ref, k_ref, l_ref, o_ref, b, reverse in chains:
            rows = pl.ds(((nchunks - 1 - cc) if reverse else cc) * c, c)
            log_f = l_ref[b, rows, :]
            f_hi = log_f.astype(BF16)
            f_lo = (log_f - f_hi.astype(F32)).astype(BF16)
            work.append(dict(rows=rows, reverse=reverse, o_ref=o_ref, b=b,
                             key=k_ref[b, rows, :].astype(F32), qs=q_ref[b, rows, :].astype(F32),
                             v=v_ref[b, rows, :], split=(f_hi, f_lo)))
        return work

    def cumulate(work):
        for w in work:
            f_hi, f_lo = w["split"]
            w["bcum"] = jnp.dot(tri2[w["reverse"]], jnp.concatenate([f_lo, f_hi], axis=0),
                                preferred_element_type=F32)

    half = c // 2
    row = lax.broadcasted_iota(jnp.int32, (c, 1), 0)
    first_half = row < half

    def decays(work):
        for w in work:
            bcum = w["bcum"]
            rev = w["reverse"]
            end = 0 if rev else c - 1
            b_end = bcum[end:end + 1, :]
            ref = jnp.where(first_half, bcum[half // 2:half // 2 + 1, :], bcum[half + half // 2:half + half // 2 + 1, :])
            qe = w["qs"] * jnp.exp2(bcum - ref)
            ke = w["key"] * jnp.exp2(ref - bcum)
            boundary = bcum[half:half + 1, :] if rev else bcum[half - 1:half, :]
            cross = jnp.exp2(-jnp.abs(bcum - boundary))
            queries = first_half if rev else jnp.logical_not(first_half)
            w["q3"] = [jnp.where(first_half, qe, 0.0).astype(BF16), jnp.where(first_half, 0.0, qe).astype(BF16),
                       jnp.where(queries, w["qs"] * cross, 0.0).astype(BF16)]
            w["k3"] = [jnp.where(first_half, ke, 0.0), jnp.where(first_half, 0.0, ke),
                       jnp.where(queries, 0.0, w["key"] * cross)]
            w["qd"] = (w["qs"] * jnp.exp2(bcum)).astype(BF16)
            w["kd"] = (w["key"] * jnp.exp2(b_end - bcum)).astype(BF16)
            w["decay"] = jnp.exp2(b_end)

    def stack3(parts):
        return jnp.concatenate([jnp.concatenate([p[:, cols] for p in parts], axis=1) for cols in head_cols], axis=0)

    def scores(work):
        for w in work:
            w["att"] = jnp.dot(stack3(w["q3"]), stack3(w["k3"]).T.astype(BF16), preferred_element_type=F32)

    def intra(work):
        for w in work:
            att = jnp.where(keep[w["reverse"]], w["att"], 0.0).astype(BF16)
            w["intra"] = jnp.dot(att, stack(w["v"]), preferred_element_type=F32)

    def inter(work):
        for k, w in enumerate(work):
            w["st"] = [s_ref[k, h] for h in range(HG_HEADS)]
            out = []
            for h in range(0, HG_HEADS, 2):
                sa = w["st"][h].T.astype(BF16)
                sb = w["st"][h + 1].T.astype(BF16)
                rhs = jnp.concatenate([jnp.concatenate([sa, zero_state], axis=1),
                                       jnp.concatenate([zero_state, sb], axis=1)], axis=0)
                pair = jnp.dot(w["qd"][:, h * HG_DK:(h + 2) * HG_DK], rhs, preferred_element_type=F32)
                out += [pair[:, :HG_DV], pair[:, HG_DV:]]
            w["inter"] = out

    def update(work):
        for w in work:
            upd = []
            for h in range(0, HG_HEADS, 2):
                ca, cb = head_cols[h], head_cols[h + 1]
                lhs = jnp.concatenate([w["v"][:, ca], w["v"][:, cb]], axis=0)
                rhs = jnp.concatenate([jnp.concatenate([w["kd"][:, ca], zero_block], axis=1),
                                       jnp.concatenate([zero_block, w["kd"][:, cb]], axis=1)], axis=0)
                pair = lax.dot_general(lhs, rhs, _TN, preferred_element_type=F32)
                upd += [pair[:, :HG_DK], pair[:, HG_DK:]]
            w["upd"] = upd

    def finish(work):
        for k, w in enumerate(work):
            for h, cols in enumerate(head_cols):
                w["o_ref"][w["b"], w["rows"], cols] = w["intra"][h * c:(h + 1) * c, :] + w["inter"][h]
                s_ref[k, h] = w["st"][h] * w["decay"][:, cols] + w["upd"][h]

    cur = gates(0)
    cumulate(cur)
    decays(cur)
    for cc in range(nchunks):
        more = cc + 1 < nchunks
        scores(cur)
        nxt = gates(cc + 1) if more else None
        if more:
            cumulate(nxt)
        intra(cur)
        inter(cur)
        update(cur)
        if more:
            decays(nxt)
        finish(cur)
        cur = nxt


def _gla(proj_bf16, proj_f32, batch, seq_len):
    d = HG_HEADS * HG_DK
    ts = min(GLA_ROWS, seq_len)
    nt = seq_len // ts
    pb = proj_bf16.reshape(batch, seq_len, proj_bf16.shape[1])
    pf = proj_f32.reshape(batch, seq_len, proj_f32.shape[1])
    bwd = lambda i: nt - 1 - i
    blk = lambda tile, col: pl.BlockSpec((batch, ts, d), lambda i: (0, tile(i), col))
    fwd = lambda i: i
    out = jax.ShapeDtypeStruct((batch, seq_len, d), F32)
    o_fwd, o_bwd = pl.pallas_call(
        functools.partial(_gla_kernel, ts=ts, batch=batch),
        grid=(nt,),
        in_specs=[blk(fwd, 0), blk(fwd, 1), blk(fwd, 3), blk(fwd, 0),
                  blk(bwd, 0), blk(bwd, 1), blk(bwd, 4), blk(bwd, 1)],
        out_specs=[blk(fwd, 0), blk(bwd, 0)],
        out_shape=[out, out],
        scratch_shapes=[pltpu.VMEM((2 * batch, HG_HEADS, HG_DV, HG_DK), F32)],
        compiler_params=_params(("arbitrary",)),
        name="gla_scan",
    )(pb, pb, pb, pf, pb, pb, pb, pf)
    return o_fwd.reshape(batch * seq_len, d), o_bwd.reshape(batch * seq_len, d)


def _top2(p):
    v1 = jnp.maximum(jnp.maximum(p[0], p[1]), jnp.maximum(p[2], p[3]))
    i1 = jnp.where(p[0] == v1, 0, jnp.where(p[1] == v1, 1, jnp.where(p[2] == v1, 2, 3)))
    q = [jnp.where(i1 == k, -1.0, p[k]) for k in range(4)]
    v2 = jnp.maximum(jnp.maximum(q[0], q[1]), jnp.maximum(q[2], q[3]))
    i2 = jnp.where(q[0] == v2, 0, jnp.where(q[1] == v2, 1, jnp.where(q[2] == v2, 2, 3)))
    return v1, i1, v2, i2


def _outproj_kernel(*refs, even, alpha, tm):
    if even:
        ret_ref, hf_ref, hb_ref, gr_ref, w0_ref, w1_ref = refs[:6]
        rest = refs[6:]
        lru = ((hf_ref[...] + hb_ref[...]) * jax.nn.gelu(gr_ref[...])).astype(BF16)
        y = (jnp.dot(ret_ref[...], w0_ref[...], preferred_element_type=F32)
             + jnp.dot(lru, w1_ref[...], preferred_element_type=F32))
    else:
        of_ref, ob_ref, g_ref, nw_ref, w0_ref = refs[:5]
        rest = refs[5:]
        o = of_ref[...] + ob_ref[...]
        ms = jnp.mean(o * o, axis=-1, keepdims=True)
        mix = o * lax.rsqrt(ms + RMS_EPS) * nw_ref[...] * _silu(g_ref[...].astype(F32))
        y = jnp.dot(mix.astype(BF16), w0_ref[...], preferred_element_type=F32)
    h_ref, lnw_ref, lnb_ref, rw_ref, o_ref, info_ref, carry_ref, before_ref = rest
    i = pl.program_id(0)

    @pl.when(i == 0)
    def _():
        carry_ref[...] = jnp.zeros_like(carry_ref)
        tt = lax.broadcasted_iota(jnp.int32, (tm, tm), 0)
        uu = lax.broadcasted_iota(jnp.int32, (tm, tm), 1)
        before_ref[...] = (tt < uu).astype(BF16)

    h1 = _layer_norm(alpha * h_ref[...] + y, lnw_ref[...], lnb_ref[...])
    o_ref[...] = h1

    h_hi = h1.astype(BF16)
    h_lo = (h1 - h_hi.astype(F32)).astype(BF16)
    rw = rw_ref[...]
    r_hi = rw.astype(BF16)
    r_lo = (rw - r_hi.astype(F32)).astype(BF16)
    both = lax.dot_general(jnp.concatenate([r_hi, r_lo], axis=0), h_hi, _NT, preferred_element_type=F32)
    logits = (both[N_EXPERTS:, :] + lax.dot_general(r_hi, h_lo, _NT, preferred_element_type=F32)
              + both[:N_EXPERTS, :])
    ex = jnp.exp(logits - jnp.max(logits, axis=0, keepdims=True))
    probs = ex / jnp.sum(ex, axis=0, keepdims=True)
    best = None
    for g in range(N_GROUPS):
        rows = [probs[g * EXPERTS_PER_GROUP + k:g * EXPERTS_PER_GROUP + k + 1, :]
                for k in range(EXPERTS_PER_GROUP)]
        v1, i1, v2, i2 = _top2(rows)
        cand = (v1 + v2, v1, i1 + g * EXPERTS_PER_GROUP, v2, i2 + g * EXPERTS_PER_GROUP)
        if best is None:
            best = cand
        else:
            take = cand[0] > best[0]
            best = tuple(jnp.where(take, cn, bs) for cn, bs in zip(cand, best))
    _, v1, e1, v2, e2 = best
    denom = v1 + v2
    g1 = v1 / denom
    g2 = v2 / denom

    eid = lax.broadcasted_iota(jnp.int32, (N_EXPERTS, tm), 0)
    oh1 = (eid == e1).astype(F32)
    oh2 = (eid == e2).astype(F32)
    oh = oh1 + oh2
    base = carry_ref[:, 0:1] + jnp.dot(oh.astype(BF16), before_ref[...], preferred_element_type=F32)
    rank1 = jnp.sum(oh1 * base, axis=0, keepdims=True)
    rank2 = jnp.sum(oh2 * base, axis=0, keepdims=True)
    carry_ref[...] = carry_ref[...] + jnp.sum(oh, axis=1, keepdims=True)
    zero = jnp.zeros_like(g1)
    info_ref[...] = jnp.concatenate(
        [e1.astype(F32), e2.astype(F32), g1, g2, rank1, rank2, zero, zero], axis=0)


def _out_project(even, mixer_inputs, weights_bf16, h, ln_w, ln_b, router_wt, alpha):
    t, d = h.shape
    tm = TOKEN_ROWS
    row = lambda i: (i, 0)
    const = lambda i: (0, 0)
    if even:
        ret, h_fwd, h_bwd, proj_f32 = mixer_inputs
        w = h_fwd.shape[1]
        in_specs = [pl.BlockSpec((tm, ret.shape[1]), row), pl.BlockSpec((tm, w), row),
                    pl.BlockSpec((tm, w), row), pl.BlockSpec((tm, w), lambda i: (i, 1))]
        args = [ret, h_fwd, h_bwd, proj_f32]
    else:
        o_fwd, o_bwd, proj_bf16, norm_w = mixer_inputs
        in_specs = [pl.BlockSpec((tm, d), row), pl.BlockSpec((tm, d), row),
                    pl.BlockSpec((tm, d), lambda i: (i, 2)), pl.BlockSpec((1, d), const)]
        args = [o_fwd, o_bwd, proj_bf16, norm_w.reshape(1, d)]
    in_specs += [pl.BlockSpec(wm.shape, const) for wm in weights_bf16]
    in_specs += [pl.BlockSpec((tm, d), row), pl.BlockSpec((1, d), const), pl.BlockSpec((1, d), const),
                 pl.BlockSpec((N_EXPERTS, d), const)]
    args += list(weights_bf16) + [h, ln_w.reshape(1, d), ln_b.reshape(1, d), router_wt]
    return pl.pallas_call(
        functools.partial(_outproj_kernel, even=even, alpha=alpha, tm=tm),
        grid=(t // tm,),
        in_specs=in_specs,
        out_specs=[pl.BlockSpec((tm, d), row),
                   pl.BlockSpec((SUBLANES, tm), lambda i: (0, i))],
        out_shape=[jax.ShapeDtypeStruct((t, d), F32),
                   jax.ShapeDtypeStruct((SUBLANES, t), F32)],
        scratch_shapes=[pltpu.VMEM((N_EXPERTS, LANES), F32), pltpu.VMEM((tm, tm), BF16)],
        compiler_params=_params(("arbitrary",)),
        name="out_proj_router",
    )(*args)


def _pack_halves(x):
    n = x.shape[1] // 2
    hi = pltpu.bitcast(x[:, :n], jnp.uint32)
    lo = pltpu.bitcast(x[:, n:], jnp.uint32)
    return hi | (lo >> 16)


def _unpack_halves(p):
    hi = pltpu.bitcast(p & jnp.uint32(0xFFFF0000), F32)
    lo = pltpu.bitcast(p << 16, F32)
    return jnp.concatenate([hi.astype(BF16), lo.astype(BF16)], axis=1)


def _seg_expert_kernel(first_ref, count_ref, tail_ref, x_ref, wg_ref, wu_ref, wd_ref, y_ref,
                       xbuf, ybuf, wgb_ref, wub_ref, wdb_ref, semx, semy):
    e = pl.program_id(0)
    wgb_ref[...] = wg_ref[...].astype(BF16)
    wub_ref[...] = wu_ref[...].astype(BF16)
    wdb_ref[...] = wd_ref[...].astype(BF16)
    first = first_ref[e]
    n_used = tail_ref[0]

    def rows(g):
        return pl.ds(pl.multiple_of(g * MOE_ROWS, MOE_ROWS), MOE_ROWS)

    def fetch(g):
        return pltpu.make_async_copy(x_ref.at[rows(g)], xbuf.at[g % 2], semx.at[g % 2])

    def put(g):
        return pltpu.make_async_copy(ybuf.at[g % 2], y_ref.at[rows(g)], semy.at[g % 2])

    @pl.when(e == 0)
    def _():
        fetch(0).start()

    def body(g, carry):
        @pl.when(g + 1 < n_used)
        def _():
            fetch(g + 1).start()

        fetch(g).wait()

        @pl.when(g >= 2)
        def _():
            put(g - 2).wait()

        x = _unpack_halves(xbuf[g % 2])
        gate = jnp.dot(x, wgb_ref[...], preferred_element_type=F32)
        up = jnp.dot(x, wub_ref[...], preferred_element_type=F32)
        hid = (_silu(gate) * up).astype(BF16)
        y = jnp.dot(hid, wdb_ref[...], preferred_element_type=F32)
        ybuf[g % 2] = _pack_halves(y.astype(BF16).astype(F32))
        put(g).start()
        return carry

    lax.fori_loop(first, first + count_ref[e], body, 0)

    @pl.when(e == pl.num_programs(0) - 1)
    def _():
        @pl.when(n_used >= 2)
        def _():
            put(n_used - 2).wait()

        put(n_used - 1).wait()
        ybuf[0] = jnp.zeros_like(ybuf[0])

        def clear(b, carry):
            cp = pltpu.make_async_copy(
                ybuf.at[0], y_ref.at[pl.ds(pl.multiple_of(b * MOE_ROWS, MOE_ROWS), MOE_ROWS)], semy.at[0])
            cp.start()
            cp.wait()
            return carry

        lax.fori_loop(tail_ref[0], y_ref.shape[0] // MOE_ROWS, clear, 0)


def _seg_experts(xb, seg_first, seg_count, n_valid, w_gate, w_up, w_down, layer):
    p, half = xb.shape
    d = 2 * half
    de = w_gate.shape[3]
    weight = lambda shape: pl.BlockSpec((None, None) + shape, lambda e, a, b, c: (layer, e, 0, 0))
    return pl.pallas_call(
        _seg_expert_kernel,
        grid_spec=pltpu.PrefetchScalarGridSpec(
            num_scalar_prefetch=3,
            grid=(N_EXPERTS,),
            in_specs=[pl.BlockSpec(memory_space=pl.ANY), weight((d, de)), weight((d, de)), weight((de, d))],
            out_specs=pl.BlockSpec(memory_space=pl.ANY),
            scratch_shapes=[pltpu.VMEM((2, MOE_ROWS, half), jnp.uint32),
                            pltpu.VMEM((2, MOE_ROWS, half), jnp.uint32),
                            pltpu.VMEM((d, de), BF16), pltpu.VMEM((d, de), BF16), pltpu.VMEM((de, d), BF16),
                            pltpu.SemaphoreType.DMA((2,)), pltpu.SemaphoreType.DMA((2,))],
        ),
        out_shape=jax.ShapeDtypeStruct((p, half), jnp.uint32),
        compiler_params=_params(("arbitrary",)),
        name="moe_experts",
    )(seg_first, seg_count, n_valid, xb, w_gate, w_up, w_down)


def _clear_padding_blocks(zs_ref, xb_ref, zero_ref, sem):
    zero_ref[...] = jnp.zeros_like(zero_ref)

    def clear(row):
        start = pl.multiple_of(row, MOE_ROWS)
        return pltpu.make_async_copy(zero_ref, xb_ref.at[pl.ds(start, MOE_ROWS)], sem)

    for e in range(N_EXPERTS):
        clear(zs_ref[e]).start()
    for e in range(N_EXPERTS):
        clear(zs_ref[e]).wait()

    def clear_tail(b, carry):
        clear(b * MOE_ROWS).start()
        clear(b * MOE_ROWS).wait()
        return carry

    lax.fori_loop(zs_ref[N_EXPERTS], xb_ref.shape[0] // MOE_ROWS, clear_tail, 0)


def _run_copy(src_ref, src_row, dst_ref, dst_row, sem):
    src = pl.multiple_of(src_row, SUBLANES)
    dst = pl.multiple_of(dst_row, SUBLANES)
    return pltpu.make_async_copy(src_ref.at[pl.ds(src, SUBLANES)], dst_ref.at[pl.ds(dst, SUBLANES)], sem)


def _sort_dispatch_kernel(dst_ref, nch_ref, zs_ref, h_ref, lp_ref, xb_ref, xs_ref, zero_ref, sem, *, tm, nck):
    i = pl.program_id(0)

    @pl.when(i == 0)
    def _():
        _clear_padding_blocks(zs_ref, xb_ref, zero_ref, sem.at[0])

    slot = i % 2
    last = pl.num_programs(0) - 1

    def wait_runs(buf, count):
        def wait(j, carry):
            _run_copy(xs_ref.at[buf], 0, xb_ref, 0, sem.at[buf]).wait()
            return carry

        lax.fori_loop(0, count, wait, 0)

    @pl.when(i >= 2)
    def _():
        wait_runs(slot, nch_ref[jnp.maximum(i - 2, 0)])

    lp = lp_ref[...]
    pos = lax.broadcasted_iota(jnp.int32, (xs_ref.shape[1], tm), 0)
    perm = jnp.logical_or(pos == lp[0:1, :], pos == lp[1:2, :]).astype(BF16)
    xs_ref[slot] = _pack_halves(jnp.dot(perm, h_ref[...].astype(BF16), preferred_element_type=F32))

    def start(j, carry):
        _run_copy(xs_ref.at[slot], j * SUBLANES, xb_ref, dst_ref[i * nck + j], sem.at[slot]).start()
        return carry

    lax.fori_loop(0, nch_ref[i], start, 0)

    @pl.when(i == last)
    def _():
        @pl.when(i >= 1)
        def _():
            wait_runs(1 - slot, nch_ref[jnp.maximum(i - 1, 0)])

        wait_runs(slot, nch_ref[i])


def _sort_dispatch(h, lp_rows, chunk_dst, n_chunks, zero_start, n_rows, tm, sorted_rows):
    t, d = h.shape
    nck = sorted_rows // SUBLANES
    return pl.pallas_call(
        functools.partial(_sort_dispatch_kernel, tm=tm, nck=nck),
        grid_spec=pltpu.PrefetchScalarGridSpec(
            num_scalar_prefetch=3,
            grid=(t // tm,),
            in_specs=[pl.BlockSpec((tm, d), lambda i, a, b, c: (i, 0)),
                      pl.BlockSpec((2, tm), lambda i, a, b, c: (0, i))],
            out_specs=pl.BlockSpec(memory_space=pl.ANY),
            scratch_shapes=[pltpu.VMEM((2, sorted_rows, d // 2), jnp.uint32),
                            pltpu.VMEM((MOE_ROWS, d // 2), jnp.uint32), pltpu.SemaphoreType.DMA((2,))],
        ),
        out_shape=jax.ShapeDtypeStruct((n_rows, d // 2), jnp.uint32),
        compiler_params=_params(("arbitrary",)),
        name="moe_dispatch",
    )(chunk_dst, n_chunks, zero_start, h, lp_rows)


def _sort_combine_kernel(src_ref, nch_ref, h_ref, lp_ref, gates_ref, lnw_ref, lnb_ref, yb_ref, o_ref,
                         ys_ref, sem, *, tm, nck, alpha):
    i = pl.program_id(0)
    slot = i % 2

    def fetch(tile, buf):
        def start(j, carry):
            _run_copy(yb_ref, src_ref[tile * nck + j], ys_ref.at[buf], j * SUBLANES, sem.at[buf]).start()
            return carry

        lax.fori_loop(0, nch_ref[tile], start, 0)

    @pl.when(i == 0)
    def _():
        ys_ref[...] = jnp.zeros_like(ys_ref)
        fetch(0, 0)

    @pl.when(i + 1 < pl.num_programs(0))
    def _():
        fetch(i + 1, 1 - slot)

    def wait(j, carry):
        _run_copy(yb_ref, 0, ys_ref.at[slot], 0, sem.at[slot]).wait()
        return carry

    lax.fori_loop(0, nch_ref[i], wait, 0)

    lp = lp_ref[...]
    pos = lax.broadcasted_iota(jnp.int32, (tm, ys_ref.shape[1]), 1)
    ys = _unpack_halves(ys_ref[slot])
    y1 = jnp.dot((pos == lp[:, 0:1]).astype(BF16), ys, preferred_element_type=F32)
    y2 = jnp.dot((pos == lp[:, 1:2]).astype(BF16), ys, preferred_element_type=F32)
    y = gates_ref[:, 0:1] * y1 + gates_ref[:, 1:2] * y2
    o_ref[...] = _layer_norm(alpha * h_ref[...] + y, lnw_ref[...], lnb_ref[...])


def _sort_combine(h, yb, lp_cols, gates, chunk_src, n_chunks, ln_w, ln_b, alpha, tm, sorted_rows):
    t, d = h.shape
    nck = sorted_rows // SUBLANES
    row = lambda i, a, b: (i, 0)
    const = lambda i, a, b: (0, 0)
    return pl.pallas_call(
        functools.partial(_sort_combine_kernel, tm=tm, nck=nck, alpha=alpha),
        grid_spec=pltpu.PrefetchScalarGridSpec(
            num_scalar_prefetch=2,
            grid=(t // tm,),
            in_specs=[pl.BlockSpec((tm, d), row), pl.BlockSpec((tm, 2), row), pl.BlockSpec((tm, 2), row),
                      pl.BlockSpec((1, d), const), pl.BlockSpec((1, d), const),
                      pl.BlockSpec(memory_space=pl.ANY)],
            out_specs=pl.BlockSpec((tm, d), row),
            scratch_shapes=[pltpu.VMEM((2, sorted_rows, d // 2), jnp.uint32), pltpu.SemaphoreType.DMA((2,))],
        ),
        out_shape=jax.ShapeDtypeStruct((t, d), F32),
        compiler_params=_params(("arbitrary",)),
        name="moe_combine",
    )(chunk_src, n_chunks, h, lp_cols, gates, ln_w.reshape(1, d), ln_b.reshape(1, d), yb)


def _moe(h1, info, w_gate, w_up, w_down, layer, ln_w, ln_b, alpha):
    t = h1.shape[0]
    tm = SORT_ROWS
    ntile = t // tm
    run_pad = SUBLANES - 1
    sorted_rows = 2 * tm + LANES
    nck = sorted_rows // SUBLANES
    n_rows = (2 * t + ntile * N_EXPERTS * run_pad + MOE_ROWS - 1) // MOE_ROWS * MOE_ROWS + N_EXPERTS * MOE_ROWS
    i32 = jnp.int32
    e1, e2 = info[0].astype(i32), info[1].astype(i32)
    rank1, rank2 = info[4].astype(i32), info[5].astype(i32)
    expert_ids = jnp.arange(N_EXPERTS, dtype=i32)[None, :]
    oh1 = e1[:, None] == expert_ids
    oh2 = e2[:, None] == expert_ids
    cnt = jnp.logical_or(oh1, oh2).astype(i32).reshape(ntile, tm, N_EXPERTS).sum(axis=1)
    cnt8 = (cnt + run_pad) // SUBLANES * SUBLANES
    local = jnp.cumsum(cnt8, axis=1) - cnt8
    before = jnp.cumsum(cnt, axis=0) - cnt
    seg = cnt8.sum(axis=0)
    seg_pad = (seg + MOE_ROWS - 1) // MOE_ROWS * MOE_ROWS
    ends = jnp.cumsum(seg_pad)
    slot = (ends - seg_pad)[None, :] + jnp.cumsum(cnt8, axis=0) - cnt8
    shift = jnp.repeat(local - before, tm, axis=0)
    lp1 = jnp.sum(jnp.where(oh1, shift, 0), axis=1) + rank1
    lp2 = jnp.sum(jnp.where(oh2, shift, 0), axis=1) + rank2
    chunk_row = jnp.arange(nck, dtype=i32) * SUBLANES
    run_of = jnp.sum(((local + cnt8)[:, None, :] <= chunk_row[None, :, None]).astype(i32), axis=2)
    run_of = jnp.minimum(run_of, N_EXPERTS - 1)
    chunk_slot = jnp.sum(jnp.where(run_of[:, :, None] == expert_ids[None], (slot - local)[:, None, :], 0),
                         axis=2) + chunk_row[None, :]
    n_chunks = cnt8.sum(axis=1) // SUBLANES
    n_valid = (ends[-1] // MOE_ROWS).astype(i32).reshape(1)
    zero_start = jnp.concatenate([jnp.maximum(ends - MOE_ROWS, 0), n_valid]).astype(i32)
    gates = jnp.stack([info[2], info[3]], axis=1)
    chunk_slot = chunk_slot.reshape(-1).astype(i32)
    seg_first = ((ends - seg_pad) // MOE_ROWS).astype(i32)
    seg_count = (seg_pad // MOE_ROWS).astype(i32)

    xb = _sort_dispatch(h1, jnp.stack([lp1, lp2], axis=0), chunk_slot, n_chunks, zero_start, n_rows, tm,
                        sorted_rows)
    yb = _seg_experts(xb, seg_first, seg_count, n_valid, w_gate, w_up, w_down, layer)
    return _sort_combine(h1, yb, jnp.stack([lp1, lp2], axis=1), gates, chunk_slot, n_chunks, ln_w, ln_b,
                         alpha, tm, sorted_rows)


def _rotary_column_order(w_in):
    d = w_in.shape[0]
    nq = RET_HEADS * RET_DK

    def perm(w):
        return w.reshape(d, RET_HEADS, RET_DK // 2, 2).transpose(0, 3, 1, 2).reshape(d, nq)

    return jnp.concatenate([perm(w_in[:, :nq]), perm(w_in[:, nq:2 * nq]), w_in[:, 2 * nq:]], axis=1)


def kernel(x, w_in_even, w_out_even, lru_conv_w, lru_conv_b, lru_gate_w, lru_gate_b, lru_lambda,
           w_in_odd, w_out_odd, hg_lower_bounds, hg_norm_w, ln_w, ln_b, router_w,
           moe_w_gate, moe_w_up, moe_w_down):
    batch, seq_len, d = x.shape
    depth = ln_w.shape[0]
    alpha = (2.0 * depth) ** 0.25
    t = batch * seq_len
    h = x.reshape(t, d)
    router_wt = router_w.T
    half = RET_DK // 2
    inv_freq = ROPE_BASE ** (-jnp.arange(0, RET_DK, 2, dtype=F32) / RET_DK)
    inv_freq = jnp.tile(inv_freq, LANES // half).reshape(1, LANES)
    nret = RET_HEADS * RET_DV
    tm = min(PROJ_ROWS, seq_len)
    tiles_per_seq = seq_len // tm
    cos, sin = _rotary_table(inv_freq, seq_len)
    pos_spec = pl.BlockSpec((tm, LANES), lambda i: (i % tiles_per_seq, 0))

    for layer in range(depth):
        j = layer // 2
        if layer % 2 == 0:
            w_in = _rotary_column_order(w_in_even[j]).astype(BF16)
            n_bf = 2 * RET_HEADS * RET_DK + 2 * nret
            proj_kernel = functools.partial(_proj_ret_kernel, n_bf=n_bf)
            proj_b, proj_f = _project(proj_kernel, "in_proj", h, w_in, [cos, sin], [pos_spec, pos_spec], n_bf,
                                      w_in.shape[1] - n_bf, seq_len)
            ret = _retention(proj_b, batch, seq_len)
            h_fwd, h_bwd = _rglru(proj_f, lru_conv_w[j], lru_conv_b[j], lru_gate_w[j], lru_gate_b[j],
                                  lru_lambda[j], batch, seq_len)
            w_out = w_out_even[j].astype(BF16)
            mixer_inputs, weights = (ret, h_fwd, h_bwd, proj_f), [w_out[:nret], w_out[nret:]]
        else:
            w = w_in_odd[j]
            w_in = jnp.concatenate([w[:, :2 * d], w[:, 4 * d:], w[:, 2 * d:4 * d]], axis=1).astype(BF16)
            proj_kernel = functools.partial(_proj_gla_kernel, layer=layer)
            bounds_spec = pl.BlockSpec(hg_lower_bounds.shape, lambda i: (0, 0))
            proj_b, proj_f = _project(proj_kernel, "in_proj_gla", h, w_in, [hg_lower_bounds], [bounds_spec],
                                      5 * d, 2 * d, seq_len)
            o_fwd, o_bwd = _gla(proj_b, proj_f, batch, seq_len)
            mixer_inputs, weights = (o_fwd, o_bwd, proj_b, hg_norm_w[j]), [w_out_odd[j].astype(BF16)]
        h1, info = _out_project(layer % 2 == 0, mixer_inputs, weights, h, ln_w[layer, 0], ln_b[layer, 0],
                                router_wt, alpha)
        h = _moe(h1, info, moe_w_gate, moe_w_up, moe_w_down, layer, ln_w[layer, 1], ln_b[layer, 1], alpha)
    return h.reshape(batch, seq_len, d)
```

```python
import functools
import math

import jax
import jax.numpy as jnp
from jax import lax
from jax.experimental import pallas as pl
from jax.experimental.pallas import tpu as pltpu

F32 = jnp.float32
BF16 = jnp.bfloat16

RET_HEADS = 4
RET_DK = 64
RET_DV = 128
RET_CHUNK = 128
ROPE_BASE = 10000.0
LRU_BLOCKS = 4
LRU_BW = 128
LRU_CONV = 4
LRU_C = 8.0
HG_HEADS = 8
HG_DK = 128
HG_DV = 128
N_EXPERTS = 16
N_GROUPS = 4
EXPERTS_PER_GROUP = 4
LN_EPS = 1e-5
RMS_EPS = 1e-6

LANES = 128
SUBLANES = 8
PROJ_ROWS = 512
PROJ_COLS = 512
RET_CHUNKS_PER_STEP = 4
LRU_ROWS = 512
GLA_CHUNK = 32
GLA_ROWS = 256
TOKEN_ROWS = 512
SORT_ROWS = 256
MOE_ROWS = 512
VMEM_LIMIT = 56 * 1024 * 1024

_NT = (((1,), (1,)), ((), ()))
_TN = (((0,), (0,)), ((), ()))


def _params(sem):
    return pltpu.CompilerParams(dimension_semantics=sem, vmem_limit_bytes=VMEM_LIMIT)


def _sigmoid(x):
    return 0.5 * jnp.tanh(0.5 * x) + 0.5


def _silu(x):
    return x * _sigmoid(x)


def _softplus(x):
    return jnp.maximum(x, 0.0) + jnp.log1p(jnp.exp(-jnp.abs(x)))


def _layer_norm(u, w, b):
    mu = jnp.mean(u, axis=-1, keepdims=True)
    d = u - mu
    var = jnp.mean(d * d, axis=-1, keepdims=True)
    return d * lax.rsqrt(var + LN_EPS) * w + b


def _rotary_table_kernel(inv_ref, cos_ref, sin_ref, *, tm):
    row = lax.broadcasted_iota(jnp.int32, (tm, 1), 0) + pl.program_id(0) * tm
    ang = row.astype(F32) * inv_ref[...]
    cos_ref[...] = jnp.cos(ang)
    sin_ref[...] = jnp.sin(ang)


def _rotary_table(inv_freq, seq_len):
    tm = min(PROJ_ROWS, seq_len)
    table = jax.ShapeDtypeStruct((seq_len, LANES), F32)
    return pl.pallas_call(
        functools.partial(_rotary_table_kernel, tm=tm),
        grid=(seq_len // tm,),
        in_specs=[pl.BlockSpec((1, LANES), lambda i: (0, 0))],
        out_specs=[pl.BlockSpec((tm, LANES), lambda i: (i, 0))] * 2,
        out_shape=[table, table],
        compiler_params=_params(("arbitrary",)),
        name="rotary_table",
    )(inv_freq)


def _proj_ret_kernel(x_ref, w_ref, cos_ref, sin_ref, ob_ref, of_ref, *, n_bf):
    xb = x_ref[...].astype(BF16)
    tn = PROJ_COLS
    for j in range(w_ref.shape[1] // tn):
        acc = jnp.dot(xb, w_ref[:, j * tn:(j + 1) * tn], preferred_element_type=F32)
        if j == 0:
            cos = cos_ref[...]
            sin = sin_ref[...]
            scale = RET_DK ** -0.5
            q1, q2 = acc[:, 0:128], acc[:, 128:256]
            k1, k2 = acc[:, 256:384], acc[:, 384:512]
            ob_ref[:, 0:128] = ((q1 * cos - q2 * sin) * scale).astype(BF16)
            ob_ref[:, 128:256] = ((q1 * sin + q2 * cos) * scale).astype(BF16)
            ob_ref[:, 256:384] = (k1 * cos - k2 * sin).astype(BF16)
            ob_ref[:, 384:512] = (k1 * sin + k2 * cos).astype(BF16)
        elif (j + 1) * tn <= n_bf:
            ob_ref[:, j * tn:(j + 1) * tn] = acc.astype(BF16)
        else:
            of_ref[:, j * tn - n_bf:(j + 1) * tn - n_bf] = acc


def _gla_lower_bound(p, layer):
    e = jnp.exp(p - jnp.max(p, axis=0, keepdims=True))
    sm = e / jnp.sum(e, axis=0, keepdims=True)
    lb = jnp.zeros((1, p.shape[1]), F32)
    for r in range(1, layer + 1):
        lb = lb + sm[r:r + 1, :]
    return lb


def _proj_gla_kernel(x_ref, w_ref, lbp_ref, ob_ref, of_ref, *, layer):
    xb = x_ref[...].astype(BF16)
    d = w_ref.shape[1] // 5
    tn = PROJ_COLS
    lb = _gla_lower_bound(lbp_ref[...], layer)
    mean = 0.5 * (1.0 + lb)
    spread = 0.5 * (1.0 - lb)
    q_scale = 0.5 * HG_DK ** -0.5
    for j in range(5 * d // tn):
        col = j * tn
        acc = jnp.dot(xb, w_ref[:, col:col + tn], preferred_element_type=F32)
        if col < d:
            ob_ref[:, col:col + tn] = (acc * (q_scale * jnp.tanh(0.5 * acc) + q_scale)).astype(BF16)
        elif col < 3 * d:
            ob_ref[:, col:col + tn] = acc.astype(BF16)
        else:
            zc = col - 3 * d
            swing = spread[:, zc % d:zc % d + tn] * jnp.tanh(0.5 * acc)
            of_ref[:, zc:zc + tn] = jnp.log2(mean[:, zc % d:zc % d + tn] + swing)
            ob_ref[:, col:col + tn] = (spread[:, zc % d:zc % d + tn] - swing).astype(BF16)


def _project(kernel_fn, name, x, w_bf16, sides, side_specs, n_bf, n_f32, seq_len):
    t, k = x.shape
    n = w_bf16.shape[1]
    tm = min(PROJ_ROWS, seq_len)
    return pl.pallas_call(
        kernel_fn,
        grid=(t // tm,),
        in_specs=[pl.BlockSpec((tm, k), lambda i: (i, 0)), pl.BlockSpec((k, n), lambda i: (0, 0))] + side_specs,
        out_specs=[pl.BlockSpec((tm, n_bf), lambda i: (i, 0)),
                   pl.BlockSpec((tm, n_f32), lambda i: (i, 0))],
        out_shape=[jax.ShapeDtypeStruct((t, n_bf), BF16),
                   jax.ShapeDtypeStruct((t, n_f32), F32)],
        compiler_params=_params(("arbitrary",)),
        name=name,
    )(x, w_bf16, *sides)


def _ret_log_gamma(head):
    out = jnp.full(head.shape, math.log1p(-(2.0 ** -5.0)), F32)
    for h in range(1, RET_HEADS):
        out = jnp.where(head == h, math.log1p(-(2.0 ** (-5.0 - h))), out)
    return out


def _ret_lane_head():
    lane = lax.broadcasted_iota(jnp.int32, (1, 2 * LANES), 1)
    return (lane % LANES) // (RET_DK // 2)


def _ret_state_mask():
    shape = (2 * LANES, RET_HEADS * RET_DV)
    row_head = (lax.broadcasted_iota(jnp.int32, shape, 0) % LANES) // (RET_DK // 2)
    col_head = lax.broadcasted_iota(jnp.int32, shape, 1) // RET_DV
    return row_head == col_head


def _ret_chunk_decay():
    row = lax.broadcasted_iota(jnp.int32, (2 * LANES, 1), 0)
    return jnp.exp(_ret_log_gamma((row % LANES) // (RET_DK // 2)) * float(RET_CHUNK))


def _ret_bstate_kernel(k_ref, v_ref, sb_ref, s_ref, *, cps):
    c = RET_CHUNK

    @pl.when(pl.program_id(1) == 0)
    def _():
        s_ref[...] = jnp.zeros_like(s_ref)

    lg = _ret_log_gamma(_ret_lane_head())
    idx = lax.broadcasted_iota(jnp.int32, (c, 1), 0).astype(F32)
    k_decay = jnp.exp(lg * idx)
    chunk_decay = _ret_chunk_decay()
    mask = _ret_state_mask()
    for cc in reversed(range(cps)):
        rows = slice(cc * c, (cc + 1) * c)
        sb_ref[0, cc] = s_ref[...].astype(BF16)
        kb = (k_ref[rows, :] * k_decay).astype(BF16)
        upd = lax.dot_general(kb, v_ref[rows, :], _TN, preferred_element_type=F32)
        s_ref[...] = s_ref[...] * chunk_decay + jnp.where(mask, upd, 0.0)


def _ret_out_kernel(q_ref, k_ref, v_ref, g_ref, sb_ref, o_ref, s_ref, *, cps):
    c = RET_CHUNK

    @pl.when(pl.program_id(1) == 0)
    def _():
        s_ref[...] = jnp.zeros_like(s_ref)

    lane_head = _ret_lane_head()
    lg = _ret_log_gamma(lane_head)
    idx = lax.broadcasted_iota(jnp.int32, (c, 1), 0).astype(F32)
    q_decay_f = jnp.exp(lg * (idx + 1.0))
    q_decay_b = jnp.exp(lg * (float(c) - idx))
    k_decay = jnp.exp(lg * (float(c) - 1.0 - idx))
    chunk_decay = _ret_chunk_decay()
    mask = _ret_state_mask()
    ii =lax.broadcasted_iota(jnp.int32, (c, c), 0)
    jj = lax.broadcasted_iota(jnp.int32, (c, c), 1)
    dist = jnp.abs(ii - jj).astype(F32)
    intra_decay = [jnp.exp(math.log1p(-(2.0 ** (-5.0 - h))) * dist) for h in range(RET_HEADS)]

    for cc in range(cps):
        rows = slice(cc * c, (cc + 1) * c)
        q = q_ref[rows, :]
        k = k_ref[rows, :]
        v = v_ref[rows, :]
        qf = (q * q_decay_f).astype(BF16)
        qb = (q * q_decay_b).astype(BF16)
        cross = (jnp.dot(qf, s_ref[...].astype(BF16), preferred_element_type=F32)
                 + jnp.dot(qb, sb_ref[0, cc], preferred_element_type=F32))
        q_heads = jnp.concatenate([jnp.where(lane_head == h, q, jnp.zeros_like(q)) for h in range(RET_HEADS)],
                                  axis=0)
        scores = lax.dot_general(q_heads, k, _NT, preferred_element_type=F32)
        for h in range(RET_HEADS):
            s = scores[h * c:(h + 1) * c, :] * intra_decay[h]
            cols = slice(h * RET_DV, (h + 1) * RET_DV)
            o = jnp.dot(s.astype(BF16), v[:, cols], preferred_element_type=F32) + cross[:, cols]
            mu = jnp.mean(o, axis=-1, keepdims=True)
            d = o - mu
            var = jnp.mean(d * d, axis=-1, keepdims=True)
            gate = _silu(g_ref[rows, cols].astype(F32))
            o_ref[rows, cols] = (gate * (d * lax.rsqrt(var + LN_EPS))).astype(BF16)
        kf = (k * k_decay).astype(BF16)
        upd = lax.dot_general(kf, v, _TN, preferred_element_type=F32)
        s_ref[...] = s_ref[...] * chunk_decay + jnp.where(mask, upd, 0.0)


def _retention(proj, batch, seq_len):
    t = proj.shape[0]
    c = RET_CHUNK
    cps = min(RET_CHUNKS_PER_STEP, seq_len // c)
    rows = cps * c
    ns = seq_len // rows
    dv = RET_HEADS * RET_DV
    state_shape = (2 * LANES, dv)
    rev = lambda b, n: b * ns + (ns - 1 - n)
    fwd = lambda b, n: b * ns + n
    sb = pl.pallas_call(
        functools.partial(_ret_bstate_kernel, cps=cps),
        grid=(batch, ns),
        in_specs=[
            pl.BlockSpec((rows, 2 * LANES), lambda b, n: (rev(b, n), 1)),
            pl.BlockSpec((rows, dv), lambda b, n: (rev(b, n), 1)),
        ],
        out_specs=pl.BlockSpec((1, cps) + state_shape, lambda b, n: (b, ns - 1 - n, 0, 0)),
        out_shape=jax.ShapeDtypeStruct((batch, ns * cps) + state_shape, BF16),
        scratch_shapes=[pltpu.VMEM(state_shape, F32)],
        compiler_params=_params(("arbitrary", "arbitrary")),
        name="ret_bstate",
    )(proj, proj)
    return pl.pallas_call(
        functools.partial(_ret_out_kernel, cps=cps),
        grid=(batch, ns),
        in_specs=[
            pl.BlockSpec((rows, 2 * LANES), lambda b, n: (fwd(b, n), 0)),
            pl.BlockSpec((rows, 2 * LANES), lambda b, n: (fwd(b, n), 1)),
            pl.BlockSpec((rows, dv), lambda b, n: (fwd(b, n), 1)),
            pl.BlockSpec((rows, dv), lambda b, n: (fwd(b, n), 2)),
            pl.BlockSpec((1, cps) + state_shape, lambda b, n: (b, n, 0, 0)),
        ],
        out_specs=pl.BlockSpec((rows, dv), lambda b, n: (fwd(b, n), 0)),
        out_shape=jax.ShapeDtypeStruct((t, dv), BF16),
        scratch_shapes=[pltpu.VMEM(state_shape, F32)],
        compiler_params=_params(("arbitrary", "arbitrary")),
        name="ret_out",
    )(proj, proj, proj, proj, sb)


def _lru_kernel(xfp_ref, xf_ref, xfn_ref, xbp_ref, xb_ref, xbn_ref, cw_ref, cb_ref, gw_ref, gb_ref,
                lam_ref, hf_ref, hb_ref, xx_ref, a_ref, b_ref, h_ref, *, nt, ts, batch):
    i = pl.program_id(0)
    halo = SUBLANES
    lo = LRU_CONV // 2

    @pl.when(i == 0)
    def _():
        h_ref[...] = jnp.zeros_like(h_ref)

    def prepare(xp_ref, x_ref, xn_ref, tile, z, slot):
        for b in range(batch):
            xx_ref[0:halo, :] = jnp.where(tile == 0, 0.0, xp_ref[b])
            xx_ref[halo:halo + ts, :] = x_ref[b]
            xx_ref[halo + ts:2 * halo + ts, :] = jnp.where(tile == nt - 1, 0.0, xn_ref[b])
            xx = xx_ref[...]
            xc = cb_ref[...]
            for tap in range(LRU_CONV):
                shift = (lo - tap) % (ts + 2 * halo)
                moved = xx if shift == 0 else pltpu.roll(xx, shift, axis=0)
                xc = xc + cw_ref[tap:tap + 1, :] * moved[halo:halo + ts, :]
            for n in range(LRU_BLOCKS):
                cols = slice(n * LRU_BW, (n + 1) * LRU_BW)
                xn = xc[:, cols]
                g = (jnp.dot(xn.astype(BF16), gw_ref[z, n], preferred_element_type=F32)
                     + gb_ref[z, n:n + 1, :])
                r = _sigmoid(g[:, :LRU_BW])
                ig = _sigmoid(g[:, LRU_BW:])
                a = jnp.exp((-LRU_C) * r * _softplus(-lam_ref[z, :, cols]))
                a_ref[slot + b, :, cols] = a
                b_ref[slot + b, :, cols] = jnp.sqrt(1.0 - a * a) * (ig * xn)

    prepare(xfp_ref, xf_ref, xfn_ref, i, 0, 0)
    prepare(xbp_ref, xb_ref, xbn_ref, nt - 1 - i, 1, batch)

    def step(s, hs):
        out = []
        for k in range(2 * batch):
            row = s if k < batch else ts - 1 - s
            h = a_ref[k, pl.ds(row, 1), :] * hs[k] + b_ref[k, pl.ds(row, 1), :]
            if k < batch:
                hf_ref[k, pl.ds(row, 1), :] = h
            else:
                hb_ref[k - batch, pl.ds(row, 1), :] = h
            out.append(h)
        return tuple(out)

    hs = lax.fori_loop(0, ts, step, tuple(h_ref[k] for k in range(2 * batch)), unroll=8)
    for k in range(2 * batch):
        h_ref[k] = hs[k]


def _rglru(proj_f32, conv_w, conv_b, gate_w, gate_b, lam, batch, seq_len):
    w = LRU_BLOCKS * LRU_BW
    ts = min(LRU_ROWS, seq_len)
    nt = seq_len // ts
    rows8 = ts // SUBLANES
    last8 = seq_len // SUBLANES - 1
    x3 = proj_f32.reshape(batch, seq_len, proj_f32.shape[1])
    gw = jnp.concatenate([gate_w[:, 0], gate_w[:, 1]], axis=-1).astype(BF16)
    gb = jnp.concatenate([gate_b[:, 0], gate_b[:, 1]], axis=-1)
    bwd = lambda i: nt - 1 - i

    def tile_specs(tile):
        return [
            pl.BlockSpec((batch, SUBLANES, w), lambda i: (0, jnp.maximum(tile(i) * rows8 - 1, 0), 0)),
            pl.BlockSpec((batch, ts, w), lambda i: (0, tile(i), 0)),
            pl.BlockSpec((batch, SUBLANES, w), lambda i: (0, jnp.minimum((tile(i) + 1) * rows8, last8), 0)),
        ]

    full = lambda a: pl.BlockSpec(a.shape, lambda i: (0,) * a.ndim)
    cb = conv_b.reshape(1, w)
    lam3 = lam.reshape(2, 1, w)
    state = jax.ShapeDtypeStruct((batch, seq_len, w), F32)
    h_fwd, h_bwd = pl.pallas_call(
        functools.partial(_lru_kernel, nt=nt, ts=ts, batch=batch),
        grid=(nt,),
        in_specs=tile_specs(lambda i: i) + tile_specs(bwd) + [full(conv_w), full(cb), full(gw), full(gb),
                                                               full(lam3)],
        out_specs=[pl.BlockSpec((batch, ts, w), lambda i: (0, i, 0)),
                   pl.BlockSpec((batch, ts, w), lambda i: (0, bwd(i), 0))],
        out_shape=[state, state],
        scratch_shapes=[pltpu.VMEM((ts + 2 * SUBLANES, w), F32), pltpu.VMEM((2 * batch, ts, w), F32),
                        pltpu.VMEM((2 * batch, ts, w), F32), pltpu.VMEM((2 * batch, 1, w), F32)],
        compiler_params=_params(("arbitrary",)),
        name="lru_scan",
    )(x3, x3, x3, x3, x3, x3, conv_w, cb, gw, gb, lam3)
    return h_fwd.reshape(batch * seq_len, w), h_bwd.reshape(batch * seq_len, w)


def _gla_kernel(qf_ref, vf_ref, kf_ref, lf_ref, qb_ref, vb_ref, kb_ref, lb_ref, of_ref, ob_ref, s_ref,
                *, ts, batch):
    c = GLA_CHUNK
    nchunks = ts // c

    @pl.when(pl.program_id(0) == 0)
    def _():
        s_ref[...] = jnp.zeros_like(s_ref)

    head_cols = [slice(h * HG_DK, (h + 1) * HG_DK) for h in range(HG_HEADS)]
    hc = HG_HEADS * c
    ii = lax.broadcasted_iota(jnp.int32, (c, c), 0)
    jj = lax.broadcasted_iota(jnp.int32, (c, c), 1)
    si = lax.broadcasted_iota(jnp.int32, (hc, hc), 0)
    sj = lax.broadcasted_iota(jnp.int32, (hc, hc), 1)
    same_head = si // c == sj // c

    def stack(a):
        return jnp.concatenate([a[:, cols] for cols in head_cols], axis=0)

    tri = {False: (jj <= ii).astype(BF16), True: (jj >= ii).astype(BF16)}
    tri2 = {r: jnp.concatenate([m, m], axis=1) for r, m in tri.items()}
    zero_block = jnp.zeros((c, HG_DK), BF16)
    zero_state = jnp.zeros((HG_DK, HG_DV), BF16)
    keep = {False: jnp.logical_and(same_head, sj <= si), True: jnp.logical_and(same_head, sj >= si)}
    chains = ([(qf_ref, vf_ref, kf_ref, lf_ref, of_ref, b, False) for b in range(batch)]
              + [(qb_ref, vb_ref, kb_ref, lb_ref, ob_ref, b, True) for b in range(batch)])

    def gates(cc):
        work = []
        for q_ref, v_ref, k_ref, l_ref, o_ref, b, reverse in chains:
            rows = pl.ds(((nchunks - 1 - cc) if reverse else cc) * c, c)
            log_f = l_ref[b, rows, :]
            f_hi = log_f.astype(BF16)
            f_lo = (log_f - f_hi.astype(F32)).astype(BF16)
            work.append(dict(rows=rows, reverse=reverse, o_ref=o_ref, b=b,
                             key=k_ref[b, rows, :].astype(F32), qs=q_ref[b, rows, :].astype(F32),
                             v=v_ref[b, rows, :], split=(f_hi, f_lo)))
        return work

    def cumulate(work):
        for w in work:
            f_hi, f_lo = w["split"]
            w["bcum"] = jnp.dot(tri2[w["reverse"]], jnp.concatenate([f_lo, f_hi], axis=0),
                                preferred_element_type=F32)

    half = c // 2
    row = lax.broadcasted_iota(jnp.int32, (c, 1), 0)
    first_half = row < half

    def decays(work):
        for w in work:
            bcum = w["bcum"]
            rev = w["reverse"]
            end = 0 if rev else c - 1
            b_end = bcum[end:end + 1, :]
            ref = jnp.where(first_half, bcum[half // 2:half // 2 + 1, :], bcum[half + half // 2:half + half // 2 + 1, :])
            qe = w["qs"] * jnp.exp2(bcum - ref)
            ke = w["key"] * jnp.exp2(ref - bcum)
            boundary = bcum[half:half + 1, :] if rev else bcum[half - 1:half, :]
            cross = jnp.exp2(-jnp.abs(bcum - boundary))
            queries = first_half if rev else jnp.logical_not(first_half)
            w["q3"] = [jnp.where(first_half, qe, 0.0).astype(BF16), jnp.where(first_half, 0.0, qe).astype(BF16),
                       jnp.where(queries, w["qs"] * cross, 0.0).astype(BF16)]
            w["k3"] = [jnp.where(first_half, ke, 0.0), jnp.where(first_half, 0.0, ke),
                       jnp.where(queries, 0.0, w["key"] * cross)]
            w["qd"] = (w["qs"] * jnp.exp2(bcum)).astype(BF16)
            w["kd"] = (w["key"] * jnp.exp2(b_end - bcum)).astype(BF16)
            w["decay"] = jnp.exp2(b_end)

    def stack3(parts):
        return jnp.concatenate([jnp.concatenate([p[:, cols] for p in parts], axis=1) for cols in head_cols], axis=0)

    def scores(work):
        for w in work:
            w["att"] = jnp.dot(stack3(w["q3"]), stack3(w["k3"]).T.astype(BF16), preferred_element_type=F32)

    def intra(work):
        for w in work:
            att = jnp.where(keep[w["reverse"]], w["att"], 0.0).astype(BF16)
            w["intra"] = jnp.dot(att, stack(w["v"]), preferred_element_type=F32)

    def inter(work):
        for k, w in enumerate(work):
            w["st"] = [s_ref[k, h] for h in range(HG_HEADS)]
            out = []
            for h in range(0, HG_HEADS, 2):
                sa = w["st"][h].T.astype(BF16)
                sb = w["st"][h + 1].T.astype(BF16)
                rhs = jnp.concatenate([jnp.concatenate([sa, zero_state], axis=1),
                                       jnp.concatenate([zero_state, sb], axis=1)], axis=0)
                pair = jnp.dot(w["qd"][:, h * HG_DK:(h + 2) * HG_DK], rhs, preferred_element_type=F32)
                out += [pair[:, :HG_DV], pair[:, HG_DV:]]
            w["inter"] = out

    def update(work):
        for w in work:
            upd = []
            for h in range(0, HG_HEADS, 2):
                ca, cb = head_cols[h], head_cols[h + 1]
                lhs = jnp.concatenate([w["v"][:, ca], w["v"][:, cb]], axis=0)
                rhs = jnp.concatenate([jnp.concatenate([w["kd"][:, ca], zero_block], axis=1),
                                       jnp.concatenate([zero_block, w["kd"][:, cb]], axis=1)], axis=0)
                pair = lax.dot_general(lhs, rhs, _TN, preferred_element_type=F32)
                upd += [pair[:, :HG_DK], pair[:, HG_DK:]]
            w["upd"] = upd

    def finish(work):
        for k, w in enumerate(work):
            for h, cols in enumerate(head_cols):
                w["o_ref"][w["b"], w["rows"], cols] = w["intra"][h * c:(h + 1) * c, :] + w["inter"][h]
                s_ref[k, h] = w["st"][h] * w["decay"][:, cols] + w["upd"][h]

    cur = gates(0)
    cumulate(cur)
    decays(cur)
    for cc in range(nchunks):
        more = cc + 1 < nchunks
        scores(cur)
        nxt = gates(cc + 1) if more else None
        if more:
            cumulate(nxt)
        intra(cur)
        inter(cur)
        update(cur)
        if more:
            decays(nxt)
        finish(cur)
        cur = nxt


def _gla(proj_bf16, proj_f32, batch, seq_len):
    d = HG_HEADS * HG_DK
    ts = min(GLA_ROWS, seq_len)
    nt = seq_len // ts
    pb = proj_bf16.reshape(batch, seq_len, proj_bf16.shape[1])
    pf = proj_f32.reshape(batch, seq_len, proj_f32.shape[1])
    bwd = lambda i: nt - 1 - i
    blk = lambda tile, col: pl.BlockSpec((batch, ts, d), lambda i: (0, tile(i), col))
    fwd = lambda i: i
    out = jax.ShapeDtypeStruct((batch, seq_len, d), F32)
    o_fwd, o_bwd = pl.pallas_call(
        functools.partial(_gla_kernel, ts=ts, batch=batch),
        grid=(nt,),
        in_specs=[blk(fwd, 0), blk(fwd, 1), blk(fwd, 3), blk(fwd, 0),
                  blk(bwd, 0), blk(bwd, 1), blk(bwd, 4), blk(bwd, 1)],
        out_specs=[blk(fwd, 0), blk(bwd, 0)],
        out_shape=[out, out],
        scratch_shapes=[pltpu.VMEM((2 * batch, HG_HEADS, HG_DV, HG_DK), F32)],
        compiler_params=_params(("arbitrary",)),
        name="gla_scan",
    )(pb, pb, pb, pf, pb, pb, pb, pf)
    return o_fwd.reshape(batch * seq_len, d), o_bwd.reshape(batch * seq_len, d)


def _top2(p):
    v1 = jnp.maximum(jnp.maximum(p[0], p[1]), jnp.maximum(p[2], p[3]))
    i1 = jnp.where(p[0] == v1, 0, jnp.where(p[1] == v1, 1, jnp.where(p[2] == v1, 2, 3)))
    q = [jnp.where(i1 == k, -1.0, p[k]) for k in range(4)]
    v2 = jnp.maximum(jnp.maximum(q[0], q[1]), jnp.maximum(q[2], q[3]))
    i2 = jnp.where(q[0] == v2, 0, jnp.where(q[1] == v2, 1, jnp.where(q[2] == v2, 2, 3)))
    return v1, i1, v2, i2


def _outproj_kernel(*refs, even, alpha, tm):
    if even:
        ret_ref, hf_ref, hb_ref, gr_ref, w0_ref, w1_ref = refs[:6]
        rest = refs[6:]
        lru = ((hf_ref[...] + hb_ref[...]) * jax.nn.gelu(gr_ref[...])).astype(BF16)
        y = (jnp.dot(ret_ref[...], w0_ref[...], preferred_element_type=F32)
             + jnp.dot(lru, w1_ref[...], preferred_element_type=F32))
    else:
        of_ref, ob_ref, g_ref, nw_ref, w0_ref = refs[:5]
        rest = refs[5:]
        o = of_ref[...] + ob_ref[...]
        ms = jnp.mean(o * o, axis=-1, keepdims=True)
        mix = o * lax.rsqrt(ms + RMS_EPS) * nw_ref[...] * _silu(g_ref[...].astype(F32))
        y = jnp.dot(mix.astype(BF16), w0_ref[...], preferred_element_type=F32)
    h_ref, lnw_ref, lnb_ref, rw_ref, o_ref, info_ref, carry_ref, before_ref = rest
    i = pl.program_id(0)

    @pl.when(i == 0)
    def _():
        carry_ref[...] = jnp.zeros_like(carry_ref)
        tt = lax.broadcasted_iota(jnp.int32, (tm, tm), 0)
        uu = lax.broadcasted_iota(jnp.int32, (tm, tm), 1)
        before_ref[...] = (tt < uu).astype(BF16)

    h1 = _layer_norm(alpha * h_ref[...] + y, lnw_ref[...], lnb_ref[...])
    o_ref[...] = h1

    h_hi = h1.astype(BF16)
    h_lo = (h1 - h_hi.astype(F32)).astype(BF16)
    rw = rw_ref[...]
    r_hi = rw.astype(BF16)
    r_lo = (rw - r_hi.astype(F32)).astype(BF16)
    both = lax.dot_general(jnp.concatenate([r_hi, r_lo], axis=0), h_hi, _NT, preferred_element_type=F32)
    logits = (both[N_EXPERTS:, :] + lax.dot_general(r_hi, h_lo, _NT, preferred_element_type=F32)
              + both[:N_EXPERTS, :])
    ex = jnp.exp(logits - jnp.max(logits, axis=0, keepdims=True))
    probs = ex / jnp.sum(ex, axis=0, keepdims=True)
    best = None
    for g in range(N_GROUPS):
        rows = [probs[g * EXPERTS_PER_GROUP + k:g * EXPERTS_PER_GROUP + k + 1, :]
                for k in range(EXPERTS_PER_GROUP)]
        v1, i1, v2, i2 = _top2(rows)
        cand = (v1 + v2, v1, i1 + g * EXPERTS_PER_GROUP, v2, i2 + g * EXPERTS_PER_GROUP)
        if best is None:
            best = cand
        else:
            take = cand[0] > best[0]
            best = tuple(jnp.where(take, cn, bs) for cn, bs in zip(cand, best))
    _, v1, e1, v2, e2 = best
    denom = v1 + v2
    g1 = v1 / denom
    g2 = v2 / denom

    eid = lax.broadcasted_iota(jnp.int32, (N_EXPERTS, tm), 0)
    oh1 = (eid == e1).astype(F32)
    oh2 = (eid == e2).astype(F32)
    oh = oh1 + oh2
    base = carry_ref[:, 0:1] + jnp.dot(oh.astype(BF16), before_ref[...], preferred_element_type=F32)
    rank1 = jnp.sum(oh1 * base, axis=0, keepdims=True)
    rank2 = jnp.sum(oh2 * base, axis=0, keepdims=True)
    carry_ref[...] = carry_ref[...] + jnp.sum(oh, axis=1, keepdims=True)
    zero = jnp.zeros_like(g1)
    info_ref[...] = jnp.concatenate(
        [e1.astype(F32), e2.astype(F32), g1, g2, rank1, rank2, zero, zero], axis=0)


def _out_project(even, mixer_inputs, weights_bf16, h, ln_w, ln_b, router_wt, alpha):
    t, d = h.shape
    tm = TOKEN_ROWS
    row = lambda i: (i, 0)
    const = lambda i: (0, 0)
    if even:
        ret, h_fwd, h_bwd, proj_f32 = mixer_inputs
        w = h_fwd.shape[1]
        in_specs = [pl.BlockSpec((tm, ret.shape[1]), row), pl.BlockSpec((tm, w), row),
                    pl.BlockSpec((tm, w), row), pl.BlockSpec((tm, w), lambda i: (i, 1))]
        args = [ret, h_fwd, h_bwd, proj_f32]
    else:
        o_fwd, o_bwd, proj_bf16, norm_w = mixer_inputs
        in_specs = [pl.BlockSpec((tm, d), row), pl.BlockSpec((tm, d), row),
                    pl.BlockSpec((tm, d), lambda i: (i, 2)), pl.BlockSpec((1, d), const)]
        args = [o_fwd, o_bwd, proj_bf16, norm_w.reshape(1, d)]
    in_specs += [pl.BlockSpec(wm.shape, const) for wm in weights_bf16]
    in_specs += [pl.BlockSpec((tm, d), row), pl.BlockSpec((1, d), const), pl.BlockSpec((1, d), const),
                 pl.BlockSpec((N_EXPERTS, d), const)]
    args += list(weights_bf16) + [h, ln_w.reshape(1, d), ln_b.reshape(1, d), router_wt]
    return pl.pallas_call(
        functools.partial(_outproj_kernel, even=even, alpha=alpha, tm=tm),
        grid=(t // tm,),
        in_specs=in_specs,
        out_specs=[pl.BlockSpec((tm, d), row),
                   pl.BlockSpec((SUBLANES, tm), lambda i: (0, i))],
        out_shape=[jax.ShapeDtypeStruct((t, d), F32),
                   jax.ShapeDtypeStruct((SUBLANES, t), F32)],
        scratch_shapes=[pltpu.VMEM((N_EXPERTS, LANES), F32), pltpu.VMEM((tm, tm), BF16)],
        compiler_params=_params(("arbitrary",)),
        name="out_proj_router",
    )(*args)


def _pack_halves(x):
    n = x.shape[1] // 2
    hi = pltpu.bitcast(x[:, :n], jnp.uint32)
    lo = pltpu.bitcast(x[:, n:], jnp.uint32)
    return hi | (lo >> 16)


def _unpack_halves(p):
    hi = pltpu.bitcast(p & jnp.uint32(0xFFFF0000), F32)
    lo = pltpu.bitcast(p << 16, F32)
    return jnp.concatenate([hi.astype(BF16), lo.astype(BF16)], axis=1)


def _seg_expert_kernel(first_ref, count_ref, tail_ref, x_ref, wg_ref, wu_ref, wd_ref, y_ref,
                       xbuf, ybuf, wgb_ref, wub_ref, wdb_ref, semx, semy):
    e = pl.program_id(0)
    wgb_ref[...] = wg_ref[...].astype(BF16)
    wub_ref[...] = wu_ref[...].astype(BF16)
    wdb_ref[...] = wd_ref[...].astype(BF16)
    first = first_ref[e]
    n_used = tail_ref[0]

    def rows(g):
        return pl.ds(pl.multiple_of(g * MOE_ROWS, MOE_ROWS), MOE_ROWS)

    def fetch(g):
        return pltpu.make_async_copy(x_ref.at[rows(g)], xbuf.at[g % 2], semx.at[g % 2])

    def put(g):
        return pltpu.make_async_copy(ybuf.at[g % 2], y_ref.at[rows(g)], semy.at[g % 2])

    @pl.when(e == 0)
    def _():
        fetch(0).start()

    def body(g, carry):
        @pl.when(g + 1 < n_used)
        def _():
            fetch(g + 1).start()

        fetch(g).wait()

        @pl.when(g >= 2)
        def _():
            put(g - 2).wait()

        x = _unpack_halves(xbuf[g % 2])
        gate = jnp.dot(x, wgb_ref[...], preferred_element_type=F32)
        up = jnp.dot(x, wub_ref[...], preferred_element_type=F32)
        hid = (_silu(gate) * up).astype(BF16)
        y = jnp.dot(hid, wdb_ref[...], preferred_element_type=F32)
        ybuf[g % 2] = _pack_halves(y.astype(BF16).astype(F32))
        put(g).start()
        return carry

    lax.fori_loop(first, first + count_ref[e], body, 0)

    @pl.when(e == pl.num_programs(0) - 1)
    def _():
        @pl.when(n_used >= 2)
        def _():
            put(n_used - 2).wait()

        put(n_used - 1).wait()
        ybuf[0] = jnp.zeros_like(ybuf[0])

        def clear(b, carry):
            cp = pltpu.make_async_copy(
                ybuf.at[0], y_ref.at[pl.ds(pl.multiple_of(b * MOE_ROWS, MOE_ROWS), MOE_ROWS)], semy.at[0])
            cp.start()
            cp.wait()
            return carry

        lax.fori_loop(tail_ref[0], y_ref.shape[0] // MOE_ROWS, clear, 0)


def _seg_experts(xb, seg_first, seg_count, n_valid, w_gate, w_up, w_down, layer):
    p, half = xb.shape
    d = 2 * half
    de = w_gate.shape[3]
    weight = lambda shape: pl.BlockSpec((None, None) + shape, lambda e, a, b, c: (layer, e, 0, 0))
    return pl.pallas_call(
        _seg_expert_kernel,
        grid_spec=pltpu.PrefetchScalarGridSpec(
            num_scalar_prefetch=3,
            grid=(N_EXPERTS,),
            in_specs=[pl.BlockSpec(memory_space=pl.ANY), weight((d, de)), weight((d, de)), weight((de, d))],
            out_specs=pl.BlockSpec(memory_space=pl.ANY),
            scratch_shapes=[pltpu.VMEM((2, MOE_ROWS, half), jnp.uint32),
                            pltpu.VMEM((2, MOE_ROWS, half), jnp.uint32),
                            pltpu.VMEM((d, de), BF16), pltpu.VMEM((d, de), BF16), pltpu.VMEM((de, d), BF16),
                            pltpu.SemaphoreType.DMA((2,)), pltpu.SemaphoreType.DMA((2,))],
        ),
        out_shape=jax.ShapeDtypeStruct((p, half), jnp.uint32),
        compiler_params=_params(("arbitrary",)),
        name="moe_experts",
    )(seg_first, seg_count, n_valid, xb, w_gate, w_up, w_down)


def _clear_padding_blocks(zs_ref, xb_ref, zero_ref, sem):
    zero_ref[...] = jnp.zeros_like(zero_ref)

    def clear(row):
        start = pl.multiple_of(row, MOE_ROWS)
        return pltpu.make_async_copy(zero_ref, xb_ref.at[pl.ds(start, MOE_ROWS)], sem)

    for e in range(N_EXPERTS):
        clear(zs_ref[e]).start()
    for e in range(N_EXPERTS):
        clear(zs_ref[e]).wait()

    def clear_tail(b, carry):
        clear(b * MOE_ROWS).start()
        clear(b * MOE_ROWS).wait()
        return carry

    lax.fori_loop(zs_ref[N_EXPERTS], xb_ref.shape[0] // MOE_ROWS, clear_tail, 0)


def _run_copy(src_ref, src_row, dst_ref, dst_row, sem):
    src = pl.multiple_of(src_row, SUBLANES)
    dst = pl.multiple_of(dst_row, SUBLANES)
    return pltpu.make_async_copy(src_ref.at[pl.ds(src, SUBLANES)], dst_ref.at[pl.ds(dst, SUBLANES)], sem)


def _sort_dispatch_kernel(dst_ref, nch_ref, zs_ref, h_ref, lp_ref, xb_ref, xs_ref, zero_ref, sem, *, tm, nck):
    i = pl.program_id(0)

    @pl.when(i == 0)
    def _():
        _clear_padding_blocks(zs_ref, xb_ref, zero_ref, sem.at[0])

    slot = i % 2
    last = pl.num_programs(0) - 1

    def wait_runs(buf, count):
        def wait(j, carry):
            _run_copy(xs_ref.at[buf], 0, xb_ref, 0, sem.at[buf]).wait()
            return carry

        lax.fori_loop(0, count, wait, 0)

    @pl.when(i >= 2)
    def _():
        wait_runs(slot, nch_ref[jnp.maximum(i - 2, 0)])

    lp = lp_ref[...]
    pos = lax.broadcasted_iota(jnp.int32, (xs_ref.shape[1], tm), 0)
    perm = jnp.logical_or(pos == lp[0:1, :], pos == lp[1:2, :]).astype(BF16)
    xs_ref[slot] = _pack_halves(jnp.dot(perm, h_ref[...].astype(BF16), preferred_element_type=F32))

    def start(j, carry):
        _run_copy(xs_ref.at[slot], j * SUBLANES, xb_ref, dst_ref[i * nck + j], sem.at[slot]).start()
        return carry

    lax.fori_loop(0, nch_ref[i], start, 0)

    @pl.when(i == last)
    def _():
        @pl.when(i >= 1)
        def _():
            wait_runs(1 - slot, nch_ref[jnp.maximum(i - 1, 0)])

        wait_runs(slot, nch_ref[i])


def _sort_dispatch(h, lp_rows, chunk_dst, n_chunks, zero_start, n_rows, tm, sorted_rows):
    t, d = h.shape
    nck = sorted_rows // SUBLANES
    return pl.pallas_call(
        functools.partial(_sort_dispatch_kernel, tm=tm, nck=nck),
        grid_spec=pltpu.PrefetchScalarGridSpec(
            num_scalar_prefetch=3,
            grid=(t // tm,),
            in_specs=[pl.BlockSpec((tm, d), lambda i, a, b, c: (i, 0)),
                      pl.BlockSpec((2, tm), lambda i, a, b, c: (0, i))],
            out_specs=pl.BlockSpec(memory_space=pl.ANY),
            scratch_shapes=[pltpu.VMEM((2, sorted_rows, d // 2), jnp.uint32),
                            pltpu.VMEM((MOE_ROWS, d // 2), jnp.uint32), pltpu.SemaphoreType.DMA((2,))],
        ),
        out_shape=jax.ShapeDtypeStruct((n_rows, d // 2), jnp.uint32),
        compiler_params=_params(("arbitrary",)),
        name="moe_dispatch",
    )(chunk_dst, n_chunks, zero_start, h, lp_rows)


def _sort_combine_kernel(src_ref, nch_ref, h_ref, lp_ref, gates_ref, lnw_ref, lnb_ref, yb_ref, o_ref,
                         ys_ref, sem, *, tm, nck, alpha):
    i = pl.program_id(0)
    slot = i % 2

    def fetch(tile, buf):
        def start(j, carry):
            _run_copy(yb_ref, src_ref[tile * nck + j], ys_ref.at[buf], j * SUBLANES, sem.at[buf]).start()
            return carry

        lax.fori_loop(0, nch_ref[tile], start, 0)

    @pl.when(i == 0)
    def _():
        ys_ref[...] = jnp.zeros_like(ys_ref)
        fetch(0, 0)

    @pl.when(i + 1 < pl.num_programs(0))
    def _():
        fetch(i + 1, 1 - slot)

    def wait(j, carry):
        _run_copy(yb_ref, 0, ys_ref.at[slot], 0, sem.at[slot]).wait()
        return carry

    lax.fori_loop(0, nch_ref[i], wait, 0)

    lp = lp_ref[...]
    pos = lax.broadcasted_iota(jnp.int32, (tm, ys_ref.shape[1]), 1)
    ys = _unpack_halves(ys_ref[slot])
    y1 = jnp.dot((pos == lp[:, 0:1]).astype(BF16), ys, preferred_element_type=F32)
    y2 = jnp.dot((pos == lp[:, 1:2]).astype(BF16), ys, preferred_element_type=F32)
    y = gates_ref[:, 0:1] * y1 + gates_ref[:, 1:2] * y2
    o_ref[...] = _layer_norm(alpha * h_ref[...] + y, lnw_ref[...], lnb_ref[...])


def _sort_combine(h, yb, lp_cols, gates, chunk_src, n_chunks, ln_w, ln_b, alpha, tm, sorted_rows):
    t, d = h.shape
    nck = sorted_rows // SUBLANES
    row = lambda i, a, b: (i, 0)
    const = lambda i, a, b: (0, 0)
    return pl.pallas_call(
        functools.partial(_sort_combine_kernel, tm=tm, nck=nck, alpha=alpha),
        grid_spec=pltpu.PrefetchScalarGridSpec(
            num_scalar_prefetch=2,
            grid=(t // tm,),
            in_specs=[pl.BlockSpec((tm, d), row), pl.BlockSpec((tm, 2), row), pl.BlockSpec((tm, 2), row),
                      pl.BlockSpec((1, d), const), pl.BlockSpec((1, d), const),
                      pl.BlockSpec(memory_space=pl.ANY)],
            out_specs=pl.BlockSpec((tm, d), row),
            scratch_shapes=[pltpu.VMEM((2, sorted_rows, d // 2), jnp.uint32), pltpu.SemaphoreType.DMA((2,))],
        ),
        out_shape=jax.ShapeDtypeStruct((t, d), F32),
        compiler_params=_params(("arbitrary",)),
        name="moe_combine",
    )(chunk_src, n_chunks, h, lp_cols, gates, ln_w.reshape(1, d), ln_b.reshape(1, d), yb)


def _moe(h1, info, w_gate, w_up, w_down, layer, ln_w, ln_b, alpha):
    t = h1.shape[0]
    tm = SORT_ROWS
    ntile = t // tm
    run_pad = SUBLANES - 1
    sorted_rows = 2 * tm + LANES
    nck = sorted_rows // SUBLANES
    n_rows = (2 * t + ntile * N_EXPERTS * run_pad + MOE_ROWS - 1) // MOE_ROWS * MOE_ROWS + N_EXPERTS * MOE_ROWS
    i32 = jnp.int32
    e1, e2 = info[0].astype(i32), info[1].astype(i32)
    rank1, rank2 = info[4].astype(i32), info[5].astype(i32)
    expert_ids = jnp.arange(N_EXPERTS, dtype=i32)[None, :]
    oh1 = e1[:, None] == expert_ids
    oh2 = e2[:, None] == expert_ids
    cnt = jnp.logical_or(oh1, oh2).astype(i32).reshape(ntile, tm, N_EXPERTS).sum(axis=1)
    cnt8 = (cnt + run_pad) // SUBLANES * SUBLANES
    local = jnp.cumsum(cnt8, axis=1) - cnt8
    before = jnp.cumsum(cnt, axis=0) - cnt
    seg = cnt8.sum(axis=0)
    seg_pad = (seg + MOE_ROWS - 1) // MOE_ROWS * MOE_ROWS
    ends = jnp.cumsum(seg_pad)
    slot = (ends - seg_pad)[None, :] + jnp.cumsum(cnt8, axis=0) - cnt8
    shift = jnp.repeat(local - before, tm, axis=0)
    lp1 = jnp.sum(jnp.where(oh1, shift, 0), axis=1) + rank1
    lp2 = jnp.sum(jnp.where(oh2, shift, 0), axis=1) + rank2
    chunk_row = jnp.arange(nck, dtype=i32) * SUBLANES
    run_of = jnp.sum(((local + cnt8)[:, None, :] <= chunk_row[None, :, None]).astype(i32), axis=2)
    run_of = jnp.minimum(run_of, N_EXPERTS - 1)
    chunk_slot = jnp.sum(jnp.where(run_of[:, :, None] == expert_ids[None], (slot - local)[:, None, :], 0),
                         axis=2) + chunk_row[None, :]
    n_chunks = cnt8.sum(axis=1) // SUBLANES
    n_valid = (ends[-1] // MOE_ROWS).astype(i32).reshape(1)
    zero_start = jnp.concatenate([jnp.maximum(ends - MOE_ROWS, 0), n_valid]).astype(i32)
    gates = jnp.stack([info[2], info[3]], axis=1)
    chunk_slot = chunk_slot.reshape(-1).astype(i32)
    seg_first = ((ends - seg_pad) // MOE_ROWS).astype(i32)
    seg_count = (seg_pad // MOE_ROWS).astype(i32)

    xb = _sort_dispatch(h1, jnp.stack([lp1, lp2], axis=0), chunk_slot, n_chunks, zero_start, n_rows, tm,
                        sorted_rows)
    yb = _seg_experts(xb, seg_first, seg_count, n_valid, w_gate, w_up, w_down, layer)
    return _sort_combine(h1, yb, jnp.stack([lp1, lp2], axis=1), gates, chunk_slot, n_chunks, ln_w, ln_b,
                         alpha, tm, sorted_rows)


def _rotary_column_order(w_in):
    d = w_in.shape[0]
    nq = RET_HEADS * RET_DK

    def perm(w):
        return w.reshape(d, RET_HEADS, RET_DK // 2, 2).transpose(0, 3, 1, 2).reshape(d, nq)

    return jnp.concatenate([perm(w_in[:, :nq]), perm(w_in[:, nq:2 * nq]), w_in[:, 2 * nq:]], axis=1)


def kernel(x, w_in_even, w_out_even, lru_conv_w, lru_conv_b, lru_gate_w, lru_gate_b, lru_lambda,
           w_in_odd, w_out_odd, hg_lower_bounds, hg_norm_w, ln_w, ln_b, router_w,
           moe_w_gate, moe_w_up, moe_w_down):
    batch, seq_len, d = x.shape
    depth = ln_w.shape[0]
    alpha = (2.0 * depth) ** 0.25
    t = batch * seq_len
    h = x.reshape(t, d)
    router_wt = router_w.T
    half = RET_DK // 2
    inv_freq = ROPE_BASE ** (-jnp.arange(0, RET_DK, 2, dtype=F32) / RET_DK)
    inv_freq = jnp.tile(inv_freq, LANES // half).reshape(1, LANES)
    nret = RET_HEADS * RET_DV
    tm = min(PROJ_ROWS, seq_len)
    tiles_per_seq = seq_len // tm
    cos, sin = _rotary_table(inv_freq, seq_len)
    pos_spec = pl.BlockSpec((tm, LANES), lambda i: (i % tiles_per_seq, 0))

    for layer in range(depth):
        j = layer // 2
        if layer % 2 == 0:
            w_in = _rotary_column_order(w_in_even[j]).astype(BF16)
            n_bf = 2 * RET_HEADS * RET_DK + 2 * nret
            proj_kernel = functools.partial(_proj_ret_kernel, n_bf=n_bf)
            proj_b, proj_f = _project(proj_kernel, "in_proj", h, w_in, [cos, sin], [pos_spec, pos_spec], n_bf,
                                      w_in.shape[1] - n_bf, seq_len)
            ret = _retention(proj_b, batch, seq_len)
            h_fwd, h_bwd = _rglru(proj_f, lru_conv_w[j], lru_conv_b[j], lru_gate_w[j], lru_gate_b[j],
                                  lru_lambda[j], batch, seq_len)
            w_out = w_out_even[j].astype(BF16)
            mixer_inputs, weights = (ret, h_fwd, h_bwd, proj_f), [w_out[:nret], w_out[nret:]]
        else:
            w = w_in_odd[j]
            w_in = jnp.concatenate([w[:, :2 * d], w[:, 4 * d:], w[:, 2 * d:4 * d]], axis=1).astype(BF16)
            proj_kernel = functools.partial(_proj_gla_kernel, layer=layer)
            bounds_spec = pl.BlockSpec(hg_lower_bounds.shape, lambda i: (0, 0))
            proj_b, proj_f = _project(proj_kernel, "in_proj_gla", h, w_in, [hg_lower_bounds], [bounds_spec],
                                      5 * d, 2 * d, seq_len)
            o_fwd, o_bwd = _gla(proj_b, proj_f, batch, seq_len)
            mixer_inputs, weights = (o_fwd, o_bwd, proj_b, hg_norm_w[j]), [w_out_odd[j].astype(BF16)]
        h1, info = _out_project(layer % 2 == 0, mixer_inputs, weights, h, ln_w[layer, 0], ln_b[layer, 0],
                                router_wt, alpha)
        h = _moe(h1, info, moe_w_gate, moe_w_up, moe_w_down, layer, ln_w[layer, 1], ln_b[layer, 1], alpha)
    return h.reshape(batch, seq_len, d)
```

```python
import functools
import math

import jax
import jax.numpy as jnp
from jax import lax
from jax.experimental import pallas as pl
from jax.experimental.pallas import tpu as pltpu

F32 = jnp.float32
BF16 = jnp.bfloat16

RET_HEADS = 4
RET_DK = 64
RET_DV = 128
RET_CHUNK = 128
ROPE_BASE = 10000.0
LRU_BLOCKS = 4
LRU_BW = 128
LRU_CONV = 4
LRU_C = 8.0
HG_HEADS = 8
HG_DK = 128
HG_DV = 128
N_EXPERTS = 16
N_GROUPS = 4
EXPERTS_PER_GROUP = 4
LN_EPS = 1e-5
RMS_EPS = 1e-6

LANES = 128
SUBLANES = 8
PROJ_ROWS = 512
PROJ_COLS = 512
RET_CHUNKS_PER_STEP = 4
LRU_ROWS = 512
GLA_CHUNK = 32
GLA_ROWS = 256
TOKEN_ROWS = 512
SORT_ROWS = 256
MOE_ROWS = 512
VMEM_LIMIT = 56 * 1024 * 1024

_NT = (((1,), (1,)), ((), ()))
_TN = (((0,), (0,)), ((), ()))


def _params(sem):
    return pltpu.CompilerParams(dimension_semantics=sem, vmem_limit_bytes=VMEM_LIMIT)


def _sigmoid(x):
    return 0.5 * jnp.tanh(0.5 * x) + 0.5


def _silu(x):
    return x * _sigmoid(x)


def _softplus(x):
    return jnp.maximum(x, 0.0) + jnp.log1p(jnp.exp(-jnp.abs(x)))


def _layer_norm(u, w, b):
    mu = jnp.mean(u, axis=-1, keepdims=True)
    d = u - mu
    var = jnp.mean(d * d, axis=-1, keepdims=True)
    return d * lax.rsqrt(var + LN_EPS) * w + b


def _rotary_table_kernel(inv_ref, cos_ref, sin_ref, *, tm):
    row = lax.broadcasted_iota(jnp.int32, (tm, 1), 0) + pl.program_id(0) * tm
    ang = row.astype(F32) * inv_ref[...]
    cos_ref[...] = jnp.cos(ang)
    sin_ref[...] = jnp.sin(ang)


def _rotary_table(inv_freq, seq_len):
    tm = min(PROJ_ROWS, seq_len)
    table = jax.ShapeDtypeStruct((seq_len, LANES), F32)
    return pl.pallas_call(
        functools.partial(_rotary_table_kernel, tm=tm),
        grid=(seq_len // tm,),
        in_specs=[pl.BlockSpec((1, LANES), lambda i: (0, 0))],
        out_specs=[pl.BlockSpec((tm, LANES), lambda i: (i, 0))] * 2,
        out_shape=[table, table],
        compiler_params=_params(("arbitrary",)),
        name="rotary_table",
    )(inv_freq)


def _proj_ret_kernel(x_ref, w_ref, cos_ref, sin_ref, ob_ref, of_ref, *, n_bf):
    xb = x_ref[...].astype(BF16)
    tn = PROJ_COLS
    for j in range(w_ref.shape[1] // tn):
        acc = jnp.dot(xb, w_ref[:, j * tn:(j + 1) * tn], preferred_element_type=F32)
        if j * tn < 4 * LANES:
            cos = cos_ref[...]
            sin = sin_ref[...]
            for base in range(0, tn, 2 * LANES):
                col = j * tn + base
                scale = RET_DK ** -0.5 if col < 2 * LANES else 1.0
                x1, x2 = acc[:, base:base + LANES], acc[:, base + LANES:base + 2 * LANES]
                ob_ref[:, col:col + LANES] = ((x1 * cos - x2 * sin) * scale).astype(BF16)
                ob_ref[:, col + LANES:col + 2 * LANES] = ((x1 * sin + x2 * cos) * scale).astype(BF16)
        elif (j + 1) * tn <= n_bf:
            ob_ref[:, j * tn:(j + 1) * tn] = acc.astype(BF16)
        else:
            of_ref[:, j * tn - n_bf:(j + 1) * tn - n_bf] = acc


def _gla_lower_bound(p, layer):
    e = jnp.exp(p - jnp.max(p, axis=0, keepdims=True))
    sm = e / jnp.sum(e, axis=0, keepdims=True)
    lb = jnp.zeros((1, p.shape[1]), F32)
    for r in range(1, layer + 1):
        lb = lb + sm[r:r + 1, :]
    return lb


def _proj_gla_kernel(x_ref, w_ref, lbp_ref, ob_ref, of_ref, *, layer):
    xb = x_ref[...].astype(BF16)
    d = w_ref.shape[1] // 5
    tn = PROJ_COLS
    lb = _gla_lower_bound(lbp_ref[...], layer)
    mean = 0.5 * (1.0 + lb)
    spread = 0.5 * (1.0 - lb)
    q_scale = 0.5 * HG_DK ** -0.5
    for j in range(5 * d // tn):
        col = j * tn
        acc = jnp.dot(xb, w_ref[:, col:col + tn], preferred_element_type=F32)
        if col < d:
            ob_ref[:, col:col + tn] = (acc * (q_scale * jnp.tanh(0.5 * acc) + q_scale)).astype(BF16)
        elif col < 3 * d:
            ob_ref[:, col:col + tn] = acc.astype(BF16)
        else:
            zc = col - 3 * d
            swing = spread[:, zc % d:zc % d + tn] * jnp.tanh(0.5 * acc)
            of_ref[:, zc:zc + tn] = jnp.log2(mean[:, zc % d:zc % d + tn] + swing)
            ob_ref[:, col:col + tn] = (spread[:, zc % d:zc % d + tn] - swing).astype(BF16)


def _project(kernel_fn, name, x, w_bf16, sides, side_specs, n_bf, n_f32, seq_len):
    t, k = x.shape
    n = w_bf16.shape[1]
    tm = min(PROJ_ROWS, seq_len)
    return pl.pallas_call(
        kernel_fn,
        grid=(t // tm,),
        in_specs=[pl.BlockSpec((tm, k), lambda i: (i, 0)), pl.BlockSpec((k, n), lambda i: (0, 0))] + side_specs,
        out_specs=[pl.BlockSpec((tm, n_bf), lambda i: (i, 0)),
                   pl.BlockSpec((tm, n_f32), lambda i: (i, 0))],
        out_shape=[jax.ShapeDtypeStruct((t, n_bf), BF16),
                   jax.ShapeDtypeStruct((t, n_f32), F32)],
        compiler_params=_params(("arbitrary",)),
        name=name,
    )(x, w_bf16, *sides)


def _ret_log_gamma(head):
    out = jnp.full(head.shape, math.log1p(-(2.0 ** -5.0)), F32)
    for h in range(1, RET_HEADS):
        out = jnp.where(head == h, math.log1p(-(2.0 ** (-5.0 - h))), out)
    return out


def _ret_lane_head():
    lane = lax.broadcasted_iota(jnp.int32, (1, 2 * LANES), 1)
    return (lane % LANES) // (RET_DK // 2)


def _ret_state_mask():
    shape = (2 * LANES, RET_HEADS * RET_DV)
    row_head = (lax.broadcasted_iota(jnp.int32, shape, 0) % LANES) // (RET_DK // 2)
    col_head = lax.broadcasted_iota(jnp.int32, shape, 1) // RET_DV
    return row_head == col_head


def _ret_chunk_decay():
    row = lax.broadcasted_iota(jnp.int32, (2 * LANES, 1), 0)
    return jnp.exp(_ret_log_gamma((row % LANES) // (RET_DK // 2)) * float(RET_CHUNK))


def _ret_bstate_kernel(k_ref, v_ref, sb_ref, s_ref, *, cps):
    c = RET_CHUNK

    @pl.when(pl.program_id(1) == 0)
    def _():
        s_ref[...] = jnp.zeros_like(s_ref)

    lg = _ret_log_gamma(_ret_lane_head())
    idx = lax.broadcasted_iota(jnp.int32, (c, 1), 0).astype(F32)
    k_decay = jnp.exp(lg * idx)
    chunk_decay = _ret_chunk_decay()
    mask = _ret_state_mask()
    for cc in reversed(range(cps)):
        rows = slice(cc * c, (cc + 1) * c)
        sb_ref[0, cc] = s_ref[...].astype(BF16)
        kb = (k_ref[rows, :] * k_decay).astype(BF16)
        upd = lax.dot_general(kb, v_ref[rows, :], _TN, preferred_element_type=F32)
        s_ref[...] = s_ref[...] * chunk_decay + jnp.where(mask, upd, 0.0)


def _ret_out_kernel(q_ref, k_ref, v_ref, g_ref, sb_ref, o_ref, s_ref, *, cps):
    c = RET_CHUNK

    @pl.when(pl.program_id(1) == 0)
    def _():
        s_ref[...] = jnp.zeros_like(s_ref)

    lane_head = _ret_lane_head()
    lg = _ret_log_gamma(lane_head)
    idx = lax.broadcasted_iota(jnp.int32, (c, 1), 0).astype(F32)
    q_decay_f = jnp.exp(lg * (idx + 1.0))
    q_decay_b = jnp.exp(lg * (float(c) - idx))
    k_decay = jnp.exp(lg * (float(c) - 1.0 - idx))
    chunk_decay = _ret_chunk_decay()
    mask = _ret_state_mask()
    ii =lax.broadcasted_iota(jnp.int32, (c, c), 0)
    jj = lax.broadcasted_iota(jnp.int32, (c, c), 1)
    dist = jnp.abs(ii - jj).astype(F32)
    intra_decay = [jnp.exp(math.log1p(-(2.0 ** (-5.0 - h))) * dist) for h in range(RET_HEADS)]

    for cc in range(cps):
        rows = slice(cc * c, (cc + 1) * c)
        q = q_ref[rows, :]
        k = k_ref[rows, :]
        v = v_ref[rows, :]
        qf = (q * q_decay_f).astype(BF16)
        qb = (q * q_decay_b).astype(BF16)
        cross = (jnp.dot(qf, s_ref[...].astype(BF16), preferred_element_type=F32)
                 + jnp.dot(qb, sb_ref[0, cc], preferred_element_type=F32))
        q_heads = jnp.concatenate([jnp.where(lane_head == h, q, jnp.zeros_like(q)) for h in range(RET_HEADS)],
                                  axis=0)
        scores = lax.dot_general(q_heads, k, _NT, preferred_element_type=F32)
        for h in range(RET_HEADS):
            s = scores[h * c:(h + 1) * c, :] * intra_decay[h]
            cols = slice(h * RET_DV, (h + 1) * RET_DV)
            o = jnp.dot(s.astype(BF16), v[:, cols], preferred_element_type=F32) + cross[:, cols]
            mu = jnp.mean(o, axis=-1, keepdims=True)
            d = o - mu
            var = jnp.mean(d * d, axis=-1, keepdims=True)
            gate = _silu(g_ref[rows, cols].astype(F32))
            o_ref[rows, cols] = (gate * (d * lax.rsqrt(var + LN_EPS))).astype(BF16)
        kf = (k * k_decay).astype(BF16)
        upd = lax.dot_general(kf, v, _TN, preferred_element_type=F32)
        s_ref[...] = s_ref[...] * chunk_decay + jnp.where(mask, upd, 0.0)


def _retention(proj, batch, seq_len):
    t = proj.shape[0]
    c = RET_CHUNK
    cps = min(RET_CHUNKS_PER_STEP, seq_len // c)
    rows = cps * c
    ns = seq_len // rows
    dv = RET_HEADS * RET_DV
    state_shape = (2 * LANES, dv)
    rev = lambda b, n: b * ns + (ns - 1 - n)
    fwd = lambda b, n: b * ns + n
    sb = pl.pallas_call(
        functools.partial(_ret_bstate_kernel, cps=cps),
        grid=(batch, ns),
        in_specs=[
            pl.BlockSpec((rows, 2 * LANES), lambda b, n: (rev(b, n), 1)),
            pl.BlockSpec((rows, dv), lambda b, n: (rev(b, n), 1)),
        ],
        out_specs=pl.BlockSpec((1, cps) + state_shape, lambda b, n: (b, ns - 1 - n, 0, 0)),
        out_shape=jax.ShapeDtypeStruct((batch, ns * cps) + state_shape, BF16),
        scratch_shapes=[pltpu.VMEM(state_shape, F32)],
        compiler_params=_params(("arbitrary", "arbitrary")),
        name="ret_bstate",
    )(proj, proj)
    return pl.pallas_call(
        functools.partial(_ret_out_kernel, cps=cps),
        grid=(batch, ns),
        in_specs=[
            pl.BlockSpec((rows, 2 * LANES), lambda b, n: (fwd(b, n), 0)),
            pl.BlockSpec((rows, 2 * LANES), lambda b, n: (fwd(b, n), 1)),
            pl.BlockSpec((rows, dv), lambda b, n: (fwd(b, n), 1)),
            pl.BlockSpec((rows, dv), lambda b, n: (fwd(b, n), 2)),
            pl.BlockSpec((1, cps) + state_shape, lambda b, n: (b, n, 0, 0)),
        ],
        out_specs=pl.BlockSpec((rows, dv), lambda b, n: (fwd(b, n), 0)),
        out_shape=jax.ShapeDtypeStruct((t, dv), BF16),
        scratch_shapes=[pltpu.VMEM(state_shape, F32)],
        compiler_params=_params(("arbitrary", "arbitrary")),
        name="ret_out",
    )(proj, proj, proj, proj, sb)


def _lru_kernel(xfp_ref, xf_ref, xfn_ref, xbp_ref, xb_ref, xbn_ref, cw_ref, cb_ref, gw_ref, gb_ref,
                lam_ref, hf_ref, hb_ref, xx_ref, a_ref, b_ref, h_ref, *, nt, ts, batch):
    i = pl.program_id(0)
    halo = SUBLANES
    lo = LRU_CONV // 2

    @pl.when(i == 0)
    def _():
        h_ref[...] = jnp.zeros_like(h_ref)

    def prepare(xp_ref, x_ref, xn_ref, tile, z, slot):
        for b in range(batch):
            xx_ref[0:halo, :] = jnp.where(tile == 0, 0.0, xp_ref[b])
            xx_ref[halo:halo + ts, :] = x_ref[b]
            xx_ref[halo + ts:2 * halo + ts, :] = jnp.where(tile == nt - 1, 0.0, xn_ref[b])
            xx = xx_ref[...]
            xc = cb_ref[...]
            for tap in range(LRU_CONV):
                shift = (lo - tap) % (ts + 2 * halo)
                moved = xx if shift == 0 else pltpu.roll(xx, shift, axis=0)
                xc = xc + cw_ref[tap:tap + 1, :] * moved[halo:halo + ts, :]
            for n in range(LRU_BLOCKS):
                cols = slice(n * LRU_BW, (n + 1) * LRU_BW)
                xn = xc[:, cols]
                g = (jnp.dot(xn.astype(BF16), gw_ref[z, n], preferred_element_type=F32)
                     + gb_ref[z, n:n + 1, :])
                r = _sigmoid(g[:, :LRU_BW])
                ig = _sigmoid(g[:, LRU_BW:])
                a = jnp.exp((-LRU_C) * r * _softplus(-lam_ref[z, :, cols]))
                a_ref[slot + b, :, cols] = a
                b_ref[slot + b, :, cols] = jnp.sqrt(1.0 - a * a) * (ig * xn)

    prepare(xfp_ref, xf_ref, xfn_ref, i, 0, 0)
    prepare(xbp_ref, xb_ref, xbn_ref, nt - 1 - i, 1, batch)

    def step(s, hs):
        out = []
        for k in range(2 * batch):
            row = s if k < batch else ts - 1 - s
            h = a_ref[k, pl.ds(row, 1), :] * hs[k] + b_ref[k, pl.ds(row, 1), :]
            if k < batch:
                hf_ref[k, pl.ds(row, 1), :] = h
            else:
                hb_ref[k - batch, pl.ds(row, 1), :] = h
            out.append(h)
        return tuple(out)

    hs = lax.fori_loop(0, ts, step, tuple(h_ref[k] for k in range(2 * batch)), unroll=8)
    for k in range(2 * batch):
        h_ref[k] = hs[k]


def _rglru(proj_f32, conv_w, conv_b, gate_w, gate_b, lam, batch, seq_len):
    w = LRU_BLOCKS * LRU_BW
    ts = min(LRU_ROWS, seq_len)
    nt = seq_len // ts
    rows8 = ts // SUBLANES
    last8 = seq_len // SUBLANES - 1
    x3 = proj_f32.reshape(batch, seq_len, proj_f32.shape[1])
    gw = jnp.concatenate([gate_w[:, 0], gate_w[:, 1]], axis=-1).astype(BF16)
    gb = jnp.concatenate([gate_b[:, 0], gate_b[:, 1]], axis=-1)
    bwd = lambda i: nt - 1 - i

    def tile_specs(tile):
        return [
            pl.BlockSpec((batch, SUBLANES, w), lambda i: (0, jnp.maximum(tile(i) * rows8 - 1, 0), 0)),
            pl.BlockSpec((batch, ts, w), lambda i: (0, tile(i), 0)),
            pl.BlockSpec((batch, SUBLANES, w), lambda i: (0, jnp.minimum((tile(i) + 1) * rows8, last8), 0)),
        ]

    full = lambda a: pl.BlockSpec(a.shape, lambda i: (0,) * a.ndim)
    cb = conv_b.reshape(1, w)
    lam3 = lam.reshape(2, 1, w)
    state = jax.ShapeDtypeStruct((batch, seq_len, w), F32)
    h_fwd, h_bwd = pl.pallas_call(
        functools.partial(_lru_kernel, nt=nt, ts=ts, batch=batch),
        grid=(nt,),
        in_specs=tile_specs(lambda i: i) + tile_specs(bwd) + [full(conv_w), full(cb), full(gw), full(gb),
                                                               full(lam3)],
        out_specs=[pl.BlockSpec((batch, ts, w), lambda i: (0, i, 0)),
                   pl.BlockSpec((batch, ts, w), lambda i: (0, bwd(i), 0))],
        out_shape=[state, state],
        scratch_shapes=[pltpu.VMEM((ts + 2 * SUBLANES, w), F32), pltpu.VMEM((2 * batch, ts, w), F32),
                        pltpu.VMEM((2 * batch, ts, w), F32), pltpu.VMEM((2 * batch, 1, w), F32)],
        compiler_params=_params(("arbitrary",)),
        name="lru_scan",
    )(x3, x3, x3, x3, x3, x3, conv_w, cb, gw, gb, lam3)
    return h_fwd.reshape(batch * seq_len, w), h_bwd.reshape(batch * seq_len, w)


def _gla_kernel(qf_ref, vf_ref, kf_ref, lf_ref, qb_ref, vb_ref, kb_ref, lb_ref, of_ref, ob_ref, s_ref,
                *, ts, batch):
    c = GLA_CHUNK
    nchunks = ts // c

    @pl.when(pl.program_id(0) == 0)
    def _():
        s_ref[...] = jnp.zeros_like(s_ref)

    head_cols = [slice(h * HG_DK, (h + 1) * HG_DK) for h in range(HG_HEADS)]
    hc = HG_HEADS * c
    ii = lax.broadcasted_iota(jnp.int32, (c, c), 0)
    jj = lax.broadcasted_iota(jnp.int32, (c, c), 1)
    si = lax.broadcasted_iota(jnp.int32, (hc, hc), 0)
    sj = lax.broadcasted_iota(jnp.int32, (hc, hc), 1)
    same_head = si // c == sj // c

    def stack(a):
        return jnp.concatenate([a[:, cols] for cols in head_cols], axis=0)

    tri = {False: (jj <= ii).astype(BF16), True: (jj >= ii).astype(BF16)}
    tri2 = {r: jnp.concatenate([m, m], axis=1) for r, m in tri.items()}
    zero_block = jnp.zeros((c, HG_DK), BF16)
    zero_state = jnp.zeros((HG_DK, HG_DV), BF16)
    keep = {False: jnp.logical_and(same_head, sj <= si), True: jnp.logical_and(same_head, sj >= si)}
    chains = ([(qf_ref, vf_ref, kf_ref, lf_ref, of_ref, b, False) for b in range(batch)]
              + [(qb_ref, vb_ref, kb_ref, lb_ref, ob_ref, b, True) for b in range(batch)])

    def gates(cc):
        work = []
        for q_ref, v_ref, k_ref, l_ref, o_ref, b, reverse in chains:
            rows = pl.ds(((nchunks - 1 - cc) if reverse else cc) * c, c)
            log_f = l_ref[b, rows, :]
            f_hi = log_f.astype(BF16)
            f_lo = (log_f - f_hi.astype(F32)).astype(BF16)
            work.append(dict(rows=rows, reverse=reverse, o_ref=o_ref, b=b,
                             key=k_ref[b, rows, :].astype(F32), qs=q_ref[b, rows, :].astype(F32),
                             v=v_ref[b, rows, :], split=(f_hi, f_lo)))
        return work

    def cumulate(work):
        for w in work:
            f_hi, f_lo = w["split"]
            w["bcum"] = jnp.dot(tri2[w["reverse"]], jnp.concatenate([f_lo, f_hi], axis=0),
                                preferred_element_type=F32)

    half = c // 2
    row = lax.broadcasted_iota(jnp.int32, (c, 1), 0)
    first_half = row < half

    def decays(work):
        for w in work:
            bcum = w["bcum"]
            rev = w["reverse"]
            end = 0 if rev else c - 1
            b_end = bcum[end:end + 1, :]
            ref = jnp.where(first_half, bcum[half // 2:half // 2 + 1, :], bcum[half + half // 2:half + half // 2 + 1, :])
            qe = w["qs"] * jnp.exp2(bcum - ref)
            ke = w["key"] * jnp.exp2(ref - bcum)
            boundary = bcum[half:half + 1, :] if rev else bcum[half - 1:half, :]
            cross = jnp.exp2(-jnp.abs(bcum - boundary))
            queries = first_half if rev else jnp.logical_not(first_half)
            w["q3"] = [jnp.where(first_half, qe, 0.0).astype(BF16), jnp.where(first_half, 0.0, qe).astype(BF16),
                       jnp.where(queries, w["qs"] * cross, 0.0).astype(BF16)]
            w["k3"] = [jnp.where(first_half, ke, 0.0), jnp.where(first_half, 0.0, ke),
                       jnp.where(queries, 0.0, w["key"] * cross)]
            w["qd"] = (w["qs"] * jnp.exp2(bcum)).astype(BF16)
            w["kd"] = (w["key"] * jnp.exp2(b_end - bcum)).astype(BF16)
            w["decay"] = jnp.exp2(b_end)

    def stack3(parts):
        return jnp.concatenate([jnp.concatenate([p[:, cols] for p in parts], axis=1) for cols in head_cols], axis=0)

    def scores(work):
        for w in work:
            w["att"] = jnp.dot(stack3(w["q3"]), stack3(w["k3"]).T.astype(BF16), preferred_element_type=F32)

    def intra(work):
        for w in work:
            att = jnp.where(keep[w["reverse"]], w["att"], 0.0).astype(BF16)
            w["intra"] = jnp.dot(att, stack(w["v"]), preferred_element_type=F32)

    def inter(work):
        for k, w in enumerate(work):
            w["st"] = [s_ref[k, h] for h in range(HG_HEADS)]
            out = []
            for h in range(0, HG_HEADS, 2):
                sa = w["st"][h].T.astype(BF16)
                sb = w["st"][h + 1].T.astype(BF16)
                rhs = jnp.concatenate([jnp.concatenate([sa, zero_state], axis=1),
                                       jnp.concatenate([zero_state, sb], axis=1)], axis=0)
                pair = jnp.dot(w["qd"][:, h * HG_DK:(h + 2) * HG_DK], rhs, preferred_element_type=F32)
                out += [pair[:, :HG_DV], pair[:, HG_DV:]]
            w["inter"] = out

    def update(work):
        for w in work:
            upd = []
            for h in range(0, HG_HEADS, 2):
                ca, cb = head_cols[h], head_cols[h + 1]
                lhs = jnp.concatenate([w["v"][:, ca], w["v"][:, cb]], axis=0)
                rhs = jnp.concatenate([jnp.concatenate([w["kd"][:, ca], zero_block], axis=1),
                                       jnp.concatenate([zero_block, w["kd"][:, cb]], axis=1)], axis=0)
                pair = lax.dot_general(lhs, rhs, _TN, preferred_element_type=F32)
                upd += [pair[:, :HG_DK], pair[:, HG_DK:]]
            w["upd"] = upd

    def finish(work):
        for k, w in enumerate(work):
            for h, cols in enumerate(head_cols):
                w["o_ref"][w["b"], w["rows"], cols] = w["intra"][h * c:(h + 1) * c, :] + w["inter"][h]
                s_ref[k, h] = w["st"][h] * w["decay"][:, cols] + w["upd"][h]

    cur = gates(0)
    cumulate(cur)
    decays(cur)
    for cc in range(nchunks):
        more = cc + 1 < nchunks
        scores(cur)
        nxt = gates(cc + 1) if more else None
        if more:
            cumulate(nxt)
        intra(cur)
        inter(cur)
        update(cur)
        if more:
            decays(nxt)
        finish(cur)
        cur = nxt


def _gla(proj_bf16, proj_f32, batch, seq_len):
    d = HG_HEADS * HG_DK
    ts = min(GLA_ROWS, seq_len)
    nt = seq_len // ts
    pb = proj_bf16.reshape(batch, seq_len, proj_bf16.shape[1])
    pf = proj_f32.reshape(batch, seq_len, proj_f32.shape[1])
    bwd = lambda i: nt - 1 - i
    blk = lambda tile, col: pl.BlockSpec((batch, ts, d), lambda i: (0, tile(i), col))
    fwd = lambda i: i
    out = jax.ShapeDtypeStruct((batch, seq_len, d), F32)
    o_fwd, o_bwd = pl.pallas_call(
        functools.partial(_gla_kernel, ts=ts, batch=batch),
        grid=(nt,),
        in_specs=[blk(fwd, 0), blk(fwd, 1), blk(fwd, 3), blk(fwd, 0),
                  blk(bwd, 0), blk(bwd, 1), blk(bwd, 4), blk(bwd, 1)],
        out_specs=[blk(fwd, 0), blk(bwd, 0)],
        out_shape=[out, out],
        scratch_shapes=[pltpu.VMEM((2 * batch, HG_HEADS, HG_DV, HG_DK), F32)],
        compiler_params=_params(("arbitrary",)),
        name="gla_scan",
    )(pb, pb, pb, pf, pb, pb, pb, pf)
    return o_fwd.reshape(batch * seq_len, d), o_bwd.reshape(batch * seq_len, d)


def _top2(p):
    v1 = jnp.maximum(jnp.maximum(p[0], p[1]), jnp.maximum(p[2], p[3]))
    i1 = jnp.where(p[0] == v1, 0, jnp.where(p[1] == v1, 1, jnp.where(p[2] == v1, 2, 3)))
    q = [jnp.where(i1 == k, -1.0, p[k]) for k in range(4)]
    v2 = jnp.maximum(jnp.maximum(q[0], q[1]), jnp.maximum(q[2], q[3]))
    i2 = jnp.where(q[0] == v2, 0, jnp.where(q[1] == v2, 1, jnp.where(q[2] == v2, 2, 3)))
    return v1, i1, v2, i2


def _outproj_kernel(*refs, even, alpha, tm):
    if even:
        ret_ref, hf_ref, hb_ref, gr_ref, w0_ref, w1_ref = refs[:6]
        rest = refs[6:]
        lru = ((hf_ref[...] + hb_ref[...]) * jax.nn.gelu(gr_ref[...])).astype(BF16)
        y = (jnp.dot(ret_ref[...], w0_ref[...], preferred_element_type=F32)
             + jnp.dot(lru, w1_ref[...], preferred_element_type=F32))
    else:
        of_ref, ob_ref, g_ref, nw_ref, w0_ref = refs[:5]
        rest = refs[5:]
        o = of_ref[...] + ob_ref[...]
        ms = jnp.mean(o * o, axis=-1, keepdims=True)
        mix = o * lax.rsqrt(ms + RMS_EPS) * nw_ref[...] * _silu(g_ref[...].astype(F32))
        y = jnp.dot(mix.astype(BF16), w0_ref[...], preferred_element_type=F32)
    h_ref, lnw_ref, lnb_ref, rw_ref, o_ref, info_ref, carry_ref, before_ref = rest
    i = pl.program_id(0)

    @pl.when(i == 0)
    def _():
        carry_ref[...] = jnp.zeros_like(carry_ref)
        tt = lax.broadcasted_iota(jnp.int32, (tm, tm), 0)
        uu = lax.broadcasted_iota(jnp.int32, (tm, tm), 1)
        before_ref[...] = (tt < uu).astype(BF16)

    h1 = _layer_norm(alpha * h_ref[...] + y, lnw_ref[...], lnb_ref[...])
    o_ref[...] = h1

    h_hi = h1.astype(BF16)
    h_lo = (h1 - h_hi.astype(F32)).astype(BF16)
    rw = rw_ref[...]
    r_hi = rw.astype(BF16)
    r_lo = (rw - r_hi.astype(F32)).astype(BF16)
    both = lax.dot_general(jnp.concatenate([r_hi, r_lo], axis=0), h_hi, _NT, preferred_element_type=F32)
    logits = (both[N_EXPERTS:, :] + lax.dot_general(r_hi, h_lo, _NT, preferred_element_type=F32)
              + both[:N_EXPERTS, :])
    ex = jnp.exp(logits - jnp.max(logits, axis=0, keepdims=True))
    probs = ex / jnp.sum(ex, axis=0, keepdims=True)
    best = None
    for g in range(N_GROUPS):
        rows = [probs[g * EXPERTS_PER_GROUP + k:g * EXPERTS_PER_GROUP + k + 1, :]
                for k in range(EXPERTS_PER_GROUP)]
        v1, i1, v2, i2 = _top2(rows)
        cand = (v1 + v2, v1, i1 + g * EXPERTS_PER_GROUP, v2, i2 + g * EXPERTS_PER_GROUP)
        if best is None:
            best = cand
        else:
            take = cand[0] > best[0]
            best = tuple(jnp.where(take, cn, bs) for cn, bs in zip(cand, best))
    _, v1, e1, v2, e2 = best
    denom = v1 + v2
    g1 = v1 / denom
    g2 = v2 / denom

    eid = lax.broadcasted_iota(jnp.int32, (N_EXPERTS, tm), 0)
    oh1 = (eid == e1).astype(F32)
    oh2 = (eid == e2).astype(F32)
    oh = oh1 + oh2
    base = carry_ref[:, 0:1] + jnp.dot(oh.astype(BF16), before_ref[...], preferred_element_type=F32)
    rank1 = jnp.sum(oh1 * base, axis=0, keepdims=True)
    rank2 = jnp.sum(oh2 * base, axis=0, keepdims=True)
    carry_ref[...] = carry_ref[...] + jnp.sum(oh, axis=1, keepdims=True)
    zero = jnp.zeros_like(g1)
    info_ref[...] = jnp.concatenate(
        [e1.astype(F32), e2.astype(F32), g1, g2, rank1, rank2, zero, zero], axis=0)


def _out_project(even, mixer_inputs, weights_bf16, h, ln_w, ln_b, router_wt, alpha):
    t, d = h.shape
    tm = TOKEN_ROWS
    row = lambda i: (i, 0)
    const = lambda i: (0, 0)
    if even:
        ret, h_fwd, h_bwd, proj_f32 = mixer_inputs
        w = h_fwd.shape[1]
        in_specs = [pl.BlockSpec((tm, ret.shape[1]), row), pl.BlockSpec((tm, w), row),
                    pl.BlockSpec((tm, w), row), pl.BlockSpec((tm, w), lambda i: (i, 1))]
        args = [ret, h_fwd, h_bwd, proj_f32]
    else:
        o_fwd, o_bwd, proj_bf16, norm_w = mixer_inputs
        in_specs = [pl.BlockSpec((tm, d), row), pl.BlockSpec((tm, d), row),
                    pl.BlockSpec((tm, d), lambda i: (i, 2)), pl.BlockSpec((1, d), const)]
        args = [o_fwd, o_bwd, proj_bf16, norm_w.reshape(1, d)]
    in_specs += [pl.BlockSpec(wm.shape, const) for wm in weights_bf16]
    in_specs += [pl.BlockSpec((tm, d), row), pl.BlockSpec((1, d), const), pl.BlockSpec((1, d), const),
                 pl.BlockSpec((N_EXPERTS, d), const)]
    args += list(weights_bf16) + [h, ln_w.reshape(1, d), ln_b.reshape(1, d), router_wt]
    return pl.pallas_call(
        functools.partial(_outproj_kernel, even=even, alpha=alpha, tm=tm),
        grid=(t // tm,),
        in_specs=in_specs,
        out_specs=[pl.BlockSpec((tm, d), row),
                   pl.BlockSpec((SUBLANES, tm), lambda i: (0, i))],
        out_shape=[jax.ShapeDtypeStruct((t, d), F32),
                   jax.ShapeDtypeStruct((SUBLANES, t), F32)],
        scratch_shapes=[pltpu.VMEM((N_EXPERTS, LANES), F32), pltpu.VMEM((tm, tm), BF16)],
        compiler_params=_params(("arbitrary",)),
        name="out_proj_router",
    )(*args)


def _pack_halves(x):
    n = x.shape[1] // 2
    hi = pltpu.bitcast(x[:, :n], jnp.uint32)
    lo = pltpu.bitcast(x[:, n:], jnp.uint32)
    return hi | (lo >> 16)


def _unpack_halves(p):
    hi = pltpu.bitcast(p & jnp.uint32(0xFFFF0000), F32)
    lo = pltpu.bitcast(p << 16, F32)
    return jnp.concatenate([hi.astype(BF16), lo.astype(BF16)], axis=1)


def _seg_expert_kernel(first_ref, count_ref, tail_ref, x_ref, wg_ref, wu_ref, wd_ref, y_ref,
                       xbuf, ybuf, wgb_ref, wub_ref, wdb_ref, semx, semy):
    e = pl.program_id(0)
    wgb_ref[...] = wg_ref[...].astype(BF16)
    wub_ref[...] = wu_ref[...].astype(BF16)
    wdb_ref[...] = wd_ref[...].astype(BF16)
    first = first_ref[e]
    n_used = tail_ref[0]

    def rows(g):
        return pl.ds(pl.multiple_of(g * MOE_ROWS, MOE_ROWS), MOE_ROWS)

    def fetch(g):
        return pltpu.make_async_copy(x_ref.at[rows(g)], xbuf.at[g % 2], semx.at[g % 2])

    def put(g):
        return pltpu.make_async_copy(ybuf.at[g % 2], y_ref.at[rows(g)], semy.at[g % 2])

    @pl.when(e == 0)
    def _():
        fetch(0).start()

    def body(g, carry):
        @pl.when(g + 1 < n_used)
        def _():
            fetch(g + 1).start()

        fetch(g).wait()

        @pl.when(g >= 2)
        def _():
            put(g - 2).wait()

        x = _unpack_halves(xbuf[g % 2])
        gate = jnp.dot(x, wgb_ref[...], preferred_element_type=F32)
        up = jnp.dot(x, wub_ref[...], preferred_element_type=F32)
        hid = (_silu(gate) * up).astype(BF16)
        y = jnp.dot(hid, wdb_ref[...], preferred_element_type=F32)
        ybuf[g % 2] = _pack_halves(y.astype(BF16).astype(F32))
        put(g).start()
        return carry

    lax.fori_loop(first, first + count_ref[e], body, 0)

    @pl.when(e == pl.num_programs(0) - 1)
    def _():
        @pl.when(n_used >= 2)
        def _():
            put(n_used - 2).wait()

        put(n_used - 1).wait()
        ybuf[0] = jnp.zeros_like(ybuf[0])

        def clear(b, carry):
            cp = pltpu.make_async_copy(
                ybuf.at[0], y_ref.at[pl.ds(pl.multiple_of(b * MOE_ROWS, MOE_ROWS), MOE_ROWS)], semy.at[0])
            cp.start()
            cp.wait()
            return carry

        lax.fori_loop(tail_ref[0], y_ref.shape[0] // MOE_ROWS, clear, 0)


def _seg_experts(xb, seg_first, seg_count, n_valid, w_gate, w_up, w_down, layer):
    p, half = xb.shape
    d = 2 * half
    de = w_gate.shape[3]
    weight = lambda shape: pl.BlockSpec((None, None) + shape, lambda e, a, b, c: (layer, e, 0, 0))
    return pl.pallas_call(
        _seg_expert_kernel,
        grid_spec=pltpu.PrefetchScalarGridSpec(
            num_scalar_prefetch=3,
            grid=(N_EXPERTS,),
            in_specs=[pl.BlockSpec(memory_space=pl.ANY), weight((d, de)), weight((d, de)), weight((de, d))],
            out_specs=pl.BlockSpec(memory_space=pl.ANY),
            scratch_shapes=[pltpu.VMEM((2, MOE_ROWS, half), jnp.uint32),
                            pltpu.VMEM((2, MOE_ROWS, half), jnp.uint32),
                            pltpu.VMEM((d, de), BF16), pltpu.VMEM((d, de), BF16), pltpu.VMEM((de, d), BF16),
                            pltpu.SemaphoreType.DMA((2,)), pltpu.SemaphoreType.DMA((2,))],
        ),
        out_shape=jax.ShapeDtypeStruct((p, half), jnp.uint32),
        compiler_params=_params(("arbitrary",)),
        name="moe_experts",
    )(seg_first, seg_count, n_valid, xb, w_gate, w_up, w_down)


def _clear_padding_blocks(zs_ref, xb_ref, zero_ref, sem):
    zero_ref[...] = jnp.zeros_like(zero_ref)

    def clear(row):
        start = pl.multiple_of(row, MOE_ROWS)
        return pltpu.make_async_copy(zero_ref, xb_ref.at[pl.ds(start, MOE_ROWS)], sem)

    for e in range(N_EXPERTS):
        clear(zs_ref[e]).start()
    for e in range(N_EXPERTS):
        clear(zs_ref[e]).wait()

    def clear_tail(b, carry):
        clear(b * MOE_ROWS).start()
        clear(b * MOE_ROWS).wait()
        return carry

    lax.fori_loop(zs_ref[N_EXPERTS], xb_ref.shape[0] // MOE_ROWS, clear_tail, 0)


def _run_copy(src_ref, src_row, dst_ref, dst_row, sem):
    src = pl.multiple_of(src_row, SUBLANES)
    dst = pl.multiple_of(dst_row, SUBLANES)
    return pltpu.make_async_copy(src_ref.at[pl.ds(src, SUBLANES)], dst_ref.at[pl.ds(dst, SUBLANES)], sem)


def _sort_dispatch_kernel(dst_ref, nch_ref, zs_ref, h_ref, lp_ref, xb_ref, xs_ref, zero_ref, sem, *, tm, nck):
    i = pl.program_id(0)

    @pl.when(i == 0)
    def _():
        _clear_padding_blocks(zs_ref, xb_ref, zero_ref, sem.at[0])

    slot = i % 2
    last = pl.num_programs(0) - 1

    def wait_runs(buf, count):
        def wait(j, carry):
            _run_copy(xs_ref.at[buf], 0, xb_ref, 0, sem.at[buf]).wait()
            return carry

        lax.fori_loop(0, count, wait, 0)

    @pl.when(i >= 2)
    def _():
        wait_runs(slot, nch_ref[jnp.maximum(i - 2, 0)])

    lp = lp_ref[...]
    pos = lax.broadcasted_iota(jnp.int32, (xs_ref.shape[1], tm), 0)
    perm = jnp.logical_or(pos == lp[0:1, :], pos == lp[1:2, :]).astype(BF16)
    xs_ref[slot] = _pack_halves(jnp.dot(perm, h_ref[...].astype(BF16), preferred_element_type=F32))

    def start(j, carry):
        _run_copy(xs_ref.at[slot], j * SUBLANES, xb_ref, dst_ref[i * nck + j], sem.at[slot]).start()
        return carry

    lax.fori_loop(0, nch_ref[i], start, 0)

    @pl.when(i == last)
    def _():
        @pl.when(i >= 1)
        def _():
            wait_runs(1 - slot, nch_ref[jnp.maximum(i - 1, 0)])

        wait_runs(slot, nch_ref[i])


def _sort_dispatch(h, lp_rows, chunk_dst, n_chunks, zero_start, n_rows, tm, sorted_rows):
    t, d = h.shape
    nck = sorted_rows // SUBLANES
    return pl.pallas_call(
        functools.partial(_sort_dispatch_kernel, tm=tm, nck=nck),
        grid_spec=pltpu.PrefetchScalarGridSpec(
            num_scalar_prefetch=3,
            grid=(t // tm,),
            in_specs=[pl.BlockSpec((tm, d), lambda i, a, b, c: (i, 0)),
                      pl.BlockSpec((2, tm), lambda i, a, b, c: (0, i))],
            out_specs=pl.BlockSpec(memory_space=pl.ANY),
            scratch_shapes=[pltpu.VMEM((2, sorted_rows, d // 2), jnp.uint32),
                            pltpu.VMEM((MOE_ROWS, d // 2), jnp.uint32), pltpu.SemaphoreType.DMA((2,))],
        ),
        out_shape=jax.ShapeDtypeStruct((n_rows, d // 2), jnp.uint32),
        compiler_params=_params(("arbitrary",)),
        name="moe_dispatch",
    )(chunk_dst, n_chunks, zero_start, h, lp_rows)


def _sort_combine_kernel(src_ref, nch_ref, h_ref, lp_ref, gates_ref, lnw_ref, lnb_ref, yb_ref, o_ref,
                         ys_ref, sem, *, tm, nck, alpha):
    i = pl.program_id(0)
    slot = i % 2

    def fetch(tile, buf):
        def start(j, carry):
            _run_copy(yb_ref, src_ref[tile * nck + j], ys_ref.at[buf], j * SUBLANES, sem.at[buf]).start()
            return carry

        lax.fori_loop(0, nch_ref[tile], start, 0)

    @pl.when(i == 0)
    def _():
        ys_ref[...] = jnp.zeros_like(ys_ref)
        fetch(0, 0)

    @pl.when(i + 1 < pl.num_programs(0))
    def _():
        fetch(i + 1, 1 - slot)

    def wait(j, carry):
        _run_copy(yb_ref, 0, ys_ref.at[slot], 0, sem.at[slot]).wait()
        return carry

    lax.fori_loop(0, nch_ref[i], wait, 0)

    lp = lp_ref[...]
    pos = lax.broadcasted_iota(jnp.int32, (tm, ys_ref.shape[1]), 1)
    pick = (jnp.where(pos == lp[:, 0:1], gates_ref[:, 0:1], 0.0)
            + jnp.where(pos == lp[:, 1:2], gates_ref[:, 1:2], 0.0))
    y = jnp.dot(pick.astype(BF16), _unpack_halves(ys_ref[slot]), preferred_element_type=F32)
    o_ref[...] = _layer_norm(alpha * h_ref[...] + y, lnw_ref[...], lnb_ref[...])


def _sort_combine(h, yb, lp_cols, gates, chunk_src, n_chunks, ln_w, ln_b, alpha, tm, sorted_rows):
    t, d = h.shape
    nck = sorted_rows // SUBLANES
    row = lambda i, a, b: (i, 0)
    const = lambda i, a, b: (0, 0)
    return pl.pallas_call(
        functools.partial(_sort_combine_kernel, tm=tm, nck=nck, alpha=alpha),
        grid_spec=pltpu.PrefetchScalarGridSpec(
            num_scalar_prefetch=2,
            grid=(t // tm,),
            in_specs=[pl.BlockSpec((tm, d), row), pl.BlockSpec((tm, 2), row), pl.BlockSpec((tm, 2), row),
                      pl.BlockSpec((1, d), const), pl.BlockSpec((1, d), const),
                      pl.BlockSpec(memory_space=pl.ANY)],
            out_specs=pl.BlockSpec((tm, d), row),
            scratch_shapes=[pltpu.VMEM((2, sorted_rows, d // 2), jnp.uint32), pltpu.SemaphoreType.DMA((2,))],
        ),
        out_shape=jax.ShapeDtypeStruct((t, d), F32),
        compiler_params=_params(("arbitrary",)),
        name="moe_combine",
    )(chunk_src, n_chunks, h, lp_cols, gates, ln_w.reshape(1, d), ln_b.reshape(1, d), yb)


def _moe(h1, info, w_gate, w_up, w_down, layer, ln_w, ln_b, alpha):
    t = h1.shape[0]
    tm = SORT_ROWS
    ntile = t // tm
    run_pad = SUBLANES - 1
    sorted_rows = 2 * tm + LANES
    nck = sorted_rows // SUBLANES
    n_rows = (2 * t + ntile * N_EXPERTS * run_pad + MOE_ROWS - 1) // MOE_ROWS * MOE_ROWS + N_EXPERTS * MOE_ROWS
    i32 = jnp.int32
    e1, e2 = info[0].astype(i32), info[1].astype(i32)
    rank1, rank2 = info[4].astype(i32), info[5].astype(i32)
    expert_ids = jnp.arange(N_EXPERTS, dtype=i32)[None, :]
    oh1 = e1[:, None] == expert_ids
    oh2 = e2[:, None] == expert_ids
    cnt = jnp.logical_or(oh1, oh2).astype(i32).reshape(ntile, tm, N_EXPERTS).sum(axis=1)
    cnt8 = (cnt + run_pad) // SUBLANES * SUBLANES
    local = jnp.cumsum(cnt8, axis=1) - cnt8
    before = jnp.cumsum(cnt, axis=0) - cnt
    seg = cnt8.sum(axis=0)
    seg_pad = (seg + MOE_ROWS - 1) // MOE_ROWS * MOE_ROWS
    ends = jnp.cumsum(seg_pad)
    slot = (ends - seg_pad)[None, :] + jnp.cumsum(cnt8, axis=0) - cnt8
    shift = jnp.repeat(local - before, tm, axis=0)
    lp1 = jnp.sum(jnp.where(oh1, shift, 0), axis=1) + rank1
    lp2 = jnp.sum(jnp.where(oh2, shift, 0), axis=1) + rank2
    chunk_row = jnp.arange(nck, dtype=i32) * SUBLANES
    run_of = jnp.sum(((local + cnt8)[:, None, :] <= chunk_row[None, :, None]).astype(i32), axis=2)
    run_of = jnp.minimum(run_of, N_EXPERTS - 1)
    chunk_slot = jnp.sum(jnp.where(run_of[:, :, None] == expert_ids[None], (slot - local)[:, None, :], 0),
                         axis=2) + chunk_row[None, :]
    n_chunks = cnt8.sum(axis=1) // SUBLANES
    n_valid = (ends[-1] // MOE_ROWS).astype(i32).reshape(1)
    zero_start = jnp.concatenate([jnp.maximum(ends - MOE_ROWS, 0), n_valid]).astype(i32)
    gates = jnp.stack([info[2], info[3]], axis=1)
    chunk_slot = chunk_slot.reshape(-1).astype(i32)
    seg_first = ((ends - seg_pad) // MOE_ROWS).astype(i32)
    seg_count = (seg_pad // MOE_ROWS).astype(i32)

    xb = _sort_dispatch(h1, jnp.stack([lp1, lp2], axis=0), chunk_slot, n_chunks, zero_start, n_rows, tm,
                        sorted_rows)
    yb = _seg_experts(xb, seg_first, seg_count, n_valid, w_gate, w_up, w_down, layer)
    return _sort_combine(h1, yb, jnp.stack([lp1, lp2], axis=1), gates, chunk_slot, n_chunks, ln_w, ln_b,
                         alpha, tm, sorted_rows)


def _rotary_column_order(w_in):
    d = w_in.shape[0]
    nq = RET_HEADS * RET_DK

    def perm(w):
        return w.reshape(d, RET_HEADS, RET_DK // 2, 2).transpose(0, 3, 1, 2).reshape(d, nq)

    return jnp.concatenate([perm(w_in[:, :nq]), perm(w_in[:, nq:2 * nq]), w_in[:, 2 * nq:]], axis=1)


def kernel(x, w_in_even, w_out_even, lru_conv_w, lru_conv_b, lru_gate_w, lru_gate_b, lru_lambda,
           w_in_odd, w_out_odd, hg_lower_bounds, hg_norm_w, ln_w, ln_b, router_w,
           moe_w_gate, moe_w_up, moe_w_down):
    batch, seq_len, d = x.shape
    depth = ln_w.shape[0]
    alpha = (2.0 * depth) ** 0.25
    t = batch * seq_len
    h = x.reshape(t, d)
    router_wt = router_w.T
    half = RET_DK // 2
    inv_freq = ROPE_BASE ** (-jnp.arange(0, RET_DK, 2, dtype=F32) / RET_DK)
    inv_freq = jnp.tile(inv_freq, LANES // half).reshape(1, LANES)
    nret = RET_HEADS * RET_DV
    tm = min(PROJ_ROWS, seq_len)
    tiles_per_seq = seq_len // tm
    cos, sin = _rotary_table(inv_freq, seq_len)
    pos_spec = pl.BlockSpec((tm, LANES), lambda i: (i % tiles_per_seq, 0))

    for layer in range(depth):
        j = layer // 2
        if layer % 2 == 0:
            w_in = _rotary_column_order(w_in_even[j]).astype(BF16)
            n_bf = 2 * RET_HEADS * RET_DK + 2 * nret
            proj_kernel = functools.partial(_proj_ret_kernel, n_bf=n_bf)
            proj_b, proj_f = _project(proj_kernel, "in_proj", h, w_in, [cos, sin], [pos_spec, pos_spec], n_bf,
                                      w_in.shape[1] - n_bf, seq_len)
            ret = _retention(proj_b, batch, seq_len)
            h_fwd, h_bwd = _rglru(proj_f, lru_conv_w[j], lru_conv_b[j], lru_gate_w[j], lru_gate_b[j],
                                  lru_lambda[j], batch, seq_len)
            w_out = w_out_even[j].astype(BF16)
            mixer_inputs, weights = (ret, h_fwd, h_bwd, proj_f), [w_out[:nret], w_out[nret:]]
        else:
            w = w_in_odd[j]
            w_in = jnp.concatenate([w[:, :2 * d], w[:, 4 * d:], w[:, 2 * d:4 * d]], axis=1).astype(BF16)
            proj_kernel = functools.partial(_proj_gla_kernel, layer=layer)
            bounds_spec = pl.BlockSpec(hg_lower_bounds.shape, lambda i: (0, 0))
            proj_b, proj_f = _project(proj_kernel, "in_proj_gla", h, w_in, [hg_lower_bounds], [bounds_spec],
                                      5 * d, 2 * d, seq_len)
            o_fwd, o_bwd = _gla(proj_b, proj_f, batch, seq_len)
            mixer_inputs, weights = (o_fwd, o_bwd, proj_b, hg_norm_w[j]), [w_out_odd[j].astype(BF16)]
        h1, info = _out_project(layer % 2 == 0, mixer_inputs, weights, h, ln_w[layer, 0], ln_b[layer, 0],
                                router_wt, alpha)
        h = _moe(h1, info, moe_w_gate, moe_w_up, moe_w_down, layer, ln_w[layer, 1], ln_b[layer, 1], alpha)
    return h.reshape(batch, seq_len, d)
```

```python
import functools
import math

import jax
import jax.numpy as jnp
from jax import lax
from jax.experimental import pallas as pl
from jax.experimental.pallas import tpu as pltpu

F32 = jnp.float32
BF16 = jnp.bfloat16

RET_HEADS = 4
RET_DK = 64
RET_DV = 128
RET_CHUNK = 128
ROPE_BASE = 10000.0
LRU_BLOCKS = 4
LRU_BW = 128
LRU_CONV = 4
LRU_C = 8.0
HG_HEADS = 8
HG_DK = 128
HG_DV = 128
N_EXPERTS = 16
N_GROUPS = 4
EXPERTS_PER_GROUP = 4
LN_EPS = 1e-5
RMS_EPS = 1e-6

LANES = 128
SUBLANES = 8
PROJ_ROWS = 512
PROJ_COLS = 512
RET_CHUNKS_PER_STEP = 8
LRU_ROWS = 512
GLA_CHUNK = 32
GLA_ROWS = 256
TOKEN_ROWS = 512
SORT_ROWS = 256
SORT_TILES_PER_STEP = 2
MOE_ROWS = 512
VMEM_LIMIT = 56 * 1024 * 1024

_NT = (((1,), (1,)), ((), ()))
_TN = (((0,), (0,)), ((), ()))


def _params(sem):
    return pltpu.CompilerParams(dimension_semantics=sem, vmem_limit_bytes=VMEM_LIMIT)


def _sigmoid(x):
    return 0.5 * jnp.tanh(0.5 * x) + 0.5


def _silu(x):
    return x * _sigmoid(x)


def _softplus(x):
    return jnp.maximum(x, 0.0) + jnp.log1p(jnp.exp(-jnp.abs(x)))


def _layer_norm(u, w, b):
    mu = jnp.mean(u, axis=-1, keepdims=True)
    d = u - mu
    var = jnp.mean(d * d, axis=-1, keepdims=True)
    return d * lax.rsqrt(var + LN_EPS) * w + b


def _rotary_table_kernel(inv_ref, cos_ref, sin_ref, *, tm):
    row = lax.broadcasted_iota(jnp.int32, (tm, 1), 0) + pl.program_id(0) * tm
    ang = row.astype(F32) * inv_ref[...]
    cos_ref[...] = jnp.cos(ang)
    sin_ref[...] = jnp.sin(ang)


def _rotary_table(inv_freq, seq_len):
    tm = min(PROJ_ROWS, seq_len)
    table = jax.ShapeDtypeStruct((seq_len, LANES), F32)
    return pl.pallas_call(
        functools.partial(_rotary_table_kernel, tm=tm),
        grid=(seq_len // tm,),
        in_specs=[pl.BlockSpec((1, LANES), lambda i: (0, 0))],
        out_specs=[pl.BlockSpec((tm, LANES), lambda i: (i, 0))] * 2,
        out_shape=[table, table],
        compiler_params=_params(("arbitrary",)),
        name="rotary_table",
    )(inv_freq)


def _proj_ret_kernel(x_ref, w_ref, cos_ref, sin_ref, ob_ref, of_ref, *, n_bf):
    xb = x_ref[...].astype(BF16)
    tn = PROJ_COLS
    for j in range(w_ref.shape[1] // tn):
        acc = jnp.dot(xb, w_ref[:, j * tn:(j + 1) * tn], preferred_element_type=F32)
        if j * tn < 4 * LANES:
            cos = cos_ref[...]
            sin = sin_ref[...]
            for base in range(0, tn, 2 * LANES):
                col = j * tn + base
                scale = RET_DK ** -0.5 if col < 2 * LANES else 1.0
                x1, x2 = acc[:, base:base + LANES], acc[:, base + LANES:base + 2 * LANES]
                ob_ref[:, col:col + LANES] = ((x1 * cos - x2 * sin) * scale).astype(BF16)
                ob_ref[:, col + LANES:col + 2 * LANES] = ((x1 * sin + x2 * cos) * scale).astype(BF16)
        elif (j + 1) * tn <= n_bf:
            ob_ref[:, j * tn:(j + 1) * tn] = acc.astype(BF16)
        else:
            of_ref[:, j * tn - n_bf:(j + 1) * tn - n_bf] = acc


def _gla_lower_bound(p, layer):
    e = jnp.exp(p - jnp.max(p, axis=0, keepdims=True))
    sm = e / jnp.sum(e, axis=0, keepdims=True)
    lb = jnp.zeros((1, p.shape[1]), F32)
    for r in range(1, layer + 1):
        lb = lb + sm[r:r + 1, :]
    return lb


def _proj_gla_kernel(x_ref, w_ref, lbp_ref, ob_ref, of_ref, *, layer):
    xb = x_ref[...].astype(BF16)
    d = w_ref.shape[1] // 5
    tn = PROJ_COLS
    lb = _gla_lower_bound(lbp_ref[...], layer)
    mean = 0.5 * (1.0 + lb)
    spread = 0.5 * (1.0 - lb)
    q_scale = 0.5 * HG_DK ** -0.5
    for j in range(5 * d // tn):
        col = j * tn
        acc = jnp.dot(xb, w_ref[:, col:col + tn], preferred_element_type=F32)
        if col < d:
            ob_ref[:, col:col + tn] = (acc * (q_scale * jnp.tanh(0.5 * acc) + q_scale)).astype(BF16)
        elif col < 3 * d:
            ob_ref[:, col:col + tn] = acc.astype(BF16)
        else:
            zc = col - 3 * d
            swing = spread[:, zc % d:zc % d + tn] * jnp.tanh(0.5 * acc)
            of_ref[:, zc:zc + tn] = jnp.log2(mean[:, zc % d:zc % d + tn] + swing)
            ob_ref[:, col:col + tn] = (spread[:, zc % d:zc % d + tn] - swing).astype(BF16)


def _project(kernel_fn, name, x, w_bf16, sides, side_specs, n_bf, n_f32, seq_len):
    t, k = x.shape
    n = w_bf16.shape[1]
    tm = min(PROJ_ROWS, seq_len)
    return pl.pallas_call(
        kernel_fn,
        grid=(t // tm,),
        in_specs=[pl.BlockSpec((tm, k), lambda i: (i, 0)), pl.BlockSpec((k, n), lambda i: (0, 0))] + side_specs,
        out_specs=[pl.BlockSpec((tm, n_bf), lambda i: (i, 0)),
                   pl.BlockSpec((tm, n_f32), lambda i: (i, 0))],
        out_shape=[jax.ShapeDtypeStruct((t, n_bf), BF16),
                   jax.ShapeDtypeStruct((t, n_f32), F32)],
        compiler_params=_params(("arbitrary",)),
        name=name,
    )(x, w_bf16, *sides)


def _ret_log_gamma(head):
    out = jnp.full(head.shape, math.log1p(-(2.0 ** -5.0)), F32)
    for h in range(1, RET_HEADS):
        out = jnp.where(head == h, math.log1p(-(2.0 ** (-5.0 - h))), out)
    return out


def _ret_lane_head():
    lane = lax.broadcasted_iota(jnp.int32, (1, 2 * LANES), 1)
    return (lane % LANES) // (RET_DK // 2)


def _ret_state_mask():
    shape = (2 * LANES, RET_HEADS * RET_DV)
    row_head = (lax.broadcasted_iota(jnp.int32, shape, 0) % LANES) // (RET_DK // 2)
    col_head = lax.broadcasted_iota(jnp.int32, shape, 1) // RET_DV
    return row_head == col_head


def _ret_chunk_decay():
    row = lax.broadcasted_iota(jnp.int32, (2 * LANES, 1), 0)
    return jnp.exp(_ret_log_gamma((row % LANES) // (RET_DK // 2)) * float(RET_CHUNK))


def _ret_bstate_kernel(k_ref, v_ref, sb_ref, s_ref, *, cps):
    c = RET_CHUNK

    @pl.when(pl.program_id(1) == 0)
    def _():
        s_ref[...] = jnp.zeros_like(s_ref)

    lg = _ret_log_gamma(_ret_lane_head())
    idx = lax.broadcasted_iota(jnp.int32, (c, 1), 0).astype(F32)
    k_decay = jnp.exp(lg * idx)
    chunk_decay = _ret_chunk_decay()
    mask = _ret_state_mask()
    for cc in reversed(range(cps)):
        rows = slice(cc * c, (cc + 1) * c)
        sb_ref[0, cc] = s_ref[...].astype(BF16)
        kb = (k_ref[rows, :] * k_decay).astype(BF16)
        upd = lax.dot_general(kb, v_ref[rows, :], _TN, preferred_element_type=F32)
        s_ref[...] = s_ref[...] * chunk_decay + jnp.where(mask, upd, 0.0)


def _ret_out_kernel(q_ref, k_ref, v_ref, g_ref, sb_ref, o_ref, s_ref, *, cps):
    c = RET_CHUNK

    @pl.when(pl.program_id(1) == 0)
    def _():
        s_ref[...] = jnp.zeros_like(s_ref)

    lane_head = _ret_lane_head()
    lg = _ret_log_gamma(lane_head)
    idx = lax.broadcasted_iota(jnp.int32, (c, 1), 0).astype(F32)
    q_decay_f = jnp.exp(lg * (idx + 1.0))
    q_decay_b = jnp.exp(lg * (float(c) - idx))
    k_decay = jnp.exp(lg * (float(c) - 1.0 - idx))
    chunk_decay = _ret_chunk_decay()
    mask = _ret_state_mask()
    ii =lax.broadcasted_iota(jnp.int32, (c, c), 0)
    jj = lax.broadcasted_iota(jnp.int32, (c, c), 1)
    dist = jnp.abs(ii - jj).astype(F32)
    intra_decay = [jnp.exp(math.log1p(-(2.0 ** (-5.0 - h))) * dist) for h in range(RET_HEADS)]

    for cc in range(cps):
        rows = slice(cc * c, (cc + 1) * c)
        q = q_ref[rows, :]
        k = k_ref[rows, :]
        v = v_ref[rows, :]
        qf = (q * q_decay_f).astype(BF16)
        qb = (q * q_decay_b).astype(BF16)
        cross = (jnp.dot(qf, s_ref[...].astype(BF16), preferred_element_type=F32)
                 + jnp.dot(qb, sb_ref[0, cc], preferred_element_type=F32))
        q_heads = jnp.concatenate([jnp.where(lane_head == h, q, jnp.zeros_like(q)) for h in range(RET_HEADS)],
                                  axis=0)
        scores = lax.dot_general(q_heads, k, _NT, preferred_element_type=F32)
        for h in range(RET_HEADS):
            s = scores[h * c:(h + 1) * c, :] * intra_decay[h]
            cols = slice(h * RET_DV, (h + 1) * RET_DV)
            o = jnp.dot(s.astype(BF16), v[:, cols], preferred_element_type=F32) + cross[:, cols]
            mu = jnp.mean(o, axis=-1, keepdims=True)
            d = o - mu
            var = jnp.mean(d * d, axis=-1, keepdims=True)
            gate = _silu(g_ref[rows, cols].astype(F32))
            o_ref[rows, cols] = (gate * (d * lax.rsqrt(var + LN_EPS))).astype(BF16)
        kf = (k * k_decay).astype(BF16)
        upd = lax.dot_general(kf, v, _TN, preferred_element_type=F32)
        s_ref[...] = s_ref[...] * chunk_decay + jnp.where(mask, upd, 0.0)


def _retention(proj, batch, seq_len):
    t = proj.shape[0]
    c = RET_CHUNK
    cps = min(RET_CHUNKS_PER_STEP, seq_len // c)
    rows = cps * c
    ns = seq_len // rows
    dv = RET_HEADS * RET_DV
    state_shape = (2 * LANES, dv)
    rev = lambda b, n: b * ns + (ns - 1 - n)
    fwd = lambda b, n: b * ns + n
    sb = pl.pallas_call(
        functools.partial(_ret_bstate_kernel, cps=cps),
        grid=(batch, ns),
        in_specs=[
            pl.BlockSpec((rows, 2 * LANES), lambda b, n: (rev(b, n), 1)),
            pl.BlockSpec((rows, dv), lambda b, n: (rev(b, n), 1)),
        ],
        out_specs=pl.BlockSpec((1, cps) + state_shape, lambda b, n: (b, ns - 1 - n, 0, 0)),
        out_shape=jax.ShapeDtypeStruct((batch, ns * cps) + state_shape, BF16),
        scratch_shapes=[pltpu.VMEM(state_shape, F32)],
        compiler_params=_params(("arbitrary", "arbitrary")),
        name="ret_bstate",
    )(proj, proj)
    return pl.pallas_call(
        functools.partial(_ret_out_kernel, cps=cps),
        grid=(batch, ns),
        in_specs=[
            pl.BlockSpec((rows, 2 * LANES), lambda b, n: (fwd(b, n), 0)),
            pl.BlockSpec((rows, 2 * LANES), lambda b, n: (fwd(b, n), 1)),
            pl.BlockSpec((rows, dv), lambda b, n: (fwd(b, n), 1)),
            pl.BlockSpec((rows, dv), lambda b, n: (fwd(b, n), 2)),
            pl.BlockSpec((1, cps) + state_shape, lambda b, n: (b, n, 0, 0)),
        ],
        out_specs=pl.BlockSpec((rows, dv), lambda b, n: (fwd(b, n), 0)),
        out_shape=jax.ShapeDtypeStruct((t, dv), BF16),
        scratch_shapes=[pltpu.VMEM(state_shape, F32)],
        compiler_params=_params(("arbitrary", "arbitrary")),
        name="ret_out",
    )(proj, proj, proj, proj, sb)


def _lru_kernel(xfp_ref, xf_ref, xfn_ref, xbp_ref, xb_ref, xbn_ref, cw_ref, cb_ref, gw_ref, gb_ref,
                lam_ref, hf_ref, hb_ref, xx_ref, a_ref, b_ref, h_ref, *, nt, ts, batch):
    i = pl.program_id(0)
    halo = SUBLANES
    lo = LRU_CONV // 2

    @pl.when(i == 0)
    def _():
        h_ref[...] = jnp.zeros_like(h_ref)

    def prepare(xp_ref, x_ref, xn_ref, tile, z, slot):
        for b in range(batch):
            xx_ref[0:halo, :] = jnp.where(tile == 0, 0.0, xp_ref[b])
            xx_ref[halo:halo + ts, :] = x_ref[b]
            xx_ref[halo + ts:2 * halo + ts, :] = jnp.where(tile == nt - 1, 0.0, xn_ref[b])
            xx = xx_ref[...]
            xc = cb_ref[...]
            for tap in range(LRU_CONV):
                shift = (lo - tap) % (ts + 2 * halo)
                moved = xx if shift == 0 else pltpu.roll(xx, shift, axis=0)
                xc = xc + cw_ref[tap:tap + 1, :] * moved[halo:halo + ts, :]
            for n in range(LRU_BLOCKS):
                cols = slice(n * LRU_BW, (n + 1) * LRU_BW)
                xn = xc[:, cols]
                g = (jnp.dot(xn.astype(BF16), gw_ref[z, n], preferred_element_type=F32)
                     + gb_ref[z, n:n + 1, :])
                r = _sigmoid(g[:, :LRU_BW])
                ig = _sigmoid(g[:, LRU_BW:])
                a = jnp.exp((-LRU_C) * r * _softplus(-lam_ref[z, :, cols]))
                a_ref[slot + b, :, cols] = a
                b_ref[slot + b, :, cols] = jnp.sqrt(1.0 - a * a) * (ig * xn)

    prepare(xfp_ref, xf_ref, xfn_ref, i, 0, 0)
    prepare(xbp_ref, xb_ref, xbn_ref, nt - 1 - i, 1, batch)

    def step(s, hs):
        out = []
        for k in range(2 * batch):
            row = s if k < batch else ts - 1 - s
            h = a_ref[k, pl.ds(row, 1), :] * hs[k] + b_ref[k, pl.ds(row, 1), :]
            if k < batch:
                hf_ref[k, pl.ds(row, 1), :] = h
            else:
                hb_ref[k - batch, pl.ds(row, 1), :] = h
            out.append(h)
        return tuple(out)

    hs = lax.fori_loop(0, ts, step, tuple(h_ref[k] for k in range(2 * batch)), unroll=8)
    for k in range(2 * batch):
        h_ref[k] = hs[k]


def _rglru(proj_f32, conv_w, conv_b, gate_w, gate_b, lam, batch, seq_len):
    w = LRU_BLOCKS * LRU_BW
    ts = min(LRU_ROWS, seq_len)
    nt = seq_len // ts
    rows8 = ts // SUBLANES
    last8 = seq_len // SUBLANES - 1
    x3 = proj_f32.reshape(batch, seq_len, proj_f32.shape[1])
    gw = jnp.concatenate([gate_w[:, 0], gate_w[:, 1]], axis=-1).astype(BF16)
    gb = jnp.concatenate([gate_b[:, 0], gate_b[:, 1]], axis=-1)
    bwd = lambda i: nt - 1 - i

    def tile_specs(tile):
        return [
            pl.BlockSpec((batch, SUBLANES, w), lambda i: (0, jnp.maximum(tile(i) * rows8 - 1, 0), 0)),
            pl.BlockSpec((batch, ts, w), lambda i: (0, tile(i), 0)),
            pl.BlockSpec((batch, SUBLANES, w), lambda i: (0, jnp.minimum((tile(i) + 1) * rows8, last8), 0)),
        ]

    full = lambda a: pl.BlockSpec(a.shape, lambda i: (0,) * a.ndim)
    cb = conv_b.reshape(1, w)
    lam3 = lam.reshape(2, 1, w)
    state = jax.ShapeDtypeStruct((batch, seq_len, w), F32)
    h_fwd, h_bwd = pl.pallas_call(
        functools.partial(_lru_kernel, nt=nt, ts=ts, batch=batch),
        grid=(nt,),
        in_specs=tile_specs(lambda i: i) + tile_specs(bwd) + [full(conv_w), full(cb), full(gw), full(gb),
                                                               full(lam3)],
        out_specs=[pl.BlockSpec((batch, ts, w), lambda i: (0, i, 0)),
                   pl.BlockSpec((batch, ts, w), lambda i: (0, bwd(i), 0))],
        out_shape=[state, state],
        scratch_shapes=[pltpu.VMEM((ts + 2 * SUBLANES, w), F32), pltpu.VMEM((2 * batch, ts, w), F32),
                        pltpu.VMEM((2 * batch, ts, w), F32), pltpu.VMEM((2 * batch, 1, w), F32)],
        compiler_params=_params(("arbitrary",)),
        name="lru_scan",
    )(x3, x3, x3, x3, x3, x3, conv_w, cb, gw, gb, lam3)
    return h_fwd.reshape(batch * seq_len, w), h_bwd.reshape(batch * seq_len, w)


def _gla_kernel(qf_ref, vf_ref, kf_ref, lf_ref, qb_ref, vb_ref, kb_ref, lb_ref, of_ref, ob_ref, s_ref,
                *, ts, batch):
    c = GLA_CHUNK
    nchunks = ts // c

    @pl.when(pl.program_id(0) == 0)
    def _():
        s_ref[...] = jnp.zeros_like(s_ref)

    head_cols = [slice(h * HG_DK, (h + 1) * HG_DK) for h in range(HG_HEADS)]
    hc = HG_HEADS * c
    ii = lax.broadcasted_iota(jnp.int32, (c, c), 0)
    jj = lax.broadcasted_iota(jnp.int32, (c, c), 1)
    si = lax.broadcasted_iota(jnp.int32, (hc, hc), 0)
    sj = lax.broadcasted_iota(jnp.int32, (hc, hc), 1)
    same_head = si // c == sj // c

    def stack(a):
        return jnp.concatenate([a[:, cols] for cols in head_cols], axis=0)

    tri = {False: (jj <= ii).astype(BF16), True: (jj >= ii).astype(BF16)}
    tri2 = {r: jnp.concatenate([m, m], axis=1) for r, m in tri.items()}
    zero_block = jnp.zeros((c, HG_DK), BF16)
    zero_state = jnp.zeros((HG_DK, HG_DV), BF16)
    keep = {False: jnp.logical_and(same_head, sj <= si), True: jnp.logical_and(same_head, sj >= si)}
    chains = ([(qf_ref, vf_ref, kf_ref, lf_ref, of_ref, b, False) for b in range(batch)]
              + [(qb_ref, vb_ref, kb_ref, lb_ref, ob_ref, b, True) for b in range(batch)])

    def gates(cc):
        work = []
        for chain, (q_ref, v_ref, k_ref, l_ref, o_ref, b, reverse) in enumerate(chains):
            rows = pl.ds(((nchunks - 1 - cc) if reverse else cc) * c, c)
            log_f = l_ref[b, rows, :]
            f_hi = log_f.astype(BF16)
            f_lo = (log_f - f_hi.astype(F32)).astype(BF16)
            work.append(dict(rows=rows, reverse=reverse, o_ref=o_ref, b=b, chain=chain,
                             key=k_ref[b, rows, :].astype(F32), qs=q_ref[b, rows, :].astype(F32),
                             v=v_ref[b, rows, :], split=(f_hi, f_lo)))
        return work

    def cumulate(work):
        for w in work:
            f_hi, f_lo = w["split"]
            w["bcum"] = jnp.dot(tri2[w["reverse"]], jnp.concatenate([f_lo, f_hi], axis=0),
                                preferred_element_type=F32)

    half = c // 2
    row = lax.broadcasted_iota(jnp.int32, (c, 1), 0)
    first_half = row < half

    def decays(work):
        for w in work:
            bcum = w["bcum"]
            rev = w["reverse"]
            end = 0 if rev else c - 1
            b_end = bcum[end:end + 1, :]
            ref = jnp.where(first_half, bcum[half // 2:half // 2 + 1, :], bcum[half + half // 2:half + half // 2 + 1, :])
            qe = w["qs"] * jnp.exp2(bcum - ref)
            ke = w["key"] * jnp.exp2(ref - bcum)
            boundary = bcum[half:half + 1, :] if rev else bcum[half - 1:half, :]
            cross = jnp.exp2(-jnp.abs(bcum - boundary))
            queries = first_half if rev else jnp.logical_not(first_half)
            w["q3"] = [jnp.where(first_half, qe, 0.0).astype(BF16), jnp.where(first_half, 0.0, qe).astype(BF16),
                       jnp.where(queries, w["qs"] * cross, 0.0).astype(BF16)]
            w["k3"] = [jnp.where(first_half, ke, 0.0), jnp.where(first_half, 0.0, ke),
                       jnp.where(queries, 0.0, w["key"] * cross)]
            w["qd"] = (w["qs"] * jnp.exp2(bcum)).astype(BF16)
            w["kd"] = (w["key"] * jnp.exp2(b_end - bcum)).astype(BF16)
            w["decay"] = jnp.exp2(b_end)

    def stack3(parts):
        return jnp.concatenate([jnp.concatenate([p[:, cols] for p in parts], axis=1) for cols in head_cols], axis=0)

    def scores(work):
        for w in work:
            w["att"] = jnp.dot(stack3(w["q3"]), stack3(w["k3"]).T.astype(BF16), preferred_element_type=F32)

    def intra(work):
        for w in work:
            att = jnp.where(keep[w["reverse"]], w["att"], 0.0).astype(BF16)
            w["intra"] = jnp.dot(att, stack(w["v"]), preferred_element_type=F32)

    def inter(work):
        for w in work:
            w["st"] = [s_ref[w["chain"], h] for h in range(HG_HEADS)]
            out = []
            for h in range(0, HG_HEADS, 2):
                sa = w["st"][h].T.astype(BF16)
                sb = w["st"][h + 1].T.astype(BF16)
                rhs = jnp.concatenate([jnp.concatenate([sa, zero_state], axis=1),
                                       jnp.concatenate([zero_state, sb], axis=1)], axis=0)
                pair = jnp.dot(w["qd"][:, h * HG_DK:(h + 2) * HG_DK], rhs, preferred_element_type=F32)
                out += [pair[:, :HG_DV], pair[:, HG_DV:]]
            w["inter"] = out

    def update(work):
        for w in work:
            upd = []
            for h in range(0, HG_HEADS, 2):
                ca, cb = head_cols[h], head_cols[h + 1]
                lhs = jnp.concatenate([w["v"][:, ca], w["v"][:, cb]], axis=0)
                rhs = jnp.concatenate([jnp.concatenate([w["kd"][:, ca], zero_block], axis=1),
                                       jnp.concatenate([zero_block, w["kd"][:, cb]], axis=1)], axis=0)
                pair = lax.dot_general(lhs, rhs, _TN, preferred_element_type=F32)
                upd += [pair[:, :HG_DK], pair[:, HG_DK:]]
            w["upd"] = upd

    def finish(work):
        for w in work:
            k = w["chain"]
            for h, cols in enumerate(head_cols):
                w["o_ref"][w["b"], w["rows"], cols] = w["intra"][h * c:(h + 1) * c, :] + w["inter"][h]
                s_ref[k, h] = w["st"][h] * w["decay"][:, cols] + w["upd"][h]

    cur = gates(0)
    cumulate(cur)
    decays(cur)
    for cc in range(nchunks):
        more = cc + 1 < nchunks
        scores(cur)
        nxt = gates(cc + 1) if more else None
        if more:
            cumulate(nxt)
        intra(cur)
        inter(cur)
        update(cur)
        if more:
            decays(nxt)
        finish(cur)
        cur = nxt


def _gla(proj_bf16, proj_f32, batch, seq_len):
    d = HG_HEADS * HG_DK
    ts = min(GLA_ROWS, seq_len)
    nt = seq_len // ts
    pb = proj_bf16.reshape(batch, seq_len, proj_bf16.shape[1])
    pf = proj_f32.reshape(batch, seq_len, proj_f32.shape[1])
    bwd = lambda i: nt - 1 - i
    blk = lambda tile, col: pl.BlockSpec((batch, ts, d), lambda i: (0, tile(i), col))
    fwd = lambda i: i
    out = jax.ShapeDtypeStruct((batch, seq_len, d), F32)
    o_fwd, o_bwd = pl.pallas_call(
        functools.partial(_gla_kernel, ts=ts, batch=batch),
        grid=(nt,),
        in_specs=[blk(fwd, 0), blk(fwd, 1), blk(fwd, 3), blk(fwd, 0),
                  blk(bwd, 0), blk(bwd, 1), blk(bwd, 4), blk(bwd, 1)],
        out_specs=[blk(fwd, 0), blk(bwd, 0)],
        out_shape=[out, out],
        scratch_shapes=[pltpu.VMEM((2 * batch, HG_HEADS, HG_DV, HG_DK), F32)],
        compiler_params=_params(("arbitrary",)),
        name="gla_scan",
    )(pb, pb, pb, pf, pb, pb, pb, pf)
    return o_fwd.reshape(batch * seq_len, d), o_bwd.reshape(batch * seq_len, d)


def _top2(p):
    v1 = jnp.maximum(jnp.maximum(p[0], p[1]), jnp.maximum(p[2], p[3]))
    i1 = jnp.where(p[0] == v1, 0, jnp.where(p[1] == v1, 1, jnp.where(p[2] == v1, 2, 3)))
    q = [jnp.where(i1 == k, -1.0, p[k]) for k in range(4)]
    v2 = jnp.maximum(jnp.maximum(q[0], q[1]), jnp.maximum(q[2], q[3]))
    i2 = jnp.where(q[0] == v2, 0, jnp.where(q[1] == v2, 1, jnp.where(q[2] == v2, 2, 3)))
    return v1, i1, v2, i2


def _outproj_kernel(*refs, even, alpha, tm):
    if even:
        ret_ref, hf_ref, hb_ref, gr_ref, w0_ref, w1_ref = refs[:6]
        rest = refs[6:]
        lru = ((hf_ref[...] + hb_ref[...]) * jax.nn.gelu(gr_ref[...])).astype(BF16)
        y = (jnp.dot(ret_ref[...], w0_ref[...], preferred_element_type=F32)
             + jnp.dot(lru, w1_ref[...], preferred_element_type=F32))
    else:
        of_ref, ob_ref, g_ref, nw_ref, w0_ref = refs[:5]
        rest = refs[5:]
        o = of_ref[...] + ob_ref[...]
        ms = jnp.mean(o * o, axis=-1, keepdims=True)
        mix = o * lax.rsqrt(ms + RMS_EPS) * nw_ref[...] * _silu(g_ref[...].astype(F32))
        y = jnp.dot(mix.astype(BF16), w0_ref[...], preferred_element_type=F32)
    h_ref, lnw_ref, lnb_ref, rw_ref, o_ref, info_ref, carry_ref, before_ref = rest
    i = pl.program_id(0)

    @pl.when(i == 0)
    def _():
        carry_ref[...] = jnp.zeros_like(carry_ref)
        tt = lax.broadcasted_iota(jnp.int32, (tm, tm), 0)
        uu = lax.broadcasted_iota(jnp.int32, (tm, tm), 1)
        before_ref[...] = (tt < uu).astype(BF16)

    h1 = _layer_norm(alpha * h_ref[...] + y, lnw_ref[...], lnb_ref[...])
    o_ref[...] = h1

    h_hi = h1.astype(BF16)
    h_lo = (h1 - h_hi.astype(F32)).astype(BF16)
    rw = rw_ref[...]
    r_hi = rw.astype(BF16)
    r_lo = (rw - r_hi.astype(F32)).astype(BF16)
    both = lax.dot_general(jnp.concatenate([r_hi, r_lo], axis=0), h_hi, _NT, preferred_element_type=F32)
    logits = (both[N_EXPERTS:, :] + lax.dot_general(r_hi, h_lo, _NT, preferred_element_type=F32)
              + both[:N_EXPERTS, :])
    ex = jnp.exp(logits - jnp.max(logits, axis=0, keepdims=True))
    probs = ex / jnp.sum(ex, axis=0, keepdims=True)
    best = None
    for g in range(N_GROUPS):
        rows = [probs[g * EXPERTS_PER_GROUP + k:g * EXPERTS_PER_GROUP + k + 1, :]
                for k in range(EXPERTS_PER_GROUP)]
        v1, i1, v2, i2 = _top2(rows)
        cand = (v1 + v2, v1, i1 + g * EXPERTS_PER_GROUP, v2, i2 + g * EXPERTS_PER_GROUP)
        if best is None:
            best = cand
        else:
            take = cand[0] > best[0]
            best = tuple(jnp.where(take, cn, bs) for cn, bs in zip(cand, best))
    _, v1, e1, v2, e2 = best
    denom = v1 + v2
    g1 = v1 / denom
    g2 = v2 / denom

    eid = lax.broadcasted_iota(jnp.int32, (N_EXPERTS, tm), 0)
    oh1 = (eid == e1).astype(F32)
    oh2 = (eid == e2).astype(F32)
    oh = oh1 + oh2
    base = carry_ref[:, 0:1] + jnp.dot(oh.astype(BF16), before_ref[...], preferred_element_type=F32)
    rank1 = jnp.sum(oh1 * base, axis=0, keepdims=True)
    rank2 = jnp.sum(oh2 * base, axis=0, keepdims=True)
    carry_ref[...] = carry_ref[...] + jnp.sum(oh, axis=1, keepdims=True)
    zero = jnp.zeros_like(g1)
    info_ref[...] = jnp.concatenate(
        [e1.astype(F32), e2.astype(F32), g1, g2, rank1, rank2, zero, zero], axis=0)


def _out_project(even, mixer_inputs, weights_bf16, h, ln_w, ln_b, router_wt, alpha):
    t, d = h.shape
    tm = TOKEN_ROWS
    row = lambda i: (i, 0)
    const = lambda i: (0, 0)
    if even:
        ret, h_fwd, h_bwd, proj_f32 = mixer_inputs
        w = h_fwd.shape[1]
        in_specs = [pl.BlockSpec((tm, ret.shape[1]), row), pl.BlockSpec((tm, w), row),
                    pl.BlockSpec((tm, w), row), pl.BlockSpec((tm, w), lambda i: (i, 1))]
        args = [ret, h_fwd, h_bwd, proj_f32]
    else:
        o_fwd, o_bwd, proj_bf16, norm_w = mixer_inputs
        in_specs = [pl.BlockSpec((tm, d), row), pl.BlockSpec((tm, d), row),
                    pl.BlockSpec((tm, d), lambda i: (i, 2)), pl.BlockSpec((1, d), const)]
        args = [o_fwd, o_bwd, proj_bf16, norm_w.reshape(1, d)]
    in_specs += [pl.BlockSpec(wm.shape, const) for wm in weights_bf16]
    in_specs += [pl.BlockSpec((tm, d), row), pl.BlockSpec((1, d), const), pl.BlockSpec((1, d), const),
                 pl.BlockSpec((N_EXPERTS, d), const)]
    args += list(weights_bf16) + [h, ln_w.reshape(1, d), ln_b.reshape(1, d), router_wt]
    return pl.pallas_call(
        functools.partial(_outproj_kernel, even=even, alpha=alpha, tm=tm),
        grid=(t // tm,),
        in_specs=in_specs,
        out_specs=[pl.BlockSpec((tm, d), row),
                   pl.BlockSpec((SUBLANES, tm), lambda i: (0, i))],
        out_shape=[jax.ShapeDtypeStruct((t, d), F32),
                   jax.ShapeDtypeStruct((SUBLANES, t), F32)],
        scratch_shapes=[pltpu.VMEM((N_EXPERTS, LANES), F32), pltpu.VMEM((tm, tm), BF16)],
        compiler_params=_params(("arbitrary",)),
        name="out_proj_router",
    )(*args)


def _pack_halves(x):
    n = x.shape[1] // 2
    hi = pltpu.bitcast(x[:, :n], jnp.uint32)
    lo = pltpu.bitcast(x[:, n:], jnp.uint32)
    return hi | (lo >> 16)


def _unpack_halves(p):
    hi = pltpu.bitcast(p & jnp.uint32(0xFFFF0000), F32)
    lo = pltpu.bitcast(p << 16, F32)
    return jnp.concatenate([hi.astype(BF16), lo.astype(BF16)], axis=1)


def _seg_expert_kernel(first_ref, count_ref, tail_ref, x_ref, wg_ref, wu_ref, wd_ref, y_ref,
                       xbuf, ybuf, wgb_ref, wub_ref, wdb_ref, semx, semy):
    e = pl.program_id(0)
    wgb_ref[...] = wg_ref[...].astype(BF16)
    wub_ref[...] = wu_ref[...].astype(BF16)
    wdb_ref[...] = wd_ref[...].astype(BF16)
    first = first_ref[e]
    n_used = tail_ref[0]

    def rows(g):
        return pl.ds(pl.multiple_of(g * MOE_ROWS, MOE_ROWS), MOE_ROWS)

    def fetch(g):
        return pltpu.make_async_copy(x_ref.at[rows(g)], xbuf.at[g % 2], semx.at[g % 2])

    def put(g):
        return pltpu.make_async_copy(ybuf.at[g % 2], y_ref.at[rows(g)], semy.at[g % 2])

    @pl.when(e == 0)
    def _():
        fetch(0).start()

    def body(g, carry):
        @pl.when(g + 1 < n_used)
        def _():
            fetch(g + 1).start()

        fetch(g).wait()

        @pl.when(g >= 2)
        def _():
            put(g - 2).wait()

        x = _unpack_halves(xbuf[g % 2])
        gate = jnp.dot(x, wgb_ref[...], preferred_element_type=F32)
        up = jnp.dot(x, wub_ref[...], preferred_element_type=F32)
        hid = (_silu(gate) * up).astype(BF16)
        y = jnp.dot(hid, wdb_ref[...], preferred_element_type=F32)
        ybuf[g % 2] = _pack_halves(y.astype(BF16).astype(F32))
        put(g).start()
        return carry

    lax.fori_loop(first, first + count_ref[e], body, 0)

    @pl.when(e == pl.num_programs(0) - 1)
    def _():
        @pl.when(n_used >= 2)
        def _():
            put(n_used - 2).wait()

        put(n_used - 1).wait()
        ybuf[0] = jnp.zeros_like(ybuf[0])

        def clear(b, carry):
            cp = pltpu.make_async_copy(
                ybuf.at[0], y_ref.at[pl.ds(pl.multiple_of(b * MOE_ROWS, MOE_ROWS), MOE_ROWS)], semy.at[0])
            cp.start()
            cp.wait()
            return carry

        lax.fori_loop(tail_ref[0], y_ref.shape[0] // MOE_ROWS, clear, 0)


def _seg_experts(xb, seg_first, seg_count, n_valid, w_gate, w_up, w_down, layer):
    p, half = xb.shape
    d = 2 * half
    de = w_gate.shape[3]
    weight = lambda shape: pl.BlockSpec((None, None) + shape, lambda e, a, b, c: (layer, e, 0, 0))
    return pl.pallas_call(
        _seg_expert_kernel,
        grid_spec=pltpu.PrefetchScalarGridSpec(
            num_scalar_prefetch=3,
            grid=(N_EXPERTS,),
            in_specs=[pl.BlockSpec(memory_space=pl.ANY), weight((d, de)), weight((d, de)), weight((de, d))],
            out_specs=pl.BlockSpec(memory_space=pl.ANY),
            scratch_shapes=[pltpu.VMEM((2, MOE_ROWS, half), jnp.uint32),
                            pltpu.VMEM((2, MOE_ROWS, half), jnp.uint32),
                            pltpu.VMEM((d, de), BF16), pltpu.VMEM((d, de), BF16), pltpu.VMEM((de, d), BF16),
                            pltpu.SemaphoreType.DMA((2,)), pltpu.SemaphoreType.DMA((2,))],
        ),
        out_shape=jax.ShapeDtypeStruct((p, half), jnp.uint32),
        compiler_params=_params(("arbitrary",)),
        name="moe_experts",
    )(seg_first, seg_count, n_valid, xb, w_gate, w_up, w_down)


def _clear_padding_blocks(zs_ref, xb_ref, zero_ref, sem):
    zero_ref[...] = jnp.zeros_like(zero_ref)

    def clear(row):
        start = pl.multiple_of(row, MOE_ROWS)
        return pltpu.make_async_copy(zero_ref, xb_ref.at[pl.ds(start, MOE_ROWS)], sem)

    for e in range(N_EXPERTS):
        clear(zs_ref[e]).start()
    for e in range(N_EXPERTS):
        clear(zs_ref[e]).wait()

    def clear_tail(b, carry):
        clear(b * MOE_ROWS).start()
        clear(b * MOE_ROWS).wait()
        return carry

    lax.fori_loop(zs_ref[N_EXPERTS], xb_ref.shape[0] // MOE_ROWS, clear_tail, 0)


def _run_copy(src_ref, src_row, dst_ref, dst_row, sem):
    src = pl.multiple_of(src_row, SUBLANES)
    dst = pl.multiple_of(dst_row, SUBLANES)
    return pltpu.make_async_copy(src_ref.at[pl.ds(src, SUBLANES)], dst_ref.at[pl.ds(dst, SUBLANES)], sem)


def _sort_dispatch_kernel(dst_ref, nch_ref, zs_ref, h_ref, lp_ref, xb_ref, xs_ref, zero_ref, sem, *, tm, nck):
    i = pl.program_id(0)

    @pl.when(i == 0)
    def _():
        _clear_padding_blocks(zs_ref, xb_ref, zero_ref, sem.at[0])

    sub = SORT_TILES_PER_STEP
    last = pl.num_programs(0) - 1

    def wait_runs(buf, count):
        def wait(j, carry):
            _run_copy(xs_ref.at[buf], 0, xb_ref, 0, sem.at[buf]).wait()
            return carry

        lax.fori_loop(0, count, wait, 0)

    pos = lax.broadcasted_iota(jnp.int32, (xs_ref.shape[1], tm), 0)
    for s in range(sub):
        tile = i * sub + s
        buf = (i % 2) * sub + s

        @pl.when(i >= 2)
        def _():
            wait_runs(buf, nch_ref[jnp.maximum(tile - 2 * sub, 0)])

        lp = lp_ref[:, s * tm:(s + 1) * tm]
        perm = jnp.logical_or(pos == lp[0:1, :], pos == lp[1:2, :]).astype(BF16)
        rows = h_ref[s * tm:(s + 1) * tm, :].astype(BF16)
        xs_ref[buf] = _pack_halves(jnp.dot(perm, rows, preferred_element_type=F32))

        def start(j, carry):
            _run_copy(xs_ref.at[buf], j * SUBLANES, xb_ref, dst_ref[tile * nck + j], sem.at[buf]).start()
            return carry

        lax.fori_loop(0, nch_ref[tile], start, 0)

    @pl.when(i == last)
    def _():
        for s in range(sub):
            @pl.when(i >= 1)
            def _():
                wait_runs((1 - i % 2) * sub + s, nch_ref[jnp.maximum((i - 1) * sub + s, 0)])

            wait_runs((i % 2) * sub + s, nch_ref[i * sub + s])


def _sort_dispatch(h, lp_rows, chunk_dst, n_chunks, zero_start, n_rows, tm, sorted_rows):
    t, d = h.shape
    nck = sorted_rows // SUBLANES
    sub = SORT_TILES_PER_STEP
    return pl.pallas_call(
        functools.partial(_sort_dispatch_kernel, tm=tm, nck=nck),
        grid_spec=pltpu.PrefetchScalarGridSpec(
            num_scalar_prefetch=3,
            grid=(t // (sub * tm),),
            in_specs=[pl.BlockSpec((sub * tm, d), lambda i, a, b, c: (i, 0)),
                      pl.BlockSpec((2, sub * tm), lambda i, a, b, c: (0, i))],
            out_specs=pl.BlockSpec(memory_space=pl.ANY),
            scratch_shapes=[pltpu.VMEM((2 * sub, sorted_rows, d // 2), jnp.uint32),
                            pltpu.VMEM((MOE_ROWS, d // 2), jnp.uint32), pltpu.SemaphoreType.DMA((2 * sub,))],
        ),
        out_shape=jax.ShapeDtypeStruct((n_rows, d // 2), jnp.uint32),
        compiler_params=_params(("arbitrary",)),
        name="moe_dispatch",
    )(chunk_dst, n_chunks, zero_start, h, lp_rows)


def _sort_combine_kernel(src_ref, nch_ref, h_ref, lp_ref, gates_ref, lnw_ref, lnb_ref, yb_ref, o_ref,
                         ys_ref, sem, *, tm, nck, alpha):
    i = pl.program_id(0)
    sub = SORT_TILES_PER_STEP

    def fetch(tile, buf):
        def start(j, carry):
            _run_copy(yb_ref, src_ref[tile * nck + j], ys_ref.at[buf], j * SUBLANES, sem.at[buf]).start()
            return carry

        lax.fori_loop(0, nch_ref[tile], start, 0)

    @pl.when(i == 0)
    def _():
        ys_ref[...] = jnp.zeros_like(ys_ref)
        for s in range(sub):
            fetch(s, s)

    @pl.when(i + 1 < pl.num_programs(0))
    def _():
        for s in range(sub):
            fetch((i + 1) * sub + s, (1 - i % 2) * sub + s)

    pos = lax.broadcasted_iota(jnp.int32, (tm, ys_ref.shape[1]), 1)
    for s in range(sub):
        buf = (i % 2) * sub + s
        rows = slice(s * tm, (s + 1) * tm)

        def wait(j, carry):
            _run_copy(yb_ref, 0, ys_ref.at[buf], 0, sem.at[buf]).wait()
            return carry

        lax.fori_loop(0, nch_ref[i * sub + s], wait, 0)

        lp = lp_ref[rows, :]
        pick = (jnp.where(pos == lp[:, 0:1], gates_ref[rows, 0:1], 0.0)
                + jnp.where(pos == lp[:, 1:2], gates_ref[rows, 1:2], 0.0))
        y = jnp.dot(pick.astype(BF16), _unpack_halves(ys_ref[buf]), preferred_element_type=F32)
        o_ref[rows, :] = _layer_norm(alpha * h_ref[rows, :] + y, lnw_ref[...], lnb_ref[...])


def _sort_combine(h, yb, lp_cols, gates, chunk_src, n_chunks, ln_w, ln_b, alpha, tm, sorted_rows):
    t, d = h.shape
    nck = sorted_rows // SUBLANES
    step = SORT_TILES_PER_STEP * tm
    row = lambda i, a, b: (i, 0)
    const = lambda i, a, b: (0, 0)
    return pl.pallas_call(
        functools.partial(_sort_combine_kernel, tm=tm, nck=nck, alpha=alpha),
        grid_spec=pltpu.PrefetchScalarGridSpec(
            num_scalar_prefetch=2,
            grid=(t // step,),
            in_specs=[pl.BlockSpec((step, d), row), pl.BlockSpec((step, 2), row), pl.BlockSpec((step, 2), row),
                      pl.BlockSpec((1, d), const), pl.BlockSpec((1, d), const),
                      pl.BlockSpec(memory_space=pl.ANY)],
            out_specs=pl.BlockSpec((step, d), row),
            scratch_shapes=[pltpu.VMEM((2 * SORT_TILES_PER_STEP, sorted_rows, d // 2), jnp.uint32),
                            pltpu.SemaphoreType.DMA((2 * SORT_TILES_PER_STEP,))],
        ),
        out_shape=jax.ShapeDtypeStruct((t, d), F32),
        compiler_params=_params(("arbitrary",)),
        name="moe_combine",
    )(chunk_src, n_chunks, h, lp_cols, gates, ln_w.reshape(1, d), ln_b.reshape(1, d), yb)


def _moe(h1, info, w_gate, w_up, w_down, layer, ln_w, ln_b, alpha):
    t = h1.shape[0]
    tm = SORT_ROWS
    ntile = t // tm
    run_pad = SUBLANES - 1
    sorted_rows = 2 * tm + LANES
    nck = sorted_rows // SUBLANES
    n_rows = (2 * t + ntile * N_EXPERTS * run_pad + MOE_ROWS - 1) // MOE_ROWS * MOE_ROWS + N_EXPERTS * MOE_ROWS
    i32 = jnp.int32
    e1, e2 = info[0].astype(i32), info[1].astype(i32)
    rank1, rank2 = info[4].astype(i32), info[5].astype(i32)
    expert_ids = jnp.arange(N_EXPERTS, dtype=i32)[None, :]
    oh1 = e1[:, None] == expert_ids
    oh2 = e2[:, None] == expert_ids
    cnt = jnp.logical_or(oh1, oh2).astype(i32).reshape(ntile, tm, N_EXPERTS).sum(axis=1)
    cnt8 = (cnt + run_pad) // SUBLANES * SUBLANES
    local = jnp.cumsum(cnt8, axis=1) - cnt8
    before = jnp.cumsum(cnt, axis=0) - cnt
    seg = cnt8.sum(axis=0)
    seg_pad = (seg + MOE_ROWS - 1) // MOE_ROWS * MOE_ROWS
    ends = jnp.cumsum(seg_pad)
    slot = (ends - seg_pad)[None, :] + jnp.cumsum(cnt8, axis=0) - cnt8
    shift = jnp.repeat(local - before, tm, axis=0)
    lp1 = jnp.sum(jnp.where(oh1, shift, 0), axis=1) + rank1
    lp2 = jnp.sum(jnp.where(oh2, shift, 0), axis=1) + rank2
    chunk_row = jnp.arange(nck, dtype=i32) * SUBLANES
    run_of = jnp.sum(((local + cnt8)[:, None, :] <= chunk_row[None, :, None]).astype(i32), axis=2)
    run_of = jnp.minimum(run_of, N_EXPERTS - 1)
    chunk_slot = jnp.sum(jnp.where(run_of[:, :, None] == expert_ids[None], (slot - local)[:, None, :], 0),
                         axis=2) + chunk_row[None, :]
    n_chunks = cnt8.sum(axis=1) // SUBLANES
    n_valid = (ends[-1] // MOE_ROWS).astype(i32).reshape(1)
    zero_start = jnp.concatenate([jnp.maximum(ends - MOE_ROWS, 0), n_valid]).astype(i32)
    gates = jnp.stack([info[2], info[3]], axis=1)
    chunk_slot = chunk_slot.reshape(-1).astype(i32)
    seg_first = ((ends - seg_pad) // MOE_ROWS).astype(i32)
    seg_count = (seg_pad // MOE_ROWS).astype(i32)

    xb = _sort_dispatch(h1, jnp.stack([lp1, lp2], axis=0), chunk_slot, n_chunks, zero_start, n_rows, tm,
                        sorted_rows)
    yb = _seg_experts(xb, seg_first, seg_count, n_valid, w_gate, w_up, w_down, layer)
    return _sort_combine(h1, yb, jnp.stack([lp1, lp2], axis=1), gates, chunk_slot, n_chunks, ln_w, ln_b,
                         alpha, tm, sorted_rows)


def _rotary_column_order(w_in):
    d = w_in.shape[0]
    nq = RET_HEADS * RET_DK

    def perm(w):
        return w.reshape(d, RET_HEADS, RET_DK // 2, 2).transpose(0, 3, 1, 2).reshape(d, nq)

    return jnp.concatenate([perm(w_in[:, :nq]), perm(w_in[:, nq:2 * nq]), w_in[:, 2 * nq:]], axis=1)


def kernel(x, w_in_even, w_out_even, lru_conv_w, lru_conv_b, lru_gate_w, lru_gate_b, lru_lambda,
           w_in_odd, w_out_odd, hg_lower_bounds, hg_norm_w, ln_w, ln_b, router_w,
           moe_w_gate, moe_w_up, moe_w_down):
    batch, seq_len, d = x.shape
    depth = ln_w.shape[0]
    alpha = (2.0 * depth) ** 0.25
    t = batch * seq_len
    h = x.reshape(t, d)
    router_wt = router_w.T
    half = RET_DK // 2
    inv_freq = ROPE_BASE ** (-jnp.arange(0, RET_DK, 2, dtype=F32) / RET_DK)
    inv_freq = jnp.tile(inv_freq, LANES // half).reshape(1, LANES)
    nret = RET_HEADS * RET_DV
    tm = min(PROJ_ROWS, seq_len)
    tiles_per_seq = seq_len // tm
    cos, sin = _rotary_table(inv_freq, seq_len)
    pos_spec = pl.BlockSpec((tm, LANES), lambda i: (i % tiles_per_seq, 0))

    for layer in range(depth):
        j = layer // 2
        if layer % 2 == 0:
            w_in = _rotary_column_order(w_in_even[j]).astype(BF16)
            n_bf = 2 * RET_HEADS * RET_DK + 2 * nret
            proj_kernel = functools.partial(_proj_ret_kernel, n_bf=n_bf)
            proj_b, proj_f = _project(proj_kernel, "in_proj", h, w_in, [cos, sin], [pos_spec, pos_spec], n_bf,
                                      w_in.shape[1] - n_bf, seq_len)
            ret = _retention(proj_b, batch, seq_len)
            h_fwd, h_bwd = _rglru(proj_f, lru_conv_w[j], lru_conv_b[j], lru_gate_w[j], lru_gate_b[j],
                                  lru_lambda[j], batch, seq_len)
            w_out = w_out_even[j].astype(BF16)
            mixer_inputs, weights = (ret, h_fwd, h_bwd, proj_f), [w_out[:nret], w_out[nret:]]
        else:
            w = w_in_odd[j]
            w_in = jnp.concatenate([w[:, :2 * d], w[:, 4 * d:], w[:, 2 * d:4 * d]], axis=1).astype(BF16)
            proj_kernel = functools.partial(_proj_gla_kernel, layer=layer)
            bounds_spec = pl.BlockSpec(hg_lower_bounds.shape, lambda i: (0, 0))
            proj_b, proj_f = _project(proj_kernel, "in_proj_gla", h, w_in, [hg_lower_bounds], [bounds_spec],
                                      5 * d, 2 * d, seq_len)
            o_fwd, o_bwd = _gla(proj_b, proj_f, batch, seq_len)
            mixer_inputs, weights = (o_fwd, o_bwd, proj_b, hg_norm_w[j]), [w_out_odd[j].astype(BF16)]
        h1, info = _out_project(layer % 2 == 0, mixer_inputs, weights, h, ln_w[layer, 0], ln_b[layer, 0],
                                router_wt, alpha)
        h = _moe(h1, info, moe_w_gate, moe_w_up, moe_w_down, layer, ln_w[layer, 1], ln_b[layer, 1], alpha)
    return h.reshape(batch, seq_len, d)
```

```python
import functools
import math

import jax
import jax.numpy as jnp
from jax import lax
from jax.experimental import pallas as pl
from jax.experimental.pallas import tpu as pltpu

F32 = jnp.float32
BF16 = jnp.bfloat16

RET_HEADS = 4
RET_DK = 64
RET_DV = 128
RET_CHUNK = 128
ROPE_BASE = 10000.0
LRU_BLOCKS = 4
LRU_BW = 128
LRU_CONV = 4
LRU_C = 8.0
HG_HEADS = 8
HG_DK = 128
HG_DV = 128
N_EXPERTS = 16
N_GROUPS = 4
EXPERTS_PER_GROUP = 4
LN_EPS = 1e-5
RMS_EPS = 1e-6

LANES = 128
SUBLANES = 8
PROJ_ROWS = 512
PROJ_COLS = 512
RET_CHUNKS_PER_STEP = 8
LRU_ROWS = 512
GLA_CHUNK = 32
GLA_ROWS = 256
TOKEN_ROWS = 512
SORT_ROWS = 256
SORT_TILES_PER_STEP = 2
MOE_ROWS = 512
VMEM_LIMIT = 56 * 1024 * 1024

_NT = (((1,), (1,)), ((), ()))
_TN = (((0,), (0,)), ((), ()))


def _params(sem):
    return pltpu.CompilerParams(dimension_semantics=sem, vmem_limit_bytes=VMEM_LIMIT)


def _sigmoid(x):
    return 0.5 * jnp.tanh(0.5 * x) + 0.5


def _silu(x):
    return x * _sigmoid(x)


def _softplus(x):
    return jnp.maximum(x, 0.0) + jnp.log1p(jnp.exp(-jnp.abs(x)))


def _layer_norm(u, w, b):
    mu = jnp.mean(u, axis=-1, keepdims=True)
    d = u - mu
    var = jnp.mean(d * d, axis=-1, keepdims=True)
    return d * lax.rsqrt(var + LN_EPS) * w + b


def _rotary_table_kernel(inv_ref, cos_ref, sin_ref, *, tm):
    row = lax.broadcasted_iota(jnp.int32, (tm, 1), 0) + pl.program_id(0) * tm
    ang = row.astype(F32) * inv_ref[...]
    cos_ref[...] = jnp.cos(ang)
    sin_ref[...] = jnp.sin(ang)


def _rotary_table(inv_freq, seq_len):
    tm = min(PROJ_ROWS, seq_len)
    table = jax.ShapeDtypeStruct((seq_len, LANES), F32)
    return pl.pallas_call(
        functools.partial(_rotary_table_kernel, tm=tm),
        grid=(seq_len // tm,),
        in_specs=[pl.BlockSpec((1, LANES), lambda i: (0, 0))],
        out_specs=[pl.BlockSpec((tm, LANES), lambda i: (i, 0))] * 2,
        out_shape=[table, table],
        compiler_params=_params(("arbitrary",)),
        name="rotary_table",
    )(inv_freq)


def _proj_ret_kernel(x_ref, w_ref, cos_ref, sin_ref, ob_ref, of_ref, *, n_bf):
    xb = x_ref[...].astype(BF16)
    tn = PROJ_COLS
    for j in range(w_ref.shape[1] // tn):
        acc = jnp.dot(xb, w_ref[:, j * tn:(j + 1) * tn], preferred_element_type=F32)
        if j * tn < 4 * LANES:
            cos = cos_ref[...]
            sin = sin_ref[...]
            for base in range(0, tn, 2 * LANES):
                col = j * tn + base
                scale = RET_DK ** -0.5 if col < 2 * LANES else 1.0
                x1, x2 = acc[:, base:base + LANES], acc[:, base + LANES:base + 2 * LANES]
                ob_ref[:, col:col + LANES] = ((x1 * cos - x2 * sin) * scale).astype(BF16)
                ob_ref[:, col + LANES:col + 2 * LANES] = ((x1 * sin + x2 * cos) * scale).astype(BF16)
        elif (j + 1) * tn <= n_bf:
            ob_ref[:, j * tn:(j + 1) * tn] = acc.astype(BF16)
        else:
            of_ref[:, j * tn - n_bf:(j + 1) * tn - n_bf] = acc


def _gla_lower_bound(p, layer):
    e = jnp.exp(p - jnp.max(p, axis=0, keepdims=True))
    sm = e / jnp.sum(e, axis=0, keepdims=True)
    lb = jnp.zeros((1, p.shape[1]), F32)
    for r in range(1, layer + 1):
        lb = lb + sm[r:r + 1, :]
    return lb


def _proj_gla_kernel(x_ref, w_ref, lbp_ref, ob_ref, of_ref, *, layer):
    xb = x_ref[...].astype(BF16)
    d = w_ref.shape[1] // 5
    tn = PROJ_COLS
    lb = _gla_lower_bound(lbp_ref[...], layer)
    mean = 0.5 * (1.0 + lb)
    spread = 0.5 * (1.0 - lb)
    q_scale = 0.5 * HG_DK ** -0.5
    for j in range(5 * d // tn):
        col = j * tn
        acc = jnp.dot(xb, w_ref[:, col:col + tn], preferred_element_type=F32)
        if col < d:
            ob_ref[:, col:col + tn] = (acc * (q_scale * jnp.tanh(0.5 * acc) + q_scale)).astype(BF16)
        elif col < 3 * d:
            ob_ref[:, col:col + tn] = acc.astype(BF16)
        else:
            zc = col - 3 * d
            swing = spread[:, zc % d:zc % d + tn] * jnp.tanh(0.5 * acc)
            of_ref[:, zc:zc + tn] = jnp.log2(mean[:, zc % d:zc % d + tn] + swing)
            ob_ref[:, col:col + tn] = (spread[:, zc % d:zc % d + tn] - swing).astype(BF16)


def _project(kernel_fn, name, x, w_bf16, sides, side_specs, n_bf, n_f32, seq_len):
    t, k = x.shape
    n = w_bf16.shape[1]
    tm = min(PROJ_ROWS, seq_len)
    return pl.pallas_call(
        kernel_fn,
        grid=(t // tm,),
        in_specs=[pl.BlockSpec((tm, k), lambda i: (i, 0)), pl.BlockSpec((k, n), lambda i: (0, 0))] + side_specs,
        out_specs=[pl.BlockSpec((tm, n_bf), lambda i: (i, 0)),
                   pl.BlockSpec((tm, n_f32), lambda i: (i, 0))],
        out_shape=[jax.ShapeDtypeStruct((t, n_bf), BF16),
                   jax.ShapeDtypeStruct((t, n_f32), F32)],
        compiler_params=_params(("arbitrary",)),
        name=name,
    )(x, w_bf16, *sides)


def _ret_log_gamma(head):
    out = jnp.full(head.shape, math.log1p(-(2.0 ** -5.0)), F32)
    for h in range(1, RET_HEADS):
        out = jnp.where(head == h, math.log1p(-(2.0 ** (-5.0 - h))), out)
    return out


def _ret_lane_head():
    lane = lax.broadcasted_iota(jnp.int32, (1, 2 * LANES), 1)
    return (lane % LANES) // (RET_DK // 2)


def _ret_state_mask():
    shape = (2 * LANES, RET_HEADS * RET_DV)
    row_head = (lax.broadcasted_iota(jnp.int32, shape, 0) % LANES) // (RET_DK // 2)
    col_head = lax.broadcasted_iota(jnp.int32, shape, 1) // RET_DV
    return row_head == col_head


def _ret_chunk_decay():
    row = lax.broadcasted_iota(jnp.int32, (2 * LANES, 1), 0)
    return jnp.exp(_ret_log_gamma((row % LANES) // (RET_DK // 2)) * float(RET_CHUNK))


def _ret_bstate_kernel(k_ref, v_ref, sb_ref, s_ref, *, cps):
    c = RET_CHUNK

    @pl.when(pl.program_id(1) == 0)
    def _():
        s_ref[...] = jnp.zeros_like(s_ref)

    lg = _ret_log_gamma(_ret_lane_head())
    idx = lax.broadcasted_iota(jnp.int32, (c, 1), 0).astype(F32)
    k_decay = jnp.exp(lg * idx)
    chunk_decay = _ret_chunk_decay()
    mask = _ret_state_mask()
    for cc in reversed(range(cps)):
        rows = slice(cc * c, (cc + 1) * c)
        sb_ref[0, cc] = s_ref[...].astype(BF16)
        kb = (k_ref[rows, :] * k_decay).astype(BF16)
        upd = lax.dot_general(kb, v_ref[rows, :], _TN, preferred_element_type=F32)
        s_ref[...] = s_ref[...] * chunk_decay + jnp.where(mask, upd, 0.0)


def _ret_out_kernel(q_ref, k_ref, v_ref, g_ref, sb_ref, o_ref, s_ref, *, cps):
    c = RET_CHUNK

    @pl.when(pl.program_id(1) == 0)
    def _():
        s_ref[...] = jnp.zeros_like(s_ref)

    lane_head = _ret_lane_head()
    lg = _ret_log_gamma(lane_head)
    idx = lax.broadcasted_iota(jnp.int32, (c, 1), 0).astype(F32)
    q_decay_f = jnp.exp(lg * (idx + 1.0))
    q_decay_b = jnp.exp(lg * (float(c) - idx))
    k_decay = jnp.exp(lg * (float(c) - 1.0 - idx))
    chunk_decay = _ret_chunk_decay()
    mask = _ret_state_mask()
    ii =lax.broadcasted_iota(jnp.int32, (c, c), 0)
    jj = lax.broadcasted_iota(jnp.int32, (c, c), 1)
    dist = jnp.abs(ii - jj).astype(F32)
    intra_decay = [jnp.exp(math.log1p(-(2.0 ** (-5.0 - h))) * dist) for h in range(RET_HEADS)]

    for cc in range(cps):
        rows = slice(cc * c, (cc + 1) * c)
        q = q_ref[rows, :]
        k = k_ref[rows, :]
        v = v_ref[rows, :]
        qf = (q * q_decay_f).astype(BF16)
        qb = (q * q_decay_b).astype(BF16)
        cross = (jnp.dot(qf, s_ref[...].astype(BF16), preferred_element_type=F32)
                 + jnp.dot(qb, sb_ref[0, cc], preferred_element_type=F32))
        q_heads = jnp.concatenate([jnp.where(lane_head == h, q, jnp.zeros_like(q)) for h in range(RET_HEADS)],
                                  axis=0)
        scores = lax.dot_general(q_heads, k, _NT, preferred_element_type=F32)
        for h in range(RET_HEADS):
            s = scores[h * c:(h + 1) * c, :] * intra_decay[h]
            cols = slice(h * RET_DV, (h + 1) * RET_DV)
            o = jnp.dot(s.astype(BF16), v[:, cols], preferred_element_type=F32) + cross[:, cols]
            mu = jnp.mean(o, axis=-1, keepdims=True)
            d = o - mu
            var = jnp.mean(d * d, axis=-1, keepdims=True)
            gate = _silu(g_ref[rows, cols].astype(F32))
            o_ref[rows, cols] = (gate * (d * lax.rsqrt(var + LN_EPS))).astype(BF16)
        kf = (k * k_decay).astype(BF16)
        upd = lax.dot_general(kf, v, _TN, preferred_element_type=F32)
        s_ref[...] = s_ref[...] * chunk_decay + jnp.where(mask, upd, 0.0)


def _retention(proj, batch, seq_len):
    t = proj.shape[0]
    c = RET_CHUNK
    cps = min(RET_CHUNKS_PER_STEP, seq_len // c)
    rows = cps * c
    ns = seq_len // rows
    dv = RET_HEADS * RET_DV
    state_shape = (2 * LANES, dv)
    rev = lambda b, n: b * ns + (ns - 1 - n)
    fwd = lambda b, n: b * ns + n
    sb = pl.pallas_call(
        functools.partial(_ret_bstate_kernel, cps=cps),
        grid=(batch, ns),
        in_specs=[
            pl.BlockSpec((rows, 2 * LANES), lambda b, n: (rev(b, n), 1)),
            pl.BlockSpec((rows, dv), lambda b, n: (rev(b, n), 1)),
        ],
        out_specs=pl.BlockSpec((1, cps) + state_shape, lambda b, n: (b, ns - 1 - n, 0, 0)),
        out_shape=jax.ShapeDtypeStruct((batch, ns * cps) + state_shape, BF16),
        scratch_shapes=[pltpu.VMEM(state_shape, F32)],
        compiler_params=_params(("arbitrary", "arbitrary")),
        name="ret_bstate",
    )(proj, proj)
    return pl.pallas_call(
        functools.partial(_ret_out_kernel, cps=cps),
        grid=(batch, ns),
        in_specs=[
            pl.BlockSpec((rows, 2 * LANES), lambda b, n: (fwd(b, n), 0)),
            pl.BlockSpec((rows, 2 * LANES), lambda b, n: (fwd(b, n), 1)),
            pl.BlockSpec((rows, dv), lambda b, n: (fwd(b, n), 1)),
            pl.BlockSpec((rows, dv), lambda b, n: (fwd(b, n), 2)),
            pl.BlockSpec((1, cps) + state_shape, lambda b, n: (b, n, 0, 0)),
        ],
        out_specs=pl.BlockSpec((rows, dv), lambda b, n: (fwd(b, n), 0)),
        out_shape=jax.ShapeDtypeStruct((t, dv), BF16),
        scratch_shapes=[pltpu.VMEM(state_shape, F32)],
        compiler_params=_params(("arbitrary", "arbitrary")),
        name="ret_out",
    )(proj, proj, proj, proj, sb)


def _lru_kernel(xfp_ref, xf_ref, xfn_ref, xbp_ref, xb_ref, xbn_ref, cw_ref, cb_ref, gw_ref, gb_ref,
                lam_ref, hf_ref, hb_ref, xx_ref, a_ref, b_ref, h_ref, *, nt, ts, batch):
    i = pl.program_id(0)
    halo = SUBLANES
    lo = LRU_CONV // 2

    @pl.when(i == 0)
    def _():
        h_ref[...] = jnp.zeros_like(h_ref)

    def prepare(xp_ref, x_ref, xn_ref, tile, z, slot):
        for b in range(batch):
            xx_ref[0:halo, :] = jnp.where(tile == 0, 0.0, xp_ref[b])
            xx_ref[halo:halo + ts, :] = x_ref[b]
            xx_ref[halo + ts:2 * halo + ts, :] = jnp.where(tile == nt - 1, 0.0, xn_ref[b])
            xx = xx_ref[...]
            xc = cb_ref[...]
            for tap in range(LRU_CONV):
                shift = (lo - tap) % (ts + 2 * halo)
                moved = xx if shift == 0 else pltpu.roll(xx, shift, axis=0)
                xc = xc + cw_ref[tap:tap + 1, :] * moved[halo:halo + ts, :]
            for n in range(LRU_BLOCKS):
                cols = slice(n * LRU_BW, (n + 1) * LRU_BW)
                xn = xc[:, cols]
                g = (jnp.dot(xn.astype(BF16), gw_ref[z, n], preferred_element_type=F32)
                     + gb_ref[z, n:n + 1, :])
                r = _sigmoid(g[:, :LRU_BW])
                ig = _sigmoid(g[:, LRU_BW:])
                a = jnp.exp((-LRU_C) * r * _softplus(-lam_ref[z, :, cols]))
                a_ref[slot + b, :, cols] = a
                b_ref[slot + b, :, cols] = jnp.sqrt(1.0 - a * a) * (ig * xn)

    prepare(xfp_ref, xf_ref, xfn_ref, i, 0, 0)
    prepare(xbp_ref, xb_ref, xbn_ref, nt - 1 - i, 1, batch)

    def step(s, hs):
        out = []
        for k in range(2 * batch):
            row = s if k < batch else ts - 1 - s
            h = a_ref[k, pl.ds(row, 1), :] * hs[k] + b_ref[k, pl.ds(row, 1), :]
            if k < batch:
                hf_ref[k, pl.ds(row, 1), :] = h
            else:
                hb_ref[k - batch, pl.ds(row, 1), :] = h
            out.append(h)
        return tuple(out)

    hs = lax.fori_loop(0, ts, step, tuple(h_ref[k] for k in range(2 * batch)), unroll=8)
    for k in range(2 * batch):
        h_ref[k] = hs[k]


def _rglru(proj_f32, conv_w, conv_b, gate_w, gate_b, lam, batch, seq_len):
    w = LRU_BLOCKS * LRU_BW
    ts = min(LRU_ROWS, seq_len)
    nt = seq_len // ts
    rows8 = ts // SUBLANES
    last8 = seq_len // SUBLANES - 1
    x3 = proj_f32.reshape(batch, seq_len, proj_f32.shape[1])
    gw = jnp.concatenate([gate_w[:, 0], gate_w[:, 1]], axis=-1).astype(BF16)
    gb = jnp.concatenate([gate_b[:, 0], gate_b[:, 1]], axis=-1)
    bwd = lambda i: nt - 1 - i

    def tile_specs(tile):
        return [
            pl.BlockSpec((batch, SUBLANES, w), lambda i: (0, jnp.maximum(tile(i) * rows8 - 1, 0), 0)),
            pl.BlockSpec((batch, ts, w), lambda i: (0, tile(i), 0)),
            pl.BlockSpec((batch, SUBLANES, w), lambda i: (0, jnp.minimum((tile(i) + 1) * rows8, last8), 0)),
        ]

    full = lambda a: pl.BlockSpec(a.shape, lambda i: (0,) * a.ndim)
    cb = conv_b.reshape(1, w)
    lam3 = lam.reshape(2, 1, w)
    state = jax.ShapeDtypeStruct((batch, seq_len, w), F32)
    h_fwd, h_bwd = pl.pallas_call(
        functools.partial(_lru_kernel, nt=nt, ts=ts, batch=batch),
        grid=(nt,),
        in_specs=tile_specs(lambda i: i) + tile_specs(bwd) + [full(conv_w), full(cb), full(gw), full(gb),
                                                               full(lam3)],
        out_specs=[pl.BlockSpec((batch, ts, w), lambda i: (0, i, 0)),
                   pl.BlockSpec((batch, ts, w), lambda i: (0, bwd(i), 0))],
        out_shape=[state, state],
        scratch_shapes=[pltpu.VMEM((ts + 2 * SUBLANES, w), F32), pltpu.VMEM((2 * batch, ts, w), F32),
                        pltpu.VMEM((2 * batch, ts, w), F32), pltpu.VMEM((2 * batch, 1, w), F32)],
        compiler_params=_params(("arbitrary",)),
        name="lru_scan",
    )(x3, x3, x3, x3, x3, x3, conv_w, cb, gw, gb, lam3)
    return h_fwd.reshape(batch * seq_len, w), h_bwd.reshape(batch * seq_len, w)


def _gla_kernel(qf_ref, vf_ref, kf_ref, lf_ref, qb_ref, vb_ref, kb_ref, lb_ref, of_ref, ob_ref, s_ref,
                *, ts, batch):
    c = GLA_CHUNK
    nchunks = ts // c

    @pl.when(pl.program_id(0) == 0)
    def _():
        s_ref[...] = jnp.zeros_like(s_ref)

    head_cols = [slice(h * HG_DK, (h + 1) * HG_DK) for h in range(HG_HEADS)]
    hc = HG_HEADS * c
    ii = lax.broadcasted_iota(jnp.int32, (c, c), 0)
    jj = lax.broadcasted_iota(jnp.int32, (c, c), 1)
    si = lax.broadcasted_iota(jnp.int32, (hc, hc), 0)
    sj = lax.broadcasted_iota(jnp.int32, (hc, hc), 1)
    same_head = si // c == sj // c

    def stack(a):
        return jnp.concatenate([a[:, cols] for cols in head_cols], axis=0)

    tri = {False: (jj <= ii).astype(BF16), True: (jj >= ii).astype(BF16)}
    tri2 = {r: jnp.concatenate([m, m], axis=1) for r, m in tri.items()}
    zero_block = jnp.zeros((c, HG_DK), BF16)
    zero_state = jnp.zeros((HG_DK, HG_DV), BF16)
    keep = {False: jnp.logical_and(same_head, sj <= si), True: jnp.logical_and(same_head, sj >= si)}
    chains = ([(qf_ref, vf_ref, kf_ref, lf_ref, of_ref, b, False) for b in range(batch)]
              + [(qb_ref, vb_ref, kb_ref, lb_ref, ob_ref, b, True) for b in range(batch)])

    def gates(cc):
        work = []
        for chain, (q_ref, v_ref, k_ref, l_ref, o_ref, b, reverse) in enumerate(chains):
            rows = pl.ds(((nchunks - 1 - cc) if reverse else cc) * c, c)
            log_f = l_ref[b, rows, :]
            f_hi = log_f.astype(BF16)
            f_lo = (log_f - f_hi.astype(F32)).astype(BF16)
            work.append(dict(rows=rows, reverse=reverse, o_ref=o_ref, b=b, chain=chain,
                             key=k_ref[b, rows, :].astype(F32), qs=q_ref[b, rows, :].astype(F32),
                             v=v_ref[b, rows, :], split=(f_hi, f_lo)))
        return work

    def cumulate(work):
        for w in work:
            f_hi, f_lo = w["split"]
            w["bcum"] = jnp.dot(tri2[w["reverse"]], jnp.concatenate([f_lo, f_hi], axis=0),
                                preferred_element_type=F32)

    half = c // 2
    row = lax.broadcasted_iota(jnp.int32, (c, 1), 0)
    first_half = row < half

    def decays(work):
        for w in work:
            bcum = w["bcum"]
            rev = w["reverse"]
            end = 0 if rev else c - 1
            b_end = bcum[end:end + 1, :]
            ref = jnp.where(first_half, bcum[half // 2:half // 2 + 1, :], bcum[half + half // 2:half + half // 2 + 1, :])
            qe = w["qs"] * jnp.exp2(bcum - ref)
            ke = w["key"] * jnp.exp2(ref - bcum)
            boundary = bcum[half:half + 1, :] if rev else bcum[half - 1:half, :]
            cross = jnp.exp2(-jnp.abs(bcum - boundary))
            queries = first_half if rev else jnp.logical_not(first_half)
            w["q3"] = [jnp.where(first_half, qe, 0.0).astype(BF16), jnp.where(first_half, 0.0, qe).astype(BF16),
                       jnp.where(queries, w["qs"] * cross, 0.0).astype(BF16)]
            w["k3"] = [jnp.where(first_half, ke, 0.0), jnp.where(first_half, 0.0, ke),
                       jnp.where(queries, 0.0, w["key"] * cross)]
            w["qd"] = (w["qs"] * jnp.exp2(bcum)).astype(BF16)
            w["kd"] = (w["key"] * jnp.exp2(b_end - bcum)).astype(BF16)
            w["decay"] = jnp.exp2(b_end)

    def stack3(parts):
        return jnp.concatenate([jnp.concatenate([p[:, cols] for p in parts], axis=1) for cols in head_cols], axis=0)

    def scores(work):
        for w in work:
            w["att"] = jnp.dot(stack3(w["q3"]), stack3(w["k3"]).T.astype(BF16), preferred_element_type=F32)

    def intra(work):
        for w in work:
            att = jnp.where(keep[w["reverse"]], w["att"], 0.0).astype(BF16)
            w["intra"] = jnp.dot(att, stack(w["v"]), preferred_element_type=F32)

    def inter(work):
        for w in work:
            w["st"] = [s_ref[w["chain"], h] for h in range(HG_HEADS)]
            out = []
            for h in range(0, HG_HEADS, 2):
                sa = w["st"][h].T.astype(BF16)
                sb = w["st"][h + 1].T.astype(BF16)
                rhs = jnp.concatenate([jnp.concatenate([sa, zero_state], axis=1),
                                       jnp.concatenate([zero_state, sb], axis=1)], axis=0)
                pair = jnp.dot(w["qd"][:, h * HG_DK:(h + 2) * HG_DK], rhs, preferred_element_type=F32)
                out += [pair[:, :HG_DV], pair[:, HG_DV:]]
            w["inter"] = out

    def update(work):
        for w in work:
            upd = []
            for h in range(0, HG_HEADS, 2):
                ca, cb = head_cols[h], head_cols[h + 1]
                lhs = jnp.concatenate([w["v"][:, ca], w["v"][:, cb]], axis=0)
                rhs = jnp.concatenate([jnp.concatenate([w["kd"][:, ca], zero_block], axis=1),
                                       jnp.concatenate([zero_block, w["kd"][:, cb]], axis=1)], axis=0)
                pair = lax.dot_general(lhs, rhs, _TN, preferred_element_type=F32)
                upd += [pair[:, :HG_DK], pair[:, HG_DK:]]
            w["upd"] = upd

    def finish(work):
        for w in work:
            k = w["chain"]
            for h, cols in enumerate(head_cols):
                w["o_ref"][w["b"], w["rows"], cols] = w["intra"][h * c:(h + 1) * c, :] + w["inter"][h]
                s_ref[k, h] = w["st"][h] * w["decay"][:, cols] + w["upd"][h]

    cur = gates(0)
    cumulate(cur)
    decays(cur)
    for cc in range(nchunks):
        more = cc + 1 < nchunks
        scores(cur)
        nxt = gates(cc + 1) if more else None
        if more:
            cumulate(nxt)
        intra(cur)
        inter(cur)
        update(cur)
        if more:
            decays(nxt)
        finish(cur)
        cur = nxt


def _gla(proj_bf16, proj_f32, batch, seq_len):
    d = HG_HEADS * HG_DK
    ts = min(GLA_ROWS, seq_len)
    nt = seq_len // ts
    pb = proj_bf16.reshape(batch, seq_len, proj_bf16.shape[1])
    pf = proj_f32.reshape(batch, seq_len, proj_f32.shape[1])
    bwd = lambda i: nt - 1 - i
    blk = lambda tile, col: pl.BlockSpec((batch, ts, d), lambda i: (0, tile(i), col))
    fwd = lambda i: i
    out = jax.ShapeDtypeStruct((batch, seq_len, d), F32)
    o_fwd, o_bwd = pl.pallas_call(
        functools.partial(_gla_kernel, ts=ts, batch=batch),
        grid=(nt,),
        in_specs=[blk(fwd, 0), blk(fwd, 1), blk(fwd, 3), blk(fwd, 0),
                  blk(bwd, 0), blk(bwd, 1), blk(bwd, 4), blk(bwd, 1)],
        out_specs=[blk(fwd, 0), blk(bwd, 0)],
        out_shape=[out, out],
        scratch_shapes=[pltpu.VMEM((2 * batch, HG_HEADS, HG_DV, HG_DK), F32)],
        compiler_params=_params(("arbitrary",)),
        name="gla_scan",
    )(pb, pb, pb, pf, pb, pb, pb, pf)
    return o_fwd.reshape(batch * seq_len, d), o_bwd.reshape(batch * seq_len, d)


def _top2(p):
    v1 = jnp.maximum(jnp.maximum(p[0], p[1]), jnp.maximum(p[2], p[3]))
    i1 = jnp.where(p[0] == v1, 0, jnp.where(p[1] == v1, 1, jnp.where(p[2] == v1, 2, 3)))
    q = [jnp.where(i1 == k, -1.0, p[k]) for k in range(4)]
    v2 = jnp.maximum(jnp.maximum(q[0], q[1]), jnp.maximum(q[2], q[3]))
    i2 = jnp.where(q[0] == v2, 0, jnp.where(q[1] == v2, 1, jnp.where(q[2] == v2, 2, 3)))
    return v1, i1, v2, i2


def _outproj_kernel(*refs, even, alpha, tm):
    if even:
        ret_ref, hf_ref, hb_ref, gr_ref, w0_ref, w1_ref = refs[:6]
        rest = refs[6:]
        lru = ((hf_ref[...] + hb_ref[...]) * jax.nn.gelu(gr_ref[...])).astype(BF16)
        y = (jnp.dot(ret_ref[...], w0_ref[...], preferred_element_type=F32)
             + jnp.dot(lru, w1_ref[...], preferred_element_type=F32))
    else:
        of_ref, ob_ref, g_ref, nw_ref, w0_ref = refs[:5]
        rest = refs[5:]
        o = of_ref[...] + ob_ref[...]
        ms = jnp.mean(o * o, axis=-1, keepdims=True)
        mix = o * lax.rsqrt(ms + RMS_EPS) * nw_ref[...] * _silu(g_ref[...].astype(F32))
        y = jnp.dot(mix.astype(BF16), w0_ref[...], preferred_element_type=F32)
    h_ref, lnw_ref, lnb_ref, rw_ref, o_ref, info_ref, carry_ref, before_ref = rest
    i = pl.program_id(0)

    @pl.when(i == 0)
    def _():
        carry_ref[...] = jnp.zeros_like(carry_ref)
        tt = lax.broadcasted_iota(jnp.int32, (tm, tm), 0)
        uu = lax.broadcasted_iota(jnp.int32, (tm, tm), 1)
        before_ref[...] = (tt < uu).astype(BF16)

    h1 = _layer_norm(alpha * h_ref[...] + y, lnw_ref[...], lnb_ref[...])
    o_ref[...] = h1

    h_hi = h1.astype(BF16)
    h_lo = (h1 - h_hi.astype(F32)).astype(BF16)
    rw = rw_ref[...]
    r_hi = rw.astype(BF16)
    r_lo = (rw - r_hi.astype(F32)).astype(BF16)
    both = lax.dot_general(jnp.concatenate([r_hi, r_lo], axis=0), h_hi, _NT, preferred_element_type=F32)
    logits = (both[N_EXPERTS:, :] + lax.dot_general(r_hi, h_lo, _NT, preferred_element_type=F32)
              + both[:N_EXPERTS, :])
    ex = jnp.exp(logits - jnp.max(logits, axis=0, keepdims=True))
    probs = ex / jnp.sum(ex, axis=0, keepdims=True)
    best = None
    for g in range(N_GROUPS):
        rows = [probs[g * EXPERTS_PER_GROUP + k:g * EXPERTS_PER_GROUP + k + 1, :]
                for k in range(EXPERTS_PER_GROUP)]
        v1, i1, v2, i2 = _top2(rows)
        cand = (v1 + v2, v1, i1 + g * EXPERTS_PER_GROUP, v2, i2 + g * EXPERTS_PER_GROUP)
        if best is None:
            best = cand
        else:
            take = cand[0] > best[0]
            best = tuple(jnp.where(take, cn, bs) for cn, bs in zip(cand, best))
    _, v1, e1, v2, e2 = best
    denom = v1 + v2
    g1 = v1 / denom
    g2 = v2 / denom

    eid = lax.broadcasted_iota(jnp.int32, (N_EXPERTS, tm), 0)
    oh1 = (eid == e1).astype(F32)
    oh2 = (eid == e2).astype(F32)
    oh = oh1 + oh2
    base = carry_ref[:, 0:1] + jnp.dot(oh.astype(BF16), before_ref[...], preferred_element_type=F32)
    rank1 = jnp.sum(oh1 * base, axis=0, keepdims=True)
    rank2 = jnp.sum(oh2 * base, axis=0, keepdims=True)
    carry_ref[...] = carry_ref[...] + jnp.sum(oh, axis=1, keepdims=True)
    zero = jnp.zeros_like(g1)
    info_ref[...] = jnp.concatenate(
        [e1.astype(F32), e2.astype(F32), g1, g2, rank1, rank2, zero, zero], axis=0)


def _out_project(even, mixer_inputs, weights_bf16, h, ln_w, ln_b, router_wt, alpha):
    t, d = h.shape
    tm = TOKEN_ROWS
    row = lambda i: (i, 0)
    const = lambda i: (0, 0)
    if even:
        ret, h_fwd, h_bwd, proj_f32 = mixer_inputs
        w = h_fwd.shape[1]
        in_specs = [pl.BlockSpec((tm, ret.shape[1]), row), pl.BlockSpec((tm, w), row),
                    pl.BlockSpec((tm, w), row), pl.BlockSpec((tm, w), lambda i: (i, 1))]
        args = [ret, h_fwd, h_bwd, proj_f32]
    else:
        o_fwd, o_bwd, proj_bf16, norm_w = mixer_inputs
        in_specs = [pl.BlockSpec((tm, d), row), pl.BlockSpec((tm, d), row),
                    pl.BlockSpec((tm, d), lambda i: (i, 2)), pl.BlockSpec((1, d), const)]
        args = [o_fwd, o_bwd, proj_bf16, norm_w.reshape(1, d)]
    in_specs += [pl.BlockSpec(wm.shape, const) for wm in weights_bf16]
    in_specs += [pl.BlockSpec((tm, d), row), pl.BlockSpec((1, d), const), pl.BlockSpec((1, d), const),
                 pl.BlockSpec((N_EXPERTS, d), const)]
    args += list(weights_bf16) + [h, ln_w.reshape(1, d), ln_b.reshape(1, d), router_wt]
    return pl.pallas_call(
        functools.partial(_outproj_kernel, even=even, alpha=alpha, tm=tm),
        grid=(t // tm,),
        in_specs=in_specs,
        out_specs=[pl.BlockSpec((tm, d), row),
                   pl.BlockSpec((SUBLANES, tm), lambda i: (0, i))],
        out_shape=[jax.ShapeDtypeStruct((t, d), F32),
                   jax.ShapeDtypeStruct((SUBLANES, t), F32)],
        scratch_shapes=[pltpu.VMEM((N_EXPERTS, LANES), F32), pltpu.VMEM((tm, tm), BF16)],
        compiler_params=_params(("arbitrary",)),
        name="out_proj_router",
    )(*args)


def _pack_halves(x):
    n = x.shape[1] // 2
    hi = pltpu.bitcast(x[:, :n], jnp.uint32)
    lo = pltpu.bitcast(x[:, n:], jnp.uint32)
    return hi | (lo >> 16)


def _unpack_halves(p):
    hi = pltpu.bitcast(p & jnp.uint32(0xFFFF0000), F32)
    lo = pltpu.bitcast(p << 16, F32)
    return jnp.concatenate([hi.astype(BF16), lo.astype(BF16)], axis=1)


def _seg_expert_kernel(first_ref, count_ref, tail_ref, x_ref, wg_ref, wu_ref, wd_ref, y_ref,
                       xbuf, ybuf, wgb_ref, wub_ref, wdb_ref, semx, semy):
    e = pl.program_id(0)
    wgb_ref[...] = wg_ref[...].astype(BF16)
    wub_ref[...] = wu_ref[...].astype(BF16)
    wdb_ref[...] = wd_ref[...].astype(BF16)
    first = first_ref[e]
    n_used = tail_ref[0]

    def rows(g):
        return pl.ds(pl.multiple_of(g * MOE_ROWS, MOE_ROWS), MOE_ROWS)

    def fetch(g):
        return pltpu.make_async_copy(x_ref.at[rows(g)], xbuf.at[g % 2], semx.at[g % 2])

    def put(g):
        return pltpu.make_async_copy(ybuf.at[g % 2], y_ref.at[rows(g)], semy.at[g % 2])

    @pl.when(e == 0)
    def _():
        fetch(0).start()

    def body(g, carry):
        @pl.when(g + 1 < n_used)
        def _():
            fetch(g + 1).start()

        fetch(g).wait()

        @pl.when(g >= 2)
        def _():
            put(g - 2).wait()

        x = _unpack_halves(xbuf[g % 2])
        gate = jnp.dot(x, wgb_ref[...], preferred_element_type=F32)
        up = jnp.dot(x, wub_ref[...], preferred_element_type=F32)
        hid = (_silu(gate) * up).astype(BF16)
        y = jnp.dot(hid, wdb_ref[...], preferred_element_type=F32)
        ybuf[g % 2] = _pack_halves(y.astype(BF16).astype(F32))
        put(g).start()
        return carry

    lax.fori_loop(first, first + count_ref[e], body, 0)

    @pl.when(e == pl.num_programs(0) - 1)
    def _():
        @pl.when(n_used >= 2)
        def _():
            put(n_used - 2).wait()

        put(n_used - 1).wait()
        ybuf[0] = jnp.zeros_like(ybuf[0])

        def clear(b, carry):
            cp = pltpu.make_async_copy(
                ybuf.at[0], y_ref.at[pl.ds(pl.multiple_of(b * MOE_ROWS, MOE_ROWS), MOE_ROWS)], semy.at[0])
            cp.start()
            cp.wait()
            return carry

        lax.fori_loop(tail_ref[0], y_ref.shape[0] // MOE_ROWS, clear, 0)


def _seg_experts(xb, seg_first, seg_count, n_valid, w_gate, w_up, w_down, layer):
    p, half = xb.shape
    d = 2 * half
    de = w_gate.shape[3]
    weight = lambda shape: pl.BlockSpec((None, None) + shape, lambda e, a, b, c: (layer, e, 0, 0))
    return pl.pallas_call(
        _seg_expert_kernel,
        grid_spec=pltpu.PrefetchScalarGridSpec(
            num_scalar_prefetch=3,
            grid=(N_EXPERTS,),
            in_specs=[pl.BlockSpec(memory_space=pl.ANY), weight((d, de)), weight((d, de)), weight((de, d))],
            out_specs=pl.BlockSpec(memory_space=pl.ANY),
            scratch_shapes=[pltpu.VMEM((2, MOE_ROWS, half), jnp.uint32),
                            pltpu.VMEM((2, MOE_ROWS, half), jnp.uint32),
                            pltpu.VMEM((d, de), BF16), pltpu.VMEM((d, de), BF16), pltpu.VMEM((de, d), BF16),
                            pltpu.SemaphoreType.DMA((2,)), pltpu.SemaphoreType.DMA((2,))],
        ),
        out_shape=jax.ShapeDtypeStruct((p, half), jnp.uint32),
        compiler_params=_params(("arbitrary",)),
        name="moe_experts",
    )(seg_first, seg_count, n_valid, xb, w_gate, w_up, w_down)


def _clear_padding_blocks(zs_ref, xb_ref, zero_ref, sem):
    zero_ref[...] = jnp.zeros_like(zero_ref)

    def clear(row):
        start = pl.multiple_of(row, MOE_ROWS)
        return pltpu.make_async_copy(zero_ref, xb_ref.at[pl.ds(start, MOE_ROWS)], sem)

    for e in range(N_EXPERTS):
        clear(zs_ref[e]).start()
    for e in range(N_EXPERTS):
        clear(zs_ref[e]).wait()

    def clear_tail(b, carry):
        clear(b * MOE_ROWS).start()
        clear(b * MOE_ROWS).wait()
        return carry

    lax.fori_loop(zs_ref[N_EXPERTS], xb_ref.shape[0] // MOE_ROWS, clear_tail, 0)


def _run_copy(src_ref, src_row, dst_ref, dst_row, sem):
    src = pl.multiple_of(src_row, SUBLANES)
    dst = pl.multiple_of(dst_row, SUBLANES)
    return pltpu.make_async_copy(src_ref.at[pl.ds(src, SUBLANES)], dst_ref.at[pl.ds(dst, SUBLANES)], sem)


def _start_runs(count, make_copy):
    def pair(p, carry):
        make_copy(2 * p).start()
        make_copy(2 * p + 1).start(priority=1)
        return carry

    lax.fori_loop(0, count // 2, pair, 0)

    @pl.when(count % 2 == 1)
    def _():
        make_copy(count - 1).start()


def _sort_dispatch_kernel(dst_ref, nch_ref, zs_ref, h_ref, lp_ref, xb_ref, xs_ref, zero_ref, sem, *, tm, nck):
    i = pl.program_id(0)

    @pl.when(i == 0)
    def _():
        _clear_padding_blocks(zs_ref, xb_ref, zero_ref, sem.at[0])

    sub = SORT_TILES_PER_STEP
    last = pl.num_programs(0) - 1

    def wait_runs(buf, count):
        def wait(j, carry):
            _run_copy(xs_ref.at[buf], 0, xb_ref, 0, sem.at[buf]).wait()
            return carry

        lax.fori_loop(0, count, wait, 0)

    pos = lax.broadcasted_iota(jnp.int32, (xs_ref.shape[1], tm), 0)
    for s in range(sub):
        tile = i * sub + s
        buf = (i % 2) * sub + s

        @pl.when(i >= 2)
        def _():
            wait_runs(buf, nch_ref[jnp.maximum(tile - 2 * sub, 0)])

        lp = lp_ref[:, s * tm:(s + 1) * tm]
        perm = jnp.logical_or(pos == lp[0:1, :], pos == lp[1:2, :]).astype(BF16)
        rows = h_ref[s * tm:(s + 1) * tm, :].astype(BF16)
        xs_ref[buf] = _pack_halves(jnp.dot(perm, rows, preferred_element_type=F32))

        _start_runs(nch_ref[tile], lambda j: _run_copy(xs_ref.at[buf], j * SUBLANES, xb_ref,
                                                       dst_ref[tile * nck + j], sem.at[buf]))

    @pl.when(i == last)
    def _():
        for s in range(sub):
            @pl.when(i >= 1)
            def _():
                wait_runs((1 - i % 2) * sub + s, nch_ref[jnp.maximum((i - 1) * sub + s, 0)])

            wait_runs((i % 2) * sub + s, nch_ref[i * sub + s])


def _sort_dispatch(h, lp_rows, chunk_dst, n_chunks, zero_start, n_rows, tm, sorted_rows):
    t, d = h.shape
    nck = sorted_rows // SUBLANES
    sub = SORT_TILES_PER_STEP
    return pl.pallas_call(
        functools.partial(_sort_dispatch_kernel, tm=tm, nck=nck),
        grid_spec=pltpu.PrefetchScalarGridSpec(
            num_scalar_prefetch=3,
            grid=(t // (sub * tm),),
            in_specs=[pl.BlockSpec((sub * tm, d), lambda i, a, b, c: (i, 0)),
                      pl.BlockSpec((2, sub * tm), lambda i, a, b, c: (0, i))],
            out_specs=pl.BlockSpec(memory_space=pl.ANY),
            scratch_shapes=[pltpu.VMEM((2 * sub, sorted_rows, d // 2), jnp.uint32),
                            pltpu.VMEM((MOE_ROWS, d // 2), jnp.uint32), pltpu.SemaphoreType.DMA((2 * sub,))],
        ),
        out_shape=jax.ShapeDtypeStruct((n_rows, d // 2), jnp.uint32),
        compiler_params=_params(("arbitrary",)),
        name="moe_dispatch",
    )(chunk_dst, n_chunks, zero_start, h, lp_rows)


def _sort_combine_kernel(src_ref, nch_ref, h_ref, lp_ref, gates_ref, lnw_ref, lnb_ref, yb_ref, o_ref,
                         ys_ref, sem, *, tm, nck, alpha):
    i = pl.program_id(0)
    sub = SORT_TILES_PER_STEP

    def fetch(tile, buf):
        _start_runs(nch_ref[tile], lambda j: _run_copy(yb_ref, src_ref[tile * nck + j], ys_ref.at[buf],
                                                       j * SUBLANES, sem.at[buf]))

    @pl.when(i == 0)
    def _():
        ys_ref[...] = jnp.zeros_like(ys_ref)
        for s in range(sub):
            fetch(s, s)

    @pl.when(i + 1 < pl.num_programs(0))
    def _():
        for s in range(sub):
            fetch((i + 1) * sub + s, (1 - i % 2) * sub + s)

    pos = lax.broadcasted_iota(jnp.int32, (tm, ys_ref.shape[1]), 1)
    for s in range(sub):
        buf = (i % 2) * sub + s
        rows = slice(s * tm, (s + 1) * tm)

        def wait(j, carry):
            _run_copy(yb_ref, 0, ys_ref.at[buf], 0, sem.at[buf]).wait()
            return carry

        lax.fori_loop(0, nch_ref[i * sub + s], wait, 0)

        lp = lp_ref[rows, :]
        pick = (jnp.where(pos == lp[:, 0:1], gates_ref[rows, 0:1], 0.0)
                + jnp.where(pos == lp[:, 1:2], gates_ref[rows, 1:2], 0.0))
        y = jnp.dot(pick.astype(BF16), _unpack_halves(ys_ref[buf]), preferred_element_type=F32)
        o_ref[rows, :] = _layer_norm(alpha * h_ref[rows, :] + y, lnw_ref[...], lnb_ref[...])


def _sort_combine(h, yb, lp_cols, gates, chunk_src, n_chunks, ln_w, ln_b, alpha, tm, sorted_rows):
    t, d = h.shape
    nck = sorted_rows // SUBLANES
    step = SORT_TILES_PER_STEP * tm
    row = lambda i, a, b: (i, 0)
    const = lambda i, a, b: (0, 0)
    return pl.pallas_call(
        functools.partial(_sort_combine_kernel, tm=tm, nck=nck, alpha=alpha),
        grid_spec=pltpu.PrefetchScalarGridSpec(
            num_scalar_prefetch=2,
            grid=(t // step,),
            in_specs=[pl.BlockSpec((step, d), row), pl.BlockSpec((step, 2), row), pl.BlockSpec((step, 2), row),
                      pl.BlockSpec((1, d), const), pl.BlockSpec((1, d), const),
                      pl.BlockSpec(memory_space=pl.ANY)],
            out_specs=pl.BlockSpec((step, d), row),
            scratch_shapes=[pltpu.VMEM((2 * SORT_TILES_PER_STEP, sorted_rows, d // 2), jnp.uint32),
                            pltpu.SemaphoreType.DMA((2 * SORT_TILES_PER_STEP,))],
        ),
        out_shape=jax.ShapeDtypeStruct((t, d), F32),
        compiler_params=_params(("arbitrary",)),
        name="moe_combine",
    )(chunk_src, n_chunks, h, lp_cols, gates, ln_w.reshape(1, d), ln_b.reshape(1, d), yb)


def _moe(h1, info, w_gate, w_up, w_down, layer, ln_w, ln_b, alpha):
    t = h1.shape[0]
    tm = SORT_ROWS
    ntile = t // tm
    run_pad = SUBLANES - 1
    sorted_rows = 2 * tm + LANES
    nck = sorted_rows // SUBLANES
    n_rows = (2 * t + ntile * N_EXPERTS * run_pad + MOE_ROWS - 1) // MOE_ROWS * MOE_ROWS + N_EXPERTS * MOE_ROWS
    i32 = jnp.int32
    e1, e2 = info[0].astype(i32), info[1].astype(i32)
    rank1, rank2 = info[4].astype(i32), info[5].astype(i32)
    expert_ids = jnp.arange(N_EXPERTS, dtype=i32)[None, :]
    oh1 = e1[:, None] == expert_ids
    oh2 = e2[:, None] == expert_ids
    cnt = jnp.logical_or(oh1, oh2).astype(i32).reshape(ntile, tm, N_EXPERTS).sum(axis=1)
    cnt8 = (cnt + run_pad) // SUBLANES * SUBLANES
    local = jnp.cumsum(cnt8, axis=1) - cnt8
    before = jnp.cumsum(cnt, axis=0) - cnt
    seg = cnt8.sum(axis=0)
    seg_pad = (seg + MOE_ROWS - 1) // MOE_ROWS * MOE_ROWS
    ends = jnp.cumsum(seg_pad)
    slot = (ends - seg_pad)[None, :] + jnp.cumsum(cnt8, axis=0) - cnt8
    shift = jnp.repeat(local - before, tm, axis=0)
    lp1 = jnp.sum(jnp.where(oh1, shift, 0), axis=1) + rank1
    lp2 = jnp.sum(jnp.where(oh2, shift, 0), axis=1) + rank2
    chunk_row = jnp.arange(nck, dtype=i32) * SUBLANES
    run_of = jnp.sum(((local + cnt8)[:, None, :] <= chunk_row[None, :, None]).astype(i32), axis=2)
    run_of = jnp.minimum(run_of, N_EXPERTS - 1)
    chunk_slot = jnp.sum(jnp.where(run_of[:, :, None] == expert_ids[None], (slot - local)[:, None, :], 0),
                         axis=2) + chunk_row[None, :]
    n_chunks = cnt8.sum(axis=1) // SUBLANES
    n_valid = (ends[-1] // MOE_ROWS).astype(i32).reshape(1)
    zero_start = jnp.concatenate([jnp.maximum(ends - MOE_ROWS, 0), n_valid]).astype(i32)
    gates = jnp.stack([info[2], info[3]], axis=1)
    chunk_slot = chunk_slot.reshape(-1).astype(i32)
    seg_first = ((ends - seg_pad) // MOE_ROWS).astype(i32)
    seg_count = (seg_pad // MOE_ROWS).astype(i32)

    xb = _sort_dispatch(h1, jnp.stack([lp1, lp2], axis=0), chunk_slot, n_chunks, zero_start, n_rows, tm,
                        sorted_rows)
    yb = _seg_experts(xb, seg_first, seg_count, n_valid, w_gate, w_up, w_down, layer)
    return _sort_combine(h1, yb, jnp.stack([lp1, lp2], axis=1), gates, chunk_slot, n_chunks, ln_w, ln_b,
                         alpha, tm, sorted_rows)


def _rotary_column_order(w_in):
    d = w_in.shape[0]
    nq = RET_HEADS * RET_DK

    def perm(w):
        return w.reshape(d, RET_HEADS, RET_DK // 2, 2).transpose(0, 3, 1, 2).reshape(d, nq)

    return jnp.concatenate([perm(w_in[:, :nq]), perm(w_in[:, nq:2 * nq]), w_in[:, 2 * nq:]], axis=1)


def kernel(x, w_in_even, w_out_even, lru_conv_w, lru_conv_b, lru_gate_w, lru_gate_b, lru_lambda,
           w_in_odd, w_out_odd, hg_lower_bounds, hg_norm_w, ln_w, ln_b, router_w,
           moe_w_gate, moe_w_up, moe_w_down):
    batch, seq_len, d = x.shape
    depth = ln_w.shape[0]
    alpha = (2.0 * depth) ** 0.25
    t = batch * seq_len
    h = x.reshape(t, d)
    router_wt = router_w.T
    half = RET_DK // 2
    inv_freq = ROPE_BASE ** (-jnp.arange(0, RET_DK, 2, dtype=F32) / RET_DK)
    inv_freq = jnp.tile(inv_freq, LANES // half).reshape(1, LANES)
    nret = RET_HEADS * RET_DV
    tm = min(PROJ_ROWS, seq_len)
    tiles_per_seq = seq_len // tm
    cos, sin = _rotary_table(inv_freq, seq_len)
    pos_spec = pl.BlockSpec((tm, LANES), lambda i: (i % tiles_per_seq, 0))

    for layer in range(depth):
        j = layer // 2
        if layer % 2 == 0:
            w_in = _rotary_column_order(w_in_even[j]).astype(BF16)
            n_bf = 2 * RET_HEADS * RET_DK + 2 * nret
            proj_kernel = functools.partial(_proj_ret_kernel, n_bf=n_bf)
            proj_b, proj_f = _project(proj_kernel, "in_proj", h, w_in, [cos, sin], [pos_spec, pos_spec], n_bf,
                                      w_in.shape[1] - n_bf, seq_len)
            ret = _retention(proj_b, batch, seq_len)
            h_fwd, h_bwd = _rglru(proj_f, lru_conv_w[j], lru_conv_b[j], lru_gate_w[j], lru_gate_b[j],
                                  lru_lambda[j], batch, seq_len)
            w_out = w_out_even[j].astype(BF16)
            mixer_inputs, weights = (ret, h_fwd, h_bwd, proj_f), [w_out[:nret], w_out[nret:]]
        else:
            w = w_in_odd[j]
            w_in = jnp.concatenate([w[:, :2 * d], w[:, 4 * d:], w[:, 2 * d:4 * d]], axis=1).astype(BF16)
            proj_kernel = functools.partial(_proj_gla_kernel, layer=layer)
            bounds_spec = pl.BlockSpec(hg_lower_bounds.shape, lambda i: (0, 0))
            proj_b, proj_f = _project(proj_kernel, "in_proj_gla", h, w_in, [hg_lower_bounds], [bounds_spec],
                                      5 * d, 2 * d, seq_len)
            o_fwd, o_bwd = _gla(proj_b, proj_f, batch, seq_len)
            mixer_inputs, weights = (o_fwd, o_bwd, proj_b, hg_norm_w[j]), [w_out_odd[j].astype(BF16)]
        h1, info = _out_project(layer % 2 == 0, mixer_inputs, weights, h, ln_w[layer, 0], ln_b[layer, 0],
                                router_wt, alpha)
        h = _moe(h1, info, moe_w_gate, moe_w_up, moe_w_down, layer, ln_w[layer, 1], ln_b[layer, 1], alpha)
    return h.reshape(batch, seq_len, d)
```

```python
import functools
import math

import jax
import jax.numpy as jnp
from jax import lax
from jax.experimental import pallas as pl
from jax.experimental.pallas import tpu as pltpu

F32 = jnp.float32
BF16 = jnp.bfloat16

RET_HEADS = 4
RET_DK = 64
RET_DV = 128
RET_CHUNK = 128
ROPE_BASE = 10000.0
LRU_BLOCKS = 4
LRU_BW = 128
LRU_CONV = 4
LRU_C = 8.0
HG_HEADS = 8
HG_DK = 128
HG_DV = 128
N_EXPERTS = 16
N_GROUPS = 4
EXPERTS_PER_GROUP = 4
LN_EPS = 1e-5
RMS_EPS = 1e-6

LANES = 128
SUBLANES = 8
PROJ_ROWS = 512
PROJ_COLS = 512
RET_CHUNKS_PER_STEP = 8
LRU_ROWS = 512
GLA_CHUNK = 32
GLA_ROWS = 256
TOKEN_ROWS = 512
SORT_ROWS = 256
SORT_TILES_PER_STEP = 2
MOE_ROWS = 512
VMEM_LIMIT = 56 * 1024 * 1024

_NT = (((1,), (1,)), ((), ()))
_TN = (((0,), (0,)), ((), ()))


def _params(sem):
    return pltpu.CompilerParams(dimension_semantics=sem, vmem_limit_bytes=VMEM_LIMIT)


def _sigmoid(x):
    return 0.5 * jnp.tanh(0.5 * x) + 0.5


def _silu(x):
    return x * _sigmoid(x)


def _softplus(x):
    return jnp.maximum(x, 0.0) + jnp.log1p(jnp.exp(-jnp.abs(x)))


def _layer_norm(u, w, b):
    mu = jnp.mean(u, axis=-1, keepdims=True)
    d = u - mu
    var = jnp.mean(d * d, axis=-1, keepdims=True)
    return d * lax.rsqrt(var + LN_EPS) * w + b


def _rotary_table_kernel(inv_ref, cos_ref, sin_ref, *, tm):
    row = lax.broadcasted_iota(jnp.int32, (tm, 1), 0) + pl.program_id(0) * tm
    ang = row.astype(F32) * inv_ref[...]
    cos_ref[...] = jnp.cos(ang)
    sin_ref[...] = jnp.sin(ang)


def _rotary_table(inv_freq, seq_len):
    tm = min(PROJ_ROWS, seq_len)
    table = jax.ShapeDtypeStruct((seq_len, LANES), F32)
    return pl.pallas_call(
        functools.partial(_rotary_table_kernel, tm=tm),
        grid=(seq_len // tm,),
        in_specs=[pl.BlockSpec((1, LANES), lambda i: (0, 0))],
        out_specs=[pl.BlockSpec((tm, LANES), lambda i: (i, 0))] * 2,
        out_shape=[table, table],
        compiler_params=_params(("arbitrary",)),
        name="rotary_table",
    )(inv_freq)


def _proj_ret_kernel(x_ref, w_ref, cos_ref, sin_ref, ob_ref, of_ref, *, n_bf):
    xb = x_ref[...].astype(BF16)
    tn = PROJ_COLS
    for j in range(w_ref.shape[1] // tn):
        acc = jnp.dot(xb, w_ref[:, j * tn:(j + 1) * tn], preferred_element_type=F32)
        if j * tn < 4 * LANES:
            cos = cos_ref[...]
            sin = sin_ref[...]
            for base in range(0, tn, 2 * LANES):
                col = j * tn + base
                scale = RET_DK ** -0.5 if col < 2 * LANES else 1.0
                x1, x2 = acc[:, base:base + LANES], acc[:, base + LANES:base + 2 * LANES]
                ob_ref[:, col:col + LANES] = ((x1 * cos - x2 * sin) * scale).astype(BF16)
                ob_ref[:, col + LANES:col + 2 * LANES] = ((x1 * sin + x2 * cos) * scale).astype(BF16)
        elif (j + 1) * tn <= n_bf:
            ob_ref[:, j * tn:(j + 1) * tn] = acc.astype(BF16)
        else:
            of_ref[:, j * tn - n_bf:(j + 1) * tn - n_bf] = acc


def _gla_lower_bound(p, layer):
    e = jnp.exp(p - jnp.max(p, axis=0, keepdims=True))
    sm = e / jnp.sum(e, axis=0, keepdims=True)
    lb = jnp.zeros((1, p.shape[1]), F32)
    for r in range(1, layer + 1):
        lb = lb + sm[r:r + 1, :]
    return lb


def _proj_gla_kernel(x_ref, w_ref, lbp_ref, ob_ref, of_ref, *, layer):
    xb = x_ref[...].astype(BF16)
    d = w_ref.shape[1] // 5
    tn = PROJ_COLS
    lb = _gla_lower_bound(lbp_ref[...], layer)
    mean = 0.5 * (1.0 + lb)
    spread = 0.5 * (1.0 - lb)
    q_scale = 0.5 * HG_DK ** -0.5
    for j in range(5 * d // tn):
        col = j * tn
        acc = jnp.dot(xb, w_ref[:, col:col + tn], preferred_element_type=F32)
        if col < d:
            ob_ref[:, col:col + tn] = (acc * (q_scale * jnp.tanh(0.5 * acc) + q_scale)).astype(BF16)
        elif col < 3 * d:
            ob_ref[:, col:col + tn] = acc.astype(BF16)
        else:
            zc = col - 3 * d
            swing = spread[:, zc % d:zc % d + tn] * jnp.tanh(0.5 * acc)
            of_ref[:, zc:zc + tn] = jnp.log2(mean[:, zc % d:zc % d + tn] + swing)
            ob_ref[:, col:col + tn] = (spread[:, zc % d:zc % d + tn] - swing).astype(BF16)


def _project(kernel_fn, name, x, w_bf16, sides, side_specs, n_bf, n_f32, seq_len):
    t, k = x.shape
    n = w_bf16.shape[1]
    tm = min(PROJ_ROWS, seq_len)
    return pl.pallas_call(
        kernel_fn,
        grid=(t // tm,),
        in_specs=[pl.BlockSpec((tm, k), lambda i: (i, 0)), pl.BlockSpec((k, n), lambda i: (0, 0))] + side_specs,
        out_specs=[pl.BlockSpec((tm, n_bf), lambda i: (i, 0)),
                   pl.BlockSpec((tm, n_f32), lambda i: (i, 0))],
        out_shape=[jax.ShapeDtypeStruct((t, n_bf), BF16),
                   jax.ShapeDtypeStruct((t, n_f32), F32)],
        compiler_params=_params(("arbitrary",)),
        name=name,
    )(x, w_bf16, *sides)


def _ret_log_gamma(head):
    out = jnp.full(head.shape, math.log1p(-(2.0 ** -5.0)), F32)
    for h in range(1, RET_HEADS):
        out = jnp.where(head == h, math.log1p(-(2.0 ** (-5.0 - h))), out)
    return out


def _ret_lane_head():
    lane = lax.broadcasted_iota(jnp.int32, (1, 2 * LANES), 1)
    return (lane % LANES) // (RET_DK // 2)


def _ret_state_mask():
    shape = (2 * LANES, RET_HEADS * RET_DV)
    row_head = (lax.broadcasted_iota(jnp.int32, shape, 0) % LANES) // (RET_DK // 2)
    col_head = lax.broadcasted_iota(jnp.int32, shape, 1) // RET_DV
    return row_head == col_head


def _ret_chunk_decay():
    row = lax.broadcasted_iota(jnp.int32, (2 * LANES, 1), 0)
    return jnp.exp(_ret_log_gamma((row % LANES) // (RET_DK // 2)) * float(RET_CHUNK))


def _ret_bstate_kernel(k_ref, v_ref, sb_ref, s_ref, *, cps):
    c = RET_CHUNK

    @pl.when(pl.program_id(1) == 0)
    def _():
        s_ref[...] = jnp.zeros_like(s_ref)

    lg = _ret_log_gamma(_ret_lane_head())
    idx = lax.broadcasted_iota(jnp.int32, (c, 1), 0).astype(F32)
    k_decay = jnp.exp(lg * idx)
    chunk_decay = _ret_chunk_decay()
    mask = _ret_state_mask()
    for cc in reversed(range(cps)):
        rows = slice(cc * c, (cc + 1) * c)
        sb_ref[0, cc] = s_ref[...].astype(BF16)
        kb = (k_ref[rows, :] * k_decay).astype(BF16)
        upd = lax.dot_general(kb, v_ref[rows, :], _TN, preferred_element_type=F32)
        s_ref[...] = s_ref[...] * chunk_decay + jnp.where(mask, upd, 0.0)


def _ret_out_kernel(q_ref, k_ref, v_ref, g_ref, sb_ref, o_ref, s_ref, *, cps):
    c = RET_CHUNK

    @pl.when(pl.program_id(1) == 0)
    def _():
        s_ref[...] = jnp.zeros_like(s_ref)

    lane_head = _ret_lane_head()
    lg = _ret_log_gamma(lane_head)
    idx = lax.broadcasted_iota(jnp.int32, (c, 1), 0).astype(F32)
    q_decay_f = jnp.exp(lg * (idx + 1.0))
    q_decay_b = jnp.exp(lg * (float(c) - idx))
    k_decay = jnp.exp(lg * (float(c) - 1.0 - idx))
    chunk_decay = _ret_chunk_decay()
    mask = _ret_state_mask()
    ii =lax.broadcasted_iota(jnp.int32, (c, c), 0)
    jj = lax.broadcasted_iota(jnp.int32, (c, c), 1)
    dist = jnp.abs(ii - jj).astype(F32)
    intra_decay = [jnp.exp(math.log1p(-(2.0 ** (-5.0 - h))) * dist) for h in range(RET_HEADS)]

    for cc in range(cps):
        rows = slice(cc * c, (cc + 1) * c)
        q = q_ref[rows, :]
        k = k_ref[rows, :]
        v = v_ref[rows, :]
        qf = (q * q_decay_f).astype(BF16)
        qb = (q * q_decay_b).astype(BF16)
        cross = (jnp.dot(qf, s_ref[...].astype(BF16), preferred_element_type=F32)
                 + jnp.dot(qb, sb_ref[0, cc], preferred_element_type=F32))
        q_heads = jnp.concatenate([jnp.where(lane_head == h, q, jnp.zeros_like(q)) for h in range(RET_HEADS)],
                                  axis=0)
        scores = lax.dot_general(q_heads, k, _NT, preferred_element_type=F32)
        for h in range(RET_HEADS):
            s = scores[h * c:(h + 1) * c, :] * intra_decay[h]
            cols = slice(h * RET_DV, (h + 1) * RET_DV)
            o = jnp.dot(s.astype(BF16), v[:, cols], preferred_element_type=F32) + cross[:, cols]
            mu = jnp.mean(o, axis=-1, keepdims=True)
            d = o - mu
            var = jnp.mean(d * d, axis=-1, keepdims=True)
            gate = _silu(g_ref[rows, cols].astype(F32))
            o_ref[rows, cols] = (gate * (d * lax.rsqrt(var + LN_EPS))).astype(BF16)
        kf = (k * k_decay).astype(BF16)
        upd = lax.dot_general(kf, v, _TN, preferred_element_type=F32)
        s_ref[...] = s_ref[...] * chunk_decay + jnp.where(mask, upd, 0.0)


def _retention(proj, batch, seq_len):
    t = proj.shape[0]
    c = RET_CHUNK
    cps = min(RET_CHUNKS_PER_STEP, seq_len // c)
    rows = cps * c
    ns = seq_len // rows
    dv = RET_HEADS * RET_DV
    state_shape = (2 * LANES, dv)
    rev = lambda b, n: b * ns + (ns - 1 - n)
    fwd = lambda b, n: b * ns + n
    sb = pl.pallas_call(
        functools.partial(_ret_bstate_kernel, cps=cps),
        grid=(batch, ns),
        in_specs=[
            pl.BlockSpec((rows, 2 * LANES), lambda b, n: (rev(b, n), 1)),
            pl.BlockSpec((rows, dv), lambda b, n: (rev(b, n), 1)),
        ],
        out_specs=pl.BlockSpec((1, cps) + state_shape, lambda b, n: (b, ns - 1 - n, 0, 0)),
        out_shape=jax.ShapeDtypeStruct((batch, ns * cps) + state_shape, BF16),
        scratch_shapes=[pltpu.VMEM(state_shape, F32)],
        compiler_params=_params(("arbitrary", "arbitrary")),
        name="ret_bstate",
    )(proj, proj)
    return pl.pallas_call(
        functools.partial(_ret_out_kernel, cps=cps),
        grid=(batch, ns),
        in_specs=[
            pl.BlockSpec((rows, 2 * LANES), lambda b, n: (fwd(b, n), 0)),
            pl.BlockSpec((rows, 2 * LANES), lambda b, n: (fwd(b, n), 1)),
            pl.BlockSpec((rows, dv), lambda b, n: (fwd(b, n), 1)),
            pl.BlockSpec((rows, dv), lambda b, n: (fwd(b, n), 2)),
            pl.BlockSpec((1, cps) + state_shape, lambda b, n: (b, n, 0, 0)),
        ],
        out_specs=pl.BlockSpec((rows, dv), lambda b, n: (fwd(b, n), 0)),
        out_shape=jax.ShapeDtypeStruct((t, dv), BF16),
        scratch_shapes=[pltpu.VMEM(state_shape, F32)],
        compiler_params=_params(("arbitrary", "arbitrary")),
        name="ret_out",
    )(proj, proj, proj, proj, sb)


def _lru_kernel(xfp_ref, xf_ref, xfn_ref, xbp_ref, xb_ref, xbn_ref, cw_ref, cb_ref, gw_ref, gb_ref,
                lam_ref, hf_ref, hb_ref, xx_ref, a_ref, b_ref, h_ref, *, nt, ts, batch):
    i = pl.program_id(0)
    halo = SUBLANES
    lo = LRU_CONV // 2

    @pl.when(i == 0)
    def _():
        h_ref[...] = jnp.zeros_like(h_ref)

    def prepare(xp_ref, x_ref, xn_ref, tile, z, slot):
        for b in range(batch):
            xx_ref[0:halo, :] = jnp.where(tile == 0, 0.0, xp_ref[b])
            xx_ref[halo:halo + ts, :] = x_ref[b]
            xx_ref[halo + ts:2 * halo + ts, :] = jnp.where(tile == nt - 1, 0.0, xn_ref[b])
            xx = xx_ref[...]
            xc = cb_ref[...]
            for tap in range(LRU_CONV):
                shift = (lo - tap) % (ts + 2 * halo)
                moved = xx if shift == 0 else pltpu.roll(xx, shift, axis=0)
                xc = xc + cw_ref[tap:tap + 1, :] * moved[halo:halo + ts, :]
            for n in range(LRU_BLOCKS):
                cols = slice(n * LRU_BW, (n + 1) * LRU_BW)
                xn = xc[:, cols]
                g = (jnp.dot(xn.astype(BF16), gw_ref[z, n], preferred_element_type=F32)
                     + gb_ref[z, n:n + 1, :])
                r = _sigmoid(g[:, :LRU_BW])
                ig = _sigmoid(g[:, LRU_BW:])
                a = jnp.exp((-LRU_C) * r * _softplus(-lam_ref[z, :, cols]))
                a_ref[slot + b, :, cols] = a
                b_ref[slot + b, :, cols] = jnp.sqrt(1.0 - a * a) * (ig * xn)

    prepare(xfp_ref, xf_ref, xfn_ref, i, 0, 0)
    prepare(xbp_ref, xb_ref, xbn_ref, nt - 1 - i, 1, batch)

    def step(s, hs):
        out = []
        for k in range(2 * batch):
            row = s if k < batch else ts - 1 - s
            h = a_ref[k, pl.ds(row, 1), :] * hs[k] + b_ref[k, pl.ds(row, 1), :]
            if k < batch:
                hf_ref[k, pl.ds(row, 1), :] = h
            else:
                hb_ref[k - batch, pl.ds(row, 1), :] = h
            out.append(h)
        return tuple(out)

    hs = lax.fori_loop(0, ts, step, tuple(h_ref[k] for k in range(2 * batch)), unroll=8)
    for k in range(2 * batch):
        h_ref[k] = hs[k]


def _rglru(proj_f32, conv_w, conv_b, gate_w, gate_b, lam, batch, seq_len):
    w = LRU_BLOCKS * LRU_BW
    ts = min(LRU_ROWS, seq_len)
    nt = seq_len // ts
    rows8 = ts // SUBLANES
    last8 = seq_len // SUBLANES - 1
    x3 = proj_f32.reshape(batch, seq_len, proj_f32.shape[1])
    gw = jnp.concatenate([gate_w[:, 0], gate_w[:, 1]], axis=-1).astype(BF16)
    gb = jnp.concatenate([gate_b[:, 0], gate_b[:, 1]], axis=-1)
    bwd = lambda i: nt - 1 - i

    def tile_specs(tile):
        return [
            pl.BlockSpec((batch, SUBLANES, w), lambda i: (0, jnp.maximum(tile(i) * rows8 - 1, 0), 0)),
            pl.BlockSpec((batch, ts, w), lambda i: (0, tile(i), 0)),
            pl.BlockSpec((batch, SUBLANES, w), lambda i: (0, jnp.minimum((tile(i) + 1) * rows8, last8), 0)),
        ]

    full = lambda a: pl.BlockSpec(a.shape, lambda i: (0,) * a.ndim)
    cb = conv_b.reshape(1, w)
    lam3 = lam.reshape(2, 1, w)
    state = jax.ShapeDtypeStruct((batch, seq_len, w), F32)
    h_fwd, h_bwd = pl.pallas_call(
        functools.partial(_lru_kernel, nt=nt, ts=ts, batch=batch),
        grid=(nt,),
        in_specs=tile_specs(lambda i: i) + tile_specs(bwd) + [full(conv_w), full(cb), full(gw), full(gb),
                                                               full(lam3)],
        out_specs=[pl.BlockSpec((batch, ts, w), lambda i: (0, i, 0)),
                   pl.BlockSpec((batch, ts, w), lambda i: (0, bwd(i), 0))],
        out_shape=[state, state],
        scratch_shapes=[pltpu.VMEM((ts + 2 * SUBLANES, w), F32), pltpu.VMEM((2 * batch, ts, w), F32),
                        pltpu.VMEM((2 * batch, ts, w), F32), pltpu.VMEM((2 * batch, 1, w), F32)],
        compiler_params=_params(("arbitrary",)),
        name="lru_scan",
    )(x3, x3, x3, x3, x3, x3, conv_w, cb, gw, gb, lam3)
    return h_fwd.reshape(batch * seq_len, w), h_bwd.reshape(batch * seq_len, w)


def _gla_kernel(qf_ref, vf_ref, kf_ref, lf_ref, qb_ref, vb_ref, kb_ref, lb_ref, of_ref, ob_ref, s_ref,
                *, ts, batch):
    c = GLA_CHUNK
    nchunks = ts // c

    @pl.when(pl.program_id(0) == 0)
    def _():
        s_ref[...] = jnp.zeros_like(s_ref)

    head_cols = [slice(h * HG_DK, (h + 1) * HG_DK) for h in range(HG_HEADS)]
    hc = HG_HEADS * c
    ii = lax.broadcasted_iota(jnp.int32, (c, c), 0)
    jj = lax.broadcasted_iota(jnp.int32, (c, c), 1)
    si = lax.broadcasted_iota(jnp.int32, (hc, hc), 0)
    sj = lax.broadcasted_iota(jnp.int32, (hc, hc), 1)
    same_head = si // c == sj // c

    def stack(a):
        return jnp.concatenate([a[:, cols] for cols in head_cols], axis=0)

    tri = {False: (jj <= ii).astype(BF16), True: (jj >= ii).astype(BF16)}
    tri2 = {r: jnp.concatenate([m, m], axis=1) for r, m in tri.items()}
    zero_block = jnp.zeros((c, HG_DK), BF16)
    zero_state = jnp.zeros((HG_DK, HG_DV), BF16)
    keep = {False: jnp.logical_and(same_head, sj <= si), True: jnp.logical_and(same_head, sj >= si)}
    chains = ([(qf_ref, vf_ref, kf_ref, lf_ref, of_ref, b, False) for b in range(batch)]
              + [(qb_ref, vb_ref, kb_ref, lb_ref, ob_ref, b, True) for b in range(batch)])

    def gates(cc):
        work = []
        for chain, (q_ref, v_ref, k_ref, l_ref, o_ref, b, reverse) in enumerate(chains):
            rows = pl.ds(((nchunks - 1 - cc) if reverse else cc) * c, c)
            log_f = l_ref[b, rows, :]
            f_hi = log_f.astype(BF16)
            f_lo = (log_f - f_hi.astype(F32)).astype(BF16)
            work.append(dict(rows=rows, reverse=reverse, o_ref=o_ref, b=b, chain=chain,
                             key=k_ref[b, rows, :].astype(F32), qs=q_ref[b, rows, :].astype(F32),
                             v=v_ref[b, rows, :], split=(f_hi, f_lo)))
        return work

    def cumulate(work):
        for w in work:
            f_hi, f_lo = w["split"]
            w["bcum"] = jnp.dot(tri2[w["reverse"]], jnp.concatenate([f_lo, f_hi], axis=0),
                                preferred_element_type=F32)

    half = c // 2
    row = lax.broadcasted_iota(jnp.int32, (c, 1), 0)
    first_half = row < half

    def decays(work):
        for w in work:
            bcum = w["bcum"]
            rev = w["reverse"]
            end = 0 if rev else c - 1
            b_end = bcum[end:end + 1, :]
            ref = jnp.where(first_half, bcum[half // 2:half // 2 + 1, :], bcum[half + half // 2:half + half // 2 + 1, :])
            qe = w["qs"] * jnp.exp2(bcum - ref)
            ke = w["key"] * jnp.exp2(ref - bcum)
            boundary = bcum[half:half + 1, :] if rev else bcum[half - 1:half, :]
            cross = jnp.exp2(-jnp.abs(bcum - boundary))
            queries = first_half if rev else jnp.logical_not(first_half)
            w["q3"] = [jnp.where(first_half, qe, 0.0).astype(BF16), jnp.where(first_half, 0.0, qe).astype(BF16),
                       jnp.where(queries, w["qs"] * cross, 0.0).astype(BF16)]
            w["k3"] = [jnp.where(first_half, ke, 0.0), jnp.where(first_half, 0.0, ke),
                       jnp.where(queries, 0.0, w["key"] * cross)]
            w["qd"] = (w["qs"] * jnp.exp2(bcum)).astype(BF16)
            w["kd"] = (w["key"] * jnp.exp2(b_end - bcum)).astype(BF16)
            w["decay"] = jnp.exp2(b_end)

    def stack3(parts):
        return jnp.concatenate([jnp.concatenate([p[:, cols] for p in parts], axis=1) for cols in head_cols], axis=0)

    def scores(work):
        for w in work:
            w["att"] = jnp.dot(stack3(w["q3"]), stack3(w["k3"]).T.astype(BF16), preferred_element_type=F32)

    def intra(work):
        for w in work:
            att = jnp.where(keep[w["reverse"]], w["att"], 0.0).astype(BF16)
            w["intra"] = jnp.dot(att, stack(w["v"]), preferred_element_type=F32)

    def inter(work):
        for w in work:
            w["st"] = [s_ref[w["chain"], h] for h in range(HG_HEADS)]
            out = []
            for h in range(0, HG_HEADS, 2):
                sa = w["st"][h].T.astype(BF16)
                sb = w["st"][h + 1].T.astype(BF16)
                rhs = jnp.concatenate([jnp.concatenate([sa, zero_state], axis=1),
                                       jnp.concatenate([zero_state, sb], axis=1)], axis=0)
                pair = jnp.dot(w["qd"][:, h * HG_DK:(h + 2) * HG_DK], rhs, preferred_element_type=F32)
                out += [pair[:, :HG_DV], pair[:, HG_DV:]]
            w["inter"] = out

    def update(work):
        for w in work:
            upd = []
            for h in range(0, HG_HEADS, 2):
                ca, cb = head_cols[h], head_cols[h + 1]
                lhs = jnp.concatenate([w["v"][:, ca], w["v"][:, cb]], axis=0)
                rhs = jnp.concatenate([jnp.concatenate([w["kd"][:, ca], zero_block], axis=1),
                                       jnp.concatenate([zero_block, w["kd"][:, cb]], axis=1)], axis=0)
                pair = lax.dot_general(lhs, rhs, _TN, preferred_element_type=F32)
                upd += [pair[:, :HG_DK], pair[:, HG_DK:]]
            w["upd"] = upd

    def finish(work):
        for w in work:
            k = w["chain"]
            for h, cols in enumerate(head_cols):
                w["o_ref"][w["b"], w["rows"], cols] = (w["intra"][h * c:(h + 1) * c, :]
                                                       + w["inter"][h]).astype(BF16)
                s_ref[k, h] = w["st"][h] * w["decay"][:, cols] + w["upd"][h]

    cur = gates(0)
    cumulate(cur)
    decays(cur)
    for cc in range(nchunks):
        more = cc + 1 < nchunks
        scores(cur)
        nxt = gates(cc + 1) if more else None
        if more:
            cumulate(nxt)
        intra(cur)
        inter(cur)
        update(cur)
        if more:
            decays(nxt)
        finish(cur)
        cur = nxt


def _gla(proj_bf16, proj_f32, batch, seq_len):
    d = HG_HEADS * HG_DK
    ts = min(GLA_ROWS, seq_len)
    nt = seq_len // ts
    pb = proj_bf16.reshape(batch, seq_len, proj_bf16.shape[1])
    pf = proj_f32.reshape(batch, seq_len, proj_f32.shape[1])
    bwd = lambda i: nt - 1 - i
    blk = lambda tile, col: pl.BlockSpec((batch, ts, d), lambda i: (0, tile(i), col))
    fwd = lambda i: i
    out = jax.ShapeDtypeStruct((batch, seq_len, d), BF16)
    o_fwd, o_bwd = pl.pallas_call(
        functools.partial(_gla_kernel, ts=ts, batch=batch),
        grid=(nt,),
        in_specs=[blk(fwd, 0), blk(fwd, 1), blk(fwd, 3), blk(fwd, 0),
                  blk(bwd, 0), blk(bwd, 1), blk(bwd, 4), blk(bwd, 1)],
        out_specs=[blk(fwd, 0), blk(bwd, 0)],
        out_shape=[out, out],
        scratch_shapes=[pltpu.VMEM((2 * batch, HG_HEADS, HG_DV, HG_DK), F32)],
        compiler_params=_params(("arbitrary",)),
        name="gla_scan",
    )(pb, pb, pb, pf, pb, pb, pb, pf)
    return o_fwd.reshape(batch * seq_len, d), o_bwd.reshape(batch * seq_len, d)


def _top2(p):
    v1 = jnp.maximum(jnp.maximum(p[0], p[1]), jnp.maximum(p[2], p[3]))
    i1 = jnp.where(p[0] == v1, 0, jnp.where(p[1] == v1, 1, jnp.where(p[2] == v1, 2, 3)))
    q = [jnp.where(i1 == k, -1.0, p[k]) for k in range(4)]
    v2 = jnp.maximum(jnp.maximum(q[0], q[1]), jnp.maximum(q[2], q[3]))
    i2 = jnp.where(q[0] == v2, 0, jnp.where(q[1] == v2, 1, jnp.where(q[2] == v2, 2, 3)))
    return v1, i1, v2, i2


def _outproj_kernel(*refs, even, alpha, tm):
    if even:
        ret_ref, hf_ref, hb_ref, gr_ref, w0_ref, w1_ref = refs[:6]
        rest = refs[6:]
        lru = ((hf_ref[...] + hb_ref[...]) * jax.nn.gelu(gr_ref[...])).astype(BF16)
        y = (jnp.dot(ret_ref[...], w0_ref[...], preferred_element_type=F32)
             + jnp.dot(lru, w1_ref[...], preferred_element_type=F32))
    else:
        of_ref, ob_ref, g_ref, nw_ref, w0_ref = refs[:5]
        rest = refs[5:]
        o = of_ref[...].astype(F32) + ob_ref[...].astype(F32)
        ms = jnp.mean(o * o, axis=-1, keepdims=True)
        mix = o * lax.rsqrt(ms + RMS_EPS) * nw_ref[...] * _silu(g_ref[...].astype(F32))
        y = jnp.dot(mix.astype(BF16), w0_ref[...], preferred_element_type=F32)
    h_ref, lnw_ref, lnb_ref, rw_ref, o_ref, info_ref, carry_ref, before_ref = rest
    i = pl.program_id(0)

    @pl.when(i == 0)
    def _():
        carry_ref[...] = jnp.zeros_like(carry_ref)
        tt = lax.broadcasted_iota(jnp.int32, (tm, tm), 0)
        uu = lax.broadcasted_iota(jnp.int32, (tm, tm), 1)
        before_ref[...] = (tt < uu).astype(BF16)

    h1 = _layer_norm(alpha * h_ref[...] + y, lnw_ref[...], lnb_ref[...])
    o_ref[...] = h1

    h_hi = h1.astype(BF16)
    h_lo = (h1 - h_hi.astype(F32)).astype(BF16)
    rw = rw_ref[...]
    r_hi = rw.astype(BF16)
    r_lo = (rw - r_hi.astype(F32)).astype(BF16)
    both = lax.dot_general(jnp.concatenate([r_hi, r_lo], axis=0), h_hi, _NT, preferred_element_type=F32)
    logits = (both[N_EXPERTS:, :] + lax.dot_general(r_hi, h_lo, _NT, preferred_element_type=F32)
              + both[:N_EXPERTS, :])
    ex = jnp.exp(logits - jnp.max(logits, axis=0, keepdims=True))
    probs = ex / jnp.sum(ex, axis=0, keepdims=True)
    best = None
    for g in range(N_GROUPS):
        rows = [probs[g * EXPERTS_PER_GROUP + k:g * EXPERTS_PER_GROUP + k + 1, :]
                for k in range(EXPERTS_PER_GROUP)]
        v1, i1, v2, i2 = _top2(rows)
        cand = (v1 + v2, v1, i1 + g * EXPERTS_PER_GROUP, v2, i2 + g * EXPERTS_PER_GROUP)
        if best is None:
            best = cand
        else:
            take = cand[0] > best[0]
            best = tuple(jnp.where(take, cn, bs) for cn, bs in zip(cand, best))
    _, v1, e1, v2, e2 = best
    denom = v1 + v2
    g1 = v1 / denom
    g2 = v2 / denom

    eid = lax.broadcasted_iota(jnp.int32, (N_EXPERTS, tm), 0)
    oh1 = (eid == e1).astype(F32)
    oh2 = (eid == e2).astype(F32)
    oh = oh1 + oh2
    base = carry_ref[:, 0:1] + jnp.dot(oh.astype(BF16), before_ref[...], preferred_element_type=F32)
    rank1 = jnp.sum(oh1 * base, axis=0, keepdims=True)
    rank2 = jnp.sum(oh2 * base, axis=0, keepdims=True)
    carry_ref[...] = carry_ref[...] + jnp.sum(oh, axis=1, keepdims=True)
    zero = jnp.zeros_like(g1)
    info_ref[...] = jnp.concatenate(
        [e1.astype(F32), e2.astype(F32), g1, g2, rank1, rank2, zero, zero], axis=0)


def _out_project(even, mixer_inputs, weights_bf16, h, ln_w, ln_b, router_wt, alpha):
    t, d = h.shape
    tm = TOKEN_ROWS
    row = lambda i: (i, 0)
    const = lambda i: (0, 0)
    if even:
        ret, h_fwd, h_bwd, proj_f32 = mixer_inputs
        w = h_fwd.shape[1]
        in_specs = [pl.BlockSpec((tm, ret.shape[1]), row), pl.BlockSpec((tm, w), row),
                    pl.BlockSpec((tm, w), row), pl.BlockSpec((tm, w), lambda i: (i, 1))]
        args = [ret, h_fwd, h_bwd, proj_f32]
    else:
        o_fwd, o_bwd, proj_bf16, norm_w = mixer_inputs
        in_specs = [pl.BlockSpec((tm, d), row), pl.BlockSpec((tm, d), row),
                    pl.BlockSpec((tm, d), lambda i: (i, 2)), pl.BlockSpec((1, d), const)]
        args = [o_fwd, o_bwd, proj_bf16, norm_w.reshape(1, d)]
    in_specs += [pl.BlockSpec(wm.shape, const) for wm in weights_bf16]
    in_specs += [pl.BlockSpec((tm, d), row), pl.BlockSpec((1, d), const), pl.BlockSpec((1, d), const),
                 pl.BlockSpec((N_EXPERTS, d), const)]
    args += list(weights_bf16) + [h, ln_w.reshape(1, d), ln_b.reshape(1, d), router_wt]
    return pl.pallas_call(
        functools.partial(_outproj_kernel, even=even, alpha=alpha, tm=tm),
        grid=(t // tm,),
        in_specs=in_specs,
        out_specs=[pl.BlockSpec((tm, d), row),
                   pl.BlockSpec((SUBLANES, tm), lambda i: (0, i))],
        out_shape=[jax.ShapeDtypeStruct((t, d), F32),
                   jax.ShapeDtypeStruct((SUBLANES, t), F32)],
        scratch_shapes=[pltpu.VMEM((N_EXPERTS, LANES), F32), pltpu.VMEM((tm, tm), BF16)],
        compiler_params=_params(("arbitrary",)),
        name="out_proj_router",
    )(*args)


def _pack_halves(x):
    n = x.shape[1] // 2
    hi = pltpu.bitcast(x[:, :n], jnp.uint32)
    lo = pltpu.bitcast(x[:, n:], jnp.uint32)
    return hi | (lo >> 16)


def _unpack_halves(p):
    hi = pltpu.bitcast(p & jnp.uint32(0xFFFF0000), F32)
    lo = pltpu.bitcast(p << 16, F32)
    return jnp.concatenate([hi.astype(BF16), lo.astype(BF16)], axis=1)


def _seg_expert_kernel(first_ref, count_ref, tail_ref, x_ref, wg_ref, wu_ref, wd_ref, y_ref,
                       xbuf, ybuf, wgb_ref, wub_ref, wdb_ref, semx, semy):
    e = pl.program_id(0)
    wgb_ref[...] = wg_ref[...].astype(BF16)
    wub_ref[...] = wu_ref[...].astype(BF16)
    wdb_ref[...] = wd_ref[...].astype(BF16)
    first = first_ref[e]
    n_used = tail_ref[0]

    def rows(g):
        return pl.ds(pl.multiple_of(g * MOE_ROWS, MOE_ROWS), MOE_ROWS)

    def fetch(g):
        return pltpu.make_async_copy(x_ref.at[rows(g)], xbuf.at[g % 2], semx.at[g % 2])

    def put(g):
        return pltpu.make_async_copy(ybuf.at[g % 2], y_ref.at[rows(g)], semy.at[g % 2])

    @pl.when(e == 0)
    def _():
        fetch(0).start()

    def body(g, carry):
        @pl.when(g + 1 < n_used)
        def _():
            fetch(g + 1).start()

        fetch(g).wait()

        @pl.when(g >= 2)
        def _():
            put(g - 2).wait()

        x = _unpack_halves(xbuf[g % 2])
        gate = jnp.dot(x, wgb_ref[...], preferred_element_type=F32)
        up = jnp.dot(x, wub_ref[...], preferred_element_type=F32)
        hid = (_silu(gate) * up).astype(BF16)
        y = jnp.dot(hid, wdb_ref[...], preferred_element_type=F32)
        ybuf[g % 2] = _pack_halves(y.astype(BF16).astype(F32))
        put(g).start()
        return carry

    lax.fori_loop(first, first + count_ref[e], body, 0)

    @pl.when(e == pl.num_programs(0) - 1)
    def _():
        @pl.when(n_used >= 2)
        def _():
            put(n_used - 2).wait()

        put(n_used - 1).wait()
        ybuf[0] = jnp.zeros_like(ybuf[0])

        def clear(b, carry):
            cp = pltpu.make_async_copy(
                ybuf.at[0], y_ref.at[pl.ds(pl.multiple_of(b * MOE_ROWS, MOE_ROWS), MOE_ROWS)], semy.at[0])
            cp.start()
            cp.wait()
            return carry

        lax.fori_loop(tail_ref[0], y_ref.shape[0] // MOE_ROWS, clear, 0)


def _seg_experts(xb, seg_first, seg_count, n_valid, w_gate, w_up, w_down, layer):
    p, half = xb.shape
    d = 2 * half
    de = w_gate.shape[3]
    weight = lambda shape: pl.BlockSpec((None, None) + shape, lambda e, a, b, c: (layer, e, 0, 0))
    return pl.pallas_call(
        _seg_expert_kernel,
        grid_spec=pltpu.PrefetchScalarGridSpec(
            num_scalar_prefetch=3,
            grid=(N_EXPERTS,),
            in_specs=[pl.BlockSpec(memory_space=pl.ANY), weight((d, de)), weight((d, de)), weight((de, d))],
            out_specs=pl.BlockSpec(memory_space=pl.ANY),
            scratch_shapes=[pltpu.VMEM((2, MOE_ROWS, half), jnp.uint32),
                            pltpu.VMEM((2, MOE_ROWS, half), jnp.uint32),
                            pltpu.VMEM((d, de), BF16), pltpu.VMEM((d, de), BF16), pltpu.VMEM((de, d), BF16),
                            pltpu.SemaphoreType.DMA((2,)), pltpu.SemaphoreType.DMA((2,))],
        ),
        out_shape=jax.ShapeDtypeStruct((p, half), jnp.uint32),
        compiler_params=_params(("arbitrary",)),
        name="moe_experts",
    )(seg_first, seg_count, n_valid, xb, w_gate, w_up, w_down)


def _clear_padding_blocks(zs_ref, xb_ref, zero_ref, sem):
    zero_ref[...] = jnp.zeros_like(zero_ref)

    def clear(row):
        start = pl.multiple_of(row, MOE_ROWS)
        return pltpu.make_async_copy(zero_ref, xb_ref.at[pl.ds(start, MOE_ROWS)], sem)

    for e in range(N_EXPERTS):
        clear(zs_ref[e]).start()
    for e in range(N_EXPERTS):
        clear(zs_ref[e]).wait()

    def clear_tail(b, carry):
        clear(b * MOE_ROWS).start()
        clear(b * MOE_ROWS).wait()
        return carry

    lax.fori_loop(zs_ref[N_EXPERTS], xb_ref.shape[0] // MOE_ROWS, clear_tail, 0)


def _run_copy(src_ref, src_row, dst_ref, dst_row, sem):
    src = pl.multiple_of(src_row, SUBLANES)
    dst = pl.multiple_of(dst_row, SUBLANES)
    return pltpu.make_async_copy(src_ref.at[pl.ds(src, SUBLANES)], dst_ref.at[pl.ds(dst, SUBLANES)], sem)


def _start_runs(count, make_copy):
    def pair(p, carry):
        make_copy(2 * p).start()
        make_copy(2 * p + 1).start(priority=1)
        return carry

    lax.fori_loop(0, count // 2, pair, 0)

    @pl.when(count % 2 == 1)
    def _():
        make_copy(count - 1).start()


def _sort_dispatch_kernel(dst_ref, nch_ref, zs_ref, h_ref, lp_ref, xb_ref, xs_ref, zero_ref, sem, *, tm, nck):
    i = pl.program_id(0)

    @pl.when(i == 0)
    def _():
        _clear_padding_blocks(zs_ref, xb_ref, zero_ref, sem.at[0])

    sub = SORT_TILES_PER_STEP
    last = pl.num_programs(0) - 1

    def wait_runs(buf, count):
        def wait(j, carry):
            _run_copy(xs_ref.at[buf], 0, xb_ref, 0, sem.at[buf]).wait()
            return carry

        lax.fori_loop(0, count, wait, 0)

    pos = lax.broadcasted_iota(jnp.int32, (xs_ref.shape[1], tm), 0)
    for s in range(sub):
        tile = i * sub + s
        buf = (i % 2) * sub + s

        @pl.when(i >= 2)
        def _():
            wait_runs(buf, nch_ref[jnp.maximum(tile - 2 * sub, 0)])

        lp = lp_ref[:, s * tm:(s + 1) * tm]
        perm = jnp.logical_or(pos == lp[0:1, :], pos == lp[1:2, :]).astype(BF16)
        rows = h_ref[s * tm:(s + 1) * tm, :].astype(BF16)
        xs_ref[buf] = _pack_halves(jnp.dot(perm, rows, preferred_element_type=F32))

        _start_runs(nch_ref[tile], lambda j: _run_copy(xs_ref.at[buf], j * SUBLANES, xb_ref,
                                                       dst_ref[tile * nck + j], sem.at[buf]))

    @pl.when(i == last)
    def _():
        for s in range(sub):
            @pl.when(i >= 1)
            def _():
                wait_runs((1 - i % 2) * sub + s, nch_ref[jnp.maximum((i - 1) * sub + s, 0)])

            wait_runs((i % 2) * sub + s, nch_ref[i * sub + s])


def _sort_dispatch(h, lp_rows, chunk_dst, n_chunks, zero_start, n_rows, tm, sorted_rows):
    t, d = h.shape
    nck = sorted_rows // SUBLANES
    sub = SORT_TILES_PER_STEP
    return pl.pallas_call(
        functools.partial(_sort_dispatch_kernel, tm=tm, nck=nck),
        grid_spec=pltpu.PrefetchScalarGridSpec(
            num_scalar_prefetch=3,
            grid=(t // (sub * tm),),
            in_specs=[pl.BlockSpec((sub * tm, d), lambda i, a, b, c: (i, 0)),
                      pl.BlockSpec((2, sub * tm), lambda i, a, b, c: (0, i))],
            out_specs=pl.BlockSpec(memory_space=pl.ANY),
            scratch_shapes=[pltpu.VMEM((2 * sub, sorted_rows, d // 2), jnp.uint32),
                            pltpu.VMEM((MOE_ROWS, d // 2), jnp.uint32), pltpu.SemaphoreType.DMA((2 * sub,))],
        ),
        out_shape=jax.ShapeDtypeStruct((n_rows, d // 2), jnp.uint32),
        compiler_params=_params(("arbitrary",)),
        name="moe_dispatch",
    )(chunk_dst, n_chunks, zero_start, h, lp_rows)


def _sort_combine_kernel(src_ref, nch_ref, h_ref, lp_ref, gates_ref, lnw_ref, lnb_ref, yb_ref, o_ref,
                         ys_ref, sem, *, tm, nck, alpha):
    i = pl.program_id(0)
    sub = SORT_TILES_PER_STEP

    def fetch(tile, buf):
        _start_runs(nch_ref[tile], lambda j: _run_copy(yb_ref, src_ref[tile * nck + j], ys_ref.at[buf],
                                                       j * SUBLANES, sem.at[buf]))

    @pl.when(i == 0)
    def _():
        ys_ref[...] = jnp.zeros_like(ys_ref)
        for s in range(sub):
            fetch(s, s)

    @pl.when(i + 1 < pl.num_programs(0))
    def _():
        for s in range(sub):
            fetch((i + 1) * sub + s, (1 - i % 2) * sub + s)

    pos = lax.broadcasted_iota(jnp.int32, (tm, ys_ref.shape[1]), 1)
    for s in range(sub):
        buf = (i % 2) * sub + s
        rows = slice(s * tm, (s + 1) * tm)

        def wait(j, carry):
            _run_copy(yb_ref, 0, ys_ref.at[buf], 0, sem.at[buf]).wait()
            return carry

        lax.fori_loop(0, nch_ref[i * sub + s], wait, 0)

        lp = lp_ref[rows, :]
        pick = (jnp.where(pos == lp[:, 0:1], gates_ref[rows, 0:1], 0.0)
                + jnp.where(pos == lp[:, 1:2], gates_ref[rows, 1:2], 0.0))
        y = jnp.dot(pick.astype(BF16), _unpack_halves(ys_ref[buf]), preferred_element_type=F32)
        o_ref[rows, :] = _layer_norm(alpha * h_ref[rows, :] + y, lnw_ref[...], lnb_ref[...])


def _sort_combine(h, yb, lp_cols, gates, chunk_src, n_chunks, ln_w, ln_b, alpha, tm, sorted_rows):
    t, d = h.shape
    nck = sorted_rows // SUBLANES
    step = SORT_TILES_PER_STEP * tm
    row = lambda i, a, b: (i, 0)
    const = lambda i, a, b: (0, 0)
    return pl.pallas_call(
        functools.partial(_sort_combine_kernel, tm=tm, nck=nck, alpha=alpha),
        grid_spec=pltpu.PrefetchScalarGridSpec(
            num_scalar_prefetch=2,
            grid=(t // step,),
            in_specs=[pl.BlockSpec((step, d), row), pl.BlockSpec((step, 2), row), pl.BlockSpec((step, 2), row),
                      pl.BlockSpec((1, d), const), pl.BlockSpec((1, d), const),
                      pl.BlockSpec(memory_space=pl.ANY)],
            out_specs=pl.BlockSpec((step, d), row),
            scratch_shapes=[pltpu.VMEM((2 * SORT_TILES_PER_STEP, sorted_rows, d // 2), jnp.uint32),
                            pltpu.SemaphoreType.DMA((2 * SORT_TILES_PER_STEP,))],
        ),
        out_shape=jax.ShapeDtypeStruct((t, d), F32),
        compiler_params=_params(("arbitrary",)),
        name="moe_combine",
    )(chunk_src, n_chunks, h, lp_cols, gates, ln_w.reshape(1, d), ln_b.reshape(1, d), yb)


def _moe(h1, info, w_gate, w_up, w_down, layer, ln_w, ln_b, alpha):
    t = h1.shape[0]
    tm = SORT_ROWS
    ntile = t // tm
    run_pad = SUBLANES - 1
    sorted_rows = 2 * tm + LANES
    nck = sorted_rows // SUBLANES
    n_rows = (2 * t + ntile * N_EXPERTS * run_pad + MOE_ROWS - 1) // MOE_ROWS * MOE_ROWS + N_EXPERTS * MOE_ROWS
    i32 = jnp.int32
    e1, e2 = info[0].astype(i32), info[1].astype(i32)
    rank1, rank2 = info[4].astype(i32), info[5].astype(i32)
    expert_ids = jnp.arange(N_EXPERTS, dtype=i32)[None, :]
    oh1 = e1[:, None] == expert_ids
    oh2 = e2[:, None] == expert_ids
    cnt = jnp.logical_or(oh1, oh2).astype(i32).reshape(ntile, tm, N_EXPERTS).sum(axis=1)
    cnt8 = (cnt + run_pad) // SUBLANES * SUBLANES
    local = jnp.cumsum(cnt8, axis=1) - cnt8
    before = jnp.cumsum(cnt, axis=0) - cnt
    seg = cnt8.sum(axis=0)
    seg_pad = (seg + MOE_ROWS - 1) // MOE_ROWS * MOE_ROWS
    ends = jnp.cumsum(seg_pad)
    slot = (ends - seg_pad)[None, :] + jnp.cumsum(cnt8, axis=0) - cnt8
    shift = jnp.repeat(local - before, tm, axis=0)
    lp1 = jnp.sum(jnp.where(oh1, shift, 0), axis=1) + rank1
    lp2 = jnp.sum(jnp.where(oh2, shift, 0), axis=1) + rank2
    chunk_row = jnp.arange(nck, dtype=i32) * SUBLANES
    run_of = jnp.sum(((local + cnt8)[:, None, :] <= chunk_row[None, :, None]).astype(i32), axis=2)
    run_of = jnp.minimum(run_of, N_EXPERTS - 1)
    chunk_slot = jnp.sum(jnp.where(run_of[:, :, None] == expert_ids[None], (slot - local)[:, None, :], 0),
                         axis=2) + chunk_row[None, :]
    n_chunks = cnt8.sum(axis=1) // SUBLANES
    n_valid = (ends[-1] // MOE_ROWS).astype(i32).reshape(1)
    zero_start = jnp.concatenate([jnp.maximum(ends - MOE_ROWS, 0), n_valid]).astype(i32)
    gates = jnp.stack([info[2], info[3]], axis=1)
    chunk_slot = chunk_slot.reshape(-1).astype(i32)
    seg_first = ((ends - seg_pad) // MOE_ROWS).astype(i32)
    seg_count = (seg_pad // MOE_ROWS).astype(i32)

    xb = _sort_dispatch(h1, jnp.stack([lp1, lp2], axis=0), chunk_slot, n_chunks, zero_start, n_rows, tm,
                        sorted_rows)
    yb = _seg_experts(xb, seg_first, seg_count, n_valid, w_gate, w_up, w_down, layer)
    return _sort_combine(h1, yb, jnp.stack([lp1, lp2], axis=1), gates, chunk_slot, n_chunks, ln_w, ln_b,
                         alpha, tm, sorted_rows)


def _rotary_column_order(w_in):
    d = w_in.shape[0]
    nq = RET_HEADS * RET_DK

    def perm(w):
        return w.reshape(d, RET_HEADS, RET_DK // 2, 2).transpose(0, 3, 1, 2).reshape(d, nq)

    return jnp.concatenate([perm(w_in[:, :nq]), perm(w_in[:, nq:2 * nq]), w_in[:, 2 * nq:]], axis=1)


def kernel(x, w_in_even, w_out_even, lru_conv_w, lru_conv_b, lru_gate_w, lru_gate_b, lru_lambda,
           w_in_odd, w_out_odd, hg_lower_bounds, hg_norm_w, ln_w, ln_b, router_w,
           moe_w_gate, moe_w_up, moe_w_down):
    batch, seq_len, d = x.shape
    depth = ln_w.shape[0]
    alpha = (2.0 * depth) ** 0.25
    t = batch * seq_len
    h = x.reshape(t, d)
    router_wt = router_w.T
    half = RET_DK // 2
    inv_freq = ROPE_BASE ** (-jnp.arange(0, RET_DK, 2, dtype=F32) / RET_DK)
    inv_freq = jnp.tile(inv_freq, LANES // half).reshape(1, LANES)
    nret = RET_HEADS * RET_DV
    tm = min(PROJ_ROWS, seq_len)
    tiles_per_seq = seq_len // tm
    cos, sin = _rotary_table(inv_freq, seq_len)
    pos_spec = pl.BlockSpec((tm, LANES), lambda i: (i % tiles_per_seq, 0))

    for layer in range(depth):
        j = layer // 2
        if layer % 2 == 0:
            w_in = _rotary_column_order(w_in_even[j]).astype(BF16)
            n_bf = 2 * RET_HEADS * RET_DK + 2 * nret
            proj_kernel = functools.partial(_proj_ret_kernel, n_bf=n_bf)
            proj_b, proj_f = _project(proj_kernel, "in_proj", h, w_in, [cos, sin], [pos_spec, pos_spec], n_bf,
                                      w_in.shape[1] - n_bf, seq_len)
            ret = _retention(proj_b, batch, seq_len)
            h_fwd, h_bwd = _rglru(proj_f, lru_conv_w[j], lru_conv_b[j], lru_gate_w[j], lru_gate_b[j],
                                  lru_lambda[j], batch, seq_len)
            w_out = w_out_even[j].astype(BF16)
            mixer_inputs, weights = (ret, h_fwd, h_bwd, proj_f), [w_out[:nret], w_out[nret:]]
        else:
            w = w_in_odd[j]
            w_in = jnp.concatenate([w[:, :2 * d], w[:, 4 * d:], w[:, 2 * d:4 * d]], axis=1).astype(BF16)
            proj_kernel = functools.partial(_proj_gla_kernel, layer=layer)
            bounds_spec = pl.BlockSpec(hg_lower_bounds.shape, lambda i: (0, 0))
            proj_b, proj_f = _project(proj_kernel, "in_proj_gla", h, w_in, [hg_lower_bounds], [bounds_spec],
                                      5 * d, 2 * d, seq_len)
            o_fwd, o_bwd = _gla(proj_b, proj_f, batch, seq_len)
            mixer_inputs, weights = (o_fwd, o_bwd, proj_b, hg_norm_w[j]), [w_out_odd[j].astype(BF16)]
        h1, info = _out_project(layer % 2 == 0, mixer_inputs, weights, h, ln_w[layer, 0], ln_b[layer, 0],
                                router_wt, alpha)
        h = _moe(h1, info, moe_w_gate, moe_w_up, moe_w_down, layer, ln_w[layer, 1], ln_b[layer, 1], alpha)
    return h.reshape(batch, seq_len, d)
```

```python
import functools
import math

import jax
import jax.numpy as jnp
from jax import lax
from jax.experimental import pallas as pl
from jax.experimental.pallas import tpu as pltpu

F32 = jnp.float32
BF16 = jnp.bfloat16

RET_HEADS = 4
RET_DK = 64
RET_DV = 128
RET_CHUNK = 128
ROPE_BASE = 10000.0
LRU_BLOCKS = 4
LRU_BW = 128
LRU_CONV = 4
LRU_C = 8.0
HG_HEADS = 8
HG_DK = 128
HG_DV = 128
N_EXPERTS = 16
N_GROUPS = 4
EXPERTS_PER_GROUP = 4
LN_EPS = 1e-5
RMS_EPS = 1e-6

LANES = 128
SUBLANES = 8
PROJ_ROWS = 512
PROJ_COLS = 512
RET_CHUNKS_PER_STEP = 8
LRU_ROWS = 512
GLA_CHUNK = 32
GLA_ROWS = 256
TOKEN_ROWS = 512
SORT_ROWS = 256
SORT_TILES_PER_STEP = 2
MOE_ROWS = 512
VMEM_LIMIT = 56 * 1024 * 1024

_NT = (((1,), (1,)), ((), ()))
_TN = (((0,), (0,)), ((), ()))


def _params(sem):
    return pltpu.CompilerParams(dimension_semantics=sem, vmem_limit_bytes=VMEM_LIMIT)


def _sigmoid(x):
    return 0.5 * jnp.tanh(0.5 * x) + 0.5


def _silu(x):
    return x * _sigmoid(x)


def _softplus(x):
    return jnp.maximum(x, 0.0) + jnp.log1p(jnp.exp(-jnp.abs(x)))


def _layer_norm(u, w, b):
    mu = jnp.mean(u, axis=-1, keepdims=True)
    d = u - mu
    var = jnp.mean(d * d, axis=-1, keepdims=True)
    return d * lax.rsqrt(var + LN_EPS) * w + b


def _rotary_table_kernel(inv_ref, cos_ref, sin_ref, *, tm):
    row = lax.broadcasted_iota(jnp.int32, (tm, 1), 0) + pl.program_id(0) * tm
    ang = row.astype(F32) * inv_ref[...]
    cos_ref[...] = jnp.cos(ang)
    sin_ref[...] = jnp.sin(ang)


def _rotary_table(inv_freq, seq_len):
    tm = min(PROJ_ROWS, seq_len)
    table = jax.ShapeDtypeStruct((seq_len, LANES), F32)
    return pl.pallas_call(
        functools.partial(_rotary_table_kernel, tm=tm),
        grid=(seq_len // tm,),
        in_specs=[pl.BlockSpec((1, LANES), lambda i: (0, 0))],
        out_specs=[pl.BlockSpec((tm, LANES), lambda i: (i, 0))] * 2,
        out_shape=[table, table],
        compiler_params=_params(("arbitrary",)),
        name="rotary_table",
    )(inv_freq)


def _proj_ret_kernel(x_ref, w_ref, cos_ref, sin_ref, ob_ref, of_ref, *, n_bf):
    xb = x_ref[...].astype(BF16)
    tn = PROJ_COLS
    for j in range(w_ref.shape[1] // tn):
        acc = jnp.dot(xb, w_ref[:, j * tn:(j + 1) * tn], preferred_element_type=F32)
        if j * tn < 4 * LANES:
            cos = cos_ref[...]
            sin = sin_ref[...]
            for base in range(0, tn, 2 * LANES):
                col = j * tn + base
                scale = RET_DK ** -0.5 if col < 2 * LANES else 1.0
                x1, x2 = acc[:, base:base + LANES], acc[:, base + LANES:base + 2 * LANES]
                ob_ref[:, col:col + LANES] = ((x1 * cos - x2 * sin) * scale).astype(BF16)
                ob_ref[:, col + LANES:col + 2 * LANES] = ((x1 * sin + x2 * cos) * scale).astype(BF16)
        elif (j + 1) * tn <= n_bf:
            ob_ref[:, j * tn:(j + 1) * tn] = acc.astype(BF16)
        else:
            of_ref[:, j * tn - n_bf:(j + 1) * tn - n_bf] = acc


def _gla_lower_bound(p, layer):
    e = jnp.exp(p - jnp.max(p, axis=0, keepdims=True))
    sm = e / jnp.sum(e, axis=0, keepdims=True)
    lb = jnp.zeros((1, p.shape[1]), F32)
    for r in range(1, layer + 1):
        lb = lb + sm[r:r + 1, :]
    return lb


def _proj_gla_kernel(x_ref, w_ref, lbp_ref, ob_ref, of_ref, *, layer):
    xb = x_ref[...].astype(BF16)
    d = w_ref.shape[1] // 5
    tn = PROJ_COLS
    lb = _gla_lower_bound(lbp_ref[...], layer)
    mean = 0.5 * (1.0 + lb)
    spread = 0.5 * (1.0 - lb)
    q_scale = 0.5 * HG_DK ** -0.5
    for j in range(5 * d // tn):
        col = j * tn
        acc = jnp.dot(xb, w_ref[:, col:col + tn], preferred_element_type=F32)
        if col < d:
            ob_ref[:, col:col + tn] = (acc * (q_scale * jnp.tanh(0.5 * acc) + q_scale)).astype(BF16)
        elif col < 3 * d:
            ob_ref[:, col:col + tn] = acc.astype(BF16)
        else:
            zc = col - 3 * d
            swing = spread[:, zc % d:zc % d + tn] * jnp.tanh(0.5 * acc)
            of_ref[:, zc:zc + tn] = jnp.log2(mean[:, zc % d:zc % d + tn] + swing)
            ob_ref[:, col:col + tn] = (spread[:, zc % d:zc % d + tn] - swing).astype(BF16)


def _project(kernel_fn, name, x, w_bf16, sides, side_specs, n_bf, n_f32, seq_len):
    t, k = x.shape
    n = w_bf16.shape[1]
    tm = min(PROJ_ROWS, seq_len)
    return pl.pallas_call(
        kernel_fn,
        grid=(t // tm,),
        in_specs=[pl.BlockSpec((tm, k), lambda i: (i, 0)), pl.BlockSpec((k, n), lambda i: (0, 0))] + side_specs,
        out_specs=[pl.BlockSpec((tm, n_bf), lambda i: (i, 0)),
                   pl.BlockSpec((tm, n_f32), lambda i: (i, 0))],
        out_shape=[jax.ShapeDtypeStruct((t, n_bf), BF16),
                   jax.ShapeDtypeStruct((t, n_f32), F32)],
        compiler_params=_params(("arbitrary",)),
        name=name,
    )(x, w_bf16, *sides)


def _ret_log_gamma(head):
    out = jnp.full(head.shape, math.log1p(-(2.0 ** -5.0)), F32)
    for h in range(1, RET_HEADS):
        out = jnp.where(head == h, math.log1p(-(2.0 ** (-5.0 - h))), out)
    return out


def _ret_lane_head():
    lane = lax.broadcasted_iota(jnp.int32, (1, 2 * LANES), 1)
    return (lane % LANES) // (RET_DK // 2)


def _ret_state_mask():
    shape = (2 * LANES, RET_HEADS * RET_DV)
    row_head = (lax.broadcasted_iota(jnp.int32, shape, 0) % LANES) // (RET_DK // 2)
    col_head = lax.broadcasted_iota(jnp.int32, shape, 1) // RET_DV
    return row_head == col_head


def _ret_chunk_decay():
    row = lax.broadcasted_iota(jnp.int32, (2 * LANES, 1), 0)
    return jnp.exp(_ret_log_gamma((row % LANES) // (RET_DK // 2)) * float(RET_CHUNK))


def _ret_bstate_kernel(k_ref, v_ref, sb_ref, s_ref, *, cps):
    c = RET_CHUNK

    @pl.when(pl.program_id(1) == 0)
    def _():
        s_ref[...] = jnp.zeros_like(s_ref)

    lg = _ret_log_gamma(_ret_lane_head())
    idx = lax.broadcasted_iota(jnp.int32, (c, 1), 0).astype(F32)
    k_decay = jnp.exp(lg * idx)
    chunk_decay = _ret_chunk_decay()
    mask = _ret_state_mask()
    for cc in reversed(range(cps)):
        rows = slice(cc * c, (cc + 1) * c)
        sb_ref[0, cc] = s_ref[...].astype(BF16)
        kb = (k_ref[rows, :] * k_decay).astype(BF16)
        upd = lax.dot_general(kb, v_ref[rows, :], _TN, preferred_element_type=F32)
        s_ref[...] = s_ref[...] * chunk_decay + jnp.where(mask, upd, 0.0)


def _ret_out_kernel(q_ref, k_ref, v_ref, g_ref, sb_ref, o_ref, s_ref, *, cps):
    c = RET_CHUNK

    @pl.when(pl.program_id(1) == 0)
    def _():
        s_ref[...] = jnp.zeros_like(s_ref)

    lane_head = _ret_lane_head()
    lg = _ret_log_gamma(lane_head)
    idx = lax.broadcasted_iota(jnp.int32, (c, 1), 0).astype(F32)
    q_decay_f = jnp.exp(lg * (idx + 1.0))
    q_decay_b = jnp.exp(lg * (float(c) - idx))
    k_decay = jnp.exp(lg * (float(c) - 1.0 - idx))
    chunk_decay = _ret_chunk_decay()
    mask = _ret_state_mask()
    ii =lax.broadcasted_iota(jnp.int32, (c, c), 0)
    jj = lax.broadcasted_iota(jnp.int32, (c, c), 1)
    dist = jnp.abs(ii - jj).astype(F32)
    intra_decay = [jnp.exp(math.log1p(-(2.0 ** (-5.0 - h))) * dist) for h in range(RET_HEADS)]

    for cc in range(cps):
        rows = slice(cc * c, (cc + 1) * c)
        q = q_ref[rows, :]
        k = k_ref[rows, :]
        v = v_ref[rows, :]
        qf = (q * q_decay_f).astype(BF16)
        qb = (q * q_decay_b).astype(BF16)
        cross = (jnp.dot(qf, s_ref[...].astype(BF16), preferred_element_type=F32)
                 + jnp.dot(qb, sb_ref[0, cc], preferred_element_type=F32))
        q_heads = jnp.concatenate([jnp.where(lane_head == h, q, jnp.zeros_like(q)) for h in range(RET_HEADS)],
                                  axis=0)
        scores = lax.dot_general(q_heads, k, _NT, preferred_element_type=F32)
        for h in range(RET_HEADS):
            s = scores[h * c:(h + 1) * c, :] * intra_decay[h]
            cols = slice(h * RET_DV, (h + 1) * RET_DV)
            o = jnp.dot(s.astype(BF16), v[:, cols], preferred_element_type=F32) + cross[:, cols]
            mu = jnp.mean(o, axis=-1, keepdims=True)
            d = o - mu
            var = jnp.mean(d * d, axis=-1, keepdims=True)
            gate = _silu(g_ref[rows, cols].astype(F32))
            o_ref[rows, cols] = (gate * (d * lax.rsqrt(var + LN_EPS))).astype(BF16)
        kf = (k * k_decay).astype(BF16)
        upd = lax.dot_general(kf, v, _TN, preferred_element_type=F32)
        s_ref[...] = s_ref[...] * chunk_decay + jnp.where(mask, upd, 0.0)


def _retention(proj, batch, seq_len):
    t = proj.shape[0]
    c = RET_CHUNK
    cps = min(RET_CHUNKS_PER_STEP, seq_len // c)
    rows = cps * c
    ns = seq_len // rows
    dv = RET_HEADS * RET_DV
    state_shape = (2 * LANES, dv)
    rev = lambda b, n: b * ns + (ns - 1 - n)
    fwd = lambda b, n: b * ns + n
    sb = pl.pallas_call(
        functools.partial(_ret_bstate_kernel, cps=cps),
        grid=(batch, ns),
        in_specs=[
            pl.BlockSpec((rows, 2 * LANES), lambda b, n: (rev(b, n), 1)),
            pl.BlockSpec((rows, dv), lambda b, n: (rev(b, n), 1)),
        ],
        out_specs=pl.BlockSpec((1, cps) + state_shape, lambda b, n: (b, ns - 1 - n, 0, 0)),
        out_shape=jax.ShapeDtypeStruct((batch, ns * cps) + state_shape, BF16),
        scratch_shapes=[pltpu.VMEM(state_shape, F32)],
        compiler_params=_params(("arbitrary", "arbitrary")),
        name="ret_bstate",
    )(proj, proj)
    return pl.pallas_call(
        functools.partial(_ret_out_kernel, cps=cps),
        grid=(batch, ns),
        in_specs=[
            pl.BlockSpec((rows, 2 * LANES), lambda b, n: (fwd(b, n), 0)),
            pl.BlockSpec((rows, 2 * LANES), lambda b, n: (fwd(b, n), 1)),
            pl.BlockSpec((rows, dv), lambda b, n: (fwd(b, n), 1)),
            pl.BlockSpec((rows, dv), lambda b, n: (fwd(b, n), 2)),
            pl.BlockSpec((1, cps) + state_shape, lambda b, n: (b, n, 0, 0)),
        ],
        out_specs=pl.BlockSpec((rows, dv), lambda b, n: (fwd(b, n), 0)),
        out_shape=jax.ShapeDtypeStruct((t, dv), BF16),
        scratch_shapes=[pltpu.VMEM(state_shape, F32)],
        compiler_params=_params(("arbitrary", "arbitrary")),
        name="ret_out",
    )(proj, proj, proj, proj, sb)


def _lru_kernel(xfp_ref, xf_ref, xfn_ref, xbp_ref, xb_ref, xbn_ref, cw_ref, cb_ref, gw_ref, gb_ref,
                lam_ref, hf_ref, hb_ref, xx_ref, a_ref, b_ref, h_ref, *, nt, ts, batch):
    i = pl.program_id(0)
    halo = SUBLANES
    lo = LRU_CONV // 2

    @pl.when(i == 0)
    def _():
        h_ref[...] = jnp.zeros_like(h_ref)

    def prepare(xp_ref, x_ref, xn_ref, tile, z, slot):
        for b in range(batch):
            xx_ref[0:halo, :] = jnp.where(tile == 0, 0.0, xp_ref[b])
            xx_ref[halo:halo + ts, :] = x_ref[b]
            xx_ref[halo + ts:2 * halo + ts, :] = jnp.where(tile == nt - 1, 0.0, xn_ref[b])
            xx = xx_ref[...]
            xc = cb_ref[...]
            for tap in range(LRU_CONV):
                shift = (lo - tap) % (ts + 2 * halo)
                moved = xx if shift == 0 else pltpu.roll(xx, shift, axis=0)
                xc = xc + cw_ref[tap:tap + 1, :] * moved[halo:halo + ts, :]
            for n in range(LRU_BLOCKS):
                cols = slice(n * LRU_BW, (n + 1) * LRU_BW)
                xn = xc[:, cols]
                g = (jnp.dot(xn.astype(BF16), gw_ref[z, n], preferred_element_type=F32)
                     + gb_ref[z, n:n + 1, :])
                r = _sigmoid(g[:, :LRU_BW])
                ig = _sigmoid(g[:, LRU_BW:])
                a = jnp.exp((-LRU_C) * r * _softplus(-lam_ref[z, :, cols]))
                a_ref[slot + b, :, cols] = a
                b_ref[slot + b, :, cols] = jnp.sqrt(1.0 - a * a) * (ig * xn)

    prepare(xfp_ref, xf_ref, xfn_ref, i, 0, 0)
    prepare(xbp_ref, xb_ref, xbn_ref, nt - 1 - i, 1, batch)

    def step(s, hs):
        out = []
        for k in range(2 * batch):
            row = s if k < batch else ts - 1 - s
            h = a_ref[k, pl.ds(row, 1), :] * hs[k] + b_ref[k, pl.ds(row, 1), :]
            if k < batch:
                hf_ref[k, pl.ds(row, 1), :] = h
            else:
                hb_ref[k - batch, pl.ds(row, 1), :] = h
            out.append(h)
        return tuple(out)

    hs = lax.fori_loop(0, ts, step, tuple(h_ref[k] for k in range(2 * batch)), unroll=8)
    for k in range(2 * batch):
        h_ref[k] = hs[k]


def _rglru(proj_f32, conv_w, conv_b, gate_w, gate_b, lam, batch, seq_len):
    w = LRU_BLOCKS * LRU_BW
    ts = min(LRU_ROWS, seq_len)
    nt = seq_len // ts
    rows8 = ts // SUBLANES
    last8 = seq_len // SUBLANES - 1
    x3 = proj_f32.reshape(batch, seq_len, proj_f32.shape[1])
    gw = jnp.concatenate([gate_w[:, 0], gate_w[:, 1]], axis=-1).astype(BF16)
    gb = jnp.concatenate([gate_b[:, 0], gate_b[:, 1]], axis=-1)
    bwd = lambda i: nt - 1 - i

    def tile_specs(tile):
        return [
            pl.BlockSpec((batch, SUBLANES, w), lambda i: (0, jnp.maximum(tile(i) * rows8 - 1, 0), 0)),
            pl.BlockSpec((batch, ts, w), lambda i: (0, tile(i), 0)),
            pl.BlockSpec((batch, SUBLANES, w), lambda i: (0, jnp.minimum((tile(i) + 1) * rows8, last8), 0)),
        ]

    full = lambda a: pl.BlockSpec(a.shape, lambda i: (0,) * a.ndim)
    cb = conv_b.reshape(1, w)
    lam3 = lam.reshape(2, 1, w)
    state = jax.ShapeDtypeStruct((batch, seq_len, w), F32)
    h_fwd, h_bwd = pl.pallas_call(
        functools.partial(_lru_kernel, nt=nt, ts=ts, batch=batch),
        grid=(nt,),
        in_specs=tile_specs(lambda i: i) + tile_specs(bwd) + [full(conv_w), full(cb), full(gw), full(gb),
                                                               full(lam3)],
        out_specs=[pl.BlockSpec((batch, ts, w), lambda i: (0, i, 0)),
                   pl.BlockSpec((batch, ts, w), lambda i: (0, bwd(i), 0))],
        out_shape=[state, state],
        scratch_shapes=[pltpu.VMEM((ts + 2 * SUBLANES, w), F32), pltpu.VMEM((2 * batch, ts, w), F32),
                        pltpu.VMEM((2 * batch, ts, w), F32), pltpu.VMEM((2 * batch, 1, w), F32)],
        compiler_params=_params(("arbitrary",)),
        name="lru_scan",
    )(x3, x3, x3, x3, x3, x3, conv_w, cb, gw, gb, lam3)
    return h_fwd.reshape(batch * seq_len, w), h_bwd.reshape(batch * seq_len, w)


def _gla_kernel(qf_ref, vf_ref, kf_ref, lf_ref, qb_ref, vb_ref, kb_ref, lb_ref, of_ref, ob_ref, s_ref,
                *, ts, batch):
    c = GLA_CHUNK
    nchunks = ts // c

    @pl.when(pl.program_id(0) == 0)
    def _():
        s_ref[...] = jnp.zeros_like(s_ref)

    head_cols = [slice(h * HG_DK, (h + 1) * HG_DK) for h in range(HG_HEADS)]
    hc = HG_HEADS * c
    ii = lax.broadcasted_iota(jnp.int32, (c, c), 0)
    jj = lax.broadcasted_iota(jnp.int32, (c, c), 1)
    si = lax.broadcasted_iota(jnp.int32, (hc, hc), 0)
    sj = lax.broadcasted_iota(jnp.int32, (hc, hc), 1)
    same_head = si // c == sj // c

    def stack(a):
        return jnp.concatenate([a[:, cols] for cols in head_cols], axis=0)

    tri = {False: (jj <= ii).astype(BF16), True: (jj >= ii).astype(BF16)}
    tri2 = {r: jnp.concatenate([m, m], axis=1) for r, m in tri.items()}
    zero_block = jnp.zeros((c, HG_DK), BF16)
    zero_state = jnp.zeros((HG_DK, HG_DV), BF16)
    keep = {False: jnp.logical_and(same_head, sj <= si), True: jnp.logical_and(same_head, sj >= si)}
    chains = ([(qf_ref, vf_ref, kf_ref, lf_ref, of_ref, b, False) for b in range(batch)]
              + [(qb_ref, vb_ref, kb_ref, lb_ref, ob_ref, b, True) for b in range(batch)])

    def gates(cc):
        work = []
        for chain, (q_ref, v_ref, k_ref, l_ref, o_ref, b, reverse) in enumerate(chains):
            rows = pl.ds(((nchunks - 1 - cc) if reverse else cc) * c, c)
            log_f = l_ref[b, rows, :]
            f_hi = log_f.astype(BF16)
            f_lo = (log_f - f_hi.astype(F32)).astype(BF16)
            work.append(dict(rows=rows, reverse=reverse, o_ref=o_ref, b=b, chain=chain,
                             key=k_ref[b, rows, :].astype(F32), qs=q_ref[b, rows, :],
                             v=v_ref[b, rows, :], split=(f_hi, f_lo)))
        return work

    def cumulate(work):
        for w in work:
            f_hi, f_lo = w["split"]
            w["bcum"] = jnp.dot(tri2[w["reverse"]], jnp.concatenate([f_lo, f_hi], axis=0),
                                preferred_element_type=F32)

    half = c // 2
    row = lax.broadcasted_iota(jnp.int32, (c, 1), 0)
    first_half = row < half

    def decays(work):
        for w in work:
            bcum = w["bcum"]
            rev = w["reverse"]
            end = 0 if rev else c - 1
            b_end = bcum[end:end + 1, :]
            ref = jnp.where(first_half, bcum[half // 2:half // 2 + 1, :], bcum[half + half // 2:half + half // 2 + 1, :])
            qe = w["qs"] * jnp.exp2(bcum - ref).astype(BF16)
            ke = w["key"] * jnp.exp2(ref - bcum)
            boundary = bcum[half:half + 1, :] if rev else bcum[half - 1:half, :]
            cross = jnp.exp2(-jnp.abs(bcum - boundary))
            queries = first_half if rev else jnp.logical_not(first_half)
            zero = jnp.zeros_like(qe)
            w["q3"] = [jnp.where(first_half, qe, zero), jnp.where(first_half, zero, qe),
                       jnp.where(queries, w["qs"] * cross.astype(BF16), zero)]
            w["k3"] = [jnp.where(first_half, ke, 0.0), jnp.where(first_half, 0.0, ke),
                       jnp.where(queries, 0.0, w["key"] * cross)]
            w["qd"] = w["qs"] * jnp.exp2(bcum).astype(BF16)
            w["kd"] = (w["key"] * jnp.exp2(b_end - bcum)).astype(BF16)
            w["decay"] = jnp.exp2(b_end)

    def stack3(parts):
        return jnp.concatenate([jnp.concatenate([p[:, cols] for p in parts], axis=1) for cols in head_cols], axis=0)

    def scores(work):
        for w in work:
            w["att"] = jnp.dot(stack3(w["q3"]), stack3(w["k3"]).T.astype(BF16), preferred_element_type=F32)

    def intra(work):
        for w in work:
            att = jnp.where(keep[w["reverse"]], w["att"], 0.0).astype(BF16)
            w["intra"] = jnp.dot(att, stack(w["v"]), preferred_element_type=F32)

    def inter(work):
        for w in work:
            w["st"] = [s_ref[w["chain"], h] for h in range(HG_HEADS)]
            out = []
            for h in range(0, HG_HEADS, 2):
                sa = w["st"][h].T.astype(BF16)
                sb = w["st"][h + 1].T.astype(BF16)
                rhs = jnp.concatenate([jnp.concatenate([sa, zero_state], axis=1),
                                       jnp.concatenate([zero_state, sb], axis=1)], axis=0)
                pair = jnp.dot(w["qd"][:, h * HG_DK:(h + 2) * HG_DK], rhs, preferred_element_type=F32)
                out += [pair[:, :HG_DV], pair[:, HG_DV:]]
            w["inter"] = out

    def update(work):
        for w in work:
            upd = []
            for h in range(0, HG_HEADS, 2):
                ca, cb = head_cols[h], head_cols[h + 1]
                lhs = jnp.concatenate([w["v"][:, ca], w["v"][:, cb]], axis=0)
                rhs = jnp.concatenate([jnp.concatenate([w["kd"][:, ca], zero_block], axis=1),
                                       jnp.concatenate([zero_block, w["kd"][:, cb]], axis=1)], axis=0)
                pair = lax.dot_general(lhs, rhs, _TN, preferred_element_type=F32)
                upd += [pair[:, :HG_DK], pair[:, HG_DK:]]
            w["upd"] = upd

    def finish(work):
        for w in work:
            k = w["chain"]
            for h, cols in enumerate(head_cols):
                w["o_ref"][w["b"], w["rows"], cols] = w["intra"][h * c:(h + 1) * c, :] + w["inter"][h]
                s_ref[k, h] = w["st"][h] * w["decay"][:, cols] + w["upd"][h]

    cur = gates(0)
    cumulate(cur)
    decays(cur)
    for cc in range(nchunks):
        more = cc + 1 < nchunks
        scores(cur)
        nxt = gates(cc + 1) if more else None
        if more:
            cumulate(nxt)
        intra(cur)
        inter(cur)
        update(cur)
        if more:
            decays(nxt)
        finish(cur)
        cur = nxt


def _gla(proj_bf16, proj_f32, batch, seq_len):
    d = HG_HEADS * HG_DK
    ts = min(GLA_ROWS, seq_len)
    nt = seq_len // ts
    pb = proj_bf16.reshape(batch, seq_len, proj_bf16.shape[1])
    pf = proj_f32.reshape(batch, seq_len, proj_f32.shape[1])
    bwd = lambda i: nt - 1 - i
    blk = lambda tile, col: pl.BlockSpec((batch, ts, d), lambda i: (0, tile(i), col))
    fwd = lambda i: i
    out = jax.ShapeDtypeStruct((batch, seq_len, d), F32)
    o_fwd, o_bwd = pl.pallas_call(
        functools.partial(_gla_kernel, ts=ts, batch=batch),
        grid=(nt,),
        in_specs=[blk(fwd, 0), blk(fwd, 1), blk(fwd, 3), blk(fwd, 0),
                  blk(bwd, 0), blk(bwd, 1), blk(bwd, 4), blk(bwd, 1)],
        out_specs=[blk(fwd, 0), blk(bwd, 0)],
        out_shape=[out, out],
        scratch_shapes=[pltpu.VMEM((2 * batch, HG_HEADS, HG_DV, HG_DK), F32)],
        compiler_params=_params(("arbitrary",)),
        name="gla_scan",
    )(pb, pb, pb, pf, pb, pb, pb, pf)
    return o_fwd.reshape(batch * seq_len, d), o_bwd.reshape(batch * seq_len, d)


def _top2(p):
    v1 = jnp.maximum(jnp.maximum(p[0], p[1]), jnp.maximum(p[2], p[3]))
    i1 = jnp.where(p[0] == v1, 0, jnp.where(p[1] == v1, 1, jnp.where(p[2] == v1, 2, 3)))
    q = [jnp.where(i1 == k, -1.0, p[k]) for k in range(4)]
    v2 = jnp.maximum(jnp.maximum(q[0], q[1]), jnp.maximum(q[2], q[3]))
    i2 = jnp.where(q[0] == v2, 0, jnp.where(q[1] == v2, 1, jnp.where(q[2] == v2, 2, 3)))
    return v1, i1, v2, i2


def _outproj_kernel(*refs, even, alpha, tm):
    if even:
        ret_ref, hf_ref, hb_ref, gr_ref, w0_ref, w1_ref = refs[:6]
        rest = refs[6:]
        lru = ((hf_ref[...] + hb_ref[...]) * jax.nn.gelu(gr_ref[...])).astype(BF16)
        y = (jnp.dot(ret_ref[...], w0_ref[...], preferred_element_type=F32)
             + jnp.dot(lru, w1_ref[...], preferred_element_type=F32))
    else:
        of_ref, ob_ref, g_ref, nw_ref, w0_ref = refs[:5]
        rest = refs[5:]
        o = of_ref[...] + ob_ref[...]
        ms = jnp.mean(o * o, axis=-1, keepdims=True)
        mix = o * lax.rsqrt(ms + RMS_EPS) * nw_ref[...] * _silu(g_ref[...].astype(F32))
        y = jnp.dot(mix.astype(BF16), w0_ref[...], preferred_element_type=F32)
    h_ref, lnw_ref, lnb_ref, rw_ref, o_ref, info_ref, carry_ref, before_ref = rest
    i = pl.program_id(0)

    @pl.when(i == 0)
    def _():
        carry_ref[...] = jnp.zeros_like(carry_ref)
        tt = lax.broadcasted_iota(jnp.int32, (tm, tm), 0)
        uu = lax.broadcasted_iota(jnp.int32, (tm, tm), 1)
        before_ref[...] = (tt < uu).astype(BF16)

    h1 = _layer_norm(alpha * h_ref[...] + y, lnw_ref[...], lnb_ref[...])
    o_ref[...] = h1

    h_hi = h1.astype(BF16)
    h_lo = (h1 - h_hi.astype(F32)).astype(BF16)
    rw = rw_ref[...]
    r_hi = rw.astype(BF16)
    r_lo = (rw - r_hi.astype(F32)).astype(BF16)
    both = lax.dot_general(jnp.concatenate([r_hi, r_lo], axis=0), h_hi, _NT, preferred_element_type=F32)
    logits = (both[N_EXPERTS:, :] + lax.dot_general(r_hi, h_lo, _NT, preferred_element_type=F32)
              + both[:N_EXPERTS, :])
    ex = jnp.exp(logits - jnp.max(logits, axis=0, keepdims=True))
    probs = ex / jnp.sum(ex, axis=0, keepdims=True)
    best = None
    for g in range(N_GROUPS):
        rows = [probs[g * EXPERTS_PER_GROUP + k:g * EXPERTS_PER_GROUP + k + 1, :]
                for k in range(EXPERTS_PER_GROUP)]
        v1, i1, v2, i2 = _top2(rows)
        cand = (v1 + v2, v1, i1 + g * EXPERTS_PER_GROUP, v2, i2 + g * EXPERTS_PER_GROUP)
        if best is None:
            best = cand
        else:
            take = cand[0] > best[0]
            best = tuple(jnp.where(take, cn, bs) for cn, bs in zip(cand, best))
    _, v1, e1, v2, e2 = best
    denom = v1 + v2
    g1 = v1 / denom
    g2 = v2 / denom

    eid = lax.broadcasted_iota(jnp.int32, (N_EXPERTS, tm), 0)
    oh1 = (eid == e1).astype(F32)
    oh2 = (eid == e2).astype(F32)
    oh = oh1 + oh2
    base = carry_ref[:, 0:1] + jnp.dot(oh.astype(BF16), before_ref[...], preferred_element_type=F32)
    rank1 = jnp.sum(oh1 * base, axis=0, keepdims=True)
    rank2 = jnp.sum(oh2 * base, axis=0, keepdims=True)
    carry_ref[...] = carry_ref[...] + jnp.sum(oh, axis=1, keepdims=True)
    zero = jnp.zeros_like(g1)
    info_ref[...] = jnp.concatenate(
        [e1.astype(F32), e2.astype(F32), g1, g2, rank1, rank2, zero, zero], axis=0)


def _out_project(even, mixer_inputs, weights_bf16, h, ln_w, ln_b, router_wt, alpha):
    t, d = h.shape
    tm = TOKEN_ROWS
    row = lambda i: (i, 0)
    const = lambda i: (0, 0)
    if even:
        ret, h_fwd, h_bwd, proj_f32 = mixer_inputs
        w = h_fwd.shape[1]
        in_specs = [pl.BlockSpec((tm, ret.shape[1]), row), pl.BlockSpec((tm, w), row),
                    pl.BlockSpec((tm, w), row), pl.BlockSpec((tm, w), lambda i: (i, 1))]
        args = [ret, h_fwd, h_bwd, proj_f32]
    else:
        o_fwd, o_bwd, proj_bf16, norm_w = mixer_inputs
        in_specs = [pl.BlockSpec((tm, d), row), pl.BlockSpec((tm, d), row),
                    pl.BlockSpec((tm, d), lambda i: (i, 2)), pl.BlockSpec((1, d), const)]
        args = [o_fwd, o_bwd, proj_bf16, norm_w.reshape(1, d)]
    in_specs += [pl.BlockSpec(wm.shape, const) for wm in weights_bf16]
    in_specs += [pl.BlockSpec((tm, d), row), pl.BlockSpec((1, d), const), pl.BlockSpec((1, d), const),
                 pl.BlockSpec((N_EXPERTS, d), const)]
    args += list(weights_bf16) + [h, ln_w.reshape(1, d), ln_b.reshape(1, d), router_wt]
    return pl.pallas_call(
        functools.partial(_outproj_kernel, even=even, alpha=alpha, tm=tm),
        grid=(t // tm,),
        in_specs=in_specs,
        out_specs=[pl.BlockSpec((tm, d), row),
                   pl.BlockSpec((SUBLANES, tm), lambda i: (0, i))],
        out_shape=[jax.ShapeDtypeStruct((t, d), F32),
                   jax.ShapeDtypeStruct((SUBLANES, t), F32)],
        scratch_shapes=[pltpu.VMEM((N_EXPERTS, LANES), F32), pltpu.VMEM((tm, tm), BF16)],
        compiler_params=_params(("arbitrary",)),
        name="out_proj_router",
    )(*args)


def _pack_halves(x):
    n = x.shape[1] // 2
    hi = pltpu.bitcast(x[:, :n], jnp.uint32)
    lo = pltpu.bitcast(x[:, n:], jnp.uint32)
    return hi | (lo >> 16)


def _unpack_halves(p):
    hi = pltpu.bitcast(p & jnp.uint32(0xFFFF0000), F32)
    lo = pltpu.bitcast(p << 16, F32)
    return jnp.concatenate([hi.astype(BF16), lo.astype(BF16)], axis=1)


def _seg_expert_kernel(first_ref, count_ref, tail_ref, x_ref, wg_ref, wu_ref, wd_ref, y_ref,
                       xbuf, ybuf, wgb_ref, wub_ref, wdb_ref, semx, semy):
    e = pl.program_id(0)
    wgb_ref[...] = wg_ref[...].astype(BF16)
    wub_ref[...] = wu_ref[...].astype(BF16)
    wdb_ref[...] = wd_ref[...].astype(BF16)
    first = first_ref[e]
    n_used = tail_ref[0]

    def rows(g):
        return pl.ds(pl.multiple_of(g * MOE_ROWS, MOE_ROWS), MOE_ROWS)

    def fetch(g):
        return pltpu.make_async_copy(x_ref.at[rows(g)], xbuf.at[g % 2], semx.at[g % 2])

    def put(g):
        return pltpu.make_async_copy(ybuf.at[g % 2], y_ref.at[rows(g)], semy.at[g % 2])

    @pl.when(e == 0)
    def _():
        fetch(0).start()

    def body(g, carry):
        @pl.when(g + 1 < n_used)
        def _():
            fetch(g + 1).start()

        fetch(g).wait()

        @pl.when(g >= 2)
        def _():
            put(g - 2).wait()

        x = _unpack_halves(xbuf[g % 2])
        gate = jnp.dot(x, wgb_ref[...], preferred_element_type=F32)
        up = jnp.dot(x, wub_ref[...], preferred_element_type=F32)
        hid = (_silu(gate) * up).astype(BF16)
        y = jnp.dot(hid, wdb_ref[...], preferred_element_type=F32)
        ybuf[g % 2] = _pack_halves(y.astype(BF16).astype(F32))
        put(g).start()
        return carry

    lax.fori_loop(first, first + count_ref[e], body, 0)

    @pl.when(e == pl.num_programs(0) - 1)
    def _():
        @pl.when(n_used >= 2)
        def _():
            put(n_used - 2).wait()

        put(n_used - 1).wait()
        ybuf[0] = jnp.zeros_like(ybuf[0])

        def clear(b, carry):
            cp = pltpu.make_async_copy(
                ybuf.at[0], y_ref.at[pl.ds(pl.multiple_of(b * MOE_ROWS, MOE_ROWS), MOE_ROWS)], semy.at[0])
            cp.start()
            cp.wait()
            return carry

        lax.fori_loop(tail_ref[0], y_ref.shape[0] // MOE_ROWS, clear, 0)


def _seg_experts(xb, seg_first, seg_count, n_valid, w_gate, w_up, w_down, layer):
    p, half = xb.shape
    d = 2 * half
    de = w_gate.shape[3]
    weight = lambda shape: pl.BlockSpec((None, None) + shape, lambda e, a, b, c: (layer, e, 0, 0))
    return pl.pallas_call(
        _seg_expert_kernel,
        grid_spec=pltpu.PrefetchScalarGridSpec(
            num_scalar_prefetch=3,
            grid=(N_EXPERTS,),
            in_specs=[pl.BlockSpec(memory_space=pl.ANY), weight((d, de)), weight((d, de)), weight((de, d))],
            out_specs=pl.BlockSpec(memory_space=pl.ANY),
            scratch_shapes=[pltpu.VMEM((2, MOE_ROWS, half), jnp.uint32),
                            pltpu.VMEM((2, MOE_ROWS, half), jnp.uint32),
                            pltpu.VMEM((d, de), BF16), pltpu.VMEM((d, de), BF16), pltpu.VMEM((de, d), BF16),
                            pltpu.SemaphoreType.DMA((2,)), pltpu.SemaphoreType.DMA((2,))],
        ),
        out_shape=jax.ShapeDtypeStruct((p, half), jnp.uint32),
        compiler_params=_params(("arbitrary",)),
        name="moe_experts",
    )(seg_first, seg_count, n_valid, xb, w_gate, w_up, w_down)


def _clear_padding_blocks(zs_ref, xb_ref, zero_ref, sem):
    zero_ref[...] = jnp.zeros_like(zero_ref)

    def clear(row):
        start = pl.multiple_of(row, MOE_ROWS)
        return pltpu.make_async_copy(zero_ref, xb_ref.at[pl.ds(start, MOE_ROWS)], sem)

    for e in range(N_EXPERTS):
        clear(zs_ref[e]).start()
    for e in range(N_EXPERTS):
        clear(zs_ref[e]).wait()

    def clear_tail(b, carry):
        clear(b * MOE_ROWS).start()
        clear(b * MOE_ROWS).wait()
        return carry

    lax.fori_loop(zs_ref[N_EXPERTS], xb_ref.shape[0] // MOE_ROWS, clear_tail, 0)


def _run_copy(src_ref, src_row, dst_ref, dst_row, sem):
    src = pl.multiple_of(src_row, SUBLANES)
    dst = pl.multiple_of(dst_row, SUBLANES)
    return pltpu.make_async_copy(src_ref.at[pl.ds(src, SUBLANES)], dst_ref.at[pl.ds(dst, SUBLANES)], sem)


def _start_runs(count, make_copy):
    def pair(p, carry):
        make_copy(2 * p).start()
        make_copy(2 * p + 1).start(priority=1)
        return carry

    lax.fori_loop(0, count // 2, pair, 0)

    @pl.when(count % 2 == 1)
    def _():
        make_copy(count - 1).start()


def _sort_dispatch_kernel(dst_ref, nch_ref, zs_ref, h_ref, lp_ref, xb_ref, xs_ref, zero_ref, sem, *, tm, nck):
    i = pl.program_id(0)

    @pl.when(i == 0)
    def _():
        _clear_padding_blocks(zs_ref, xb_ref, zero_ref, sem.at[0])

    sub = SORT_TILES_PER_STEP
    last = pl.num_programs(0) - 1

    def wait_runs(buf, count):
        def wait(j, carry):
            _run_copy(xs_ref.at[buf], 0, xb_ref, 0, sem.at[buf]).wait()
            return carry

        lax.fori_loop(0, count, wait, 0)

    pos = lax.broadcasted_iota(jnp.int32, (xs_ref.shape[1], tm), 0)
    for s in range(sub):
        tile = i * sub + s
        buf = (i % 2) * sub + s

        @pl.when(i >= 2)
        def _():
            wait_runs(buf, nch_ref[jnp.maximum(tile - 2 * sub, 0)])

        lp = lp_ref[:, s * tm:(s + 1) * tm]
        perm = jnp.logical_or(pos == lp[0:1, :], pos == lp[1:2, :]).astype(BF16)
        rows = h_ref[s * tm:(s + 1) * tm, :].astype(BF16)
        xs_ref[buf] = _pack_halves(jnp.dot(perm, rows, preferred_element_type=F32))

        _start_runs(nch_ref[tile], lambda j: _run_copy(xs_ref.at[buf], j * SUBLANES, xb_ref,
                                                       dst_ref[tile * nck + j], sem.at[buf]))

    @pl.when(i == last)
    def _():
        for s in range(sub):
            @pl.when(i >= 1)
            def _():
                wait_runs((1 - i % 2) * sub + s, nch_ref[jnp.maximum((i - 1) * sub + s, 0)])

            wait_runs((i % 2) * sub + s, nch_ref[i * sub + s])


def _sort_dispatch(h, lp_rows, chunk_dst, n_chunks, zero_start, n_rows, tm, sorted_rows):
    t, d = h.shape
    nck = sorted_rows // SUBLANES
    sub = SORT_TILES_PER_STEP
    return pl.pallas_call(
        functools.partial(_sort_dispatch_kernel, tm=tm, nck=nck),
        grid_spec=pltpu.PrefetchScalarGridSpec(
            num_scalar_prefetch=3,
            grid=(t // (sub * tm),),
            in_specs=[pl.BlockSpec((sub * tm, d), lambda i, a, b, c: (i, 0)),
                      pl.BlockSpec((2, sub * tm), lambda i, a, b, c: (0, i))],
            out_specs=pl.BlockSpec(memory_space=pl.ANY),
            scratch_shapes=[pltpu.VMEM((2 * sub, sorted_rows, d // 2), jnp.uint32),
                            pltpu.VMEM((MOE_ROWS, d // 2), jnp.uint32), pltpu.SemaphoreType.DMA((2 * sub,))],
        ),
        out_shape=jax.ShapeDtypeStruct((n_rows, d // 2), jnp.uint32),
        compiler_params=_params(("arbitrary",)),
        name="moe_dispatch",
    )(chunk_dst, n_chunks, zero_start, h, lp_rows)


def _sort_combine_kernel(src_ref, nch_ref, h_ref, lp_ref, gates_ref, lnw_ref, lnb_ref, yb_ref, o_ref,
                         ys_ref, sem, *, tm, nck, alpha):
    i = pl.program_id(0)
    sub = SORT_TILES_PER_STEP

    def fetch(tile, buf):
        _start_runs(nch_ref[tile], lambda j: _run_copy(yb_ref, src_ref[tile * nck + j], ys_ref.at[buf],
                                                       j * SUBLANES, sem.at[buf]))

    @pl.when(i == 0)
    def _():
        ys_ref[...] = jnp.zeros_like(ys_ref)
        for s in range(sub):
            fetch(s, s)

    @pl.when(i + 1 < pl.num_programs(0))
    def _():
        for s in range(sub):
            fetch((i + 1) * sub + s, (1 - i % 2) * sub + s)

    pos = lax.broadcasted_iota(jnp.int32, (tm, ys_ref.shape[1]), 1)
    for s in range(sub):
        buf = (i % 2) * sub + s
        rows = slice(s * tm, (s + 1) * tm)

        def wait(j, carry):
            _run_copy(yb_ref, 0, ys_ref.at[buf], 0, sem.at[buf]).wait()
            return carry

        lax.fori_loop(0, nch_ref[i * sub + s], wait, 0)

        lp = lp_ref[rows, :]
        pick = (jnp.where(pos == lp[:, 0:1], gates_ref[rows, 0:1], 0.0)
                + jnp.where(pos == lp[:, 1:2], gates_ref[rows, 1:2], 0.0))
        y = jnp.dot(pick.astype(BF16), _unpack_halves(ys_ref[buf]), preferred_element_type=F32)
        o_ref[rows, :] = _layer_norm(alpha * h_ref[rows, :] + y, lnw_ref[...], lnb_ref[...])


def _sort_combine(h, yb, lp_cols, gates, chunk_src, n_chunks, ln_w, ln_b, alpha, tm, sorted_rows):
    t, d = h.shape
    nck = sorted_rows // SUBLANES
    step = SORT_TILES_PER_STEP * tm
    row = lambda i, a, b: (i, 0)
    const = lambda i, a, b: (0, 0)
    return pl.pallas_call(
        functools.partial(_sort_combine_kernel, tm=tm, nck=nck, alpha=alpha),
        grid_spec=pltpu.PrefetchScalarGridSpec(
            num_scalar_prefetch=2,
            grid=(t // step,),
            in_specs=[pl.BlockSpec((step, d), row), pl.BlockSpec((step, 2), row), pl.BlockSpec((step, 2), row),
                      pl.BlockSpec((1, d), const), pl.BlockSpec((1, d), const),
                      pl.BlockSpec(memory_space=pl.ANY)],
            out_specs=pl.BlockSpec((step, d), row),
            scratch_shapes=[pltpu.VMEM((2 * SORT_TILES_PER_STEP, sorted_rows, d // 2), jnp.uint32),
                            pltpu.SemaphoreType.DMA((2 * SORT_TILES_PER_STEP,))],
        ),
        out_shape=jax.ShapeDtypeStruct((t, d), F32),
        compiler_params=_params(("arbitrary",)),
        name="moe_combine",
    )(chunk_src, n_chunks, h, lp_cols, gates, ln_w.reshape(1, d), ln_b.reshape(1, d), yb)


def _moe(h1, info, w_gate, w_up, w_down, layer, ln_w, ln_b, alpha):
    t = h1.shape[0]
    tm = SORT_ROWS
    ntile = t // tm
    run_pad = SUBLANES - 1
    sorted_rows = 2 * tm + LANES
    nck = sorted_rows // SUBLANES
    n_rows = (2 * t + ntile * N_EXPERTS * run_pad + MOE_ROWS - 1) // MOE_ROWS * MOE_ROWS + N_EXPERTS * MOE_ROWS
    i32 = jnp.int32
    e1, e2 = info[0].astype(i32), info[1].astype(i32)
    rank1, rank2 = info[4].astype(i32), info[5].astype(i32)
    expert_ids = jnp.arange(N_EXPERTS, dtype=i32)[None, :]
    oh1 = e1[:, None] == expert_ids
    oh2 = e2[:, None] == expert_ids
    cnt = jnp.logical_or(oh1, oh2).astype(i32).reshape(ntile, tm, N_EXPERTS).sum(axis=1)
    cnt8 = (cnt + run_pad) // SUBLANES * SUBLANES
    local = jnp.cumsum(cnt8, axis=1) - cnt8
    before = jnp.cumsum(cnt, axis=0) - cnt
    seg = cnt8.sum(axis=0)
    seg_pad = (seg + MOE_ROWS - 1) // MOE_ROWS * MOE_ROWS
    ends = jnp.cumsum(seg_pad)
    slot = (ends - seg_pad)[None, :] + jnp.cumsum(cnt8, axis=0) - cnt8
    shift = jnp.repeat(local - before, tm, axis=0)
    lp1 = jnp.sum(jnp.where(oh1, shift, 0), axis=1) + rank1
    lp2 = jnp.sum(jnp.where(oh2, shift, 0), axis=1) + rank2
    chunk_row = jnp.arange(nck, dtype=i32) * SUBLANES
    run_of = jnp.sum(((local + cnt8)[:, None, :] <= chunk_row[None, :, None]).astype(i32), axis=2)
    run_of = jnp.minimum(run_of, N_EXPERTS - 1)
    chunk_slot = jnp.sum(jnp.where(run_of[:, :, None] == expert_ids[None], (slot - local)[:, None, :], 0),
                         axis=2) + chunk_row[None, :]
    n_chunks = cnt8.sum(axis=1) // SUBLANES
    n_valid = (ends[-1] // MOE_ROWS).astype(i32).reshape(1)
    zero_start = jnp.concatenate([jnp.maximum(ends - MOE_ROWS, 0), n_valid]).astype(i32)
    gates = jnp.stack([info[2], info[3]], axis=1)
    chunk_slot = chunk_slot.reshape(-1).astype(i32)
    seg_first = ((ends - seg_pad) // MOE_ROWS).astype(i32)
    seg_count = (seg_pad // MOE_ROWS).astype(i32)

    xb = _sort_dispatch(h1, jnp.stack([lp1, lp2], axis=0), chunk_slot, n_chunks, zero_start, n_rows, tm,
                        sorted_rows)
    yb = _seg_experts(xb, seg_first, seg_count, n_valid, w_gate, w_up, w_down, layer)
    return _sort_combine(h1, yb, jnp.stack([lp1, lp2], axis=1), gates, chunk_slot, n_chunks, ln_w, ln_b,
                         alpha, tm, sorted_rows)


def _rotary_column_order(w_in):
    d = w_in.shape[0]
    nq = RET_HEADS * RET_DK

    def perm(w):
        return w.reshape(d, RET_HEADS, RET_DK // 2, 2).transpose(0, 3, 1, 2).reshape(d, nq)

    return jnp.concatenate([perm(w_in[:, :nq]), perm(w_in[:, nq:2 * nq]), w_in[:, 2 * nq:]], axis=1)


def kernel(x, w_in_even, w_out_even, lru_conv_w, lru_conv_b, lru_gate_w, lru_gate_b, lru_lambda,
           w_in_odd, w_out_odd, hg_lower_bounds, hg_norm_w, ln_w, ln_b, router_w,
           moe_w_gate, moe_w_up, moe_w_down):
    batch, seq_len, d = x.shape
    depth = ln_w.shape[0]
    alpha = (2.0 * depth) ** 0.25
    t = batch * seq_len
    h = x.reshape(t, d)
    router_wt = router_w.T
    half = RET_DK // 2
    inv_freq = ROPE_BASE ** (-jnp.arange(0, RET_DK, 2, dtype=F32) / RET_DK)
    inv_freq = jnp.tile(inv_freq, LANES // half).reshape(1, LANES)
    nret = RET_HEADS * RET_DV
    tm = min(PROJ_ROWS, seq_len)
    tiles_per_seq = seq_len // tm
    cos, sin = _rotary_table(inv_freq, seq_len)
    pos_spec = pl.BlockSpec((tm, LANES), lambda i: (i % tiles_per_seq, 0))

    for layer in range(depth):
        j = layer // 2
        if layer % 2 == 0:
            w_in = _rotary_column_order(w_in_even[j]).astype(BF16)
            n_bf = 2 * RET_HEADS * RET_DK + 2 * nret
            proj_kernel = functools.partial(_proj_ret_kernel, n_bf=n_bf)
            proj_b, proj_f = _project(proj_kernel, "in_proj", h, w_in, [cos, sin], [pos_spec, pos_spec], n_bf,
                                      w_in.shape[1] - n_bf, seq_len)
            ret = _retention(proj_b, batch, seq_len)
            h_fwd, h_bwd = _rglru(proj_f, lru_conv_w[j], lru_conv_b[j], lru_gate_w[j], lru_gate_b[j],
                                  lru_lambda[j], batch, seq_len)
            w_out = w_out_even[j].astype(BF16)
            mixer_inputs, weights = (ret, h_fwd, h_bwd, proj_f), [w_out[:nret], w_out[nret:]]
        else:
            w = w_in_odd[j]
            w_in = jnp.concatenate([w[:, :2 * d], w[:, 4 * d:], w[:, 2 * d:4 * d]], axis=1).astype(BF16)
            proj_kernel = functools.partial(_proj_gla_kernel, layer=layer)
            bounds_spec = pl.BlockSpec(hg_lower_bounds.shape, lambda i: (0, 0))
            proj_b, proj_f = _project(proj_kernel, "in_proj_gla", h, w_in, [hg_lower_bounds], [bounds_spec],
                                      5 * d, 2 * d, seq_len)
            o_fwd, o_bwd = _gla(proj_b, proj_f, batch, seq_len)
            mixer_inputs, weights = (o_fwd, o_bwd, proj_b, hg_norm_w[j]), [w_out_odd[j].astype(BF16)]
        h1, info = _out_project(layer % 2 == 0, mixer_inputs, weights, h, ln_w[layer, 0], ln_b[layer, 0],
                                router_wt, alpha)
        h = _moe(h1, info, moe_w_gate, moe_w_up, moe_w_down, layer, ln_w[layer, 1], ln_b[layer, 1], alpha)
    return h.reshape(batch, seq_len, d)
```

```python
import functools
import math

import jax
import jax.numpy as jnp
from jax import lax
from jax.experimental import pallas as pl
from jax.experimental.pallas import tpu as pltpu

F32 = jnp.float32
BF16 = jnp.bfloat16

RET_HEADS = 4
RET_DK = 64
RET_DV = 128
RET_CHUNK = 128
ROPE_BASE = 10000.0
LRU_BLOCKS = 4
LRU_BW = 128
LRU_CONV = 4
LRU_C = 8.0
HG_HEADS = 8
HG_DK = 128
HG_DV = 128
N_EXPERTS = 16
N_GROUPS = 4
EXPERTS_PER_GROUP = 4
LN_EPS = 1e-5
RMS_EPS = 1e-6

LANES = 128
SUBLANES = 8
PROJ_ROWS = 1024
PROJ_COLS = 512
RET_CHUNKS_PER_STEP = 8
LRU_ROWS = 512
GLA_CHUNK = 32
GLA_ROWS = 256
TOKEN_ROWS = 512
SORT_ROWS = 256
SORT_TILES_PER_STEP = 2
MOE_ROWS = 512
VMEM_LIMIT = 58 * 1024 * 1024

_NT = (((1,), (1,)), ((), ()))
_TN = (((0,), (0,)), ((), ()))


def _params(sem):
    return pltpu.CompilerParams(dimension_semantics=sem, vmem_limit_bytes=VMEM_LIMIT)


def _sigmoid(x):
    return 0.5 * jnp.tanh(0.5 * x) + 0.5


def _silu(x):
    return x * _sigmoid(x)


def _softplus(x):
    return jnp.maximum(x, 0.0) + jnp.log1p(jnp.exp(-jnp.abs(x)))


def _layer_norm(u, w, b):
    mu = jnp.mean(u, axis=-1, keepdims=True)
    d = u - mu
    var = jnp.mean(d * d, axis=-1, keepdims=True)
    return d * lax.rsqrt(var + LN_EPS) * w + b


def _rotary_table_kernel(inv_ref, cos_ref, sin_ref, *, tm):
    row = lax.broadcasted_iota(jnp.int32, (tm, 1), 0) + pl.program_id(0) * tm
    ang = row.astype(F32) * inv_ref[...]
    cos_ref[...] = jnp.cos(ang)
    sin_ref[...] = jnp.sin(ang)


def _rotary_table(inv_freq, seq_len):
    tm = min(PROJ_ROWS, seq_len)
    table = jax.ShapeDtypeStruct((seq_len, LANES), F32)
    return pl.pallas_call(
        functools.partial(_rotary_table_kernel, tm=tm),
        grid=(seq_len // tm,),
        in_specs=[pl.BlockSpec((1, LANES), lambda i: (0, 0))],
        out_specs=[pl.BlockSpec((tm, LANES), lambda i: (i, 0))] * 2,
        out_shape=[table, table],
        compiler_params=_params(("arbitrary",)),
        name="rotary_table",
    )(inv_freq)


def _proj_ret_kernel(x_ref, w_ref, cos_ref, sin_ref, ob_ref, of_ref, *, n_bf):
    xb = x_ref[...].astype(BF16)
    tn = PROJ_COLS
    for j in range(w_ref.shape[1] // tn):
        acc = jnp.dot(xb, w_ref[:, j * tn:(j + 1) * tn], preferred_element_type=F32)
        if j * tn < 4 * LANES:
            cos = cos_ref[...]
            sin = sin_ref[...]
            for base in range(0, tn, 2 * LANES):
                col = j * tn + base
                scale = RET_DK ** -0.5 if col < 2 * LANES else 1.0
                x1, x2 = acc[:, base:base + LANES], acc[:, base + LANES:base + 2 * LANES]
                ob_ref[:, col:col + LANES] = ((x1 * cos - x2 * sin) * scale).astype(BF16)
                ob_ref[:, col + LANES:col + 2 * LANES] = ((x1 * sin + x2 * cos) * scale).astype(BF16)
        elif (j + 1) * tn <= n_bf:
            ob_ref[:, j * tn:(j + 1) * tn] = acc.astype(BF16)
        else:
            of_ref[:, j * tn - n_bf:(j + 1) * tn - n_bf] = acc


def _gla_lower_bound(p, layer):
    e = jnp.exp(p - jnp.max(p, axis=0, keepdims=True))
    sm = e / jnp.sum(e, axis=0, keepdims=True)
    lb = jnp.zeros((1, p.shape[1]), F32)
    for r in range(1, layer + 1):
        lb = lb + sm[r:r + 1, :]
    return lb


def _proj_gla_kernel(x_ref, w_ref, lbp_ref, ob_ref, of_ref, *, layer):
    xb = x_ref[...].astype(BF16)
    d = w_ref.shape[1] // 5
    tn = PROJ_COLS
    lb = _gla_lower_bound(lbp_ref[...], layer)
    mean = 0.5 * (1.0 + lb)
    spread = 0.5 * (1.0 - lb)
    q_scale = 0.5 * HG_DK ** -0.5
    for j in range(5 * d // tn):
        col = j * tn
        acc = jnp.dot(xb, w_ref[:, col:col + tn], preferred_element_type=F32)
        if col < d:
            ob_ref[:, col:col + tn] = (acc * (q_scale * jnp.tanh(0.5 * acc) + q_scale)).astype(BF16)
        elif col < 3 * d:
            ob_ref[:, col:col + tn] = acc.astype(BF16)
        else:
            zc = col - 3 * d
            swing = spread[:, zc % d:zc % d + tn] * jnp.tanh(0.5 * acc)
            of_ref[:, zc:zc + tn] = jnp.log2(mean[:, zc % d:zc % d + tn] + swing)
            ob_ref[:, col:col + tn] = (spread[:, zc % d:zc % d + tn] - swing).astype(BF16)


def _project(kernel_fn, name, x, w_bf16, sides, side_specs, n_bf, n_f32, seq_len):
    t, k = x.shape
    n = w_bf16.shape[1]
    tm = min(PROJ_ROWS, seq_len)
    return pl.pallas_call(
        kernel_fn,
        grid=(t // tm,),
        in_specs=[pl.BlockSpec((tm, k), lambda i: (i, 0)),
                  pl.BlockSpec((k, n), lambda i: (0, 0), pipeline_mode=pl.Buffered(1))] + side_specs,
        out_specs=[pl.BlockSpec((tm, n_bf), lambda i: (i, 0)),
                   pl.BlockSpec((tm, n_f32), lambda i: (i, 0))],
        out_shape=[jax.ShapeDtypeStruct((t, n_bf), BF16),
                   jax.ShapeDtypeStruct((t, n_f32), F32)],
        compiler_params=_params(("arbitrary",)),
        name=name,
    )(x, w_bf16, *sides)


def _ret_log_gamma(head):
    out = jnp.full(head.shape, math.log1p(-(2.0 ** -5.0)), F32)
    for h in range(1, RET_HEADS):
        out = jnp.where(head == h, math.log1p(-(2.0 ** (-5.0 - h))), out)
    return out


def _ret_lane_head():
    lane = lax.broadcasted_iota(jnp.int32, (1, 2 * LANES), 1)
    return (lane % LANES) // (RET_DK // 2)


def _ret_state_mask():
    shape = (2 * LANES, RET_HEADS * RET_DV)
    row_head = (lax.broadcasted_iota(jnp.int32, shape, 0) % LANES) // (RET_DK // 2)
    col_head = lax.broadcasted_iota(jnp.int32, shape, 1) // RET_DV
    return row_head == col_head


def _ret_chunk_decay():
    row = lax.broadcasted_iota(jnp.int32, (2 * LANES, 1), 0)
    return jnp.exp(_ret_log_gamma((row % LANES) // (RET_DK // 2)) * float(RET_CHUNK))


def _ret_bstate_kernel(k_ref, v_ref, sb_ref, s_ref, *, cps):
    c = RET_CHUNK

    @pl.when(pl.program_id(1) == 0)
    def _():
        s_ref[...] = jnp.zeros_like(s_ref)

    lg = _ret_log_gamma(_ret_lane_head())
    idx = lax.broadcasted_iota(jnp.int32, (c, 1), 0).astype(F32)
    k_decay = jnp.exp(lg * idx)
    chunk_decay = _ret_chunk_decay()
    mask = _ret_state_mask()
    for cc in reversed(range(cps)):
        rows = slice(cc * c, (cc + 1) * c)
        sb_ref[0, cc] = s_ref[...].astype(BF16)
        kb = (k_ref[rows, :] * k_decay).astype(BF16)
        upd = lax.dot_general(kb, v_ref[rows, :], _TN, preferred_element_type=F32)
        s_ref[...] = s_ref[...] * chunk_decay + jnp.where(mask, upd, 0.0)


def _ret_out_kernel(q_ref, k_ref, v_ref, g_ref, sb_ref, o_ref, s_ref, *, cps):
    c = RET_CHUNK

    @pl.when(pl.program_id(1) == 0)
    def _():
        s_ref[...] = jnp.zeros_like(s_ref)

    lane_head = _ret_lane_head()
    lg = _ret_log_gamma(lane_head)
    idx = lax.broadcasted_iota(jnp.int32, (c, 1), 0).astype(F32)
    q_decay_f = jnp.exp(lg * (idx + 1.0))
    q_decay_b = jnp.exp(lg * (float(c) - idx))
    k_decay = jnp.exp(lg * (float(c) - 1.0 - idx))
    chunk_decay = _ret_chunk_decay()
    mask = _ret_state_mask()
    ii =lax.broadcasted_iota(jnp.int32, (c, c), 0)
    jj = lax.broadcasted_iota(jnp.int32, (c, c), 1)
    dist = jnp.abs(ii - jj).astype(F32)
    intra_decay = [jnp.exp(math.log1p(-(2.0 ** (-5.0 - h))) * dist) for h in range(RET_HEADS)]

    for cc in range(cps):
        rows = slice(cc * c, (cc + 1) * c)
        q = q_ref[rows, :]
        k = k_ref[rows, :]
        v = v_ref[rows, :]
        qf = (q * q_decay_f).astype(BF16)
        qb = (q * q_decay_b).astype(BF16)
        cross = (jnp.dot(qf, s_ref[...].astype(BF16), preferred_element_type=F32)
                 + jnp.dot(qb, sb_ref[0, cc], preferred_element_type=F32))
        q_heads = jnp.concatenate([jnp.where(lane_head == h, q, jnp.zeros_like(q)) for h in range(RET_HEADS)],
                                  axis=0)
        scores = lax.dot_general(q_heads, k, _NT, preferred_element_type=F32)
        for h in range(RET_HEADS):
            s = scores[h * c:(h + 1) * c, :] * intra_decay[h]
            cols = slice(h * RET_DV, (h + 1) * RET_DV)
            o = jnp.dot(s.astype(BF16), v[:, cols], preferred_element_type=F32) + cross[:, cols]
            mu = jnp.mean(o, axis=-1, keepdims=True)
            d = o - mu
            var = jnp.mean(d * d, axis=-1, keepdims=True)
            gate = _silu(g_ref[rows, cols].astype(F32))
            o_ref[rows, cols] = (gate * (d * lax.rsqrt(var + LN_EPS))).astype(BF16)
        kf = (k * k_decay).astype(BF16)
        upd = lax.dot_general(kf, v, _TN, preferred_element_type=F32)
        s_ref[...] = s_ref[...] * chunk_decay + jnp.where(mask, upd, 0.0)


def _retention(proj, batch, seq_len):
    t = proj.shape[0]
    c = RET_CHUNK
    cps = min(RET_CHUNKS_PER_STEP, seq_len // c)
    rows = cps * c
    ns = seq_len // rows
    dv = RET_HEADS * RET_DV
    state_shape = (2 * LANES, dv)
    rev = lambda b, n: b * ns + (ns - 1 - n)
    fwd = lambda b, n: b * ns + n
    sb = pl.pallas_call(
        functools.partial(_ret_bstate_kernel, cps=cps),
        grid=(batch, ns),
        in_specs=[
            pl.BlockSpec((rows, 2 * LANES), lambda b, n: (rev(b, n), 1)),
            pl.BlockSpec((rows, dv), lambda b, n: (rev(b, n), 1)),
        ],
        out_specs=pl.BlockSpec((1, cps) + state_shape, lambda b, n: (b, ns - 1 - n, 0, 0)),
        out_shape=jax.ShapeDtypeStruct((batch, ns * cps) + state_shape, BF16),
        scratch_shapes=[pltpu.VMEM(state_shape, F32)],
        compiler_params=_params(("arbitrary", "arbitrary")),
        name="ret_bstate",
    )(proj, proj)
    return pl.pallas_call(
        functools.partial(_ret_out_kernel, cps=cps),
        grid=(batch, ns),
        in_specs=[
            pl.BlockSpec((rows, 2 * LANES), lambda b, n: (fwd(b, n), 0)),
            pl.BlockSpec((rows, 2 * LANES), lambda b, n: (fwd(b, n), 1)),
            pl.BlockSpec((rows, dv), lambda b, n: (fwd(b, n), 1)),
            pl.BlockSpec((rows, dv), lambda b, n: (fwd(b, n), 2)),
            pl.BlockSpec((1, cps) + state_shape, lambda b, n: (b, n, 0, 0)),
        ],
        out_specs=pl.BlockSpec((rows, dv), lambda b, n: (fwd(b, n), 0)),
        out_shape=jax.ShapeDtypeStruct((t, dv), BF16),
        scratch_shapes=[pltpu.VMEM(state_shape, F32)],
        compiler_params=_params(("arbitrary", "arbitrary")),
        name="ret_out",
    )(proj, proj, proj, proj, sb)


def _lru_kernel(xfp_ref, xf_ref, xfn_ref, xbp_ref, xb_ref, xbn_ref, cw_ref, cb_ref, gw_ref, gb_ref,
                lam_ref, hf_ref, hb_ref, xx_ref, a_ref, b_ref, h_ref, *, nt, ts, batch):
    i = pl.program_id(0)
    halo = SUBLANES
    lo = LRU_CONV // 2

    @pl.when(i == 0)
    def _():
        h_ref[...] = jnp.zeros_like(h_ref)

    def prepare(xp_ref, x_ref, xn_ref, tile, z, slot):
        for b in range(batch):
            xx_ref[0:halo, :] = jnp.where(tile == 0, 0.0, xp_ref[b])
            xx_ref[halo:halo + ts, :] = x_ref[b]
            xx_ref[halo + ts:2 * halo + ts, :] = jnp.where(tile == nt - 1, 0.0, xn_ref[b])
            xx = xx_ref[...]
            xc = cb_ref[...]
            for tap in range(LRU_CONV):
                shift = (lo - tap) % (ts + 2 * halo)
                moved = xx if shift == 0 else pltpu.roll(xx, shift, axis=0)
                xc = xc + cw_ref[tap:tap + 1, :] * moved[halo:halo + ts, :]
            for n in range(LRU_BLOCKS):
                cols = slice(n * LRU_BW, (n + 1) * LRU_BW)
                xn = xc[:, cols]
                g = (jnp.dot(xn.astype(BF16), gw_ref[z, n], preferred_element_type=F32)
                     + gb_ref[z, n:n + 1, :])
                r = _sigmoid(g[:, :LRU_BW])
                ig = _sigmoid(g[:, LRU_BW:])
                a = jnp.exp((-LRU_C) * r * _softplus(-lam_ref[z, :, cols]))
                a_ref[slot + b, :, cols] = a
                b_ref[slot + b, :, cols] = jnp.sqrt(1.0 - a * a) * (ig * xn)

    prepare(xfp_ref, xf_ref, xfn_ref, i, 0, 0)
    prepare(xbp_ref, xb_ref, xbn_ref, nt - 1 - i, 1, batch)

    def step(s, hs):
        out = []
        for k in range(2 * batch):
            row = s if k < batch else ts - 1 - s
            h = a_ref[k, pl.ds(row, 1), :] * hs[k] + b_ref[k, pl.ds(row, 1), :]
            if k < batch:
                hf_ref[k, pl.ds(row, 1), :] = h
            else:
                hb_ref[k - batch, pl.ds(row, 1), :] = h
            out.append(h)
        return tuple(out)

    hs = lax.fori_loop(0, ts, step, tuple(h_ref[k] for k in range(2 * batch)), unroll=8)
    for k in range(2 * batch):
        h_ref[k] = hs[k]


def _rglru(proj_f32, conv_w, conv_b, gate_w, gate_b, lam, batch, seq_len):
    w = LRU_BLOCKS * LRU_BW
    ts = min(LRU_ROWS, seq_len)
    nt = seq_len // ts
    rows8 = ts // SUBLANES
    last8 = seq_len // SUBLANES - 1
    x3 = proj_f32.reshape(batch, seq_len, proj_f32.shape[1])
    gw = jnp.concatenate([gate_w[:, 0], gate_w[:, 1]], axis=-1).astype(BF16)
    gb = jnp.concatenate([gate_b[:, 0], gate_b[:, 1]], axis=-1)
    bwd = lambda i: nt - 1 - i

    def tile_specs(tile):
        return [
            pl.BlockSpec((batch, SUBLANES, w), lambda i: (0, jnp.maximum(tile(i) * rows8 - 1, 0), 0)),
            pl.BlockSpec((batch, ts, w), lambda i: (0, tile(i), 0)),
            pl.BlockSpec((batch, SUBLANES, w), lambda i: (0, jnp.minimum((tile(i) + 1) * rows8, last8), 0)),
        ]

    full = lambda a: pl.BlockSpec(a.shape, lambda i: (0,) * a.ndim)
    cb = conv_b.reshape(1, w)
    lam3 = lam.reshape(2, 1, w)
    state = jax.ShapeDtypeStruct((batch, seq_len, w), F32)
    h_fwd, h_bwd = pl.pallas_call(
        functools.partial(_lru_kernel, nt=nt, ts=ts, batch=batch),
        grid=(nt,),
        in_specs=tile_specs(lambda i: i) + tile_specs(bwd) + [full(conv_w), full(cb), full(gw), full(gb),
                                                               full(lam3)],
        out_specs=[pl.BlockSpec((batch, ts, w), lambda i: (0, i, 0)),
                   pl.BlockSpec((batch, ts, w), lambda i: (0, bwd(i), 0))],
        out_shape=[state, state],
        scratch_shapes=[pltpu.VMEM((ts + 2 * SUBLANES, w), F32), pltpu.VMEM((2 * batch, ts, w), F32),
                        pltpu.VMEM((2 * batch, ts, w), F32), pltpu.VMEM((2 * batch, 1, w), F32)],
        compiler_params=_params(("arbitrary",)),
        name="lru_scan",
    )(x3, x3, x3, x3, x3, x3, conv_w, cb, gw, gb, lam3)
    return h_fwd.reshape(batch * seq_len, w), h_bwd.reshape(batch * seq_len, w)


def _gla_kernel(qf_ref, vf_ref, kf_ref, lf_ref, qb_ref, vb_ref, kb_ref, lb_ref, of_ref, ob_ref, s_ref,
                *, ts, batch):
    c = GLA_CHUNK
    nchunks = ts // c

    @pl.when(pl.program_id(0) == 0)
    def _():
        s_ref[...] = jnp.zeros_like(s_ref)

    head_cols = [slice(h * HG_DK, (h + 1) * HG_DK) for h in range(HG_HEADS)]
    hc = HG_HEADS * c
    ii = lax.broadcasted_iota(jnp.int32, (c, c), 0)
    jj = lax.broadcasted_iota(jnp.int32, (c, c), 1)
    si = lax.broadcasted_iota(jnp.int32, (hc, hc), 0)
    sj = lax.broadcasted_iota(jnp.int32, (hc, hc), 1)
    same_head = si // c == sj // c

    def stack(a):
        return jnp.concatenate([a[:, cols] for cols in head_cols], axis=0)

    tri = {False: (jj <= ii).astype(BF16), True: (jj >= ii).astype(BF16)}
    tri2 = {r: jnp.concatenate([m, m], axis=1) for r, m in tri.items()}
    zero_block = jnp.zeros((c, HG_DK), BF16)
    zero_state = jnp.zeros((HG_DK, HG_DV), BF16)
    keep = {False: jnp.logical_and(same_head, sj <= si), True: jnp.logical_and(same_head, sj >= si)}
    chains = ([(qf_ref, vf_ref, kf_ref, lf_ref, of_ref, b, False) for b in range(batch)]
              + [(qb_ref, vb_ref, kb_ref, lb_ref, ob_ref, b, True) for b in range(batch)])

    def gates(cc):
        work = []
        for chain, (q_ref, v_ref, k_ref, l_ref, o_ref, b, reverse) in enumerate(chains):
            rows = pl.ds(((nchunks - 1 - cc) if reverse else cc) * c, c)
            log_f = l_ref[b, rows, :]
            f_hi = log_f.astype(BF16)
            f_lo = (log_f - f_hi.astype(F32)).astype(BF16)
            work.append(dict(rows=rows, reverse=reverse, o_ref=o_ref, b=b, chain=chain,
                             key=k_ref[b, rows, :].astype(F32), qs=q_ref[b, rows, :],
                             v=v_ref[b, rows, :], split=(f_hi, f_lo)))
        return work

    def cumulate(work):
        for w in work:
            f_hi, f_lo = w["split"]
            w["bcum"] = jnp.dot(tri2[w["reverse"]], jnp.concatenate([f_lo, f_hi], axis=0),
                                preferred_element_type=F32)

    half = c // 2
    row = lax.broadcasted_iota(jnp.int32, (c, 1), 0)
    first_half = row < half

    def decays(work):
        for w in work:
            bcum = w["bcum"]
            rev = w["reverse"]
            end = 0 if rev else c - 1
            b_end = bcum[end:end + 1, :]
            ref = jnp.where(first_half, bcum[half // 2:half // 2 + 1, :], bcum[half + half // 2:half + half // 2 + 1, :])
            qe = w["qs"] * jnp.exp2(bcum - ref).astype(BF16)
            ke = w["key"] * jnp.exp2(ref - bcum)
            boundary = bcum[half:half + 1, :] if rev else bcum[half - 1:half, :]
            cross = jnp.exp2(-jnp.abs(bcum - boundary))
            queries = first_half if rev else jnp.logical_not(first_half)
            zero = jnp.zeros_like(qe)
            w["q3"] = [jnp.where(first_half, qe, zero), jnp.where(first_half, zero, qe),
                       jnp.where(queries, w["qs"] * cross.astype(BF16), zero)]
            w["k3"] = [jnp.where(first_half, ke, 0.0), jnp.where(first_half, 0.0, ke),
                       jnp.where(queries, 0.0, w["key"] * cross)]
            w["qd"] = w["qs"] * jnp.exp2(bcum).astype(BF16)
            w["kd"] = (w["key"] * jnp.exp2(b_end - bcum)).astype(BF16)
            w["decay"] = jnp.exp2(b_end)

    def stack3(parts):
        return jnp.concatenate([jnp.concatenate([p[:, cols] for p in parts], axis=1) for cols in head_cols], axis=0)

    def scores(work):
        for w in work:
            w["att"] = jnp.dot(stack3(w["q3"]), stack3(w["k3"]).T.astype(BF16), preferred_element_type=F32)

    def intra(work):
        for w in work:
            att = jnp.where(keep[w["reverse"]], w["att"], 0.0).astype(BF16)
            w["intra"] = jnp.dot(att, stack(w["v"]), preferred_element_type=F32)

    def inter(work):
        for w in work:
            w["st"] = [s_ref[w["chain"], h] for h in range(HG_HEADS)]
            out = []
            for h in range(0, HG_HEADS, 2):
                sa = w["st"][h].T.astype(BF16)
                sb = w["st"][h + 1].T.astype(BF16)
                rhs = jnp.concatenate([jnp.concatenate([sa, zero_state], axis=1),
                                       jnp.concatenate([zero_state, sb], axis=1)], axis=0)
                pair = jnp.dot(w["qd"][:, h * HG_DK:(h + 2) * HG_DK], rhs, preferred_element_type=F32)
                out += [pair[:, :HG_DV], pair[:, HG_DV:]]
            w["inter"] = out

    def update(work):
        for w in work:
            upd = []
            for h in range(0, HG_HEADS, 2):
                ca, cb = head_cols[h], head_cols[h + 1]
                lhs = jnp.concatenate([w["v"][:, ca], w["v"][:, cb]], axis=0)
                rhs = jnp.concatenate([jnp.concatenate([w["kd"][:, ca], zero_block], axis=1),
                                       jnp.concatenate([zero_block, w["kd"][:, cb]], axis=1)], axis=0)
                pair = lax.dot_general(lhs, rhs, _TN, preferred_element_type=F32)
                upd += [pair[:, :HG_DK], pair[:, HG_DK:]]
            w["upd"] = upd

    def finish(work):
        for w in work:
            k = w["chain"]
            for h, cols in enumerate(head_cols):
                w["o_ref"][w["b"], w["rows"], cols] = w["intra"][h * c:(h + 1) * c, :] + w["inter"][h]
                s_ref[k, h] = w["st"][h] * w["decay"][:, cols] + w["upd"][h]

    cur = gates(0)
    cumulate(cur)
    decays(cur)
    for cc in range(nchunks):
        more = cc + 1 < nchunks
        scores(cur)
        nxt = gates(cc + 1) if more else None
        if more:
            cumulate(nxt)
        intra(cur)
        inter(cur)
        update(cur)
        if more:
            decays(nxt)
        finish(cur)
        cur = nxt


def _gla(proj_bf16, proj_f32, batch, seq_len):
    d = HG_HEADS * HG_DK
    ts = min(GLA_ROWS, seq_len)
    nt = seq_len // ts
    pb = proj_bf16.reshape(batch, seq_len, proj_bf16.shape[1])
    pf = proj_f32.reshape(batch, seq_len, proj_f32.shape[1])
    bwd = lambda i: nt - 1 - i
    blk = lambda tile, col: pl.BlockSpec((batch, ts, d), lambda i: (0, tile(i), col))
    fwd = lambda i: i
    out = jax.ShapeDtypeStruct((batch, seq_len, d), F32)
    o_fwd, o_bwd = pl.pallas_call(
        functools.partial(_gla_kernel, ts=ts, batch=batch),
        grid=(nt,),
        in_specs=[blk(fwd, 0), blk(fwd, 1), blk(fwd, 3), blk(fwd, 0),
                  blk(bwd, 0), blk(bwd, 1), blk(bwd, 4), blk(bwd, 1)],
        out_specs=[blk(fwd, 0), blk(bwd, 0)],
        out_shape=[out, out],
        scratch_shapes=[pltpu.VMEM((2 * batch, HG_HEADS, HG_DV, HG_DK), F32)],
        compiler_params=_params(("arbitrary",)),
        name="gla_scan",
    )(pb, pb, pb, pf, pb, pb, pb, pf)
    return o_fwd.reshape(batch * seq_len, d), o_bwd.reshape(batch * seq_len, d)


def _top2(p):
    v1 = jnp.maximum(jnp.maximum(p[0], p[1]), jnp.maximum(p[2], p[3]))
    i1 = jnp.where(p[0] == v1, 0, jnp.where(p[1] == v1, 1, jnp.where(p[2] == v1, 2, 3)))
    q = [jnp.where(i1 == k, -1.0, p[k]) for k in range(4)]
    v2 = jnp.maximum(jnp.maximum(q[0], q[1]), jnp.maximum(q[2], q[3]))
    i2 = jnp.where(q[0] == v2, 0, jnp.where(q[1] == v2, 1, jnp.where(q[2] == v2, 2, 3)))
    return v1, i1, v2, i2


def _outproj_kernel(*refs, even, alpha, tm):
    if even:
        ret_ref, hf_ref, hb_ref, gr_ref, w0_ref, w1_ref = refs[:6]
        rest = refs[6:]
        lru = ((hf_ref[...] + hb_ref[...]) * jax.nn.gelu(gr_ref[...])).astype(BF16)
        y = (jnp.dot(ret_ref[...], w0_ref[...], preferred_element_type=F32)
             + jnp.dot(lru, w1_ref[...], preferred_element_type=F32))
    else:
        of_ref, ob_ref, g_ref, nw_ref, w0_ref = refs[:5]
        rest = refs[5:]
        o = of_ref[...] + ob_ref[...]
        ms = jnp.mean(o * o, axis=-1, keepdims=True)
        mix = o * lax.rsqrt(ms + RMS_EPS) * nw_ref[...] * _silu(g_ref[...].astype(F32))
        y = jnp.dot(mix.astype(BF16), w0_ref[...], preferred_element_type=F32)
    h_ref, lnw_ref, lnb_ref, rw_ref, o_ref, info_ref, carry_ref, before_ref = rest
    i = pl.program_id(0)

    @pl.when(i == 0)
    def _():
        carry_ref[...] = jnp.zeros_like(carry_ref)
        tt = lax.broadcasted_iota(jnp.int32, (tm, tm), 0)
        uu = lax.broadcasted_iota(jnp.int32, (tm, tm), 1)
        before_ref[...] = (tt < uu).astype(BF16)

    h1 = _layer_norm(alpha * h_ref[...] + y, lnw_ref[...], lnb_ref[...])
    o_ref[...] = h1

    h_hi = h1.astype(BF16)
    h_lo = (h1 - h_hi.astype(F32)).astype(BF16)
    rw = rw_ref[...]
    r_hi = rw.astype(BF16)
    r_lo = (rw - r_hi.astype(F32)).astype(BF16)
    both = lax.dot_general(jnp.concatenate([r_hi, r_lo], axis=0), h_hi, _NT, preferred_element_type=F32)
    logits = (both[N_EXPERTS:, :] + lax.dot_general(r_hi, h_lo, _NT, preferred_element_type=F32)
              + both[:N_EXPERTS, :])
    ex = jnp.exp(logits - jnp.max(logits, axis=0, keepdims=True))
    probs = ex / jnp.sum(ex, axis=0, keepdims=True)
    best = None
    for g in range(N_GROUPS):
        rows = [probs[g * EXPERTS_PER_GROUP + k:g * EXPERTS_PER_GROUP + k + 1, :]
                for k in range(EXPERTS_PER_GROUP)]
        v1, i1, v2, i2 = _top2(rows)
        cand = (v1 + v2, v1, i1 + g * EXPERTS_PER_GROUP, v2, i2 + g * EXPERTS_PER_GROUP)
        if best is None:
            best = cand
        else:
            take = cand[0] > best[0]
            best = tuple(jnp.where(take, cn, bs) for cn, bs in zip(cand, best))
    _, v1, e1, v2, e2 = best
    denom = v1 + v2
    g1 = v1 / denom
    g2 = v2 / denom

    eid = lax.broadcasted_iota(jnp.int32, (N_EXPERTS, tm), 0)
    oh1 = (eid == e1).astype(F32)
    oh2 = (eid == e2).astype(F32)
    oh = oh1 + oh2
    base = carry_ref[:, 0:1] + jnp.dot(oh.astype(BF16), before_ref[...], preferred_element_type=F32)
    rank1 = jnp.sum(oh1 * base, axis=0, keepdims=True)
    rank2 = jnp.sum(oh2 * base, axis=0, keepdims=True)
    carry_ref[...] = carry_ref[...] + jnp.sum(oh, axis=1, keepdims=True)
    zero = jnp.zeros_like(g1)
    info_ref[...] = jnp.concatenate(
        [e1.astype(F32), e2.astype(F32), g1, g2, rank1, rank2, zero, zero], axis=0)


def _out_project(even, mixer_inputs, weights_bf16, h, ln_w, ln_b, router_wt, alpha):
    t, d = h.shape
    tm = TOKEN_ROWS
    row = lambda i: (i, 0)
    const = lambda i: (0, 0)
    if even:
        ret, h_fwd, h_bwd, proj_f32 = mixer_inputs
        w = h_fwd.shape[1]
        in_specs = [pl.BlockSpec((tm, ret.shape[1]), row), pl.BlockSpec((tm, w), row),
                    pl.BlockSpec((tm, w), row), pl.BlockSpec((tm, w), lambda i: (i, 1))]
        args = [ret, h_fwd, h_bwd, proj_f32]
    else:
        o_fwd, o_bwd, proj_bf16, norm_w = mixer_inputs
        in_specs = [pl.BlockSpec((tm, d), row), pl.BlockSpec((tm, d), row),
                    pl.BlockSpec((tm, d), lambda i: (i, 2)), pl.BlockSpec((1, d), const)]
        args = [o_fwd, o_bwd, proj_bf16, norm_w.reshape(1, d)]
    in_specs += [pl.BlockSpec(wm.shape, const) for wm in weights_bf16]
    in_specs += [pl.BlockSpec((tm, d), row), pl.BlockSpec((1, d), const), pl.BlockSpec((1, d), const),
                 pl.BlockSpec((N_EXPERTS, d), const)]
    args += list(weights_bf16) + [h, ln_w.reshape(1, d), ln_b.reshape(1, d), router_wt]
    return pl.pallas_call(
        functools.partial(_outproj_kernel, even=even, alpha=alpha, tm=tm),
        grid=(t // tm,),
        in_specs=in_specs,
        out_specs=[pl.BlockSpec((tm, d), row),
                   pl.BlockSpec((SUBLANES, tm), lambda i: (0, i))],
        out_shape=[jax.ShapeDtypeStruct((t, d), F32),
                   jax.ShapeDtypeStruct((SUBLANES, t), F32)],
        scratch_shapes=[pltpu.VMEM((N_EXPERTS, LANES), F32), pltpu.VMEM((tm, tm), BF16)],
        compiler_params=_params(("arbitrary",)),
        name="out_proj_router",
    )(*args)


def _pack_halves(x):
    n = x.shape[1] // 2
    hi = pltpu.bitcast(x[:, :n], jnp.uint32)
    lo = pltpu.bitcast(x[:, n:], jnp.uint32)
    return hi | (lo >> 16)


def _unpack_halves(p):
    hi = pltpu.bitcast(p & jnp.uint32(0xFFFF0000), F32)
    lo = pltpu.bitcast(p << 16, F32)
    return jnp.concatenate([hi.astype(BF16), lo.astype(BF16)], axis=1)


def _seg_expert_kernel(first_ref, count_ref, tail_ref, x_ref, wg_ref, wu_ref, wd_ref, y_ref,
                       xbuf, ybuf, wgb_ref, wub_ref, wdb_ref, semx, semy):
    e = pl.program_id(0)
    wgb_ref[...] = wg_ref[...].astype(BF16)
    wub_ref[...] = wu_ref[...].astype(BF16)
    wdb_ref[...] = wd_ref[...].astype(BF16)
    first = first_ref[e]
    n_used = tail_ref[0]

    def rows(g):
        return pl.ds(pl.multiple_of(g * MOE_ROWS, MOE_ROWS), MOE_ROWS)

    def fetch(g):
        return pltpu.make_async_copy(x_ref.at[rows(g)], xbuf.at[g % 2], semx.at[g % 2])

    def put(g):
        return pltpu.make_async_copy(ybuf.at[g % 2], y_ref.at[rows(g)], semy.at[g % 2])

    @pl.when(e == 0)
    def _():
        fetch(0).start()

    def body(g, carry):
        @pl.when(g + 1 < n_used)
        def _():
            fetch(g + 1).start()

        fetch(g).wait()

        @pl.when(g >= 2)
        def _():
            put(g - 2).wait()

        x = _unpack_halves(xbuf[g % 2])
        gate = jnp.dot(x, wgb_ref[...], preferred_element_type=F32)
        up = jnp.dot(x, wub_ref[...], preferred_element_type=F32)
        hid = (_silu(gate) * up).astype(BF16)
        y = jnp.dot(hid, wdb_ref[...], preferred_element_type=F32)
        ybuf[g % 2] = _pack_halves(y.astype(BF16).astype(F32))
        put(g).start()
        return carry

    lax.fori_loop(first, first + count_ref[e], body, 0)

    @pl.when(e == pl.num_programs(0) - 1)
    def _():
        @pl.when(n_used >= 2)
        def _():
            put(n_used - 2).wait()

        put(n_used - 1).wait()
        ybuf[0] = jnp.zeros_like(ybuf[0])

        def clear(b, carry):
            cp = pltpu.make_async_copy(
                ybuf.at[0], y_ref.at[pl.ds(pl.multiple_of(b * MOE_ROWS, MOE_ROWS), MOE_ROWS)], semy.at[0])
            cp.start()
            cp.wait()
            return carry

        lax.fori_loop(tail_ref[0], y_ref.shape[0] // MOE_ROWS, clear, 0)


def _seg_experts(xb, seg_first, seg_count, n_valid, w_gate, w_up, w_down, layer):
    p, half = xb.shape
    d = 2 * half
    de = w_gate.shape[3]
    weight = lambda shape: pl.BlockSpec((None, None) + shape, lambda e, a, b, c: (layer, e, 0, 0))
    return pl.pallas_call(
        _seg_expert_kernel,
        grid_spec=pltpu.PrefetchScalarGridSpec(
            num_scalar_prefetch=3,
            grid=(N_EXPERTS,),
            in_specs=[pl.BlockSpec(memory_space=pl.ANY), weight((d, de)), weight((d, de)), weight((de, d))],
            out_specs=pl.BlockSpec(memory_space=pl.ANY),
            scratch_shapes=[pltpu.VMEM((2, MOE_ROWS, half), jnp.uint32),
                            pltpu.VMEM((2, MOE_ROWS, half), jnp.uint32),
                            pltpu.VMEM((d, de), BF16), pltpu.VMEM((d, de), BF16), pltpu.VMEM((de, d), BF16),
                            pltpu.SemaphoreType.DMA((2,)), pltpu.SemaphoreType.DMA((2,))],
        ),
        out_shape=jax.ShapeDtypeStruct((p, half), jnp.uint32),
        compiler_params=_params(("arbitrary",)),
        name="moe_experts",
    )(seg_first, seg_count, n_valid, xb, w_gate, w_up, w_down)


def _clear_padding_blocks(zs_ref, xb_ref, zero_ref, sem):
    zero_ref[...] = jnp.zeros_like(zero_ref)

    def clear(row):
        start = pl.multiple_of(row, MOE_ROWS)
        return pltpu.make_async_copy(zero_ref, xb_ref.at[pl.ds(start, MOE_ROWS)], sem)

    for e in range(N_EXPERTS):
        clear(zs_ref[e]).start()
    for e in range(N_EXPERTS):
        clear(zs_ref[e]).wait()

    def clear_tail(b, carry):
        clear(b * MOE_ROWS).start()
        clear(b * MOE_ROWS).wait()
        return carry

    lax.fori_loop(zs_ref[N_EXPERTS], xb_ref.shape[0] // MOE_ROWS, clear_tail, 0)


def _run_copy(src_ref, src_row, dst_ref, dst_row, sem):
    src = pl.multiple_of(src_row, SUBLANES)
    dst = pl.multiple_of(dst_row, SUBLANES)
    return pltpu.make_async_copy(src_ref.at[pl.ds(src, SUBLANES)], dst_ref.at[pl.ds(dst, SUBLANES)], sem)


def _start_runs(count, make_copy):
    def pair(p, carry):
        make_copy(2 * p).start()
        make_copy(2 * p + 1).start(priority=1)
        return carry

    lax.fori_loop(0, count // 2, pair, 0)

    @pl.when(count % 2 == 1)
    def _():
        make_copy(count - 1).start()


def _sort_dispatch_kernel(dst_ref, nch_ref, zs_ref, h_ref, lp_ref, xb_ref, xs_ref, zero_ref, sem, *, tm, nck):
    i = pl.program_id(0)

    @pl.when(i == 0)
    def _():
        _clear_padding_blocks(zs_ref, xb_ref, zero_ref, sem.at[0])

    sub = SORT_TILES_PER_STEP
    last = pl.num_programs(0) - 1

    def wait_runs(buf, count):
        def wait(j, carry):
            _run_copy(xs_ref.at[buf], 0, xb_ref, 0, sem.at[buf]).wait()
            return carry

        lax.fori_loop(0, count, wait, 0)

    pos = lax.broadcasted_iota(jnp.int32, (xs_ref.shape[1], tm), 0)
    for s in range(sub):
        tile = i * sub + s
        buf = (i % 2) * sub + s

        @pl.when(i >= 2)
        def _():
            wait_runs(buf, nch_ref[jnp.maximum(tile - 2 * sub, 0)])

        lp = lp_ref[:, s * tm:(s + 1) * tm]
        perm = jnp.logical_or(pos == lp[0:1, :], pos == lp[1:2, :]).astype(BF16)
        rows = h_ref[s * tm:(s + 1) * tm, :].astype(BF16)
        xs_ref[buf] = _pack_halves(jnp.dot(perm, rows, preferred_element_type=F32))

        _start_runs(nch_ref[tile], lambda j: _run_copy(xs_ref.at[buf], j * SUBLANES, xb_ref,
                                                       dst_ref[tile * nck + j], sem.at[buf]))

    @pl.when(i == last)
    def _():
        for s in range(sub):
            @pl.when(i >= 1)
            def _():
                wait_runs((1 - i % 2) * sub + s, nch_ref[jnp.maximum((i - 1) * sub + s, 0)])

            wait_runs((i % 2) * sub + s, nch_ref[i * sub + s])


def _sort_dispatch(h, lp_rows, chunk_dst, n_chunks, zero_start, n_rows, tm, sorted_rows):
    t, d = h.shape
    nck = sorted_rows // SUBLANES
    sub = SORT_TILES_PER_STEP
    return pl.pallas_call(
        functools.partial(_sort_dispatch_kernel, tm=tm, nck=nck),
        grid_spec=pltpu.PrefetchScalarGridSpec(
            num_scalar_prefetch=3,
            grid=(t // (sub * tm),),
            in_specs=[pl.BlockSpec((sub * tm, d), lambda i, a, b, c: (i, 0)),
                      pl.BlockSpec((2, sub * tm), lambda i, a, b, c: (0, i))],
            out_specs=pl.BlockSpec(memory_space=pl.ANY),
            scratch_shapes=[pltpu.VMEM((2 * sub, sorted_rows, d // 2), jnp.uint32),
                            pltpu.VMEM((MOE_ROWS, d // 2), jnp.uint32), pltpu.SemaphoreType.DMA((2 * sub,))],
        ),
        out_shape=jax.ShapeDtypeStruct((n_rows, d // 2), jnp.uint32),
        compiler_params=_params(("arbitrary",)),
        name="moe_dispatch",
    )(chunk_dst, n_chunks, zero_start, h, lp_rows)


def _sort_combine_kernel(src_ref, nch_ref, h_ref, lp_ref, gates_ref, lnw_ref, lnb_ref, yb_ref, o_ref,
                         ys_ref, sem, *, tm, nck, alpha):
    i = pl.program_id(0)
    sub = SORT_TILES_PER_STEP

    def fetch(tile, buf):
        _start_runs(nch_ref[tile], lambda j: _run_copy(yb_ref, src_ref[tile * nck + j], ys_ref.at[buf],
                                                       j * SUBLANES, sem.at[buf]))

    @pl.when(i == 0)
    def _():
        ys_ref[...] = jnp.zeros_like(ys_ref)
        for s in range(sub):
            fetch(s, s)

    @pl.when(i + 1 < pl.num_programs(0))
    def _():
        for s in range(sub):
            fetch((i + 1) * sub + s, (1 - i % 2) * sub + s)

    pos = lax.broadcasted_iota(jnp.int32, (tm, ys_ref.shape[1]), 1)
    for s in range(sub):
        buf = (i % 2) * sub + s
        rows = slice(s * tm, (s + 1) * tm)

        def wait(j, carry):
            _run_copy(yb_ref, 0, ys_ref.at[buf], 0, sem.at[buf]).wait()
            return carry

        lax.fori_loop(0, nch_ref[i * sub + s], wait, 0)

        lp = lp_ref[rows, :]
        pick = (jnp.where(pos == lp[:, 0:1], gates_ref[rows, 0:1], 0.0)
                + jnp.where(pos == lp[:, 1:2], gates_ref[rows, 1:2], 0.0))
        y = jnp.dot(pick.astype(BF16), _unpack_halves(ys_ref[buf]), preferred_element_type=F32)
        o_ref[rows, :] = _layer_norm(alpha * h_ref[rows, :] + y, lnw_ref[...], lnb_ref[...])


def _sort_combine(h, yb, lp_cols, gates, chunk_src, n_chunks, ln_w, ln_b, alpha, tm, sorted_rows):
    t, d = h.shape
    nck = sorted_rows // SUBLANES
    step = SORT_TILES_PER_STEP * tm
    row = lambda i, a, b: (i, 0)
    const = lambda i, a, b: (0, 0)
    return pl.pallas_call(
        functools.partial(_sort_combine_kernel, tm=tm, nck=nck, alpha=alpha),
        grid_spec=pltpu.PrefetchScalarGridSpec(
            num_scalar_prefetch=2,
            grid=(t // step,),
            in_specs=[pl.BlockSpec((step, d), row), pl.BlockSpec((step, 2), row), pl.BlockSpec((step, 2), row),
                      pl.BlockSpec((1, d), const), pl.BlockSpec((1, d), const),
                      pl.BlockSpec(memory_space=pl.ANY)],
            out_specs=pl.BlockSpec((step, d), row),
            scratch_shapes=[pltpu.VMEM((2 * SORT_TILES_PER_STEP, sorted_rows, d // 2), jnp.uint32),
                            pltpu.SemaphoreType.DMA((2 * SORT_TILES_PER_STEP,))],
        ),
        out_shape=jax.ShapeDtypeStruct((t, d), F32),
        compiler_params=_params(("arbitrary",)),
        name="moe_combine",
    )(chunk_src, n_chunks, h, lp_cols, gates, ln_w.reshape(1, d), ln_b.reshape(1, d), yb)


def _moe(h1, info, w_gate, w_up, w_down, layer, ln_w, ln_b, alpha):
    t = h1.shape[0]
    tm = SORT_ROWS
    ntile = t // tm
    run_pad = SUBLANES - 1
    sorted_rows = 2 * tm + LANES
    nck = sorted_rows // SUBLANES
    n_rows = (2 * t + ntile * N_EXPERTS * run_pad + MOE_ROWS - 1) // MOE_ROWS * MOE_ROWS + N_EXPERTS * MOE_ROWS
    i32 = jnp.int32
    e1, e2 = info[0].astype(i32), info[1].astype(i32)
    rank1, rank2 = info[4].astype(i32), info[5].astype(i32)
    expert_ids = jnp.arange(N_EXPERTS, dtype=i32)[None, :]
    oh1 = e1[:, None] == expert_ids
    oh2 = e2[:, None] == expert_ids
    cnt = jnp.logical_or(oh1, oh2).astype(i32).reshape(ntile, tm, N_EXPERTS).sum(axis=1)
    cnt8 = (cnt + run_pad) // SUBLANES * SUBLANES
    local = jnp.cumsum(cnt8, axis=1) - cnt8
    before = jnp.cumsum(cnt, axis=0) - cnt
    seg = cnt8.sum(axis=0)
    seg_pad = (seg + MOE_ROWS - 1) // MOE_ROWS * MOE_ROWS
    ends = jnp.cumsum(seg_pad)
    slot = (ends - seg_pad)[None, :] + jnp.cumsum(cnt8, axis=0) - cnt8
    shift = jnp.repeat(local - before, tm, axis=0)
    lp1 = jnp.sum(jnp.where(oh1, shift, 0), axis=1) + rank1
    lp2 = jnp.sum(jnp.where(oh2, shift, 0), axis=1) + rank2
    chunk_row = jnp.arange(nck, dtype=i32) * SUBLANES
    run_of = jnp.sum(((local + cnt8)[:, None, :] <= chunk_row[None, :, None]).astype(i32), axis=2)
    run_of = jnp.minimum(run_of, N_EXPERTS - 1)
    chunk_slot = jnp.sum(jnp.where(run_of[:, :, None] == expert_ids[None], (slot - local)[:, None, :], 0),
                         axis=2) + chunk_row[None, :]
    n_chunks = cnt8.sum(axis=1) // SUBLANES
    n_valid = (ends[-1] // MOE_ROWS).astype(i32).reshape(1)
    zero_start = jnp.concatenate([jnp.maximum(ends - MOE_ROWS, 0), n_valid]).astype(i32)
    gates = jnp.stack([info[2], info[3]], axis=1)
    chunk_slot = chunk_slot.reshape(-1).astype(i32)
    seg_first = ((ends - seg_pad) // MOE_ROWS).astype(i32)
    seg_count = (seg_pad // MOE_ROWS).astype(i32)

    xb = _sort_dispatch(h1, jnp.stack([lp1, lp2], axis=0), chunk_slot, n_chunks, zero_start, n_rows, tm,
                        sorted_rows)
    yb = _seg_experts(xb, seg_first, seg_count, n_valid, w_gate, w_up, w_down, layer)
    return _sort_combine(h1, yb, jnp.stack([lp1, lp2], axis=1), gates, chunk_slot, n_chunks, ln_w, ln_b,
                         alpha, tm, sorted_rows)


def _rotary_column_order(w_in):
    d = w_in.shape[0]
    nq = RET_HEADS * RET_DK

    def perm(w):
        return w.reshape(d, RET_HEADS, RET_DK // 2, 2).transpose(0, 3, 1, 2).reshape(d, nq)

    return jnp.concatenate([perm(w_in[:, :nq]), perm(w_in[:, nq:2 * nq]), w_in[:, 2 * nq:]], axis=1)


def kernel(x, w_in_even, w_out_even, lru_conv_w, lru_conv_b, lru_gate_w, lru_gate_b, lru_lambda,
           w_in_odd, w_out_odd, hg_lower_bounds, hg_norm_w, ln_w, ln_b, router_w,
           moe_w_gate, moe_w_up, moe_w_down):
    batch, seq_len, d = x.shape
    depth = ln_w.shape[0]
    alpha = (2.0 * depth) ** 0.25
    t = batch * seq_len
    h = x.reshape(t, d)
    router_wt = router_w.T
    half = RET_DK // 2
    inv_freq = ROPE_BASE ** (-jnp.arange(0, RET_DK, 2, dtype=F32) / RET_DK)
    inv_freq = jnp.tile(inv_freq, LANES // half).reshape(1, LANES)
    nret = RET_HEADS * RET_DV
    tm = min(PROJ_ROWS, seq_len)
    tiles_per_seq = seq_len // tm
    cos, sin = _rotary_table(inv_freq, seq_len)
    pos_spec = pl.BlockSpec((tm, LANES), lambda i: (i % tiles_per_seq, 0))

    for layer in range(depth):
        j = layer // 2
        if layer % 2 == 0:
            w_in = _rotary_column_order(w_in_even[j]).astype(BF16)
            n_bf = 2 * RET_HEADS * RET_DK + 2 * nret
            proj_kernel = functools.partial(_proj_ret_kernel, n_bf=n_bf)
            proj_b, proj_f = _project(proj_kernel, "in_proj", h, w_in, [cos, sin], [pos_spec, pos_spec], n_bf,
                                      w_in.shape[1] - n_bf, seq_len)
            ret = _retention(proj_b, batch, seq_len)
            h_fwd, h_bwd = _rglru(proj_f, lru_conv_w[j], lru_conv_b[j], lru_gate_w[j], lru_gate_b[j],
                                  lru_lambda[j], batch, seq_len)
            w_out = w_out_even[j].astype(BF16)
            mixer_inputs, weights = (ret, h_fwd, h_bwd, proj_f), [w_out[:nret], w_out[nret:]]
        else:
            w = w_in_odd[j]
            w_in = jnp.concatenate([w[:, :2 * d], w[:, 4 * d:], w[:, 2 * d:4 * d]], axis=1).astype(BF16)
            proj_kernel = functools.partial(_proj_gla_kernel, layer=layer)
            bounds_spec = pl.BlockSpec(hg_lower_bounds.shape, lambda i: (0, 0))
            proj_b, proj_f = _project(proj_kernel, "in_proj_gla", h, w_in, [hg_lower_bounds], [bounds_spec],
                                      5 * d, 2 * d, seq_len)
            o_fwd, o_bwd = _gla(proj_b, proj_f, batch, seq_len)
            mixer_inputs, weights = (o_fwd, o_bwd, proj_b, hg_norm_w[j]), [w_out_odd[j].astype(BF16)]
        h1, info = _out_project(layer % 2 == 0, mixer_inputs, weights, h, ln_w[layer, 0], ln_b[layer, 0],
                                router_wt, alpha)
        h = _moe(h1, info, moe_w_gate, moe_w_up, moe_w_down, layer, ln_w[layer, 1], ln_b[layer, 1], alpha)
    return h.reshape(batch, seq_len, d)
```
